```python
import jax, jax.numpy as jnp
from jax import lax
import numpy as np

D_MODEL = 1024
BATCH = 8
SEQ = 8192
DEPTH = 1

GRID_W = 64
Q_BLOCK = 128
ROPE_THETA = 10000.0
EPS = 1e-6
D_MIX = D_MODEL
A_HEAD_DIM = 64
A_WIDTH = D_MIX // 2
A_HEADS = A_WIDTH // A_HEAD_DIM
A_KV_HEADS = 2
B_WIDTH = D_MIX - A_WIDTH
B_V_HEAD_DIM = 128
B_HEADS = B_WIDTH // B_V_HEAD_DIM
B_NOPE_DIM = 64
B_ROPE_DIM = 32
B_QK_DIM = B_NOPE_DIM + B_ROPE_DIM
B_Q_RANK = 384
B_KV_RANK = 256

_SIZES = (
    A_WIDTH,
    A_KV_HEADS * A_HEAD_DIM,
    A_KV_HEADS * A_HEAD_DIM,
    A_WIDTH,
    B_Q_RANK,
    B_KV_RANK,
    B_ROPE_DIM,
    B_WIDTH,
)
N_IN = int(sum(_SIZES))
_SPLITS = tuple(int(v) for v in np.cumsum(_SIZES)[:-1])

kernel_name = "hybrid_gqa_mla_parallel_groups"


def rms_norm(x, g):
    xf = x.astype(jnp.float32)
    y = xf * lax.rsqrt(jnp.mean(xf * xf, axis=-1, keepdims=True) + EPS)
    return (y * g.astype(jnp.float32)).astype(x.dtype)


def axial_rope_tables(seq_len, dim):
    rows = seq_len // GRID_W
    row = jnp.repeat(jnp.arange(rows, dtype=jnp.float32), GRID_W)
    col = jnp.tile(jnp.arange(GRID_W, dtype=jnp.float32), rows)
    half = dim // 2
    inv = 1.0 / (ROPE_THETA ** (jnp.arange(0, half, 2, dtype=jnp.float32) / half))
    ang_r = row[:, None] * inv[None, :]
    ang_c = col[:, None] * inv[None, :]
    return jnp.cos(ang_r), jnp.sin(ang_r), jnp.cos(ang_c), jnp.sin(ang_c)


def _rotate(xh, cos, sin):
    x1, x2 = jnp.split(xh, 2, axis=-1)
    return jnp.concatenate([x1 * cos - x2 * sin, x1 * sin + x2 * cos], axis=-1)


def apply_axial_rope(x, tables):
    cr, sr, cc, sc = tables
    xf = x.astype(jnp.float32)
    xr, xc = jnp.split(xf, 2, axis=-1)
    out = jnp.concatenate([_rotate(xr, cr, sr), _rotate(xc, cc, sc)], axis=-1)
    return out.astype(x.dtype)


def blocked_attention(q, k, v):
    bsz, n_heads, s_len, dk = q.shape
    n_kv = k.shape[1]
    grp = n_heads // n_kv
    dv = v.shape[-1]
    nb = s_len // Q_BLOCK
    scale = 1.0 / float(np.sqrt(dk))
    qb = q.reshape(bsz, n_kv, grp, nb, Q_BLOCK, dk).transpose(3, 0, 1, 2, 4, 5)

    def one_block(qblk):
        s = jnp.einsum('bkgqd,bksd->bkgqs', qblk, k).astype(jnp.float32) * scale
        p = jax.nn.softmax(s, axis=-1).astype(v.dtype)
        return jnp.einsum('bkgqs,bksd->bkgqd', p, v)

    out = lax.map(one_block, qb)
    out = out.transpose(1, 0, 4, 2, 3, 5)
    return out.reshape(bsz, s_len, n_heads * dv)


def _fwd_setup_inputs(seed: int = 0) -> dict:
    key = jax.random.key(seed)
    ks = jax.random.split(key, 16)
    f32 = jnp.float32

    def w(k, shape, fan_in):
        return jax.random.normal(k, shape, f32) * (fan_in ** -0.5)

    def gain(k, shape):
        return 1.0 + 0.02 * jax.random.normal(k, shape, f32)

    return {
        "x": jax.random.normal(ks[0], (BATCH, SEQ, D_MODEL), f32),
        "norm_in": gain(ks[1], (DEPTH, D_MODEL)),
        "w_in": w(ks[2], (DEPTH, D_MODEL, N_IN), D_MODEL),
        "a_q_norm": gain(ks[3], (DEPTH, A_HEAD_DIM)),
        "a_k_norm": gain(ks[4], (DEPTH, A_HEAD_DIM)),
        "b_cq_norm": gain(ks[5], (DEPTH, B_Q_RANK)),
        "b_ckv_norm": gain(ks[6], (DEPTH, B_KV_RANK)),
        "w_uq": w(ks[7], (DEPTH, B_Q_RANK, B_HEADS * B_QK_DIM), B_Q_RANK),
        "w_ukv": w(ks[8], (DEPTH, B_KV_RANK, B_HEADS * (B_NOPE_DIM + B_V_HEAD_DIM)), B_KV_RANK),
        "b_q_norm": gain(ks[9], (DEPTH, B_QK_DIM)),
        "b_k_norm": gain(ks[10], (DEPTH, B_QK_DIM)),
        "w_out": w(ks[11], (DEPTH, D_MIX, D_MODEL), D_MIX),
    }


def _fwd_reference(x, norm_in, w_in, a_q_norm, a_k_norm, b_cq_norm, b_ckv_norm,
              w_uq, w_ukv, b_q_norm, b_k_norm, w_out):
    bsz, s_len, _ = x.shape
    rope_a = axial_rope_tables(s_len, A_HEAD_DIM)
    rope_b = axial_rope_tables(s_len, B_ROPE_DIM)
    h = x
    for l in range(DEPTH):
        xn = rms_norm(h, norm_in[l])
        proj = jnp.einsum('bsd,dn->bsn', xn, w_in[l])
        a_q, a_k, a_v, a_g, b_cq, b_ckv, b_kr, b_g = jnp.split(proj, _SPLITS, axis=-1)

        q = a_q.reshape(bsz, s_len, A_HEADS, A_HEAD_DIM).transpose(0, 2, 1, 3)
        k = a_k.reshape(bsz, s_len, A_KV_HEADS, A_HEAD_DIM).transpose(0, 2, 1, 3)
        v = a_v.reshape(bsz, s_len, A_KV_HEADS, A_HEAD_DIM).transpose(0, 2, 1, 3)
        q = apply_axial_rope(rms_norm(q, a_q_norm[l]), rope_a)
        k = apply_axial_rope(rms_norm(k, a_k_norm[l]), rope_a)
        y_a = blocked_attention(q, k, v)

        c_q = rms_norm(b_cq, b_cq_norm[l])
        c_kv = rms_norm(b_ckv, b_ckv_norm[l])
        qb = jnp.einsum('bsr,rn->bsn', c_q, w_uq[l]).reshape(bsz, s_len, B_HEADS, B_QK_DIM)
        kvb = jnp.einsum('bsr,rn->bsn', c_kv, w_ukv[l]).reshape(
            bsz, s_len, B_HEADS, B_NOPE_DIM + B_V_HEAD_DIM)
        k_nope, vb = jnp.split(kvb, [B_NOPE_DIM], axis=-1)
        k_rope = jnp.broadcast_to(b_kr[:, :, None, :], (bsz, s_len, B_HEADS, B_ROPE_DIM))
        kb = jnp.concatenate([k_nope, k_rope], axis=-1)
        qb = rms_norm(qb, b_q_norm[l]).transpose(0, 2, 1, 3)
        kb = rms_norm(kb, b_k_norm[l]).transpose(0, 2, 1, 3)
        qb = jnp.concatenate([qb[..., :B_NOPE_DIM],
                              apply_axial_rope(qb[..., B_NOPE_DIM:], rope_b)], axis=-1)
        kb = jnp.concatenate([kb[..., :B_NOPE_DIM],
                              apply_axial_rope(kb[..., B_NOPE_DIM:], rope_b)], axis=-1)
        vb = vb.transpose(0, 2, 1, 3)
        y_b = blocked_attention(qb, kb, vb)

        y = jnp.concatenate([y_a * jax.nn.silu(a_g), y_b * jax.nn.silu(b_g)], axis=-1)
        h = h + jnp.einsum('bsm,md->bsd', y, w_out[l])
    return h


import jax as _jax
import jax.numpy as _jnp

TWIN_FORMAT = 'train_step'
FWD_PARAMS = ['x', 'norm_in', 'w_in', 'a_q_norm', 'a_k_norm', 'b_cq_norm', 'b_ckv_norm', 'w_uq', 'w_ukv', 'b_q_norm', 'b_k_norm', 'w_out']
TWIN_WEIGHTS = ['norm_in', 'w_in', 'a_q_norm', 'a_k_norm', 'b_cq_norm', 'b_ckv_norm', 'w_uq', 'w_ukv', 'b_q_norm', 'b_k_norm', 'w_out']
TWIN_DIFF_INPUT = 'x'
TWIN_INPUTS = ['x', 'norm_in', 'w_in', 'a_q_norm', 'a_k_norm', 'b_cq_norm', 'b_ckv_norm', 'w_uq', 'w_ukv', 'b_q_norm', 'b_k_norm', 'w_out', 'loss_target', 'm_norm_in', 'm_w_in', 'm_a_q_norm', 'm_a_k_norm', 'm_b_cq_norm', 'm_b_ckv_norm', 'm_w_uq', 'm_w_ukv', 'm_b_q_norm', 'm_b_k_norm', 'm_w_out', 'v_norm_in', 'v_w_in', 'v_a_q_norm', 'v_a_k_norm', 'v_b_cq_norm', 'v_b_ckv_norm', 'v_w_uq', 'v_w_ukv', 'v_b_q_norm', 'v_b_k_norm', 'v_w_out']
TWIN_OUTPUTS = ['loss', 'grad_x', 'grad_norm_in', 'grad_w_in', 'grad_a_q_norm', 'grad_a_k_norm', 'grad_b_cq_norm', 'grad_b_ckv_norm', 'grad_w_uq', 'grad_w_ukv', 'grad_b_q_norm', 'grad_b_k_norm', 'grad_w_out', 'delta_norm_in', 'delta_w_in', 'delta_a_q_norm', 'delta_a_k_norm', 'delta_b_cq_norm', 'delta_b_ckv_norm', 'delta_w_uq', 'delta_w_ukv', 'delta_b_q_norm', 'delta_b_k_norm', 'delta_w_out', 'new_m_norm_in', 'new_m_w_in', 'new_m_a_q_norm', 'new_m_a_k_norm', 'new_m_b_cq_norm', 'new_m_b_ckv_norm', 'new_m_w_uq', 'new_m_w_ukv', 'new_m_b_q_norm', 'new_m_b_k_norm', 'new_m_w_out', 'new_v_norm_in', 'new_v_w_in', 'new_v_a_q_norm', 'new_v_a_k_norm', 'new_v_b_cq_norm', 'new_v_b_ckv_norm', 'new_v_w_uq', 'new_v_w_ukv', 'new_v_b_q_norm', 'new_v_b_k_norm', 'new_v_w_out']
TWIN_LEAF_KINDS = {'loss': 'loss', 'grad_x': 'grad_x', 'grad_norm_in': 'grad_w', 'grad_w_in': 'grad_w', 'grad_a_q_norm': 'grad_w', 'grad_a_k_norm': 'grad_w', 'grad_b_cq_norm': 'grad_w', 'grad_b_ckv_norm': 'grad_w', 'grad_w_uq': 'grad_w', 'grad_w_ukv': 'grad_w', 'grad_b_q_norm': 'grad_w', 'grad_b_k_norm': 'grad_w', 'grad_w_out': 'grad_w', 'delta_norm_in': 'delta_w', 'delta_w_in': 'delta_w', 'delta_a_q_norm': 'delta_w', 'delta_a_k_norm': 'delta_w', 'delta_b_cq_norm': 'delta_w', 'delta_b_ckv_norm': 'delta_w', 'delta_w_uq': 'delta_w', 'delta_w_ukv': 'delta_w', 'delta_b_q_norm': 'delta_w', 'delta_b_k_norm': 'delta_w', 'delta_w_out': 'delta_w', 'new_m_norm_in': 'new_m', 'new_m_w_in': 'new_m', 'new_m_a_q_norm': 'new_m', 'new_m_a_k_norm': 'new_m', 'new_m_b_cq_norm': 'new_m', 'new_m_b_ckv_norm': 'new_m', 'new_m_w_uq': 'new_m', 'new_m_w_ukv': 'new_m', 'new_m_b_q_norm': 'new_m', 'new_m_b_k_norm': 'new_m', 'new_m_w_out': 'new_m', 'new_v_norm_in': 'new_v', 'new_v_w_in': 'new_v', 'new_v_a_q_norm': 'new_v', 'new_v_a_k_norm': 'new_v', 'new_v_b_cq_norm': 'new_v', 'new_v_b_ckv_norm': 'new_v', 'new_v_w_uq': 'new_v', 'new_v_w_ukv': 'new_v', 'new_v_b_q_norm': 'new_v', 'new_v_b_k_norm': 'new_v', 'new_v_w_out': 'new_v'}


def _forward(args):
    return _fwd_reference(*[args[k] for k in FWD_PARAMS])


def _output_shape():
    def fwd():
        inp = _fwd_setup_inputs(0)
        return _fwd_reference(*[inp[k] for k in FWD_PARAMS])
    out = _jax.eval_shape(fwd)
    return out.shape, out.dtype

N_MICROBATCH = 1
ADAM_LR = 0.001
ADAM_B1 = 0.9
ADAM_B2 = 0.999
ADAM_EPS = 1e-08
ADAM_WD = 0.01
ADAM_STEP = 10
PER_EXAMPLE_BATCH_AXIS = {'x': 0, 'loss_target': 0}
SHARED_INPUTS = []
_WEIGHT_DTYPES = {'norm_in': _jnp.float32, 'w_in': _jnp.float32, 'a_q_norm': _jnp.float32, 'a_k_norm': _jnp.float32, 'b_cq_norm': _jnp.float32, 'b_ckv_norm': _jnp.float32, 'w_uq': _jnp.float32, 'w_ukv': _jnp.float32, 'b_q_norm': _jnp.float32, 'b_k_norm': _jnp.float32, 'w_out': _jnp.float32}
MOMENT_SCALE = {'norm_in': 8.660223e-02, 'w_in': 3.017942e-02, 'a_q_norm': 2.879155e-01, 'a_k_norm': 2.896899e-01, 'b_cq_norm': 2.813938e-02, 'b_ckv_norm': 1.260592e-01, 'w_uq': 2.842769e-02, 'w_ukv': 2.767968e-02, 'b_q_norm': 2.601863e-01, 'b_k_norm': 2.604992e-01, 'w_out': 2.350466e-02}


def _to_microbatches(a, axis):
    t = _jnp.moveaxis(a, axis, 0)
    t = t.reshape((N_MICROBATCH, t.shape[0] // N_MICROBATCH) + t.shape[1:])
    return _jnp.moveaxis(t, 1, axis + 1)


def setup_inputs(seed: int = 0) -> dict:
    inp = _fwd_setup_inputs(seed)
    key = _jax.random.fold_in(_jax.random.key(seed), 7919)
    shape, _ = _output_shape()
    out = dict(inp)
    out["loss_target"] = _jax.random.normal(_jax.random.fold_in(key, 0), shape, _jnp.float32)
    for i, name in enumerate(TWIN_WEIGHTS):
        w = inp[name].astype(_jnp.float32)
        if MOMENT_SCALE is None:
            s = _jnp.sqrt(_jnp.mean(_jnp.square(w)) + 1e-30)
        else:
            s = MOMENT_SCALE[name]
        km, kv = _jax.random.split(_jax.random.fold_in(key, i + 1))
        out[name] = w
        out["m_" + name] = s * _jax.random.normal(km, w.shape, _jnp.float32)
        out["v_" + name] = (s * s) * _jax.random.uniform(kv, w.shape, _jnp.float32, 0.5, 1.5)
    if N_MICROBATCH > 1:
        for name, axis in PER_EXAMPLE_BATCH_AXIS.items():
            out[name] = _to_microbatches(out[name], axis)
    return {'x': out['x'], 'norm_in': out['norm_in'], 'w_in': out['w_in'], 'a_q_norm': out['a_q_norm'], 'a_k_norm': out['a_k_norm'], 'b_cq_norm': out['b_cq_norm'], 'b_ckv_norm': out['b_ckv_norm'], 'w_uq': out['w_uq'], 'w_ukv': out['w_ukv'], 'b_q_norm': out['b_q_norm'], 'b_k_norm': out['b_k_norm'], 'w_out': out['w_out'], 'loss_target': out['loss_target'], 'm_norm_in': out['m_norm_in'], 'm_w_in': out['m_w_in'], 'm_a_q_norm': out['m_a_q_norm'], 'm_a_k_norm': out['m_a_k_norm'], 'm_b_cq_norm': out['m_b_cq_norm'], 'm_b_ckv_norm': out['m_b_ckv_norm'], 'm_w_uq': out['m_w_uq'], 'm_w_ukv': out['m_w_ukv'], 'm_b_q_norm': out['m_b_q_norm'], 'm_b_k_norm': out['m_b_k_norm'], 'm_w_out': out['m_w_out'], 'v_norm_in': out['v_norm_in'], 'v_w_in': out['v_w_in'], 'v_a_q_norm': out['v_a_q_norm'], 'v_a_k_norm': out['v_a_k_norm'], 'v_b_cq_norm': out['v_b_cq_norm'], 'v_b_ckv_norm': out['v_b_ckv_norm'], 'v_w_uq': out['v_w_uq'], 'v_w_ukv': out['v_w_ukv'], 'v_b_q_norm': out['v_b_q_norm'], 'v_b_k_norm': out['v_b_k_norm'], 'v_w_out': out['v_w_out']}


def _loss(weights, diff, rest, loss_target):
    with _jax.named_scope("forward"):
        args = {**rest, TWIN_DIFF_INPUT: diff, **{k: w.astype(_WEIGHT_DTYPES[k]) for k, w in weights.items()}}
        y = _forward(args)
    with _jax.named_scope("loss_head"):
        err = _jnp.square(y.astype(_jnp.float32) - loss_target)
        return 0.5 * _jnp.sum(_jnp.mean(err, axis=-1)) if err.ndim else 0.5 * err


def _adamw(w, g, m, v):
    m = ADAM_B1 * m + (1.0 - ADAM_B1) * g
    v = ADAM_B2 * v + (1.0 - ADAM_B2) * _jnp.square(g)
    m_hat = m / (1.0 - ADAM_B1 ** ADAM_STEP)
    v_hat = v / (1.0 - ADAM_B2 ** ADAM_STEP)
    delta = -ADAM_LR * (m_hat / (_jnp.sqrt(v_hat) + ADAM_EPS) + ADAM_WD * w)
    return delta, m, v


def reference(x, norm_in, w_in, a_q_norm, a_k_norm, b_cq_norm, b_ckv_norm, w_uq, w_ukv, b_q_norm, b_k_norm, w_out, loss_target, m_norm_in, m_w_in, m_a_q_norm, m_a_k_norm, m_b_cq_norm, m_b_ckv_norm, m_w_uq, m_w_ukv, m_b_q_norm, m_b_k_norm, m_w_out, v_norm_in, v_w_in, v_a_q_norm, v_a_k_norm, v_b_cq_norm, v_b_ckv_norm, v_w_uq, v_w_ukv, v_b_q_norm, v_b_k_norm, v_w_out):
    given = dict(x=x, norm_in=norm_in, w_in=w_in, a_q_norm=a_q_norm, a_k_norm=a_k_norm, b_cq_norm=b_cq_norm, b_ckv_norm=b_ckv_norm, w_uq=w_uq, w_ukv=w_ukv, b_q_norm=b_q_norm, b_k_norm=b_k_norm, w_out=w_out, loss_target=loss_target, m_norm_in=m_norm_in, m_w_in=m_w_in, m_a_q_norm=m_a_q_norm, m_a_k_norm=m_a_k_norm, m_b_cq_norm=m_b_cq_norm, m_b_ckv_norm=m_b_ckv_norm, m_w_uq=m_w_uq, m_w_ukv=m_w_ukv, m_b_q_norm=m_b_q_norm, m_b_k_norm=m_b_k_norm, m_w_out=m_w_out, v_norm_in=v_norm_in, v_w_in=v_w_in, v_a_q_norm=v_a_q_norm, v_a_k_norm=v_a_k_norm, v_b_cq_norm=v_b_cq_norm, v_b_ckv_norm=v_b_ckv_norm, v_w_uq=v_w_uq, v_w_ukv=v_w_ukv, v_b_q_norm=v_b_q_norm, v_b_k_norm=v_b_k_norm, v_w_out=v_w_out)
    weights = {n: given[n] for n in TWIN_WEIGHTS}
    shared = {n: given[n] for n in SHARED_INPUTS}
    per_example = {n: given[n] for n in ['x']}
    grad_fn = _jax.value_and_grad(_loss, argnums=(0, 1))

    def one_microbatch(ex, loss_target):
        ex = dict(ex)
        diff = ex.pop(TWIN_DIFF_INPUT)
        return grad_fn(weights, diff, {**shared, **ex}, loss_target)

    if N_MICROBATCH == 1:
        loss, (grad_w, grad_x) = one_microbatch(per_example, given["loss_target"])
    else:
        def body(carry, xs):
            loss_sum, grad_sum = carry
            l_k, (gw_k, gx_k) = one_microbatch(xs[0], xs[1])
            with _jax.named_scope("update"):
                return (loss_sum + l_k, _jax.tree.map(_jnp.add, grad_sum, gw_k)), gx_k

        init = (_jnp.zeros((), _jnp.float32), _jax.tree.map(_jnp.zeros_like, weights))
        (loss, grad_w), grad_x = _jax.lax.scan(body, init, (per_example, given["loss_target"]))
    with _jax.named_scope("update"):
        delta_w, new_m, new_v = {}, {}, {}
        for n in TWIN_WEIGHTS:
            delta_w[n], new_m[n], new_v[n] = _adamw(weights[n], grad_w[n], given["m_" + n], given["v_" + n])
    return (loss, grad_x, *[grad_w[n] for n in TWIN_WEIGHTS], *[delta_w[n] for n in TWIN_WEIGHTS],
            *[new_m[n] for n in TWIN_WEIGHTS], *[new_v[n] for n in TWIN_WEIGHTS])
```

```python
import functools

import jax
import jax.numpy as jnp
import numpy as np
from jax import lax
from jax.experimental import pallas as pl
from jax.experimental.pallas import tpu as pltpu

F32 = jnp.float32
BF16 = jnp.bfloat16
MESH = pl.DeviceIdType.MESH

D_MODEL = 1024
GRID_W = 64
ROPE_THETA = 10000.0
EPS = 1e-6
A_HEADS, A_KV, A_DIM = 8, 2, 64
A_GROUP = A_HEADS // A_KV
B_HEADS, B_NOPE, B_ROPE, B_V = 4, 64, 32, 128
B_QK = B_NOPE + B_ROPE
B_Q_RANK, B_KV_RANK = 384, 256
N_IN = 2464
SCALE_A = 1.0 / float(np.sqrt(A_DIM))
SCALE_B = 1.0 / float(np.sqrt(B_QK))
ADAM_LR, ADAM_B1, ADAM_B2, ADAM_EPS, ADAM_WD, ADAM_STEP = 0.001, 0.9, 0.999, 1e-08, 0.01, 10

LANE = 128
VMEM_BYTES = 64 * 1024 * 1024
VMEM_LIMIT = VMEM_BYTES - 8 * 1024 * 1024

QA0 = 0
KA0 = QA0 + A_HEADS * LANE
VA0 = KA0 + A_KV * LANE
GA0 = VA0 + A_KV * LANE
GB0 = GA0 + A_HEADS * LANE
CQ0 = GB0 + B_HEADS * LANE
CKV0 = CQ0 + B_Q_RANK
KR0 = CKV0 + B_KV_RANK
N_EXT = KR0 + LANE
N_GATE = (A_HEADS + B_HEADS) * LANE
N_PRE = KA0 + A_KV * LANE + B_Q_RANK + B_KV_RANK

N_CHIPS = 4
SH_IN = (D_MODEL, N_IN // N_CHIPS)
SH_UQ = (B_Q_RANK // N_CHIPS, B_HEADS * B_QK)
SH_UKV = (B_KV_RANK, B_HEADS * (B_NOPE + B_V) // N_CHIPS)
SH_OUT = (D_MODEL // N_CHIPS, D_MODEL)
R_IN = SH_IN[0] * SH_IN[1] // LANE
R_UQ = SH_UQ[0] * SH_UQ[1] // LANE
R_UKV = SH_UKV[0] * SH_UKV[1] // LANE
R_OUT = SH_OUT[0] * SH_OUT[1] // LANE
R_BIG = R_IN + R_UQ + R_UKV + R_OUT
R_SMALL = 16
R_PACK = 7680
R_HALF = R_PACK // 2
SMALL_SIZES = (D_MODEL, A_DIM, A_DIM, B_Q_RANK, B_KV_RANK, B_QK, B_QK)
LOSS_AT = sum(SMALL_SIZES)


def _pallas(body, **kw):
    return pl.pallas_call(body, **kw)


def _params(sem=None, vmem=None):
    return pltpu.CompilerParams(dimension_semantics=sem, vmem_limit_bytes=vmem)


def _rms_fwd(x, g, n):
    r = lax.rsqrt(jnp.sum(x * x, axis=-1, keepdims=True) * (1.0 / n) + EPS)
    return x * r * g


def _rms_bwd(dy, x, g, n):
    r = lax.rsqrt(jnp.sum(x * x, axis=-1, keepdims=True) * (1.0 / n) + EPS)
    xhat = x * r
    u = dy * g
    dx = r * (u - xhat * (jnp.sum(u * xhat, axis=-1, keepdims=True) * (1.0 / n)))
    return dx, dy * xhat


def _partner(y, half):
    lane = lax.broadcasted_iota(jnp.int32, y.shape, 1)
    first = (lane % (2 * half)) < half
    return jnp.where(first, pltpu.roll(y, LANE - half, 1), pltpu.roll(y, half, 1))


def _rope_fwd(y, cos, sin, half):
    return y * cos + _partner(y, half) * sin


def _rope_bwd(d, cos, sin, half):
    return d * cos - _partner(d, half) * sin


def _nt(a, b):
    return lax.dot_general(a, b, (((1,), (1,)), ((), ())), preferred_element_type=F32)


def _tn(a, b):
    return lax.dot_general(a, b, (((0,), (0,)), ((), ())), preferred_element_type=F32)


def _nn(a, b):
    return jnp.dot(a, b, preferred_element_type=F32)


def _full(shape):
    return pl.BlockSpec(shape, lambda *_: (0,) * len(shape))


def _gather_weights(w_pack):
    def body(w_ref, out_ref, send_sems, recv_sems):
        x, y, c = lax.axis_index("x"), lax.axis_index("y"), lax.axis_index("c")
        sibling = (x, y, 1 - c)
        chips = [(1 - x, y), (x, 1 - y), (1 - x, 1 - y)]
        me = 2 * x + y

        def half(j, hc):
            return out_ref.at[j, pl.ds(pl.multiple_of(hc * R_HALF, R_HALF), R_HALF), :]

        def copy(k, j, hc, to):
            return pltpu.make_async_remote_copy(
                src_ref=half(j, hc), dst_ref=half(j, hc), send_sem=send_sems.at[k], recv_sem=recv_sems.at[k],
                device_id=to, device_id_type=MESH)

        out_ref[me] = w_ref[...].astype(BF16)
        first = [copy(k, me, c, (*chip, c)) for k, chip in enumerate(chips)]
        for cp in first:
            cp.start()
        passed = [copy(3 + k, 2 * chip[0] + chip[1], c, sibling) for k, chip in enumerate(chips)]
        for k, chip in enumerate(chips):
            copy(k, 2 * chip[0] + chip[1], c, (*chip, c)).wait_recv()
            passed[k].start()
        for k, chip in enumerate(chips):
            copy(3 + k, 2 * chip[0] + chip[1], 1 - c, sibling).wait_recv()
        for cp in first + passed:
            cp.wait_send()

    return _pallas(
        body, name="gather_weights",
        out_shape=jax.ShapeDtypeStruct((N_CHIPS, R_PACK, LANE), BF16),
        in_specs=[pl.BlockSpec(memory_space=pltpu.VMEM)],
        out_specs=pl.BlockSpec(memory_space=pltpu.VMEM),
        scratch_shapes=[pltpu.SemaphoreType.DMA((6,)), pltpu.SemaphoreType.DMA((6,))],
        compiler_params=_params(vmem=32 * 1024 * 1024),
    )(w_pack)


def _reduce_grads(parts):
    def body(p_ref, out_ref, rec_a, rec_b, sa_send, sa_recv, sb_send, sb_recv, sc_send, sc_recv):
        x, y, c = lax.axis_index("x"), lax.axis_index("y"), lax.axis_index("c")
        sibling = (x, y, 1 - c)
        me = 2 * x + y
        mine = pl.ds(pl.multiple_of(c * R_HALF, R_HALF), R_HALF)
        other = pl.ds(pl.multiple_of((1 - c) * R_HALF, R_HALF), R_HALF)

        def copy_a(j):
            return pltpu.make_async_remote_copy(
                src_ref=p_ref.at[j, other, :], dst_ref=rec_a.at[j], send_sem=sa_send.at[j], recv_sem=sa_recv.at[j],
                device_id=sibling, device_id_type=MESH)

        def copy_b(r):
            j = me ^ r
            return pltpu.make_async_remote_copy(
                src_ref=rec_a.at[j], dst_ref=rec_b.at[r], send_sem=sb_send.at[r - 1], recv_sem=sb_recv.at[r - 1],
                device_id=(j // 2, j % 2, c), device_id_type=MESH)

        copy_c = pltpu.make_async_remote_copy(
            src_ref=out_ref.at[mine, :], dst_ref=out_ref.at[mine, :], send_sem=sc_send, recv_sem=sc_recv,
            device_id=sibling, device_id_type=MESH)

        for j in range(N_CHIPS):
            copy_a(j).start()
        for r in range(1, N_CHIPS):
            j = me ^ r
            copy_a(j).wait_recv()
            rec_a[j] = rec_a[j] + p_ref[j, mine, :]
            copy_b(r).start()
        copy_a(me).wait_recv()
        rec_b[0] = rec_a[me] + p_ref[me, mine, :]
        for r in range(1, N_CHIPS):
            copy_b(r).wait_recv()
        total = rec_b[me]
        for j in range(1, N_CHIPS):
            total = total + rec_b[j ^ me]
        out_ref[mine, :] = total
        copy_c.start()
        copy_c.wait_recv()
        for j in range(N_CHIPS):
            copy_a(j).wait_send()
        for r in range(1, N_CHIPS):
            copy_b(r).wait_send()
        copy_c.wait_send()

    return _pallas(
        body, name="reduce_grads",
        out_shape=jax.ShapeDtypeStruct((R_PACK, LANE), F32),
        in_specs=[pl.BlockSpec(memory_space=pltpu.VMEM)],
        out_specs=pl.BlockSpec(memory_space=pltpu.VMEM),
        scratch_shapes=[
            pltpu.VMEM((N_CHIPS, R_HALF, LANE), F32),
            pltpu.VMEM((N_CHIPS, R_HALF, LANE), F32),
            pltpu.SemaphoreType.DMA((N_CHIPS,)), pltpu.SemaphoreType.DMA((N_CHIPS,)),
            pltpu.SemaphoreType.DMA((N_CHIPS - 1,)), pltpu.SemaphoreType.DMA((N_CHIPS - 1,)),
            pltpu.SemaphoreType.DMA, pltpu.SemaphoreType.DMA,
        ],
        compiler_params=_params(vmem=VMEM_LIMIT),
    )(parts)


def _pre(x, tabs, w_in_ext, w_uq_pad, w_ukv_ext, gains, tm):
    s_len = x.shape[0]
    nt = s_len // tm

    def body(x_ref, ca_ref, sa_ref, cb_ref, sb_ref, win_ref, wuq_ref, wukv_ref,
             gin_ref, gaq_ref, gak_ref, gcq_ref, gckv_ref, gbq_ref, gbk_ref,
             xn_ref, gates_ref, pre_ref, qbpre_ref, kbpre_ref, cq_ref, ckv_ref,
             qa_ref, ka_ref, va_ref, qb_ref, kb_ref, vb_ref, proj):
        xn = _rms_fwd(x_ref[...], gin_ref[...], D_MODEL).astype(BF16)
        xn_ref[...] = xn
        proj[...] = _nn(xn, win_ref[...])
        gates_ref[...] = proj[:, GA0:GA0 + N_GATE]
        pre_ref[:, 0:VA0] = proj[:, 0:VA0]
        pre_ref[:, VA0:N_PRE] = proj[:, CQ0:KR0]
        ca, sa, cb, sb = ca_ref[...], sa_ref[...], cb_ref[...], sb_ref[...]
        for h in range(A_HEADS):
            yq = _rms_fwd(proj[:, QA0 + LANE * h:QA0 + LANE * (h + 1)], gaq_ref[...], A_DIM)
            qa_ref[h] = (_rope_fwd(yq, ca, sa, A_DIM // 4) * SCALE_A).astype(BF16)
        for h in range(A_KV):
            yk = _rms_fwd(proj[:, KA0 + LANE * h:KA0 + LANE * (h + 1)], gak_ref[...], A_DIM)
            ka_ref[h] = _rope_fwd(yk, ca, sa, A_DIM // 4).astype(BF16)
            va_ref[h] = proj[:, VA0 + LANE * h:VA0 + LANE * (h + 1)].astype(BF16)
        cq = _rms_fwd(proj[:, CQ0:CQ0 + B_Q_RANK], gcq_ref[...], B_Q_RANK).astype(BF16)
        cq_ref[...] = cq
        qbpre_ref[...] = _nn(cq, wuq_ref[...])
        ckv = _rms_fwd(proj[:, CKV0:CKV0 + B_KV_RANK], gckv_ref[...], B_KV_RANK).astype(BF16)
        ckv_ref[...] = ckv
        kvb = _nn(ckv, wukv_ref[...])
        kr = proj[:, KR0:KR0 + LANE]
        for h in range(B_HEADS):
            yq = _rms_fwd(qbpre_ref[:, LANE * h:LANE * (h + 1)], gbq_ref[...], B_QK)
            qb_ref[h] = (_rope_fwd(yq, cb, sb, B_ROPE // 4) * SCALE_B).astype(BF16)
            kp = kvb[:, LANE * h:LANE * (h + 1)] + kr
            kbpre_ref[:, LANE * h:LANE * (h + 1)] = kp
            kb_ref[h] = _rope_fwd(_rms_fwd(kp, gbk_ref[...], B_QK), cb, sb, B_ROPE // 4).astype(BF16)
            vb_ref[h] = kvb[:, B_HEADS * LANE + LANE * h:B_HEADS * LANE + LANE * (h + 1)].astype(BF16)

    row = lambda w: pl.BlockSpec((tm, w), lambda i: (i, 0))
    heads = lambda n: pl.BlockSpec((n, tm, LANE), lambda i: (0, i, 0))
    hs = lambda n: jax.ShapeDtypeStruct((n, s_len, LANE), BF16)
    return _pallas(
        body, name="pre", grid=(nt,),
        in_specs=[row(D_MODEL), row(LANE), row(LANE), row(LANE), row(LANE),
                  _full(w_in_ext.shape), _full(w_uq_pad.shape), _full(w_ukv_ext.shape)]
                 + [_full(g.shape) for g in gains],
        out_specs=[row(D_MODEL), row(N_GATE), row(N_PRE), row(B_HEADS * LANE), row(B_HEADS * LANE),
                   row(B_Q_RANK), row(B_KV_RANK),
                   heads(A_HEADS), heads(A_KV), heads(A_KV), heads(B_HEADS), heads(B_HEADS), heads(B_HEADS)],
        out_shape=[jax.ShapeDtypeStruct((s_len, D_MODEL), BF16), jax.ShapeDtypeStruct((s_len, N_GATE), F32),
                   jax.ShapeDtypeStruct((s_len, N_PRE), F32), jax.ShapeDtypeStruct((s_len, B_HEADS * LANE), F32),
                   jax.ShapeDtypeStruct((s_len, B_HEADS * LANE), F32),
                   jax.ShapeDtypeStruct((s_len, B_Q_RANK), BF16), jax.ShapeDtypeStruct((s_len, B_KV_RANK), BF16),
                   hs(A_HEADS), hs(A_KV), hs(A_KV), hs(B_HEADS), hs(B_HEADS), hs(B_HEADS)],
        scratch_shapes=[pltpu.VMEM((tm, N_EXT), F32)],
        compiler_params=_params(("parallel",), VMEM_LIMIT),
    )(x, *tabs, w_in_ext, w_uq_pad, w_ukv_ext, *gains)


def _attn_fwd(q, k, v, group, tq, tk, name):
    n_heads, s_len, _ = q.shape
    nk = s_len // tk

    def body(q_ref, k_ref, v_ref, o_ref, lse_ref):
        qv = q_ref[...]

        def step(j, carry):
            m, l, acc = carry
            rows = pl.ds(pl.multiple_of(j * tk, tk), tk)
            s = _nt(qv, k_ref[rows, :])
            m_new = jnp.maximum(m, jnp.max(s, axis=-1, keepdims=True))
            p = jnp.exp(s - m_new)
            alpha = jnp.exp(m - m_new)
            l = alpha * l + jnp.sum(p, axis=-1, keepdims=True)
            acc = alpha * acc + _nn(p.astype(BF16), v_ref[rows, :])
            return m_new, l, acc

        init = (jnp.full((tq, 1), -1e30, F32), jnp.zeros((tq, 1), F32), jnp.zeros((tq, LANE), F32))
        m, l, acc = lax.fori_loop(0, nk, step, init)
        o_ref[...] = acc * (1.0 / l)
        lse_ref[...] = m + jnp.log(l)

    return _pallas(
        body, name=name, grid=(n_heads, s_len // tq),
        in_specs=[pl.BlockSpec((None, tq, LANE), lambda h, i: (h, i, 0)),
                  pl.BlockSpec((None, s_len, LANE), lambda h, i: (h // group, 0, 0)),
                  pl.BlockSpec((None, s_len, LANE), lambda h, i: (h // group, 0, 0))],
        out_specs=[pl.BlockSpec((None, tq, LANE), lambda h, i: (h, i, 0)),
                   pl.BlockSpec((None, tq, 1), lambda h, i: (h, i, 0))],
        out_shape=[jax.ShapeDtypeStruct((n_heads, s_len, LANE), F32),
                   jax.ShapeDtypeStruct((n_heads, s_len, 1), F32)],
        compiler_params=_params(("parallel", "parallel"), 48 * 1024 * 1024),
    )(q, k, v)


def _mid(x, target, o_a, o_b, gates, w_out_ext, tm):
    s_len = x.shape[0]
    nt = s_len // tm
    n_heads = A_HEADS + B_HEADS

    def body(x_ref, t_ref, oa_ref, ob_ref, g_ref, w_ref,
             y_ref, dh_ref, dgate_ref, doa_ref, dob_ref, delta_ref, loss_ref, silu_scr, dsilu_scr):
        @pl.when(pl.program_id(0) == 0)
        def _():
            loss_ref[...] = jnp.zeros_like(loss_ref)

        def o_of(h):
            return oa_ref[h] if h < A_HEADS else ob_ref[h - A_HEADS]

        for h in range(n_heads):
            cols = slice(LANE * h, LANE * (h + 1))
            g = g_ref[:, cols]
            sig = 1.0 / (1.0 + jnp.exp(-g))
            silu = g * sig
            silu_scr[:, cols] = silu
            dsilu_scr[:, cols] = sig * (1.0 + g * (1.0 - sig))
            y_ref[:, cols] = (o_of(h) * silu).astype(BF16)
        err = x_ref[...] + _nn(y_ref[...], w_ref[...]) - t_ref[...]
        sq = jnp.sum(jnp.sum(err * err, axis=-1, keepdims=True), axis=0, keepdims=True)
        loss_ref[...] += jnp.broadcast_to(sq * (0.5 / D_MODEL), loss_ref.shape)
        dh = err * (1.0 / D_MODEL)
        dh_ref[...] = dh
        dy = _nt(dh.astype(BF16), w_ref[...])
        for h in range(n_heads):
            cols = slice(LANE * h, LANE * (h + 1))
            dyh = dy[:, cols]
            oh = o_of(h)
            do = dyh * silu_scr[:, cols]
            dgate_ref[:, cols] = (dyh * oh * dsilu_scr[:, cols]).astype(BF16)
            delta_ref[h] = jnp.sum(do * oh, axis=-1, keepdims=True)
            if h < A_HEADS:
                doa_ref[h] = do.astype(BF16)
            else:
                dob_ref[h - A_HEADS] = do.astype(BF16)

    row = lambda w: pl.BlockSpec((tm, w), lambda i: (i, 0))
    heads = lambda n, w=LANE: pl.BlockSpec((n, tm, w), lambda i: (0, i, 0))
    return _pallas(
        body, name="mid", grid=(nt,),
        in_specs=[row(D_MODEL), row(D_MODEL), heads(A_HEADS), heads(B_HEADS), row(N_GATE), _full(w_out_ext.shape)],
        out_specs=[row(N_GATE), row(D_MODEL), row(N_GATE), heads(A_HEADS), heads(B_HEADS), heads(n_heads, 1),
                   _full((8, LANE))],
        out_shape=[jax.ShapeDtypeStruct((s_len, N_GATE), BF16), jax.ShapeDtypeStruct((s_len, D_MODEL), F32),
                   jax.ShapeDtypeStruct((s_len, N_GATE), BF16),
                   jax.ShapeDtypeStruct((A_HEADS, s_len, LANE), BF16), jax.ShapeDtypeStruct((B_HEADS, s_len, LANE), BF16),
                   jax.ShapeDtypeStruct((n_heads, s_len, 1), F32), jax.ShapeDtypeStruct((8, LANE), F32)],
        scratch_shapes=[pltpu.VMEM((tm, N_GATE), F32), pltpu.VMEM((tm, N_GATE), F32)],
        compiler_params=_params(("arbitrary",), VMEM_LIMIT),
    )(x, target, o_a, o_b, gates, w_out_ext)


def _attn_bwd(q, k, v, do, lse, delta, group, tq, tk, name):
    n_heads, s_len, _ = q.shape
    nq = s_len // tq

    def body(q_ref, do_ref, lse_ref, delta_ref, k_ref, v_ref, dq_ref, dk_ref, dv_ref):
        @pl.when(pl.program_id(1) == 0)
        def _():
            dq_ref[...] = jnp.zeros_like(dq_ref)

        kv, vv = k_ref[...], v_ref[...]

        def step(i, carry):
            dk, dv = carry
            rows = pl.ds(pl.multiple_of(i * tq, tq), tq)
            qv, dov = q_ref[rows, :], do_ref[rows, :]
            p = jnp.exp(_nt(kv, qv) - lse_ref[i])
            ds = (p * (_nt(vv, dov) - delta_ref[i])).astype(BF16)
            dv = dv + _nn(p.astype(BF16), dov)
            dk = dk + _nn(ds, qv)
            dq_ref[rows, :] += _tn(ds, kv)
            return dk, dv

        zero = jnp.zeros((tk, LANE), F32)
        dk, dv = lax.fori_loop(0, nq, step, (zero, zero))
        dk_ref[...] = dk
        dv_ref[...] = dv

    whole = lambda: pl.BlockSpec((None, s_len, LANE), lambda h, j: (h, 0, 0))
    stat = lambda: pl.BlockSpec((None, nq, 1, tq), lambda h, j: (h, 0, 0, 0))
    kvb = lambda: pl.BlockSpec((None, tk, LANE), lambda h, j: (h // group, j, 0))
    outb = lambda: pl.BlockSpec((None, tk, LANE), lambda h, j: (h, j, 0))
    shape = jax.ShapeDtypeStruct((n_heads, s_len, LANE), F32)
    return _pallas(
        body, name=name, grid=(n_heads, s_len // tk),
        in_specs=[whole(), whole(), stat(), stat(), kvb(), kvb()],
        out_specs=[whole(), outb(), outb()],
        out_shape=[shape, shape, shape],
        compiler_params=_params(("parallel", "arbitrary"), 48 * 1024 * 1024),
    )(q, do, lse, delta, k, v)


def _post(x, dh, pre, qbpre, kbpre, dgate, dqa, dka, dva, dqb, dkb, dvb, tabs,
          w_in_ext, w_uq_pad, w_ukv_ext, gains, tm):
    s_len = x.shape[0]
    nt = s_len // tm

    def body(x_ref, dh_ref, pre_ref, qbpre_ref, kbpre_ref, dgate_ref,
             dqa_ref, dka_ref, dva_ref, dqb_ref, dkb_ref, dvb_ref,
             ca_ref, sa_ref, cb_ref, sb_ref, win_ref, wuq_ref, wukv_ref,
             gin_ref, gaq_ref, gak_ref, gcq_ref, gckv_ref, gbq_ref, gbk_ref,
             gx_ref, dproj_ref, dqbpre_ref, dkvb_ref,
             dgin_ref, dgaq_ref, dgak_ref, dgcq_ref, dgckv_ref, dgbq_ref, dgbk_ref):
        @pl.when(pl.program_id(0) == 0)
        def _():
            for r in (dgin_ref, dgaq_ref, dgak_ref, dgcq_ref, dgckv_ref, dgbq_ref, dgbk_ref):
                r[...] = jnp.zeros_like(r)

        def tok_sum(a):
            return jnp.sum(a, axis=0, keepdims=True)

        ca, sa, cb, sb = ca_ref[...], sa_ref[...], cb_ref[...], sb_ref[...]
        lane = lax.broadcasted_iota(jnp.int32, (tm, LANE), 1)
        dg = jnp.zeros((1, LANE), F32)
        for h in range(A_HEADS):
            dn = _rope_bwd(dqa_ref[h] * SCALE_A, ca, sa, A_DIM // 4)
            dx, dgr = _rms_bwd(dn, pre_ref[:, QA0 + LANE * h:QA0 + LANE * (h + 1)], gaq_ref[...], A_DIM)
            dproj_ref[:, QA0 + LANE * h:QA0 + LANE * (h + 1)] = dx.astype(BF16)
            dg = dg + tok_sum(dgr)
        dgaq_ref[...] += dg
        dg = jnp.zeros((1, LANE), F32)
        for h in range(A_KV):
            dk = dka_ref[A_GROUP * h]
            dv = dva_ref[A_GROUP * h]
            for g in range(1, A_GROUP):
                dk = dk + dka_ref[A_GROUP * h + g]
                dv = dv + dva_ref[A_GROUP * h + g]
            dn = _rope_bwd(dk, ca, sa, A_DIM // 4)
            dx, dgr = _rms_bwd(dn, pre_ref[:, KA0 + LANE * h:KA0 + LANE * (h + 1)], gak_ref[...], A_DIM)
            dproj_ref[:, KA0 + LANE * h:KA0 + LANE * (h + 1)] = dx.astype(BF16)
            dproj_ref[:, VA0 + LANE * h:VA0 + LANE * (h + 1)] = dv.astype(BF16)
            dg = dg + tok_sum(dgr)
        dgak_ref[...] += dg
        dproj_ref[:, GA0:GA0 + N_GATE] = dgate_ref[...]
        dg = jnp.zeros((1, LANE), F32)
        for h in range(B_HEADS):
            cols = slice(LANE * h, LANE * (h + 1))
            dn = _rope_bwd(dqb_ref[h] * SCALE_B, cb, sb, B_ROPE // 4)
            dx, dgr = _rms_bwd(dn, qbpre_ref[:, cols], gbq_ref[...], B_QK)
            dqbpre_ref[:, cols] = dx.astype(BF16)
            dg = dg + tok_sum(dgr)
        dgbq_ref[...] += dg
        dcq = _nt(dqbpre_ref[...], wuq_ref[...])
        dx, dgr = _rms_bwd(dcq, pre_ref[:, VA0:VA0 + B_Q_RANK], gcq_ref[...], B_Q_RANK)
        dproj_ref[:, CQ0:CQ0 + B_Q_RANK] = dx.astype(BF16)
        dgcq_ref[...] += tok_sum(dgr)
        dg = jnp.zeros((1, LANE), F32)
        dkr = jnp.zeros((tm, LANE), F32)
        for h in range(B_HEADS):
            cols = slice(LANE * h, LANE * (h + 1))
            dn = _rope_bwd(dkb_ref[h], cb, sb, B_ROPE // 4)
            dx, dgr = _rms_bwd(dn, kbpre_ref[:, cols], gbk_ref[...], B_QK)
            dkvb_ref[:, cols] = jnp.where(lane < B_NOPE, dx, 0.0).astype(BF16)
            dkvb_ref[:, B_HEADS * LANE + LANE * h:B_HEADS * LANE + LANE * (h + 1)] = dvb_ref[h].astype(BF16)
            dkr = dkr + dx
            dg = dg + tok_sum(dgr)
        dgbk_ref[...] += dg
        dproj_ref[:, KR0:KR0 + LANE] = jnp.where((lane >= B_NOPE) & (lane < B_QK), dkr, 0.0).astype(BF16)
        dckv = _nt(dkvb_ref[...], wukv_ref[...])
        dx, dgr = _rms_bwd(dckv, pre_ref[:, VA0 + B_Q_RANK:N_PRE], gckv_ref[...], B_KV_RANK)
        dproj_ref[:, CKV0:CKV0 + B_KV_RANK] = dx.astype(BF16)
        dgckv_ref[...] += tok_sum(dgr)
        dxn = _nt(dproj_ref[...], win_ref[...])
        dx, dgr = _rms_bwd(dxn, x_ref[...], gin_ref[...], D_MODEL)
        gx_ref[...] = dh_ref[...] + dx
        dgin_ref[...] += tok_sum(dgr)

    row = lambda w: pl.BlockSpec((tm, w), lambda i: (i, 0))
    heads = lambda n: pl.BlockSpec((n, tm, LANE), lambda i: (0, i, 0))
    return _pallas(
        body, name="post", grid=(nt,),
        in_specs=[row(D_MODEL), row(D_MODEL), row(N_PRE), row(B_HEADS * LANE), row(B_HEADS * LANE), row(N_GATE),
                  heads(A_HEADS), heads(A_HEADS), heads(A_HEADS), heads(B_HEADS), heads(B_HEADS), heads(B_HEADS),
                  row(LANE), row(LANE), row(LANE), row(LANE),
                  _full(w_in_ext.shape), _full(w_uq_pad.shape), _full(w_ukv_ext.shape)]
                 + [_full(g.shape) for g in gains],
        out_specs=[row(D_MODEL), row(N_EXT), row(B_HEADS * LANE), row(2 * B_HEADS * LANE)]
                  + [_full(g.shape) for g in gains],
        out_shape=[jax.ShapeDtypeStruct((s_len, D_MODEL), F32), jax.ShapeDtypeStruct((s_len, N_EXT), BF16),
                   jax.ShapeDtypeStruct((s_len, B_HEADS * LANE), BF16),
                   jax.ShapeDtypeStruct((s_len, 2 * B_HEADS * LANE), BF16)]
                  + [jax.ShapeDtypeStruct(g.shape, F32) for g in gains],
        compiler_params=_params(("arbitrary",), VMEM_LIMIT),
    )(x, dh, pre, qbpre, kbpre, dgate, dqa, dka, dva, dqb, dkb, dvb, *tabs,
      w_in_ext, w_uq_pad, w_ukv_ext, *gains)


def _grad_w(a, b, tn, ts, name):
    s_len, m = a.shape
    n = b.shape[1]

    def body(a_ref, b_ref, o_ref):
        @pl.when(pl.program_id(1) == 0)
        def _():
            o_ref[...] = jnp.zeros_like(o_ref)

        o_ref[...] += _tn(a_ref[...].astype(BF16), b_ref[...].astype(BF16))

    return _pallas(
        body, name=name, grid=(n // tn, s_len // ts),
        in_specs=[pl.BlockSpec((ts, m), lambda j, t: (t, 0)), pl.BlockSpec((ts, tn), lambda j, t: (t, j))],
        out_specs=pl.BlockSpec((m, tn), lambda j, t: (0, j)),
        out_shape=jax.ShapeDtypeStruct((m, n), F32),
        compiler_params=_params(("parallel", "arbitrary"), 48 * 1024 * 1024),
    )(a, b)


def _adamw(w, g, m, v, tr):
    rows = w.shape[0]

    def body(w_ref, g_ref, m_ref, v_ref, d_ref, nm_ref, nv_ref):
        gv = g_ref[...]
        nm = ADAM_B1 * m_ref[...] + (1.0 - ADAM_B1) * gv
        nv = ADAM_B2 * v_ref[...] + (1.0 - ADAM_B2) * (gv * gv)
        m_hat = nm / (1.0 - ADAM_B1 ** ADAM_STEP)
        v_hat = nv / (1.0 - ADAM_B2 ** ADAM_STEP)
        d_ref[...] = -ADAM_LR * (m_hat / (jnp.sqrt(v_hat) + ADAM_EPS) + ADAM_WD * w_ref[...])
        nm_ref[...] = nm
        nv_ref[...] = nv

    blk = pl.BlockSpec((tr, LANE), lambda i: (i, 0))
    shape = jax.ShapeDtypeStruct((rows, LANE), F32)
    return _pallas(
        body, name="adamw", grid=(rows // tr,),
        in_specs=[blk, blk, blk, blk], out_specs=[blk, blk, blk], out_shape=[shape, shape, shape],
        compiler_params=_params(("parallel",), 32 * 1024 * 1024),
    )(w, g, m, v)


def _rope_tables(s_len):
    rows = s_len // GRID_W
    row = jnp.repeat(jnp.arange(rows, dtype=F32), GRID_W)
    col = jnp.tile(jnp.arange(GRID_W, dtype=F32), rows)

    def tables(dim):
        half = dim // 2
        inv = 1.0 / (ROPE_THETA ** (jnp.arange(0, half, 2, dtype=F32) / half))
        ang_r, ang_c = row[:, None] * inv[None, :], col[:, None] * inv[None, :]
        return jnp.cos(ang_r), jnp.sin(ang_r), jnp.cos(ang_c), jnp.sin(ang_c)

    def lay(cr, sr, cc, sc, lead):
        ones = jnp.ones((s_len, lead), F32)
        zeros = jnp.zeros((s_len, lead), F32)
        cos = jnp.concatenate([ones, cr, cr, cc, cc], axis=1)
        sin = jnp.concatenate([zeros, -sr, sr, -sc, sc], axis=1)
        pad = LANE - cos.shape[1]
        return jnp.pad(cos, ((0, 0), (0, pad))), jnp.pad(sin, ((0, 0), (0, pad)))

    cos_a, sin_a = lay(*tables(A_DIM), 0)
    cos_b, sin_b = lay(*tables(B_ROPE), B_NOPE)
    return cos_a, sin_a, cos_b, sin_b


def _pad_heads(w, n_heads, dim, axis):
    shape = w.shape[:axis] + (n_heads, dim) + w.shape[axis + 1:]
    pad = [(0, 0)] * len(shape)
    pad[axis + 1] = (0, LANE - dim)
    out = jnp.pad(w.reshape(shape), pad)
    return out.reshape(w.shape[:axis] + (n_heads * LANE,) + w.shape[axis + 1:])


def _unpad_heads(w, n_heads, dim, axis):
    shape = w.shape[:axis] + (n_heads, LANE) + w.shape[axis + 1:]
    out = lax.slice_in_dim(w.reshape(shape), 0, dim, axis=axis + 1)
    return out.reshape(w.shape[:axis] + (n_heads * dim,) + w.shape[axis + 1:])


def _pad_vec(g):
    return jnp.pad(g, ((0, 0), (0, LANE - g.shape[1])))


def _pack(w_in, w_uq, w_ukv, w_out, smalls):
    parts = [w_in.reshape(R_IN, LANE), w_uq.reshape(R_UQ, LANE), w_ukv.reshape(R_UKV, LANE),
             w_out.reshape(R_OUT, LANE), smalls.reshape(R_SMALL, LANE),
             jnp.zeros((R_PACK - R_BIG - R_SMALL, LANE), smalls.dtype)]
    return jnp.concatenate(parts, axis=0)


def _unpack(p):
    r0, r1, r2 = R_IN, R_IN + R_UQ, R_IN + R_UQ + R_UKV
    smalls = p[R_BIG:R_BIG + R_SMALL].reshape(R_SMALL * LANE)
    return (p[:r0].reshape(SH_IN), p[r0:r1].reshape(SH_UQ), p[r1:r2].reshape(SH_UKV),
            p[r2:R_BIG].reshape(SH_OUT), smalls)


def _flat_smalls(vecs, loss=None):
    flat = jnp.concatenate([v.reshape(-1) for v in vecs])
    tail = jnp.zeros((R_SMALL * LANE - LOSS_AT,), F32)
    if loss is not None:
        tail = tail.at[0].set(loss)
    return jnp.concatenate([flat, tail])


def _split_smalls(flat):
    out, at = [], 0
    for n in SMALL_SIZES:
        out.append(flat[at:at + n].reshape(1, n))
        at += n
    return out


def _ext_weights(full):
    r0, r1, r2 = R_IN, R_IN + R_UQ, R_IN + R_UQ + R_UKV
    w_in = full[:, :r0].reshape((N_CHIPS,) + SH_IN).transpose(1, 0, 2).reshape(D_MODEL, N_IN)
    w_uq = full[:, r0:r1].reshape(B_Q_RANK, B_HEADS * B_QK)
    w_ukv = full[:, r1:r2].reshape((N_CHIPS,) + SH_UKV).transpose(1, 0, 2).reshape(B_KV_RANK, B_HEADS * (B_NOPE + B_V))
    w_out = full[:, r2:R_BIG].reshape(D_MODEL, D_MODEL)
    a_w = A_HEADS * A_DIM
    kv_w = A_KV * A_DIM
    o = 0
    secs = []
    for width, heads in ((a_w, A_HEADS), (kv_w, A_KV), (kv_w, A_KV), (a_w, A_HEADS)):
        secs.append(_pad_heads(w_in[:, o:o + width], heads, A_DIM, 1))
        o += width
    a_q, a_k, a_v, a_g = secs
    b_cq = w_in[:, o:o + B_Q_RANK]
    o += B_Q_RANK
    b_ckv = w_in[:, o:o + B_KV_RANK]
    o += B_KV_RANK
    b_kr = jnp.pad(w_in[:, o:o + B_ROPE], ((0, 0), (B_NOPE, LANE - B_QK)))
    o += B_ROPE
    b_g = w_in[:, o:]
    w_in_ext = jnp.concatenate([a_q, a_k, a_v, a_g, b_g, b_cq, b_ckv, b_kr], axis=1)
    w_uq_pad = _pad_heads(w_uq, B_HEADS, B_QK, 1)
    kv3 = w_ukv.reshape(B_KV_RANK, B_HEADS, B_NOPE + B_V)
    w_ukv_ext = jnp.concatenate([_pad_heads(kv3[:, :, :B_NOPE].reshape(B_KV_RANK, B_HEADS * B_NOPE), B_HEADS, B_NOPE, 1),
                                 kv3[:, :, B_NOPE:].reshape(B_KV_RANK, B_HEADS * B_V)], axis=1)
    w_out_ext = jnp.concatenate([_pad_heads(w_out[:a_w], A_HEADS, A_DIM, 0), w_out[a_w:]], axis=0)
    return w_in_ext, w_uq_pad, w_ukv_ext, w_out_ext


def _fold_grads(d_in_ext, d_uq_pad, d_ukv_ext, d_out_ext):
    d_in = jnp.concatenate([
        _unpad_heads(d_in_ext[:, QA0:KA0], A_HEADS, A_DIM, 1),
        _unpad_heads(d_in_ext[:, KA0:VA0], A_KV, A_DIM, 1),
        _unpad_heads(d_in_ext[:, VA0:GA0], A_KV, A_DIM, 1),
        _unpad_heads(d_in_ext[:, GA0:GB0], A_HEADS, A_DIM, 1),
        d_in_ext[:, CQ0:KR0],
        d_in_ext[:, KR0 + B_NOPE:KR0 + B_QK],
        d_in_ext[:, GB0:CQ0]], axis=1)
    d_uq = _unpad_heads(d_uq_pad, B_HEADS, B_QK, 1)
    k3 = _unpad_heads(d_ukv_ext[:, :B_HEADS * LANE], B_HEADS, B_NOPE, 1).reshape(B_KV_RANK, B_HEADS, B_NOPE)
    v3 = d_ukv_ext[:, B_HEADS * LANE:].reshape(B_KV_RANK, B_HEADS, B_V)
    d_ukv = jnp.concatenate([k3, v3], axis=2).reshape(B_KV_RANK, B_HEADS * (B_NOPE + B_V))
    d_out = jnp.concatenate([_unpad_heads(d_out_ext[:A_HEADS * LANE], A_HEADS, A_DIM, 0), d_out_ext[A_HEADS * LANE:]], axis=0)
    return (d_in.reshape(D_MODEL, N_CHIPS, SH_IN[1]).transpose(1, 0, 2),
            d_uq.reshape((N_CHIPS,) + SH_UQ),
            d_ukv.reshape(B_KV_RANK, N_CHIPS, SH_UKV[1]).transpose(1, 0, 2),
            d_out.reshape((N_CHIPS,) + SH_OUT))


def kernel(x, norm_in, w_in, a_q_norm, a_k_norm, b_cq_norm, b_ckv_norm, w_uq, w_ukv, b_q_norm, b_k_norm, w_out, loss_target, m_norm_in, m_w_in, m_a_q_norm, m_a_k_norm, m_b_cq_norm, m_b_ckv_norm, m_w_uq, m_w_ukv, m_b_q_norm, m_b_k_norm, m_w_out, v_norm_in, v_w_in, v_a_q_norm, v_a_k_norm, v_b_cq_norm, v_b_ckv_norm, v_w_uq, v_w_ukv, v_b_q_norm, v_b_k_norm, v_w_out):
    s_len = x.shape[1]
    xs, ts = x[0], loss_target[0]
    tm = min(256, s_len)
    tq = tk = min(512, s_len)

    small_w = (norm_in, a_q_norm, a_k_norm, b_cq_norm, b_ckv_norm, b_q_norm, b_k_norm)
    w_pack = _pack(w_in[0], w_uq[0], w_ukv[0], w_out[0], _flat_smalls(small_w))
    m_pack = _pack(m_w_in[0], m_w_uq[0], m_w_ukv[0], m_w_out[0],
                   _flat_smalls((m_norm_in, m_a_q_norm, m_a_k_norm, m_b_cq_norm, m_b_ckv_norm, m_b_q_norm, m_b_k_norm)))
    v_pack = _pack(v_w_in[0], v_w_uq[0], v_w_ukv[0], v_w_out[0],
                   _flat_smalls((v_norm_in, v_a_q_norm, v_a_k_norm, v_b_cq_norm, v_b_ckv_norm, v_b_q_norm, v_b_k_norm)))

    w_in_ext, w_uq_pad, w_ukv_ext, w_out_ext = _ext_weights(_gather_weights(w_pack))
    gains = (norm_in, _pad_vec(a_q_norm), _pad_vec(a_k_norm), b_cq_norm, b_ckv_norm, _pad_vec(b_q_norm), _pad_vec(b_k_norm))
    tabs = _rope_tables(s_len)

    (xn, gates, pre, qbpre, kbpre, cq, ckv, qa, ka, va, qb, kb, vb) = _pre(
        xs, tabs, w_in_ext, w_uq_pad, w_ukv_ext, gains, tm)
    o_a, lse_a = _attn_fwd(qa, ka, va, A_GROUP, tq, tk, "attn_fwd_a")
    o_b, lse_b = _attn_fwd(qb, kb, vb, 1, tq, tk, "attn_fwd_b")
    y, dh, dgate, do_a, do_b, delta, loss_part = _mid(xs, ts, o_a, o_b, gates, w_out_ext, tm)

    def stat(a):
        return a.reshape(a.shape[0], s_len // tq, 1, tq)

    dqa, dka, dva = _attn_bwd(qa, ka, va, do_a, stat(lse_a), stat(delta[:A_HEADS]), A_GROUP, tq, tk, "attn_bwd_a")
    dqb, dkb, dvb = _attn_bwd(qb, kb, vb, do_b, stat(lse_b), stat(delta[A_HEADS:]), 1, tq, tk, "attn_bwd_b")
    (grad_x, dproj, dqbpre, dkvb, *d_gains) = _post(
        xs, dh, pre, qbpre, kbpre, dgate, dqa, dka, dva, dqb, dkb, dvb, tabs, w_in_ext, w_uq_pad, w_ukv_ext, gains, tm)

    ts_w = min(512, s_len)
    d_in_ext = _grad_w(xn, dproj, 768, ts_w, "grad_w_in")
    d_out_ext = _grad_w(y, dh, 512, ts_w, "grad_w_out")
    d_uq_pad = _grad_w(cq, dqbpre, 512, ts_w, "grad_w_uq")
    d_ukv_ext = _grad_w(ckv, dkvb, 1024, ts_w, "grad_w_ukv")

    g_in, g_uq, g_ukv, g_out = _fold_grads(d_in_ext, d_uq_pad, d_ukv_ext, d_out_ext)
    d_small = _flat_smalls(
        [d_gains[0], d_gains[1][:, :A_DIM], d_gains[2][:, :A_DIM], d_gains[3], d_gains[4],
         d_gains[5][:, :B_QK], d_gains[6][:, :B_QK]], loss=loss_part[0, 0])
    parts = jnp.stack([_pack(g_in[j], g_uq[j], g_ukv[j], g_out[j], d_small) for j in range(N_CHIPS)])
    g_pack = _reduce_grads(parts)
    d_pack, nm_pack, nv_pack = _adamw(w_pack, g_pack, m_pack, v_pack, 1536)

    def leaves(p):
        p_in, p_uq, p_ukv, p_out, flat = _unpack(p)
        s = _split_smalls(flat)
        return [s[0], p_in[None], s[1], s[2], s[3], s[4], p_uq[None], p_ukv[None], s[5], s[6], p_out[None]], flat

    grads, g_flat = leaves(g_pack)
    loss = g_flat[LOSS_AT]
    return (loss, grad_x[None], *grads, *leaves(d_pack)[0], *leaves(nm_pack)[0], *leaves(nv_pack)[0])
```

```python
import functools

import jax
import jax.numpy as jnp
import numpy as np
from jax import lax
from jax.experimental import pallas as pl
from jax.experimental.pallas import tpu as pltpu

F32 = jnp.float32
BF16 = jnp.bfloat16
MESH = pl.DeviceIdType.MESH

D_MODEL = 1024
GRID_W = 64
ROPE_THETA = 10000.0
EPS = 1e-6
A_HEADS, A_KV, A_DIM = 8, 2, 64
A_GROUP = A_HEADS // A_KV
B_HEADS, B_NOPE, B_ROPE, B_V = 4, 64, 32, 128
B_QK = B_NOPE + B_ROPE
B_Q_RANK, B_KV_RANK = 384, 256
N_IN = 2464
SCALE_A = 1.0 / float(np.sqrt(A_DIM))
SCALE_B = 1.0 / float(np.sqrt(B_QK))
LOG2E = float(np.log2(np.e))
LN2 = float(np.log(2.0))
ADAM_LR, ADAM_B1, ADAM_B2, ADAM_EPS, ADAM_WD, ADAM_STEP = 0.001, 0.9, 0.999, 1e-08, 0.01, 10

LANE = 128
VMEM_BYTES = 64 * 1024 * 1024
VMEM_LIMIT = VMEM_BYTES - 8 * 1024 * 1024

QA0 = 0
KA0 = QA0 + A_HEADS * LANE
VA0 = KA0 + A_KV * LANE
GA0 = VA0 + A_KV * LANE
GB0 = GA0 + A_HEADS * LANE
CQ0 = GB0 + B_HEADS * LANE
CKV0 = CQ0 + B_Q_RANK
KR0 = CKV0 + B_KV_RANK
N_EXT = KR0 + LANE
N_GATE = (A_HEADS + B_HEADS) * LANE
N_PRE = KA0 + A_KV * LANE + B_Q_RANK + B_KV_RANK

N_CHIPS = 4
SH_IN = (D_MODEL, N_IN // N_CHIPS)
SH_UQ = (B_Q_RANK // N_CHIPS, B_HEADS * B_QK)
SH_UKV = (B_KV_RANK, B_HEADS * (B_NOPE + B_V) // N_CHIPS)
SH_OUT = (D_MODEL // N_CHIPS, D_MODEL)
R_IN = SH_IN[0] * SH_IN[1] // LANE
R_UQ = SH_UQ[0] * SH_UQ[1] // LANE
R_UKV = SH_UKV[0] * SH_UKV[1] // LANE
R_OUT = SH_OUT[0] * SH_OUT[1] // LANE
R_BIG = R_IN + R_UQ + R_UKV + R_OUT
R_SMALL = 16
R_PACK = 7680
R_HALF = R_PACK // 2
SMALL_SIZES = (D_MODEL, A_DIM, A_DIM, B_Q_RANK, B_KV_RANK, B_QK, B_QK)
LOSS_AT = sum(SMALL_SIZES)


def _pallas(body, **kw):
    return pl.pallas_call(body, **kw)


def _params(sem=None, vmem=None):
    return pltpu.CompilerParams(dimension_semantics=sem, vmem_limit_bytes=vmem)


def _rms_fwd(x, g, n):
    r = lax.rsqrt(jnp.sum(x * x, axis=-1, keepdims=True) * (1.0 / n) + EPS)
    return x * r * g


def _rms_bwd(dy, x, g, n):
    r = lax.rsqrt(jnp.sum(x * x, axis=-1, keepdims=True) * (1.0 / n) + EPS)
    xhat = x * r
    u = dy * g
    dx = r * (u - xhat * (jnp.sum(u * xhat, axis=-1, keepdims=True) * (1.0 / n)))
    return dx, dy * xhat


def _partner(y, half):
    lane = lax.broadcasted_iota(jnp.int32, y.shape, 1)
    first = (lane % (2 * half)) < half
    return jnp.where(first, pltpu.roll(y, LANE - half, 1), pltpu.roll(y, half, 1))


def _rope_fwd(y, cos, sin, half):
    return y * cos + _partner(y, half) * sin


def _rope_bwd(d, cos, sin, half):
    return d * cos - _partner(d, half) * sin


def _nt(a, b):
    return lax.dot_general(a, b, (((1,), (1,)), ((), ())), preferred_element_type=F32)


def _tn(a, b):
    return lax.dot_general(a, b, (((0,), (0,)), ((), ())), preferred_element_type=F32)


def _nn(a, b):
    return jnp.dot(a, b, preferred_element_type=F32)


def _block_rows(i, size):
    if isinstance(i, int):
        return pl.ds(i * size, size)
    return pl.ds(pl.multiple_of(i * size, size), size)


def _full(shape):
    return pl.BlockSpec(shape, lambda *_: (0,) * len(shape))


def _gather_weights(w_pack):
    def body(w_ref, out_ref, send_sems, recv_sems):
        x, y, c = lax.axis_index("x"), lax.axis_index("y"), lax.axis_index("c")
        sibling = (x, y, 1 - c)
        chips = [(1 - x, y), (x, 1 - y), (1 - x, 1 - y)]
        me = 2 * x + y

        def half(j, hc):
            return out_ref.at[j, pl.ds(pl.multiple_of(hc * R_HALF, R_HALF), R_HALF), :]

        def copy(k, j, hc, to):
            return pltpu.make_async_remote_copy(
                src_ref=half(j, hc), dst_ref=half(j, hc), send_sem=send_sems.at[k], recv_sem=recv_sems.at[k],
                device_id=to, device_id_type=MESH)

        out_ref[me] = w_ref[...].astype(BF16)
        first = [copy(k, me, c, (*chip, c)) for k, chip in enumerate(chips)]
        for cp in first:
            cp.start()
        passed = [copy(3 + k, 2 * chip[0] + chip[1], c, sibling) for k, chip in enumerate(chips)]
        for k, chip in enumerate(chips):
            copy(k, 2 * chip[0] + chip[1], c, (*chip, c)).wait_recv()
            passed[k].start()
        for k, chip in enumerate(chips):
            copy(3 + k, 2 * chip[0] + chip[1], 1 - c, sibling).wait_recv()
        for cp in first + passed:
            cp.wait_send()

    return _pallas(
        body, name="gather_weights",
        out_shape=jax.ShapeDtypeStruct((N_CHIPS, R_PACK, LANE), BF16),
        in_specs=[pl.BlockSpec(memory_space=pltpu.VMEM)],
        out_specs=pl.BlockSpec(memory_space=pltpu.VMEM),
        scratch_shapes=[pltpu.SemaphoreType.DMA((6,)), pltpu.SemaphoreType.DMA((6,))],
        compiler_params=_params(vmem=32 * 1024 * 1024),
    )(w_pack)


def _reduce_grads(parts):
    def body(p_ref, out_ref, rec_a, rec_b, sa_send, sa_recv, sb_send, sb_recv, sc_send, sc_recv):
        x, y, c = lax.axis_index("x"), lax.axis_index("y"), lax.axis_index("c")
        sibling = (x, y, 1 - c)
        me = 2 * x + y
        mine = pl.ds(pl.multiple_of(c * R_HALF, R_HALF), R_HALF)
        other = pl.ds(pl.multiple_of((1 - c) * R_HALF, R_HALF), R_HALF)

        def copy_a(j):
            return pltpu.make_async_remote_copy(
                src_ref=p_ref.at[j, other, :], dst_ref=rec_a.at[j], send_sem=sa_send.at[j], recv_sem=sa_recv.at[j],
                device_id=sibling, device_id_type=MESH)

        def copy_b(r):
            j = me ^ r
            return pltpu.make_async_remote_copy(
                src_ref=rec_a.at[j], dst_ref=rec_b.at[r], send_sem=sb_send.at[r - 1], recv_sem=sb_recv.at[r - 1],
                device_id=(j // 2, j % 2, c), device_id_type=MESH)

        copy_c = pltpu.make_async_remote_copy(
            src_ref=out_ref.at[mine, :], dst_ref=out_ref.at[mine, :], send_sem=sc_send, recv_sem=sc_recv,
            device_id=sibling, device_id_type=MESH)

        for j in range(N_CHIPS):
            copy_a(j).start()
        for r in range(1, N_CHIPS):
            j = me ^ r
            copy_a(j).wait_recv()
            rec_a[j] = rec_a[j] + p_ref[j, mine, :]
            copy_b(r).start()
        copy_a(me).wait_recv()
        rec_b[0] = rec_a[me] + p_ref[me, mine, :]
        for r in range(1, N_CHIPS):
            copy_b(r).wait_recv()
        total = rec_b[me]
        for j in range(1, N_CHIPS):
            total = total + rec_b[j ^ me]
        out_ref[mine, :] = total
        copy_c.start()
        copy_c.wait_recv()
        for j in range(N_CHIPS):
            copy_a(j).wait_send()
        for r in range(1, N_CHIPS):
            copy_b(r).wait_send()
        copy_c.wait_send()

    return _pallas(
        body, name="reduce_grads",
        out_shape=jax.ShapeDtypeStruct((R_PACK, LANE), F32),
        in_specs=[pl.BlockSpec(memory_space=pltpu.VMEM)],
        out_specs=pl.BlockSpec(memory_space=pltpu.VMEM),
        scratch_shapes=[
            pltpu.VMEM((N_CHIPS, R_HALF, LANE), F32),
            pltpu.VMEM((N_CHIPS, R_HALF, LANE), F32),
            pltpu.SemaphoreType.DMA((N_CHIPS,)), pltpu.SemaphoreType.DMA((N_CHIPS,)),
            pltpu.SemaphoreType.DMA((N_CHIPS - 1,)), pltpu.SemaphoreType.DMA((N_CHIPS - 1,)),
            pltpu.SemaphoreType.DMA, pltpu.SemaphoreType.DMA,
        ],
        compiler_params=_params(vmem=VMEM_LIMIT),
    )(parts)


def _pre(x, tabs, w_in_ext, w_uq_pad, w_ukv_ext, gains, tm):
    s_len = x.shape[0]
    nt = s_len // tm

    def body(x_ref, ca_ref, sa_ref, cb_ref, sb_ref, win_ref, wuq_ref, wukv_ref,
             gin_ref, gaq_ref, gak_ref, gcq_ref, gckv_ref, gbq_ref, gbk_ref,
             xn_ref, gates_ref, pre_ref, qbpre_ref, kbpre_ref, cq_ref, ckv_ref,
             qa_ref, ka_ref, va_ref, qb_ref, kb_ref, vb_ref, proj):
        xn = _rms_fwd(x_ref[...], gin_ref[...], D_MODEL).astype(BF16)
        xn_ref[...] = xn
        proj[...] = _nn(xn, win_ref[...])
        gates_ref[...] = proj[:, GA0:GA0 + N_GATE]
        pre_ref[:, 0:VA0] = proj[:, 0:VA0]
        pre_ref[:, VA0:N_PRE] = proj[:, CQ0:KR0]
        ca, sa, cb, sb = ca_ref[...], sa_ref[...], cb_ref[...], sb_ref[...]
        lane = lax.broadcasted_iota(jnp.int32, (tm, LANE), 1)
        for h in range(A_HEADS):
            yq = _rms_fwd(proj[:, QA0 + LANE * h:QA0 + LANE * (h + 1)], gaq_ref[...], A_DIM)
            qa_ref[h] = (_rope_fwd(yq, ca, sa, A_DIM // 4) * (SCALE_A * LOG2E)).astype(BF16)
        for h in range(A_KV):
            yk = _rms_fwd(proj[:, KA0 + LANE * h:KA0 + LANE * (h + 1)], gak_ref[...], A_DIM)
            ka_ref[h] = _rope_fwd(yk, ca, sa, A_DIM // 4).astype(BF16)
            va_ref[h] = jnp.where(lane == A_DIM, 1.0, proj[:, VA0 + LANE * h:VA0 + LANE * (h + 1)]).astype(BF16)
        cq = _rms_fwd(proj[:, CQ0:CQ0 + B_Q_RANK], gcq_ref[...], B_Q_RANK).astype(BF16)
        cq_ref[...] = cq
        qbpre_ref[...] = _nn(cq, wuq_ref[...])
        ckv = _rms_fwd(proj[:, CKV0:CKV0 + B_KV_RANK], gckv_ref[...], B_KV_RANK).astype(BF16)
        ckv_ref[...] = ckv
        kvb = _nn(ckv, wukv_ref[...])
        kr = proj[:, KR0:KR0 + LANE]
        for h in range(B_HEADS):
            yq = _rms_fwd(qbpre_ref[:, LANE * h:LANE * (h + 1)], gbq_ref[...], B_QK)
            qb_ref[h] = (_rope_fwd(yq, cb, sb, B_ROPE // 4) * (SCALE_B * LOG2E)).astype(BF16)
            kp = kvb[:, LANE * h:LANE * (h + 1)] + kr
            kbpre_ref[:, LANE * h:LANE * (h + 1)] = kp
            kb_ref[h] = _rope_fwd(_rms_fwd(kp, gbk_ref[...], B_QK), cb, sb, B_ROPE // 4).astype(BF16)
            vb_ref[h, :, 0:LANE] = kvb[:, B_HEADS * LANE + LANE * h:B_HEADS * LANE + LANE * (h + 1)].astype(BF16)
            vb_ref[h, :, LANE:2 * LANE] = jnp.where(lane == 0, 1.0, 0.0).astype(BF16)

    row = lambda w: pl.BlockSpec((tm, w), lambda i: (i, 0))
    heads = lambda n: pl.BlockSpec((n, tm, LANE), lambda i: (0, i, 0))
    hs = lambda n: jax.ShapeDtypeStruct((n, s_len, LANE), BF16)
    return _pallas(
        body, name="pre", grid=(nt,),
        in_specs=[row(D_MODEL), row(LANE), row(LANE), row(LANE), row(LANE),
                  _full(w_in_ext.shape), _full(w_uq_pad.shape), _full(w_ukv_ext.shape)]
                 + [_full(g.shape) for g in gains],
        out_specs=[row(D_MODEL), row(N_GATE), row(N_PRE), row(B_HEADS * LANE), row(B_HEADS * LANE),
                   row(B_Q_RANK), row(B_KV_RANK),
                   heads(A_HEADS), heads(A_KV), heads(A_KV), heads(B_HEADS), heads(B_HEADS),
                   pl.BlockSpec((B_HEADS, tm, 2 * LANE), lambda i: (0, i, 0))],
        out_shape=[jax.ShapeDtypeStruct((s_len, D_MODEL), BF16), jax.ShapeDtypeStruct((s_len, N_GATE), F32),
                   jax.ShapeDtypeStruct((s_len, N_PRE), F32), jax.ShapeDtypeStruct((s_len, B_HEADS * LANE), F32),
                   jax.ShapeDtypeStruct((s_len, B_HEADS * LANE), F32),
                   jax.ShapeDtypeStruct((s_len, B_Q_RANK), BF16), jax.ShapeDtypeStruct((s_len, B_KV_RANK), BF16),
                   hs(A_HEADS), hs(A_KV), hs(A_KV), hs(B_HEADS), hs(B_HEADS),
                   jax.ShapeDtypeStruct((B_HEADS, s_len, 2 * LANE), BF16)],
        scratch_shapes=[pltpu.VMEM((tm, N_EXT), F32)],
        compiler_params=_params(("parallel",), VMEM_LIMIT),
    )(x, *tabs, w_in_ext, w_uq_pad, w_ukv_ext, *gains)


def _attn_fwd(q, k, v, group, l_col, tq, tk, name):
    n_heads, s_len, _ = q.shape
    v_w = v.shape[2]
    nk = s_len // tk
    assert nk >= 2 and nk % 2 == 0

    def body(q_ref, k_ref, v_ref, o_ref, lse_ref, s_buf, p_buf, a_buf, m_ref, acc_ref):
        def rows(j):
            return _block_rows(j, tk)

        def scores(j, slot):
            s_buf[slot] = _nt(q_ref[...], k_ref[rows(j), :])

        def softmax(slot):
            s = s_buf[slot]
            m_old = m_ref[...]
            m_new = jnp.maximum(m_old, jnp.max(s, axis=-1, keepdims=True))
            m_ref[...] = m_new
            a_buf[slot] = jnp.exp2(m_old - m_new)
            p_buf[slot] = jnp.exp2(s - jnp.tile(m_new, (1, tk // LANE))).astype(BF16)

        def values(j, slot):
            pv = _nn(p_buf[slot], v_ref[rows(j), :])
            for c in range(0, v_w, LANE):
                acc_ref[:, c:c + LANE] = a_buf[slot] * acc_ref[:, c:c + LANE] + pv[:, c:c + LANE]

        m_ref[...] = jnp.full(m_ref.shape, -1e30, F32)
        acc_ref[...] = jnp.zeros(acc_ref.shape, F32)
        scores(0, 0)
        scores(1, 1)
        softmax(0)

        def pair(t, carry):
            j = 2 * t + 1
            scores(j + 1, 0)
            softmax(1)
            values(j - 1, 0)
            scores(j + 2, 1)
            softmax(0)
            values(j, 1)
            return carry

        lax.fori_loop(0, (nk - 2) // 2, pair, 0)
        softmax(1)
        values(nk - 2, 0)
        values(nk - 1, 1)
        l = acc_ref[:, l_col:l_col + 1]
        o = acc_ref[:, 0:LANE] * (1.0 / l)
        if l_col < LANE:
            lane = lax.broadcasted_iota(jnp.int32, o.shape, 1)
            o = jnp.where(lane == l_col, 0.0, o)
        o_ref[...] = o
        lse_ref[...] = m_ref[:, 0:1] + jnp.log2(l)

    return _pallas(
        body, name=name, grid=(n_heads, s_len // tq),
        in_specs=[pl.BlockSpec((None, tq, LANE), lambda h, i: (h, i, 0)),
                  pl.BlockSpec((None, s_len, LANE), lambda h, i: (h // group, 0, 0)),
                  pl.BlockSpec((None, s_len, v_w), lambda h, i: (h // group, 0, 0))],
        out_specs=[pl.BlockSpec((None, tq, LANE), lambda h, i: (h, i, 0)),
                   pl.BlockSpec((None, tq, 1), lambda h, i: (h, i, 0))],
        out_shape=[jax.ShapeDtypeStruct((n_heads, s_len, LANE), F32),
                   jax.ShapeDtypeStruct((n_heads, s_len, 1), F32)],
        scratch_shapes=[pltpu.VMEM((2, tq, tk), F32), pltpu.VMEM((2, tq, tk), BF16), pltpu.VMEM((2, tq, LANE), F32),
                        pltpu.VMEM((tq, LANE), F32), pltpu.VMEM((tq, v_w), F32)],
        compiler_params=_params(("parallel", "parallel"), 48 * 1024 * 1024),
    )(q, k, v)


def _mid(x, target, o_a, o_b, gates, w_out_ext, tm):
    s_len = x.shape[0]
    nt = s_len // tm
    n_heads = A_HEADS + B_HEADS

    def body(x_ref, t_ref, oa_ref, ob_ref, g_ref, w_ref,
             y_ref, dh_ref, dgate_ref, doa_ref, dob_ref, delta_ref, loss_ref, silu_scr, dsilu_scr):
        @pl.when(pl.program_id(0) == 0)
        def _():
            loss_ref[...] = jnp.zeros_like(loss_ref)

        def o_of(h):
            return oa_ref[h] if h < A_HEADS else ob_ref[h - A_HEADS]

        for h in range(n_heads):
            cols = slice(LANE * h, LANE * (h + 1))
            g = g_ref[:, cols]
            sig = 1.0 / (1.0 + jnp.exp(-g))
            silu = g * sig
            silu_scr[:, cols] = silu
            dsilu_scr[:, cols] = sig * (1.0 + g * (1.0 - sig))
            y_ref[:, cols] = (o_of(h) * silu).astype(BF16)
        err = x_ref[...] + _nn(y_ref[...], w_ref[...]) - t_ref[...]
        sq = jnp.sum(jnp.sum(err * err, axis=-1, keepdims=True), axis=0, keepdims=True)
        loss_ref[...] += jnp.broadcast_to(sq * (0.5 / D_MODEL), loss_ref.shape)
        dh = err * (1.0 / D_MODEL)
        dh_ref[...] = dh
        dy = _nt(dh.astype(BF16), w_ref[...])
        for h in range(n_heads):
            cols = slice(LANE * h, LANE * (h + 1))
            dyh = dy[:, cols]
            oh = o_of(h)
            do = dyh * silu_scr[:, cols]
            dgate_ref[:, cols] = (dyh * oh * dsilu_scr[:, cols]).astype(BF16)
            delta_ref[h] = jnp.sum(do * oh, axis=-1, keepdims=True)
            if h < A_HEADS:
                doa_ref[h] = do.astype(BF16)
            else:
                dob_ref[h - A_HEADS] = do.astype(BF16)

    row = lambda w: pl.BlockSpec((tm, w), lambda i: (i, 0))
    heads = lambda n, w=LANE: pl.BlockSpec((n, tm, w), lambda i: (0, i, 0))
    return _pallas(
        body, name="mid", grid=(nt,),
        in_specs=[row(D_MODEL), row(D_MODEL), heads(A_HEADS), heads(B_HEADS), row(N_GATE), _full(w_out_ext.shape)],
        out_specs=[row(N_GATE), row(D_MODEL), row(N_GATE), heads(A_HEADS), heads(B_HEADS), heads(n_heads, 1),
                   _full((8, LANE))],
        out_shape=[jax.ShapeDtypeStruct((s_len, N_GATE), BF16), jax.ShapeDtypeStruct((s_len, D_MODEL), F32),
                   jax.ShapeDtypeStruct((s_len, N_GATE), BF16),
                   jax.ShapeDtypeStruct((A_HEADS, s_len, LANE), BF16), jax.ShapeDtypeStruct((B_HEADS, s_len, LANE), BF16),
                   jax.ShapeDtypeStruct((n_heads, s_len, 1), F32), jax.ShapeDtypeStruct((8, LANE), F32)],
        scratch_shapes=[pltpu.VMEM((tm, N_GATE), F32), pltpu.VMEM((tm, N_GATE), F32)],
        compiler_params=_params(("arbitrary",), VMEM_LIMIT),
    )(x, target, o_a, o_b, gates, w_out_ext)


def _attn_bwd(q, k, v, do, lse, delta, group, tq, tk, name):
    n_heads, s_len, _ = q.shape
    nq = s_len // tq
    assert nq >= 2 and nq % 2 == 0

    def body(q_ref, do_ref, lse_ref, delta_ref, k_ref, v_ref, dq_ref, dk_ref, dv_ref, s_buf, dp_buf, p_buf, ds_buf):
        @pl.when(pl.program_id(1) == 0)
        def _():
            dq_ref[...] = jnp.zeros_like(dq_ref)

        dk_ref[...] = jnp.zeros_like(dk_ref)
        dv_ref[...] = jnp.zeros_like(dv_ref)

        def rows(i):
            return _block_rows(i, tq)

        def scores(i, slot):
            s_buf[slot] = _nt(k_ref[...], q_ref[rows(i), :])
            dp_buf[slot] = _nt(v_ref[...], do_ref[rows(i), :])

        def elementwise(i, slot):
            p = jnp.exp2(s_buf[slot] - lse_ref[i])
            p_buf[slot] = p.astype(BF16)
            ds_buf[slot] = (p * (dp_buf[slot] - delta_ref[i])).astype(BF16)

        def grads(i, slot):
            dv_ref[...] += _nn(p_buf[slot], do_ref[rows(i), :])
            dk_ref[...] += _nn(ds_buf[slot], q_ref[rows(i), :])
            dq_ref[rows(i), :] += _tn(ds_buf[slot], k_ref[...])

        scores(0, 0)
        scores(1, 1)
        elementwise(0, 0)

        def pair(t, carry):
            i = 2 * t + 1
            scores(i + 1, 0)
            elementwise(i, 1)
            grads(i - 1, 0)
            scores(i + 2, 1)
            elementwise(i + 1, 0)
            grads(i, 1)
            return carry

        lax.fori_loop(0, (nq - 2) // 2, pair, 0)
        elementwise(nq - 1, 1)
        grads(nq - 2, 0)
        grads(nq - 1, 1)

    whole = lambda: pl.BlockSpec((None, s_len, LANE), lambda h, j: (h, 0, 0))
    stat = lambda: pl.BlockSpec((None, nq, 1, tq), lambda h, j: (h, 0, 0, 0))
    kvb = lambda: pl.BlockSpec((None, tk, LANE), lambda h, j: (h // group, j, 0))
    outb = lambda: pl.BlockSpec((None, tk, LANE), lambda h, j: (h, j, 0))
    shape = jax.ShapeDtypeStruct((n_heads, s_len, LANE), F32)
    return _pallas(
        body, name=name, grid=(n_heads, s_len // tk),
        in_specs=[whole(), whole(), stat(), stat(), kvb(), kvb()],
        out_specs=[whole(), outb(), outb()],
        out_shape=[shape, shape, shape],
        scratch_shapes=[pltpu.VMEM((2, tk, tq), F32), pltpu.VMEM((2, tk, tq), F32),
                        pltpu.VMEM((2, tk, tq), BF16), pltpu.VMEM((2, tk, tq), BF16)],
        compiler_params=_params(("parallel", "arbitrary"), 48 * 1024 * 1024),
    )(q, do, lse, delta, k, v)


def _post(x, dh, pre, qbpre, kbpre, dgate, dqa, dka, dva, dqb, dkb, dvb, tabs,
          w_in_ext, w_uq_pad, w_ukv_ext, gains, tm):
    s_len = x.shape[0]
    nt = s_len // tm

    def body(x_ref, dh_ref, pre_ref, qbpre_ref, kbpre_ref, dgate_ref,
             dqa_ref, dka_ref, dva_ref, dqb_ref, dkb_ref, dvb_ref,
             ca_ref, sa_ref, cb_ref, sb_ref, win_ref, wuq_ref, wukv_ref,
             gin_ref, gaq_ref, gak_ref, gcq_ref, gckv_ref, gbq_ref, gbk_ref,
             gx_ref, dproj_ref, dqbpre_ref, dkvb_ref,
             dgin_ref, dgaq_ref, dgak_ref, dgcq_ref, dgckv_ref, dgbq_ref, dgbk_ref):
        @pl.when(pl.program_id(0) == 0)
        def _():
            for r in (dgin_ref, dgaq_ref, dgak_ref, dgcq_ref, dgckv_ref, dgbq_ref, dgbk_ref):
                r[...] = jnp.zeros_like(r)

        def tok_sum(a):
            return jnp.sum(a, axis=0, keepdims=True)

        ca, sa, cb, sb = ca_ref[...], sa_ref[...], cb_ref[...], sb_ref[...]
        lane = lax.broadcasted_iota(jnp.int32, (tm, LANE), 1)
        dg = jnp.zeros((1, LANE), F32)
        for h in range(A_HEADS):
            dn = _rope_bwd(dqa_ref[h] * SCALE_A, ca, sa, A_DIM // 4)
            dx, dgr = _rms_bwd(dn, pre_ref[:, QA0 + LANE * h:QA0 + LANE * (h + 1)], gaq_ref[...], A_DIM)
            dproj_ref[:, QA0 + LANE * h:QA0 + LANE * (h + 1)] = dx.astype(BF16)
            dg = dg + tok_sum(dgr)
        dgaq_ref[...] += dg
        dg = jnp.zeros((1, LANE), F32)
        for h in range(A_KV):
            dk = dka_ref[A_GROUP * h]
            dv = dva_ref[A_GROUP * h]
            for g in range(1, A_GROUP):
                dk = dk + dka_ref[A_GROUP * h + g]
                dv = dv + dva_ref[A_GROUP * h + g]
            dn = _rope_bwd(dk * LN2, ca, sa, A_DIM // 4)
            dx, dgr = _rms_bwd(dn, pre_ref[:, KA0 + LANE * h:KA0 + LANE * (h + 1)], gak_ref[...], A_DIM)
            dproj_ref[:, KA0 + LANE * h:KA0 + LANE * (h + 1)] = dx.astype(BF16)
            dproj_ref[:, VA0 + LANE * h:VA0 + LANE * (h + 1)] = dv.astype(BF16)
            dg = dg + tok_sum(dgr)
        dgak_ref[...] += dg
        dproj_ref[:, GA0:GA0 + N_GATE] = dgate_ref[...]
        dg = jnp.zeros((1, LANE), F32)
        for h in range(B_HEADS):
            cols = slice(LANE * h, LANE * (h + 1))
            dn = _rope_bwd(dqb_ref[h] * SCALE_B, cb, sb, B_ROPE // 4)
            dx, dgr = _rms_bwd(dn, qbpre_ref[:, cols], gbq_ref[...], B_QK)
            dqbpre_ref[:, cols] = dx.astype(BF16)
            dg = dg + tok_sum(dgr)
        dgbq_ref[...] += dg
        dcq = _nt(dqbpre_ref[...], wuq_ref[...])
        dx, dgr = _rms_bwd(dcq, pre_ref[:, VA0:VA0 + B_Q_RANK], gcq_ref[...], B_Q_RANK)
        dproj_ref[:, CQ0:CQ0 + B_Q_RANK] = dx.astype(BF16)
        dgcq_ref[...] += tok_sum(dgr)
        dg = jnp.zeros((1, LANE), F32)
        dkr = jnp.zeros((tm, LANE), F32)
        for h in range(B_HEADS):
            cols = slice(LANE * h, LANE * (h + 1))
            dn = _rope_bwd(dkb_ref[h] * LN2, cb, sb, B_ROPE // 4)
            dx, dgr = _rms_bwd(dn, kbpre_ref[:, cols], gbk_ref[...], B_QK)
            dkvb_ref[:, cols] = jnp.where(lane < B_NOPE, dx, 0.0).astype(BF16)
            dkvb_ref[:, B_HEADS * LANE + LANE * h:B_HEADS * LANE + LANE * (h + 1)] = dvb_ref[h].astype(BF16)
            dkr = dkr + dx
            dg = dg + tok_sum(dgr)
        dgbk_ref[...] += dg
        dproj_ref[:, KR0:KR0 + LANE] = jnp.where((lane >= B_NOPE) & (lane < B_QK), dkr, 0.0).astype(BF16)
        dckv = _nt(dkvb_ref[...], wukv_ref[...])
        dx, dgr = _rms_bwd(dckv, pre_ref[:, VA0 + B_Q_RANK:N_PRE], gckv_ref[...], B_KV_RANK)
        dproj_ref[:, CKV0:CKV0 + B_KV_RANK] = dx.astype(BF16)
        dgckv_ref[...] += tok_sum(dgr)
        dxn = _nt(dproj_ref[...], win_ref[...])
        dx, dgr = _rms_bwd(dxn, x_ref[...], gin_ref[...], D_MODEL)
        gx_ref[...] = dh_ref[...] + dx
        dgin_ref[...] += tok_sum(dgr)

    row = lambda w: pl.BlockSpec((tm, w), lambda i: (i, 0))
    heads = lambda n: pl.BlockSpec((n, tm, LANE), lambda i: (0, i, 0))
    return _pallas(
        body, name="post", grid=(nt,),
        in_specs=[row(D_MODEL), row(D_MODEL), row(N_PRE), row(B_HEADS * LANE), row(B_HEADS * LANE), row(N_GATE),
                  heads(A_HEADS), heads(A_HEADS), heads(A_HEADS), heads(B_HEADS), heads(B_HEADS), heads(B_HEADS),
                  row(LANE), row(LANE), row(LANE), row(LANE),
                  _full(w_in_ext.shape), _full(w_uq_pad.shape), _full(w_ukv_ext.shape)]
                 + [_full(g.shape) for g in gains],
        out_specs=[row(D_MODEL), row(N_EXT), row(B_HEADS * LANE), row(2 * B_HEADS * LANE)]
                  + [_full(g.shape) for g in gains],
        out_shape=[jax.ShapeDtypeStruct((s_len, D_MODEL), F32), jax.ShapeDtypeStruct((s_len, N_EXT), BF16),
                   jax.ShapeDtypeStruct((s_len, B_HEADS * LANE), BF16),
                   jax.ShapeDtypeStruct((s_len, 2 * B_HEADS * LANE), BF16)]
                  + [jax.ShapeDtypeStruct(g.shape, F32) for g in gains],
        compiler_params=_params(("arbitrary",), VMEM_LIMIT),
    )(x, dh, pre, qbpre, kbpre, dgate, dqa, dka, dva, dqb, dkb, dvb, *tabs,
      w_in_ext, w_uq_pad, w_ukv_ext, *gains)


def _grad_w(a, b, tn, ts, name):
    s_len, m = a.shape
    n = b.shape[1]

    def body(a_ref, b_ref, o_ref):
        @pl.when(pl.program_id(1) == 0)
        def _():
            o_ref[...] = jnp.zeros_like(o_ref)

        o_ref[...] += _tn(a_ref[...].astype(BF16), b_ref[...].astype(BF16))

    return _pallas(
        body, name=name, grid=(n // tn, s_len // ts),
        in_specs=[pl.BlockSpec((ts, m), lambda j, t: (t, 0)), pl.BlockSpec((ts, tn), lambda j, t: (t, j))],
        out_specs=pl.BlockSpec((m, tn), lambda j, t: (0, j)),
        out_shape=jax.ShapeDtypeStruct((m, n), F32),
        compiler_params=_params(("parallel", "arbitrary"), 48 * 1024 * 1024),
    )(a, b)


def _adamw(w, g, m, v, tr):
    rows = w.shape[0]

    def body(w_ref, g_ref, m_ref, v_ref, d_ref, nm_ref, nv_ref):
        gv = g_ref[...]
        nm = ADAM_B1 * m_ref[...] + (1.0 - ADAM_B1) * gv
        nv = ADAM_B2 * v_ref[...] + (1.0 - ADAM_B2) * (gv * gv)
        m_hat = nm / (1.0 - ADAM_B1 ** ADAM_STEP)
        v_hat = nv / (1.0 - ADAM_B2 ** ADAM_STEP)
        d_ref[...] = -ADAM_LR * (m_hat / (jnp.sqrt(v_hat) + ADAM_EPS) + ADAM_WD * w_ref[...])
        nm_ref[...] = nm
        nv_ref[...] = nv

    blk = pl.BlockSpec((tr, LANE), lambda i: (i, 0))
    shape = jax.ShapeDtypeStruct((rows, LANE), F32)
    return _pallas(
        body, name="adamw", grid=(rows // tr,),
        in_specs=[blk, blk, blk, blk], out_specs=[blk, blk, blk], out_shape=[shape, shape, shape],
        compiler_params=_params(("parallel",), 32 * 1024 * 1024),
    )(w, g, m, v)


def _rope_tables(s_len):
    rows = s_len // GRID_W
    row = jnp.repeat(jnp.arange(rows, dtype=F32), GRID_W)
    col = jnp.tile(jnp.arange(GRID_W, dtype=F32), rows)

    def tables(dim):
        half = dim // 2
        inv = 1.0 / (ROPE_THETA ** (jnp.arange(0, half, 2, dtype=F32) / half))
        ang_r, ang_c = row[:, None] * inv[None, :], col[:, None] * inv[None, :]
        return jnp.cos(ang_r), jnp.sin(ang_r), jnp.cos(ang_c), jnp.sin(ang_c)

    def lay(cr, sr, cc, sc, lead):
        ones = jnp.ones((s_len, lead), F32)
        zeros = jnp.zeros((s_len, lead), F32)
        cos = jnp.concatenate([ones, cr, cr, cc, cc], axis=1)
        sin = jnp.concatenate([zeros, -sr, sr, -sc, sc], axis=1)
        pad = LANE - cos.shape[1]
        return jnp.pad(cos, ((0, 0), (0, pad))), jnp.pad(sin, ((0, 0), (0, pad)))

    cos_a, sin_a = lay(*tables(A_DIM), 0)
    cos_b, sin_b = lay(*tables(B_ROPE), B_NOPE)
    return cos_a, sin_a, cos_b, sin_b


def _pad_heads(w, n_heads, dim, axis):
    shape = w.shape[:axis] + (n_heads, dim) + w.shape[axis + 1:]
    pad = [(0, 0)] * len(shape)
    pad[axis + 1] = (0, LANE - dim)
    out = jnp.pad(w.reshape(shape), pad)
    return out.reshape(w.shape[:axis] + (n_heads * LANE,) + w.shape[axis + 1:])


def _unpad_heads(w, n_heads, dim, axis):
    shape = w.shape[:axis] + (n_heads, LANE) + w.shape[axis + 1:]
    out = lax.slice_in_dim(w.reshape(shape), 0, dim, axis=axis + 1)
    return out.reshape(w.shape[:axis] + (n_heads * dim,) + w.shape[axis + 1:])


def _pad_vec(g):
    return jnp.pad(g, ((0, 0), (0, LANE - g.shape[1])))


def _pack(w_in, w_uq, w_ukv, w_out, smalls):
    parts = [w_in.reshape(R_IN, LANE), w_uq.reshape(R_UQ, LANE), w_ukv.reshape(R_UKV, LANE),
             w_out.reshape(R_OUT, LANE), smalls.reshape(R_SMALL, LANE),
             jnp.zeros((R_PACK - R_BIG - R_SMALL, LANE), smalls.dtype)]
    return jnp.concatenate(parts, axis=0)


def _unpack(p):
    r0, r1, r2 = R_IN, R_IN + R_UQ, R_IN + R_UQ + R_UKV
    smalls = p[R_BIG:R_BIG + R_SMALL].reshape(R_SMALL * LANE)
    return (p[:r0].reshape(SH_IN), p[r0:r1].reshape(SH_UQ), p[r1:r2].reshape(SH_UKV),
            p[r2:R_BIG].reshape(SH_OUT), smalls)


def _flat_smalls(vecs, loss=None):
    flat = jnp.concatenate([v.reshape(-1) for v in vecs])
    tail = jnp.zeros((R_SMALL * LANE - LOSS_AT,), F32)
    if loss is not None:
        tail = tail.at[0].set(loss)
    return jnp.concatenate([flat, tail])


def _split_smalls(flat):
    out, at = [], 0
    for n in SMALL_SIZES:
        out.append(flat[at:at + n].reshape(1, n))
        at += n
    return out


def _ext_weights(full):
    r0, r1, r2 = R_IN, R_IN + R_UQ, R_IN + R_UQ + R_UKV
    w_in = full[:, :r0].reshape((N_CHIPS,) + SH_IN).transpose(1, 0, 2).reshape(D_MODEL, N_IN)
    w_uq = full[:, r0:r1].reshape(B_Q_RANK, B_HEADS * B_QK)
    w_ukv = full[:, r1:r2].reshape((N_CHIPS,) + SH_UKV).transpose(1, 0, 2).reshape(B_KV_RANK, B_HEADS * (B_NOPE + B_V))
    w_out = full[:, r2:R_BIG].reshape(D_MODEL, D_MODEL)
    a_w = A_HEADS * A_DIM
    kv_w = A_KV * A_DIM
    o = 0
    secs = []
    for width, heads in ((a_w, A_HEADS), (kv_w, A_KV), (kv_w, A_KV), (a_w, A_HEADS)):
        secs.append(_pad_heads(w_in[:, o:o + width], heads, A_DIM, 1))
        o += width
    a_q, a_k, a_v, a_g = secs
    b_cq = w_in[:, o:o + B_Q_RANK]
    o += B_Q_RANK
    b_ckv = w_in[:, o:o + B_KV_RANK]
    o += B_KV_RANK
    b_kr = jnp.pad(w_in[:, o:o + B_ROPE], ((0, 0), (B_NOPE, LANE - B_QK)))
    o += B_ROPE
    b_g = w_in[:, o:]
    w_in_ext = jnp.concatenate([a_q, a_k, a_v, a_g, b_g, b_cq, b_ckv, b_kr], axis=1)
    w_uq_pad = _pad_heads(w_uq, B_HEADS, B_QK, 1)
    kv3 = w_ukv.reshape(B_KV_RANK, B_HEADS, B_NOPE + B_V)
    w_ukv_ext = jnp.concatenate([_pad_heads(kv3[:, :, :B_NOPE].reshape(B_KV_RANK, B_HEADS * B_NOPE), B_HEADS, B_NOPE, 1),
                                 kv3[:, :, B_NOPE:].reshape(B_KV_RANK, B_HEADS * B_V)], axis=1)
    w_out_ext = jnp.concatenate([_pad_heads(w_out[:a_w], A_HEADS, A_DIM, 0), w_out[a_w:]], axis=0)
    return w_in_ext, w_uq_pad, w_ukv_ext, w_out_ext


def _fold_grads(d_in_ext, d_uq_pad, d_ukv_ext, d_out_ext):
    d_in = jnp.concatenate([
        _unpad_heads(d_in_ext[:, QA0:KA0], A_HEADS, A_DIM, 1),
        _unpad_heads(d_in_ext[:, KA0:VA0], A_KV, A_DIM, 1),
        _unpad_heads(d_in_ext[:, VA0:GA0], A_KV, A_DIM, 1),
        _unpad_heads(d_in_ext[:, GA0:GB0], A_HEADS, A_DIM, 1),
        d_in_ext[:, CQ0:KR0],
        d_in_ext[:, KR0 + B_NOPE:KR0 + B_QK],
        d_in_ext[:, GB0:CQ0]], axis=1)
    d_uq = _unpad_heads(d_uq_pad, B_HEADS, B_QK, 1)
    k3 = _unpad_heads(d_ukv_ext[:, :B_HEADS * LANE], B_HEADS, B_NOPE, 1).reshape(B_KV_RANK, B_HEADS, B_NOPE)
    v3 = d_ukv_ext[:, B_HEADS * LANE:].reshape(B_KV_RANK, B_HEADS, B_V)
    d_ukv = jnp.concatenate([k3, v3], axis=2).reshape(B_KV_RANK, B_HEADS * (B_NOPE + B_V))
    d_out = jnp.concatenate([_unpad_heads(d_out_ext[:A_HEADS * LANE], A_HEADS, A_DIM, 0), d_out_ext[A_HEADS * LANE:]], axis=0)
    return (d_in.reshape(D_MODEL, N_CHIPS, SH_IN[1]).transpose(1, 0, 2),
            d_uq.reshape((N_CHIPS,) + SH_UQ),
            d_ukv.reshape(B_KV_RANK, N_CHIPS, SH_UKV[1]).transpose(1, 0, 2),
            d_out.reshape((N_CHIPS,) + SH_OUT))


def kernel(x, norm_in, w_in, a_q_norm, a_k_norm, b_cq_norm, b_ckv_norm, w_uq, w_ukv, b_q_norm, b_k_norm, w_out, loss_target, m_norm_in, m_w_in, m_a_q_norm, m_a_k_norm, m_b_cq_norm, m_b_ckv_norm, m_w_uq, m_w_ukv, m_b_q_norm, m_b_k_norm, m_w_out, v_norm_in, v_w_in, v_a_q_norm, v_a_k_norm, v_b_cq_norm, v_b_ckv_norm, v_w_uq, v_w_ukv, v_b_q_norm, v_b_k_norm, v_w_out):
    s_len = x.shape[1]
    xs, ts = x[0], loss_target[0]
    tm = min(256, s_len)
    tq = tk = min(512, s_len // 2)

    small_w = (norm_in, a_q_norm, a_k_norm, b_cq_norm, b_ckv_norm, b_q_norm, b_k_norm)
    w_pack = _pack(w_in[0], w_uq[0], w_ukv[0], w_out[0], _flat_smalls(small_w))
    m_pack = _pack(m_w_in[0], m_w_uq[0], m_w_ukv[0], m_w_out[0],
                   _flat_smalls((m_norm_in, m_a_q_norm, m_a_k_norm, m_b_cq_norm, m_b_ckv_norm, m_b_q_norm, m_b_k_norm)))
    v_pack = _pack(v_w_in[0], v_w_uq[0], v_w_ukv[0], v_w_out[0],
                   _flat_smalls((v_norm_in, v_a_q_norm, v_a_k_norm, v_b_cq_norm, v_b_ckv_norm, v_b_q_norm, v_b_k_norm)))

    w_in_ext, w_uq_pad, w_ukv_ext, w_out_ext = _ext_weights(_gather_weights(w_pack))
    gains = (norm_in, _pad_vec(a_q_norm), _pad_vec(a_k_norm), b_cq_norm, b_ckv_norm, _pad_vec(b_q_norm), _pad_vec(b_k_norm))
    tabs = _rope_tables(s_len)

    (xn, gates, pre, qbpre, kbpre, cq, ckv, qa, ka, va, qb, kb, vb) = _pre(
        xs, tabs, w_in_ext, w_uq_pad, w_ukv_ext, gains, tm)
    o_a, lse_a = _attn_fwd(qa, ka, va, A_GROUP, A_DIM, tq, tk, "attn_fwd_a")
    o_b, lse_b = _attn_fwd(qb, kb, vb, 1, B_V, tq, tk, "attn_fwd_b")
    y, dh, dgate, do_a, do_b, delta, loss_part = _mid(xs, ts, o_a, o_b, gates, w_out_ext, tm)

    def stat(a):
        return a.reshape(a.shape[0], s_len // tq, 1, tq)

    dqa, dka, dva = _attn_bwd(qa, ka, va, do_a, stat(lse_a), stat(delta[:A_HEADS]), A_GROUP, tq, tk, "attn_bwd_a")
    dqb, dkb, dvb = _attn_bwd(qb, kb, vb, do_b, stat(lse_b), stat(delta[A_HEADS:]), 1, tq, tk, "attn_bwd_b")
    (grad_x, dproj, dqbpre, dkvb, *d_gains) = _post(
        xs, dh, pre, qbpre, kbpre, dgate, dqa, dka, dva, dqb, dkb, dvb, tabs, w_in_ext, w_uq_pad, w_ukv_ext, gains, tm)

    ts_w = min(512, s_len)
    d_in_ext = _grad_w(xn, dproj, 768, ts_w, "grad_w_in")
    d_out_ext = _grad_w(y, dh, 512, ts_w, "grad_w_out")
    d_uq_pad = _grad_w(cq, dqbpre, 512, ts_w, "grad_w_uq")
    d_ukv_ext = _grad_w(ckv, dkvb, 1024, ts_w, "grad_w_ukv")

    g_in, g_uq, g_ukv, g_out = _fold_grads(d_in_ext, d_uq_pad, d_ukv_ext, d_out_ext)
    d_small = _flat_smalls(
        [d_gains[0], d_gains[1][:, :A_DIM], d_gains[2][:, :A_DIM], d_gains[3], d_gains[4],
         d_gains[5][:, :B_QK], d_gains[6][:, :B_QK]], loss=loss_part[0, 0])
    parts = jnp.stack([_pack(g_in[j], g_uq[j], g_ukv[j], g_out[j], d_small) for j in range(N_CHIPS)])
    g_pack = _reduce_grads(parts)
    d_pack, nm_pack, nv_pack = _adamw(w_pack, g_pack, m_pack, v_pack, 1536)

    def leaves(p):
        p_in, p_uq, p_ukv, p_out, flat = _unpack(p)
        s = _split_smalls(flat)
        return [s[0], p_in[None], s[1], s[2], s[3], s[4], p_uq[None], p_ukv[None], s[5], s[6], p_out[None]], flat

    grads, g_flat = leaves(g_pack)
    loss = g_flat[LOSS_AT]
    return (loss, grad_x[None], *grads, *leaves(d_pack)[0], *leaves(nm_pack)[0], *leaves(nv_pack)[0])
```

```python
import functools

import jax
import jax.numpy as jnp
import numpy as np
from jax import lax
from jax.experimental import pallas as pl
from jax.experimental.pallas import tpu as pltpu

F32 = jnp.float32
BF16 = jnp.bfloat16
MESH = pl.DeviceIdType.MESH

D_MODEL = 1024
GRID_W = 64
ROPE_THETA = 10000.0
EPS = 1e-6
A_HEADS, A_KV, A_DIM = 8, 2, 64
A_GROUP = A_HEADS // A_KV
B_HEADS, B_NOPE, B_ROPE, B_V = 4, 64, 32, 128
B_QK = B_NOPE + B_ROPE
B_Q_RANK, B_KV_RANK = 384, 256
N_IN = 2464
SCALE_A = 1.0 / float(np.sqrt(A_DIM))
SCALE_B = 1.0 / float(np.sqrt(B_QK))
LOG2E = float(np.log2(np.e))
LN2 = float(np.log(2.0))
ADAM_LR, ADAM_B1, ADAM_B2, ADAM_EPS, ADAM_WD, ADAM_STEP = 0.001, 0.9, 0.999, 1e-08, 0.01, 10

LANE = 128
VMEM_BYTES = 64 * 1024 * 1024
VMEM_LIMIT = VMEM_BYTES - 8 * 1024 * 1024

QA0 = 0
KA0 = QA0 + A_HEADS * LANE
VA0 = KA0 + A_KV * LANE
GA0 = VA0 + A_KV * LANE
GB0 = GA0 + A_HEADS * LANE
CQ0 = GB0 + B_HEADS * LANE
CKV0 = CQ0 + B_Q_RANK
KR0 = CKV0 + B_KV_RANK
N_EXT = KR0 + LANE
N_GATE = (A_HEADS + B_HEADS) * LANE
DELTA_ROWS = 16
N_PRE = KA0 + A_KV * LANE + B_Q_RANK + B_KV_RANK

N_CHIPS = 4
SH_IN = (D_MODEL, N_IN // N_CHIPS)
SH_UQ = (B_Q_RANK // N_CHIPS, B_HEADS * B_QK)
SH_UKV = (B_KV_RANK, B_HEADS * (B_NOPE + B_V) // N_CHIPS)
SH_OUT = (D_MODEL // N_CHIPS, D_MODEL)
R_IN = SH_IN[0] * SH_IN[1] // LANE
R_UQ = SH_UQ[0] * SH_UQ[1] // LANE
R_UKV = SH_UKV[0] * SH_UKV[1] // LANE
R_OUT = SH_OUT[0] * SH_OUT[1] // LANE
R_BIG = R_IN + R_UQ + R_UKV + R_OUT
R_SMALL = 16
R_PACK = 7680
R_HALF = R_PACK // 2
SMALL_SIZES = (D_MODEL, A_DIM, A_DIM, B_Q_RANK, B_KV_RANK, B_QK, B_QK)
LOSS_AT = sum(SMALL_SIZES)


def _pallas(body, **kw):
    return pl.pallas_call(body, **kw)


def _params(sem=None, vmem=None):
    return pltpu.CompilerParams(dimension_semantics=sem, vmem_limit_bytes=vmem)


def _rms_fwd(x, g, n):
    r = lax.rsqrt(jnp.sum(x * x, axis=-1, keepdims=True) * (1.0 / n) + EPS)
    return x * r * g


def _rms_bwd(dy, x, g, n):
    r = lax.rsqrt(jnp.sum(x * x, axis=-1, keepdims=True) * (1.0 / n) + EPS)
    xhat = x * r
    u = dy * g
    dx = r * (u - xhat * (jnp.sum(u * xhat, axis=-1, keepdims=True) * (1.0 / n)))
    return dx, dy * xhat


def _partner(y, half):
    lane = lax.broadcasted_iota(jnp.int32, y.shape, 1)
    first = (lane % (2 * half)) < half
    return jnp.where(first, pltpu.roll(y, LANE - half, 1), pltpu.roll(y, half, 1))


def _rope_fwd(y, cos, sin, half):
    return y * cos + _partner(y, half) * sin


def _rope_bwd(d, cos, sin, half):
    return d * cos - _partner(d, half) * sin


def _nt(a, b):
    return lax.dot_general(a, b, (((1,), (1,)), ((), ())), preferred_element_type=F32)


def _tn(a, b):
    return lax.dot_general(a, b, (((0,), (0,)), ((), ())), preferred_element_type=F32)


def _nn(a, b):
    return jnp.dot(a, b, preferred_element_type=F32)


def _block_rows(i, size):
    if isinstance(i, int):
        return pl.ds(i * size, size)
    return pl.ds(pl.multiple_of(i * size, size), size)


def _full(shape):
    return pl.BlockSpec(shape, lambda *_: (0,) * len(shape))


def _gather_weights(w_pack):
    def body(w_ref, out_ref, send_sems, recv_sems):
        x, y, c = lax.axis_index("x"), lax.axis_index("y"), lax.axis_index("c")
        sibling = (x, y, 1 - c)
        chips = [(1 - x, y), (x, 1 - y), (1 - x, 1 - y)]
        me = 2 * x + y

        def half(j, hc):
            return out_ref.at[j, pl.ds(pl.multiple_of(hc * R_HALF, R_HALF), R_HALF), :]

        def copy(k, j, hc, to):
            return pltpu.make_async_remote_copy(
                src_ref=half(j, hc), dst_ref=half(j, hc), send_sem=send_sems.at[k], recv_sem=recv_sems.at[k],
                device_id=to, device_id_type=MESH)

        out_ref[me] = w_ref[...].astype(BF16)
        first = [copy(k, me, c, (*chip, c)) for k, chip in enumerate(chips)]
        for cp in first:
            cp.start()
        passed = [copy(3 + k, 2 * chip[0] + chip[1], c, sibling) for k, chip in enumerate(chips)]
        for k, chip in enumerate(chips):
            copy(k, 2 * chip[0] + chip[1], c, (*chip, c)).wait_recv()
            passed[k].start()
        for k, chip in enumerate(chips):
            copy(3 + k, 2 * chip[0] + chip[1], 1 - c, sibling).wait_recv()
        for cp in first + passed:
            cp.wait_send()

    return _pallas(
        body, name="gather_weights",
        out_shape=jax.ShapeDtypeStruct((N_CHIPS, R_PACK, LANE), BF16),
        in_specs=[pl.BlockSpec(memory_space=pltpu.VMEM)],
        out_specs=pl.BlockSpec(memory_space=pltpu.VMEM),
        scratch_shapes=[pltpu.SemaphoreType.DMA((6,)), pltpu.SemaphoreType.DMA((6,))],
        compiler_params=_params(vmem=32 * 1024 * 1024),
    )(w_pack)


def _reduce_grads(parts):
    def body(p_ref, out_ref, rec_a, rec_b, sa_send, sa_recv, sb_send, sb_recv, sc_send, sc_recv):
        x, y, c = lax.axis_index("x"), lax.axis_index("y"), lax.axis_index("c")
        sibling = (x, y, 1 - c)
        me = 2 * x + y
        mine = pl.ds(pl.multiple_of(c * R_HALF, R_HALF), R_HALF)
        other = pl.ds(pl.multiple_of((1 - c) * R_HALF, R_HALF), R_HALF)

        def copy_a(j):
            return pltpu.make_async_remote_copy(
                src_ref=p_ref.at[j, other, :], dst_ref=rec_a.at[j], send_sem=sa_send.at[j], recv_sem=sa_recv.at[j],
                device_id=sibling, device_id_type=MESH)

        def copy_b(r):
            j = me ^ r
            return pltpu.make_async_remote_copy(
                src_ref=rec_a.at[j], dst_ref=rec_b.at[r], send_sem=sb_send.at[r - 1], recv_sem=sb_recv.at[r - 1],
                device_id=(j // 2, j % 2, c), device_id_type=MESH)

        copy_c = pltpu.make_async_remote_copy(
            src_ref=out_ref.at[mine, :], dst_ref=out_ref.at[mine, :], send_sem=sc_send, recv_sem=sc_recv,
            device_id=sibling, device_id_type=MESH)

        for j in range(N_CHIPS):
            copy_a(j).start()
        for r in range(1, N_CHIPS):
            j = me ^ r
            copy_a(j).wait_recv()
            rec_a[j] = rec_a[j] + p_ref[j, mine, :]
            copy_b(r).start()
        copy_a(me).wait_recv()
        rec_b[0] = rec_a[me] + p_ref[me, mine, :]
        for r in range(1, N_CHIPS):
            copy_b(r).wait_recv()
        total = rec_b[me]
        for j in range(1, N_CHIPS):
            total = total + rec_b[j ^ me]
        out_ref[mine, :] = total
        copy_c.start()
        copy_c.wait_recv()
        for j in range(N_CHIPS):
            copy_a(j).wait_send()
        for r in range(1, N_CHIPS):
            copy_b(r).wait_send()
        copy_c.wait_send()

    return _pallas(
        body, name="reduce_grads",
        out_shape=jax.ShapeDtypeStruct((R_PACK, LANE), F32),
        in_specs=[pl.BlockSpec(memory_space=pltpu.VMEM)],
        out_specs=pl.BlockSpec(memory_space=pltpu.VMEM),
        scratch_shapes=[
            pltpu.VMEM((N_CHIPS, R_HALF, LANE), F32),
            pltpu.VMEM((N_CHIPS, R_HALF, LANE), F32),
            pltpu.SemaphoreType.DMA((N_CHIPS,)), pltpu.SemaphoreType.DMA((N_CHIPS,)),
            pltpu.SemaphoreType.DMA((N_CHIPS - 1,)), pltpu.SemaphoreType.DMA((N_CHIPS - 1,)),
            pltpu.SemaphoreType.DMA, pltpu.SemaphoreType.DMA,
        ],
        compiler_params=_params(vmem=VMEM_LIMIT),
    )(parts)


def _pre(x, tabs, w_in_ext, w_uq_pad, w_ukv_ext, gains, tm):
    s_len = x.shape[0]
    nt = s_len // tm

    def body(x_ref, ca_ref, sa_ref, cb_ref, sb_ref, win_ref, wuq_ref, wukv_ref,
             gin_ref, gaq_ref, gak_ref, gcq_ref, gckv_ref, gbq_ref, gbk_ref,
             xn_ref, gates_ref, pre_ref, qbpre_ref, kbpre_ref, cq_ref, ckv_ref,
             qa_ref, ka_ref, va_ref, qb_ref, kb_ref, vb_ref, proj):
        xn = _rms_fwd(x_ref[...], gin_ref[...], D_MODEL).astype(BF16)
        xn_ref[...] = xn
        proj[...] = _nn(xn, win_ref[...])
        gates_ref[...] = proj[:, GA0:GA0 + N_GATE]
        pre_ref[:, 0:VA0] = proj[:, 0:VA0]
        pre_ref[:, VA0:N_PRE] = proj[:, CQ0:KR0]
        ca, sa, cb, sb = ca_ref[...], sa_ref[...], cb_ref[...], sb_ref[...]
        lane = lax.broadcasted_iota(jnp.int32, (tm, LANE), 1)
        for h in range(A_HEADS):
            yq = _rms_fwd(proj[:, QA0 + LANE * h:QA0 + LANE * (h + 1)], gaq_ref[...], A_DIM)
            qa_ref[h] = (_rope_fwd(yq, ca, sa, A_DIM // 4) * (SCALE_A * LOG2E)).astype(BF16)
        for h in range(A_KV):
            yk = _rms_fwd(proj[:, KA0 + LANE * h:KA0 + LANE * (h + 1)], gak_ref[...], A_DIM)
            ka_ref[h] = _rope_fwd(yk, ca, sa, A_DIM // 4).astype(BF16)
            va_ref[h] = jnp.where(lane == A_DIM, 1.0, proj[:, VA0 + LANE * h:VA0 + LANE * (h + 1)]).astype(BF16)
        cq = _rms_fwd(proj[:, CQ0:CQ0 + B_Q_RANK], gcq_ref[...], B_Q_RANK).astype(BF16)
        cq_ref[...] = cq
        qbpre_ref[...] = _nn(cq, wuq_ref[...])
        ckv = _rms_fwd(proj[:, CKV0:CKV0 + B_KV_RANK], gckv_ref[...], B_KV_RANK).astype(BF16)
        ckv_ref[...] = ckv
        kvb = _nn(ckv, wukv_ref[...])
        kr = proj[:, KR0:KR0 + LANE]
        for h in range(B_HEADS):
            yq = _rms_fwd(qbpre_ref[:, LANE * h:LANE * (h + 1)], gbq_ref[...], B_QK)
            qb_ref[h] = (_rope_fwd(yq, cb, sb, B_ROPE // 4) * (SCALE_B * LOG2E)).astype(BF16)
            kp = kvb[:, LANE * h:LANE * (h + 1)] + kr
            kbpre_ref[:, LANE * h:LANE * (h + 1)] = kp
            kb_ref[h] = _rope_fwd(_rms_fwd(kp, gbk_ref[...], B_QK), cb, sb, B_ROPE // 4).astype(BF16)
            vb_ref[h, :, 0:LANE] = kvb[:, B_HEADS * LANE + LANE * h:B_HEADS * LANE + LANE * (h + 1)].astype(BF16)
            vb_ref[h, :, LANE:2 * LANE] = jnp.where(lane == 0, 1.0, 0.0).astype(BF16)

    row = lambda w: pl.BlockSpec((tm, w), lambda i: (i, 0))
    heads = lambda n: pl.BlockSpec((n, tm, LANE), lambda i: (0, i, 0))
    hs = lambda n: jax.ShapeDtypeStruct((n, s_len, LANE), BF16)
    return _pallas(
        body, name="pre", grid=(nt,),
        in_specs=[row(D_MODEL), row(LANE), row(LANE), row(LANE), row(LANE),
                  _full(w_in_ext.shape), _full(w_uq_pad.shape), _full(w_ukv_ext.shape)]
                 + [_full(g.shape) for g in gains],
        out_specs=[row(D_MODEL), row(N_GATE), row(N_PRE), row(B_HEADS * LANE), row(B_HEADS * LANE),
                   row(B_Q_RANK), row(B_KV_RANK),
                   heads(A_HEADS), heads(A_KV), heads(A_KV), heads(B_HEADS), heads(B_HEADS),
                   pl.BlockSpec((B_HEADS, tm, 2 * LANE), lambda i: (0, i, 0))],
        out_shape=[jax.ShapeDtypeStruct((s_len, D_MODEL), BF16), jax.ShapeDtypeStruct((s_len, N_GATE), F32),
                   jax.ShapeDtypeStruct((s_len, N_PRE), F32), jax.ShapeDtypeStruct((s_len, B_HEADS * LANE), F32),
                   jax.ShapeDtypeStruct((s_len, B_HEADS * LANE), F32),
                   jax.ShapeDtypeStruct((s_len, B_Q_RANK), BF16), jax.ShapeDtypeStruct((s_len, B_KV_RANK), BF16),
                   hs(A_HEADS), hs(A_KV), hs(A_KV), hs(B_HEADS), hs(B_HEADS),
                   jax.ShapeDtypeStruct((B_HEADS, s_len, 2 * LANE), BF16)],
        scratch_shapes=[pltpu.VMEM((tm, N_EXT), F32)],
        compiler_params=_params(("parallel",), VMEM_LIMIT),
    )(x, *tabs, w_in_ext, w_uq_pad, w_ukv_ext, *gains)


def _attn_fwd(q, k, v, group, l_col, tq, tk, name):
    n_heads, s_len, _ = q.shape
    v_w = v.shape[2]
    nk = s_len // tk
    assert nk >= 2 and nk % 2 == 0

    def body(q_ref, k_ref, v_ref, o_ref, lse_ref, s_buf, p_buf, a_buf, m_ref, acc_ref):
        def rows(j):
            return _block_rows(j, tk)

        def scores(j, slot):
            s_buf[slot] = _nt(q_ref[...], k_ref[rows(j), :])

        def softmax(slot):
            s = s_buf[slot]
            m_old = m_ref[...]
            m_new = jnp.maximum(m_old, jnp.max(s, axis=-1, keepdims=True))
            m_ref[...] = m_new
            a_buf[slot] = jnp.exp2(m_old - m_new)
            p_buf[slot] = jnp.exp2(s - jnp.tile(m_new, (1, tk // LANE))).astype(BF16)

        def values(j, slot):
            pv = _nn(p_buf[slot], v_ref[rows(j), :])
            for c in range(0, v_w, LANE):
                acc_ref[:, c:c + LANE] = a_buf[slot] * acc_ref[:, c:c + LANE] + pv[:, c:c + LANE]

        m_ref[...] = jnp.full(m_ref.shape, -1e30, F32)
        acc_ref[...] = jnp.zeros(acc_ref.shape, F32)
        scores(0, 0)
        scores(1, 1)
        softmax(0)

        def pair(t, carry):
            j = 2 * t + 1
            scores(j + 1, 0)
            softmax(1)
            values(j - 1, 0)
            scores(j + 2, 1)
            softmax(0)
            values(j, 1)
            return carry

        lax.fori_loop(0, (nk - 2) // 2, pair, 0)
        softmax(1)
        values(nk - 2, 0)
        values(nk - 1, 1)
        l = acc_ref[:, l_col:l_col + 1]
        o = acc_ref[:, 0:LANE] * (1.0 / l)
        if l_col < LANE:
            lane = lax.broadcasted_iota(jnp.int32, o.shape, 1)
            o = jnp.where(lane == l_col, 0.0, o)
        o_ref[...] = o
        lse_ref[...] = jnp.transpose(m_ref[...] + jnp.log2(jnp.broadcast_to(l, (tq, LANE))))[0:1, :]

    return _pallas(
        body, name=name, grid=(n_heads, s_len // tq),
        in_specs=[pl.BlockSpec((None, tq, LANE), lambda h, i: (h, i, 0)),
                  pl.BlockSpec((None, s_len, LANE), lambda h, i: (h // group, 0, 0)),
                  pl.BlockSpec((None, s_len, v_w), lambda h, i: (h // group, 0, 0))],
        out_specs=[pl.BlockSpec((None, tq, LANE), lambda h, i: (h, i, 0)),
                   pl.BlockSpec((None, None, 1, tq), lambda h, i: (h, i, 0, 0))],
        out_shape=[jax.ShapeDtypeStruct((n_heads, s_len, LANE), F32),
                   jax.ShapeDtypeStruct((n_heads, s_len // tq, 1, tq), F32)],
        scratch_shapes=[pltpu.VMEM((2, tq, tk), F32), pltpu.VMEM((2, tq, tk), BF16), pltpu.VMEM((2, tq, LANE), F32),
                        pltpu.VMEM((tq, LANE), F32), pltpu.VMEM((tq, v_w), F32)],
        compiler_params=_params(("parallel", "parallel"), 48 * 1024 * 1024),
    )(q, k, v)


def _mid(x, target, o_a, o_b, gates, w_out_ext, tm):
    s_len = x.shape[0]
    nt = s_len // tm
    n_heads = A_HEADS + B_HEADS

    def body(x_ref, t_ref, oa_ref, ob_ref, g_ref, w_ref,
             y_ref, dh_ref, dgate_ref, doa_ref, dob_ref, delta_ref, loss_ref, silu_scr, dsilu_scr):
        @pl.when(pl.program_id(0) == 0)
        def _():
            loss_ref[...] = jnp.zeros_like(loss_ref)

        def o_of(h):
            return oa_ref[h] if h < A_HEADS else ob_ref[h - A_HEADS]

        for h in range(n_heads):
            cols = slice(LANE * h, LANE * (h + 1))
            g = g_ref[:, cols]
            sig = 1.0 / (1.0 + jnp.exp(-g))
            silu = g * sig
            silu_scr[:, cols] = silu
            dsilu_scr[:, cols] = sig * (1.0 + g * (1.0 - sig))
            y_ref[:, cols] = (o_of(h) * silu).astype(BF16)
        err = x_ref[...] + _nn(y_ref[...], w_ref[...]) - t_ref[...]
        sq = jnp.sum(jnp.sum(err * err, axis=-1, keepdims=True), axis=0, keepdims=True)
        loss_ref[...] += jnp.broadcast_to(sq * (0.5 / D_MODEL), loss_ref.shape)
        dh = err * (1.0 / D_MODEL)
        dh_ref[...] = dh
        dy = _nt(dh.astype(BF16), w_ref[...])
        lane = lax.broadcasted_iota(jnp.int32, (tm, LANE), 1)
        delta = jnp.zeros((tm, LANE), F32)
        for h in range(n_heads):
            cols = slice(LANE * h, LANE * (h + 1))
            dyh = dy[:, cols]
            oh = o_of(h)
            do = dyh * silu_scr[:, cols]
            dgate_ref[:, cols] = (dyh * oh * dsilu_scr[:, cols]).astype(BF16)
            delta = jnp.where(lane == h, jnp.sum(do * oh, axis=-1, keepdims=True), delta)
            if h < A_HEADS:
                doa_ref[h] = do.astype(BF16)
            else:
                dob_ref[h - A_HEADS] = do.astype(BF16)
        delta_ref[...] = jnp.transpose(delta)[0:DELTA_ROWS, :]

    row = lambda w: pl.BlockSpec((tm, w), lambda i: (i, 0))
    heads = lambda n, w=LANE: pl.BlockSpec((n, tm, w), lambda i: (0, i, 0))
    return _pallas(
        body, name="mid", grid=(nt,),
        in_specs=[row(D_MODEL), row(D_MODEL), heads(A_HEADS), heads(B_HEADS), row(N_GATE), _full(w_out_ext.shape)],
        out_specs=[row(N_GATE), row(D_MODEL), row(N_GATE), heads(A_HEADS), heads(B_HEADS),
                   pl.BlockSpec((DELTA_ROWS, tm), lambda i: (0, i)),
                   _full((8, LANE))],
        out_shape=[jax.ShapeDtypeStruct((s_len, N_GATE), BF16), jax.ShapeDtypeStruct((s_len, D_MODEL), F32),
                   jax.ShapeDtypeStruct((s_len, N_GATE), BF16),
                   jax.ShapeDtypeStruct((A_HEADS, s_len, LANE), BF16), jax.ShapeDtypeStruct((B_HEADS, s_len, LANE), BF16),
                   jax.ShapeDtypeStruct((DELTA_ROWS, s_len), F32), jax.ShapeDtypeStruct((8, LANE), F32)],
        scratch_shapes=[pltpu.VMEM((tm, N_GATE), F32), pltpu.VMEM((tm, N_GATE), F32)],
        compiler_params=_params(("arbitrary",), VMEM_LIMIT),
    )(x, target, o_a, o_b, gates, w_out_ext)


def _attn_bwd(q, k, v, do, lse, delta, group, tq, tk, name):
    n_heads, s_len, _ = q.shape
    nq = s_len // tq
    assert nq >= 2 and nq % 2 == 0

    def body(q_ref, do_ref, lse_ref, delta_ref, k_ref, v_ref, dq_ref, dk_ref, dv_ref, s_buf, dp_buf, p_buf, ds_buf):
        @pl.when(pl.program_id(1) == 0)
        def _():
            dq_ref[...] = jnp.zeros_like(dq_ref)

        dk_ref[...] = jnp.zeros_like(dk_ref)
        dv_ref[...] = jnp.zeros_like(dv_ref)

        def rows(i):
            return _block_rows(i, tq)

        def scores(i, slot):
            s_buf[slot] = _nt(k_ref[...], q_ref[rows(i), :])
            dp_buf[slot] = _nt(v_ref[...], do_ref[rows(i), :])

        def elementwise(i, slot):
            p = jnp.exp2(s_buf[slot] - lse_ref[i])
            p_buf[slot] = p.astype(BF16)
            ds_buf[slot] = (p * (dp_buf[slot] - delta_ref[i])).astype(BF16)

        def grads(i, slot):
            dv_ref[...] += _nn(p_buf[slot], do_ref[rows(i), :])
            dk_ref[...] += _nn(ds_buf[slot], q_ref[rows(i), :])
            dq_ref[rows(i), :] += _tn(ds_buf[slot], k_ref[...])

        scores(0, 0)
        scores(1, 1)
        elementwise(0, 0)

        def pair(t, carry):
            i = 2 * t + 1
            scores(i + 1, 0)
            elementwise(i, 1)
            grads(i - 1, 0)
            scores(i + 2, 1)
            elementwise(i + 1, 0)
            grads(i, 1)
            return carry

        lax.fori_loop(0, (nq - 2) // 2, pair, 0)
        elementwise(nq - 1, 1)
        grads(nq - 2, 0)
        grads(nq - 1, 1)

    whole = lambda: pl.BlockSpec((None, s_len, LANE), lambda h, j: (h, 0, 0))
    stat = lambda: pl.BlockSpec((None, nq, 1, tq), lambda h, j: (h, 0, 0, 0))
    kvb = lambda: pl.BlockSpec((None, tk, LANE), lambda h, j: (h // group, j, 0))
    outb = lambda: pl.BlockSpec((None, tk, LANE), lambda h, j: (h, j, 0))
    shape = jax.ShapeDtypeStruct((n_heads, s_len, LANE), F32)
    return _pallas(
        body, name=name, grid=(n_heads, s_len // tk),
        in_specs=[whole(), whole(), stat(), stat(), kvb(), kvb()],
        out_specs=[whole(), outb(), outb()],
        out_shape=[shape, shape, shape],
        scratch_shapes=[pltpu.VMEM((2, tk, tq), F32), pltpu.VMEM((2, tk, tq), F32),
                        pltpu.VMEM((2, tk, tq), BF16), pltpu.VMEM((2, tk, tq), BF16)],
        compiler_params=_params(("parallel", "arbitrary"), 48 * 1024 * 1024),
    )(q, do, lse, delta, k, v)


def _post(x, dh, pre, qbpre, kbpre, dgate, dqa, dka, dva, dqb, dkb, dvb, tabs,
          w_in_ext, w_uq_pad, w_ukv_ext, gains, tm):
    s_len = x.shape[0]
    nt = s_len // tm

    def body(x_ref, dh_ref, pre_ref, qbpre_ref, kbpre_ref, dgate_ref,
             dqa_ref, dka_ref, dva_ref, dqb_ref, dkb_ref, dvb_ref,
             ca_ref, sa_ref, cb_ref, sb_ref, win_ref, wuq_ref, wukv_ref,
             gin_ref, gaq_ref, gak_ref, gcq_ref, gckv_ref, gbq_ref, gbk_ref,
             gx_ref, dproj_ref, dqbpre_ref, dkvb_ref,
             dgin_ref, dgaq_ref, dgak_ref, dgcq_ref, dgckv_ref, dgbq_ref, dgbk_ref):
        @pl.when(pl.program_id(0) == 0)
        def _():
            for r in (dgin_ref, dgaq_ref, dgak_ref, dgcq_ref, dgckv_ref, dgbq_ref, dgbk_ref):
                r[...] = jnp.zeros_like(r)

        def tok_sum(a):
            return jnp.sum(a, axis=0, keepdims=True)

        ca, sa, cb, sb = ca_ref[...], sa_ref[...], cb_ref[...], sb_ref[...]
        lane = lax.broadcasted_iota(jnp.int32, (tm, LANE), 1)
        dg = jnp.zeros((1, LANE), F32)
        for h in range(A_HEADS):
            dn = _rope_bwd(dqa_ref[h] * SCALE_A, ca, sa, A_DIM // 4)
            dx, dgr = _rms_bwd(dn, pre_ref[:, QA0 + LANE * h:QA0 + LANE * (h + 1)], gaq_ref[...], A_DIM)
            dproj_ref[:, QA0 + LANE * h:QA0 + LANE * (h + 1)] = dx.astype(BF16)
            dg = dg + tok_sum(dgr)
        dgaq_ref[...] += dg
        dg = jnp.zeros((1, LANE), F32)
        for h in range(A_KV):
            dk = dka_ref[A_GROUP * h]
            dv = dva_ref[A_GROUP * h]
            for g in range(1, A_GROUP):
                dk = dk + dka_ref[A_GROUP * h + g]
                dv = dv + dva_ref[A_GROUP * h + g]
            dn = _rope_bwd(dk * LN2, ca, sa, A_DIM // 4)
            dx, dgr = _rms_bwd(dn, pre_ref[:, KA0 + LANE * h:KA0 + LANE * (h + 1)], gak_ref[...], A_DIM)
            dproj_ref[:, KA0 + LANE * h:KA0 + LANE * (h + 1)] = dx.astype(BF16)
            dproj_ref[:, VA0 + LANE * h:VA0 + LANE * (h + 1)] = dv.astype(BF16)
            dg = dg + tok_sum(dgr)
        dgak_ref[...] += dg
        dproj_ref[:, GA0:GA0 + N_GATE] = dgate_ref[...]
        dg = jnp.zeros((1, LANE), F32)
        for h in range(B_HEADS):
            cols = slice(LANE * h, LANE * (h + 1))
            dn = _rope_bwd(dqb_ref[h] * SCALE_B, cb, sb, B_ROPE // 4)
            dx, dgr = _rms_bwd(dn, qbpre_ref[:, cols], gbq_ref[...], B_QK)
            dqbpre_ref[:, cols] = dx.astype(BF16)
            dg = dg + tok_sum(dgr)
        dgbq_ref[...] += dg
        dcq = _nt(dqbpre_ref[...], wuq_ref[...])
        dx, dgr = _rms_bwd(dcq, pre_ref[:, VA0:VA0 + B_Q_RANK], gcq_ref[...], B_Q_RANK)
        dproj_ref[:, CQ0:CQ0 + B_Q_RANK] = dx.astype(BF16)
        dgcq_ref[...] += tok_sum(dgr)
        dg = jnp.zeros((1, LANE), F32)
        dkr = jnp.zeros((tm, LANE), F32)
        for h in range(B_HEADS):
            cols = slice(LANE * h, LANE * (h + 1))
            dn = _rope_bwd(dkb_ref[h] * LN2, cb, sb, B_ROPE // 4)
            dx, dgr = _rms_bwd(dn, kbpre_ref[:, cols], gbk_ref[...], B_QK)
            dkvb_ref[:, cols] = jnp.where(lane < B_NOPE, dx, 0.0).astype(BF16)
            dkvb_ref[:, B_HEADS * LANE + LANE * h:B_HEADS * LANE + LANE * (h + 1)] = dvb_ref[h].astype(BF16)
            dkr = dkr + dx
            dg = dg + tok_sum(dgr)
        dgbk_ref[...] += dg
        dproj_ref[:, KR0:KR0 + LANE] = jnp.where((lane >= B_NOPE) & (lane < B_QK), dkr, 0.0).astype(BF16)
        dckv = _nt(dkvb_ref[...], wukv_ref[...])
        dx, dgr = _rms_bwd(dckv, pre_ref[:, VA0 + B_Q_RANK:N_PRE], gckv_ref[...], B_KV_RANK)
        dproj_ref[:, CKV0:CKV0 + B_KV_RANK] = dx.astype(BF16)
        dgckv_ref[...] += tok_sum(dgr)
        dxn = _nt(dproj_ref[...], win_ref[...])
        dx, dgr = _rms_bwd(dxn, x_ref[...], gin_ref[...], D_MODEL)
        gx_ref[...] = dh_ref[...] + dx
        dgin_ref[...] += tok_sum(dgr)

    row = lambda w: pl.BlockSpec((tm, w), lambda i: (i, 0))
    heads = lambda n: pl.BlockSpec((n, tm, LANE), lambda i: (0, i, 0))
    return _pallas(
        body, name="post", grid=(nt,),
        in_specs=[row(D_MODEL), row(D_MODEL), row(N_PRE), row(B_HEADS * LANE), row(B_HEADS * LANE), row(N_GATE),
                  heads(A_HEADS), heads(A_HEADS), heads(A_HEADS), heads(B_HEADS), heads(B_HEADS), heads(B_HEADS),
                  row(LANE), row(LANE), row(LANE), row(LANE),
                  _full(w_in_ext.shape), _full(w_uq_pad.shape), _full(w_ukv_ext.shape)]
                 + [_full(g.shape) for g in gains],
        out_specs=[row(D_MODEL), row(N_EXT), row(B_HEADS * LANE), row(2 * B_HEADS * LANE)]
                  + [_full(g.shape) for g in gains],
        out_shape=[jax.ShapeDtypeStruct((s_len, D_MODEL), F32), jax.ShapeDtypeStruct((s_len, N_EXT), BF16),
                   jax.ShapeDtypeStruct((s_len, B_HEADS * LANE), BF16),
                   jax.ShapeDtypeStruct((s_len, 2 * B_HEADS * LANE), BF16)]
                  + [jax.ShapeDtypeStruct(g.shape, F32) for g in gains],
        compiler_params=_params(("arbitrary",), VMEM_LIMIT),
    )(x, dh, pre, qbpre, kbpre, dgate, dqa, dka, dva, dqb, dkb, dvb, *tabs,
      w_in_ext, w_uq_pad, w_ukv_ext, *gains)


def _grad_w(a, b, tn, ts, name):
    s_len, m = a.shape
    n = b.shape[1]

    def body(a_ref, b_ref, o_ref):
        @pl.when(pl.program_id(1) == 0)
        def _():
            o_ref[...] = jnp.zeros_like(o_ref)

        o_ref[...] += _tn(a_ref[...].astype(BF16), b_ref[...].astype(BF16))

    return _pallas(
        body, name=name, grid=(n // tn, s_len // ts),
        in_specs=[pl.BlockSpec((ts, m), lambda j, t: (t, 0)), pl.BlockSpec((ts, tn), lambda j, t: (t, j))],
        out_specs=pl.BlockSpec((m, tn), lambda j, t: (0, j)),
        out_shape=jax.ShapeDtypeStruct((m, n), F32),
        compiler_params=_params(("parallel", "arbitrary"), 48 * 1024 * 1024),
    )(a, b)


def _adamw(w, g, m, v, tr):
    rows = w.shape[0]

    def body(w_ref, g_ref, m_ref, v_ref, d_ref, nm_ref, nv_ref):
        gv = g_ref[...]
        nm = ADAM_B1 * m_ref[...] + (1.0 - ADAM_B1) * gv
        nv = ADAM_B2 * v_ref[...] + (1.0 - ADAM_B2) * (gv * gv)
        m_hat = nm / (1.0 - ADAM_B1 ** ADAM_STEP)
        v_hat = nv / (1.0 - ADAM_B2 ** ADAM_STEP)
        d_ref[...] = -ADAM_LR * (m_hat / (jnp.sqrt(v_hat) + ADAM_EPS) + ADAM_WD * w_ref[...])
        nm_ref[...] = nm
        nv_ref[...] = nv

    blk = pl.BlockSpec((tr, LANE), lambda i: (i, 0))
    shape = jax.ShapeDtypeStruct((rows, LANE), F32)
    return _pallas(
        body, name="adamw", grid=(rows // tr,),
        in_specs=[blk, blk, blk, blk], out_specs=[blk, blk, blk], out_shape=[shape, shape, shape],
        compiler_params=_params(("parallel",), 32 * 1024 * 1024),
    )(w, g, m, v)


def _rope_tables(s_len):
    rows = s_len // GRID_W
    row = jnp.arange(rows, dtype=F32)
    col = jnp.arange(GRID_W, dtype=F32)

    def lay(dim, lead):
        half = dim // 2
        inv = 1.0 / (ROPE_THETA ** (jnp.arange(0, half, 2, dtype=F32) / half))
        ang_r, ang_c = row[:, None] * inv[None, :], col[:, None] * inv[None, :]
        quarter = half // 2
        tail = LANE - lead - dim

        def place(a, b, at, n, fill=0.0):
            return jnp.concatenate([jnp.full((n, lead), fill if at else 0.0, F32) if lead else jnp.zeros((n, 0), F32),
                                    jnp.zeros((n, at * half), F32), a, b,
                                    jnp.zeros((n, (1 - at) * half + tail), F32)], axis=1)

        cos_r = place(jnp.cos(ang_r), jnp.cos(ang_r), 0, rows)
        cos_c = place(jnp.cos(ang_c), jnp.cos(ang_c), 1, GRID_W, 1.0)
        sin_r = place(-jnp.sin(ang_r), jnp.sin(ang_r), 0, rows)
        sin_c = place(-jnp.sin(ang_c), jnp.sin(ang_c), 1, GRID_W)
        assert cos_r.shape[1] == LANE and quarter * 2 == half
        cos = (cos_r[:, None, :] + cos_c[None, :, :]).reshape(s_len, LANE)
        sin = (sin_r[:, None, :] + sin_c[None, :, :]).reshape(s_len, LANE)
        return cos, sin

    cos_a, sin_a = lay(A_DIM, 0)
    cos_b, sin_b = lay(B_ROPE, B_NOPE)
    return cos_a, sin_a, cos_b, sin_b


def _pad_heads(w, n_heads, dim, axis):
    shape = w.shape[:axis] + (n_heads, dim) + w.shape[axis + 1:]
    pad = [(0, 0)] * len(shape)
    pad[axis + 1] = (0, LANE - dim)
    out = jnp.pad(w.reshape(shape), pad)
    return out.reshape(w.shape[:axis] + (n_heads * LANE,) + w.shape[axis + 1:])


def _unpad_heads(w, n_heads, dim, axis):
    shape = w.shape[:axis] + (n_heads, LANE) + w.shape[axis + 1:]
    out = lax.slice_in_dim(w.reshape(shape), 0, dim, axis=axis + 1)
    return out.reshape(w.shape[:axis] + (n_heads * dim,) + w.shape[axis + 1:])


def _pad_vec(g):
    return jnp.pad(g, ((0, 0), (0, LANE - g.shape[1])))


def _pack(w_in, w_uq, w_ukv, w_out, smalls):
    parts = [w_in.reshape(R_IN, LANE), w_uq.reshape(R_UQ, LANE), w_ukv.reshape(R_UKV, LANE),
             w_out.reshape(R_OUT, LANE), smalls.reshape(R_SMALL, LANE),
             jnp.zeros((R_PACK - R_BIG - R_SMALL, LANE), smalls.dtype)]
    return jnp.concatenate(parts, axis=0)


def _unpack(p):
    r0, r1, r2 = R_IN, R_IN + R_UQ, R_IN + R_UQ + R_UKV
    smalls = p[R_BIG:R_BIG + R_SMALL].reshape(R_SMALL * LANE)
    return (p[:r0].reshape(SH_IN), p[r0:r1].reshape(SH_UQ), p[r1:r2].reshape(SH_UKV),
            p[r2:R_BIG].reshape(SH_OUT), smalls)


def _flat_smalls(vecs, loss=None):
    flat = jnp.concatenate([v.reshape(-1) for v in vecs])
    tail = jnp.zeros((R_SMALL * LANE - LOSS_AT,), F32)
    if loss is not None:
        tail = tail.at[0].set(loss)
    return jnp.concatenate([flat, tail])


def _split_smalls(flat):
    out, at = [], 0
    for n in SMALL_SIZES:
        out.append(flat[at:at + n].reshape(1, n))
        at += n
    return out


def _ext_weights(full):
    r0, r1, r2 = R_IN, R_IN + R_UQ, R_IN + R_UQ + R_UKV
    w_in = full[:, :r0].reshape((N_CHIPS,) + SH_IN).transpose(1, 0, 2).reshape(D_MODEL, N_IN)
    w_uq = full[:, r0:r1].reshape(B_Q_RANK, B_HEADS * B_QK)
    w_ukv = full[:, r1:r2].reshape((N_CHIPS,) + SH_UKV).transpose(1, 0, 2).reshape(B_KV_RANK, B_HEADS * (B_NOPE + B_V))
    w_out = full[:, r2:R_BIG].reshape(D_MODEL, D_MODEL)
    a_w = A_HEADS * A_DIM
    kv_w = A_KV * A_DIM
    o = 0
    secs = []
    for width, heads in ((a_w, A_HEADS), (kv_w, A_KV), (kv_w, A_KV), (a_w, A_HEADS)):
        secs.append(_pad_heads(w_in[:, o:o + width], heads, A_DIM, 1))
        o += width
    a_q, a_k, a_v, a_g = secs
    b_cq = w_in[:, o:o + B_Q_RANK]
    o += B_Q_RANK
    b_ckv = w_in[:, o:o + B_KV_RANK]
    o += B_KV_RANK
    b_kr = jnp.pad(w_in[:, o:o + B_ROPE], ((0, 0), (B_NOPE, LANE - B_QK)))
    o += B_ROPE
    b_g = w_in[:, o:]
    w_in_ext = jnp.concatenate([a_q, a_k, a_v, a_g, b_g, b_cq, b_ckv, b_kr], axis=1)
    w_uq_pad = _pad_heads(w_uq, B_HEADS, B_QK, 1)
    kv3 = w_ukv.reshape(B_KV_RANK, B_HEADS, B_NOPE + B_V)
    w_ukv_ext = jnp.concatenate([_pad_heads(kv3[:, :, :B_NOPE].reshape(B_KV_RANK, B_HEADS * B_NOPE), B_HEADS, B_NOPE, 1),
                                 kv3[:, :, B_NOPE:].reshape(B_KV_RANK, B_HEADS * B_V)], axis=1)
    w_out_ext = jnp.concatenate([_pad_heads(w_out[:a_w], A_HEADS, A_DIM, 0), w_out[a_w:]], axis=0)
    return w_in_ext, w_uq_pad, w_ukv_ext, w_out_ext


def _fold_grads(d_in_ext, d_uq_pad, d_ukv_ext, d_out_ext):
    d_in = jnp.concatenate([
        _unpad_heads(d_in_ext[:, QA0:KA0], A_HEADS, A_DIM, 1),
        _unpad_heads(d_in_ext[:, KA0:VA0], A_KV, A_DIM, 1),
        _unpad_heads(d_in_ext[:, VA0:GA0], A_KV, A_DIM, 1),
        _unpad_heads(d_in_ext[:, GA0:GB0], A_HEADS, A_DIM, 1),
        d_in_ext[:, CQ0:KR0],
        d_in_ext[:, KR0 + B_NOPE:KR0 + B_QK],
        d_in_ext[:, GB0:CQ0]], axis=1)
    d_uq = _unpad_heads(d_uq_pad, B_HEADS, B_QK, 1)
    k3 = _unpad_heads(d_ukv_ext[:, :B_HEADS * LANE], B_HEADS, B_NOPE, 1).reshape(B_KV_RANK, B_HEADS, B_NOPE)
    v3 = d_ukv_ext[:, B_HEADS * LANE:].reshape(B_KV_RANK, B_HEADS, B_V)
    d_ukv = jnp.concatenate([k3, v3], axis=2).reshape(B_KV_RANK, B_HEADS * (B_NOPE + B_V))
    d_out = jnp.concatenate([_unpad_heads(d_out_ext[:A_HEADS * LANE], A_HEADS, A_DIM, 0), d_out_ext[A_HEADS * LANE:]], axis=0)
    return (d_in.reshape(D_MODEL, N_CHIPS, SH_IN[1]).transpose(1, 0, 2),
            d_uq.reshape((N_CHIPS,) + SH_UQ),
            d_ukv.reshape(B_KV_RANK, N_CHIPS, SH_UKV[1]).transpose(1, 0, 2),
            d_out.reshape((N_CHIPS,) + SH_OUT))


def kernel(x, norm_in, w_in, a_q_norm, a_k_norm, b_cq_norm, b_ckv_norm, w_uq, w_ukv, b_q_norm, b_k_norm, w_out, loss_target, m_norm_in, m_w_in, m_a_q_norm, m_a_k_norm, m_b_cq_norm, m_b_ckv_norm, m_w_uq, m_w_ukv, m_b_q_norm, m_b_k_norm, m_w_out, v_norm_in, v_w_in, v_a_q_norm, v_a_k_norm, v_b_cq_norm, v_b_ckv_norm, v_w_uq, v_w_ukv, v_b_q_norm, v_b_k_norm, v_w_out):
    s_len = x.shape[1]
    xs, ts = x[0], loss_target[0]
    tm = min(256, s_len)
    tq = min(512, s_len // 2)
    tk_f = min(1024, s_len // 2)
    tk_b = min(1024, s_len)

    small_w = (norm_in, a_q_norm, a_k_norm, b_cq_norm, b_ckv_norm, b_q_norm, b_k_norm)
    w_pack = _pack(w_in[0], w_uq[0], w_ukv[0], w_out[0], _flat_smalls(small_w))
    m_pack = _pack(m_w_in[0], m_w_uq[0], m_w_ukv[0], m_w_out[0],
                   _flat_smalls((m_norm_in, m_a_q_norm, m_a_k_norm, m_b_cq_norm, m_b_ckv_norm, m_b_q_norm, m_b_k_norm)))
    v_pack = _pack(v_w_in[0], v_w_uq[0], v_w_ukv[0], v_w_out[0],
                   _flat_smalls((v_norm_in, v_a_q_norm, v_a_k_norm, v_b_cq_norm, v_b_ckv_norm, v_b_q_norm, v_b_k_norm)))

    w_in_ext, w_uq_pad, w_ukv_ext, w_out_ext = _ext_weights(_gather_weights(w_pack))
    gains = (norm_in, _pad_vec(a_q_norm), _pad_vec(a_k_norm), b_cq_norm, b_ckv_norm, _pad_vec(b_q_norm), _pad_vec(b_k_norm))
    tabs = _rope_tables(s_len)

    (xn, gates, pre, qbpre, kbpre, cq, ckv, qa, ka, va, qb, kb, vb) = _pre(
        xs, tabs, w_in_ext, w_uq_pad, w_ukv_ext, gains, tm)
    o_a, lse_a = _attn_fwd(qa, ka, va, A_GROUP, A_DIM, tq, tk_f, "attn_fwd_a")
    o_b, lse_b = _attn_fwd(qb, kb, vb, 1, B_V, tq, tk_f, "attn_fwd_b")
    y, dh, dgate, do_a, do_b, delta, loss_part = _mid(xs, ts, o_a, o_b, gates, w_out_ext, tm)

    def stat(a):
        return a.reshape(a.shape[0], s_len // tq, 1, tq)

    dqa, dka, dva = _attn_bwd(qa, ka, va, do_a, lse_a, stat(delta[:A_HEADS]), A_GROUP, tq, tk_b, "attn_bwd_a")
    dqb, dkb, dvb = _attn_bwd(qb, kb, vb, do_b, lse_b, stat(delta[A_HEADS:A_HEADS + B_HEADS]), 1, tq, tk_b, "attn_bwd_b")
    (grad_x, dproj, dqbpre, dkvb, *d_gains) = _post(
        xs, dh, pre, qbpre, kbpre, dgate, dqa, dka, dva, dqb, dkb, dvb, tabs, w_in_ext, w_uq_pad, w_ukv_ext, gains, tm)

    ts_w = min(512, s_len)
    d_in_ext = _grad_w(xn, dproj, 768, ts_w, "grad_w_in")
    d_out_ext = _grad_w(y, dh, 512, ts_w, "grad_w_out")
    d_uq_pad = _grad_w(cq, dqbpre, 512, ts_w, "grad_w_uq")
    d_ukv_ext = _grad_w(ckv, dkvb, 1024, ts_w, "grad_w_ukv")

    g_in, g_uq, g_ukv, g_out = _fold_grads(d_in_ext, d_uq_pad, d_ukv_ext, d_out_ext)
    d_small = _flat_smalls(
        [d_gains[0], d_gains[1][:, :A_DIM], d_gains[2][:, :A_DIM], d_gains[3], d_gains[4],
         d_gains[5][:, :B_QK], d_gains[6][:, :B_QK]], loss=loss_part[0, 0])
    parts = jnp.stack([_pack(g_in[j], g_uq[j], g_ukv[j], g_out[j], d_small) for j in range(N_CHIPS)])
    g_pack = _reduce_grads(parts)
    d_pack, nm_pack, nv_pack = _adamw(w_pack, g_pack, m_pack, v_pack, 1536)

    def leaves(p):
        p_in, p_uq, p_ukv, p_out, flat = _unpack(p)
        s = _split_smalls(flat)
        return [s[0], p_in[None], s[1], s[2], s[3], s[4], p_uq[None], p_ukv[None], s[5], s[6], p_out[None]], flat

    grads, g_flat = leaves(g_pack)
    loss = g_flat[LOSS_AT]
    return (loss, grad_x[None], *grads, *leaves(d_pack)[0], *leaves(nm_pack)[0], *leaves(nv_pack)[0])
```

```python
import functools

import jax
import jax.numpy as jnp
import numpy as np
from jax import lax
from jax.experimental import pallas as pl
from jax.experimental.pallas import tpu as pltpu

F32 = jnp.float32
BF16 = jnp.bfloat16
MESH = pl.DeviceIdType.MESH

D_MODEL = 1024
GRID_W = 64
ROPE_THETA = 10000.0
EPS = 1e-6
A_HEADS, A_KV, A_DIM = 8, 2, 64
A_GROUP = A_HEADS // A_KV
B_HEADS, B_NOPE, B_ROPE, B_V = 4, 64, 32, 128
B_QK = B_NOPE + B_ROPE
B_Q_RANK, B_KV_RANK = 384, 256
N_IN = 2464
SCALE_A = 1.0 / float(np.sqrt(A_DIM))
SCALE_B = 1.0 / float(np.sqrt(B_QK))
LOG2E = float(np.log2(np.e))
LN2 = float(np.log(2.0))
ADAM_LR, ADAM_B1, ADAM_B2, ADAM_EPS, ADAM_WD, ADAM_STEP = 0.001, 0.9, 0.999, 1e-08, 0.01, 10

LANE = 128
VMEM_BYTES = 64 * 1024 * 1024
VMEM_LIMIT = VMEM_BYTES - 8 * 1024 * 1024

QA0 = 0
KA0 = QA0 + A_HEADS * LANE
VA0 = KA0 + A_KV * LANE
GA0 = VA0 + A_KV * LANE
GB0 = GA0 + A_HEADS * LANE
CQ0 = GB0 + B_HEADS * LANE
CKV0 = CQ0 + B_Q_RANK
KR0 = CKV0 + B_KV_RANK
N_EXT = KR0 + LANE
N_GATE = (A_HEADS + B_HEADS) * LANE
DELTA_ROWS = 16
N_PRE = KA0 + A_KV * LANE + B_Q_RANK + B_KV_RANK

N_CHIPS = 4
SH_IN = (D_MODEL, N_IN // N_CHIPS)
SH_UQ = (B_Q_RANK // N_CHIPS, B_HEADS * B_QK)
SH_UKV = (B_KV_RANK, B_HEADS * (B_NOPE + B_V) // N_CHIPS)
SH_OUT = (D_MODEL // N_CHIPS, D_MODEL)
R_IN = SH_IN[0] * SH_IN[1] // LANE
R_UQ = SH_UQ[0] * SH_UQ[1] // LANE
R_UKV = SH_UKV[0] * SH_UKV[1] // LANE
R_OUT = SH_OUT[0] * SH_OUT[1] // LANE
R_BIG = R_IN + R_UQ + R_UKV + R_OUT
R_SMALL = 16
R_PACK = 7680
R_HALF = R_PACK // 2
SMALL_SIZES = (D_MODEL, A_DIM, A_DIM, B_Q_RANK, B_KV_RANK, B_QK, B_QK)
LOSS_AT = sum(SMALL_SIZES)


def _pallas(body, **kw):
    return pl.pallas_call(body, **kw)


def _params(sem=None, vmem=None):
    return pltpu.CompilerParams(dimension_semantics=sem, vmem_limit_bytes=vmem)


def _rms_fwd(x, g, n):
    r = lax.rsqrt(jnp.sum(x * x, axis=-1, keepdims=True) * (1.0 / n) + EPS)
    return x * r * g


def _rms_bwd(dy, x, g, n):
    r = lax.rsqrt(jnp.sum(x * x, axis=-1, keepdims=True) * (1.0 / n) + EPS)
    xhat = x * r
    u = dy * g
    dx = r * (u - xhat * (jnp.sum(u * xhat, axis=-1, keepdims=True) * (1.0 / n)))
    return dx, dy * xhat


def _partner(y, half):
    lane = lax.broadcasted_iota(jnp.int32, y.shape, 1)
    first = (lane % (2 * half)) < half
    return jnp.where(first, pltpu.roll(y, LANE - half, 1), pltpu.roll(y, half, 1))


def _rope_fwd(y, cos, sin, half):
    return y * cos + _partner(y, half) * sin


def _rope_bwd(d, cos, sin, half):
    return d * cos - _partner(d, half) * sin


def _nt(a, b):
    return lax.dot_general(a, b, (((1,), (1,)), ((), ())), preferred_element_type=F32)


def _tn(a, b):
    return lax.dot_general(a, b, (((0,), (0,)), ((), ())), preferred_element_type=F32)


def _nn(a, b):
    return jnp.dot(a, b, preferred_element_type=F32)


def _block_rows(i, size):
    if isinstance(i, int):
        return pl.ds(i * size, size)
    return pl.ds(pl.multiple_of(i * size, size), size)


MAX_STATIC_BLOCKS = 8


def _three_stage(n, first, second, third):
    assert n >= 2 and n % 2 == 0
    first(0, 0)
    first(1, 1)
    second(0, 0)
    if n <= MAX_STATIC_BLOCKS:
        for i in range(1, n - 1):
            first(i + 1, (i + 1) % 2)
            second(i, i % 2)
            third(i - 1, (i - 1) % 2)
    else:
        def pair(t, carry):
            i = 2 * t + 1
            first(i + 1, 0)
            second(i, 1)
            third(i - 1, 0)
            first(i + 2, 1)
            second(i + 1, 0)
            third(i, 1)
            return carry

        lax.fori_loop(0, (n - 2) // 2, pair, 0)
    second(n - 1, 1)
    third(n - 2, 0)
    third(n - 1, 1)


def _full(shape):
    return pl.BlockSpec(shape, lambda *_: (0,) * len(shape))


def _gather_weights(w_pack):
    def body(w_ref, out_ref, send_sems, recv_sems):
        x, y, c = lax.axis_index("x"), lax.axis_index("y"), lax.axis_index("c")
        sibling = (x, y, 1 - c)
        chips = [(1 - x, y), (x, 1 - y), (1 - x, 1 - y)]
        me = 2 * x + y

        def half(j, hc):
            return out_ref.at[j, pl.ds(pl.multiple_of(hc * R_HALF, R_HALF), R_HALF), :]

        def copy(k, j, hc, to):
            return pltpu.make_async_remote_copy(
                src_ref=half(j, hc), dst_ref=half(j, hc), send_sem=send_sems.at[k], recv_sem=recv_sems.at[k],
                device_id=to, device_id_type=MESH)

        out_ref[me] = w_ref[...].astype(BF16)
        first = [copy(k, me, c, (*chip, c)) for k, chip in enumerate(chips)]
        for cp in first:
            cp.start()
        passed = [copy(3 + k, 2 * chip[0] + chip[1], c, sibling) for k, chip in enumerate(chips)]
        for k, chip in enumerate(chips):
            copy(k, 2 * chip[0] + chip[1], c, (*chip, c)).wait_recv()
            passed[k].start()
        for k, chip in enumerate(chips):
            copy(3 + k, 2 * chip[0] + chip[1], 1 - c, sibling).wait_recv()
        for cp in first + passed:
            cp.wait_send()

    return _pallas(
        body, name="gather_weights",
        out_shape=jax.ShapeDtypeStruct((N_CHIPS, R_PACK, LANE), BF16),
        in_specs=[pl.BlockSpec(memory_space=pltpu.VMEM)],
        out_specs=pl.BlockSpec(memory_space=pltpu.VMEM),
        scratch_shapes=[pltpu.SemaphoreType.DMA((6,)), pltpu.SemaphoreType.DMA((6,))],
        compiler_params=_params(vmem=32 * 1024 * 1024),
    )(w_pack)


def _reduce_grads(parts):
    def body(p_ref, out_ref, rec_a, rec_b, sa_send, sa_recv, sb_send, sb_recv, sc_send, sc_recv):
        x, y, c = lax.axis_index("x"), lax.axis_index("y"), lax.axis_index("c")
        sibling = (x, y, 1 - c)
        me = 2 * x + y
        mine = pl.ds(pl.multiple_of(c * R_HALF, R_HALF), R_HALF)
        other = pl.ds(pl.multiple_of((1 - c) * R_HALF, R_HALF), R_HALF)

        def copy_a(j):
            return pltpu.make_async_remote_copy(
                src_ref=p_ref.at[j, other, :], dst_ref=rec_a.at[j], send_sem=sa_send.at[j], recv_sem=sa_recv.at[j],
                device_id=sibling, device_id_type=MESH)

        def copy_b(r):
            j = me ^ r
            return pltpu.make_async_remote_copy(
                src_ref=rec_a.at[j], dst_ref=rec_b.at[r], send_sem=sb_send.at[r - 1], recv_sem=sb_recv.at[r - 1],
                device_id=(j // 2, j % 2, c), device_id_type=MESH)

        copy_c = pltpu.make_async_remote_copy(
            src_ref=out_ref.at[mine, :], dst_ref=out_ref.at[mine, :], send_sem=sc_send, recv_sem=sc_recv,
            device_id=sibling, device_id_type=MESH)

        for j in range(N_CHIPS):
            copy_a(j).start()
        for r in range(1, N_CHIPS):
            j = me ^ r
            copy_a(j).wait_recv()
            rec_a[j] = rec_a[j] + p_ref[j, mine, :]
            copy_b(r).start()
        copy_a(me).wait_recv()
        rec_b[0] = rec_a[me] + p_ref[me, mine, :]
        for r in range(1, N_CHIPS):
            copy_b(r).wait_recv()
        total = rec_b[me]
        for j in range(1, N_CHIPS):
            total = total + rec_b[j ^ me]
        out_ref[mine, :] = total
        copy_c.start()
        copy_c.wait_recv()
        for j in range(N_CHIPS):
            copy_a(j).wait_send()
        for r in range(1, N_CHIPS):
            copy_b(r).wait_send()
        copy_c.wait_send()

    return _pallas(
        body, name="reduce_grads",
        out_shape=jax.ShapeDtypeStruct((R_PACK, LANE), F32),
        in_specs=[pl.BlockSpec(memory_space=pltpu.VMEM)],
        out_specs=pl.BlockSpec(memory_space=pltpu.VMEM),
        scratch_shapes=[
            pltpu.VMEM((N_CHIPS, R_HALF, LANE), F32),
            pltpu.VMEM((N_CHIPS, R_HALF, LANE), F32),
            pltpu.SemaphoreType.DMA((N_CHIPS,)), pltpu.SemaphoreType.DMA((N_CHIPS,)),
            pltpu.SemaphoreType.DMA((N_CHIPS - 1,)), pltpu.SemaphoreType.DMA((N_CHIPS - 1,)),
            pltpu.SemaphoreType.DMA, pltpu.SemaphoreType.DMA,
        ],
        compiler_params=_params(vmem=VMEM_LIMIT),
    )(parts)


def _pre(x, tabs, w_in_ext, w_uq_pad, w_ukv_ext, gains, tm):
    s_len = x.shape[0]
    nt = s_len // tm

    def body(x_ref, ca_ref, sa_ref, cb_ref, sb_ref, win_ref, wuq_ref, wukv_ref,
             gin_ref, gaq_ref, gak_ref, gcq_ref, gckv_ref, gbq_ref, gbk_ref,
             xn_ref, gates_ref, pre_ref, qbpre_ref, kbpre_ref, cq_ref, ckv_ref,
             qa_ref, ka_ref, va_ref, qb_ref, kb_ref, vb_ref, proj):
        xn = _rms_fwd(x_ref[...], gin_ref[...], D_MODEL).astype(BF16)
        xn_ref[...] = xn
        proj[...] = _nn(xn, win_ref[...])
        gates_ref[...] = proj[:, GA0:GA0 + N_GATE]
        pre_ref[:, 0:VA0] = proj[:, 0:VA0]
        pre_ref[:, VA0:N_PRE] = proj[:, CQ0:KR0]
        ca, sa, cb, sb = ca_ref[...], sa_ref[...], cb_ref[...], sb_ref[...]
        lane = lax.broadcasted_iota(jnp.int32, (tm, LANE), 1)
        for h in range(A_HEADS):
            yq = _rms_fwd(proj[:, QA0 + LANE * h:QA0 + LANE * (h + 1)], gaq_ref[...], A_DIM)
            qa_ref[h] = (_rope_fwd(yq, ca, sa, A_DIM // 4) * (SCALE_A * LOG2E)).astype(BF16)
        for h in range(A_KV):
            yk = _rms_fwd(proj[:, KA0 + LANE * h:KA0 + LANE * (h + 1)], gak_ref[...], A_DIM)
            ka_ref[h] = _rope_fwd(yk, ca, sa, A_DIM // 4).astype(BF16)
            va_ref[h] = jnp.where(lane == A_DIM, 1.0, proj[:, VA0 + LANE * h:VA0 + LANE * (h + 1)]).astype(BF16)
        cq = _rms_fwd(proj[:, CQ0:CQ0 + B_Q_RANK], gcq_ref[...], B_Q_RANK).astype(BF16)
        cq_ref[...] = cq
        qbpre_ref[...] = _nn(cq, wuq_ref[...])
        ckv = _rms_fwd(proj[:, CKV0:CKV0 + B_KV_RANK], gckv_ref[...], B_KV_RANK).astype(BF16)
        ckv_ref[...] = ckv
        kvb = _nn(ckv, wukv_ref[...])
        kr = proj[:, KR0:KR0 + LANE]
        for h in range(B_HEADS):
            yq = _rms_fwd(qbpre_ref[:, LANE * h:LANE * (h + 1)], gbq_ref[...], B_QK)
            qb_ref[h] = (_rope_fwd(yq, cb, sb, B_ROPE // 4) * (SCALE_B * LOG2E)).astype(BF16)
            kp = kvb[:, LANE * h:LANE * (h + 1)] + kr
            kbpre_ref[:, LANE * h:LANE * (h + 1)] = kp
            kb_ref[h] = _rope_fwd(_rms_fwd(kp, gbk_ref[...], B_QK), cb, sb, B_ROPE // 4).astype(BF16)
            vb_ref[h, :, 0:LANE] = kvb[:, B_HEADS * LANE + LANE * h:B_HEADS * LANE + LANE * (h + 1)].astype(BF16)
            vb_ref[h, :, LANE:2 * LANE] = jnp.where(lane == 0, 1.0, 0.0).astype(BF16)

    row = lambda w: pl.BlockSpec((tm, w), lambda i: (i, 0))
    heads = lambda n: pl.BlockSpec((n, tm, LANE), lambda i: (0, i, 0))
    hs = lambda n: jax.ShapeDtypeStruct((n, s_len, LANE), BF16)
    return _pallas(
        body, name="pre", grid=(nt,),
        in_specs=[row(D_MODEL), row(LANE), row(LANE), row(LANE), row(LANE),
                  _full(w_in_ext.shape), _full(w_uq_pad.shape), _full(w_ukv_ext.shape)]
                 + [_full(g.shape) for g in gains],
        out_specs=[row(D_MODEL), row(N_GATE), row(N_PRE), row(B_HEADS * LANE), row(B_HEADS * LANE),
                   row(B_Q_RANK), row(B_KV_RANK),
                   heads(A_HEADS), heads(A_KV), heads(A_KV), heads(B_HEADS), heads(B_HEADS),
                   pl.BlockSpec((B_HEADS, tm, 2 * LANE), lambda i: (0, i, 0))],
        out_shape=[jax.ShapeDtypeStruct((s_len, D_MODEL), BF16), jax.ShapeDtypeStruct((s_len, N_GATE), F32),
                   jax.ShapeDtypeStruct((s_len, N_PRE), F32), jax.ShapeDtypeStruct((s_len, B_HEADS * LANE), F32),
                   jax.ShapeDtypeStruct((s_len, B_HEADS * LANE), F32),
                   jax.ShapeDtypeStruct((s_len, B_Q_RANK), BF16), jax.ShapeDtypeStruct((s_len, B_KV_RANK), BF16),
                   hs(A_HEADS), hs(A_KV), hs(A_KV), hs(B_HEADS), hs(B_HEADS),
                   jax.ShapeDtypeStruct((B_HEADS, s_len, 2 * LANE), BF16)],
        scratch_shapes=[pltpu.VMEM((tm, N_EXT), F32)],
        compiler_params=_params(("parallel",), VMEM_LIMIT),
    )(x, *tabs, w_in_ext, w_uq_pad, w_ukv_ext, *gains)


def _attn_fwd(q, k, v, group, l_col, tq, tk, name):
    n_heads, s_len, _ = q.shape
    v_w = v.shape[2]
    nk = s_len // tk
    assert nk >= 2 and nk % 2 == 0

    def body(q_ref, k_ref, v_ref, o_ref, lse_ref, s_buf, p_buf, a_buf, m_ref, acc_ref):
        def rows(j):
            return _block_rows(j, tk)

        def scores(j, slot):
            s_buf[slot] = _nt(q_ref[...], k_ref[rows(j), :])

        def softmax(j, slot):
            s = s_buf[slot]
            m_old = m_ref[...]
            m_new = jnp.maximum(m_old, jnp.max(s, axis=-1, keepdims=True))
            m_ref[...] = m_new
            a_buf[slot] = jnp.exp2(m_old - m_new)
            p_buf[slot] = jnp.exp2(s - jnp.tile(m_new, (1, tk // LANE))).astype(BF16)

        def values(j, slot):
            pv = _nn(p_buf[slot], v_ref[rows(j), :])
            for c in range(0, v_w, LANE):
                acc_ref[:, c:c + LANE] = a_buf[slot] * acc_ref[:, c:c + LANE] + pv[:, c:c + LANE]

        m_ref[...] = jnp.full(m_ref.shape, -1e30, F32)
        acc_ref[...] = jnp.zeros(acc_ref.shape, F32)
        _three_stage(nk, scores, softmax, values)
        l = acc_ref[:, l_col:l_col + 1]
        o = acc_ref[:, 0:LANE] * (1.0 / l)
        if l_col < LANE:
            lane = lax.broadcasted_iota(jnp.int32, o.shape, 1)
            o = jnp.where(lane == l_col, 0.0, o)
        o_ref[...] = o
        lse_ref[...] = jnp.transpose(m_ref[...] + jnp.log2(jnp.broadcast_to(l, (tq, LANE))))[0:1, :]

    return _pallas(
        body, name=name, grid=(n_heads, s_len // tq),
        in_specs=[pl.BlockSpec((None, tq, LANE), lambda h, i: (h, i, 0)),
                  pl.BlockSpec((None, s_len, LANE), lambda h, i: (h // group, 0, 0)),
                  pl.BlockSpec((None, s_len, v_w), lambda h, i: (h // group, 0, 0))],
        out_specs=[pl.BlockSpec((None, tq, LANE), lambda h, i: (h, i, 0)),
                   pl.BlockSpec((None, None, 1, tq), lambda h, i: (h, i, 0, 0))],
        out_shape=[jax.ShapeDtypeStruct((n_heads, s_len, LANE), F32),
                   jax.ShapeDtypeStruct((n_heads, s_len // tq, 1, tq), F32)],
        scratch_shapes=[pltpu.VMEM((2, tq, tk), F32), pltpu.VMEM((2, tq, tk), BF16), pltpu.VMEM((2, tq, LANE), F32),
                        pltpu.VMEM((tq, LANE), F32), pltpu.VMEM((tq, v_w), F32)],
        compiler_params=_params(("parallel", "parallel"), 48 * 1024 * 1024),
    )(q, k, v)


def _mid(x, target, o_a, o_b, gates, w_out_ext, tm):
    s_len = x.shape[0]
    nt = s_len // tm
    n_heads = A_HEADS + B_HEADS

    def body(x_ref, t_ref, oa_ref, ob_ref, g_ref, w_ref,
             y_ref, dh_ref, dgate_ref, doa_ref, dob_ref, delta_ref, loss_ref, silu_scr, dsilu_scr):
        @pl.when(pl.program_id(0) == 0)
        def _():
            loss_ref[...] = jnp.zeros_like(loss_ref)

        def o_of(h):
            return oa_ref[h] if h < A_HEADS else ob_ref[h - A_HEADS]

        for h in range(n_heads):
            cols = slice(LANE * h, LANE * (h + 1))
            g = g_ref[:, cols]
            sig = 1.0 / (1.0 + jnp.exp(-g))
            silu = g * sig
            silu_scr[:, cols] = silu
            dsilu_scr[:, cols] = sig * (1.0 + g * (1.0 - sig))
            y_ref[:, cols] = (o_of(h) * silu).astype(BF16)
        err = x_ref[...] + _nn(y_ref[...], w_ref[...]) - t_ref[...]
        sq = jnp.sum(jnp.sum(err * err, axis=-1, keepdims=True), axis=0, keepdims=True)
        loss_ref[...] += jnp.broadcast_to(sq * (0.5 / D_MODEL), loss_ref.shape)
        dh = err * (1.0 / D_MODEL)
        dh_ref[...] = dh
        dy = _nt(dh.astype(BF16), w_ref[...])
        lane = lax.broadcasted_iota(jnp.int32, (tm, LANE), 1)
        delta = jnp.zeros((tm, LANE), F32)
        for h in range(n_heads):
            cols = slice(LANE * h, LANE * (h + 1))
            dyh = dy[:, cols]
            oh = o_of(h)
            do = dyh * silu_scr[:, cols]
            dgate_ref[:, cols] = (dyh * oh * dsilu_scr[:, cols]).astype(BF16)
            delta = jnp.where(lane == h, jnp.sum(do * oh, axis=-1, keepdims=True), delta)
            if h < A_HEADS:
                doa_ref[h] = do.astype(BF16)
            else:
                dob_ref[h - A_HEADS] = do.astype(BF16)
        delta_ref[...] = jnp.transpose(delta)[0:DELTA_ROWS, :]

    row = lambda w: pl.BlockSpec((tm, w), lambda i: (i, 0))
    heads = lambda n, w=LANE: pl.BlockSpec((n, tm, w), lambda i: (0, i, 0))
    return _pallas(
        body, name="mid", grid=(nt,),
        in_specs=[row(D_MODEL), row(D_MODEL), heads(A_HEADS), heads(B_HEADS), row(N_GATE), _full(w_out_ext.shape)],
        out_specs=[row(N_GATE), row(D_MODEL), row(N_GATE), heads(A_HEADS), heads(B_HEADS),
                   pl.BlockSpec((DELTA_ROWS, tm), lambda i: (0, i)),
                   _full((8, LANE))],
        out_shape=[jax.ShapeDtypeStruct((s_len, N_GATE), BF16), jax.ShapeDtypeStruct((s_len, D_MODEL), F32),
                   jax.ShapeDtypeStruct((s_len, N_GATE), BF16),
                   jax.ShapeDtypeStruct((A_HEADS, s_len, LANE), BF16), jax.ShapeDtypeStruct((B_HEADS, s_len, LANE), BF16),
                   jax.ShapeDtypeStruct((DELTA_ROWS, s_len), F32), jax.ShapeDtypeStruct((8, LANE), F32)],
        scratch_shapes=[pltpu.VMEM((tm, N_GATE), F32), pltpu.VMEM((tm, N_GATE), F32)],
        compiler_params=_params(("arbitrary",), VMEM_LIMIT),
    )(x, target, o_a, o_b, gates, w_out_ext)


def _attn_bwd(q, k, v, do, lse, delta, group, tq, tk, name):
    n_heads, s_len, _ = q.shape
    nq = s_len // tq
    assert nq >= 2 and nq % 2 == 0

    def body(q_ref, do_ref, lse_ref, delta_ref, k_ref, v_ref, dq_ref, dk_ref, dv_ref, s_buf, dp_buf, p_buf, ds_buf):
        @pl.when(pl.program_id(1) == 0)
        def _():
            dq_ref[...] = jnp.zeros_like(dq_ref)

        dk_ref[...] = jnp.zeros_like(dk_ref)
        dv_ref[...] = jnp.zeros_like(dv_ref)

        def rows(i):
            return _block_rows(i, tq)

        def scores(i, slot):
            s_buf[slot] = _nt(k_ref[...], q_ref[rows(i), :])
            dp_buf[slot] = _nt(v_ref[...], do_ref[rows(i), :])

        def elementwise(i, slot):
            p = jnp.exp2(s_buf[slot] - lse_ref[i])
            p_buf[slot] = p.astype(BF16)
            ds_buf[slot] = (p * (dp_buf[slot] - delta_ref[i])).astype(BF16)

        def grads(i, slot):
            dv_ref[...] += _nn(p_buf[slot], do_ref[rows(i), :])
            dk_ref[...] += _nn(ds_buf[slot], q_ref[rows(i), :])
            dq_ref[rows(i), :] += _tn(ds_buf[slot], k_ref[...])

        _three_stage(nq, scores, elementwise, grads)

    whole = lambda: pl.BlockSpec((None, s_len, LANE), lambda h, j: (h, 0, 0))
    stat = lambda: pl.BlockSpec((None, nq, 1, tq), lambda h, j: (h, 0, 0, 0))
    kvb = lambda: pl.BlockSpec((None, tk, LANE), lambda h, j: (h // group, j, 0))
    outb = lambda: pl.BlockSpec((None, tk, LANE), lambda h, j: (h, j, 0))
    shape = jax.ShapeDtypeStruct((n_heads, s_len, LANE), F32)
    return _pallas(
        body, name=name, grid=(n_heads, s_len // tk),
        in_specs=[whole(), whole(), stat(), stat(), kvb(), kvb()],
        out_specs=[whole(), outb(), outb()],
        out_shape=[shape, shape, shape],
        scratch_shapes=[pltpu.VMEM((2, tk, tq), F32), pltpu.VMEM((2, tk, tq), F32),
                        pltpu.VMEM((2, tk, tq), BF16), pltpu.VMEM((2, tk, tq), BF16)],
        compiler_params=_params(("parallel", "arbitrary"), 48 * 1024 * 1024),
    )(q, do, lse, delta, k, v)


def _post(x, dh, pre, qbpre, kbpre, dgate, dqa, dka, dva, dqb, dkb, dvb, tabs,
          w_in_ext, w_uq_pad, w_ukv_ext, gains, tm):
    s_len = x.shape[0]
    nt = s_len // tm

    def body(x_ref, dh_ref, pre_ref, qbpre_ref, kbpre_ref, dgate_ref,
             dqa_ref, dka_ref, dva_ref, dqb_ref, dkb_ref, dvb_ref,
             ca_ref, sa_ref, cb_ref, sb_ref, win_ref, wuq_ref, wukv_ref,
             gin_ref, gaq_ref, gak_ref, gcq_ref, gckv_ref, gbq_ref, gbk_ref,
             gx_ref, dproj_ref, dqbpre_ref, dkvb_ref,
             dgin_ref, dgaq_ref, dgak_ref, dgcq_ref, dgckv_ref, dgbq_ref, dgbk_ref):
        @pl.when(pl.program_id(0) == 0)
        def _():
            for r in (dgin_ref, dgaq_ref, dgak_ref, dgcq_ref, dgckv_ref, dgbq_ref, dgbk_ref):
                r[...] = jnp.zeros_like(r)

        def tok_sum(a):
            return jnp.sum(a, axis=0, keepdims=True)

        ca, sa, cb, sb = ca_ref[...], sa_ref[...], cb_ref[...], sb_ref[...]
        lane = lax.broadcasted_iota(jnp.int32, (tm, LANE), 1)
        dg = jnp.zeros((1, LANE), F32)
        for h in range(A_HEADS):
            dn = _rope_bwd(dqa_ref[h] * SCALE_A, ca, sa, A_DIM // 4)
            dx, dgr = _rms_bwd(dn, pre_ref[:, QA0 + LANE * h:QA0 + LANE * (h + 1)], gaq_ref[...], A_DIM)
            dproj_ref[:, QA0 + LANE * h:QA0 + LANE * (h + 1)] = dx.astype(BF16)
            dg = dg + tok_sum(dgr)
        dgaq_ref[...] += dg
        dg = jnp.zeros((1, LANE), F32)
        for h in range(A_KV):
            dk = dka_ref[A_GROUP * h]
            dv = dva_ref[A_GROUP * h]
            for g in range(1, A_GROUP):
                dk = dk + dka_ref[A_GROUP * h + g]
                dv = dv + dva_ref[A_GROUP * h + g]
            dn = _rope_bwd(dk * LN2, ca, sa, A_DIM // 4)
            dx, dgr = _rms_bwd(dn, pre_ref[:, KA0 + LANE * h:KA0 + LANE * (h + 1)], gak_ref[...], A_DIM)
            dproj_ref[:, KA0 + LANE * h:KA0 + LANE * (h + 1)] = dx.astype(BF16)
            dproj_ref[:, VA0 + LANE * h:VA0 + LANE * (h + 1)] = dv.astype(BF16)
            dg = dg + tok_sum(dgr)
        dgak_ref[...] += dg
        dproj_ref[:, GA0:GA0 + N_GATE] = dgate_ref[...]
        dg = jnp.zeros((1, LANE), F32)
        for h in range(B_HEADS):
            cols = slice(LANE * h, LANE * (h + 1))
            dn = _rope_bwd(dqb_ref[h] * SCALE_B, cb, sb, B_ROPE // 4)
            dx, dgr = _rms_bwd(dn, qbpre_ref[:, cols], gbq_ref[...], B_QK)
            dqbpre_ref[:, cols] = dx.astype(BF16)
            dg = dg + tok_sum(dgr)
        dgbq_ref[...] += dg
        dcq = _nt(dqbpre_ref[...], wuq_ref[...])
        dx, dgr = _rms_bwd(dcq, pre_ref[:, VA0:VA0 + B_Q_RANK], gcq_ref[...], B_Q_RANK)
        dproj_ref[:, CQ0:CQ0 + B_Q_RANK] = dx.astype(BF16)
        dgcq_ref[...] += tok_sum(dgr)
        dg = jnp.zeros((1, LANE), F32)
        dkr = jnp.zeros((tm, LANE), F32)
        for h in range(B_HEADS):
            cols = slice(LANE * h, LANE * (h + 1))
            dn = _rope_bwd(dkb_ref[h] * LN2, cb, sb, B_ROPE // 4)
            dx, dgr = _rms_bwd(dn, kbpre_ref[:, cols], gbk_ref[...], B_QK)
            dkvb_ref[:, cols] = jnp.where(lane < B_NOPE, dx, 0.0).astype(BF16)
            dkvb_ref[:, B_HEADS * LANE + LANE * h:B_HEADS * LANE + LANE * (h + 1)] = dvb_ref[h].astype(BF16)
            dkr = dkr + dx
            dg = dg + tok_sum(dgr)
        dgbk_ref[...] += dg
        dproj_ref[:, KR0:KR0 + LANE] = jnp.where((lane >= B_NOPE) & (lane < B_QK), dkr, 0.0).astype(BF16)
        dckv = _nt(dkvb_ref[...], wukv_ref[...])
        dx, dgr = _rms_bwd(dckv, pre_ref[:, VA0 + B_Q_RANK:N_PRE], gckv_ref[...], B_KV_RANK)
        dproj_ref[:, CKV0:CKV0 + B_KV_RANK] = dx.astype(BF16)
        dgckv_ref[...] += tok_sum(dgr)
        dxn = _nt(dproj_ref[...], win_ref[...])
        dx, dgr = _rms_bwd(dxn, x_ref[...], gin_ref[...], D_MODEL)
        gx_ref[...] = dh_ref[...] + dx
        dgin_ref[...] += tok_sum(dgr)

    row = lambda w: pl.BlockSpec((tm, w), lambda i: (i, 0))
    heads = lambda n: pl.BlockSpec((n, tm, LANE), lambda i: (0, i, 0))
    return _pallas(
        body, name="post", grid=(nt,),
        in_specs=[row(D_MODEL), row(D_MODEL), row(N_PRE), row(B_HEADS * LANE), row(B_HEADS * LANE), row(N_GATE),
                  heads(A_HEADS), heads(A_HEADS), heads(A_HEADS), heads(B_HEADS), heads(B_HEADS), heads(B_HEADS),
                  row(LANE), row(LANE), row(LANE), row(LANE),
                  _full(w_in_ext.shape), _full(w_uq_pad.shape), _full(w_ukv_ext.shape)]
                 + [_full(g.shape) for g in gains],
        out_specs=[row(D_MODEL), row(N_EXT), row(B_HEADS * LANE), row(2 * B_HEADS * LANE)]
                  + [_full(g.shape) for g in gains],
        out_shape=[jax.ShapeDtypeStruct((s_len, D_MODEL), F32), jax.ShapeDtypeStruct((s_len, N_EXT), BF16),
                   jax.ShapeDtypeStruct((s_len, B_HEADS * LANE), BF16),
                   jax.ShapeDtypeStruct((s_len, 2 * B_HEADS * LANE), BF16)]
                  + [jax.ShapeDtypeStruct(g.shape, F32) for g in gains],
        compiler_params=_params(("arbitrary",), VMEM_LIMIT),
    )(x, dh, pre, qbpre, kbpre, dgate, dqa, dka, dva, dqb, dkb, dvb, *tabs,
      w_in_ext, w_uq_pad, w_ukv_ext, *gains)


def _grad_w(a, b, tn, ts, name):
    s_len, m = a.shape
    n = b.shape[1]

    def body(a_ref, b_ref, o_ref):
        @pl.when(pl.program_id(1) == 0)
        def _():
            o_ref[...] = jnp.zeros_like(o_ref)

        o_ref[...] += _tn(a_ref[...].astype(BF16), b_ref[...].astype(BF16))

    return _pallas(
        body, name=name, grid=(n // tn, s_len // ts),
        in_specs=[pl.BlockSpec((ts, m), lambda j, t: (t, 0)), pl.BlockSpec((ts, tn), lambda j, t: (t, j))],
        out_specs=pl.BlockSpec((m, tn), lambda j, t: (0, j)),
        out_shape=jax.ShapeDtypeStruct((m, n), F32),
        compiler_params=_params(("parallel", "arbitrary"), 48 * 1024 * 1024),
    )(a, b)


def _adamw(w, g, m, v, tr):
    rows = w.shape[0]

    def body(w_ref, g_ref, m_ref, v_ref, d_ref, nm_ref, nv_ref):
        gv = g_ref[...]
        nm = ADAM_B1 * m_ref[...] + (1.0 - ADAM_B1) * gv
        nv = ADAM_B2 * v_ref[...] + (1.0 - ADAM_B2) * (gv * gv)
        m_hat = nm / (1.0 - ADAM_B1 ** ADAM_STEP)
        v_hat = nv / (1.0 - ADAM_B2 ** ADAM_STEP)
        d_ref[...] = -ADAM_LR * (m_hat / (jnp.sqrt(v_hat) + ADAM_EPS) + ADAM_WD * w_ref[...])
        nm_ref[...] = nm
        nv_ref[...] = nv

    blk = pl.BlockSpec((tr, LANE), lambda i: (i, 0))
    shape = jax.ShapeDtypeStruct((rows, LANE), F32)
    return _pallas(
        body, name="adamw", grid=(rows // tr,),
        in_specs=[blk, blk, blk, blk], out_specs=[blk, blk, blk], out_shape=[shape, shape, shape],
        compiler_params=_params(("parallel",), 32 * 1024 * 1024),
    )(w, g, m, v)


def _rope_tables(s_len):
    rows = s_len // GRID_W
    row = jnp.arange(rows, dtype=F32)
    col = jnp.arange(GRID_W, dtype=F32)

    def lay(dim, lead):
        half = dim // 2
        inv = 1.0 / (ROPE_THETA ** (jnp.arange(0, half, 2, dtype=F32) / half))
        ang_r, ang_c = row[:, None] * inv[None, :], col[:, None] * inv[None, :]
        quarter = half // 2
        tail = LANE - lead - dim

        def place(a, b, at, n, fill=0.0):
            return jnp.concatenate([jnp.full((n, lead), fill if at else 0.0, F32) if lead else jnp.zeros((n, 0), F32),
                                    jnp.zeros((n, at * half), F32), a, b,
                                    jnp.zeros((n, (1 - at) * half + tail), F32)], axis=1)

        cos_r = place(jnp.cos(ang_r), jnp.cos(ang_r), 0, rows)
        cos_c = place(jnp.cos(ang_c), jnp.cos(ang_c), 1, GRID_W, 1.0)
        sin_r = place(-jnp.sin(ang_r), jnp.sin(ang_r), 0, rows)
        sin_c = place(-jnp.sin(ang_c), jnp.sin(ang_c), 1, GRID_W)
        assert cos_r.shape[1] == LANE and quarter * 2 == half
        cos = (cos_r[:, None, :] + cos_c[None, :, :]).reshape(s_len, LANE)
        sin = (sin_r[:, None, :] + sin_c[None, :, :]).reshape(s_len, LANE)
        return cos, sin

    cos_a, sin_a = lay(A_DIM, 0)
    cos_b, sin_b = lay(B_ROPE, B_NOPE)
    return cos_a, sin_a, cos_b, sin_b


def _pad_heads(w, n_heads, dim, axis):
    shape = w.shape[:axis] + (n_heads, dim) + w.shape[axis + 1:]
    pad = [(0, 0)] * len(shape)
    pad[axis + 1] = (0, LANE - dim)
    out = jnp.pad(w.reshape(shape), pad)
    return out.reshape(w.shape[:axis] + (n_heads * LANE,) + w.shape[axis + 1:])


def _unpad_heads(w, n_heads, dim, axis):
    shape = w.shape[:axis] + (n_heads, LANE) + w.shape[axis + 1:]
    out = lax.slice_in_dim(w.reshape(shape), 0, dim, axis=axis + 1)
    return out.reshape(w.shape[:axis] + (n_heads * dim,) + w.shape[axis + 1:])


def _pad_vec(g):
    return jnp.pad(g, ((0, 0), (0, LANE - g.shape[1])))


def _pack(w_in, w_uq, w_ukv, w_out, smalls):
    parts = [w_in.reshape(R_IN, LANE), w_uq.reshape(R_UQ, LANE), w_ukv.reshape(R_UKV, LANE),
             w_out.reshape(R_OUT, LANE), smalls.reshape(R_SMALL, LANE),
             jnp.zeros((R_PACK - R_BIG - R_SMALL, LANE), smalls.dtype)]
    return jnp.concatenate(parts, axis=0)


def _unpack(p):
    r0, r1, r2 = R_IN, R_IN + R_UQ, R_IN + R_UQ + R_UKV
    smalls = p[R_BIG:R_BIG + R_SMALL].reshape(R_SMALL * LANE)
    return (p[:r0].reshape(SH_IN), p[r0:r1].reshape(SH_UQ), p[r1:r2].reshape(SH_UKV),
            p[r2:R_BIG].reshape(SH_OUT), smalls)


def _flat_smalls(vecs, loss=None):
    flat = jnp.concatenate([v.reshape(-1) for v in vecs])
    tail = jnp.zeros((R_SMALL * LANE - LOSS_AT,), F32)
    if loss is not None:
        tail = tail.at[0].set(loss)
    return jnp.concatenate([flat, tail])


def _split_smalls(flat):
    out, at = [], 0
    for n in SMALL_SIZES:
        out.append(flat[at:at + n].reshape(1, n))
        at += n
    return out


def _ext_weights(full):
    r0, r1, r2 = R_IN, R_IN + R_UQ, R_IN + R_UQ + R_UKV
    w_in = full[:, :r0].reshape((N_CHIPS,) + SH_IN).transpose(1, 0, 2).reshape(D_MODEL, N_IN)
    w_uq = full[:, r0:r1].reshape(B_Q_RANK, B_HEADS * B_QK)
    w_ukv = full[:, r1:r2].reshape((N_CHIPS,) + SH_UKV).transpose(1, 0, 2).reshape(B_KV_RANK, B_HEADS * (B_NOPE + B_V))
    w_out = full[:, r2:R_BIG].reshape(D_MODEL, D_MODEL)
    a_w = A_HEADS * A_DIM
    kv_w = A_KV * A_DIM
    o = 0
    secs = []
    for width, heads in ((a_w, A_HEADS), (kv_w, A_KV), (kv_w, A_KV), (a_w, A_HEADS)):
        secs.append(_pad_heads(w_in[:, o:o + width], heads, A_DIM, 1))
        o += width
    a_q, a_k, a_v, a_g = secs
    b_cq = w_in[:, o:o + B_Q_RANK]
    o += B_Q_RANK
    b_ckv = w_in[:, o:o + B_KV_RANK]
    o += B_KV_RANK
    b_kr = jnp.pad(w_in[:, o:o + B_ROPE], ((0, 0), (B_NOPE, LANE - B_QK)))
    o += B_ROPE
    b_g = w_in[:, o:]
    w_in_ext = jnp.concatenate([a_q, a_k, a_v, a_g, b_g, b_cq, b_ckv, b_kr], axis=1)
    w_uq_pad = _pad_heads(w_uq, B_HEADS, B_QK, 1)
    kv3 = w_ukv.reshape(B_KV_RANK, B_HEADS, B_NOPE + B_V)
    w_ukv_ext = jnp.concatenate([_pad_heads(kv3[:, :, :B_NOPE].reshape(B_KV_RANK, B_HEADS * B_NOPE), B_HEADS, B_NOPE, 1),
                                 kv3[:, :, B_NOPE:].reshape(B_KV_RANK, B_HEADS * B_V)], axis=1)
    w_out_ext = jnp.concatenate([_pad_heads(w_out[:a_w], A_HEADS, A_DIM, 0), w_out[a_w:]], axis=0)
    return w_in_ext, w_uq_pad, w_ukv_ext, w_out_ext


def _fold_grads(d_in_ext, d_uq_pad, d_ukv_ext, d_out_ext):
    d_in = jnp.concatenate([
        _unpad_heads(d_in_ext[:, QA0:KA0], A_HEADS, A_DIM, 1),
        _unpad_heads(d_in_ext[:, KA0:VA0], A_KV, A_DIM, 1),
        _unpad_heads(d_in_ext[:, VA0:GA0], A_KV, A_DIM, 1),
        _unpad_heads(d_in_ext[:, GA0:GB0], A_HEADS, A_DIM, 1),
        d_in_ext[:, CQ0:KR0],
        d_in_ext[:, KR0 + B_NOPE:KR0 + B_QK],
        d_in_ext[:, GB0:CQ0]], axis=1)
    d_uq = _unpad_heads(d_uq_pad, B_HEADS, B_QK, 1)
    k3 = _unpad_heads(d_ukv_ext[:, :B_HEADS * LANE], B_HEADS, B_NOPE, 1).reshape(B_KV_RANK, B_HEADS, B_NOPE)
    v3 = d_ukv_ext[:, B_HEADS * LANE:].reshape(B_KV_RANK, B_HEADS, B_V)
    d_ukv = jnp.concatenate([k3, v3], axis=2).reshape(B_KV_RANK, B_HEADS * (B_NOPE + B_V))
    d_out = jnp.concatenate([_unpad_heads(d_out_ext[:A_HEADS * LANE], A_HEADS, A_DIM, 0), d_out_ext[A_HEADS * LANE:]], axis=0)
    return (d_in.reshape(D_MODEL, N_CHIPS, SH_IN[1]).transpose(1, 0, 2),
            d_uq.reshape((N_CHIPS,) + SH_UQ),
            d_ukv.reshape(B_KV_RANK, N_CHIPS, SH_UKV[1]).transpose(1, 0, 2),
            d_out.reshape((N_CHIPS,) + SH_OUT))


def kernel(x, norm_in, w_in, a_q_norm, a_k_norm, b_cq_norm, b_ckv_norm, w_uq, w_ukv, b_q_norm, b_k_norm, w_out, loss_target, m_norm_in, m_w_in, m_a_q_norm, m_a_k_norm, m_b_cq_norm, m_b_ckv_norm, m_w_uq, m_w_ukv, m_b_q_norm, m_b_k_norm, m_w_out, v_norm_in, v_w_in, v_a_q_norm, v_a_k_norm, v_b_cq_norm, v_b_ckv_norm, v_w_uq, v_w_ukv, v_b_q_norm, v_b_k_norm, v_w_out):
    s_len = x.shape[1]
    xs, ts = x[0], loss_target[0]
    tm = min(256, s_len)
    tq, tk_f = min(512, s_len // 2), min(1024, s_len // 2)
    tq_b, tk_b = min(1024, s_len // 2), min(512, s_len)

    small_w = (norm_in, a_q_norm, a_k_norm, b_cq_norm, b_ckv_norm, b_q_norm, b_k_norm)
    w_pack = _pack(w_in[0], w_uq[0], w_ukv[0], w_out[0], _flat_smalls(small_w))
    m_pack = _pack(m_w_in[0], m_w_uq[0], m_w_ukv[0], m_w_out[0],
                   _flat_smalls((m_norm_in, m_a_q_norm, m_a_k_norm, m_b_cq_norm, m_b_ckv_norm, m_b_q_norm, m_b_k_norm)))
    v_pack = _pack(v_w_in[0], v_w_uq[0], v_w_ukv[0], v_w_out[0],
                   _flat_smalls((v_norm_in, v_a_q_norm, v_a_k_norm, v_b_cq_norm, v_b_ckv_norm, v_b_q_norm, v_b_k_norm)))

    w_in_ext, w_uq_pad, w_ukv_ext, w_out_ext = _ext_weights(_gather_weights(w_pack))
    gains = (norm_in, _pad_vec(a_q_norm), _pad_vec(a_k_norm), b_cq_norm, b_ckv_norm, _pad_vec(b_q_norm), _pad_vec(b_k_norm))
    tabs = _rope_tables(s_len)

    (xn, gates, pre, qbpre, kbpre, cq, ckv, qa, ka, va, qb, kb, vb) = _pre(
        xs, tabs, w_in_ext, w_uq_pad, w_ukv_ext, gains, tm)
    o_a, lse_a = _attn_fwd(qa, ka, va, A_GROUP, A_DIM, tq, tk_f, "attn_fwd_a")
    o_b, lse_b = _attn_fwd(qb, kb, vb, 1, B_V, tq, tk_f, "attn_fwd_b")
    y, dh, dgate, do_a, do_b, delta, loss_part = _mid(xs, ts, o_a, o_b, gates, w_out_ext, tm)

    def stat(a):
        return a.reshape(a.shape[0], s_len // tq_b, 1, tq_b)

    dqa, dka, dva = _attn_bwd(qa, ka, va, do_a, stat(lse_a), stat(delta[:A_HEADS]), A_GROUP, tq_b, tk_b, "attn_bwd_a")
    dqb, dkb, dvb = _attn_bwd(qb, kb, vb, do_b, stat(lse_b), stat(delta[A_HEADS:A_HEADS + B_HEADS]), 1, tq_b, tk_b,
                              "attn_bwd_b")
    (grad_x, dproj, dqbpre, dkvb, *d_gains) = _post(
        xs, dh, pre, qbpre, kbpre, dgate, dqa, dka, dva, dqb, dkb, dvb, tabs, w_in_ext, w_uq_pad, w_ukv_ext, gains, tm)

    ts_w = min(512, s_len)
    d_in_ext = _grad_w(xn, dproj, 768, ts_w, "grad_w_in")
    d_out_ext = _grad_w(y, dh, 512, ts_w, "grad_w_out")
    d_uq_pad = _grad_w(cq, dqbpre, 512, ts_w, "grad_w_uq")
    d_ukv_ext = _grad_w(ckv, dkvb, 1024, ts_w, "grad_w_ukv")

    g_in, g_uq, g_ukv, g_out = _fold_grads(d_in_ext, d_uq_pad, d_ukv_ext, d_out_ext)
    d_small = _flat_smalls(
        [d_gains[0], d_gains[1][:, :A_DIM], d_gains[2][:, :A_DIM], d_gains[3], d_gains[4],
         d_gains[5][:, :B_QK], d_gains[6][:, :B_QK]], loss=loss_part[0, 0])
    parts = jnp.stack([_pack(g_in[j], g_uq[j], g_ukv[j], g_out[j], d_small) for j in range(N_CHIPS)])
    g_pack = _reduce_grads(parts)
    d_pack, nm_pack, nv_pack = _adamw(w_pack, g_pack, m_pack, v_pack, 1536)

    def leaves(p):
        p_in, p_uq, p_ukv, p_out, flat = _unpack(p)
        s = _split_smalls(flat)
        return [s[0], p_in[None], s[1], s[2], s[3], s[4], p_uq[None], p_ukv[None], s[5], s[6], p_out[None]], flat

    grads, g_flat = leaves(g_pack)
    loss = g_flat[LOSS_AT]
    return (loss, grad_x[None], *grads, *leaves(d_pack)[0], *leaves(nm_pack)[0], *leaves(nv_pack)[0])
```

```python
import jax
import jax.numpy as jnp
import numpy as np
from jax import lax
from jax.experimental import pallas as pl
from jax.experimental.pallas import tpu as pltpu

F32 = jnp.float32
BF16 = jnp.bfloat16
MESH = pl.DeviceIdType.MESH

D_MODEL = 1024
GRID_W = 64
ROPE_THETA = 10000.0
EPS = 1e-6
A_HEADS, A_KV, A_DIM = 8, 2, 64
A_GROUP = A_HEADS // A_KV
B_HEADS, B_NOPE, B_ROPE, B_V = 4, 64, 32, 128
B_QK = B_NOPE + B_ROPE
B_Q_RANK, B_KV_RANK = 384, 256
N_IN = 2464
SCALE_A = 1.0 / float(np.sqrt(A_DIM))
SCALE_B = 1.0 / float(np.sqrt(B_QK))
LOG2E = float(np.log2(np.e))
LN2 = float(np.log(2.0))
ADAM_LR, ADAM_B1, ADAM_B2, ADAM_EPS, ADAM_WD, ADAM_STEP = 0.001, 0.9, 0.999, 1e-08, 0.01, 10

LANE = 128
VMEM_BYTES = 64 * 1024 * 1024
VMEM_LIMIT = VMEM_BYTES - 8 * 1024 * 1024

QA0 = 0
KA0 = QA0 + A_HEADS * LANE
VA0 = KA0 + A_KV * LANE
GA0 = VA0 + A_KV * LANE
GB0 = GA0 + A_HEADS * LANE
CQ0 = GB0 + B_HEADS * LANE
CKV0 = CQ0 + B_Q_RANK
KR0 = CKV0 + B_KV_RANK
N_EXT = KR0 + LANE
N_GATE = (A_HEADS + B_HEADS) * LANE
DELTA_ROWS = 16
N_PRE = KA0 + A_KV * LANE + B_Q_RANK + B_KV_RANK

N_CHIPS = 4
SH_IN = (D_MODEL, N_IN // N_CHIPS)
SH_UQ = (B_Q_RANK // N_CHIPS, B_HEADS * B_QK)
SH_UKV = (B_KV_RANK, B_HEADS * (B_NOPE + B_V) // N_CHIPS)
SH_OUT = (D_MODEL // N_CHIPS, D_MODEL)
SM_ROWS, SM_W = 16, D_MODEL
SM_IN, SM_AQ, SM_AK, SM_CQ, SM_CKV, SM_BQ, SM_BK, SM_LOSS = range(8)
F32_ROWS, BF16_ROWS = 8, 16


def _pallas(body, **kw):
    return pl.pallas_call(body, **kw)


def _params(sem=None, vmem=None):
    return pltpu.CompilerParams(dimension_semantics=sem, vmem_limit_bytes=vmem)


def _rms_fwd(x, g, n):
    r = lax.rsqrt(jnp.sum(x * x, axis=-1, keepdims=True) * (1.0 / n) + EPS)
    return x * r * g


def _rms_bwd(dy, x, g, n):
    r = lax.rsqrt(jnp.sum(x * x, axis=-1, keepdims=True) * (1.0 / n) + EPS)
    xhat = x * r
    u = dy * g
    dx = r * (u - xhat * (jnp.sum(u * xhat, axis=-1, keepdims=True) * (1.0 / n)))
    return dx, dy * xhat


def _partner(y, half):
    lane = lax.broadcasted_iota(jnp.int32, y.shape, 1)
    first = (lane % (2 * half)) < half
    return jnp.where(first, pltpu.roll(y, LANE - half, 1), pltpu.roll(y, half, 1))


def _rope_fwd(y, cos, sin, half):
    return y * cos + _partner(y, half) * sin


def _rope_bwd(d, cos, sin, half):
    return d * cos - _partner(d, half) * sin


def _nt(a, b):
    return lax.dot_general(a, b, (((1,), (1,)), ((), ())), preferred_element_type=F32)


def _tn(a, b):
    return lax.dot_general(a, b, (((0,), (0,)), ((), ())), preferred_element_type=F32)


def _nn(a, b):
    return jnp.dot(a, b, preferred_element_type=F32)


def _block_rows(i, size):
    if isinstance(i, int):
        return pl.ds(i * size, size)
    return pl.ds(pl.multiple_of(i * size, size), size)


MAX_STATIC_BLOCKS = 8


def _three_stage(n, first, second, third):
    assert n >= 2 and n % 2 == 0
    first(0, 0)
    first(1, 1)
    second(0, 0)
    if n <= MAX_STATIC_BLOCKS:
        for i in range(1, n - 1):
            first(i + 1, (i + 1) % 2)
            second(i, i % 2)
            third(i - 1, (i - 1) % 2)
    else:
        def pair(t, carry):
            i = 2 * t + 1
            first(i + 1, 0)
            second(i, 1)
            third(i - 1, 0)
            first(i + 2, 1)
            second(i + 1, 0)
            third(i, 1)
            return carry

        lax.fori_loop(0, (n - 2) // 2, pair, 0)
    second(n - 1, 1)
    third(n - 2, 0)
    third(n - 1, 1)


def _full(shape):
    return pl.BlockSpec(shape, lambda *_: (0,) * len(shape))


def _gather_weights(shards):
    n = len(shards)
    halves = [w.shape[0] // 2 for w in shards]

    def body(*refs):
        w_refs, out_refs, (send_sems, recv_sems) = refs[:n], refs[n:2 * n], refs[2 * n:]
        x, y, c = lax.axis_index("x"), lax.axis_index("y"), lax.axis_index("c")
        sibling = (x, y, 1 - c)
        chips = [(1 - x, y), (x, 1 - y), (1 - x, 1 - y)]
        me = 2 * x + y

        def copy(a, k, j, hc, to):
            part = out_refs[a].at[j, pl.ds(pl.multiple_of(hc * halves[a], BF16_ROWS), halves[a]), :]
            return pltpu.make_async_remote_copy(
                src_ref=part, dst_ref=part, send_sem=send_sems.at[6 * a + k], recv_sem=recv_sems.at[6 * a + k],
                device_id=to, device_id_type=MESH)

        started = []
        for a in range(n):
            out_refs[a][me] = w_refs[a][...].astype(BF16)
            for k, chip in enumerate(chips):
                started.append(copy(a, k, me, c, (*chip, c)))
                started[-1].start()
        for k, chip in enumerate(chips):
            for a in range(n):
                copy(a, k, 2 * chip[0] + chip[1], c, (*chip, c)).wait_recv()
                started.append(copy(a, 3 + k, 2 * chip[0] + chip[1], c, sibling))
                started[-1].start()
        for k, chip in enumerate(chips):
            for a in range(n):
                copy(a, 3 + k, 2 * chip[0] + chip[1], 1 - c, sibling).wait_recv()
        for cp in started:
            cp.wait_send()

    return _pallas(
        body, name="gather_weights",
        out_shape=[jax.ShapeDtypeStruct((N_CHIPS,) + w.shape, BF16) for w in shards],
        in_specs=[pl.BlockSpec(memory_space=pltpu.VMEM)] * n,
        out_specs=[pl.BlockSpec(memory_space=pltpu.VMEM)] * n,
        scratch_shapes=[pltpu.SemaphoreType.DMA((6 * n,)), pltpu.SemaphoreType.DMA((6 * n,))],
        compiler_params=_params(vmem=32 * 1024 * 1024),
    )(*shards)


def _reduce_grads(parts, small):
    n_big = len(parts)
    n = n_big + 1
    shapes = [p.shape[1:] for p in parts] + [small.shape]
    halves = [sh[0] // 2 for sh in shapes]

    def body(*refs):
        p_refs, out_refs, rec_a, rec_b = refs[:n], refs[n:2 * n], refs[2 * n:3 * n], refs[3 * n:4 * n]
        sa_send, sa_recv, sb_send, sb_recv, sc_send, sc_recv = refs[4 * n:]
        x, y, c = lax.axis_index("x"), lax.axis_index("y"), lax.axis_index("c")
        sibling = (x, y, 1 - c)
        me = 2 * x + y

        def rows(a, hc):
            return pl.ds(pl.multiple_of(hc * halves[a], F32_ROWS), halves[a])

        def partial(a, j, hc):
            return p_refs[a].at[j, rows(a, hc), :] if a < n_big else p_refs[a].at[rows(a, hc), :]

        def copy_a(a, j):
            return pltpu.make_async_remote_copy(
                src_ref=partial(a, j, 1 - c), dst_ref=rec_a[a].at[j],
                send_sem=sa_send.at[N_CHIPS * a + j], recv_sem=sa_recv.at[N_CHIPS * a + j],
                device_id=sibling, device_id_type=MESH)

        def copy_b(a, r):
            j = me ^ r
            k = (N_CHIPS - 1) * a + r - 1
            return pltpu.make_async_remote_copy(
                src_ref=rec_a[a].at[j], dst_ref=rec_b[a].at[r], send_sem=sb_send.at[k], recv_sem=sb_recv.at[k],
                device_id=(j // 2, j % 2, c), device_id_type=MESH)

        def copy_c(a):
            return pltpu.make_async_remote_copy(
                src_ref=out_refs[a].at[rows(a, c), :], dst_ref=out_refs[a].at[rows(a, c), :],
                send_sem=sc_send.at[a], recv_sem=sc_recv.at[a], device_id=sibling, device_id_type=MESH)

        for a in range(n):
            for j in range(N_CHIPS):
                copy_a(a, j).start()
        for r in range(1, N_CHIPS):
            j = me ^ r
            for a in range(n):
                copy_a(a, j).wait_recv()
                rec_a[a][j] = rec_a[a][j] + partial(a, j, c)[...]
                copy_b(a, r).start()
        for a in range(n):
            copy_a(a, me).wait_recv()
            rec_b[a][0] = rec_a[a][me] + partial(a, me, c)[...]
        for a in range(n):
            for r in range(1, N_CHIPS):
                copy_b(a, r).wait_recv()
            total = rec_b[a][me]
            for j in range(1, N_CHIPS):
                total = total + rec_b[a][j ^ me]
            out_refs[a][rows(a, c), :] = total
            copy_c(a).start()
        for a in range(n):
            copy_c(a).wait_recv()
        for a in range(n):
            for j in range(N_CHIPS):
                copy_a(a, j).wait_send()
            for r in range(1, N_CHIPS):
                copy_b(a, r).wait_send()
            copy_c(a).wait_send()

    dma = pltpu.SemaphoreType.DMA
    return _pallas(
        body, name="reduce_grads",
        out_shape=[jax.ShapeDtypeStruct(sh, F32) for sh in shapes],
        in_specs=[pl.BlockSpec(memory_space=pltpu.VMEM)] * n,
        out_specs=[pl.BlockSpec(memory_space=pltpu.VMEM)] * n,
        scratch_shapes=[pltpu.VMEM((N_CHIPS, h) + sh[1:], F32) for h, sh in zip(halves, shapes)] * 2
                       + [dma((N_CHIPS * n,)), dma((N_CHIPS * n,)), dma(((N_CHIPS - 1) * n,)), dma(((N_CHIPS - 1) * n,)),
                          dma((n,)), dma((n,))],
        compiler_params=_params(vmem=VMEM_LIMIT),
    )(*parts, small)


def _pre(x, tabs, w_in_ext, w_uq_pad, w_ukv_ext, gains, tm):
    s_len = x.shape[0]
    nt = s_len // tm

    def body(x_ref, ca_ref, sa_ref, cb_ref, sb_ref, win_ref, wuq_ref, wukv_ref,
             gin_ref, gaq_ref, gak_ref, gcq_ref, gckv_ref, gbq_ref, gbk_ref,
             xn_ref, gates_ref, pre_ref, qbpre_ref, kbpre_ref, cq_ref, ckv_ref,
             qa_ref, ka_ref, va_ref, qb_ref, kb_ref, vb_ref, proj):
        xn = _rms_fwd(x_ref[...], gin_ref[...], D_MODEL).astype(BF16)
        xn_ref[...] = xn
        proj[...] = _nn(xn, win_ref[...])
        gates_ref[...] = proj[:, GA0:GA0 + N_GATE]
        pre_ref[:, 0:VA0] = proj[:, 0:VA0]
        pre_ref[:, VA0:N_PRE] = proj[:, CQ0:KR0]
        ca, sa, cb, sb = ca_ref[...], sa_ref[...], cb_ref[...], sb_ref[...]
        lane = lax.broadcasted_iota(jnp.int32, (tm, LANE), 1)
        for h in range(A_HEADS):
            yq = _rms_fwd(proj[:, QA0 + LANE * h:QA0 + LANE * (h + 1)], gaq_ref[...], A_DIM)
            qa_ref[h] = (_rope_fwd(yq, ca, sa, A_DIM // 4) * (SCALE_A * LOG2E)).astype(BF16)
        for h in range(A_KV):
            yk = _rms_fwd(proj[:, KA0 + LANE * h:KA0 + LANE * (h + 1)], gak_ref[...], A_DIM)
            ka_ref[h] = _rope_fwd(yk, ca, sa, A_DIM // 4).astype(BF16)
            va_ref[h] = jnp.where(lane == A_DIM, 1.0, proj[:, VA0 + LANE * h:VA0 + LANE * (h + 1)]).astype(BF16)
        cq = _rms_fwd(proj[:, CQ0:CQ0 + B_Q_RANK], gcq_ref[...], B_Q_RANK).astype(BF16)
        cq_ref[...] = cq
        qbpre_ref[...] = _nn(cq, wuq_ref[...])
        ckv = _rms_fwd(proj[:, CKV0:CKV0 + B_KV_RANK], gckv_ref[...], B_KV_RANK).astype(BF16)
        ckv_ref[...] = ckv
        kvb = _nn(ckv, wukv_ref[...])
        kr = proj[:, KR0:KR0 + LANE]
        for h in range(B_HEADS):
            yq = _rms_fwd(qbpre_ref[:, LANE * h:LANE * (h + 1)], gbq_ref[...], B_QK)
            qb_ref[h] = (_rope_fwd(yq, cb, sb, B_ROPE // 4) * (SCALE_B * LOG2E)).astype(BF16)
            kp = kvb[:, LANE * h:LANE * (h + 1)] + kr
            kbpre_ref[:, LANE * h:LANE * (h + 1)] = kp
            kb_ref[h] = _rope_fwd(_rms_fwd(kp, gbk_ref[...], B_QK), cb, sb, B_ROPE // 4).astype(BF16)
            vb_ref[h, :, 0:LANE] = kvb[:, B_HEADS * LANE + LANE * h:B_HEADS * LANE + LANE * (h + 1)].astype(BF16)
            vb_ref[h, :, LANE:2 * LANE] = jnp.where(lane == 0, 1.0, 0.0).astype(BF16)

    row = lambda w: pl.BlockSpec((tm, w), lambda i: (i, 0))
    heads = lambda n: pl.BlockSpec((n, tm, LANE), lambda i: (0, i, 0))
    hs = lambda n: jax.ShapeDtypeStruct((n, s_len, LANE), BF16)
    return _pallas(
        body, name="pre", grid=(nt,),
        in_specs=[row(D_MODEL), row(LANE), row(LANE), row(LANE), row(LANE),
                  _full(w_in_ext.shape), _full(w_uq_pad.shape), _full(w_ukv_ext.shape)]
                 + [_full(g.shape) for g in gains],
        out_specs=[row(D_MODEL), row(N_GATE), row(N_PRE), row(B_HEADS * LANE), row(B_HEADS * LANE),
                   row(B_Q_RANK), row(B_KV_RANK),
                   heads(A_HEADS), heads(A_KV), heads(A_KV), heads(B_HEADS), heads(B_HEADS),
                   pl.BlockSpec((B_HEADS, tm, 2 * LANE), lambda i: (0, i, 0))],
        out_shape=[jax.ShapeDtypeStruct((s_len, D_MODEL), BF16), jax.ShapeDtypeStruct((s_len, N_GATE), F32),
                   jax.ShapeDtypeStruct((s_len, N_PRE), F32), jax.ShapeDtypeStruct((s_len, B_HEADS * LANE), F32),
                   jax.ShapeDtypeStruct((s_len, B_HEADS * LANE), F32),
                   jax.ShapeDtypeStruct((s_len, B_Q_RANK), BF16), jax.ShapeDtypeStruct((s_len, B_KV_RANK), BF16),
                   hs(A_HEADS), hs(A_KV), hs(A_KV), hs(B_HEADS), hs(B_HEADS),
                   jax.ShapeDtypeStruct((B_HEADS, s_len, 2 * LANE), BF16)],
        scratch_shapes=[pltpu.VMEM((tm, N_EXT), F32)],
        compiler_params=_params(("parallel",), VMEM_LIMIT),
    )(x, *tabs, w_in_ext, w_uq_pad, w_ukv_ext, *gains)


def _attn_fwd(q, k, v, group, l_col, tq, tk, name):
    n_heads, s_len, _ = q.shape
    v_w = v.shape[2]
    nk = s_len // tk
    assert nk >= 2 and nk % 2 == 0

    def body(q_ref, k_ref, v_ref, o_ref, lse_ref, s_buf, p_buf, a_buf, m_ref, acc_ref):
        def rows(j):
            return _block_rows(j, tk)

        def scores(j, slot):
            s_buf[slot] = _nt(q_ref[...], k_ref[rows(j), :])

        def softmax(j, slot):
            s = s_buf[slot]
            m_old = m_ref[...]
            m_new = jnp.maximum(m_old, jnp.max(s, axis=-1, keepdims=True))
            m_ref[...] = m_new
            a_buf[slot] = jnp.exp2(m_old - m_new)
            p_buf[slot] = jnp.exp2(s - jnp.tile(m_new, (1, tk // LANE))).astype(BF16)

        def values(j, slot):
            pv = _nn(p_buf[slot], v_ref[rows(j), :])
            for c in range(0, v_w, LANE):
                acc_ref[:, c:c + LANE] = a_buf[slot] * acc_ref[:, c:c + LANE] + pv[:, c:c + LANE]

        m_ref[...] = jnp.full(m_ref.shape, -1e30, F32)
        acc_ref[...] = jnp.zeros(acc_ref.shape, F32)
        _three_stage(nk, scores, softmax, values)
        l = acc_ref[:, l_col:l_col + 1]
        o = acc_ref[:, 0:LANE] * (1.0 / l)
        if l_col < LANE:
            lane = lax.broadcasted_iota(jnp.int32, o.shape, 1)
            o = jnp.where(lane == l_col, 0.0, o)
        o_ref[...] = o
        lse_ref[...] = jnp.transpose(m_ref[...] + jnp.log2(jnp.broadcast_to(l, (tq, LANE))))[0:1, :]

    return _pallas(
        body, name=name, grid=(n_heads, s_len // tq),
        in_specs=[pl.BlockSpec((None, tq, LANE), lambda h, i: (h, i, 0)),
                  pl.BlockSpec((None, s_len, LANE), lambda h, i: (h // group, 0, 0)),
                  pl.BlockSpec((None, s_len, v_w), lambda h, i: (h // group, 0, 0))],
        out_specs=[pl.BlockSpec((None, tq, LANE), lambda h, i: (h, i, 0)),
                   pl.BlockSpec((None, None, 1, tq), lambda h, i: (h, i, 0, 0))],
        out_shape=[jax.ShapeDtypeStruct((n_heads, s_len, LANE), F32),
                   jax.ShapeDtypeStruct((n_heads, s_len // tq, 1, tq), F32)],
        scratch_shapes=[pltpu.VMEM((2, tq, tk), F32), pltpu.VMEM((2, tq, tk), BF16), pltpu.VMEM((2, tq, LANE), F32),
                        pltpu.VMEM((tq, LANE), F32), pltpu.VMEM((tq, v_w), F32)],
        compiler_params=_params(("parallel", "parallel"), 48 * 1024 * 1024),
    )(q, k, v)


def _mid(x, target, o_a, o_b, gates, w_out_ext, tm):
    s_len = x.shape[0]
    nt = s_len // tm
    n_heads = A_HEADS + B_HEADS

    def body(x_ref, t_ref, oa_ref, ob_ref, g_ref, w_ref,
             y_ref, dh_ref, dgate_ref, doa_ref, dob_ref, delta_ref, loss_ref, silu_scr, dsilu_scr):
        @pl.when(pl.program_id(0) == 0)
        def _():
            loss_ref[...] = jnp.zeros_like(loss_ref)

        def o_of(h):
            return oa_ref[h] if h < A_HEADS else ob_ref[h - A_HEADS]

        for h in range(n_heads):
            cols = slice(LANE * h, LANE * (h + 1))
            g = g_ref[:, cols]
            sig = 1.0 / (1.0 + jnp.exp(-g))
            silu = g * sig
            silu_scr[:, cols] = silu
            dsilu_scr[:, cols] = sig * (1.0 + g * (1.0 - sig))
            y_ref[:, cols] = (o_of(h) * silu).astype(BF16)
        err = x_ref[...] + _nn(y_ref[...], w_ref[...]) - t_ref[...]
        sq = jnp.sum(jnp.sum(err * err, axis=-1, keepdims=True), axis=0, keepdims=True)
        loss_ref[...] += jnp.broadcast_to(sq * (0.5 / D_MODEL), loss_ref.shape)
        dh = err * (1.0 / D_MODEL)
        dh_ref[...] = dh
        dy = _nt(dh.astype(BF16), w_ref[...])
        lane = lax.broadcasted_iota(jnp.int32, (tm, LANE), 1)
        delta = jnp.zeros((tm, LANE), F32)
        for h in range(n_heads):
            cols = slice(LANE * h, LANE * (h + 1))
            dyh = dy[:, cols]
            oh = o_of(h)
            do = dyh * silu_scr[:, cols]
            dgate_ref[:, cols] = (dyh * oh * dsilu_scr[:, cols]).astype(BF16)
            delta = jnp.where(lane == h, jnp.sum(do * oh, axis=-1, keepdims=True), delta)
            if h < A_HEADS:
                doa_ref[h] = do.astype(BF16)
            else:
                dob_ref[h - A_HEADS] = do.astype(BF16)
        delta_ref[...] = jnp.transpose(delta)[0:DELTA_ROWS, :]

    row = lambda w: pl.BlockSpec((tm, w), lambda i: (i, 0))
    heads = lambda n, w=LANE: pl.BlockSpec((n, tm, w), lambda i: (0, i, 0))
    return _pallas(
        body, name="mid", grid=(nt,),
        in_specs=[row(D_MODEL), row(D_MODEL), heads(A_HEADS), heads(B_HEADS), row(N_GATE), _full(w_out_ext.shape)],
        out_specs=[row(N_GATE), row(D_MODEL), row(N_GATE), heads(A_HEADS), heads(B_HEADS),
                   pl.BlockSpec((DELTA_ROWS, tm), lambda i: (0, i)),
                   _full((8, LANE))],
        out_shape=[jax.ShapeDtypeStruct((s_len, N_GATE), BF16), jax.ShapeDtypeStruct((s_len, D_MODEL), F32),
                   jax.ShapeDtypeStruct((s_len, N_GATE), BF16),
                   jax.ShapeDtypeStruct((A_HEADS, s_len, LANE), BF16), jax.ShapeDtypeStruct((B_HEADS, s_len, LANE), BF16),
                   jax.ShapeDtypeStruct((DELTA_ROWS, s_len), F32), jax.ShapeDtypeStruct((8, LANE), F32)],
        scratch_shapes=[pltpu.VMEM((tm, N_GATE), F32), pltpu.VMEM((tm, N_GATE), F32)],
        compiler_params=_params(("arbitrary",), VMEM_LIMIT),
    )(x, target, o_a, o_b, gates, w_out_ext)


def _attn_bwd(q, k, v, do, lse, delta, group, tq, tk, name):
    n_heads, s_len, _ = q.shape
    nq = s_len // tq
    assert nq >= 2 and nq % 2 == 0

    def body(q_ref, do_ref, lse_ref, delta_ref, k_ref, v_ref, dq_ref, dk_ref, dv_ref, s_buf, dp_buf, p_buf, ds_buf):
        @pl.when(pl.program_id(1) == 0)
        def _():
            dq_ref[...] = jnp.zeros_like(dq_ref)

        dk_ref[...] = jnp.zeros_like(dk_ref)
        dv_ref[...] = jnp.zeros_like(dv_ref)

        def rows(i):
            return _block_rows(i, tq)

        def scores(i, slot):
            s_buf[slot] = _nt(k_ref[...], q_ref[rows(i), :])
            dp_buf[slot] = _nt(v_ref[...], do_ref[rows(i), :])

        def elementwise(i, slot):
            p = jnp.exp2(s_buf[slot] - lse_ref[i])
            p_buf[slot] = p.astype(BF16)
            ds_buf[slot] = (p * (dp_buf[slot] - delta_ref[i])).astype(BF16)

        def grads(i, slot):
            dv_ref[...] += _nn(p_buf[slot], do_ref[rows(i), :])
            dk_ref[...] += _nn(ds_buf[slot], q_ref[rows(i), :])
            dq_ref[rows(i), :] += _tn(ds_buf[slot], k_ref[...])

        _three_stage(nq, scores, elementwise, grads)

    whole = lambda: pl.BlockSpec((None, s_len, LANE), lambda h, j: (h, 0, 0))
    stat = lambda: pl.BlockSpec((None, nq, 1, tq), lambda h, j: (h, 0, 0, 0))
    kvb = lambda: pl.BlockSpec((None, tk, LANE), lambda h, j: (h // group, j, 0))
    outb = lambda: pl.BlockSpec((None, tk, LANE), lambda h, j: (h, j, 0))
    shape = jax.ShapeDtypeStruct((n_heads, s_len, LANE), F32)
    return _pallas(
        body, name=name, grid=(n_heads, s_len // tk),
        in_specs=[whole(), whole(), stat(), stat(), kvb(), kvb()],
        out_specs=[whole(), outb(), outb()],
        out_shape=[shape, shape, shape],
        scratch_shapes=[pltpu.VMEM((2, tk, tq), F32), pltpu.VMEM((2, tk, tq), F32),
                        pltpu.VMEM((2, tk, tq), BF16), pltpu.VMEM((2, tk, tq), BF16)],
        compiler_params=_params(("parallel", "arbitrary"), 48 * 1024 * 1024),
    )(q, do, lse, delta, k, v)


def _post(x, dh, pre, qbpre, kbpre, dgate, dqa, dka, dva, dqb, dkb, dvb, loss_part, tabs,
          w_in_ext, w_uq_pad, w_ukv_ext, gains, tm):
    s_len = x.shape[0]
    nt = s_len // tm

    def body(x_ref, dh_ref, pre_ref, qbpre_ref, kbpre_ref, dgate_ref,
             dqa_ref, dka_ref, dva_ref, dqb_ref, dkb_ref, dvb_ref, loss_ref,
             ca_ref, sa_ref, cb_ref, sb_ref, win_ref, wuq_ref, wukv_ref,
             gin_ref, gaq_ref, gak_ref, gcq_ref, gckv_ref, gbq_ref, gbk_ref,
             gx_ref, dproj_ref, dqbpre_ref, dkvb_ref, dsm_ref):
        @pl.when(pl.program_id(0) == 0)
        def _():
            dsm_ref[...] = jnp.zeros_like(dsm_ref)
            dsm_ref[SM_LOSS:SM_LOSS + 1, 0:LANE] = loss_ref[0:1, :]

        def add_small(r, dg):
            dsm_ref[r:r + 1, 0:dg.shape[1]] += dg

        def tok_sum(a):
            return jnp.sum(a, axis=0, keepdims=True)

        ca, sa, cb, sb = ca_ref[...], sa_ref[...], cb_ref[...], sb_ref[...]
        lane = lax.broadcasted_iota(jnp.int32, (tm, LANE), 1)
        dg = jnp.zeros((1, LANE), F32)
        for h in range(A_HEADS):
            dn = _rope_bwd(dqa_ref[h] * SCALE_A, ca, sa, A_DIM // 4)
            dx, dgr = _rms_bwd(dn, pre_ref[:, QA0 + LANE * h:QA0 + LANE * (h + 1)], gaq_ref[...], A_DIM)
            dproj_ref[:, QA0 + LANE * h:QA0 + LANE * (h + 1)] = dx.astype(BF16)
            dg = dg + tok_sum(dgr)
        add_small(SM_AQ, dg)
        dg = jnp.zeros((1, LANE), F32)
        for h in range(A_KV):
            dk = dka_ref[A_GROUP * h]
            dv = dva_ref[A_GROUP * h]
            for g in range(1, A_GROUP):
                dk = dk + dka_ref[A_GROUP * h + g]
                dv = dv + dva_ref[A_GROUP * h + g]
            dn = _rope_bwd(dk * LN2, ca, sa, A_DIM // 4)
            dx, dgr = _rms_bwd(dn, pre_ref[:, KA0 + LANE * h:KA0 + LANE * (h + 1)], gak_ref[...], A_DIM)
            dproj_ref[:, KA0 + LANE * h:KA0 + LANE * (h + 1)] = dx.astype(BF16)
            dproj_ref[:, VA0 + LANE * h:VA0 + LANE * (h + 1)] = dv.astype(BF16)
            dg = dg + tok_sum(dgr)
        add_small(SM_AK, dg)
        dproj_ref[:, GA0:GA0 + N_GATE] = dgate_ref[...]
        dg = jnp.zeros((1, LANE), F32)
        for h in range(B_HEADS):
            cols = slice(LANE * h, LANE * (h + 1))
            dn = _rope_bwd(dqb_ref[h] * SCALE_B, cb, sb, B_ROPE // 4)
            dx, dgr = _rms_bwd(dn, qbpre_ref[:, cols], gbq_ref[...], B_QK)
            dqbpre_ref[:, cols] = dx.astype(BF16)
            dg = dg + tok_sum(dgr)
        add_small(SM_BQ, dg)
        dcq = _nt(dqbpre_ref[...], wuq_ref[...])
        dx, dgr = _rms_bwd(dcq, pre_ref[:, VA0:VA0 + B_Q_RANK], gcq_ref[...], B_Q_RANK)
        dproj_ref[:, CQ0:CQ0 + B_Q_RANK] = dx.astype(BF16)
        add_small(SM_CQ, tok_sum(dgr))
        dg = jnp.zeros((1, LANE), F32)
        dkr = jnp.zeros((tm, LANE), F32)
        for h in range(B_HEADS):
            cols = slice(LANE * h, LANE * (h + 1))
            dn = _rope_bwd(dkb_ref[h] * LN2, cb, sb, B_ROPE // 4)
            dx, dgr = _rms_bwd(dn, kbpre_ref[:, cols], gbk_ref[...], B_QK)
            dkvb_ref[:, cols] = jnp.where(lane < B_NOPE, dx, 0.0).astype(BF16)
            dkvb_ref[:, B_HEADS * LANE + LANE * h:B_HEADS * LANE + LANE * (h + 1)] = dvb_ref[h].astype(BF16)
            dkr = dkr + dx
            dg = dg + tok_sum(dgr)
        add_small(SM_BK, dg)
        dproj_ref[:, KR0:KR0 + LANE] = jnp.where((lane >= B_NOPE) & (lane < B_QK), dkr, 0.0).astype(BF16)
        dckv = _nt(dkvb_ref[...], wukv_ref[...])
        dx, dgr = _rms_bwd(dckv, pre_ref[:, VA0 + B_Q_RANK:N_PRE], gckv_ref[...], B_KV_RANK)
        dproj_ref[:, CKV0:CKV0 + B_KV_RANK] = dx.astype(BF16)
        add_small(SM_CKV, tok_sum(dgr))
        dxn = _nt(dproj_ref[...], win_ref[...])
        dx, dgr = _rms_bwd(dxn, x_ref[...], gin_ref[...], D_MODEL)
        gx_ref[...] = dh_ref[...] + dx
        add_small(SM_IN, tok_sum(dgr))

    row = lambda w: pl.BlockSpec((tm, w), lambda i: (i, 0))
    heads = lambda n: pl.BlockSpec((n, tm, LANE), lambda i: (0, i, 0))
    return _pallas(
        body, name="post", grid=(nt,),
        in_specs=[row(D_MODEL), row(D_MODEL), row(N_PRE), row(B_HEADS * LANE), row(B_HEADS * LANE), row(N_GATE),
                  heads(A_HEADS), heads(A_HEADS), heads(A_HEADS), heads(B_HEADS), heads(B_HEADS), heads(B_HEADS),
                  _full(loss_part.shape), row(LANE), row(LANE), row(LANE), row(LANE),
                  _full(w_in_ext.shape), _full(w_uq_pad.shape), _full(w_ukv_ext.shape)]
                 + [_full(g.shape) for g in gains],
        out_specs=[row(D_MODEL), row(N_EXT), row(B_HEADS * LANE), row(2 * B_HEADS * LANE), _full((SM_ROWS, SM_W))],
        out_shape=[jax.ShapeDtypeStruct((s_len, D_MODEL), F32), jax.ShapeDtypeStruct((s_len, N_EXT), BF16),
                   jax.ShapeDtypeStruct((s_len, B_HEADS * LANE), BF16),
                   jax.ShapeDtypeStruct((s_len, 2 * B_HEADS * LANE), BF16),
                   jax.ShapeDtypeStruct((SM_ROWS, SM_W), F32)],
        compiler_params=_params(("arbitrary",), VMEM_LIMIT),
    )(x, dh, pre, qbpre, kbpre, dgate, dqa, dka, dva, dqb, dkb, dvb, loss_part, *tabs,
      w_in_ext, w_uq_pad, w_ukv_ext, *gains)


def _grad_w(a, b, tn, ts, name):
    s_len, m = a.shape
    n = b.shape[1]

    def body(a_ref, b_ref, o_ref):
        @pl.when(pl.program_id(1) == 0)
        def _():
            o_ref[...] = jnp.zeros_like(o_ref)

        o_ref[...] += _tn(a_ref[...].astype(BF16), b_ref[...].astype(BF16))

    return _pallas(
        body, name=name, grid=(n // tn, s_len // ts),
        in_specs=[pl.BlockSpec((ts, m), lambda j, t: (t, 0)), pl.BlockSpec((ts, tn), lambda j, t: (t, j))],
        out_specs=pl.BlockSpec((m, tn), lambda j, t: (0, j)),
        out_shape=jax.ShapeDtypeStruct((m, n), F32),
        compiler_params=_params(("parallel", "arbitrary"), 48 * 1024 * 1024),
    )(a, b)


def _adam_math(w, g, m, v):
    nm = ADAM_B1 * m + (1.0 - ADAM_B1) * g
    nv = ADAM_B2 * v + (1.0 - ADAM_B2) * (g * g)
    m_hat = nm / (1.0 - ADAM_B1 ** ADAM_STEP)
    v_hat = nv / (1.0 - ADAM_B2 ** ADAM_STEP)
    return -ADAM_LR * (m_hat / (jnp.sqrt(v_hat) + ADAM_EPS) + ADAM_WD * w), nm, nv


def _adamw_rows(w, g, m, v, tr):
    rows, cols = w.shape

    def body(w_ref, g_ref, m_ref, v_ref, d_ref, nm_ref, nv_ref):
        d_ref[...], nm_ref[...], nv_ref[...] = _adam_math(w_ref[...], g_ref[...], m_ref[...], v_ref[...])

    blk = pl.BlockSpec((tr, cols), lambda i: (i, 0))
    shape = jax.ShapeDtypeStruct((rows, cols), F32)
    return _pallas(
        body, name="adamw_w_in", grid=(rows // tr,),
        in_specs=[blk] * 4, out_specs=[blk] * 3, out_shape=[shape] * 3,
        compiler_params=_params(("parallel",), 32 * 1024 * 1024),
    )(w, g, m, v)


def _adamw_rest(bigs, smalls, g_small):
    nb, ns = len(bigs), len(smalls)

    def body(*refs):
        ins, outs = refs[:4 * nb + 3 * ns + 1], refs[4 * nb + 3 * ns + 1:]
        for i in range(nb):
            w_ref, g_ref, m_ref, v_ref = ins[4 * i:4 * i + 4]
            d_ref, nm_ref, nv_ref = outs[3 * i:3 * i + 3]
            d_ref[...], nm_ref[...], nv_ref[...] = _adam_math(w_ref[...], g_ref[...], m_ref[...], v_ref[...])
        gs_ref = ins[-1]
        for i in range(ns):
            w_ref, m_ref, v_ref = ins[4 * nb + 3 * i:4 * nb + 3 * i + 3]
            g_ref, d_ref, nm_ref, nv_ref = outs[3 * nb + 4 * i:3 * nb + 4 * i + 4]
            g = gs_ref[i:i + 1, 0:w_ref.shape[1]]
            g_ref[...] = g
            d_ref[...], nm_ref[...], nv_ref[...] = _adam_math(w_ref[...], g, m_ref[...], v_ref[...])

    flat_in = [a for quad in bigs for a in quad] + [a for tri in smalls for a in tri] + [g_small]
    out_shape = ([jax.ShapeDtypeStruct(q[0].shape, F32) for q in bigs for _ in range(3)]
                 + [jax.ShapeDtypeStruct(t[0].shape, F32) for t in smalls for _ in range(4)])
    return _pallas(
        body, name="adamw_rest",
        in_specs=[pl.BlockSpec(memory_space=pltpu.VMEM)] * len(flat_in),
        out_specs=[pl.BlockSpec(memory_space=pltpu.VMEM)] * len(out_shape),
        out_shape=out_shape,
        compiler_params=_params(vmem=32 * 1024 * 1024),
    )(*flat_in)


def _rope_tables(s_len):
    rows = s_len // GRID_W
    row = jnp.arange(rows, dtype=F32)
    col = jnp.arange(GRID_W, dtype=F32)

    def lay(dim, lead):
        half = dim // 2
        inv = 1.0 / (ROPE_THETA ** (jnp.arange(0, half, 2, dtype=F32) / half))
        ang_r, ang_c = row[:, None] * inv[None, :], col[:, None] * inv[None, :]
        tail = LANE - lead - dim

        def place(a, b, at, n, fill=0.0):
            return jnp.concatenate([jnp.full((n, lead), fill if at else 0.0, F32) if lead else jnp.zeros((n, 0), F32),
                                    jnp.zeros((n, at * half), F32), a, b,
                                    jnp.zeros((n, (1 - at) * half + tail), F32)], axis=1)

        cos_r = place(jnp.cos(ang_r), jnp.cos(ang_r), 0, rows)
        cos_c = place(jnp.cos(ang_c), jnp.cos(ang_c), 1, GRID_W, 1.0)
        sin_r = place(-jnp.sin(ang_r), jnp.sin(ang_r), 0, rows)
        sin_c = place(-jnp.sin(ang_c), jnp.sin(ang_c), 1, GRID_W)
        assert cos_r.shape[1] == LANE
        cos = (cos_r[:, None, :] + cos_c[None, :, :]).reshape(s_len, LANE)
        sin = (sin_r[:, None, :] + sin_c[None, :, :]).reshape(s_len, LANE)
        return cos, sin

    cos_a, sin_a = lay(A_DIM, 0)
    cos_b, sin_b = lay(B_ROPE, B_NOPE)
    return cos_a, sin_a, cos_b, sin_b


def _pad_heads(w, n_heads, dim, axis):
    shape = w.shape[:axis] + (n_heads, dim) + w.shape[axis + 1:]
    pad = [(0, 0)] * len(shape)
    pad[axis + 1] = (0, LANE - dim)
    out = jnp.pad(w.reshape(shape), pad)
    return out.reshape(w.shape[:axis] + (n_heads * LANE,) + w.shape[axis + 1:])


def _unpad_heads(w, n_heads, dim, axis):
    shape = w.shape[:axis] + (n_heads, LANE) + w.shape[axis + 1:]
    out = lax.slice_in_dim(w.reshape(shape), 0, dim, axis=axis + 1)
    return out.reshape(w.shape[:axis] + (n_heads * dim,) + w.shape[axis + 1:])


def _pad_vec(g):
    return jnp.pad(g, ((0, 0), (0, LANE - g.shape[1])))


def _ext_weights(g_in, g_uq, g_ukv, g_out):
    w_in = g_in.transpose(1, 0, 2).reshape(D_MODEL, N_IN)
    w_uq = g_uq.reshape(B_Q_RANK, B_HEADS * B_QK)
    w_ukv = g_ukv.transpose(1, 0, 2).reshape(B_KV_RANK, B_HEADS * (B_NOPE + B_V))
    w_out = g_out.reshape(D_MODEL, D_MODEL)
    a_w = A_HEADS * A_DIM
    kv_w = A_KV * A_DIM
    o = 0
    secs = []
    for width, heads in ((a_w, A_HEADS), (kv_w, A_KV), (kv_w, A_KV), (a_w, A_HEADS)):
        secs.append(_pad_heads(w_in[:, o:o + width], heads, A_DIM, 1))
        o += width
    a_q, a_k, a_v, a_g = secs
    b_cq = w_in[:, o:o + B_Q_RANK]
    o += B_Q_RANK
    b_ckv = w_in[:, o:o + B_KV_RANK]
    o += B_KV_RANK
    b_kr = jnp.pad(w_in[:, o:o + B_ROPE], ((0, 0), (B_NOPE, LANE - B_QK)))
    o += B_ROPE
    b_g = w_in[:, o:]
    w_in_ext = jnp.concatenate([a_q, a_k, a_v, a_g, b_g, b_cq, b_ckv, b_kr], axis=1)
    w_uq_pad = _pad_heads(w_uq, B_HEADS, B_QK, 1)
    kv3 = w_ukv.reshape(B_KV_RANK, B_HEADS, B_NOPE + B_V)
    w_ukv_ext = jnp.concatenate([_pad_heads(kv3[:, :, :B_NOPE].reshape(B_KV_RANK, B_HEADS * B_NOPE), B_HEADS, B_NOPE, 1),
                                 kv3[:, :, B_NOPE:].reshape(B_KV_RANK, B_HEADS * B_V)], axis=1)
    w_out_ext = jnp.concatenate([_pad_heads(w_out[:a_w], A_HEADS, A_DIM, 0), w_out[a_w:]], axis=0)
    return w_in_ext, w_uq_pad, w_ukv_ext, w_out_ext


def _fold_grads(d_in_ext, d_uq_pad, d_ukv_ext, d_out_ext):
    d_in = jnp.concatenate([
        _unpad_heads(d_in_ext[:, QA0:KA0], A_HEADS, A_DIM, 1),
        _unpad_heads(d_in_ext[:, KA0:VA0], A_KV, A_DIM, 1),
        _unpad_heads(d_in_ext[:, VA0:GA0], A_KV, A_DIM, 1),
        _unpad_heads(d_in_ext[:, GA0:GB0], A_HEADS, A_DIM, 1),
        d_in_ext[:, CQ0:KR0],
        d_in_ext[:, KR0 + B_NOPE:KR0 + B_QK],
        d_in_ext[:, GB0:CQ0]], axis=1)
    d_uq = _unpad_heads(d_uq_pad, B_HEADS, B_QK, 1)
    k3 = _unpad_heads(d_ukv_ext[:, :B_HEADS * LANE], B_HEADS, B_NOPE, 1).reshape(B_KV_RANK, B_HEADS, B_NOPE)
    v3 = d_ukv_ext[:, B_HEADS * LANE:].reshape(B_KV_RANK, B_HEADS, B_V)
    d_ukv = jnp.concatenate([k3, v3], axis=2).reshape(B_KV_RANK, B_HEADS * (B_NOPE + B_V))
    d_out = jnp.concatenate([_unpad_heads(d_out_ext[:A_HEADS * LANE], A_HEADS, A_DIM, 0), d_out_ext[A_HEADS * LANE:]], axis=0)
    return (d_in.reshape(D_MODEL, N_CHIPS, SH_IN[1]).transpose(1, 0, 2),
            d_uq.reshape((N_CHIPS,) + SH_UQ),
            d_ukv.reshape(B_KV_RANK, N_CHIPS, SH_UKV[1]).transpose(1, 0, 2),
            d_out.reshape((N_CHIPS,) + SH_OUT))


def kernel(x, norm_in, w_in, a_q_norm, a_k_norm, b_cq_norm, b_ckv_norm, w_uq, w_ukv, b_q_norm, b_k_norm, w_out, loss_target, m_norm_in, m_w_in, m_a_q_norm, m_a_k_norm, m_b_cq_norm, m_b_ckv_norm, m_w_uq, m_w_ukv, m_b_q_norm, m_b_k_norm, m_w_out, v_norm_in, v_w_in, v_a_q_norm, v_a_k_norm, v_b_cq_norm, v_b_ckv_norm, v_w_uq, v_w_ukv, v_b_q_norm, v_b_k_norm, v_w_out):
    s_len = x.shape[1]
    xs, ts = x[0], loss_target[0]
    tm = min(256, s_len)
    tq, tk_f = min(512, s_len // 2), min(1024, s_len // 2)
    tq_b, tk_b = min(1024, s_len // 2), min(512, s_len)

    w_in_ext, w_uq_pad, w_ukv_ext, w_out_ext = _ext_weights(*_gather_weights((w_in[0], w_uq[0], w_ukv[0], w_out[0])))
    gains = (norm_in, _pad_vec(a_q_norm), _pad_vec(a_k_norm), b_cq_norm, b_ckv_norm, _pad_vec(b_q_norm), _pad_vec(b_k_norm))
    tabs = _rope_tables(s_len)

    (xn, gates, pre, qbpre, kbpre, cq, ckv, qa, ka, va, qb, kb, vb) = _pre(
        xs, tabs, w_in_ext, w_uq_pad, w_ukv_ext, gains, tm)
    o_a, lse_a = _attn_fwd(qa, ka, va, A_GROUP, A_DIM, tq, tk_f, "attn_fwd_a")
    o_b, lse_b = _attn_fwd(qb, kb, vb, 1, B_V, tq, tk_f, "attn_fwd_b")
    y, dh, dgate, do_a, do_b, delta, loss_part = _mid(xs, ts, o_a, o_b, gates, w_out_ext, tm)

    def stat(a):
        return a.reshape(a.shape[0], s_len // tq_b, 1, tq_b)

    dqa, dka, dva = _attn_bwd(qa, ka, va, do_a, stat(lse_a), stat(delta[:A_HEADS]), A_GROUP, tq_b, tk_b, "attn_bwd_a")
    dqb, dkb, dvb = _attn_bwd(qb, kb, vb, do_b, stat(lse_b), stat(delta[A_HEADS:A_HEADS + B_HEADS]), 1, tq_b, tk_b,
                              "attn_bwd_b")
    grad_x, dproj, dqbpre, dkvb, d_small = _post(
        xs, dh, pre, qbpre, kbpre, dgate, dqa, dka, dva, dqb, dkb, dvb, loss_part, tabs,
        w_in_ext, w_uq_pad, w_ukv_ext, gains, tm)

    ts_w = min(512, s_len)
    d_in_ext = _grad_w(xn, dproj, 768, ts_w, "grad_w_in")
    d_out_ext = _grad_w(y, dh, 512, ts_w, "grad_w_out")
    d_uq_pad = _grad_w(cq, dqbpre, 512, ts_w, "grad_w_uq")
    d_ukv_ext = _grad_w(ckv, dkvb, 1024, ts_w, "grad_w_ukv")

    g_in, g_uq, g_ukv, g_out, g_small = _reduce_grads(_fold_grads(d_in_ext, d_uq_pad, d_ukv_ext, d_out_ext), d_small)
    d_in, nm_in, nv_in = _adamw_rows(w_in[0], g_in, m_w_in[0], v_w_in[0], 256)
    rest = _adamw_rest(
        [(w_uq[0], g_uq, m_w_uq[0], v_w_uq[0]), (w_ukv[0], g_ukv, m_w_ukv[0], v_w_ukv[0]),
         (w_out[0], g_out, m_w_out[0], v_w_out[0])],
        [(norm_in, m_norm_in, v_norm_in), (a_q_norm, m_a_q_norm, v_a_q_norm), (a_k_norm, m_a_k_norm, v_a_k_norm),
         (b_cq_norm, m_b_cq_norm, v_b_cq_norm), (b_ckv_norm, m_b_ckv_norm, v_b_ckv_norm),
         (b_q_norm, m_b_q_norm, v_b_q_norm), (b_k_norm, m_b_k_norm, v_b_k_norm)], g_small)
    (d_uq, nm_uq, nv_uq), (d_ukv, nm_ukv, nv_ukv), (d_out, nm_out, nv_out) = (rest[3 * i:3 * i + 3] for i in range(3))
    sm = [rest[9 + 4 * i:9 + 4 * i + 4] for i in range(7)]

    def leaves(k, p_in, p_uq, p_ukv, p_out):
        return [sm[SM_IN][k], p_in[None], sm[SM_AQ][k], sm[SM_AK][k], sm[SM_CQ][k], sm[SM_CKV][k], p_uq[None], p_ukv[None],
                sm[SM_BQ][k], sm[SM_BK][k], p_out[None]]

    return (g_small[SM_LOSS, 0], grad_x[None], *leaves(0, g_in, g_uq, g_ukv, g_out), *leaves(1, d_in, d_uq, d_ukv, d_out),
            *leaves(2, nm_in, nm_uq, nm_ukv, nm_out), *leaves(3, nv_in, nv_uq, nv_ukv, nv_out))
```

```python
import jax
import jax.numpy as jnp
import numpy as np
from jax import lax
from jax.experimental import pallas as pl
from jax.experimental.pallas import tpu as pltpu

F32 = jnp.float32
BF16 = jnp.bfloat16
MESH = pl.DeviceIdType.MESH

D_MODEL = 1024
GRID_W = 64
ROPE_THETA = 10000.0
EPS = 1e-6
A_HEADS, A_KV, A_DIM = 8, 2, 64
A_GROUP = A_HEADS // A_KV
B_HEADS, B_NOPE, B_ROPE, B_V = 4, 64, 32, 128
B_QK = B_NOPE + B_ROPE
B_Q_RANK, B_KV_RANK = 384, 256
N_IN = 2464
SCALE_A = 1.0 / float(np.sqrt(A_DIM))
SCALE_B = 1.0 / float(np.sqrt(B_QK))
LOG2E = float(np.log2(np.e))
LN2 = float(np.log(2.0))
ADAM_LR, ADAM_B1, ADAM_B2, ADAM_EPS, ADAM_WD, ADAM_STEP = 0.001, 0.9, 0.999, 1e-08, 0.01, 10

LANE = 128
VMEM_BYTES = 64 * 1024 * 1024
VMEM_LIMIT = VMEM_BYTES - 8 * 1024 * 1024

QA0 = 0
KA0 = QA0 + A_HEADS * LANE
VA0 = KA0 + A_KV * LANE
GA0 = VA0 + A_KV * LANE
GB0 = GA0 + A_HEADS * LANE
CQ0 = GB0 + B_HEADS * LANE
CKV0 = CQ0 + B_Q_RANK
KR0 = CKV0 + B_KV_RANK
N_EXT = KR0 + LANE
N_GATE = (A_HEADS + B_HEADS) * LANE
DELTA_ROWS = 16
N_PRE = KA0 + A_KV * LANE + B_Q_RANK + B_KV_RANK

N_CHIPS = 4
SH_IN = (D_MODEL, N_IN // N_CHIPS)
SH_UQ = (B_Q_RANK // N_CHIPS, B_HEADS * B_QK)
SH_UKV = (B_KV_RANK, B_HEADS * (B_NOPE + B_V) // N_CHIPS)
SH_OUT = (D_MODEL // N_CHIPS, D_MODEL)
SM_ROWS, SM_W = 16, D_MODEL
SM_IN, SM_AQ, SM_AK, SM_CQ, SM_CKV, SM_BQ, SM_BK, SM_LOSS = range(8)
F32_ROWS, BF16_ROWS = 8, 16


def _pallas(body, **kw):
    return pl.pallas_call(body, **kw)


def _params(sem=None, vmem=None):
    return pltpu.CompilerParams(dimension_semantics=sem, vmem_limit_bytes=vmem)


def _rms_fwd(x, g, n):
    r = lax.rsqrt(jnp.sum(x * x, axis=-1, keepdims=True) * (1.0 / n) + EPS)
    return x * r * g


def _rms_bwd(dy, x, g, n):
    u = dy * g
    r = lax.rsqrt(jnp.sum(x * x, axis=-1, keepdims=True) * (1.0 / n) + EPS)
    ux = jnp.sum(u * x, axis=-1, keepdims=True)
    xhat = x * r
    dx = r * (u - xhat * (r * ux * (1.0 / n)))
    return dx, dy * xhat


def _partner(y, half):
    lane = lax.broadcasted_iota(jnp.int32, y.shape, 1)
    first = (lane % (2 * half)) < half
    return jnp.where(first, pltpu.roll(y, LANE - half, 1), pltpu.roll(y, half, 1))


def _rope_fwd(y, cos, sin, half):
    return y * cos + _partner(y, half) * sin


def _rope_bwd(d, cos, sin, half):
    return d * cos - _partner(d, half) * sin


def _nt(a, b):
    return lax.dot_general(a, b, (((1,), (1,)), ((), ())), preferred_element_type=F32)


def _tn(a, b):
    return lax.dot_general(a, b, (((0,), (0,)), ((), ())), preferred_element_type=F32)


def _nn(a, b):
    return jnp.dot(a, b, preferred_element_type=F32)


def _block_rows(i, size):
    if isinstance(i, int):
        return pl.ds(i * size, size)
    return pl.ds(pl.multiple_of(i * size, size), size)


MAX_STATIC_BLOCKS = 8


def _three_stage(n, first, second, third):
    assert n >= 2 and n % 2 == 0
    first(0, 0)
    first(1, 1)
    second(0, 0)
    if n <= MAX_STATIC_BLOCKS:
        for i in range(1, n - 1):
            first(i + 1, (i + 1) % 2)
            second(i, i % 2)
            third(i - 1, (i - 1) % 2)
    else:
        def pair(t, carry):
            i = 2 * t + 1
            first(i + 1, 0)
            second(i, 1)
            third(i - 1, 0)
            first(i + 2, 1)
            second(i + 1, 0)
            third(i, 1)
            return carry

        lax.fori_loop(0, (n - 2) // 2, pair, 0)
    second(n - 1, 1)
    third(n - 2, 0)
    third(n - 1, 1)


def _full(shape):
    return pl.BlockSpec(shape, lambda *_: (0,) * len(shape))


def _gather_weights(shards):
    n = len(shards)
    halves = [w.shape[0] // 2 for w in shards]

    def body(*refs):
        w_refs, out_refs, (send_sems, recv_sems) = refs[:n], refs[n:2 * n], refs[2 * n:]
        x, y, c = lax.axis_index("x"), lax.axis_index("y"), lax.axis_index("c")
        sibling = (x, y, 1 - c)
        chips = [(1 - x, y), (x, 1 - y), (1 - x, 1 - y)]
        me = 2 * x + y

        def copy(a, k, j, hc, to):
            part = out_refs[a].at[j, pl.ds(pl.multiple_of(hc * halves[a], BF16_ROWS), halves[a]), :]
            return pltpu.make_async_remote_copy(
                src_ref=part, dst_ref=part, send_sem=send_sems.at[6 * a + k], recv_sem=recv_sems.at[6 * a + k],
                device_id=to, device_id_type=MESH)

        started = []
        for a in range(n):
            out_refs[a][me] = w_refs[a][...].astype(BF16)
            for k, chip in enumerate(chips):
                started.append(copy(a, k, me, c, (*chip, c)))
                started[-1].start()
        for k, chip in enumerate(chips):
            for a in range(n):
                copy(a, k, 2 * chip[0] + chip[1], c, (*chip, c)).wait_recv()
                started.append(copy(a, 3 + k, 2 * chip[0] + chip[1], c, sibling))
                started[-1].start()
        for k, chip in enumerate(chips):
            for a in range(n):
                copy(a, 3 + k, 2 * chip[0] + chip[1], 1 - c, sibling).wait_recv()
        for cp in started:
            cp.wait_send()

    return _pallas(
        body, name="gather_weights",
        out_shape=[jax.ShapeDtypeStruct((N_CHIPS,) + w.shape, BF16) for w in shards],
        in_specs=[pl.BlockSpec(memory_space=pltpu.VMEM)] * n,
        out_specs=[pl.BlockSpec(memory_space=pltpu.VMEM)] * n,
        scratch_shapes=[pltpu.SemaphoreType.DMA((6 * n,)), pltpu.SemaphoreType.DMA((6 * n,))],
        compiler_params=_params(vmem=32 * 1024 * 1024),
    )(*shards)


def _reduce_grads(parts, small):
    n_big = len(parts)
    n = n_big + 1
    shapes = [p.shape[1:] for p in parts] + [small.shape]
    halves = [sh[0] // 2 for sh in shapes]

    def body(*refs):
        p_refs, out_refs, rec_a, rec_b = refs[:n], refs[n:2 * n], refs[2 * n:3 * n], refs[3 * n:4 * n]
        sa_send, sa_recv, sb_send, sb_recv, sc_send, sc_recv = refs[4 * n:]
        x, y, c = lax.axis_index("x"), lax.axis_index("y"), lax.axis_index("c")
        sibling = (x, y, 1 - c)
        me = 2 * x + y

        def rows(a, hc):
            return pl.ds(pl.multiple_of(hc * halves[a], F32_ROWS), halves[a])

        def partial(a, j, hc):
            return p_refs[a].at[j, rows(a, hc), :] if a < n_big else p_refs[a].at[rows(a, hc), :]

        def copy_a(a, j):
            return pltpu.make_async_remote_copy(
                src_ref=partial(a, j, 1 - c), dst_ref=rec_a[a].at[j],
                send_sem=sa_send.at[N_CHIPS * a + j], recv_sem=sa_recv.at[N_CHIPS * a + j],
                device_id=sibling, device_id_type=MESH)

        def copy_b(a, r):
            j = me ^ r
            k = (N_CHIPS - 1) * a + r - 1
            return pltpu.make_async_remote_copy(
                src_ref=rec_a[a].at[j], dst_ref=rec_b[a].at[r], send_sem=sb_send.at[k], recv_sem=sb_recv.at[k],
                device_id=(j // 2, j % 2, c), device_id_type=MESH)

        def copy_c(a):
            return pltpu.make_async_remote_copy(
                src_ref=out_refs[a].at[rows(a, c), :], dst_ref=out_refs[a].at[rows(a, c), :],
                send_sem=sc_send.at[a], recv_sem=sc_recv.at[a], device_id=sibling, device_id_type=MESH)

        for a in range(n):
            for j in range(N_CHIPS):
                copy_a(a, j).start()
        for r in range(1, N_CHIPS):
            j = me ^ r
            for a in range(n):
                copy_a(a, j).wait_recv()
                rec_a[a][j] = rec_a[a][j] + partial(a, j, c)[...]
                copy_b(a, r).start()
        for a in range(n):
            copy_a(a, me).wait_recv()
            rec_b[a][0] = rec_a[a][me] + partial(a, me, c)[...]
        for a in range(n):
            for r in range(1, N_CHIPS):
                copy_b(a, r).wait_recv()
            total = rec_b[a][me]
            for j in range(1, N_CHIPS):
                total = total + rec_b[a][j ^ me]
            out_refs[a][rows(a, c), :] = total
            copy_c(a).start()
        for a in range(n):
            copy_c(a).wait_recv()
        for a in range(n):
            for j in range(N_CHIPS):
                copy_a(a, j).wait_send()
            for r in range(1, N_CHIPS):
                copy_b(a, r).wait_send()
            copy_c(a).wait_send()

    dma = pltpu.SemaphoreType.DMA
    return _pallas(
        body, name="reduce_grads",
        out_shape=[jax.ShapeDtypeStruct(sh, F32) for sh in shapes],
        in_specs=[pl.BlockSpec(memory_space=pltpu.VMEM)] * n,
        out_specs=[pl.BlockSpec(memory_space=pltpu.VMEM)] * n,
        scratch_shapes=[pltpu.VMEM((N_CHIPS, h) + sh[1:], F32) for h, sh in zip(halves, shapes)] * 2
                       + [dma((N_CHIPS * n,)), dma((N_CHIPS * n,)), dma(((N_CHIPS - 1) * n,)), dma(((N_CHIPS - 1) * n,)),
                          dma((n,)), dma((n,))],
        compiler_params=_params(vmem=VMEM_LIMIT),
    )(*parts, small)


def _pre(x, tabs, w_in_ext, w_uq_pad, w_ukv_ext, gains, tm):
    s_len = x.shape[0]
    nt = s_len // tm

    def body(x_ref, ca_ref, sa_ref, cb_ref, sb_ref, win_ref, wuq_ref, wukv_ref,
             gin_ref, gaq_ref, gak_ref, gcq_ref, gckv_ref, gbq_ref, gbk_ref,
             xn_ref, gates_ref, pre_ref, qbpre_ref, kbpre_ref, cq_ref, ckv_ref,
             qa_ref, ka_ref, va_ref, qb_ref, kb_ref, vb_ref, proj):
        xn = _rms_fwd(x_ref[...], gin_ref[...], D_MODEL)
        xn_ref[...] = jnp.transpose(xn).astype(BF16)
        proj[...] = _nn(xn.astype(BF16), win_ref[...])
        gates_ref[...] = proj[:, GA0:GA0 + N_GATE]
        pre_ref[:, 0:VA0] = proj[:, 0:VA0]
        pre_ref[:, VA0:N_PRE] = proj[:, CQ0:KR0]
        ca, sa, cb, sb = ca_ref[...], sa_ref[...], cb_ref[...], sb_ref[...]
        lane = lax.broadcasted_iota(jnp.int32, (tm, LANE), 1)
        for h in range(A_HEADS):
            yq = _rms_fwd(proj[:, QA0 + LANE * h:QA0 + LANE * (h + 1)], gaq_ref[...], A_DIM)
            qa_ref[h] = (_rope_fwd(yq, ca, sa, A_DIM // 4) * (SCALE_A * LOG2E)).astype(BF16)
        for h in range(A_KV):
            yk = _rms_fwd(proj[:, KA0 + LANE * h:KA0 + LANE * (h + 1)], gak_ref[...], A_DIM)
            ka_ref[h] = _rope_fwd(yk, ca, sa, A_DIM // 4).astype(BF16)
            va_ref[h] = jnp.where(lane == A_DIM, 1.0, proj[:, VA0 + LANE * h:VA0 + LANE * (h + 1)]).astype(BF16)
        cq = _rms_fwd(proj[:, CQ0:CQ0 + B_Q_RANK], gcq_ref[...], B_Q_RANK)
        cq_ref[...] = jnp.transpose(cq).astype(BF16)
        qbpre_ref[...] = _nn(cq.astype(BF16), wuq_ref[...])
        ckv = _rms_fwd(proj[:, CKV0:CKV0 + B_KV_RANK], gckv_ref[...], B_KV_RANK)
        ckv_ref[...] = jnp.transpose(ckv).astype(BF16)
        kvb = _nn(ckv.astype(BF16), wukv_ref[...])
        kr = proj[:, KR0:KR0 + LANE]
        for h in range(B_HEADS):
            yq = _rms_fwd(qbpre_ref[:, LANE * h:LANE * (h + 1)], gbq_ref[...], B_QK)
            qb_ref[h] = (_rope_fwd(yq, cb, sb, B_ROPE // 4) * (SCALE_B * LOG2E)).astype(BF16)
            kp = kvb[:, LANE * h:LANE * (h + 1)] + kr
            kbpre_ref[:, LANE * h:LANE * (h + 1)] = kp
            kb_ref[h] = _rope_fwd(_rms_fwd(kp, gbk_ref[...], B_QK), cb, sb, B_ROPE // 4).astype(BF16)
            vb_ref[h, :, 0:LANE] = kvb[:, B_HEADS * LANE + LANE * h:B_HEADS * LANE + LANE * (h + 1)].astype(BF16)
            vb_ref[h, :, LANE:2 * LANE] = jnp.where(lane == 0, 1.0, 0.0).astype(BF16)

    row = lambda w: pl.BlockSpec((tm, w), lambda i: (i, 0))
    col = lambda w: pl.BlockSpec((w, tm), lambda i: (0, i))
    heads = lambda n: pl.BlockSpec((n, tm, LANE), lambda i: (0, i, 0))
    hs = lambda n: jax.ShapeDtypeStruct((n, s_len, LANE), BF16)
    return _pallas(
        body, name="pre", grid=(nt,),
        in_specs=[row(D_MODEL), row(LANE), row(LANE), row(LANE), row(LANE),
                  _full(w_in_ext.shape), _full(w_uq_pad.shape), _full(w_ukv_ext.shape)]
                 + [_full(g.shape) for g in gains],
        out_specs=[col(D_MODEL), row(N_GATE), row(N_PRE), row(B_HEADS * LANE), row(B_HEADS * LANE),
                   col(B_Q_RANK), col(B_KV_RANK),
                   heads(A_HEADS), heads(A_KV), heads(A_KV), heads(B_HEADS), heads(B_HEADS),
                   pl.BlockSpec((B_HEADS, tm, 2 * LANE), lambda i: (0, i, 0))],
        out_shape=[jax.ShapeDtypeStruct((D_MODEL, s_len), BF16), jax.ShapeDtypeStruct((s_len, N_GATE), F32),
                   jax.ShapeDtypeStruct((s_len, N_PRE), F32), jax.ShapeDtypeStruct((s_len, B_HEADS * LANE), F32),
                   jax.ShapeDtypeStruct((s_len, B_HEADS * LANE), F32),
                   jax.ShapeDtypeStruct((B_Q_RANK, s_len), BF16), jax.ShapeDtypeStruct((B_KV_RANK, s_len), BF16),
                   hs(A_HEADS), hs(A_KV), hs(A_KV), hs(B_HEADS), hs(B_HEADS),
                   jax.ShapeDtypeStruct((B_HEADS, s_len, 2 * LANE), BF16)],
        scratch_shapes=[pltpu.VMEM((tm, N_EXT), F32)],
        compiler_params=_params(("parallel",), VMEM_LIMIT),
    )(x, *tabs, w_in_ext, w_uq_pad, w_ukv_ext, *gains)


def _attn_fwd(q, k, v, group, l_col, tq, tk, name):
    n_heads, s_len, _ = q.shape
    v_w = v.shape[2]
    nk = s_len // tk
    assert nk >= 2 and nk % 2 == 0

    def body(q_ref, k_ref, v_ref, o_ref, lse_ref, s_buf, p_buf, a_buf, m_ref, acc_ref):
        def rows(j):
            return _block_rows(j, tk)

        def scores(j, slot):
            s_buf[slot] = _nt(q_ref[...], k_ref[rows(j), :])

        def softmax(j, slot):
            s = s_buf[slot]
            m_old = m_ref[...]
            m_new = jnp.maximum(m_old, jnp.max(s, axis=-1, keepdims=True))
            m_ref[...] = m_new
            a_buf[slot] = jnp.exp2(m_old - m_new)
            p_buf[slot] = jnp.exp2(s - jnp.tile(m_new, (1, tk // LANE))).astype(BF16)

        def values(j, slot):
            pv = _nn(p_buf[slot], v_ref[rows(j), :])
            for c in range(0, v_w, LANE):
                acc_ref[:, c:c + LANE] = a_buf[slot] * acc_ref[:, c:c + LANE] + pv[:, c:c + LANE]

        m_ref[...] = jnp.full(m_ref.shape, -1e30, F32)
        acc_ref[...] = jnp.zeros(acc_ref.shape, F32)
        _three_stage(nk, scores, softmax, values)
        l = acc_ref[:, l_col:l_col + 1]
        o = acc_ref[:, 0:LANE] * (1.0 / l)
        if l_col < LANE:
            lane = lax.broadcasted_iota(jnp.int32, o.shape, 1)
            o = jnp.where(lane == l_col, 0.0, o)
        o_ref[...] = o
        lse_ref[...] = jnp.transpose(m_ref[...] + jnp.log2(jnp.broadcast_to(l, (tq, LANE))))[0:1, :]

    return _pallas(
        body, name=name, grid=(n_heads, s_len // tq),
        in_specs=[pl.BlockSpec((None, tq, LANE), lambda h, i: (h, i, 0)),
                  pl.BlockSpec((None, s_len, LANE), lambda h, i: (h // group, 0, 0)),
                  pl.BlockSpec((None, s_len, v_w), lambda h, i: (h // group, 0, 0))],
        out_specs=[pl.BlockSpec((None, tq, LANE), lambda h, i: (h, i, 0)),
                   pl.BlockSpec((None, None, 1, tq), lambda h, i: (h, i, 0, 0))],
        out_shape=[jax.ShapeDtypeStruct((n_heads, s_len, LANE), F32),
                   jax.ShapeDtypeStruct((n_heads, s_len // tq, 1, tq), F32)],
        scratch_shapes=[pltpu.VMEM((2, tq, tk), F32), pltpu.VMEM((2, tq, tk), BF16), pltpu.VMEM((2, tq, LANE), F32),
                        pltpu.VMEM((tq, LANE), F32), pltpu.VMEM((tq, v_w), F32)],
        compiler_params=_params(("parallel", "parallel"), 48 * 1024 * 1024),
    )(q, k, v)


def _mid(x, target, o_a, o_b, gates, w_out_ext, tm):
    s_len = x.shape[0]
    nt = s_len // tm
    n_heads = A_HEADS + B_HEADS

    def body(x_ref, t_ref, oa_ref, ob_ref, g_ref, w_ref,
             yt_ref, dh_ref, dgate_ref, doa_ref, dob_ref, delta_ref, loss_ref, silu_scr, dsilu_scr, y_ref):
        @pl.when(pl.program_id(0) == 0)
        def _():
            loss_ref[...] = jnp.zeros_like(loss_ref)

        def o_of(h):
            return oa_ref[h] if h < A_HEADS else ob_ref[h - A_HEADS]

        for h in range(n_heads):
            cols = slice(LANE * h, LANE * (h + 1))
            g = g_ref[:, cols]
            sig = 1.0 / (1.0 + jnp.exp(-g))
            silu = g * sig
            silu_scr[:, cols] = silu
            dsilu_scr[:, cols] = sig * (1.0 + g * (1.0 - sig))
            y = o_of(h) * silu
            y_ref[:, cols] = y.astype(BF16)
            yt_ref[cols, :] = jnp.transpose(y).astype(BF16)
        err =x_ref[...] + _nn(y_ref[...], w_ref[...]) - t_ref[...]
        sq = jnp.sum(jnp.sum(err * err, axis=-1, keepdims=True), axis=0, keepdims=True)
        loss_ref[...] += jnp.broadcast_to(sq * (0.5 / D_MODEL), loss_ref.shape)
        dh = err * (1.0 / D_MODEL)
        dh_ref[...] = dh
        dy = _nt(dh.astype(BF16), w_ref[...])
        lane = lax.broadcasted_iota(jnp.int32, (tm, LANE), 1)
        delta = jnp.zeros((tm, LANE), F32)
        for h in range(n_heads):
            cols = slice(LANE * h, LANE * (h + 1))
            dyh = dy[:, cols]
            oh = o_of(h)
            do = dyh * silu_scr[:, cols]
            dgate_ref[:, cols] = (dyh * oh * dsilu_scr[:, cols]).astype(BF16)
            delta = jnp.where(lane == h, jnp.sum(do * oh, axis=-1, keepdims=True), delta)
            if h < A_HEADS:
                doa_ref[h] = do.astype(BF16)
            else:
                dob_ref[h - A_HEADS] = do.astype(BF16)
        delta_ref[...] = jnp.transpose(delta)[0:DELTA_ROWS, :]

    row = lambda w: pl.BlockSpec((tm, w), lambda i: (i, 0))
    heads = lambda n, w=LANE: pl.BlockSpec((n, tm, w), lambda i: (0, i, 0))
    return _pallas(
        body, name="mid", grid=(nt,),
        in_specs=[row(D_MODEL), row(D_MODEL), heads(A_HEADS), heads(B_HEADS), row(N_GATE), _full(w_out_ext.shape)],
        out_specs=[pl.BlockSpec((N_GATE, tm), lambda i: (0, i)), row(D_MODEL), row(N_GATE), heads(A_HEADS), heads(B_HEADS),
                   pl.BlockSpec((DELTA_ROWS, tm), lambda i: (0, i)),
                   _full((8, LANE))],
        out_shape=[jax.ShapeDtypeStruct((N_GATE, s_len), BF16), jax.ShapeDtypeStruct((s_len, D_MODEL), F32),
                   jax.ShapeDtypeStruct((s_len, N_GATE), BF16),
                   jax.ShapeDtypeStruct((A_HEADS, s_len, LANE), BF16), jax.ShapeDtypeStruct((B_HEADS, s_len, LANE), BF16),
                   jax.ShapeDtypeStruct((DELTA_ROWS, s_len), F32), jax.ShapeDtypeStruct((8, LANE), F32)],
        scratch_shapes=[pltpu.VMEM((tm, N_GATE), F32), pltpu.VMEM((tm, N_GATE), F32), pltpu.VMEM((tm, N_GATE), BF16)],
        compiler_params=_params(("arbitrary",), VMEM_LIMIT),
    )(x, target, o_a, o_b, gates, w_out_ext)


def _attn_bwd(q, k, v, do, lse, delta, group, tq, tk, name):
    n_heads, s_len, _ = q.shape
    nq = s_len // tq
    assert nq >= 2 and nq % 2 == 0

    def body(q_ref, do_ref, lse_ref, delta_ref, k_ref, v_ref, dq_ref, dk_ref, dv_ref, s_buf, dp_buf, p_buf, ds_buf):
        @pl.when(pl.program_id(1) == 0)
        def _():
            dq_ref[...] = jnp.zeros_like(dq_ref)

        dk_ref[...] = jnp.zeros_like(dk_ref)
        dv_ref[...] = jnp.zeros_like(dv_ref)

        def rows(i):
            return _block_rows(i, tq)

        def scores(i, slot):
            s_buf[slot] = _nt(k_ref[...], q_ref[rows(i), :])
            dp_buf[slot] = _nt(v_ref[...], do_ref[rows(i), :])

        def elementwise(i, slot):
            p = jnp.exp2(s_buf[slot] - lse_ref[i])
            p_buf[slot] = p.astype(BF16)
            ds_buf[slot] = (p * (dp_buf[slot] - delta_ref[i])).astype(BF16)

        def grads(i, slot):
            dv_ref[...] += _nn(p_buf[slot], do_ref[rows(i), :])
            dk_ref[...] += _nn(ds_buf[slot], q_ref[rows(i), :])
            dq_ref[rows(i), :] += _tn(ds_buf[slot], k_ref[...])

        _three_stage(nq, scores, elementwise, grads)

    whole = lambda: pl.BlockSpec((None, s_len, LANE), lambda h, j: (h, 0, 0))
    stat = lambda: pl.BlockSpec((None, nq, 1, tq), lambda h, j: (h, 0, 0, 0))
    kvb = lambda: pl.BlockSpec((None, tk, LANE), lambda h, j: (h // group, j, 0))
    outb = lambda: pl.BlockSpec((None, tk, LANE), lambda h, j: (h, j, 0))
    shape = jax.ShapeDtypeStruct((n_heads, s_len, LANE), F32)
    return _pallas(
        body, name=name, grid=(n_heads, s_len // tk),
        in_specs=[whole(), whole(), stat(), stat(), kvb(), kvb()],
        out_specs=[whole(), outb(), outb()],
        out_shape=[shape, shape, shape],
        scratch_shapes=[pltpu.VMEM((2, tk, tq), F32), pltpu.VMEM((2, tk, tq), F32),
                        pltpu.VMEM((2, tk, tq), BF16), pltpu.VMEM((2, tk, tq), BF16)],
        compiler_params=_params(("parallel", "arbitrary"), 48 * 1024 * 1024),
    )(q, do, lse, delta, k, v)


def _post(x, dh, pre, qbpre, kbpre, dgate, dqa, dka, dva, dqb, dkb, dvb, loss_part, tabs,
          w_in_ext, w_uq_pad, w_ukv_ext, gains, tm):
    s_len = x.shape[0]
    nt = s_len // tm

    def body(x_ref, dh_ref, pre_ref, qbpre_ref, kbpre_ref, dgate_ref,
             dqa_ref, dka_ref, dva_ref, dqb_ref, dkb_ref, dvb_ref, loss_ref,
             ca_ref, sa_ref, cb_ref, sb_ref, win_ref, wuq_ref, wukv_ref,
             gin_ref, gaq_ref, gak_ref, gcq_ref, gckv_ref, gbq_ref, gbk_ref,
             gx_ref, dproj_ref, dqbpre_ref, dkvb_ref, dsm_ref):
        @pl.when(pl.program_id(0) == 0)
        def _():
            dsm_ref[...] = jnp.zeros_like(dsm_ref)
            dsm_ref[SM_LOSS:SM_LOSS + 1, 0:LANE] = loss_ref[0:1, :]

        def add_small(r, dg):
            dsm_ref[r:r + 1, 0:dg.shape[1]] += dg

        def tok_sum(a):
            return jnp.sum(a, axis=0, keepdims=True)

        ca, sa, cb, sb = ca_ref[...], sa_ref[...], cb_ref[...], sb_ref[...]
        lane = lax.broadcasted_iota(jnp.int32, (tm, LANE), 1)

        def back(c0, c1):
            return _nt(dproj_ref[:, c0:c1], win_ref[:, c0:c1])

        dproj_ref[:, GA0:GA0 + N_GATE] = dgate_ref[...]
        dxn = back(GA0, GA0 + N_GATE)
        dg = jnp.zeros((1, LANE), F32)
        for h in range(A_HEADS):
            dn = _rope_bwd(dqa_ref[h] * SCALE_A, ca, sa, A_DIM // 4)
            dx, dgr = _rms_bwd(dn, pre_ref[:, QA0 + LANE * h:QA0 + LANE * (h + 1)], gaq_ref[...], A_DIM)
            dproj_ref[:, QA0 + LANE * h:QA0 + LANE * (h + 1)] = dx.astype(BF16)
            dg = dg + tok_sum(dgr)
        add_small(SM_AQ, dg)
        dxn = dxn + back(QA0, KA0)
        dg = jnp.zeros((1, LANE), F32)
        for h in range(A_KV):
            dk = dka_ref[A_GROUP * h]
            dv = dva_ref[A_GROUP * h]
            for g in range(1, A_GROUP):
                dk = dk + dka_ref[A_GROUP * h + g]
                dv = dv + dva_ref[A_GROUP * h + g]
            dn = _rope_bwd(dk * LN2, ca, sa, A_DIM // 4)
            dx, dgr = _rms_bwd(dn, pre_ref[:, KA0 + LANE * h:KA0 + LANE * (h + 1)], gak_ref[...], A_DIM)
            dproj_ref[:, KA0 + LANE * h:KA0 + LANE * (h + 1)] = dx.astype(BF16)
            dproj_ref[:, VA0 + LANE * h:VA0 + LANE * (h + 1)] = dv.astype(BF16)
            dg = dg + tok_sum(dgr)
        add_small(SM_AK, dg)
        dxn = dxn + back(KA0, GA0)
        dg = jnp.zeros((1, LANE), F32)
        for h in range(B_HEADS):
            cols = slice(LANE * h, LANE * (h + 1))
            dn = _rope_bwd(dqb_ref[h] * SCALE_B, cb, sb, B_ROPE // 4)
            dx, dgr = _rms_bwd(dn, qbpre_ref[:, cols], gbq_ref[...], B_QK)
            dqbpre_ref[:, cols] = dx.astype(BF16)
            dg = dg + tok_sum(dgr)
        add_small(SM_BQ, dg)
        dcq = _nt(dqbpre_ref[...], wuq_ref[...])
        dx, dgr = _rms_bwd(dcq, pre_ref[:, VA0:VA0 + B_Q_RANK], gcq_ref[...], B_Q_RANK)
        dproj_ref[:, CQ0:CQ0 + B_Q_RANK] = dx.astype(BF16)
        add_small(SM_CQ, tok_sum(dgr))
        dxn = dxn + back(CQ0, CKV0)
        dg = jnp.zeros((1, LANE), F32)
        dkr = jnp.zeros((tm, LANE), F32)
        for h in range(B_HEADS):
            cols = slice(LANE * h, LANE * (h + 1))
            dn = _rope_bwd(dkb_ref[h] * LN2, cb, sb, B_ROPE // 4)
            dx, dgr = _rms_bwd(dn, kbpre_ref[:, cols], gbk_ref[...], B_QK)
            dkvb_ref[:, cols] = jnp.where(lane < B_NOPE, dx, 0.0).astype(BF16)
            dkvb_ref[:, B_HEADS * LANE + LANE * h:B_HEADS * LANE + LANE * (h + 1)] = dvb_ref[h].astype(BF16)
            dkr = dkr + dx
            dg = dg + tok_sum(dgr)
        add_small(SM_BK, dg)
        dproj_ref[:, KR0:KR0 + LANE] = jnp.where((lane >= B_NOPE) & (lane < B_QK), dkr, 0.0).astype(BF16)
        dckv = _nt(dkvb_ref[...], wukv_ref[...])
        dx, dgr = _rms_bwd(dckv, pre_ref[:, VA0 + B_Q_RANK:N_PRE], gckv_ref[...], B_KV_RANK)
        dproj_ref[:, CKV0:CKV0 + B_KV_RANK] = dx.astype(BF16)
        add_small(SM_CKV, tok_sum(dgr))
        dxn = dxn + back(CKV0, N_EXT)
        dx, dgr = _rms_bwd(dxn, x_ref[...], gin_ref[...], D_MODEL)
        gx_ref[...] = dh_ref[...] + dx
        add_small(SM_IN, tok_sum(dgr))

    row = lambda w: pl.BlockSpec((tm, w), lambda i: (i, 0))
    heads = lambda n: pl.BlockSpec((n, tm, LANE), lambda i: (0, i, 0))
    return _pallas(
        body, name="post", grid=(nt,),
        in_specs=[row(D_MODEL), row(D_MODEL), row(N_PRE), row(B_HEADS * LANE), row(B_HEADS * LANE), row(N_GATE),
                  heads(A_HEADS), heads(A_HEADS), heads(A_HEADS), heads(B_HEADS), heads(B_HEADS), heads(B_HEADS),
                  _full(loss_part.shape), row(LANE), row(LANE), row(LANE), row(LANE),
                  _full(w_in_ext.shape), _full(w_uq_pad.shape), _full(w_ukv_ext.shape)]
                 + [_full(g.shape) for g in gains],
        out_specs=[row(D_MODEL), row(N_EXT), row(B_HEADS * LANE), row(2 * B_HEADS * LANE), _full((SM_ROWS, SM_W))],
        out_shape=[jax.ShapeDtypeStruct((s_len, D_MODEL), F32), jax.ShapeDtypeStruct((s_len, N_EXT), BF16),
                   jax.ShapeDtypeStruct((s_len, B_HEADS * LANE), BF16),
                   jax.ShapeDtypeStruct((s_len, 2 * B_HEADS * LANE), BF16),
                   jax.ShapeDtypeStruct((SM_ROWS, SM_W), F32)],
        compiler_params=_params(("arbitrary",), VMEM_LIMIT),
    )(x, dh, pre, qbpre, kbpre, dgate, dqa, dka, dva, dqb, dkb, dvb, loss_part, *tabs,
      w_in_ext, w_uq_pad, w_ukv_ext, *gains)


def _grad_w(a_t, b, tn, ts, name):
    m, s_len = a_t.shape
    n = b.shape[1]

    def body(a_ref, b_ref, o_ref):
        @pl.when(pl.program_id(1) == 0)
        def _():
            o_ref[...] = jnp.zeros_like(o_ref)

        o_ref[...] += _nn(a_ref[...], b_ref[...].astype(BF16))

    return _pallas(
        body, name=name, grid=(n // tn, s_len // ts),
        in_specs=[pl.BlockSpec((m, ts), lambda j, t: (0, t)), pl.BlockSpec((ts, tn), lambda j, t: (t, j))],
        out_specs=pl.BlockSpec((m, tn), lambda j, t: (0, j)),
        out_shape=jax.ShapeDtypeStruct((m, n), F32),
        compiler_params=_params(("parallel", "arbitrary"), 48 * 1024 * 1024),
    )(a_t, b)


def _adam_math(w, g, m, v):
    nm = ADAM_B1 * m + (1.0 - ADAM_B1) * g
    nv = ADAM_B2 * v + (1.0 - ADAM_B2) * (g * g)
    m_hat = nm / (1.0 - ADAM_B1 ** ADAM_STEP)
    v_hat = nv / (1.0 - ADAM_B2 ** ADAM_STEP)
    return -ADAM_LR * (m_hat / (jnp.sqrt(v_hat) + ADAM_EPS) + ADAM_WD * w), nm, nv


def _adamw_rows(w, g, m, v, tr):
    rows, cols = w.shape

    def body(w_ref, g_ref, m_ref, v_ref, d_ref, nm_ref, nv_ref):
        d_ref[...], nm_ref[...], nv_ref[...] = _adam_math(w_ref[...], g_ref[...], m_ref[...], v_ref[...])

    blk = pl.BlockSpec((tr, cols), lambda i: (i, 0))
    shape = jax.ShapeDtypeStruct((rows, cols), F32)
    return _pallas(
        body, name="adamw_w_in", grid=(rows // tr,),
        in_specs=[blk] * 4, out_specs=[blk] * 3, out_shape=[shape] * 3,
        compiler_params=_params(("parallel",), 32 * 1024 * 1024),
    )(w, g, m, v)


def _adamw_rest(bigs, smalls, g_small):
    nb, ns = len(bigs), len(smalls)

    def body(*refs):
        ins, outs = refs[:4 * nb + 3 * ns + 1], refs[4 * nb + 3 * ns + 1:]
        for i in range(nb):
            w_ref, g_ref, m_ref, v_ref = ins[4 * i:4 * i + 4]
            d_ref, nm_ref, nv_ref = outs[3 * i:3 * i + 3]
            d_ref[...], nm_ref[...], nv_ref[...] = _adam_math(w_ref[...], g_ref[...], m_ref[...], v_ref[...])
        gs_ref = ins[-1]
        for i in range(ns):
            w_ref, m_ref, v_ref = ins[4 * nb + 3 * i:4 * nb + 3 * i + 3]
            g_ref, d_ref, nm_ref, nv_ref = outs[3 * nb + 4 * i:3 * nb + 4 * i + 4]
            g = gs_ref[i:i + 1, 0:w_ref.shape[1]]
            g_ref[...] = g
            d_ref[...], nm_ref[...], nv_ref[...] = _adam_math(w_ref[...], g, m_ref[...], v_ref[...])

    flat_in = [a for quad in bigs for a in quad] + [a for tri in smalls for a in tri] + [g_small]
    out_shape = ([jax.ShapeDtypeStruct(q[0].shape, F32) for q in bigs for _ in range(3)]
                 + [jax.ShapeDtypeStruct(t[0].shape, F32) for t in smalls for _ in range(4)])
    return _pallas(
        body, name="adamw_rest",
        in_specs=[pl.BlockSpec(memory_space=pltpu.VMEM)] * len(flat_in),
        out_specs=[pl.BlockSpec(memory_space=pltpu.VMEM)] * len(out_shape),
        out_shape=out_shape,
        compiler_params=_params(vmem=32 * 1024 * 1024),
    )(*flat_in)


def _rope_tables(s_len):
    rows = s_len // GRID_W
    row = jnp.arange(rows, dtype=F32)
    col = jnp.arange(GRID_W, dtype=F32)

    def lay(dim, lead):
        half = dim // 2
        inv = 1.0 / (ROPE_THETA ** (jnp.arange(0, half, 2, dtype=F32) / half))
        ang_r, ang_c = row[:, None] * inv[None, :], col[:, None] * inv[None, :]
        tail = LANE - lead - dim

        def place(a, b, at, n, fill=0.0):
            return jnp.concatenate([jnp.full((n, lead), fill if at else 0.0, F32) if lead else jnp.zeros((n, 0), F32),
                                    jnp.zeros((n, at * half), F32), a, b,
                                    jnp.zeros((n, (1 - at) * half + tail), F32)], axis=1)

        cos_r = place(jnp.cos(ang_r), jnp.cos(ang_r), 0, rows)
        cos_c = place(jnp.cos(ang_c), jnp.cos(ang_c), 1, GRID_W, 1.0)
        sin_r = place(-jnp.sin(ang_r), jnp.sin(ang_r), 0, rows)
        sin_c = place(-jnp.sin(ang_c), jnp.sin(ang_c), 1, GRID_W)
        assert cos_r.shape[1] == LANE
        cos = (cos_r[:, None, :] + cos_c[None, :, :]).reshape(s_len, LANE)
        sin = (sin_r[:, None, :] + sin_c[None, :, :]).reshape(s_len, LANE)
        return cos, sin

    cos_a, sin_a = lay(A_DIM, 0)
    cos_b, sin_b = lay(B_ROPE, B_NOPE)
    return cos_a, sin_a, cos_b, sin_b


def _pad_heads(w, n_heads, dim, axis):
    shape = w.shape[:axis] + (n_heads, dim) + w.shape[axis + 1:]
    pad = [(0, 0)] * len(shape)
    pad[axis + 1] = (0, LANE - dim)
    out = jnp.pad(w.reshape(shape), pad)
    return out.reshape(w.shape[:axis] + (n_heads * LANE,) + w.shape[axis + 1:])


def _unpad_heads(w, n_heads, dim, axis):
    shape = w.shape[:axis] + (n_heads, LANE) + w.shape[axis + 1:]
    out = lax.slice_in_dim(w.reshape(shape), 0, dim, axis=axis + 1)
    return out.reshape(w.shape[:axis] + (n_heads * dim,) + w.shape[axis + 1:])


def _pad_vec(g):
    return jnp.pad(g, ((0, 0), (0, LANE - g.shape[1])))


def _ext_weights(g_in, g_uq, g_ukv, g_out):
    w_in = g_in.transpose(1, 0, 2).reshape(D_MODEL, N_IN)
    w_uq = g_uq.reshape(B_Q_RANK, B_HEADS * B_QK)
    w_ukv = g_ukv.transpose(1, 0, 2).reshape(B_KV_RANK, B_HEADS * (B_NOPE + B_V))
    w_out = g_out.reshape(D_MODEL, D_MODEL)
    a_w = A_HEADS * A_DIM
    kv_w = A_KV * A_DIM
    o = 0
    secs = []
    for width, heads in ((a_w, A_HEADS), (kv_w, A_KV), (kv_w, A_KV), (a_w, A_HEADS)):
        secs.append(_pad_heads(w_in[:, o:o + width], heads, A_DIM, 1))
        o += width
    a_q, a_k, a_v, a_g = secs
    b_cq = w_in[:, o:o + B_Q_RANK]
    o += B_Q_RANK
    b_ckv = w_in[:, o:o + B_KV_RANK]
    o += B_KV_RANK
    b_kr = jnp.pad(w_in[:, o:o + B_ROPE], ((0, 0), (B_NOPE, LANE - B_QK)))
    o += B_ROPE
    b_g = w_in[:, o:]
    w_in_ext = jnp.concatenate([a_q, a_k, a_v, a_g, b_g, b_cq, b_ckv, b_kr], axis=1)
    w_uq_pad = _pad_heads(w_uq, B_HEADS, B_QK, 1)
    kv3 = w_ukv.reshape(B_KV_RANK, B_HEADS, B_NOPE + B_V)
    w_ukv_ext = jnp.concatenate([_pad_heads(kv3[:, :, :B_NOPE].reshape(B_KV_RANK, B_HEADS * B_NOPE), B_HEADS, B_NOPE, 1),
                                 kv3[:, :, B_NOPE:].reshape(B_KV_RANK, B_HEADS * B_V)], axis=1)
    w_out_ext = jnp.concatenate([_pad_heads(w_out[:a_w], A_HEADS, A_DIM, 0), w_out[a_w:]], axis=0)
    return w_in_ext, w_uq_pad, w_ukv_ext, w_out_ext


def _fold_grads(d_in_ext, d_uq_pad, d_ukv_ext, d_out_ext):
    d_in = jnp.concatenate([
        _unpad_heads(d_in_ext[:, QA0:KA0], A_HEADS, A_DIM, 1),
        _unpad_heads(d_in_ext[:, KA0:VA0], A_KV, A_DIM, 1),
        _unpad_heads(d_in_ext[:, VA0:GA0], A_KV, A_DIM, 1),
        _unpad_heads(d_in_ext[:, GA0:GB0], A_HEADS, A_DIM, 1),
        d_in_ext[:, CQ0:KR0],
        d_in_ext[:, KR0 + B_NOPE:KR0 + B_QK],
        d_in_ext[:, GB0:CQ0]], axis=1)
    d_uq = _unpad_heads(d_uq_pad, B_HEADS, B_QK, 1)
    k3 = _unpad_heads(d_ukv_ext[:, :B_HEADS * LANE], B_HEADS, B_NOPE, 1).reshape(B_KV_RANK, B_HEADS, B_NOPE)
    v3 = d_ukv_ext[:, B_HEADS * LANE:].reshape(B_KV_RANK, B_HEADS, B_V)
    d_ukv = jnp.concatenate([k3, v3], axis=2).reshape(B_KV_RANK, B_HEADS * (B_NOPE + B_V))
    d_out = jnp.concatenate([_unpad_heads(d_out_ext[:A_HEADS * LANE], A_HEADS, A_DIM, 0), d_out_ext[A_HEADS * LANE:]], axis=0)
    return (d_in.reshape(D_MODEL, N_CHIPS, SH_IN[1]).transpose(1, 0, 2),
            d_uq.reshape((N_CHIPS,) + SH_UQ),
            d_ukv.reshape(B_KV_RANK, N_CHIPS, SH_UKV[1]).transpose(1, 0, 2),
            d_out.reshape((N_CHIPS,) + SH_OUT))


def kernel(x, norm_in, w_in, a_q_norm, a_k_norm, b_cq_norm, b_ckv_norm, w_uq, w_ukv, b_q_norm, b_k_norm, w_out, loss_target, m_norm_in, m_w_in, m_a_q_norm, m_a_k_norm, m_b_cq_norm, m_b_ckv_norm, m_w_uq, m_w_ukv, m_b_q_norm, m_b_k_norm, m_w_out, v_norm_in, v_w_in, v_a_q_norm, v_a_k_norm, v_b_cq_norm, v_b_ckv_norm, v_w_uq, v_w_ukv, v_b_q_norm, v_b_k_norm, v_w_out):
    s_len = x.shape[1]
    xs, ts = x[0], loss_target[0]
    tm = min(256, s_len)
    tq, tk_f = min(512, s_len // 2), min(1024, s_len // 2)
    tq_b, tk_b = min(1024, s_len // 2), min(512, s_len)

    w_in_ext, w_uq_pad, w_ukv_ext, w_out_ext = _ext_weights(*_gather_weights((w_in[0], w_uq[0], w_ukv[0], w_out[0])))
    gains = (norm_in, _pad_vec(a_q_norm), _pad_vec(a_k_norm), b_cq_norm, b_ckv_norm, _pad_vec(b_q_norm), _pad_vec(b_k_norm))
    tabs = _rope_tables(s_len)

    (xn_t, gates, pre, qbpre, kbpre, cq_t, ckv_t, qa, ka, va, qb, kb, vb) = _pre(
        xs, tabs, w_in_ext, w_uq_pad, w_ukv_ext, gains, tm)
    o_a, lse_a = _attn_fwd(qa, ka, va, A_GROUP, A_DIM, tq, tk_f, "attn_fwd_a")
    o_b, lse_b = _attn_fwd(qb, kb, vb, 1, B_V, tq, tk_f, "attn_fwd_b")
    y_t, dh, dgate, do_a, do_b, delta, loss_part = _mid(xs, ts, o_a, o_b, gates, w_out_ext, tm)

    def stat(a):
        return a.reshape(a.shape[0], s_len // tq_b, 1, tq_b)

    dqa, dka, dva = _attn_bwd(qa, ka, va, do_a, stat(lse_a), stat(delta[:A_HEADS]), A_GROUP, tq_b, tk_b, "attn_bwd_a")
    dqb, dkb, dvb = _attn_bwd(qb, kb, vb, do_b, stat(lse_b), stat(delta[A_HEADS:A_HEADS + B_HEADS]), 1, tq_b, tk_b,
                              "attn_bwd_b")
    grad_x, dproj, dqbpre, dkvb, d_small = _post(
        xs, dh, pre, qbpre, kbpre, dgate, dqa, dka, dva, dqb, dkb, dvb, loss_part, tabs,
        w_in_ext, w_uq_pad, w_ukv_ext, gains, tm)

    ts_w = min(2048, s_len)
    d_in_ext = _grad_w(xn_t, dproj, 768, ts_w, "grad_w_in")
    d_out_ext = _grad_w(y_t, dh, 512, ts_w, "grad_w_out")
    d_uq_pad = _grad_w(cq_t, dqbpre, 512, ts_w, "grad_w_uq")
    d_ukv_ext = _grad_w(ckv_t, dkvb, 1024, ts_w, "grad_w_ukv")

    g_in, g_uq, g_ukv, g_out, g_small = _reduce_grads(_fold_grads(d_in_ext, d_uq_pad, d_ukv_ext, d_out_ext), d_small)
    d_in, nm_in, nv_in = _adamw_rows(w_in[0], g_in, m_w_in[0], v_w_in[0], 256)
    rest = _adamw_rest(
        [(w_uq[0], g_uq, m_w_uq[0], v_w_uq[0]), (w_ukv[0], g_ukv, m_w_ukv[0], v_w_ukv[0]),
         (w_out[0], g_out, m_w_out[0], v_w_out[0])],
        [(norm_in, m_norm_in, v_norm_in), (a_q_norm, m_a_q_norm, v_a_q_norm), (a_k_norm, m_a_k_norm, v_a_k_norm),
         (b_cq_norm, m_b_cq_norm, v_b_cq_norm), (b_ckv_norm, m_b_ckv_norm, v_b_ckv_norm),
         (b_q_norm, m_b_q_norm, v_b_q_norm), (b_k_norm, m_b_k_norm, v_b_k_norm)], g_small)
    (d_uq, nm_uq, nv_uq), (d_ukv, nm_ukv, nv_ukv), (d_out, nm_out, nv_out) = (rest[3 * i:3 * i + 3] for i in range(3))
    sm = [rest[9 + 4 * i:9 + 4 * i + 4] for i in range(7)]

    def leaves(k, p_in, p_uq, p_ukv, p_out):
        return [sm[SM_IN][k], p_in[None], sm[SM_AQ][k], sm[SM_AK][k], sm[SM_CQ][k], sm[SM_CKV][k], p_uq[None], p_ukv[None],
                sm[SM_BQ][k], sm[SM_BK][k], p_out[None]]

    return (g_small[SM_LOSS, 0], grad_x[None], *leaves(0, g_in, g_uq, g_ukv, g_out), *leaves(1, d_in, d_uq, d_ukv, d_out),
            *leaves(2, nm_in, nm_uq, nm_ukv, nm_out), *leaves(3, nv_in, nv_uq, nv_ukv, nv_out))
```

```python
import jax
import jax.numpy as jnp
import numpy as np
from jax import lax
from jax.experimental import pallas as pl
from jax.experimental.pallas import tpu as pltpu

F32 = jnp.float32
BF16 = jnp.bfloat16
MESH = pl.DeviceIdType.MESH

D_MODEL = 1024
GRID_W = 64
ROPE_THETA = 10000.0
EPS = 1e-6
A_HEADS, A_KV, A_DIM = 8, 2, 64
A_GROUP = A_HEADS // A_KV
B_HEADS, B_NOPE, B_ROPE, B_V = 4, 64, 32, 128
B_QK = B_NOPE + B_ROPE
B_Q_RANK, B_KV_RANK = 384, 256
N_IN = 2464
SCALE_A = 1.0 / float(np.sqrt(A_DIM))
SCALE_B = 1.0 / float(np.sqrt(B_QK))
LOG2E = float(np.log2(np.e))
LN2 = float(np.log(2.0))
ADAM_LR, ADAM_B1, ADAM_B2, ADAM_EPS, ADAM_WD, ADAM_STEP = 0.001, 0.9, 0.999, 1e-08, 0.01, 10

LANE = 128
VMEM_BYTES = 64 * 1024 * 1024
VMEM_LIMIT = VMEM_BYTES - 8 * 1024 * 1024

QA0 = 0
KA0 = QA0 + A_HEADS * LANE
VA0 = KA0 + A_KV * LANE
GA0 = VA0 + A_KV * LANE
GB0 = GA0 + A_HEADS * LANE
CQ0 = GB0 + B_HEADS * LANE
CKV0 = CQ0 + B_Q_RANK
KR0 = CKV0 + B_KV_RANK
N_EXT = KR0 + LANE
N_GATE = (A_HEADS + B_HEADS) * LANE
DELTA_ROWS = 16
N_PRE = KA0 + A_KV * LANE + B_Q_RANK + B_KV_RANK

N_CHIPS = 4
SH_IN = (D_MODEL, N_IN // N_CHIPS)
SH_UQ = (B_Q_RANK // N_CHIPS, B_HEADS * B_QK)
SH_UKV = (B_KV_RANK, B_HEADS * (B_NOPE + B_V) // N_CHIPS)
SH_OUT = (D_MODEL // N_CHIPS, D_MODEL)
SM_ROWS, SM_W = 16, D_MODEL
SM_IN, SM_AQ, SM_AK, SM_CQ, SM_CKV, SM_BQ, SM_BK, SM_LOSS = range(8)
F32_ROWS, BF16_ROWS = 8, 16


def _pallas(body, **kw):
    return pl.pallas_call(body, **kw)


def _params(sem=None, vmem=None):
    return pltpu.CompilerParams(dimension_semantics=sem, vmem_limit_bytes=vmem)


def _rms_fwd(x, g, n):
    r = lax.rsqrt(jnp.sum(x * x, axis=-1, keepdims=True) * (1.0 / n) + EPS)
    return x * r * g


def _rms_bwd(dy, x, g, n):
    u = dy * g
    r = lax.rsqrt(jnp.sum(x * x, axis=-1, keepdims=True) * (1.0 / n) + EPS)
    ux = jnp.sum(u * x, axis=-1, keepdims=True)
    xhat = x * r
    dx = r * (u - xhat * (r * ux * (1.0 / n)))
    return dx, dy * xhat


def _partner(y, half):
    lane = lax.broadcasted_iota(jnp.int32, y.shape, 1)
    first = (lane % (2 * half)) < half
    return jnp.where(first, pltpu.roll(y, LANE - half, 1), pltpu.roll(y, half, 1))


def _rope_fwd(y, cos, sin, half):
    return y * cos + _partner(y, half) * sin


def _rope_bwd(d, cos, sin, half):
    return d * cos - _partner(d, half) * sin


def _nt(a, b):
    return lax.dot_general(a, b, (((1,), (1,)), ((), ())), preferred_element_type=F32)


def _tn(a, b):
    return lax.dot_general(a, b, (((0,), (0,)), ((), ())), preferred_element_type=F32)


def _nn(a, b):
    return jnp.dot(a, b, preferred_element_type=F32)


def _block_rows(i, size):
    if isinstance(i, int):
        return pl.ds(i * size, size)
    return pl.ds(pl.multiple_of(i * size, size), size)


MAX_STATIC_BLOCKS = 16


def _three_stage(n, first, second, third):
    assert n >= 2 and n % 2 == 0
    first(0, 0)
    first(1, 1)
    second(0, 0)
    if n <= MAX_STATIC_BLOCKS:
        for i in range(1, n - 1):
            first(i + 1, (i + 1) % 2)
            second(i, i % 2)
            third(i - 1, (i - 1) % 2)
    else:
        def pair(t, carry):
            i = 2 * t + 1
            first(i + 1, 0)
            second(i, 1)
            third(i - 1, 0)
            first(i + 2, 1)
            second(i + 1, 0)
            third(i, 1)
            return carry

        lax.fori_loop(0, (n - 2) // 2, pair, 0)
    second(n - 1, 1)
    third(n - 2, 0)
    third(n - 1, 1)


def _full(shape):
    return pl.BlockSpec(shape, lambda *_: (0,) * len(shape))


def _gather_weights(shards):
    n = len(shards)
    halves = [w.shape[0] // 2 for w in shards]

    def body(*refs):
        w_refs, out_refs, (send_sems, recv_sems) = refs[:n], refs[n:2 * n], refs[2 * n:]
        x, y, c = lax.axis_index("x"), lax.axis_index("y"), lax.axis_index("c")
        sibling = (x, y, 1 - c)
        chips = [(1 - x, y), (x, 1 - y), (1 - x, 1 - y)]
        me = 2 * x + y

        def copy(a, k, j, hc, to):
            part = out_refs[a].at[j, pl.ds(pl.multiple_of(hc * halves[a], BF16_ROWS), halves[a]), :]
            return pltpu.make_async_remote_copy(
                src_ref=part, dst_ref=part, send_sem=send_sems.at[6 * a + k], recv_sem=recv_sems.at[6 * a + k],
                device_id=to, device_id_type=MESH)

        started = []
        for a in range(n):
            out_refs[a][me] = w_refs[a][...].astype(BF16)
            for k, chip in enumerate(chips):
                started.append(copy(a, k, me, c, (*chip, c)))
                started[-1].start()
        for k, chip in enumerate(chips):
            for a in range(n):
                copy(a, k, 2 * chip[0] + chip[1], c, (*chip, c)).wait_recv()
                started.append(copy(a, 3 + k, 2 * chip[0] + chip[1], c, sibling))
                started[-1].start()
        for k, chip in enumerate(chips):
            for a in range(n):
                copy(a, 3 + k, 2 * chip[0] + chip[1], 1 - c, sibling).wait_recv()
        for cp in started:
            cp.wait_send()

    return _pallas(
        body, name="gather_weights",
        out_shape=[jax.ShapeDtypeStruct((N_CHIPS,) + w.shape, BF16) for w in shards],
        in_specs=[pl.BlockSpec(memory_space=pltpu.VMEM)] * n,
        out_specs=[pl.BlockSpec(memory_space=pltpu.VMEM)] * n,
        scratch_shapes=[pltpu.SemaphoreType.DMA((6 * n,)), pltpu.SemaphoreType.DMA((6 * n,))],
        compiler_params=_params(vmem=32 * 1024 * 1024),
    )(*shards)


def _reduce_grads(parts, small):
    n_big = len(parts)
    n = n_big + 1
    shapes = [p.shape[1:] for p in parts] + [small.shape]
    halves = [sh[0] // 2 for sh in shapes]

    def body(*refs):
        p_refs, out_refs, rec_a, rec_b = refs[:n], refs[n:2 * n], refs[2 * n:3 * n], refs[3 * n:4 * n]
        send_b = refs[4 * n:4 * n + n_big]
        sa_send, sa_recv, sb_send, sb_recv, sc_send, sc_recv = refs[4 * n + n_big:]
        x, y, c = lax.axis_index("x"), lax.axis_index("y"), lax.axis_index("c")
        sibling = (x, y, 1 - c)
        me = 2 * x + y

        def rows(a, hc):
            return pl.ds(pl.multiple_of(hc * halves[a], F32_ROWS), halves[a])

        def partial(a, j, hc):
            return p_refs[a].at[j, rows(a, hc), :] if a < n_big else p_refs[a].at[rows(a, hc), :]

        def copy_a(a, j):
            return pltpu.make_async_remote_copy(
                src_ref=partial(a, j, 1 - c), dst_ref=rec_a[a].at[j],
                send_sem=sa_send.at[N_CHIPS * a + j], recv_sem=sa_recv.at[N_CHIPS * a + j],
                device_id=sibling, device_id_type=MESH)

        def copy_b(a, r):
            j = me ^ r
            k = (N_CHIPS - 1) * a + r - 1
            return pltpu.make_async_remote_copy(
                src_ref=(send_b[a] if a < n_big else rec_a[a]).at[j], dst_ref=rec_b[a].at[r],
                send_sem=sb_send.at[k], recv_sem=sb_recv.at[k], device_id=(j // 2, j % 2, c), device_id_type=MESH)

        def copy_c(a):
            return pltpu.make_async_remote_copy(
                src_ref=out_refs[a].at[rows(a, c), :], dst_ref=out_refs[a].at[rows(a, c), :],
                send_sem=sc_send.at[a], recv_sem=sc_recv.at[a], device_id=sibling, device_id_type=MESH)

        for a in range(n):
            for j in range(N_CHIPS):
                copy_a(a, j).start()
        for r in range(1, N_CHIPS):
            j = me ^ r
            for a in range(n):
                copy_a(a, j).wait_recv()
                chip_part = rec_a[a][j] + partial(a, j, c)[...]
                if a < n_big:
                    send_b[a][j] = chip_part.astype(BF16)
                else:
                    rec_a[a][j] = chip_part
                copy_b(a, r).start()
        for a in range(n):
            copy_a(a, me).wait_recv()
            rec_b[a][0] = (rec_a[a][me] + partial(a, me, c)[...]).astype(rec_b[a].dtype)
        for a in range(n):
            for r in range(1, N_CHIPS):
                copy_b(a, r).wait_recv()
            total = rec_b[a][me].astype(F32)
            for j in range(1, N_CHIPS):
                total = total + rec_b[a][j ^ me].astype(F32)
            out_refs[a][rows(a, c), :] = total
            copy_c(a).start()
        for a in range(n):
            copy_c(a).wait_recv()
        for a in range(n):
            for j in range(N_CHIPS):
                copy_a(a, j).wait_send()
            for r in range(1, N_CHIPS):
                copy_b(a, r).wait_send()
            copy_c(a).wait_send()

    dma = pltpu.SemaphoreType.DMA
    return _pallas(
        body, name="reduce_grads",
        out_shape=[jax.ShapeDtypeStruct(sh, F32) for sh in shapes],
        in_specs=[pl.BlockSpec(memory_space=pltpu.VMEM)] * n,
        out_specs=[pl.BlockSpec(memory_space=pltpu.VMEM)] * n,
        scratch_shapes=[pltpu.VMEM((N_CHIPS, h) + sh[1:], F32) for h, sh in zip(halves, shapes)]
                       + [pltpu.VMEM((N_CHIPS, h) + sh[1:], BF16 if a < n_big else F32)
                          for a, (h, sh) in enumerate(zip(halves, shapes))]
                       + [pltpu.VMEM((N_CHIPS, h) + sh[1:], BF16) for h, sh in zip(halves[:n_big], shapes[:n_big])]
                       + [dma((N_CHIPS * n,)), dma((N_CHIPS * n,)), dma(((N_CHIPS - 1) * n,)), dma(((N_CHIPS - 1) * n,)),
                          dma((n,)), dma((n,))],
        compiler_params=_params(vmem=VMEM_LIMIT),
    )(*parts, small)


def _pre(x, tabs, w_in_ext, w_uq_pad, w_ukv_ext, gains, tm):
    s_len = x.shape[0]
    nt = s_len // tm

    def body(x_ref, ca_ref, sa_ref, cb_ref, sb_ref, win_ref, wuq_ref, wukv_ref,
             gin_ref, gaq_ref, gak_ref, gcq_ref, gckv_ref, gbq_ref, gbk_ref,
             xn_ref, gates_ref, pre_ref, qbpre_ref, kbpre_ref, cq_ref, ckv_ref,
             qa_ref, ka_ref, va_ref, qb_ref, kb_ref, vb_ref, proj):
        xn = _rms_fwd(x_ref[...], gin_ref[...], D_MODEL)
        xn_ref[...] = jnp.transpose(xn).astype(BF16)
        proj[...] = _nn(xn.astype(BF16), win_ref[...])
        gates_ref[...] = proj[:, GA0:GA0 + N_GATE]
        pre_ref[:, 0:VA0] = proj[:, 0:VA0]
        pre_ref[:, VA0:N_PRE] = proj[:, CQ0:KR0]
        ca, sa, cb, sb = ca_ref[...], sa_ref[...], cb_ref[...], sb_ref[...]
        lane = lax.broadcasted_iota(jnp.int32, (tm, LANE), 1)
        for h in range(A_HEADS):
            yq = _rms_fwd(proj[:, QA0 + LANE * h:QA0 + LANE * (h + 1)], gaq_ref[...], A_DIM)
            qa_ref[h] = (_rope_fwd(yq, ca, sa, A_DIM // 4) * (SCALE_A * LOG2E)).astype(BF16)
        for h in range(A_KV):
            yk = _rms_fwd(proj[:, KA0 + LANE * h:KA0 + LANE * (h + 1)], gak_ref[...], A_DIM)
            ka_ref[h] = _rope_fwd(yk, ca, sa, A_DIM // 4).astype(BF16)
            va_ref[h] = jnp.where(lane == A_DIM, 1.0, proj[:, VA0 + LANE * h:VA0 + LANE * (h + 1)]).astype(BF16)
        cq = _rms_fwd(proj[:, CQ0:CQ0 + B_Q_RANK], gcq_ref[...], B_Q_RANK)
        cq_ref[...] = jnp.transpose(cq).astype(BF16)
        qbpre_ref[...] = _nn(cq.astype(BF16), wuq_ref[...])
        ckv = _rms_fwd(proj[:, CKV0:CKV0 + B_KV_RANK], gckv_ref[...], B_KV_RANK)
        ckv_ref[...] = jnp.transpose(ckv).astype(BF16)
        kvb = _nn(ckv.astype(BF16), wukv_ref[...])
        kr = proj[:, KR0:KR0 + LANE]
        for h in range(B_HEADS):
            yq = _rms_fwd(qbpre_ref[:, LANE * h:LANE * (h + 1)], gbq_ref[...], B_QK)
            qb_ref[h] = (_rope_fwd(yq, cb, sb, B_ROPE // 4) * (SCALE_B * LOG2E)).astype(BF16)
            kp = kvb[:, LANE * h:LANE * (h + 1)] + kr
            kbpre_ref[:, LANE * h:LANE * (h + 1)] = kp
            kb_ref[h] = _rope_fwd(_rms_fwd(kp, gbk_ref[...], B_QK), cb, sb, B_ROPE // 4).astype(BF16)
            vb_ref[h, :, 0:LANE] = kvb[:, B_HEADS * LANE + LANE * h:B_HEADS * LANE + LANE * (h + 1)].astype(BF16)
            vb_ref[h, :, LANE:2 * LANE] = jnp.where(lane == 0, 1.0, 0.0).astype(BF16)

    row = lambda w: pl.BlockSpec((tm, w), lambda i: (i, 0))
    col = lambda w: pl.BlockSpec((w, tm), lambda i: (0, i))
    heads = lambda n: pl.BlockSpec((n, tm, LANE), lambda i: (0, i, 0))
    hs = lambda n: jax.ShapeDtypeStruct((n, s_len, LANE), BF16)
    return _pallas(
        body, name="pre", grid=(nt,),
        in_specs=[row(D_MODEL), row(LANE), row(LANE), row(LANE), row(LANE),
                  _full(w_in_ext.shape), _full(w_uq_pad.shape), _full(w_ukv_ext.shape)]
                 + [_full(g.shape) for g in gains],
        out_specs=[col(D_MODEL), row(N_GATE), row(N_PRE), row(B_HEADS * LANE), row(B_HEADS * LANE),
                   col(B_Q_RANK), col(B_KV_RANK),
                   heads(A_HEADS), heads(A_KV), heads(A_KV), heads(B_HEADS), heads(B_HEADS),
                   pl.BlockSpec((B_HEADS, tm, 2 * LANE), lambda i: (0, i, 0))],
        out_shape=[jax.ShapeDtypeStruct((D_MODEL, s_len), BF16), jax.ShapeDtypeStruct((s_len, N_GATE), F32),
                   jax.ShapeDtypeStruct((s_len, N_PRE), F32), jax.ShapeDtypeStruct((s_len, B_HEADS * LANE), F32),
                   jax.ShapeDtypeStruct((s_len, B_HEADS * LANE), F32),
                   jax.ShapeDtypeStruct((B_Q_RANK, s_len), BF16), jax.ShapeDtypeStruct((B_KV_RANK, s_len), BF16),
                   hs(A_HEADS), hs(A_KV), hs(A_KV), hs(B_HEADS), hs(B_HEADS),
                   jax.ShapeDtypeStruct((B_HEADS, s_len, 2 * LANE), BF16)],
        scratch_shapes=[pltpu.VMEM((tm, N_EXT), F32)],
        compiler_params=_params(("parallel",), VMEM_LIMIT),
    )(x, *tabs, w_in_ext, w_uq_pad, w_ukv_ext, *gains)


def _attn_fwd(q, k, v, group, l_col, tq, tk, tiles, name):
    n_heads, s_len, _ = q.shape
    v_w = v.shape[2]
    nk = s_len // tk

    def body(q_ref, k_ref, v_ref, o_ref, lse_ref, s_buf, p_buf, a_buf, m_ref, acc_ref):
        def scores(g, slot):
            s_buf[slot] = _nt(q_ref[_block_rows(g // nk, tq), :], k_ref[_block_rows(g % nk, tk), :])

        def softmax(g, slot):
            t = g // nk
            s = s_buf[slot]
            m_old = m_ref[t]
            m_new = jnp.maximum(m_old, jnp.max(s, axis=-1, keepdims=True))
            m_ref[t] = m_new
            a_buf[slot] = jnp.exp2(m_old - m_new)
            p_buf[slot] = jnp.exp2(s - jnp.tile(m_new, (1, tk // LANE))).astype(BF16)

        def values(g, slot):
            t = g // nk
            pv = _nn(p_buf[slot], v_ref[_block_rows(g % nk, tk), :])
            for c in range(0, v_w, LANE):
                acc_ref[t, :, c:c + LANE] = a_buf[slot] * acc_ref[t, :, c:c + LANE] + pv[:, c:c + LANE]

        m_ref[...] = jnp.full(m_ref.shape, -1e30, F32)
        acc_ref[...] = jnp.zeros(acc_ref.shape, F32)
        _three_stage(tiles * nk, scores, softmax, values)
        for t in range(tiles):
            l = acc_ref[t, :, l_col:l_col + 1]
            o = acc_ref[t, :, 0:LANE] * (1.0 / l)
            if l_col < LANE:
                lane = lax.broadcasted_iota(jnp.int32, o.shape, 1)
                o = jnp.where(lane == l_col, 0.0, o)
            o_ref[t * tq:(t + 1) * tq, :] = o
            lse_ref[t] = jnp.transpose(m_ref[t] + jnp.log2(jnp.broadcast_to(l, (tq, LANE))))[0:1, :]

    return _pallas(
        body, name=name, grid=(n_heads, s_len // (tiles * tq)),
        in_specs=[pl.BlockSpec((None, tiles * tq, LANE), lambda h, i: (h, i, 0)),
                  pl.BlockSpec((None, s_len, LANE), lambda h, i: (h // group, 0, 0)),
                  pl.BlockSpec((None, s_len, v_w), lambda h, i: (h // group, 0, 0))],
        out_specs=[pl.BlockSpec((None, tiles * tq, LANE), lambda h, i: (h, i, 0)),
                   pl.BlockSpec((None, tiles, 1, tq), lambda h, i: (h, i, 0, 0))],
        out_shape=[jax.ShapeDtypeStruct((n_heads, s_len, LANE), F32),
                   jax.ShapeDtypeStruct((n_heads, s_len // tq, 1, tq), F32)],
        scratch_shapes=[pltpu.VMEM((2, tq, tk), F32), pltpu.VMEM((2, tq, tk), BF16), pltpu.VMEM((2, tq, LANE), F32),
                        pltpu.VMEM((tiles, tq, LANE), F32), pltpu.VMEM((tiles, tq, v_w), F32)],
        compiler_params=_params(("parallel", "parallel"), 48 * 1024 * 1024),
    )(q, k, v)


def _mid(x, target, o_a, o_b, gates, w_out_ext, tm):
    s_len = x.shape[0]
    nt = s_len // tm
    n_heads = A_HEADS + B_HEADS

    def body(x_ref, t_ref, oa_ref, ob_ref, g_ref, w_ref,
             yt_ref, dh_ref, dgate_ref, doa_ref, dob_ref, delta_ref, loss_ref, silu_scr, dsilu_scr, y_ref):
        @pl.when(pl.program_id(0) == 0)
        def _():
            loss_ref[...] = jnp.zeros_like(loss_ref)

        def o_of(h):
            return oa_ref[h] if h < A_HEADS else ob_ref[h - A_HEADS]

        for h in range(n_heads):
            cols = slice(LANE * h, LANE * (h + 1))
            g = g_ref[:, cols]
            sig = 1.0 / (1.0 + jnp.exp(-g))
            silu = g * sig
            silu_scr[:, cols] = silu
            dsilu_scr[:, cols] = sig * (1.0 + g * (1.0 - sig))
            y = o_of(h) * silu
            y_ref[:, cols] = y.astype(BF16)
            yt_ref[cols, :] = jnp.transpose(y).astype(BF16)
        err =x_ref[...] + _nn(y_ref[...], w_ref[...]) - t_ref[...]
        sq = jnp.sum(jnp.sum(err * err, axis=-1, keepdims=True), axis=0, keepdims=True)
        loss_ref[...] += jnp.broadcast_to(sq * (0.5 / D_MODEL), loss_ref.shape)
        dh = err * (1.0 / D_MODEL)
        dh_ref[...] = dh
        dy = _nt(dh.astype(BF16), w_ref[...])
        lane = lax.broadcasted_iota(jnp.int32, (tm, LANE), 1)
        delta = jnp.zeros((tm, LANE), F32)
        for h in range(n_heads):
            cols = slice(LANE * h, LANE * (h + 1))
            dyh = dy[:, cols]
            oh = o_of(h)
            do = dyh * silu_scr[:, cols]
            dgate_ref[:, cols] = (dyh * oh * dsilu_scr[:, cols]).astype(BF16)
            delta = jnp.where(lane == h, jnp.sum(do * oh, axis=-1, keepdims=True), delta)
            if h < A_HEADS:
                doa_ref[h] = do.astype(BF16)
            else:
                dob_ref[h - A_HEADS] = do.astype(BF16)
        delta_ref[...] = jnp.transpose(delta)[0:DELTA_ROWS, :]

    row = lambda w: pl.BlockSpec((tm, w), lambda i: (i, 0))
    heads = lambda n, w=LANE: pl.BlockSpec((n, tm, w), lambda i: (0, i, 0))
    return _pallas(
        body, name="mid", grid=(nt,),
        in_specs=[row(D_MODEL), row(D_MODEL), heads(A_HEADS), heads(B_HEADS), row(N_GATE), _full(w_out_ext.shape)],
        out_specs=[pl.BlockSpec((N_GATE, tm), lambda i: (0, i)), row(D_MODEL), row(N_GATE), heads(A_HEADS), heads(B_HEADS),
                   pl.BlockSpec((DELTA_ROWS, tm), lambda i: (0, i)),
                   _full((8, LANE))],
        out_shape=[jax.ShapeDtypeStruct((N_GATE, s_len), BF16), jax.ShapeDtypeStruct((s_len, D_MODEL), F32),
                   jax.ShapeDtypeStruct((s_len, N_GATE), BF16),
                   jax.ShapeDtypeStruct((A_HEADS, s_len, LANE), BF16), jax.ShapeDtypeStruct((B_HEADS, s_len, LANE), BF16),
                   jax.ShapeDtypeStruct((DELTA_ROWS, s_len), F32), jax.ShapeDtypeStruct((8, LANE), F32)],
        scratch_shapes=[pltpu.VMEM((tm, N_GATE), F32), pltpu.VMEM((tm, N_GATE), F32), pltpu.VMEM((tm, N_GATE), BF16)],
        compiler_params=_params(("arbitrary",), VMEM_LIMIT),
    )(x, target, o_a, o_b, gates, w_out_ext)


def _attn_bwd(q, k, v, do, lse, delta, group, tq, tk, name):
    n_heads, s_len, _ = q.shape
    nq = s_len // tq
    assert nq >= 2 and nq % 2 == 0

    def body(q_ref, do_ref, lse_ref, delta_ref, k_ref, v_ref, dq_ref, dk_ref, dv_ref, s_buf, dp_buf, p_buf, ds_buf):
        @pl.when(pl.program_id(1) == 0)
        def _():
            dq_ref[...] = jnp.zeros_like(dq_ref)

        dk_ref[...] = jnp.zeros_like(dk_ref)
        dv_ref[...] = jnp.zeros_like(dv_ref)

        def rows(i):
            return _block_rows(i, tq)

        def scores(i, slot):
            s_buf[slot] = _nt(k_ref[...], q_ref[rows(i), :])
            dp_buf[slot] = _nt(v_ref[...], do_ref[rows(i), :])

        def elementwise(i, slot):
            p = jnp.exp2(s_buf[slot] - lse_ref[i])
            p_buf[slot] = p.astype(BF16)
            ds_buf[slot] = (p * (dp_buf[slot] - delta_ref[i])).astype(BF16)

        def grads(i, slot):
            dv_ref[...] += _nn(p_buf[slot], do_ref[rows(i), :])
            dk_ref[...] += _nn(ds_buf[slot], q_ref[rows(i), :])
            dq_ref[rows(i), :] += _tn(ds_buf[slot], k_ref[...])

        _three_stage(nq, scores, elementwise, grads)

    whole = lambda: pl.BlockSpec((None, s_len, LANE), lambda h, j: (h, 0, 0))
    stat = lambda: pl.BlockSpec((None, nq, 1, tq), lambda h, j: (h, 0, 0, 0))
    kvb = lambda: pl.BlockSpec((None, tk, LANE), lambda h, j: (h // group, j, 0))
    outb = lambda: pl.BlockSpec((None, tk, LANE), lambda h, j: (h, j, 0))
    shape = jax.ShapeDtypeStruct((n_heads, s_len, LANE), F32)
    return _pallas(
        body, name=name, grid=(n_heads, s_len // tk),
        in_specs=[whole(), whole(), stat(), stat(), kvb(), kvb()],
        out_specs=[whole(), outb(), outb()],
        out_shape=[shape, shape, shape],
        scratch_shapes=[pltpu.VMEM((2, tk, tq), F32), pltpu.VMEM((2, tk, tq), F32),
                        pltpu.VMEM((2, tk, tq), BF16), pltpu.VMEM((2, tk, tq), BF16)],
        compiler_params=_params(("parallel", "arbitrary"), 48 * 1024 * 1024),
    )(q, do, lse, delta, k, v)


def _post(x, dh, pre, qbpre, kbpre, dgate, dqa, dka, dva, dqb, dkb, dvb, loss_part, tabs,
          w_in_ext, w_uq_pad, w_ukv_ext, gains, tm):
    s_len = x.shape[0]
    nt = s_len // tm

    def body(x_ref, dh_ref, pre_ref, qbpre_ref, kbpre_ref, dgate_ref,
             dqa_ref, dka_ref, dva_ref, dqb_ref, dkb_ref, dvb_ref, loss_ref,
             ca_ref, sa_ref, cb_ref, sb_ref, win_ref, wuq_ref, wukv_ref,
             gin_ref, gaq_ref, gak_ref, gcq_ref, gckv_ref, gbq_ref, gbk_ref,
             gx_ref, dproj_ref, dqbpre_ref, dkvb_ref, dsm_ref):
        @pl.when(pl.program_id(0) == 0)
        def _():
            dsm_ref[...] = jnp.zeros_like(dsm_ref)
            dsm_ref[SM_LOSS:SM_LOSS + 1, 0:LANE] = loss_ref[0:1, :]

        def add_small(r, dg):
            dsm_ref[r:r + 1, 0:dg.shape[1]] += dg

        def tok_sum(a):
            return jnp.sum(a, axis=0, keepdims=True)

        ca, sa, cb, sb = ca_ref[...], sa_ref[...], cb_ref[...], sb_ref[...]
        lane = lax.broadcasted_iota(jnp.int32, (tm, LANE), 1)

        def back(c0, c1):
            return _nt(dproj_ref[:, c0:c1], win_ref[:, c0:c1])

        dproj_ref[:, GA0:GA0 + N_GATE] = dgate_ref[...]
        dxn = back(GA0, GA0 + N_GATE)
        dg = jnp.zeros((1, LANE), F32)
        for h in range(A_HEADS):
            dn = _rope_bwd(dqa_ref[h] * SCALE_A, ca, sa, A_DIM // 4)
            dx, dgr = _rms_bwd(dn, pre_ref[:, QA0 + LANE * h:QA0 + LANE * (h + 1)], gaq_ref[...], A_DIM)
            dproj_ref[:, QA0 + LANE * h:QA0 + LANE * (h + 1)] = dx.astype(BF16)
            dg = dg + tok_sum(dgr)
        add_small(SM_AQ, dg)
        dxn = dxn + back(QA0, KA0)
        dg = jnp.zeros((1, LANE), F32)
        for h in range(A_KV):
            dk = dka_ref[A_GROUP * h]
            dv = dva_ref[A_GROUP * h]
            for g in range(1, A_GROUP):
                dk = dk + dka_ref[A_GROUP * h + g]
                dv = dv + dva_ref[A_GROUP * h + g]
            dn = _rope_bwd(dk * LN2, ca, sa, A_DIM // 4)
            dx, dgr = _rms_bwd(dn, pre_ref[:, KA0 + LANE * h:KA0 + LANE * (h + 1)], gak_ref[...], A_DIM)
            dproj_ref[:, KA0 + LANE * h:KA0 + LANE * (h + 1)] = dx.astype(BF16)
            dproj_ref[:, VA0 + LANE * h:VA0 + LANE * (h + 1)] = dv.astype(BF16)
            dg = dg + tok_sum(dgr)
        add_small(SM_AK, dg)
        dxn = dxn + back(KA0, GA0)
        dg = jnp.zeros((1, LANE), F32)
        for h in range(B_HEADS):
            cols = slice(LANE * h, LANE * (h + 1))
            dn = _rope_bwd(dqb_ref[h] * SCALE_B, cb, sb, B_ROPE // 4)
            dx, dgr = _rms_bwd(dn, qbpre_ref[:, cols], gbq_ref[...], B_QK)
            dqbpre_ref[:, cols] = dx.astype(BF16)
            dg = dg + tok_sum(dgr)
        add_small(SM_BQ, dg)
        dcq = _nt(dqbpre_ref[...], wuq_ref[...])
        dx, dgr = _rms_bwd(dcq, pre_ref[:, VA0:VA0 + B_Q_RANK], gcq_ref[...], B_Q_RANK)
        dproj_ref[:, CQ0:CQ0 + B_Q_RANK] = dx.astype(BF16)
        add_small(SM_CQ, tok_sum(dgr))
        dxn = dxn + back(CQ0, CKV0)
        dg = jnp.zeros((1, LANE), F32)
        dkr = jnp.zeros((tm, LANE), F32)
        for h in range(B_HEADS):
            cols = slice(LANE * h, LANE * (h + 1))
            dn = _rope_bwd(dkb_ref[h] * LN2, cb, sb, B_ROPE // 4)
            dx, dgr = _rms_bwd(dn, kbpre_ref[:, cols], gbk_ref[...], B_QK)
            dkvb_ref[:, cols] = jnp.where(lane < B_NOPE, dx, 0.0).astype(BF16)
            dkvb_ref[:, B_HEADS * LANE + LANE * h:B_HEADS * LANE + LANE * (h + 1)] = dvb_ref[h].astype(BF16)
            dkr = dkr + dx
            dg = dg + tok_sum(dgr)
        add_small(SM_BK, dg)
        dproj_ref[:, KR0:KR0 + LANE] = jnp.where((lane >= B_NOPE) & (lane < B_QK), dkr, 0.0).astype(BF16)
        dckv = _nt(dkvb_ref[...], wukv_ref[...])
        dx, dgr = _rms_bwd(dckv, pre_ref[:, VA0 + B_Q_RANK:N_PRE], gckv_ref[...], B_KV_RANK)
        dproj_ref[:, CKV0:CKV0 + B_KV_RANK] = dx.astype(BF16)
        add_small(SM_CKV, tok_sum(dgr))
        dxn = dxn + back(CKV0, N_EXT)
        dx, dgr = _rms_bwd(dxn, x_ref[...], gin_ref[...], D_MODEL)
        gx_ref[...] = dh_ref[...] + dx
        add_small(SM_IN, tok_sum(dgr))

    row = lambda w: pl.BlockSpec((tm, w), lambda i: (i, 0))
    heads = lambda n: pl.BlockSpec((n, tm, LANE), lambda i: (0, i, 0))
    return _pallas(
        body, name="post", grid=(nt,),
        in_specs=[row(D_MODEL), row(D_MODEL), row(N_PRE), row(B_HEADS * LANE), row(B_HEADS * LANE), row(N_GATE),
                  heads(A_HEADS), heads(A_HEADS), heads(A_HEADS), heads(B_HEADS), heads(B_HEADS), heads(B_HEADS),
                  _full(loss_part.shape), row(LANE), row(LANE), row(LANE), row(LANE),
                  _full(w_in_ext.shape), _full(w_uq_pad.shape), _full(w_ukv_ext.shape)]
                 + [_full(g.shape) for g in gains],
        out_specs=[row(D_MODEL), row(N_EXT), row(B_HEADS * LANE), row(2 * B_HEADS * LANE), _full((SM_ROWS, SM_W))],
        out_shape=[jax.ShapeDtypeStruct((s_len, D_MODEL), F32), jax.ShapeDtypeStruct((s_len, N_EXT), BF16),
                   jax.ShapeDtypeStruct((s_len, B_HEADS * LANE), BF16),
                   jax.ShapeDtypeStruct((s_len, 2 * B_HEADS * LANE), BF16),
                   jax.ShapeDtypeStruct((SM_ROWS, SM_W), F32)],
        compiler_params=_params(("arbitrary",), VMEM_LIMIT),
    )(x, dh, pre, qbpre, kbpre, dgate, dqa, dka, dva, dqb, dkb, dvb, loss_part, *tabs,
      w_in_ext, w_uq_pad, w_ukv_ext, *gains)


def _grad_w(a_t, b, tn, ts, name):
    m, s_len = a_t.shape
    n = b.shape[1]

    def body(a_ref, b_ref, o_ref):
        @pl.when(pl.program_id(1) == 0)
        def _():
            o_ref[...] = jnp.zeros_like(o_ref)

        o_ref[...] += _nn(a_ref[...], b_ref[...].astype(BF16))

    return _pallas(
        body, name=name, grid=(n // tn, s_len // ts),
        in_specs=[pl.BlockSpec((m, ts), lambda j, t: (0, t)), pl.BlockSpec((ts, tn), lambda j, t: (t, j))],
        out_specs=pl.BlockSpec((m, tn), lambda j, t: (0, j)),
        out_shape=jax.ShapeDtypeStruct((m, n), F32),
        compiler_params=_params(("parallel", "arbitrary"), 48 * 1024 * 1024),
    )(a_t, b)


def _adam_math(w, g, m, v):
    nm = ADAM_B1 * m + (1.0 - ADAM_B1) * g
    nv = ADAM_B2 * v + (1.0 - ADAM_B2) * (g * g)
    m_hat = nm / (1.0 - ADAM_B1 ** ADAM_STEP)
    v_hat = nv / (1.0 - ADAM_B2 ** ADAM_STEP)
    return -ADAM_LR * (m_hat / (jnp.sqrt(v_hat) + ADAM_EPS) + ADAM_WD * w), nm, nv


def _adamw_rows(w, g, m, v, tr):
    rows, cols = w.shape

    def body(w_ref, g_ref, m_ref, v_ref, d_ref, nm_ref, nv_ref):
        d_ref[...], nm_ref[...], nv_ref[...] = _adam_math(w_ref[...], g_ref[...], m_ref[...], v_ref[...])

    blk = pl.BlockSpec((tr, cols), lambda i: (i, 0))
    shape = jax.ShapeDtypeStruct((rows, cols), F32)
    return _pallas(
        body, name="adamw_w_in", grid=(rows // tr,),
        in_specs=[blk] * 4, out_specs=[blk] * 3, out_shape=[shape] * 3,
        compiler_params=_params(("parallel",), 32 * 1024 * 1024),
    )(w, g, m, v)


def _adamw_rest(bigs, smalls, g_small):
    nb, ns = len(bigs), len(smalls)

    def body(*refs):
        ins, outs = refs[:4 * nb + 3 * ns + 1], refs[4 * nb + 3 * ns + 1:]
        for i in range(nb):
            w_ref, g_ref, m_ref, v_ref = ins[4 * i:4 * i + 4]
            d_ref, nm_ref, nv_ref = outs[3 * i:3 * i + 3]
            d_ref[...], nm_ref[...], nv_ref[...] = _adam_math(w_ref[...], g_ref[...], m_ref[...], v_ref[...])
        gs_ref = ins[-1]
        for i in range(ns):
            w_ref, m_ref, v_ref = ins[4 * nb + 3 * i:4 * nb + 3 * i + 3]
            g_ref, d_ref, nm_ref, nv_ref = outs[3 * nb + 4 * i:3 * nb + 4 * i + 4]
            g = gs_ref[i:i + 1, 0:w_ref.shape[1]]
            g_ref[...] = g
            d_ref[...], nm_ref[...], nv_ref[...] = _adam_math(w_ref[...], g, m_ref[...], v_ref[...])

    flat_in = [a for quad in bigs for a in quad] + [a for tri in smalls for a in tri] + [g_small]
    out_shape = ([jax.ShapeDtypeStruct(q[0].shape, F32) for q in bigs for _ in range(3)]
                 + [jax.ShapeDtypeStruct(t[0].shape, F32) for t in smalls for _ in range(4)])
    return _pallas(
        body, name="adamw_rest",
        in_specs=[pl.BlockSpec(memory_space=pltpu.VMEM)] * len(flat_in),
        out_specs=[pl.BlockSpec(memory_space=pltpu.VMEM)] * len(out_shape),
        out_shape=out_shape,
        compiler_params=_params(vmem=32 * 1024 * 1024),
    )(*flat_in)


def _rope_tables(s_len):
    rows = s_len // GRID_W
    row = jnp.arange(rows, dtype=F32)
    col = jnp.arange(GRID_W, dtype=F32)

    def lay(dim, lead):
        half = dim // 2
        inv = 1.0 / (ROPE_THETA ** (jnp.arange(0, half, 2, dtype=F32) / half))
        ang_r, ang_c = row[:, None] * inv[None, :], col[:, None] * inv[None, :]
        tail = LANE - lead - dim

        def place(a, b, at, n, fill=0.0):
            return jnp.concatenate([jnp.full((n, lead), fill if at else 0.0, F32) if lead else jnp.zeros((n, 0), F32),
                                    jnp.zeros((n, at * half), F32), a, b,
                                    jnp.zeros((n, (1 - at) * half + tail), F32)], axis=1)

        cos_r = place(jnp.cos(ang_r), jnp.cos(ang_r), 0, rows)
        cos_c = place(jnp.cos(ang_c), jnp.cos(ang_c), 1, GRID_W, 1.0)
        sin_r = place(-jnp.sin(ang_r), jnp.sin(ang_r), 0, rows)
        sin_c = place(-jnp.sin(ang_c), jnp.sin(ang_c), 1, GRID_W)
        assert cos_r.shape[1] == LANE
        cos = (cos_r[:, None, :] + cos_c[None, :, :]).reshape(s_len, LANE)
        sin = (sin_r[:, None, :] + sin_c[None, :, :]).reshape(s_len, LANE)
        return cos, sin

    cos_a, sin_a = lay(A_DIM, 0)
    cos_b, sin_b = lay(B_ROPE, B_NOPE)
    return cos_a, sin_a, cos_b, sin_b


def _pad_heads(w, n_heads, dim, axis):
    shape = w.shape[:axis] + (n_heads, dim) + w.shape[axis + 1:]
    pad = [(0, 0)] * len(shape)
    pad[axis + 1] = (0, LANE - dim)
    out = jnp.pad(w.reshape(shape), pad)
    return out.reshape(w.shape[:axis] + (n_heads * LANE,) + w.shape[axis + 1:])


def _unpad_heads(w, n_heads, dim, axis):
    shape = w.shape[:axis] + (n_heads, LANE) + w.shape[axis + 1:]
    out = lax.slice_in_dim(w.reshape(shape), 0, dim, axis=axis + 1)
    return out.reshape(w.shape[:axis] + (n_heads * dim,) + w.shape[axis + 1:])


def _pad_vec(g):
    return jnp.pad(g, ((0, 0), (0, LANE - g.shape[1])))


def _ext_weights(g_in, g_uq, g_ukv, g_out):
    w_in = g_in.transpose(1, 0, 2).reshape(D_MODEL, N_IN)
    w_uq = g_uq.reshape(B_Q_RANK, B_HEADS * B_QK)
    w_ukv = g_ukv.transpose(1, 0, 2).reshape(B_KV_RANK, B_HEADS * (B_NOPE + B_V))
    w_out = g_out.reshape(D_MODEL, D_MODEL)
    a_w = A_HEADS * A_DIM
    kv_w = A_KV * A_DIM
    o = 0
    secs = []
    for width, heads in ((a_w, A_HEADS), (kv_w, A_KV), (kv_w, A_KV), (a_w, A_HEADS)):
        secs.append(_pad_heads(w_in[:, o:o + width], heads, A_DIM, 1))
        o += width
    a_q, a_k, a_v, a_g = secs
    b_cq = w_in[:, o:o + B_Q_RANK]
    o += B_Q_RANK
    b_ckv = w_in[:, o:o + B_KV_RANK]
    o += B_KV_RANK
    b_kr = jnp.pad(w_in[:, o:o + B_ROPE], ((0, 0), (B_NOPE, LANE - B_QK)))
    o += B_ROPE
    b_g = w_in[:, o:]
    w_in_ext = jnp.concatenate([a_q, a_k, a_v, a_g, b_g, b_cq, b_ckv, b_kr], axis=1)
    w_uq_pad = _pad_heads(w_uq, B_HEADS, B_QK, 1)
    kv3 = w_ukv.reshape(B_KV_RANK, B_HEADS, B_NOPE + B_V)
    w_ukv_ext = jnp.concatenate([_pad_heads(kv3[:, :, :B_NOPE].reshape(B_KV_RANK, B_HEADS * B_NOPE), B_HEADS, B_NOPE, 1),
                                 kv3[:, :, B_NOPE:].reshape(B_KV_RANK, B_HEADS * B_V)], axis=1)
    w_out_ext = jnp.concatenate([_pad_heads(w_out[:a_w], A_HEADS, A_DIM, 0), w_out[a_w:]], axis=0)
    return w_in_ext, w_uq_pad, w_ukv_ext, w_out_ext


def _fold_grads(d_in_ext, d_uq_pad, d_ukv_ext, d_out_ext):
    d_in = jnp.concatenate([
        _unpad_heads(d_in_ext[:, QA0:KA0], A_HEADS, A_DIM, 1),
        _unpad_heads(d_in_ext[:, KA0:VA0], A_KV, A_DIM, 1),
        _unpad_heads(d_in_ext[:, VA0:GA0], A_KV, A_DIM, 1),
        _unpad_heads(d_in_ext[:, GA0:GB0], A_HEADS, A_DIM, 1),
        d_in_ext[:, CQ0:KR0],
        d_in_ext[:, KR0 + B_NOPE:KR0 + B_QK],
        d_in_ext[:, GB0:CQ0]], axis=1)
    d_uq = _unpad_heads(d_uq_pad, B_HEADS, B_QK, 1)
    k3 = _unpad_heads(d_ukv_ext[:, :B_HEADS * LANE], B_HEADS, B_NOPE, 1).reshape(B_KV_RANK, B_HEADS, B_NOPE)
    v3 = d_ukv_ext[:, B_HEADS * LANE:].reshape(B_KV_RANK, B_HEADS, B_V)
    d_ukv = jnp.concatenate([k3, v3], axis=2).reshape(B_KV_RANK, B_HEADS * (B_NOPE + B_V))
    d_out = jnp.concatenate([_unpad_heads(d_out_ext[:A_HEADS * LANE], A_HEADS, A_DIM, 0), d_out_ext[A_HEADS * LANE:]], axis=0)
    return (d_in.reshape(D_MODEL, N_CHIPS, SH_IN[1]).transpose(1, 0, 2),
            d_uq.reshape((N_CHIPS,) + SH_UQ),
            d_ukv.reshape(B_KV_RANK, N_CHIPS, SH_UKV[1]).transpose(1, 0, 2),
            d_out.reshape((N_CHIPS,) + SH_OUT))


def kernel(x, norm_in, w_in, a_q_norm, a_k_norm, b_cq_norm, b_ckv_norm, w_uq, w_ukv, b_q_norm, b_k_norm, w_out, loss_target, m_norm_in, m_w_in, m_a_q_norm, m_a_k_norm, m_b_cq_norm, m_b_ckv_norm, m_w_uq, m_w_ukv, m_b_q_norm, m_b_k_norm, m_w_out, v_norm_in, v_w_in, v_a_q_norm, v_a_k_norm, v_b_cq_norm, v_b_ckv_norm, v_w_uq, v_w_ukv, v_b_q_norm, v_b_k_norm, v_w_out):
    s_len = x.shape[1]
    xs, ts = x[0], loss_target[0]
    tm = min(256, s_len)
    tq, tk_f = min(512, s_len // 2), min(1024, s_len // 2)
    tq_b, tk_b = min(1024, s_len // 2), min(512, s_len)
    tiles_f = min(2, s_len // tq)

    w_in_ext, w_uq_pad, w_ukv_ext, w_out_ext = _ext_weights(*_gather_weights((w_in[0], w_uq[0], w_ukv[0], w_out[0])))
    gains = (norm_in, _pad_vec(a_q_norm), _pad_vec(a_k_norm), b_cq_norm, b_ckv_norm, _pad_vec(b_q_norm), _pad_vec(b_k_norm))
    tabs = _rope_tables(s_len)

    (xn_t, gates, pre, qbpre, kbpre, cq_t, ckv_t, qa, ka, va, qb, kb, vb) = _pre(
        xs, tabs, w_in_ext, w_uq_pad, w_ukv_ext, gains, tm)
    o_a, lse_a = _attn_fwd(qa, ka, va, A_GROUP, A_DIM, tq, tk_f, tiles_f, "attn_fwd_a")
    o_b, lse_b = _attn_fwd(qb, kb, vb, 1, B_V, tq, tk_f, tiles_f, "attn_fwd_b")
    y_t, dh, dgate, do_a, do_b, delta, loss_part = _mid(xs, ts, o_a, o_b, gates, w_out_ext, tm)

    def stat(a):
        return a.reshape(a.shape[0], s_len // tq_b, 1, tq_b)

    dqa, dka, dva = _attn_bwd(qa, ka, va, do_a, stat(lse_a), stat(delta[:A_HEADS]), A_GROUP, tq_b, tk_b, "attn_bwd_a")
    dqb, dkb, dvb = _attn_bwd(qb, kb, vb, do_b, stat(lse_b), stat(delta[A_HEADS:A_HEADS + B_HEADS]), 1, tq_b, tk_b,
                              "attn_bwd_b")
    grad_x, dproj, dqbpre, dkvb, d_small = _post(
        xs, dh, pre, qbpre, kbpre, dgate, dqa, dka, dva, dqb, dkb, dvb, loss_part, tabs,
        w_in_ext, w_uq_pad, w_ukv_ext, gains, tm)

    ts_w = min(2048, s_len)
    d_in_ext = _grad_w(xn_t, dproj, 768, ts_w, "grad_w_in")
    d_out_ext = _grad_w(y_t, dh, 512, ts_w, "grad_w_out")
    d_uq_pad = _grad_w(cq_t, dqbpre, 512, ts_w, "grad_w_uq")
    d_ukv_ext = _grad_w(ckv_t, dkvb, 1024, ts_w, "grad_w_ukv")

    g_in, g_uq, g_ukv, g_out, g_small = _reduce_grads(_fold_grads(d_in_ext, d_uq_pad, d_ukv_ext, d_out_ext), d_small)
    d_in, nm_in, nv_in = _adamw_rows(w_in[0], g_in, m_w_in[0], v_w_in[0], 256)
    rest = _adamw_rest(
        [(w_uq[0], g_uq, m_w_uq[0], v_w_uq[0]), (w_ukv[0], g_ukv, m_w_ukv[0], v_w_ukv[0]),
         (w_out[0], g_out, m_w_out[0], v_w_out[0])],
        [(norm_in, m_norm_in, v_norm_in), (a_q_norm, m_a_q_norm, v_a_q_norm), (a_k_norm, m_a_k_norm, v_a_k_norm),
         (b_cq_norm, m_b_cq_norm, v_b_cq_norm), (b_ckv_norm, m_b_ckv_norm, v_b_ckv_norm),
         (b_q_norm, m_b_q_norm, v_b_q_norm), (b_k_norm, m_b_k_norm, v_b_k_norm)], g_small)
    (d_uq, nm_uq, nv_uq), (d_ukv, nm_ukv, nv_ukv), (d_out, nm_out, nv_out) = (rest[3 * i:3 * i + 3] for i in range(3))
    sm = [rest[9 + 4 * i:9 + 4 * i + 4] for i in range(7)]

    def leaves(k, p_in, p_uq, p_ukv, p_out):
        return [sm[SM_IN][k], p_in[None], sm[SM_AQ][k], sm[SM_AK][k], sm[SM_CQ][k], sm[SM_CKV][k], p_uq[None], p_ukv[None],
                sm[SM_BQ][k], sm[SM_BK][k], p_out[None]]

    return (g_small[SM_LOSS, 0], grad_x[None], *leaves(0, g_in, g_uq, g_ukv, g_out), *leaves(1, d_in, d_uq, d_ukv, d_out),
            *leaves(2, nm_in, nm_uq, nm_ukv, nm_out), *leaves(3, nv_in, nv_uq, nv_ukv, nv_out))
```

```python
import jax
import jax.numpy as jnp
import numpy as np
from jax import lax
from jax.experimental import pallas as pl
from jax.experimental.pallas import tpu as pltpu

F32 = jnp.float32
BF16 = jnp.bfloat16
MESH = pl.DeviceIdType.MESH

D_MODEL = 1024
GRID_W = 64
ROPE_THETA = 10000.0
EPS = 1e-6
A_HEADS, A_KV, A_DIM = 8, 2, 64
A_GROUP = A_HEADS // A_KV
B_HEADS, B_NOPE, B_ROPE, B_V = 4, 64, 32, 128
B_QK = B_NOPE + B_ROPE
B_Q_RANK, B_KV_RANK = 384, 256
N_IN = 2464
SCALE_A = 1.0 / float(np.sqrt(A_DIM))
SCALE_B = 1.0 / float(np.sqrt(B_QK))
LOG2E = float(np.log2(np.e))
LN2 = float(np.log(2.0))
ADAM_LR, ADAM_B1, ADAM_B2, ADAM_EPS, ADAM_WD, ADAM_STEP = 0.001, 0.9, 0.999, 1e-08, 0.01, 10

LANE = 128
VMEM_BYTES = 64 * 1024 * 1024
VMEM_LIMIT = VMEM_BYTES - 8 * 1024 * 1024

QA0 = 0
KA0 = QA0 + A_HEADS * LANE
VA0 = KA0 + A_KV * LANE
GA0 = VA0 + A_KV * LANE
GB0 = GA0 + A_HEADS * LANE
CQ0 = GB0 + B_HEADS * LANE
CKV0 = CQ0 + B_Q_RANK
KR0 = CKV0 + B_KV_RANK
N_EXT = KR0 + LANE
N_GATE = (A_HEADS + B_HEADS) * LANE
DELTA_ROWS = 16
N_PRE = KA0 + A_KV * LANE + B_Q_RANK + B_KV_RANK

N_CHIPS = 4
SH_IN = (D_MODEL, N_IN // N_CHIPS)
SH_UQ = (B_Q_RANK // N_CHIPS, B_HEADS * B_QK)
SH_UKV = (B_KV_RANK, B_HEADS * (B_NOPE + B_V) // N_CHIPS)
SH_OUT = (D_MODEL // N_CHIPS, D_MODEL)
SM_ROWS, SM_W = 16, D_MODEL
SM_IN, SM_AQ, SM_AK, SM_CQ, SM_CKV, SM_BQ, SM_BK, SM_LOSS = range(8)
F32_ROWS, BF16_ROWS = 8, 16


def _pallas(body, **kw):
    return pl.pallas_call(body, **kw)


def _params(sem=None, vmem=None):
    return pltpu.CompilerParams(dimension_semantics=sem, vmem_limit_bytes=vmem)


def _rms_fwd(x, g, n):
    r = lax.rsqrt(jnp.sum(x * x, axis=-1, keepdims=True) * (1.0 / n) + EPS)
    return x * r * g


def _rms_bwd(dy, x, g, n):
    u = dy * g
    r = lax.rsqrt(jnp.sum(x * x, axis=-1, keepdims=True) * (1.0 / n) + EPS)
    ux = jnp.sum(u * x, axis=-1, keepdims=True)
    xhat = x * r
    dx = r * (u - xhat * (r * ux * (1.0 / n)))
    return dx, dy * xhat


def _partner(y, half):
    lane = lax.broadcasted_iota(jnp.int32, y.shape, 1)
    first = (lane % (2 * half)) < half
    return jnp.where(first, pltpu.roll(y, LANE - half, 1), pltpu.roll(y, half, 1))


def _rope_fwd(y, cos, sin, half):
    return y * cos + _partner(y, half) * sin


def _rope_bwd(d, cos, sin, half):
    return d * cos - _partner(d, half) * sin


def _nt(a, b):
    return lax.dot_general(a, b, (((1,), (1,)), ((), ())), preferred_element_type=F32)


def _tn(a, b):
    return lax.dot_general(a, b, (((0,), (0,)), ((), ())), preferred_element_type=F32)


def _nn(a, b):
    return jnp.dot(a, b, preferred_element_type=F32)


def _block_rows(i, size):
    if isinstance(i, int):
        return pl.ds(i * size, size)
    return pl.ds(pl.multiple_of(i * size, size), size)


MAX_STATIC_BLOCKS = 16


def _three_stage(n, first, second, third):
    assert n >= 2 and n % 2 == 0
    first(0, 0)
    first(1, 1)
    second(0, 0)
    if n <= MAX_STATIC_BLOCKS:
        for i in range(1, n - 1):
            first(i + 1, (i + 1) % 2)
            second(i, i % 2)
            third(i - 1, (i - 1) % 2)
    else:
        def pair(t, carry):
            i = 2 * t + 1
            first(i + 1, 0)
            second(i, 1)
            third(i - 1, 0)
            first(i + 2, 1)
            second(i + 1, 0)
            third(i, 1)
            return carry

        lax.fori_loop(0, (n - 2) // 2, pair, 0)
    second(n - 1, 1)
    third(n - 2, 0)
    third(n - 1, 1)


def _full(shape):
    return pl.BlockSpec(shape, lambda *_: (0,) * len(shape))


def _gather_weights(shards):
    n = len(shards)
    halves = [w.shape[0] // 2 for w in shards]

    def body(*refs):
        w_refs, out_refs, (send_sems, recv_sems) = refs[:n], refs[n:2 * n], refs[2 * n:]
        x, y, c = lax.axis_index("x"), lax.axis_index("y"), lax.axis_index("c")
        sibling = (x, y, 1 - c)
        chips = [(1 - x, y), (x, 1 - y), (1 - x, 1 - y)]
        me = 2 * x + y

        def copy(a, k, j, hc, to):
            part = out_refs[a].at[j, pl.ds(pl.multiple_of(hc * halves[a], BF16_ROWS), halves[a]), :]
            return pltpu.make_async_remote_copy(
                src_ref=part, dst_ref=part, send_sem=send_sems.at[6 * a + k], recv_sem=recv_sems.at[6 * a + k],
                device_id=to, device_id_type=MESH)

        started = []
        for a in range(n):
            out_refs[a][me] = w_refs[a][...].astype(BF16)
            for k, chip in enumerate(chips):
                started.append(copy(a, k, me, c, (*chip, c)))
                started[-1].start()
        for k, chip in enumerate(chips):
            for a in range(n):
                copy(a, k, 2 * chip[0] + chip[1], c, (*chip, c)).wait_recv()
                started.append(copy(a, 3 + k, 2 * chip[0] + chip[1], c, sibling))
                started[-1].start()
        for k, chip in enumerate(chips):
            for a in range(n):
                copy(a, 3 + k, 2 * chip[0] + chip[1], 1 - c, sibling).wait_recv()
        for cp in started:
            cp.wait_send()

    return _pallas(
        body, name="gather_weights",
        out_shape=[jax.ShapeDtypeStruct((N_CHIPS,) + w.shape, BF16) for w in shards],
        in_specs=[pl.BlockSpec(memory_space=pltpu.VMEM)] * n,
        out_specs=[pl.BlockSpec(memory_space=pltpu.VMEM)] * n,
        scratch_shapes=[pltpu.SemaphoreType.DMA((6 * n,)), pltpu.SemaphoreType.DMA((6 * n,))],
        compiler_params=_params(vmem=32 * 1024 * 1024),
    )(*shards)


def _reduce_grads(parts, small):
    n_big = len(parts)
    n = n_big + 1
    shapes = [p.shape[1:] for p in parts] + [small.shape]
    halves = [sh[0] // 2 for sh in shapes]

    def body(*refs):
        p_refs, out_refs, rec_a, rec_b = refs[:n], refs[n:2 * n], refs[2 * n:3 * n], refs[3 * n:4 * n]
        send_b = refs[4 * n:4 * n + n_big]
        sa_send, sa_recv, sb_send, sb_recv, sc_send, sc_recv = refs[4 * n + n_big:]
        x, y, c = lax.axis_index("x"), lax.axis_index("y"), lax.axis_index("c")
        sibling = (x, y, 1 - c)
        me = 2 * x + y

        def rows(a, hc):
            return pl.ds(pl.multiple_of(hc * halves[a], F32_ROWS), halves[a])

        def partial(a, j, hc):
            return p_refs[a].at[j, rows(a, hc), :] if a < n_big else p_refs[a].at[rows(a, hc), :]

        def copy_a(a, j):
            return pltpu.make_async_remote_copy(
                src_ref=partial(a, j, 1 - c), dst_ref=rec_a[a].at[j],
                send_sem=sa_send.at[N_CHIPS * a + j], recv_sem=sa_recv.at[N_CHIPS * a + j],
                device_id=sibling, device_id_type=MESH)

        def copy_b(a, r):
            j = me ^ r
            k = (N_CHIPS - 1) * a + r - 1
            return pltpu.make_async_remote_copy(
                src_ref=(send_b[a] if a < n_big else rec_a[a]).at[j], dst_ref=rec_b[a].at[r],
                send_sem=sb_send.at[k], recv_sem=sb_recv.at[k], device_id=(j // 2, j % 2, c), device_id_type=MESH)

        def copy_c(a):
            return pltpu.make_async_remote_copy(
                src_ref=out_refs[a].at[rows(a, c), :], dst_ref=out_refs[a].at[rows(a, c), :],
                send_sem=sc_send.at[a], recv_sem=sc_recv.at[a], device_id=sibling, device_id_type=MESH)

        for a in range(n):
            for j in range(N_CHIPS):
                copy_a(a, j).start()
        for r in range(1, N_CHIPS):
            j = me ^ r
            for a in range(n):
                copy_a(a, j).wait_recv()
                chip_part = rec_a[a][j] + partial(a, j, c)[...]
                if a < n_big:
                    send_b[a][j] = chip_part.astype(BF16)
                else:
                    rec_a[a][j] = chip_part
                copy_b(a, r).start()
        for a in range(n):
            copy_a(a, me).wait_recv()
            rec_b[a][0] = (rec_a[a][me] + partial(a, me, c)[...]).astype(rec_b[a].dtype)
        for a in range(n):
            for r in range(1, N_CHIPS):
                copy_b(a, r).wait_recv()
            total = rec_b[a][me].astype(F32)
            for j in range(1, N_CHIPS):
                total = total + rec_b[a][j ^ me].astype(F32)
            out_refs[a][rows(a, c), :] = total
            copy_c(a).start()
        for a in range(n):
            copy_c(a).wait_recv()
        for a in range(n):
            for j in range(N_CHIPS):
                copy_a(a, j).wait_send()
            for r in range(1, N_CHIPS):
                copy_b(a, r).wait_send()
            copy_c(a).wait_send()

    dma = pltpu.SemaphoreType.DMA
    return _pallas(
        body, name="reduce_grads",
        out_shape=[jax.ShapeDtypeStruct(sh, F32) for sh in shapes],
        in_specs=[pl.BlockSpec(memory_space=pltpu.VMEM)] * n,
        out_specs=[pl.BlockSpec(memory_space=pltpu.VMEM)] * n,
        scratch_shapes=[pltpu.VMEM((N_CHIPS, h) + sh[1:], F32) for h, sh in zip(halves, shapes)]
                       + [pltpu.VMEM((N_CHIPS, h) + sh[1:], BF16 if a < n_big else F32)
                          for a, (h, sh) in enumerate(zip(halves, shapes))]
                       + [pltpu.VMEM((N_CHIPS, h) + sh[1:], BF16) for h, sh in zip(halves[:n_big], shapes[:n_big])]
                       + [dma((N_CHIPS * n,)), dma((N_CHIPS * n,)), dma(((N_CHIPS - 1) * n,)), dma(((N_CHIPS - 1) * n,)),
                          dma((n,)), dma((n,))],
        compiler_params=_params(vmem=VMEM_LIMIT),
    )(*parts, small)


def _pre(x, tabs, w_in_ext, w_uq_pad, w_ukv_ext, gains, tm):
    s_len = x.shape[0]
    nt = s_len // tm

    def body(x_ref, ca_ref, sa_ref, cb_ref, sb_ref, win_ref, wuq_ref, wukv_ref,
             gin_ref, gaq_ref, gak_ref, gcq_ref, gckv_ref, gbq_ref, gbk_ref,
             xn_ref, gates_ref, pre_ref, qbpre_ref, kbpre_ref, cq_ref, ckv_ref,
             qa_ref, ka_ref, va_ref, qb_ref, kb_ref, vb_ref, proj):
        xn = _rms_fwd(x_ref[...], gin_ref[...], D_MODEL)
        xn_ref[...] = jnp.transpose(xn).astype(BF16)
        proj[...] = _nn(xn.astype(BF16), win_ref[...])
        gates_ref[...] = proj[:, GA0:GA0 + N_GATE]
        pre_ref[:, 0:VA0] = proj[:, 0:VA0]
        pre_ref[:, VA0:N_PRE] = proj[:, CQ0:KR0]
        ca, sa, cb, sb = ca_ref[...], sa_ref[...], cb_ref[...], sb_ref[...]
        lane = lax.broadcasted_iota(jnp.int32, (tm, LANE), 1)
        for h in range(A_HEADS):
            yq = _rms_fwd(proj[:, QA0 + LANE * h:QA0 + LANE * (h + 1)], gaq_ref[...], A_DIM)
            qa_ref[h] = (_rope_fwd(yq, ca, sa, A_DIM // 4) * (SCALE_A * LOG2E)).astype(BF16)
        for h in range(A_KV):
            yk = _rms_fwd(proj[:, KA0 + LANE * h:KA0 + LANE * (h + 1)], gak_ref[...], A_DIM)
            ka_ref[h] = _rope_fwd(yk, ca, sa, A_DIM // 4).astype(BF16)
            va_ref[h] = jnp.where(lane == A_DIM, 1.0, proj[:, VA0 + LANE * h:VA0 + LANE * (h + 1)]).astype(BF16)
        cq = _rms_fwd(proj[:, CQ0:CQ0 + B_Q_RANK], gcq_ref[...], B_Q_RANK)
        cq_ref[...] = jnp.transpose(cq).astype(BF16)
        qbpre_ref[...] = _nn(cq.astype(BF16), wuq_ref[...])
        ckv = _rms_fwd(proj[:, CKV0:CKV0 + B_KV_RANK], gckv_ref[...], B_KV_RANK)
        ckv_ref[...] = jnp.transpose(ckv).astype(BF16)
        kvb = _nn(ckv.astype(BF16), wukv_ref[...])
        kr = proj[:, KR0:KR0 + LANE]
        for h in range(B_HEADS):
            yq = _rms_fwd(qbpre_ref[:, LANE * h:LANE * (h + 1)], gbq_ref[...], B_QK)
            qb_ref[h] = (_rope_fwd(yq, cb, sb, B_ROPE // 4) * (SCALE_B * LOG2E)).astype(BF16)
            kp = kvb[:, LANE * h:LANE * (h + 1)] + kr
            kbpre_ref[:, LANE * h:LANE * (h + 1)] = kp
            kb_ref[h] = _rope_fwd(_rms_fwd(kp, gbk_ref[...], B_QK), cb, sb, B_ROPE // 4).astype(BF16)
            vb_ref[h, :, 0:LANE] = kvb[:, B_HEADS * LANE + LANE * h:B_HEADS * LANE + LANE * (h + 1)].astype(BF16)
            vb_ref[h, :, LANE:2 * LANE] = jnp.where(lane == 0, 1.0, 0.0).astype(BF16)

    row = lambda w: pl.BlockSpec((tm, w), lambda i: (i, 0))
    col = lambda w: pl.BlockSpec((w, tm), lambda i: (0, i))
    heads = lambda n: pl.BlockSpec((n, tm, LANE), lambda i: (0, i, 0))
    hs = lambda n: jax.ShapeDtypeStruct((n, s_len, LANE), BF16)
    return _pallas(
        body, name="pre", grid=(nt,),
        in_specs=[row(D_MODEL), row(LANE), row(LANE), row(LANE), row(LANE),
                  _full(w_in_ext.shape), _full(w_uq_pad.shape), _full(w_ukv_ext.shape)]
                 + [_full(g.shape) for g in gains],
        out_specs=[col(D_MODEL), row(N_GATE), row(N_PRE), row(B_HEADS * LANE), row(B_HEADS * LANE),
                   col(B_Q_RANK), col(B_KV_RANK),
                   heads(A_HEADS), heads(A_KV), heads(A_KV), heads(B_HEADS), heads(B_HEADS),
                   pl.BlockSpec((B_HEADS, tm, 2 * LANE), lambda i: (0, i, 0))],
        out_shape=[jax.ShapeDtypeStruct((D_MODEL, s_len), BF16), jax.ShapeDtypeStruct((s_len, N_GATE), F32),
                   jax.ShapeDtypeStruct((s_len, N_PRE), F32), jax.ShapeDtypeStruct((s_len, B_HEADS * LANE), F32),
                   jax.ShapeDtypeStruct((s_len, B_HEADS * LANE), F32),
                   jax.ShapeDtypeStruct((B_Q_RANK, s_len), BF16), jax.ShapeDtypeStruct((B_KV_RANK, s_len), BF16),
                   hs(A_HEADS), hs(A_KV), hs(A_KV), hs(B_HEADS), hs(B_HEADS),
                   jax.ShapeDtypeStruct((B_HEADS, s_len, 2 * LANE), BF16)],
        scratch_shapes=[pltpu.VMEM((tm, N_EXT), F32)],
        compiler_params=_params(("parallel",), VMEM_LIMIT),
    )(x, *tabs, w_in_ext, w_uq_pad, w_ukv_ext, *gains)


def _attn_fwd(q, k, v, group, l_col, tq, tk, tiles, name):
    n_heads, s_len, _ = q.shape
    v_w = v.shape[2]
    nk = s_len // tk

    def body(q_ref, k_ref, v_ref, o_ref, lse_ref, s_buf, p_buf, a_buf, m_ref, acc_ref):
        def scores(g, slot):
            s_buf[slot] = _nt(q_ref[_block_rows(g // nk, tq), :], k_ref[_block_rows(g % nk, tk), :])

        def softmax(g, slot):
            t = g // nk
            s = s_buf[slot]
            m_old = m_ref[t]
            m_new = jnp.maximum(m_old, jnp.max(s, axis=-1, keepdims=True))
            m_ref[t] = m_new
            a_buf[slot] = jnp.exp2(m_old - m_new)
            p_buf[slot] = jnp.exp2(s - jnp.tile(m_new, (1, tk // LANE))).astype(BF16)

        def values(g, slot):
            t = g // nk
            pv = _nn(p_buf[slot], v_ref[_block_rows(g % nk, tk), :])
            for c in range(0, v_w, LANE):
                acc_ref[t, :, c:c + LANE] = a_buf[slot] * acc_ref[t, :, c:c + LANE] + pv[:, c:c + LANE]

        m_ref[...] = jnp.full(m_ref.shape, -1e30, F32)
        acc_ref[...] = jnp.zeros(acc_ref.shape, F32)
        _three_stage(tiles * nk, scores, softmax, values)
        for t in range(tiles):
            l = acc_ref[t, :, l_col:l_col + 1]
            o = acc_ref[t, :, 0:LANE] * (1.0 / l)
            if l_col < LANE:
                lane = lax.broadcasted_iota(jnp.int32, o.shape, 1)
                o = jnp.where(lane == l_col, 0.0, o)
            o_ref[t * tq:(t + 1) * tq, :] = o
            lse_ref[t] = jnp.transpose(m_ref[t] + jnp.log2(jnp.broadcast_to(l, (tq, LANE))))[0:1, :]

    return _pallas(
        body, name=name, grid=(n_heads, s_len // (tiles * tq)),
        in_specs=[pl.BlockSpec((None, tiles * tq, LANE), lambda h, i: (h, i, 0)),
                  pl.BlockSpec((None, s_len, LANE), lambda h, i: (h // group, 0, 0)),
                  pl.BlockSpec((None, s_len, v_w), lambda h, i: (h // group, 0, 0))],
        out_specs=[pl.BlockSpec((None, tiles * tq, LANE), lambda h, i: (h, i, 0)),
                   pl.BlockSpec((None, tiles, 1, tq), lambda h, i: (h, i, 0, 0))],
        out_shape=[jax.ShapeDtypeStruct((n_heads, s_len, LANE), F32),
                   jax.ShapeDtypeStruct((n_heads, s_len // tq, 1, tq), F32)],
        scratch_shapes=[pltpu.VMEM((2, tq, tk), F32), pltpu.VMEM((2, tq, tk), BF16), pltpu.VMEM((2, tq, LANE), F32),
                        pltpu.VMEM((tiles, tq, LANE), F32), pltpu.VMEM((tiles, tq, v_w), F32)],
        compiler_params=_params(("parallel", "parallel"), 48 * 1024 * 1024),
    )(q, k, v)


def _mid(x, target, o_a, o_b, gates, w_out_ext, tm):
    s_len = x.shape[0]
    nt = s_len // tm
    n_heads = A_HEADS + B_HEADS

    def body(x_ref, t_ref, oa_ref, ob_ref, g_ref, w_ref,
             yt_ref, dh_ref, dgate_ref, doa_ref, dob_ref, delta_ref, loss_ref, silu_scr, dsilu_scr, y_ref):
        @pl.when(pl.program_id(0) == 0)
        def _():
            loss_ref[...] = jnp.zeros_like(loss_ref)

        def o_of(h):
            return oa_ref[h] if h < A_HEADS else ob_ref[h - A_HEADS]

        for h in range(n_heads):
            cols = slice(LANE * h, LANE * (h + 1))
            g = g_ref[:, cols]
            sig = 1.0 / (1.0 + jnp.exp(-g))
            silu = g * sig
            silu_scr[:, cols] = silu
            dsilu_scr[:, cols] = sig * (1.0 + g * (1.0 - sig))
            y = o_of(h) * silu
            y_ref[:, cols] = y.astype(BF16)
            yt_ref[cols, :] = jnp.transpose(y).astype(BF16)
        err =x_ref[...] + _nn(y_ref[...], w_ref[...]) - t_ref[...]
        sq = jnp.sum(jnp.sum(err * err, axis=-1, keepdims=True), axis=0, keepdims=True)
        loss_ref[...] += jnp.broadcast_to(sq * (0.5 / D_MODEL), loss_ref.shape)
        dh = err * (1.0 / D_MODEL)
        dh_ref[...] = dh
        dy = _nt(dh.astype(BF16), w_ref[...])
        lane = lax.broadcasted_iota(jnp.int32, (tm, LANE), 1)
        delta = jnp.zeros((tm, LANE), F32)
        for h in range(n_heads):
            cols = slice(LANE * h, LANE * (h + 1))
            dyh = dy[:, cols]
            oh = o_of(h)
            do = dyh * silu_scr[:, cols]
            dgate_ref[:, cols] = (dyh * oh * dsilu_scr[:, cols]).astype(BF16)
            delta = jnp.where(lane == h, jnp.sum(do * oh, axis=-1, keepdims=True), delta)
            if h < A_HEADS:
                doa_ref[h] = do.astype(BF16)
            else:
                dob_ref[h - A_HEADS] = do.astype(BF16)
        delta_ref[...] = jnp.transpose(delta)[0:DELTA_ROWS, :]

    row = lambda w: pl.BlockSpec((tm, w), lambda i: (i, 0))
    heads = lambda n, w=LANE: pl.BlockSpec((n, tm, w), lambda i: (0, i, 0))
    return _pallas(
        body, name="mid", grid=(nt,),
        in_specs=[row(D_MODEL), row(D_MODEL), heads(A_HEADS), heads(B_HEADS), row(N_GATE), _full(w_out_ext.shape)],
        out_specs=[pl.BlockSpec((N_GATE, tm), lambda i: (0, i)), row(D_MODEL), row(N_GATE), heads(A_HEADS), heads(B_HEADS),
                   pl.BlockSpec((DELTA_ROWS, tm), lambda i: (0, i)),
                   _full((8, LANE))],
        out_shape=[jax.ShapeDtypeStruct((N_GATE, s_len), BF16), jax.ShapeDtypeStruct((s_len, D_MODEL), F32),
                   jax.ShapeDtypeStruct((s_len, N_GATE), BF16),
                   jax.ShapeDtypeStruct((A_HEADS, s_len, LANE), BF16), jax.ShapeDtypeStruct((B_HEADS, s_len, LANE), BF16),
                   jax.ShapeDtypeStruct((DELTA_ROWS, s_len), F32), jax.ShapeDtypeStruct((8, LANE), F32)],
        scratch_shapes=[pltpu.VMEM((tm, N_GATE), F32), pltpu.VMEM((tm, N_GATE), F32), pltpu.VMEM((tm, N_GATE), BF16)],
        compiler_params=_params(("arbitrary",), VMEM_LIMIT),
    )(x, target, o_a, o_b, gates, w_out_ext)


def _attn_bwd(q, k, v, do, lse, delta, group, tq, tk, tiles, name):
    n_heads, s_len, _ = q.shape
    nq = s_len // tq

    def body(q_ref, do_ref, lse_ref, delta_ref, k_ref, v_ref, dq_ref, dk_ref, dv_ref, s_buf, dp_buf, p_buf, ds_buf):
        @pl.when(pl.program_id(1) == 0)
        def _():
            dq_ref[...] = jnp.zeros_like(dq_ref)

        dk_ref[...] = jnp.zeros_like(dk_ref)
        dv_ref[...] = jnp.zeros_like(dv_ref)

        def keys(g):
            return _block_rows(g // nq, tk)

        def queries(g):
            return _block_rows(g % nq, tq)

        def scores(g, slot):
            s_buf[slot] = _nt(k_ref[keys(g), :], q_ref[queries(g), :])
            dp_buf[slot] = _nt(v_ref[keys(g), :], do_ref[queries(g), :])

        def elementwise(g, slot):
            p = jnp.exp2(s_buf[slot] - lse_ref[g % nq])
            p_buf[slot] = p.astype(BF16)
            ds_buf[slot] = (p * (dp_buf[slot] - delta_ref[g % nq])).astype(BF16)

        def grads(g, slot):
            dv_ref[keys(g), :] += _nn(p_buf[slot], do_ref[queries(g), :])
            dk_ref[keys(g), :] += _nn(ds_buf[slot], q_ref[queries(g), :])
            dq_ref[queries(g), :] += _tn(ds_buf[slot], k_ref[keys(g), :])

        _three_stage(tiles * nq, scores, elementwise, grads)

    whole = lambda: pl.BlockSpec((None, s_len, LANE), lambda h, j: (h, 0, 0))
    stat = lambda: pl.BlockSpec((None, nq, 1, tq), lambda h, j: (h, 0, 0, 0))
    kvb = lambda: pl.BlockSpec((None, tiles * tk, LANE), lambda h, j: (h // group, j, 0))
    outb = lambda: pl.BlockSpec((None, tiles * tk, LANE), lambda h, j: (h, j, 0))
    shape = jax.ShapeDtypeStruct((n_heads, s_len, LANE), F32)
    return _pallas(
        body, name=name, grid=(n_heads, s_len // (tiles * tk)),
        in_specs=[whole(), whole(), stat(), stat(), kvb(), kvb()],
        out_specs=[whole(), outb(), outb()],
        out_shape=[shape, shape, shape],
        scratch_shapes=[pltpu.VMEM((2, tk, tq), F32), pltpu.VMEM((2, tk, tq), F32),
                        pltpu.VMEM((2, tk, tq), BF16), pltpu.VMEM((2, tk, tq), BF16)],
        compiler_params=_params(("parallel", "arbitrary"), 48 * 1024 * 1024),
    )(q, do, lse, delta, k, v)


def _post(x, dh, pre, qbpre, kbpre, dgate, dqa, dka, dva, dqb, dkb, dvb, loss_part, tabs,
          w_in_ext, w_uq_pad, w_ukv_ext, gains, tm):
    s_len = x.shape[0]
    nt = s_len // tm

    def body(x_ref, dh_ref, pre_ref, qbpre_ref, kbpre_ref, dgate_ref,
             dqa_ref, dka_ref, dva_ref, dqb_ref, dkb_ref, dvb_ref, loss_ref,
             ca_ref, sa_ref, cb_ref, sb_ref, win_ref, wuq_ref, wukv_ref,
             gin_ref, gaq_ref, gak_ref, gcq_ref, gckv_ref, gbq_ref, gbk_ref,
             gx_ref, dproj_ref, dqbpre_ref, dkvb_ref, dsm_ref):
        @pl.when(pl.program_id(0) == 0)
        def _():
            dsm_ref[...] = jnp.zeros_like(dsm_ref)
            dsm_ref[SM_LOSS:SM_LOSS + 1, 0:LANE] = loss_ref[0:1, :]

        def add_small(r, dg):
            dsm_ref[r:r + 1, 0:dg.shape[1]] += dg

        def tok_sum(a):
            return jnp.sum(a, axis=0, keepdims=True)

        ca, sa, cb, sb = ca_ref[...], sa_ref[...], cb_ref[...], sb_ref[...]
        lane = lax.broadcasted_iota(jnp.int32, (tm, LANE), 1)

        def back(c0, c1):
            return _nt(dproj_ref[:, c0:c1], win_ref[:, c0:c1])

        dproj_ref[:, GA0:GA0 + N_GATE] = dgate_ref[...]
        dxn = back(GA0, GA0 + N_GATE)
        dg = jnp.zeros((1, LANE), F32)
        for h in range(A_HEADS):
            dn = _rope_bwd(dqa_ref[h] * SCALE_A, ca, sa, A_DIM // 4)
            dx, dgr = _rms_bwd(dn, pre_ref[:, QA0 + LANE * h:QA0 + LANE * (h + 1)], gaq_ref[...], A_DIM)
            dproj_ref[:, QA0 + LANE * h:QA0 + LANE * (h + 1)] = dx.astype(BF16)
            dg = dg + tok_sum(dgr)
        add_small(SM_AQ, dg)
        dxn = dxn + back(QA0, KA0)
        dg = jnp.zeros((1, LANE), F32)
        for h in range(A_KV):
            dk = dka_ref[A_GROUP * h]
            dv = dva_ref[A_GROUP * h]
            for g in range(1, A_GROUP):
                dk = dk + dka_ref[A_GROUP * h + g]
                dv = dv + dva_ref[A_GROUP * h + g]
            dn = _rope_bwd(dk * LN2, ca, sa, A_DIM // 4)
            dx, dgr = _rms_bwd(dn, pre_ref[:, KA0 + LANE * h:KA0 + LANE * (h + 1)], gak_ref[...], A_DIM)
            dproj_ref[:, KA0 + LANE * h:KA0 + LANE * (h + 1)] = dx.astype(BF16)
            dproj_ref[:, VA0 + LANE * h:VA0 + LANE * (h + 1)] = dv.astype(BF16)
            dg = dg + tok_sum(dgr)
        add_small(SM_AK, dg)
        dxn = dxn + back(KA0, GA0)
        dg = jnp.zeros((1, LANE), F32)
        for h in range(B_HEADS):
            cols = slice(LANE * h, LANE * (h + 1))
            dn = _rope_bwd(dqb_ref[h] * SCALE_B, cb, sb, B_ROPE // 4)
            dx, dgr = _rms_bwd(dn, qbpre_ref[:, cols], gbq_ref[...], B_QK)
            dqbpre_ref[:, cols] = dx.astype(BF16)
            dg = dg + tok_sum(dgr)
        add_small(SM_BQ, dg)
        dcq = _nt(dqbpre_ref[...], wuq_ref[...])
        dx, dgr = _rms_bwd(dcq, pre_ref[:, VA0:VA0 + B_Q_RANK], gcq_ref[...], B_Q_RANK)
        dproj_ref[:, CQ0:CQ0 + B_Q_RANK] = dx.astype(BF16)
        add_small(SM_CQ, tok_sum(dgr))
        dxn = dxn + back(CQ0, CKV0)
        dg = jnp.zeros((1, LANE), F32)
        dkr = jnp.zeros((tm, LANE), F32)
        for h in range(B_HEADS):
            cols = slice(LANE * h, LANE * (h + 1))
            dn = _rope_bwd(dkb_ref[h] * LN2, cb, sb, B_ROPE // 4)
            dx, dgr = _rms_bwd(dn, kbpre_ref[:, cols], gbk_ref[...], B_QK)
            dkvb_ref[:, cols] = jnp.where(lane < B_NOPE, dx, 0.0).astype(BF16)
            dkvb_ref[:, B_HEADS * LANE + LANE * h:B_HEADS * LANE + LANE * (h + 1)] = dvb_ref[h].astype(BF16)
            dkr = dkr + dx
            dg = dg + tok_sum(dgr)
        add_small(SM_BK, dg)
        dproj_ref[:, KR0:KR0 + LANE] = jnp.where((lane >= B_NOPE) & (lane < B_QK), dkr, 0.0).astype(BF16)
        dckv = _nt(dkvb_ref[...], wukv_ref[...])
        dx, dgr = _rms_bwd(dckv, pre_ref[:, VA0 + B_Q_RANK:N_PRE], gckv_ref[...], B_KV_RANK)
        dproj_ref[:, CKV0:CKV0 + B_KV_RANK] = dx.astype(BF16)
        add_small(SM_CKV, tok_sum(dgr))
        dxn = dxn + back(CKV0, N_EXT)
        dx, dgr = _rms_bwd(dxn, x_ref[...], gin_ref[...], D_MODEL)
        gx_ref[...] = dh_ref[...] + dx
        add_small(SM_IN, tok_sum(dgr))

    row = lambda w: pl.BlockSpec((tm, w), lambda i: (i, 0))
    heads = lambda n: pl.BlockSpec((n, tm, LANE), lambda i: (0, i, 0))
    return _pallas(
        body, name="post", grid=(nt,),
        in_specs=[row(D_MODEL), row(D_MODEL), row(N_PRE), row(B_HEADS * LANE), row(B_HEADS * LANE), row(N_GATE),
                  heads(A_HEADS), heads(A_HEADS), heads(A_HEADS), heads(B_HEADS), heads(B_HEADS), heads(B_HEADS),
                  _full(loss_part.shape), row(LANE), row(LANE), row(LANE), row(LANE),
                  _full(w_in_ext.shape), _full(w_uq_pad.shape), _full(w_ukv_ext.shape)]
                 + [_full(g.shape) for g in gains],
        out_specs=[row(D_MODEL), row(N_EXT), row(B_HEADS * LANE), row(2 * B_HEADS * LANE), _full((SM_ROWS, SM_W))],
        out_shape=[jax.ShapeDtypeStruct((s_len, D_MODEL), F32), jax.ShapeDtypeStruct((s_len, N_EXT), BF16),
                   jax.ShapeDtypeStruct((s_len, B_HEADS * LANE), BF16),
                   jax.ShapeDtypeStruct((s_len, 2 * B_HEADS * LANE), BF16),
                   jax.ShapeDtypeStruct((SM_ROWS, SM_W), F32)],
        compiler_params=_params(("arbitrary",), VMEM_LIMIT),
    )(x, dh, pre, qbpre, kbpre, dgate, dqa, dka, dva, dqb, dkb, dvb, loss_part, *tabs,
      w_in_ext, w_uq_pad, w_ukv_ext, *gains)


def _grad_w(a_t, b, tn, ts, name):
    m, s_len = a_t.shape
    n = b.shape[1]

    def body(a_ref, b_ref, o_ref):
        @pl.when(pl.program_id(1) == 0)
        def _():
            o_ref[...] = jnp.zeros_like(o_ref)

        o_ref[...] += _nn(a_ref[...], b_ref[...].astype(BF16))

    return _pallas(
        body, name=name, grid=(n // tn, s_len // ts),
        in_specs=[pl.BlockSpec((m, ts), lambda j, t: (0, t)), pl.BlockSpec((ts, tn), lambda j, t: (t, j))],
        out_specs=pl.BlockSpec((m, tn), lambda j, t: (0, j)),
        out_shape=jax.ShapeDtypeStruct((m, n), F32),
        compiler_params=_params(("parallel", "arbitrary"), 48 * 1024 * 1024),
    )(a_t, b)


def _adam_math(w, g, m, v):
    nm = ADAM_B1 * m + (1.0 - ADAM_B1) * g
    nv = ADAM_B2 * v + (1.0 - ADAM_B2) * (g * g)
    m_hat = nm / (1.0 - ADAM_B1 ** ADAM_STEP)
    v_hat = nv / (1.0 - ADAM_B2 ** ADAM_STEP)
    return -ADAM_LR * (m_hat / (jnp.sqrt(v_hat) + ADAM_EPS) + ADAM_WD * w), nm, nv


def _adamw_rows(w, g, m, v, tr):
    rows, cols = w.shape

    def body(w_ref, g_ref, m_ref, v_ref, d_ref, nm_ref, nv_ref):
        d_ref[...], nm_ref[...], nv_ref[...] = _adam_math(w_ref[...], g_ref[...], m_ref[...], v_ref[...])

    blk = pl.BlockSpec((tr, cols), lambda i: (i, 0))
    shape = jax.ShapeDtypeStruct((rows, cols), F32)
    return _pallas(
        body, name="adamw_w_in", grid=(rows // tr,),
        in_specs=[blk] * 4, out_specs=[blk] * 3, out_shape=[shape] * 3,
        compiler_params=_params(("parallel",), 32 * 1024 * 1024),
    )(w, g, m, v)


def _adamw_rest(bigs, smalls, g_small):
    nb, ns = len(bigs), len(smalls)

    def body(*refs):
        ins, outs = refs[:4 * nb + 3 * ns + 1], refs[4 * nb + 3 * ns + 1:]
        for i in range(nb):
            w_ref, g_ref, m_ref, v_ref = ins[4 * i:4 * i + 4]
            d_ref, nm_ref, nv_ref = outs[3 * i:3 * i + 3]
            d_ref[...], nm_ref[...], nv_ref[...] = _adam_math(w_ref[...], g_ref[...], m_ref[...], v_ref[...])
        gs_ref = ins[-1]
        for i in range(ns):
            w_ref, m_ref, v_ref = ins[4 * nb + 3 * i:4 * nb + 3 * i + 3]
            g_ref, d_ref, nm_ref, nv_ref = outs[3 * nb + 4 * i:3 * nb + 4 * i + 4]
            g = gs_ref[i:i + 1, 0:w_ref.shape[1]]
            g_ref[...] = g
            d_ref[...], nm_ref[...], nv_ref[...] = _adam_math(w_ref[...], g, m_ref[...], v_ref[...])

    flat_in = [a for quad in bigs for a in quad] + [a for tri in smalls for a in tri] + [g_small]
    out_shape = ([jax.ShapeDtypeStruct(q[0].shape, F32) for q in bigs for _ in range(3)]
                 + [jax.ShapeDtypeStruct(t[0].shape, F32) for t in smalls for _ in range(4)])
    return _pallas(
        body, name="adamw_rest",
        in_specs=[pl.BlockSpec(memory_space=pltpu.VMEM)] * len(flat_in),
        out_specs=[pl.BlockSpec(memory_space=pltpu.VMEM)] * len(out_shape),
        out_shape=out_shape,
        compiler_params=_params(vmem=32 * 1024 * 1024),
    )(*flat_in)


def _rope_tables(s_len):
    rows = s_len // GRID_W
    row = jnp.arange(rows, dtype=F32)
    col = jnp.arange(GRID_W, dtype=F32)

    def lay(dim, lead):
        half = dim // 2
        inv = 1.0 / (ROPE_THETA ** (jnp.arange(0, half, 2, dtype=F32) / half))
        ang_r, ang_c = row[:, None] * inv[None, :], col[:, None] * inv[None, :]
        tail = LANE - lead - dim

        def place(a, b, at, n, fill=0.0):
            return jnp.concatenate([jnp.full((n, lead), fill if at else 0.0, F32) if lead else jnp.zeros((n, 0), F32),
                                    jnp.zeros((n, at * half), F32), a, b,
                                    jnp.zeros((n, (1 - at) * half + tail), F32)], axis=1)

        cos_r = place(jnp.cos(ang_r), jnp.cos(ang_r), 0, rows)
        cos_c = place(jnp.cos(ang_c), jnp.cos(ang_c), 1, GRID_W, 1.0)
        sin_r = place(-jnp.sin(ang_r), jnp.sin(ang_r), 0, rows)
        sin_c = place(-jnp.sin(ang_c), jnp.sin(ang_c), 1, GRID_W)
        assert cos_r.shape[1] == LANE
        cos = (cos_r[:, None, :] + cos_c[None, :, :]).reshape(s_len, LANE)
        sin = (sin_r[:, None, :] + sin_c[None, :, :]).reshape(s_len, LANE)
        return cos, sin

    cos_a, sin_a = lay(A_DIM, 0)
    cos_b, sin_b = lay(B_ROPE, B_NOPE)
    return cos_a, sin_a, cos_b, sin_b


def _pad_heads(w, n_heads, dim, axis):
    shape = w.shape[:axis] + (n_heads, dim) + w.shape[axis + 1:]
    pad = [(0, 0)] * len(shape)
    pad[axis + 1] = (0, LANE - dim)
    out = jnp.pad(w.reshape(shape), pad)
    return out.reshape(w.shape[:axis] + (n_heads * LANE,) + w.shape[axis + 1:])


def _unpad_heads(w, n_heads, dim, axis):
    shape = w.shape[:axis] + (n_heads, LANE) + w.shape[axis + 1:]
    out = lax.slice_in_dim(w.reshape(shape), 0, dim, axis=axis + 1)
    return out.reshape(w.shape[:axis] + (n_heads * dim,) + w.shape[axis + 1:])


def _pad_vec(g):
    return jnp.pad(g, ((0, 0), (0, LANE - g.shape[1])))


def _ext_weights(g_in, g_uq, g_ukv, g_out):
    w_in = g_in.transpose(1, 0, 2).reshape(D_MODEL, N_IN)
    w_uq = g_uq.reshape(B_Q_RANK, B_HEADS * B_QK)
    w_ukv = g_ukv.transpose(1, 0, 2).reshape(B_KV_RANK, B_HEADS * (B_NOPE + B_V))
    w_out = g_out.reshape(D_MODEL, D_MODEL)
    a_w = A_HEADS * A_DIM
    kv_w = A_KV * A_DIM
    o = 0
    secs = []
    for width, heads in ((a_w, A_HEADS), (kv_w, A_KV), (kv_w, A_KV), (a_w, A_HEADS)):
        secs.append(_pad_heads(w_in[:, o:o + width], heads, A_DIM, 1))
        o += width
    a_q, a_k, a_v, a_g = secs
    b_cq = w_in[:, o:o + B_Q_RANK]
    o += B_Q_RANK
    b_ckv = w_in[:, o:o + B_KV_RANK]
    o += B_KV_RANK
    b_kr = jnp.pad(w_in[:, o:o + B_ROPE], ((0, 0), (B_NOPE, LANE - B_QK)))
    o += B_ROPE
    b_g = w_in[:, o:]
    w_in_ext = jnp.concatenate([a_q, a_k, a_v, a_g, b_g, b_cq, b_ckv, b_kr], axis=1)
    w_uq_pad = _pad_heads(w_uq, B_HEADS, B_QK, 1)
    kv3 = w_ukv.reshape(B_KV_RANK, B_HEADS, B_NOPE + B_V)
    w_ukv_ext = jnp.concatenate([_pad_heads(kv3[:, :, :B_NOPE].reshape(B_KV_RANK, B_HEADS * B_NOPE), B_HEADS, B_NOPE, 1),
                                 kv3[:, :, B_NOPE:].reshape(B_KV_RANK, B_HEADS * B_V)], axis=1)
    w_out_ext = jnp.concatenate([_pad_heads(w_out[:a_w], A_HEADS, A_DIM, 0), w_out[a_w:]], axis=0)
    return w_in_ext, w_uq_pad, w_ukv_ext, w_out_ext


def _fold_grads(d_in_ext, d_uq_pad, d_ukv_ext, d_out_ext):
    d_in = jnp.concatenate([
        _unpad_heads(d_in_ext[:, QA0:KA0], A_HEADS, A_DIM, 1),
        _unpad_heads(d_in_ext[:, KA0:VA0], A_KV, A_DIM, 1),
        _unpad_heads(d_in_ext[:, VA0:GA0], A_KV, A_DIM, 1),
        _unpad_heads(d_in_ext[:, GA0:GB0], A_HEADS, A_DIM, 1),
        d_in_ext[:, CQ0:KR0],
        d_in_ext[:, KR0 + B_NOPE:KR0 + B_QK],
        d_in_ext[:, GB0:CQ0]], axis=1)
    d_uq = _unpad_heads(d_uq_pad, B_HEADS, B_QK, 1)
    k3 = _unpad_heads(d_ukv_ext[:, :B_HEADS * LANE], B_HEADS, B_NOPE, 1).reshape(B_KV_RANK, B_HEADS, B_NOPE)
    v3 = d_ukv_ext[:, B_HEADS * LANE:].reshape(B_KV_RANK, B_HEADS, B_V)
    d_ukv = jnp.concatenate([k3, v3], axis=2).reshape(B_KV_RANK, B_HEADS * (B_NOPE + B_V))
    d_out = jnp.concatenate([_unpad_heads(d_out_ext[:A_HEADS * LANE], A_HEADS, A_DIM, 0), d_out_ext[A_HEADS * LANE:]], axis=0)
    return (d_in.reshape(D_MODEL, N_CHIPS, SH_IN[1]).transpose(1, 0, 2),
            d_uq.reshape((N_CHIPS,) + SH_UQ),
            d_ukv.reshape(B_KV_RANK, N_CHIPS, SH_UKV[1]).transpose(1, 0, 2),
            d_out.reshape((N_CHIPS,) + SH_OUT))


def kernel(x, norm_in, w_in, a_q_norm, a_k_norm, b_cq_norm, b_ckv_norm, w_uq, w_ukv, b_q_norm, b_k_norm, w_out, loss_target, m_norm_in, m_w_in, m_a_q_norm, m_a_k_norm, m_b_cq_norm, m_b_ckv_norm, m_w_uq, m_w_ukv, m_b_q_norm, m_b_k_norm, m_w_out, v_norm_in, v_w_in, v_a_q_norm, v_a_k_norm, v_b_cq_norm, v_b_ckv_norm, v_w_uq, v_w_ukv, v_b_q_norm, v_b_k_norm, v_w_out):
    s_len = x.shape[1]
    xs, ts = x[0], loss_target[0]
    tm = min(256, s_len)
    tq, tk_f = min(512, s_len // 2), min(1024, s_len // 2)
    tq_b, tk_b = min(1024, s_len // 2), min(512, s_len)
    tiles_f = min(2, s_len // tq)
    tiles_b = min(2, s_len // tk_b)

    w_in_ext, w_uq_pad, w_ukv_ext, w_out_ext = _ext_weights(*_gather_weights((w_in[0], w_uq[0], w_ukv[0], w_out[0])))
    gains = (norm_in, _pad_vec(a_q_norm), _pad_vec(a_k_norm), b_cq_norm, b_ckv_norm, _pad_vec(b_q_norm), _pad_vec(b_k_norm))
    tabs = _rope_tables(s_len)

    (xn_t, gates, pre, qbpre, kbpre, cq_t, ckv_t, qa, ka, va, qb, kb, vb) = _pre(
        xs, tabs, w_in_ext, w_uq_pad, w_ukv_ext, gains, tm)
    o_a, lse_a = _attn_fwd(qa, ka, va, A_GROUP, A_DIM, tq, tk_f, tiles_f, "attn_fwd_a")
    o_b, lse_b = _attn_fwd(qb, kb, vb, 1, B_V, tq, tk_f, tiles_f, "attn_fwd_b")
    y_t, dh, dgate, do_a, do_b, delta, loss_part = _mid(xs, ts, o_a, o_b, gates, w_out_ext, tm)

    def stat(a):
        return a.reshape(a.shape[0], s_len // tq_b, 1, tq_b)

    dqa, dka, dva = _attn_bwd(qa, ka, va, do_a, stat(lse_a), stat(delta[:A_HEADS]), A_GROUP, tq_b, tk_b, tiles_b, "attn_bwd_a")
    dqb, dkb, dvb = _attn_bwd(qb, kb, vb, do_b, stat(lse_b), stat(delta[A_HEADS:A_HEADS + B_HEADS]), 1, tq_b, tk_b,
                              tiles_b, "attn_bwd_b")
    grad_x, dproj, dqbpre, dkvb, d_small = _post(
        xs, dh, pre, qbpre, kbpre, dgate, dqa, dka, dva, dqb, dkb, dvb, loss_part, tabs,
        w_in_ext, w_uq_pad, w_ukv_ext, gains, tm)

    ts_w = min(2048, s_len)
    d_in_ext = _grad_w(xn_t, dproj, 768, ts_w, "grad_w_in")
    d_out_ext = _grad_w(y_t, dh, 512, ts_w, "grad_w_out")
    d_uq_pad = _grad_w(cq_t, dqbpre, 512, ts_w, "grad_w_uq")
    d_ukv_ext = _grad_w(ckv_t, dkvb, 1024, ts_w, "grad_w_ukv")

    g_in, g_uq, g_ukv, g_out, g_small = _reduce_grads(_fold_grads(d_in_ext, d_uq_pad, d_ukv_ext, d_out_ext), d_small)
    d_in, nm_in, nv_in = _adamw_rows(w_in[0], g_in, m_w_in[0], v_w_in[0], 256)
    rest = _adamw_rest(
        [(w_uq[0], g_uq, m_w_uq[0], v_w_uq[0]), (w_ukv[0], g_ukv, m_w_ukv[0], v_w_ukv[0]),
         (w_out[0], g_out, m_w_out[0], v_w_out[0])],
        [(norm_in, m_norm_in, v_norm_in), (a_q_norm, m_a_q_norm, v_a_q_norm), (a_k_norm, m_a_k_norm, v_a_k_norm),
         (b_cq_norm, m_b_cq_norm, v_b_cq_norm), (b_ckv_norm, m_b_ckv_norm, v_b_ckv_norm),
         (b_q_norm, m_b_q_norm, v_b_q_norm), (b_k_norm, m_b_k_norm, v_b_k_norm)], g_small)
    (d_uq, nm_uq, nv_uq), (d_ukv, nm_ukv, nv_ukv), (d_out, nm_out, nv_out) = (rest[3 * i:3 * i + 3] for i in range(3))
    sm = [rest[9 + 4 * i:9 + 4 * i + 4] for i in range(7)]

    def leaves(k, p_in, p_uq, p_ukv, p_out):
        return [sm[SM_IN][k], p_in[None], sm[SM_AQ][k], sm[SM_AK][k], sm[SM_CQ][k], sm[SM_CKV][k], p_uq[None], p_ukv[None],
                sm[SM_BQ][k], sm[SM_BK][k], p_out[None]]

    return (g_small[SM_LOSS, 0], grad_x[None], *leaves(0, g_in, g_uq, g_ukv, g_out), *leaves(1, d_in, d_uq, d_ukv, d_out),
            *leaves(2, nm_in, nm_uq, nm_ukv, nm_out), *leaves(3, nv_in, nv_uq, nv_ukv, nv_out))
```

```python
import jax
import jax.numpy as jnp
import numpy as np
from jax import lax
from jax.experimental import pallas as pl
from jax.experimental.pallas import tpu as pltpu

F32 = jnp.float32
BF16 = jnp.bfloat16
MESH = pl.DeviceIdType.MESH

D_MODEL = 1024
GRID_W = 64
ROPE_THETA = 10000.0
EPS = 1e-6
A_HEADS, A_KV, A_DIM = 8, 2, 64
A_GROUP = A_HEADS // A_KV
B_HEADS, B_NOPE, B_ROPE, B_V = 4, 64, 32, 128
B_QK = B_NOPE + B_ROPE
B_Q_RANK, B_KV_RANK = 384, 256
N_IN = 2464
SCALE_A = 1.0 / float(np.sqrt(A_DIM))
SCALE_B = 1.0 / float(np.sqrt(B_QK))
LOG2E = float(np.log2(np.e))
LN2 = float(np.log(2.0))
ADAM_LR, ADAM_B1, ADAM_B2, ADAM_EPS, ADAM_WD, ADAM_STEP = 0.001, 0.9, 0.999, 1e-08, 0.01, 10

LANE = 128
VMEM_BYTES = 64 * 1024 * 1024
VMEM_LIMIT = VMEM_BYTES - 8 * 1024 * 1024

QA0 = 0
KA0 = QA0 + A_HEADS * LANE
VA0 = KA0 + A_KV * LANE
GA0 = VA0 + A_KV * LANE
GB0 = GA0 + A_HEADS * LANE
CQ0 = GB0 + B_HEADS * LANE
CKV0 = CQ0 + B_Q_RANK
KR0 = CKV0 + B_KV_RANK
N_EXT = KR0 + LANE
N_GATE = (A_HEADS + B_HEADS) * LANE
DELTA_ROWS = 16
N_PRE = KA0 + A_KV * LANE + B_Q_RANK + B_KV_RANK

ROT = LANE // 2
_QA = A_DIM // 4
_QB = B_ROPE // 4
LAY_PLAIN_A = ((0, A_DIM, 0),)
LAY_ROPE_A = ((0, _QA, 0), (2 * _QA, _QA, _QA), (_QA, _QA, ROT), (3 * _QA, _QA, ROT + _QA))
LAY_KR = ((0, _QB, 0), (2 * _QB, _QB, _QB), (_QB, _QB, ROT), (3 * _QB, _QB, ROT + _QB))
LAY_NOPE = ((0, B_NOPE // 2, 2 * _QB), (B_NOPE // 2, B_NOPE // 2, ROT + 2 * _QB))
LAY_ROPE_B = LAY_NOPE + tuple((B_NOPE + a, n, at) for a, n, at in LAY_KR)

N_CHIPS = 4
SH_IN = (D_MODEL, N_IN // N_CHIPS)
SH_UQ = (B_Q_RANK // N_CHIPS, B_HEADS * B_QK)
SH_UKV = (B_KV_RANK, B_HEADS * (B_NOPE + B_V) // N_CHIPS)
SH_OUT = (D_MODEL // N_CHIPS, D_MODEL)
SM_ROWS, SM_W = 16, D_MODEL
SM_IN, SM_AQ, SM_AK, SM_CQ, SM_CKV, SM_BQ, SM_BK, SM_LOSS = range(8)
F32_ROWS, BF16_ROWS = 8, 16


def _pallas(body, **kw):
    return pl.pallas_call(body, **kw)


def _params(sem=None, vmem=None):
    return pltpu.CompilerParams(dimension_semantics=sem, vmem_limit_bytes=vmem)


def _rms_fwd(x, g, n):
    r = lax.rsqrt(jnp.sum(x * x, axis=-1, keepdims=True) * (1.0 / n) + EPS)
    return x * r * g


def _rms_bwd(dy, x, g, n):
    u = dy * g
    r = lax.rsqrt(jnp.sum(x * x, axis=-1, keepdims=True) * (1.0 / n) + EPS)
    ux = jnp.sum(u * x, axis=-1, keepdims=True)
    xhat = x * r
    dx = r * (u - xhat * (r * ux * (1.0 / n)))
    return dx, dy * xhat


def _rope_fwd(y, cos, sin):
    return y * cos + pltpu.roll(y, ROT, 1) * sin


def _rope_bwd(d, cos, sin):
    return d * cos - pltpu.roll(d, ROT, 1) * sin


def _lanes_of(lane, layout):
    m = None
    for _, n, at in layout:
        seg = (lane >= at) & (lane < at + n)
        m = seg if m is None else (m | seg)
    return m


def _unspread_row(v, layout):
    v8 = jnp.broadcast_to(v, (F32_ROWS, LANE))
    lane = lax.broadcasted_iota(jnp.int32, v8.shape, 1)
    out = jnp.zeros_like(v8)
    for a, n, at in layout:
        moved = v8 if a == at else pltpu.roll(v8, (a - at) % LANE, 1)
        out = jnp.where((lane >= a) & (lane < a + n), moved, out)
    return out[0:1, :]


def _nt(a, b):
    return lax.dot_general(a, b, (((1,), (1,)), ((), ())), preferred_element_type=F32)


def _tn(a, b):
    return lax.dot_general(a, b, (((0,), (0,)), ((), ())), preferred_element_type=F32)


def _nn(a, b):
    return jnp.dot(a, b, preferred_element_type=F32)


def _block_rows(i, size):
    if isinstance(i, int):
        return pl.ds(i * size, size)
    return pl.ds(pl.multiple_of(i * size, size), size)


MAX_STATIC_BLOCKS = 32


def _three_stage(n, first, second, third):
    assert n >= 2 and n % 2 == 0
    first(0, 0)
    first(1, 1)
    second(0, 0)
    if n <= MAX_STATIC_BLOCKS:
        for i in range(1, n - 1):
            first(i + 1, (i + 1) % 2)
            second(i, i % 2)
            third(i - 1, (i - 1) % 2)
    else:
        def pair(t, carry):
            i = 2 * t + 1
            first(i + 1, 0)
            second(i, 1)
            third(i - 1, 0)
            first(i + 2, 1)
            second(i + 1, 0)
            third(i, 1)
            return carry

        lax.fori_loop(0, (n - 2) // 2, pair, 0)
    second(n - 1, 1)
    third(n - 2, 0)
    third(n - 1, 1)


def _full(shape):
    return pl.BlockSpec(shape, lambda *_: (0,) * len(shape))


def _gather_weights(shards):
    n = len(shards)
    halves = [w.shape[0] // 2 for w in shards]

    def body(*refs):
        w_refs, out_refs, (send_sems, recv_sems) = refs[:n], refs[n:2 * n], refs[2 * n:]
        x, y, c = lax.axis_index("x"), lax.axis_index("y"), lax.axis_index("c")
        sibling = (x, y, 1 - c)
        chips = [(1 - x, y), (x, 1 - y), (1 - x, 1 - y)]
        me = 2 * x + y

        def copy(a, k, j, hc, to):
            part = out_refs[a].at[j, pl.ds(pl.multiple_of(hc * halves[a], BF16_ROWS), halves[a]), :]
            return pltpu.make_async_remote_copy(
                src_ref=part, dst_ref=part, send_sem=send_sems.at[6 * a + k], recv_sem=recv_sems.at[6 * a + k],
                device_id=to, device_id_type=MESH)

        started = []
        for a in range(n):
            out_refs[a][me] = w_refs[a][...].astype(BF16)
            for k, chip in enumerate(chips):
                started.append(copy(a, k, me, c, (*chip, c)))
                started[-1].start()
        for k, chip in enumerate(chips):
            for a in range(n):
                copy(a, k, 2 * chip[0] + chip[1], c, (*chip, c)).wait_recv()
                started.append(copy(a, 3 + k, 2 * chip[0] + chip[1], c, sibling))
                started[-1].start()
        for k, chip in enumerate(chips):
            for a in range(n):
                copy(a, 3 + k, 2 * chip[0] + chip[1], 1 - c, sibling).wait_recv()
        for cp in started:
            cp.wait_send()

    return _pallas(
        body, name="gather_weights",
        out_shape=[jax.ShapeDtypeStruct((N_CHIPS,) + w.shape, BF16) for w in shards],
        in_specs=[pl.BlockSpec(memory_space=pltpu.VMEM)] * n,
        out_specs=[pl.BlockSpec(memory_space=pltpu.VMEM)] * n,
        scratch_shapes=[pltpu.SemaphoreType.DMA((6 * n,)), pltpu.SemaphoreType.DMA((6 * n,))],
        compiler_params=_params(vmem=32 * 1024 * 1024),
    )(*shards)


def _reduce_grads(parts, small):
    n_big = len(parts)
    n = n_big + 1
    shapes = [p.shape[1:] for p in parts] + [small.shape]
    halves = [sh[0] // 2 for sh in shapes]

    def body(*refs):
        p_refs, out_refs, rec_a, rec_b = refs[:n], refs[n:2 * n], refs[2 * n:3 * n], refs[3 * n:4 * n]
        send_b = refs[4 * n:4 * n + n_big]
        sa_send, sa_recv, sb_send, sb_recv, sc_send, sc_recv = refs[4 * n + n_big:]
        x, y, c = lax.axis_index("x"), lax.axis_index("y"), lax.axis_index("c")
        sibling = (x, y, 1 - c)
        me = 2 * x + y

        def rows(a, hc):
            return pl.ds(pl.multiple_of(hc * halves[a], F32_ROWS), halves[a])

        def partial(a, j, hc):
            return p_refs[a].at[j, rows(a, hc), :] if a < n_big else p_refs[a].at[rows(a, hc), :]

        def copy_a(a, j):
            return pltpu.make_async_remote_copy(
                src_ref=partial(a, j, 1 - c), dst_ref=rec_a[a].at[j],
                send_sem=sa_send.at[N_CHIPS * a + j], recv_sem=sa_recv.at[N_CHIPS * a + j],
                device_id=sibling, device_id_type=MESH)

        def copy_b(a, r):
            j = me ^ r
            k = (N_CHIPS - 1) * a + r - 1
            return pltpu.make_async_remote_copy(
                src_ref=(send_b[a] if a < n_big else rec_a[a]).at[j], dst_ref=rec_b[a].at[r],
                send_sem=sb_send.at[k], recv_sem=sb_recv.at[k], device_id=(j // 2, j % 2, c), device_id_type=MESH)

        def copy_c(a):
            return pltpu.make_async_remote_copy(
                src_ref=out_refs[a].at[rows(a, c), :], dst_ref=out_refs[a].at[rows(a, c), :],
                send_sem=sc_send.at[a], recv_sem=sc_recv.at[a], device_id=sibling, device_id_type=MESH)

        for a in range(n):
            for j in range(N_CHIPS):
                copy_a(a, j).start()
        for r in range(1, N_CHIPS):
            j = me ^ r
            for a in range(n):
                copy_a(a, j).wait_recv()
                chip_part = rec_a[a][j] + partial(a, j, c)[...]
                if a < n_big:
                    send_b[a][j] = chip_part.astype(BF16)
                else:
                    rec_a[a][j] = chip_part
                copy_b(a, r).start()
        for a in range(n):
            copy_a(a, me).wait_recv()
            rec_b[a][0] = (rec_a[a][me] + partial(a, me, c)[...]).astype(rec_b[a].dtype)
        for a in range(n):
            for r in range(1, N_CHIPS):
                copy_b(a, r).wait_recv()
            total = rec_b[a][me].astype(F32)
            for j in range(1, N_CHIPS):
                total = total + rec_b[a][j ^ me].astype(F32)
            out_refs[a][rows(a, c), :] = total
            copy_c(a).start()
        for a in range(n):
            copy_c(a).wait_recv()
        for a in range(n):
            for j in range(N_CHIPS):
                copy_a(a, j).wait_send()
            for r in range(1, N_CHIPS):
                copy_b(a, r).wait_send()
            copy_c(a).wait_send()

    dma = pltpu.SemaphoreType.DMA
    return _pallas(
        body, name="reduce_grads",
        out_shape=[jax.ShapeDtypeStruct(sh, F32) for sh in shapes],
        in_specs=[pl.BlockSpec(memory_space=pltpu.VMEM)] * n,
        out_specs=[pl.BlockSpec(memory_space=pltpu.VMEM)] * n,
        scratch_shapes=[pltpu.VMEM((N_CHIPS, h) + sh[1:], F32) for h, sh in zip(halves, shapes)]
                       + [pltpu.VMEM((N_CHIPS, h) + sh[1:], BF16 if a < n_big else F32)
                          for a, (h, sh) in enumerate(zip(halves, shapes))]
                       + [pltpu.VMEM((N_CHIPS, h) + sh[1:], BF16) for h, sh in zip(halves[:n_big], shapes[:n_big])]
                       + [dma((N_CHIPS * n,)), dma((N_CHIPS * n,)), dma(((N_CHIPS - 1) * n,)), dma(((N_CHIPS - 1) * n,)),
                          dma((n,)), dma((n,))],
        compiler_params=_params(vmem=VMEM_LIMIT),
    )(*parts, small)


def _pre(x, tabs, w_in_ext, w_uq_pad, w_ukv_ext, gains, tm):
    s_len = x.shape[0]
    nt = s_len // tm

    def body(x_ref, ca_ref, sa_ref, cb_ref, sb_ref, win_ref, wuq_ref, wukv_ref,
             gin_ref, gaq_ref, gak_ref, gcq_ref, gckv_ref, gbq_ref, gbk_ref,
             xn_ref, gates_ref, pre_ref, qbpre_ref, kbpre_ref, cq_ref, ckv_ref,
             qa_ref, ka_ref, va_ref, qb_ref, kb_ref, vb_ref, proj):
        xn = _rms_fwd(x_ref[...], gin_ref[...], D_MODEL)
        xn_ref[...] = jnp.transpose(xn).astype(BF16)
        proj[...] = _nn(xn.astype(BF16), win_ref[...])
        gates_ref[...] = proj[:, GA0:GA0 + N_GATE]
        pre_ref[:, 0:VA0] = proj[:, 0:VA0]
        pre_ref[:, VA0:N_PRE] = proj[:, CQ0:KR0]
        ca, sa, cb, sb = ca_ref[...], sa_ref[...], cb_ref[...], sb_ref[...]
        lane = lax.broadcasted_iota(jnp.int32, (tm, LANE), 1)
        for h in range(A_HEADS):
            yq = _rms_fwd(proj[:, QA0 + LANE * h:QA0 + LANE * (h + 1)], gaq_ref[...], A_DIM)
            qa_ref[h] = (_rope_fwd(yq, ca, sa) * (SCALE_A * LOG2E)).astype(BF16)
        for h in range(A_KV):
            yk = _rms_fwd(proj[:, KA0 + LANE * h:KA0 + LANE * (h + 1)], gak_ref[...], A_DIM)
            ka_ref[h] = _rope_fwd(yk, ca, sa).astype(BF16)
            va_ref[h] = jnp.where(lane == A_DIM, 1.0, proj[:, VA0 + LANE * h:VA0 + LANE * (h + 1)]).astype(BF16)
        cq = _rms_fwd(proj[:, CQ0:CQ0 + B_Q_RANK], gcq_ref[...], B_Q_RANK)
        cq_ref[...] = jnp.transpose(cq).astype(BF16)
        qbpre_ref[...] = _nn(cq.astype(BF16), wuq_ref[...])
        ckv = _rms_fwd(proj[:, CKV0:CKV0 + B_KV_RANK], gckv_ref[...], B_KV_RANK)
        ckv_ref[...] = jnp.transpose(ckv).astype(BF16)
        kvb = _nn(ckv.astype(BF16), wukv_ref[...])
        kr = proj[:, KR0:KR0 + LANE]
        for h in range(B_HEADS):
            yq = _rms_fwd(qbpre_ref[:, LANE * h:LANE * (h + 1)], gbq_ref[...], B_QK)
            qb_ref[h] = (_rope_fwd(yq, cb, sb) * (SCALE_B * LOG2E)).astype(BF16)
            kp = kvb[:, LANE * h:LANE * (h + 1)] + kr
            kbpre_ref[:, LANE * h:LANE * (h + 1)] = kp
            kb_ref[h] = _rope_fwd(_rms_fwd(kp, gbk_ref[...], B_QK), cb, sb).astype(BF16)
            vb_ref[h, :, 0:LANE] = kvb[:, B_HEADS * LANE + LANE * h:B_HEADS * LANE + LANE * (h + 1)].astype(BF16)
            vb_ref[h, :, LANE:2 * LANE] = jnp.where(lane == 0, 1.0, 0.0).astype(BF16)

    row = lambda w: pl.BlockSpec((tm, w), lambda i: (i, 0))
    col = lambda w: pl.BlockSpec((w, tm), lambda i: (0, i))
    heads = lambda n: pl.BlockSpec((n, tm, LANE), lambda i: (0, i, 0))
    hs = lambda n: jax.ShapeDtypeStruct((n, s_len, LANE), BF16)
    return _pallas(
        body, name="pre", grid=(nt,),
        in_specs=[row(D_MODEL), row(LANE), row(LANE), row(LANE), row(LANE),
                  _full(w_in_ext.shape), _full(w_uq_pad.shape), _full(w_ukv_ext.shape)]
                 + [_full(g.shape) for g in gains],
        out_specs=[col(D_MODEL), row(N_GATE), row(N_PRE), row(B_HEADS * LANE), row(B_HEADS * LANE),
                   col(B_Q_RANK), col(B_KV_RANK),
                   heads(A_HEADS), heads(A_KV), heads(A_KV), heads(B_HEADS), heads(B_HEADS),
                   pl.BlockSpec((B_HEADS, tm, 2 * LANE), lambda i: (0, i, 0))],
        out_shape=[jax.ShapeDtypeStruct((D_MODEL, s_len), BF16), jax.ShapeDtypeStruct((s_len, N_GATE), F32),
                   jax.ShapeDtypeStruct((s_len, N_PRE), F32), jax.ShapeDtypeStruct((s_len, B_HEADS * LANE), F32),
                   jax.ShapeDtypeStruct((s_len, B_HEADS * LANE), F32),
                   jax.ShapeDtypeStruct((B_Q_RANK, s_len), BF16), jax.ShapeDtypeStruct((B_KV_RANK, s_len), BF16),
                   hs(A_HEADS), hs(A_KV), hs(A_KV), hs(B_HEADS), hs(B_HEADS),
                   jax.ShapeDtypeStruct((B_HEADS, s_len, 2 * LANE), BF16)],
        scratch_shapes=[pltpu.VMEM((tm, N_EXT), F32)],
        compiler_params=_params(("parallel",), VMEM_LIMIT),
    )(x, *tabs, w_in_ext, w_uq_pad, w_ukv_ext, *gains)


def _attn_fwd(q, k, v, group, l_col, tq, tk, tiles, name):
    n_heads, s_len, _ = q.shape
    v_w = v.shape[2]
    nk = s_len // tk

    def body(q_ref, k_ref, v_ref, o_ref, lse_ref, s_buf, p_buf, a_buf, m_ref, acc_ref):
        def scores(g, slot):
            s_buf[slot] = _nt(q_ref[_block_rows(g // nk, tq), :], k_ref[_block_rows(g % nk, tk), :])

        def softmax(g, slot):
            t = g // nk
            s = s_buf[slot]
            m_old = m_ref[t]
            m_new = jnp.maximum(m_old, jnp.max(s, axis=-1, keepdims=True))
            m_ref[t] = m_new
            a_buf[slot] = jnp.exp2(m_old - m_new)
            p_buf[slot] = jnp.exp2(s - jnp.tile(m_new, (1, tk // LANE))).astype(BF16)

        def values(g, slot):
            t = g // nk
            pv = _nn(p_buf[slot], v_ref[_block_rows(g % nk, tk), :])
            for c in range(0, v_w, LANE):
                acc_ref[t, :, c:c + LANE] = a_buf[slot] * acc_ref[t, :, c:c + LANE] + pv[:, c:c + LANE]

        m_ref[...] = jnp.full(m_ref.shape, -1e30, F32)
        acc_ref[...] = jnp.zeros(acc_ref.shape, F32)
        _three_stage(tiles * nk, scores, softmax, values)
        for t in range(tiles):
            l = acc_ref[t, :, l_col:l_col + 1]
            o = acc_ref[t, :, 0:LANE] * (1.0 / l)
            if l_col < LANE:
                lane = lax.broadcasted_iota(jnp.int32, o.shape, 1)
                o = jnp.where(lane == l_col, 0.0, o)
            o_ref[t * tq:(t + 1) * tq, :] = o
            lse_ref[t] = jnp.transpose(m_ref[t] + jnp.log2(jnp.broadcast_to(l, (tq, LANE))))[0:1, :]

    return _pallas(
        body, name=name, grid=(n_heads, s_len // (tiles * tq)),
        in_specs=[pl.BlockSpec((None, tiles * tq, LANE), lambda h, i: (h, i, 0)),
                  pl.BlockSpec((None, s_len, LANE), lambda h, i: (h // group, 0, 0)),
                  pl.BlockSpec((None, s_len, v_w), lambda h, i: (h // group, 0, 0))],
        out_specs=[pl.BlockSpec((None, tiles * tq, LANE), lambda h, i: (h, i, 0)),
                   pl.BlockSpec((None, tiles, 1, tq), lambda h, i: (h, i, 0, 0))],
        out_shape=[jax.ShapeDtypeStruct((n_heads, s_len, LANE), F32),
                   jax.ShapeDtypeStruct((n_heads, s_len // tq, 1, tq), F32)],
        scratch_shapes=[pltpu.VMEM((2, tq, tk), F32), pltpu.VMEM((2, tq, tk), BF16), pltpu.VMEM((2, tq, LANE), F32),
                        pltpu.VMEM((tiles, tq, LANE), F32), pltpu.VMEM((tiles, tq, v_w), F32)],
        compiler_params=_params(("parallel", "parallel"), 48 * 1024 * 1024),
    )(q, k, v)


def _mid(x, target, o_a, o_b, gates, w_out_ext, tm):
    s_len = x.shape[0]
    nt = s_len // tm
    n_heads = A_HEADS + B_HEADS

    def body(x_ref, t_ref, oa_ref, ob_ref, g_ref, w_ref,
             yt_ref, dh_ref, dgate_ref, doa_ref, dob_ref, delta_ref, loss_ref, silu_scr, dsilu_scr, y_ref):
        @pl.when(pl.program_id(0) == 0)
        def _():
            loss_ref[...] = jnp.zeros_like(loss_ref)

        def o_of(h):
            return oa_ref[h] if h < A_HEADS else ob_ref[h - A_HEADS]

        for h in range(n_heads):
            cols = slice(LANE * h, LANE * (h + 1))
            g = g_ref[:, cols]
            sig = 1.0 / (1.0 + jnp.exp(-g))
            silu = g * sig
            silu_scr[:, cols] = silu
            dsilu_scr[:, cols] = sig * (1.0 + g * (1.0 - sig))
            y = o_of(h) * silu
            y_ref[:, cols] = y.astype(BF16)
            yt_ref[cols, :] = jnp.transpose(y).astype(BF16)
        err =x_ref[...] + _nn(y_ref[...], w_ref[...]) - t_ref[...]
        sq = jnp.sum(jnp.sum(err * err, axis=-1, keepdims=True), axis=0, keepdims=True)
        loss_ref[...] += jnp.broadcast_to(sq * (0.5 / D_MODEL), loss_ref.shape)
        dh = err * (1.0 / D_MODEL)
        dh_ref[...] = dh
        dy = _nt(dh.astype(BF16), w_ref[...])
        lane = lax.broadcasted_iota(jnp.int32, (tm, LANE), 1)
        delta = jnp.zeros((tm, LANE), F32)
        for h in range(n_heads):
            cols = slice(LANE * h, LANE * (h + 1))
            dyh = dy[:, cols]
            oh = o_of(h)
            do = dyh * silu_scr[:, cols]
            dgate_ref[:, cols] = (dyh * oh * dsilu_scr[:, cols]).astype(BF16)
            delta = jnp.where(lane == h, jnp.sum(do * oh, axis=-1, keepdims=True), delta)
            if h < A_HEADS:
                doa_ref[h] = do.astype(BF16)
            else:
                dob_ref[h - A_HEADS] = do.astype(BF16)
        delta_ref[...] = jnp.transpose(delta)[0:DELTA_ROWS, :]

    row = lambda w: pl.BlockSpec((tm, w), lambda i: (i, 0))
    heads = lambda n, w=LANE: pl.BlockSpec((n, tm, w), lambda i: (0, i, 0))
    return _pallas(
        body, name="mid", grid=(nt,),
        in_specs=[row(D_MODEL), row(D_MODEL), heads(A_HEADS), heads(B_HEADS), row(N_GATE), _full(w_out_ext.shape)],
        out_specs=[pl.BlockSpec((N_GATE, tm), lambda i: (0, i)), row(D_MODEL), row(N_GATE), heads(A_HEADS), heads(B_HEADS),
                   pl.BlockSpec((DELTA_ROWS, tm), lambda i: (0, i)),
                   _full((8, LANE))],
        out_shape=[jax.ShapeDtypeStruct((N_GATE, s_len), BF16), jax.ShapeDtypeStruct((s_len, D_MODEL), F32),
                   jax.ShapeDtypeStruct((s_len, N_GATE), BF16),
                   jax.ShapeDtypeStruct((A_HEADS, s_len, LANE), BF16), jax.ShapeDtypeStruct((B_HEADS, s_len, LANE), BF16),
                   jax.ShapeDtypeStruct((DELTA_ROWS, s_len), F32), jax.ShapeDtypeStruct((8, LANE), F32)],
        scratch_shapes=[pltpu.VMEM((tm, N_GATE), F32), pltpu.VMEM((tm, N_GATE), F32), pltpu.VMEM((tm, N_GATE), BF16)],
        compiler_params=_params(("arbitrary",), VMEM_LIMIT),
    )(x, target, o_a, o_b, gates, w_out_ext)


def _attn_bwd(q, k, v, do, lse, delta, group, tq, tk, tiles, name):
    n_heads, s_len, _ = q.shape
    nq = s_len // tq

    def body(q_ref, do_ref, lse_ref, delta_ref, k_ref, v_ref, dq_ref, dk_ref, dv_ref, s_buf, dp_buf, p_buf, ds_buf):
        @pl.when(pl.program_id(1) == 0)
        def _():
            dq_ref[...] = jnp.zeros_like(dq_ref)

        dk_ref[...] = jnp.zeros_like(dk_ref)
        dv_ref[...] = jnp.zeros_like(dv_ref)

        def keys(g):
            return _block_rows(g // nq, tk)

        def queries(g):
            return _block_rows(g % nq, tq)

        def scores(g, slot):
            s_buf[slot] = _nt(k_ref[keys(g), :], q_ref[queries(g), :])
            dp_buf[slot] = _nt(v_ref[keys(g), :], do_ref[queries(g), :])

        def elementwise(g, slot):
            p = jnp.exp2(s_buf[slot] - lse_ref[g % nq])
            p_buf[slot] = p.astype(BF16)
            ds_buf[slot] = (p * (dp_buf[slot] - delta_ref[g % nq])).astype(BF16)

        def grads(g, slot):
            dv_ref[keys(g), :] += _nn(p_buf[slot], do_ref[queries(g), :])
            dk_ref[keys(g), :] += _nn(ds_buf[slot], q_ref[queries(g), :])
            dq_ref[queries(g), :] += _tn(ds_buf[slot], k_ref[keys(g), :])

        _three_stage(tiles * nq, scores, elementwise, grads)

    whole = lambda: pl.BlockSpec((None, s_len, LANE), lambda h, j: (h, 0, 0))
    stat = lambda: pl.BlockSpec((None, nq, 1, tq), lambda h, j: (h, 0, 0, 0))
    kvb = lambda: pl.BlockSpec((None, tiles * tk, LANE), lambda h, j: (h // group, j, 0))
    outb = lambda: pl.BlockSpec((None, tiles * tk, LANE), lambda h, j: (h, j, 0))
    shape = jax.ShapeDtypeStruct((n_heads, s_len, LANE), F32)
    return _pallas(
        body, name=name, grid=(n_heads, s_len // (tiles * tk)),
        in_specs=[whole(), whole(), stat(), stat(), kvb(), kvb()],
        out_specs=[whole(), outb(), outb()],
        out_shape=[shape, shape, shape],
        scratch_shapes=[pltpu.VMEM((2, tk, tq), F32), pltpu.VMEM((2, tk, tq), F32),
                        pltpu.VMEM((2, tk, tq), BF16), pltpu.VMEM((2, tk, tq), BF16)],
        compiler_params=_params(("parallel", "arbitrary"), 48 * 1024 * 1024),
    )(q, do, lse, delta, k, v)


def _post(x, dh, pre, qbpre, kbpre, dgate, dqa, dka, dva, dqb, dkb, dvb, loss_part, tabs,
          w_in_ext, w_uq_pad, w_ukv_ext, gains, tm):
    s_len = x.shape[0]
    nt = s_len // tm

    def body(x_ref, dh_ref, pre_ref, qbpre_ref, kbpre_ref, dgate_ref,
             dqa_ref, dka_ref, dva_ref, dqb_ref, dkb_ref, dvb_ref, loss_ref,
             ca_ref, sa_ref, cb_ref, sb_ref, win_ref, wuq_ref, wukv_ref,
             gin_ref, gaq_ref, gak_ref, gcq_ref, gckv_ref, gbq_ref, gbk_ref,
             gx_ref, dproj_ref, dqbpre_ref, dkvb_ref, dsm_ref):
        @pl.when(pl.program_id(0) == 0)
        def _():
            dsm_ref[...] = jnp.zeros_like(dsm_ref)
            dsm_ref[SM_LOSS:SM_LOSS + 1, 0:LANE] = loss_ref[0:1, :]

        def add_small(r, dg):
            dsm_ref[r:r + 1, 0:dg.shape[1]] += dg

        def tok_sum(a):
            return jnp.sum(a, axis=0, keepdims=True)

        ca, sa, cb, sb = ca_ref[...], sa_ref[...], cb_ref[...], sb_ref[...]
        lane = lax.broadcasted_iota(jnp.int32, (tm, LANE), 1)

        nope_lanes = _lanes_of(lane, LAY_NOPE)

        def back(c0, c1):
            return _nt(dproj_ref[:, c0:c1], win_ref[:, c0:c1])

        dproj_ref[:, GA0:GA0 + N_GATE] = dgate_ref[...]
        dxn = back(GA0, GA0 + N_GATE)
        dg = jnp.zeros((1, LANE), F32)
        for h in range(A_HEADS):
            dn = _rope_bwd(dqa_ref[h] * SCALE_A, ca, sa)
            dx, dgr = _rms_bwd(dn, pre_ref[:, QA0 + LANE * h:QA0 + LANE * (h + 1)], gaq_ref[...], A_DIM)
            dproj_ref[:, QA0 + LANE * h:QA0 + LANE * (h + 1)] = dx.astype(BF16)
            dg = dg + tok_sum(dgr)
        add_small(SM_AQ, _unspread_row(dg, LAY_ROPE_A))
        dxn = dxn + back(QA0, KA0)
        dg = jnp.zeros((1, LANE), F32)
        for h in range(A_KV):
            dk = dka_ref[A_GROUP * h]
            dv = dva_ref[A_GROUP * h]
            for g in range(1, A_GROUP):
                dk = dk + dka_ref[A_GROUP * h + g]
                dv = dv + dva_ref[A_GROUP * h + g]
            dn = _rope_bwd(dk * LN2, ca, sa)
            dx, dgr = _rms_bwd(dn, pre_ref[:, KA0 + LANE * h:KA0 + LANE * (h + 1)], gak_ref[...], A_DIM)
            dproj_ref[:, KA0 + LANE * h:KA0 + LANE * (h + 1)] = dx.astype(BF16)
            dproj_ref[:, VA0 + LANE * h:VA0 + LANE * (h + 1)] = dv.astype(BF16)
            dg = dg + tok_sum(dgr)
        add_small(SM_AK, _unspread_row(dg, LAY_ROPE_A))
        dxn = dxn + back(KA0, GA0)
        dg = jnp.zeros((1, LANE), F32)
        for h in range(B_HEADS):
            cols = slice(LANE * h, LANE * (h + 1))
            dn = _rope_bwd(dqb_ref[h] * SCALE_B, cb, sb)
            dx, dgr = _rms_bwd(dn, qbpre_ref[:, cols], gbq_ref[...], B_QK)
            dqbpre_ref[:, cols] = dx.astype(BF16)
            dg = dg + tok_sum(dgr)
        add_small(SM_BQ, _unspread_row(dg, LAY_ROPE_B))
        dcq = _nt(dqbpre_ref[...], wuq_ref[...])
        dx, dgr = _rms_bwd(dcq, pre_ref[:, VA0:VA0 + B_Q_RANK], gcq_ref[...], B_Q_RANK)
        dproj_ref[:, CQ0:CQ0 + B_Q_RANK] = dx.astype(BF16)
        add_small(SM_CQ, tok_sum(dgr))
        dxn = dxn + back(CQ0, CKV0)
        dg = jnp.zeros((1, LANE), F32)
        dkr = jnp.zeros((tm, LANE), F32)
        for h in range(B_HEADS):
            cols = slice(LANE * h, LANE * (h + 1))
            dn = _rope_bwd(dkb_ref[h] * LN2, cb, sb)
            dx, dgr = _rms_bwd(dn, kbpre_ref[:, cols], gbk_ref[...], B_QK)
            dkvb_ref[:, cols] = jnp.where(nope_lanes, dx, 0.0).astype(BF16)
            dkvb_ref[:, B_HEADS * LANE + LANE * h:B_HEADS * LANE + LANE * (h + 1)] = dvb_ref[h].astype(BF16)
            dkr = dkr + dx
            dg = dg + tok_sum(dgr)
        add_small(SM_BK, _unspread_row(dg, LAY_ROPE_B))
        dproj_ref[:, KR0:KR0 + LANE] = jnp.where(_lanes_of(lane, LAY_KR), dkr, 0.0).astype(BF16)
        dckv = _nt(dkvb_ref[...], wukv_ref[...])
        dx, dgr = _rms_bwd(dckv, pre_ref[:, VA0 + B_Q_RANK:N_PRE], gckv_ref[...], B_KV_RANK)
        dproj_ref[:, CKV0:CKV0 + B_KV_RANK] = dx.astype(BF16)
        add_small(SM_CKV, tok_sum(dgr))
        dxn = dxn + back(CKV0, N_EXT)
        dx, dgr = _rms_bwd(dxn, x_ref[...], gin_ref[...], D_MODEL)
        gx_ref[...] = dh_ref[...] + dx
        add_small(SM_IN, tok_sum(dgr))

    row = lambda w: pl.BlockSpec((tm, w), lambda i: (i, 0))
    heads = lambda n: pl.BlockSpec((n, tm, LANE), lambda i: (0, i, 0))
    return _pallas(
        body, name="post", grid=(nt,),
        in_specs=[row(D_MODEL), row(D_MODEL), row(N_PRE), row(B_HEADS * LANE), row(B_HEADS * LANE), row(N_GATE),
                  heads(A_HEADS), heads(A_HEADS), heads(A_HEADS), heads(B_HEADS), heads(B_HEADS), heads(B_HEADS),
                  _full(loss_part.shape), row(LANE), row(LANE), row(LANE), row(LANE),
                  _full(w_in_ext.shape), _full(w_uq_pad.shape), _full(w_ukv_ext.shape)]
                 + [_full(g.shape) for g in gains],
        out_specs=[row(D_MODEL), row(N_EXT), row(B_HEADS * LANE), row(2 * B_HEADS * LANE), _full((SM_ROWS, SM_W))],
        out_shape=[jax.ShapeDtypeStruct((s_len, D_MODEL), F32), jax.ShapeDtypeStruct((s_len, N_EXT), BF16),
                   jax.ShapeDtypeStruct((s_len, B_HEADS * LANE), BF16),
                   jax.ShapeDtypeStruct((s_len, 2 * B_HEADS * LANE), BF16),
                   jax.ShapeDtypeStruct((SM_ROWS, SM_W), F32)],
        compiler_params=_params(("arbitrary",), VMEM_LIMIT),
    )(x, dh, pre, qbpre, kbpre, dgate, dqa, dka, dva, dqb, dkb, dvb, loss_part, *tabs,
      w_in_ext, w_uq_pad, w_ukv_ext, *gains)


def _grad_w(a_t, b, tn, ts, name):
    m, s_len = a_t.shape
    n = b.shape[1]

    def body(a_ref, b_ref, o_ref):
        @pl.when(pl.program_id(1) == 0)
        def _():
            o_ref[...] = jnp.zeros_like(o_ref)

        o_ref[...] += _nn(a_ref[...], b_ref[...].astype(BF16))

    return _pallas(
        body, name=name, grid=(n // tn, s_len // ts),
        in_specs=[pl.BlockSpec((m, ts), lambda j, t: (0, t)), pl.BlockSpec((ts, tn), lambda j, t: (t, j))],
        out_specs=pl.BlockSpec((m, tn), lambda j, t: (0, j)),
        out_shape=jax.ShapeDtypeStruct((m, n), F32),
        compiler_params=_params(("parallel", "arbitrary"), 48 * 1024 * 1024),
    )(a_t, b)


def _adam_math(w, g, m, v):
    nm = ADAM_B1 * m + (1.0 - ADAM_B1) * g
    nv = ADAM_B2 * v + (1.0 - ADAM_B2) * (g * g)
    m_hat = nm / (1.0 - ADAM_B1 ** ADAM_STEP)
    v_hat = nv / (1.0 - ADAM_B2 ** ADAM_STEP)
    return -ADAM_LR * (m_hat / (jnp.sqrt(v_hat) + ADAM_EPS) + ADAM_WD * w), nm, nv


def _adamw_rows(w, g, m, v, tr):
    rows, cols = w.shape

    def body(w_ref, g_ref, m_ref, v_ref, d_ref, nm_ref, nv_ref):
        d_ref[...], nm_ref[...], nv_ref[...] = _adam_math(w_ref[...], g_ref[...], m_ref[...], v_ref[...])

    blk = pl.BlockSpec((tr, cols), lambda i: (i, 0))
    shape = jax.ShapeDtypeStruct((rows, cols), F32)
    return _pallas(
        body, name="adamw_w_in", grid=(rows // tr,),
        in_specs=[blk] * 4, out_specs=[blk] * 3, out_shape=[shape] * 3,
        compiler_params=_params(("parallel",), 32 * 1024 * 1024),
    )(w, g, m, v)


def _adamw_rest(bigs, smalls, g_small):
    nb, ns = len(bigs), len(smalls)

    def body(*refs):
        ins, outs = refs[:4 * nb + 3 * ns + 1], refs[4 * nb + 3 * ns + 1:]
        for i in range(nb):
            w_ref, g_ref, m_ref, v_ref = ins[4 * i:4 * i + 4]
            d_ref, nm_ref, nv_ref = outs[3 * i:3 * i + 3]
            d_ref[...], nm_ref[...], nv_ref[...] = _adam_math(w_ref[...], g_ref[...], m_ref[...], v_ref[...])
        gs_ref = ins[-1]
        for i in range(ns):
            w_ref, m_ref, v_ref = ins[4 * nb + 3 * i:4 * nb + 3 * i + 3]
            g_ref, d_ref, nm_ref, nv_ref = outs[3 * nb + 4 * i:3 * nb + 4 * i + 4]
            g = gs_ref[i:i + 1, 0:w_ref.shape[1]]
            g_ref[...] = g
            d_ref[...], nm_ref[...], nv_ref[...] = _adam_math(w_ref[...], g, m_ref[...], v_ref[...])

    flat_in = [a for quad in bigs for a in quad] + [a for tri in smalls for a in tri] + [g_small]
    out_shape = ([jax.ShapeDtypeStruct(q[0].shape, F32) for q in bigs for _ in range(3)]
                 + [jax.ShapeDtypeStruct(t[0].shape, F32) for t in smalls for _ in range(4)])
    return _pallas(
        body, name="adamw_rest",
        in_specs=[pl.BlockSpec(memory_space=pltpu.VMEM)] * len(flat_in),
        out_specs=[pl.BlockSpec(memory_space=pltpu.VMEM)] * len(out_shape),
        out_shape=out_shape,
        compiler_params=_params(vmem=32 * 1024 * 1024),
    )(*flat_in)


def _place(pieces, n):
    out, at = [], 0
    for lane0, arr in sorted(pieces, key=lambda p: p[0]):
        out += [jnp.zeros((n, lane0 - at), F32), arr]
        at = lane0 + arr.shape[1]
    return jnp.concatenate(out + [jnp.zeros((n, LANE - at), F32)], axis=1)


def _rope_tables(s_len):
    rows = s_len // GRID_W
    row = jnp.arange(rows, dtype=F32)
    col = jnp.arange(GRID_W, dtype=F32)

    def lay(dim, layout, first_dim, ones):
        half = dim // 2
        inv = 1.0 / (ROPE_THETA ** (jnp.arange(0, half, 2, dtype=F32) / half))
        ang_r, ang_c = row[:, None] * inv[None, :], col[:, None] * inv[None, :]
        at = {a - first_dim: lane0 for a, _, lane0 in layout}
        q = dim // 4
        r1, r2, c1, c2 = at[0], at[q], at[2 * q], at[3 * q]
        cos_r = _place([(r1, jnp.cos(ang_r)), (r2, jnp.cos(ang_r))], rows)
        sin_r = _place([(r1, -jnp.sin(ang_r)), (r2, jnp.sin(ang_r))], rows)
        cos_c = _place([(c1, jnp.cos(ang_c)), (c2, jnp.cos(ang_c))] + [(l0, jnp.ones((GRID_W, n), F32)) for _, n, l0 in ones],
                       GRID_W)
        sin_c = _place([(c1, -jnp.sin(ang_c)), (c2, jnp.sin(ang_c))], GRID_W)
        cos = (cos_r[:, None, :] + cos_c[None, :, :]).reshape(s_len, LANE)
        sin = (sin_r[:, None, :] + sin_c[None, :, :]).reshape(s_len, LANE)
        return cos, sin

    cos_a, sin_a = lay(A_DIM, LAY_ROPE_A, 0, ())
    cos_b, sin_b = lay(B_ROPE, LAY_KR, 0, LAY_NOPE)
    return cos_a, sin_a, cos_b, sin_b


def _spread(w, n_heads, dim, axis, layout):
    w3 = w.reshape(w.shape[:axis] + (n_heads, dim) + w.shape[axis + 1:])
    out, at = [], 0

    def zeros(n):
        return jnp.zeros(w3.shape[:axis + 1] + (n,) + w3.shape[axis + 2:], w.dtype)

    for a0, n, lane0 in sorted(layout, key=lambda seg: seg[2]):
        out += [zeros(lane0 - at), lax.slice_in_dim(w3, a0, a0 + n, axis=axis + 1)]
        at = lane0 + n
    out = jnp.concatenate(out + [zeros(LANE - at)], axis=axis + 1)
    return out.reshape(w.shape[:axis] + (n_heads * LANE,) + w.shape[axis + 1:])


def _unspread(w, n_heads, dim, axis, layout):
    w3 = w.reshape(w.shape[:axis] + (n_heads, LANE) + w.shape[axis + 1:])
    parts = [lax.slice_in_dim(w3, lane0, lane0 + n, axis=axis + 1) for _, n, lane0 in sorted(layout)]
    out = parts[0] if len(parts) == 1 else jnp.concatenate(parts, axis=axis + 1)
    return out.reshape(w.shape[:axis] + (n_heads * dim,) + w.shape[axis + 1:])


def _ext_weights(g_in, g_uq, g_ukv, g_out):
    w_in = g_in.transpose(1, 0, 2).reshape(D_MODEL, N_IN)
    w_uq = g_uq.reshape(B_Q_RANK, B_HEADS * B_QK)
    w_ukv = g_ukv.transpose(1, 0, 2).reshape(B_KV_RANK, B_HEADS * (B_NOPE + B_V))
    w_out = g_out.reshape(D_MODEL, D_MODEL)
    a_w = A_HEADS * A_DIM
    kv_w = A_KV * A_DIM
    o = 0
    secs = []
    for width, heads, layout in ((a_w, A_HEADS, LAY_ROPE_A), (kv_w, A_KV, LAY_ROPE_A), (kv_w, A_KV, LAY_PLAIN_A),
                                 (a_w, A_HEADS, LAY_PLAIN_A)):
        secs.append(_spread(w_in[:, o:o + width], heads, A_DIM, 1, layout))
        o += width
    a_q, a_k, a_v, a_g = secs
    b_cq = w_in[:, o:o + B_Q_RANK]
    o += B_Q_RANK
    b_ckv = w_in[:, o:o + B_KV_RANK]
    o += B_KV_RANK
    b_kr = _spread(w_in[:, o:o + B_ROPE], 1, B_ROPE, 1, LAY_KR)
    o += B_ROPE
    b_g = w_in[:, o:]
    w_in_ext = jnp.concatenate([a_q, a_k, a_v, a_g, b_g, b_cq, b_ckv, b_kr], axis=1)
    w_uq_pad = _spread(w_uq, B_HEADS, B_QK, 1, LAY_ROPE_B)
    kv3 = w_ukv.reshape(B_KV_RANK, B_HEADS, B_NOPE + B_V)
    w_ukv_ext = jnp.concatenate([_spread(kv3[:, :, :B_NOPE].reshape(B_KV_RANK, B_HEADS * B_NOPE), B_HEADS, B_NOPE, 1, LAY_NOPE),
                                 kv3[:, :, B_NOPE:].reshape(B_KV_RANK, B_HEADS * B_V)], axis=1)
    w_out_ext = jnp.concatenate([_spread(w_out[:a_w], A_HEADS, A_DIM, 0, LAY_PLAIN_A), w_out[a_w:]], axis=0)
    return w_in_ext, w_uq_pad, w_ukv_ext, w_out_ext


def _fold_grads(d_in_ext, d_uq_pad, d_ukv_ext, d_out_ext):
    d_in = jnp.concatenate([
        _unspread(d_in_ext[:, QA0:KA0], A_HEADS, A_DIM, 1, LAY_ROPE_A),
        _unspread(d_in_ext[:, KA0:VA0], A_KV, A_DIM, 1, LAY_ROPE_A),
        _unspread(d_in_ext[:, VA0:GA0], A_KV, A_DIM, 1, LAY_PLAIN_A),
        _unspread(d_in_ext[:, GA0:GB0], A_HEADS, A_DIM, 1, LAY_PLAIN_A),
        d_in_ext[:, CQ0:KR0],
        _unspread(d_in_ext[:, KR0:N_EXT], 1, B_ROPE, 1, LAY_KR),
        d_in_ext[:, GB0:CQ0]], axis=1)
    d_uq = _unspread(d_uq_pad, B_HEADS, B_QK, 1, LAY_ROPE_B)
    k3 = _unspread(d_ukv_ext[:, :B_HEADS * LANE], B_HEADS, B_NOPE, 1, LAY_NOPE).reshape(B_KV_RANK, B_HEADS, B_NOPE)
    v3 = d_ukv_ext[:, B_HEADS * LANE:].reshape(B_KV_RANK, B_HEADS, B_V)
    d_ukv = jnp.concatenate([k3, v3], axis=2).reshape(B_KV_RANK, B_HEADS * (B_NOPE + B_V))
    d_out = jnp.concatenate([_unspread(d_out_ext[:A_HEADS * LANE], A_HEADS, A_DIM, 0, LAY_PLAIN_A), d_out_ext[A_HEADS * LANE:]], axis=0)
    return (d_in.reshape(D_MODEL, N_CHIPS, SH_IN[1]).transpose(1, 0, 2),
            d_uq.reshape((N_CHIPS,) + SH_UQ),
            d_ukv.reshape(B_KV_RANK, N_CHIPS, SH_UKV[1]).transpose(1, 0, 2),
            d_out.reshape((N_CHIPS,) + SH_OUT))


def kernel(x, norm_in, w_in, a_q_norm, a_k_norm, b_cq_norm, b_ckv_norm, w_uq, w_ukv, b_q_norm, b_k_norm, w_out, loss_target, m_norm_in, m_w_in, m_a_q_norm, m_a_k_norm, m_b_cq_norm, m_b_ckv_norm, m_w_uq, m_w_ukv, m_b_q_norm, m_b_k_norm, m_w_out, v_norm_in, v_w_in, v_a_q_norm, v_a_k_norm, v_b_cq_norm, v_b_ckv_norm, v_w_uq, v_w_ukv, v_b_q_norm, v_b_k_norm, v_w_out):
    s_len = x.shape[1]
    xs, ts = x[0], loss_target[0]
    tm = min(256, s_len)
    tq, tk_f = min(512, s_len // 2), min(1024, s_len // 2)
    tq_b, tk_b = min(1024, s_len // 2), min(512, s_len)
    tiles_f = min(4, s_len // tq)
    tiles_b = min(2, s_len // tk_b)

    w_in_ext, w_uq_pad, w_ukv_ext, w_out_ext = _ext_weights(*_gather_weights((w_in[0], w_uq[0], w_ukv[0], w_out[0])))
    gains = (norm_in, _spread(a_q_norm, 1, A_DIM, 1, LAY_ROPE_A), _spread(a_k_norm, 1, A_DIM, 1, LAY_ROPE_A), b_cq_norm, b_ckv_norm,
             _spread(b_q_norm, 1, B_QK, 1, LAY_ROPE_B), _spread(b_k_norm, 1, B_QK, 1, LAY_ROPE_B))
    tabs = _rope_tables(s_len)

    (xn_t, gates, pre, qbpre, kbpre, cq_t, ckv_t, qa, ka, va, qb, kb, vb) = _pre(
        xs, tabs, w_in_ext, w_uq_pad, w_ukv_ext, gains, tm)
    o_a, lse_a = _attn_fwd(qa, ka, va, A_GROUP, A_DIM, tq, tk_f, tiles_f, "attn_fwd_a")
    o_b, lse_b = _attn_fwd(qb, kb, vb, 1, B_V, tq, tk_f, tiles_f, "attn_fwd_b")
    y_t, dh, dgate, do_a, do_b, delta, loss_part = _mid(xs, ts, o_a, o_b, gates, w_out_ext, tm)

    def stat(a):
        return a.reshape(a.shape[0], s_len // tq_b, 1, tq_b)

    dqa, dka, dva = _attn_bwd(qa, ka, va, do_a, stat(lse_a), stat(delta[:A_HEADS]), A_GROUP, tq_b, tk_b, tiles_b, "attn_bwd_a")
    dqb, dkb, dvb = _attn_bwd(qb, kb, vb, do_b, stat(lse_b), stat(delta[A_HEADS:A_HEADS + B_HEADS]), 1, tq_b, tk_b,
                              tiles_b, "attn_bwd_b")
    grad_x, dproj, dqbpre, dkvb, d_small = _post(
        xs, dh, pre, qbpre, kbpre, dgate, dqa, dka, dva, dqb, dkb, dvb, loss_part, tabs,
        w_in_ext, w_uq_pad, w_ukv_ext, gains, tm)

    ts_w = min(2048, s_len)
    d_in_ext = _grad_w(xn_t, dproj, 768, ts_w, "grad_w_in")
    d_out_ext = _grad_w(y_t, dh, 512, ts_w, "grad_w_out")
    d_uq_pad = _grad_w(cq_t, dqbpre, 512, ts_w, "grad_w_uq")
    d_ukv_ext = _grad_w(ckv_t, dkvb, 1024, ts_w, "grad_w_ukv")

    g_in, g_uq, g_ukv, g_out, g_small = _reduce_grads(_fold_grads(d_in_ext, d_uq_pad, d_ukv_ext, d_out_ext), d_small)
    d_in, nm_in, nv_in = (a.T for a in _adamw_rows(w_in[0].T, g_in.T, m_w_in[0].T, v_w_in[0].T, SH_IN[1] // 7))
    rest = _adamw_rest(
        [(w_uq[0], g_uq, m_w_uq[0], v_w_uq[0]), (w_ukv[0], g_ukv, m_w_ukv[0], v_w_ukv[0]),
         (w_out[0], g_out, m_w_out[0], v_w_out[0])],
        [(norm_in, m_norm_in, v_norm_in), (a_q_norm, m_a_q_norm, v_a_q_norm), (a_k_norm, m_a_k_norm, v_a_k_norm),
         (b_cq_norm, m_b_cq_norm, v_b_cq_norm), (b_ckv_norm, m_b_ckv_norm, v_b_ckv_norm),
         (b_q_norm, m_b_q_norm, v_b_q_norm), (b_k_norm, m_b_k_norm, v_b_k_norm)], g_small)
    (d_uq, nm_uq, nv_uq), (d_ukv, nm_ukv, nv_ukv), (d_out, nm_out, nv_out) = (rest[3 * i:3 * i + 3] for i in range(3))
    sm = [rest[9 + 4 * i:9 + 4 * i + 4] for i in range(7)]

    def leaves(k, p_in, p_uq, p_ukv, p_out):
        return [sm[SM_IN][k], p_in[None], sm[SM_AQ][k], sm[SM_AK][k], sm[SM_CQ][k], sm[SM_CKV][k], p_uq[None], p_ukv[None],
                sm[SM_BQ][k], sm[SM_BK][k], p_out[None]]

    return (g_small[SM_LOSS, 0], grad_x[None], *leaves(0, g_in, g_uq, g_ukv, g_out), *leaves(1, d_in, d_uq, d_ukv, d_out),
            *leaves(2, nm_in, nm_uq, nm_ukv, nm_out), *leaves(3, nv_in, nv_uq, nv_ukv, nv_out))
```

```python
import jax
import jax.numpy as jnp
import numpy as np
from jax import lax
from jax.experimental import pallas as pl
from jax.experimental.pallas import tpu as pltpu

F32 = jnp.float32
BF16 = jnp.bfloat16
MESH = pl.DeviceIdType.MESH

D_MODEL = 1024
GRID_W = 64
ROPE_THETA = 10000.0
EPS = 1e-6
A_HEADS, A_KV, A_DIM = 8, 2, 64
A_GROUP = A_HEADS // A_KV
B_HEADS, B_NOPE, B_ROPE, B_V = 4, 64, 32, 128
B_QK = B_NOPE + B_ROPE
B_Q_RANK, B_KV_RANK = 384, 256
N_IN = 2464
SCALE_A = 1.0 / float(np.sqrt(A_DIM))
SCALE_B = 1.0 / float(np.sqrt(B_QK))
LOG2E = float(np.log2(np.e))
LN2 = float(np.log(2.0))
ADAM_LR, ADAM_B1, ADAM_B2, ADAM_EPS, ADAM_WD, ADAM_STEP = 0.001, 0.9, 0.999, 1e-08, 0.01, 10

LANE = 128
VMEM_BYTES = 64 * 1024 * 1024
VMEM_LIMIT = VMEM_BYTES - 8 * 1024 * 1024

QA0 = 0
KA0 = QA0 + A_HEADS * LANE
VA0 = KA0 + A_KV * LANE
GA0 = VA0 + A_KV * LANE
GB0 = GA0 + A_HEADS * LANE
CQ0 = GB0 + B_HEADS * LANE
CKV0 = CQ0 + B_Q_RANK
KR0 = CKV0 + B_KV_RANK
N_EXT = KR0 + LANE
N_GATE = (A_HEADS + B_HEADS) * LANE
DELTA_ROWS = 16
N_PRE = KA0 + A_KV * LANE + B_Q_RANK + B_KV_RANK

ROT = LANE // 2
_QA = A_DIM // 4
_QB = B_ROPE // 4
LAY_PLAIN_A = ((0, A_DIM, 0),)
LAY_ROPE_A = ((0, _QA, 0), (2 * _QA, _QA, _QA), (_QA, _QA, ROT), (3 * _QA, _QA, ROT + _QA))
LAY_KR = ((0, _QB, 0), (2 * _QB, _QB, _QB), (_QB, _QB, ROT), (3 * _QB, _QB, ROT + _QB))
LAY_NOPE = ((0, B_NOPE // 2, 2 * _QB), (B_NOPE // 2, B_NOPE // 2, ROT + 2 * _QB))
LAY_ROPE_B = LAY_NOPE + tuple((B_NOPE + a, n, at) for a, n, at in LAY_KR)

N_CHIPS = 4
SH_IN = (D_MODEL, N_IN // N_CHIPS)
SH_UQ = (B_Q_RANK // N_CHIPS, B_HEADS * B_QK)
SH_UKV = (B_KV_RANK, B_HEADS * (B_NOPE + B_V) // N_CHIPS)
SH_OUT = (D_MODEL // N_CHIPS, D_MODEL)
SM_ROWS, SM_W = 16, D_MODEL
SM_IN, SM_AQ, SM_AK, SM_CQ, SM_CKV, SM_BQ, SM_BK, SM_LOSS = range(8)
F32_ROWS, BF16_ROWS = 8, 16


def _pallas(body, **kw):
    return pl.pallas_call(body, **kw)


def _params(sem=None, vmem=None):
    return pltpu.CompilerParams(dimension_semantics=sem, vmem_limit_bytes=vmem)


def _rms_fwd(x, g, n):
    r = lax.rsqrt(jnp.sum(x * x, axis=-1, keepdims=True) * (1.0 / n) + EPS)
    return x * r * g


def _rms_bwd(dy, x, g, n):
    u = dy * g
    r = lax.rsqrt(jnp.sum(x * x, axis=-1, keepdims=True) * (1.0 / n) + EPS)
    ux = jnp.sum(u * x, axis=-1, keepdims=True)
    xhat = x * r
    dx = r * (u - xhat * (r * ux * (1.0 / n)))
    return dx, dy * xhat


def _rope_fwd(y, cos, sin):
    return y * cos + pltpu.roll(y, ROT, 1) * sin


def _rope_bwd(d, cos, sin):
    return d * cos - pltpu.roll(d, ROT, 1) * sin


def _lanes_of(lane, layout):
    m = None
    for _, n, at in layout:
        seg = (lane >= at) & (lane < at + n)
        m = seg if m is None else (m | seg)
    return m


def _unspread_row(v, layout):
    v8 = jnp.broadcast_to(v, (F32_ROWS, LANE))
    lane = lax.broadcasted_iota(jnp.int32, v8.shape, 1)
    out = jnp.zeros_like(v8)
    for a, n, at in layout:
        moved = v8 if a == at else pltpu.roll(v8, (a - at) % LANE, 1)
        out = jnp.where((lane >= a) & (lane < a + n), moved, out)
    return out[0:1, :]


def _nt(a, b):
    return lax.dot_general(a, b, (((1,), (1,)), ((), ())), preferred_element_type=F32)


def _tn(a, b):
    return lax.dot_general(a, b, (((0,), (0,)), ((), ())), preferred_element_type=F32)


def _nn(a, b):
    return jnp.dot(a, b, preferred_element_type=F32)


def _block_rows(i, size):
    if isinstance(i, int):
        return pl.ds(i * size, size)
    return pl.ds(pl.multiple_of(i * size, size), size)


MAX_STATIC_BLOCKS = 32


def _three_stage(n, first, second, third):
    assert n >= 2 and n % 2 == 0
    first(0, 0)
    first(1, 1)
    second(0, 0)
    if n <= MAX_STATIC_BLOCKS:
        for i in range(1, n - 1):
            first(i + 1, (i + 1) % 2)
            second(i, i % 2)
            third(i - 1, (i - 1) % 2)
    else:
        def pair(t, carry):
            i = 2 * t + 1
            first(i + 1, 0)
            second(i, 1)
            third(i - 1, 0)
            first(i + 2, 1)
            second(i + 1, 0)
            third(i, 1)
            return carry

        lax.fori_loop(0, (n - 2) // 2, pair, 0)
    second(n - 1, 1)
    third(n - 2, 0)
    third(n - 1, 1)


def _full(shape):
    return pl.BlockSpec(shape, lambda *_: (0,) * len(shape))


def _gather_weights(shards):
    n = len(shards)
    halves = [w.shape[0] // 2 for w in shards]

    def body(*refs):
        w_refs, out_refs, (send_sems, recv_sems) = refs[:n], refs[n:2 * n], refs[2 * n:]
        x, y, c = lax.axis_index("x"), lax.axis_index("y"), lax.axis_index("c")
        sibling = (x, y, 1 - c)
        chips = [(1 - x, y), (x, 1 - y), (1 - x, 1 - y)]
        me = 2 * x + y

        def copy(a, k, j, hc, to):
            part = out_refs[a].at[j, pl.ds(pl.multiple_of(hc * halves[a], BF16_ROWS), halves[a]), :]
            return pltpu.make_async_remote_copy(
                src_ref=part, dst_ref=part, send_sem=send_sems.at[6 * a + k], recv_sem=recv_sems.at[6 * a + k],
                device_id=to, device_id_type=MESH)

        started = []
        for a in range(n):
            out_refs[a][me] = w_refs[a][...].astype(BF16)
            for k, chip in enumerate(chips):
                started.append(copy(a, k, me, c, (*chip, c)))
                started[-1].start()
        for k, chip in enumerate(chips):
            for a in range(n):
                copy(a, k, 2 * chip[0] + chip[1], c, (*chip, c)).wait_recv()
                started.append(copy(a, 3 + k, 2 * chip[0] + chip[1], c, sibling))
                started[-1].start()
        for k, chip in enumerate(chips):
            for a in range(n):
                copy(a, 3 + k, 2 * chip[0] + chip[1], 1 - c, sibling).wait_recv()
        for cp in started:
            cp.wait_send()

    return _pallas(
        body, name="gather_weights",
        out_shape=[jax.ShapeDtypeStruct((N_CHIPS,) + w.shape, BF16) for w in shards],
        in_specs=[pl.BlockSpec(memory_space=pltpu.VMEM)] * n,
        out_specs=[pl.BlockSpec(memory_space=pltpu.VMEM)] * n,
        scratch_shapes=[pltpu.SemaphoreType.DMA((6 * n,)), pltpu.SemaphoreType.DMA((6 * n,))],
        compiler_params=_params(vmem=32 * 1024 * 1024),
    )(*shards)


def _reduce_grads(parts, small):
    n_big = len(parts)
    n = n_big + 1
    shapes = [p.shape[1:] for p in parts] + [small.shape]
    halves = [sh[0] // 2 for sh in shapes]

    def body(*refs):
        p_refs, out_refs, rec_a, rec_b = refs[:n], refs[n:2 * n], refs[2 * n:3 * n], refs[3 * n:4 * n]
        send_b = refs[4 * n:4 * n + n_big]
        sa_send, sa_recv, sb_send, sb_recv, sc_send, sc_recv = refs[4 * n + n_big:]
        x, y, c = lax.axis_index("x"), lax.axis_index("y"), lax.axis_index("c")
        sibling = (x, y, 1 - c)
        me = 2 * x + y

        def rows(a, hc):
            return pl.ds(pl.multiple_of(hc * halves[a], F32_ROWS), halves[a])

        def partial(a, j, hc):
            return p_refs[a].at[j, rows(a, hc), :] if a < n_big else p_refs[a].at[rows(a, hc), :]

        def copy_a(a, j):
            return pltpu.make_async_remote_copy(
                src_ref=partial(a, j, 1 - c), dst_ref=rec_a[a].at[j],
                send_sem=sa_send.at[N_CHIPS * a + j], recv_sem=sa_recv.at[N_CHIPS * a + j],
                device_id=sibling, device_id_type=MESH)

        def copy_b(a, r):
            j = me ^ r
            k = (N_CHIPS - 1) * a + r - 1
            return pltpu.make_async_remote_copy(
                src_ref=(send_b[a] if a < n_big else rec_a[a]).at[j], dst_ref=rec_b[a].at[r],
                send_sem=sb_send.at[k], recv_sem=sb_recv.at[k], device_id=(j // 2, j % 2, c), device_id_type=MESH)

        def copy_c(a):
            return pltpu.make_async_remote_copy(
                src_ref=out_refs[a].at[rows(a, c), :], dst_ref=out_refs[a].at[rows(a, c), :],
                send_sem=sc_send.at[a], recv_sem=sc_recv.at[a], device_id=sibling, device_id_type=MESH)

        for a in range(n):
            for j in range(N_CHIPS):
                copy_a(a, j).start()
        for r in range(1, N_CHIPS):
            j = me ^ r
            for a in range(n):
                copy_a(a, j).wait_recv()
                chip_part = rec_a[a][j] + partial(a, j, c)[...]
                if a < n_big:
                    send_b[a][j] = chip_part.astype(BF16)
                else:
                    rec_a[a][j] = chip_part
                copy_b(a, r).start()
        for a in range(n):
            copy_a(a, me).wait_recv()
            rec_b[a][0] = (rec_a[a][me] + partial(a, me, c)[...]).astype(rec_b[a].dtype)
        for a in range(n):
            for r in range(1, N_CHIPS):
                copy_b(a, r).wait_recv()
            total = rec_b[a][me].astype(F32)
            for j in range(1, N_CHIPS):
                total = total + rec_b[a][j ^ me].astype(F32)
            out_refs[a][rows(a, c), :] = total
            copy_c(a).start()
        for a in range(n):
            copy_c(a).wait_recv()
        for a in range(n):
            for j in range(N_CHIPS):
                copy_a(a, j).wait_send()
            for r in range(1, N_CHIPS):
                copy_b(a, r).wait_send()
            copy_c(a).wait_send()

    dma = pltpu.SemaphoreType.DMA
    return _pallas(
        body, name="reduce_grads",
        out_shape=[jax.ShapeDtypeStruct(sh, F32) for sh in shapes],
        in_specs=[pl.BlockSpec(memory_space=pltpu.VMEM)] * n,
        out_specs=[pl.BlockSpec(memory_space=pltpu.VMEM)] * n,
        scratch_shapes=[pltpu.VMEM((N_CHIPS, h) + sh[1:], F32) for h, sh in zip(halves, shapes)]
                       + [pltpu.VMEM((N_CHIPS, h) + sh[1:], BF16 if a < n_big else F32)
                          for a, (h, sh) in enumerate(zip(halves, shapes))]
                       + [pltpu.VMEM((N_CHIPS, h) + sh[1:], BF16) for h, sh in zip(halves[:n_big], shapes[:n_big])]
                       + [dma((N_CHIPS * n,)), dma((N_CHIPS * n,)), dma(((N_CHIPS - 1) * n,)), dma(((N_CHIPS - 1) * n,)),
                          dma((n,)), dma((n,))],
        compiler_params=_params(vmem=VMEM_LIMIT),
    )(*parts, small)


def _pre(x, tabs, w_in_ext, w_uq_pad, w_ukv_ext, gains, tm):
    s_len = x.shape[0]
    nt = s_len // tm

    def body(x_ref, ca_ref, sa_ref, cb_ref, sb_ref, win_ref, wuq_ref, wukv_ref,
             gin_ref, gaq_ref, gak_ref, gcq_ref, gckv_ref, gbq_ref, gbk_ref,
             xn_ref, gates_ref, pre_ref, qbpre_ref, kbpre_ref, cq_ref, ckv_ref,
             qa_ref, ka_ref, va_ref, qb_ref, kb_ref, vb_ref, proj):
        xn = _rms_fwd(x_ref[...], gin_ref[...], D_MODEL)
        xn_ref[...] = jnp.transpose(xn).astype(BF16)
        proj[...] = _nn(xn.astype(BF16), win_ref[...])
        gates_ref[...] = proj[:, GA0:GA0 + N_GATE]
        pre_ref[:, 0:VA0] = proj[:, 0:VA0]
        pre_ref[:, VA0:N_PRE] = proj[:, CQ0:KR0]
        ca, sa, cb, sb = ca_ref[...], sa_ref[...], cb_ref[...], sb_ref[...]
        lane = lax.broadcasted_iota(jnp.int32, (tm, LANE), 1)
        for h in range(A_HEADS):
            yq = _rms_fwd(proj[:, QA0 + LANE * h:QA0 + LANE * (h + 1)], gaq_ref[...], A_DIM)
            qa_ref[h] = (_rope_fwd(yq, ca, sa) * (SCALE_A * LOG2E)).astype(BF16)
        for h in range(A_KV):
            yk = _rms_fwd(proj[:, KA0 + LANE * h:KA0 + LANE * (h + 1)], gak_ref[...], A_DIM)
            ka_ref[h] = _rope_fwd(yk, ca, sa).astype(BF16)
            va_ref[h] = jnp.where(lane == A_DIM, 1.0, proj[:, VA0 + LANE * h:VA0 + LANE * (h + 1)]).astype(BF16)
        cq = _rms_fwd(proj[:, CQ0:CQ0 + B_Q_RANK], gcq_ref[...], B_Q_RANK)
        cq_ref[...] = jnp.transpose(cq).astype(BF16)
        qbpre_ref[...] = _nn(cq.astype(BF16), wuq_ref[...])
        ckv = _rms_fwd(proj[:, CKV0:CKV0 + B_KV_RANK], gckv_ref[...], B_KV_RANK)
        ckv_ref[...] = jnp.transpose(ckv).astype(BF16)
        kvb = _nn(ckv.astype(BF16), wukv_ref[...])
        kr = proj[:, KR0:KR0 + LANE]
        for h in range(B_HEADS):
            yq = _rms_fwd(qbpre_ref[:, LANE * h:LANE * (h + 1)], gbq_ref[...], B_QK)
            qb_ref[h] = (_rope_fwd(yq, cb, sb) * (SCALE_B * LOG2E)).astype(BF16)
            kp = kvb[:, LANE * h:LANE * (h + 1)] + kr
            kbpre_ref[:, LANE * h:LANE * (h + 1)] = kp
            kb_ref[h] = _rope_fwd(_rms_fwd(kp, gbk_ref[...], B_QK), cb, sb).astype(BF16)
            vb_ref[h, :, 0:LANE] = kvb[:, B_HEADS * LANE + LANE * h:B_HEADS * LANE + LANE * (h + 1)].astype(BF16)
            vb_ref[h, :, LANE:2 * LANE] = jnp.where(lane == 0, 1.0, 0.0).astype(BF16)

    row = lambda w: pl.BlockSpec((tm, w), lambda i: (i, 0))
    col = lambda w: pl.BlockSpec((w, tm), lambda i: (0, i))
    heads = lambda n: pl.BlockSpec((n, tm, LANE), lambda i: (0, i, 0))
    hs = lambda n: jax.ShapeDtypeStruct((n, s_len, LANE), BF16)
    return _pallas(
        body, name="pre", grid=(nt,),
        in_specs=[row(D_MODEL), row(LANE), row(LANE), row(LANE), row(LANE),
                  _full(w_in_ext.shape), _full(w_uq_pad.shape), _full(w_ukv_ext.shape)]
                 + [_full(g.shape) for g in gains],
        out_specs=[col(D_MODEL), row(N_GATE), row(N_PRE), row(B_HEADS * LANE), row(B_HEADS * LANE),
                   col(B_Q_RANK), col(B_KV_RANK),
                   heads(A_HEADS), heads(A_KV), heads(A_KV), heads(B_HEADS), heads(B_HEADS),
                   pl.BlockSpec((B_HEADS, tm, 2 * LANE), lambda i: (0, i, 0))],
        out_shape=[jax.ShapeDtypeStruct((D_MODEL, s_len), BF16), jax.ShapeDtypeStruct((s_len, N_GATE), F32),
                   jax.ShapeDtypeStruct((s_len, N_PRE), F32), jax.ShapeDtypeStruct((s_len, B_HEADS * LANE), F32),
                   jax.ShapeDtypeStruct((s_len, B_HEADS * LANE), F32),
                   jax.ShapeDtypeStruct((B_Q_RANK, s_len), BF16), jax.ShapeDtypeStruct((B_KV_RANK, s_len), BF16),
                   hs(A_HEADS), hs(A_KV), hs(A_KV), hs(B_HEADS), hs(B_HEADS),
                   jax.ShapeDtypeStruct((B_HEADS, s_len, 2 * LANE), BF16)],
        scratch_shapes=[pltpu.VMEM((tm, N_EXT), F32)],
        compiler_params=_params(("parallel",), VMEM_LIMIT),
    )(x, *tabs, w_in_ext, w_uq_pad, w_ukv_ext, *gains)


def _attn_fwd(q, k, v, group, l_col, tq, tk, tiles, name):
    n_heads, s_len, _ = q.shape
    v_w = v.shape[2]
    nk = s_len // tk

    def body(q_ref, k_ref, v_ref, o_ref, lse_ref, s_buf, p_buf, a_buf, m_ref, acc_ref):
        def scores(g, slot):
            s_buf[slot] = _nt(q_ref[_block_rows(g // nk, tq), :], k_ref[_block_rows(g % nk, tk), :])

        def softmax(g, slot):
            t = g // nk
            s = s_buf[slot]
            m_old = m_ref[t]
            m_new = jnp.maximum(m_old, jnp.max(s, axis=-1, keepdims=True))
            m_ref[t] = m_new
            a_buf[slot] = jnp.exp2(m_old - m_new)
            p_buf[slot] = jnp.exp2(s - jnp.tile(m_new, (1, tk // LANE))).astype(BF16)

        def values(g, slot):
            t = g // nk
            pv = _nn(p_buf[slot], v_ref[_block_rows(g % nk, tk), :])
            for c in range(0, v_w, LANE):
                acc_ref[t, :, c:c + LANE] = a_buf[slot] * acc_ref[t, :, c:c + LANE] + pv[:, c:c + LANE]

        m_ref[...] = jnp.full(m_ref.shape, -1e30, F32)
        acc_ref[...] = jnp.zeros(acc_ref.shape, F32)
        _three_stage(tiles * nk, scores, softmax, values)
        for t in range(tiles):
            l = acc_ref[t, :, l_col:l_col + 1]
            o = acc_ref[t, :, 0:LANE] * (1.0 / l)
            if l_col < LANE:
                lane = lax.broadcasted_iota(jnp.int32, o.shape, 1)
                o = jnp.where(lane == l_col, 0.0, o)
            o_ref[t * tq:(t + 1) * tq, :] = o
            lse_ref[t] = jnp.transpose(m_ref[t] + jnp.log2(jnp.broadcast_to(l, (tq, LANE))))[0:1, :]

    return _pallas(
        body, name=name, grid=(n_heads, s_len // (tiles * tq)),
        in_specs=[pl.BlockSpec((None, tiles * tq, LANE), lambda h, i: (h, i, 0)),
                  pl.BlockSpec((None, s_len, LANE), lambda h, i: (h // group, 0, 0)),
                  pl.BlockSpec((None, s_len, v_w), lambda h, i: (h // group, 0, 0))],
        out_specs=[pl.BlockSpec((None, tiles * tq, LANE), lambda h, i: (h, i, 0)),
                   pl.BlockSpec((None, tiles, 1, tq), lambda h, i: (h, i, 0, 0))],
        out_shape=[jax.ShapeDtypeStruct((n_heads, s_len, LANE), F32),
                   jax.ShapeDtypeStruct((n_heads, s_len // tq, 1, tq), F32)],
        scratch_shapes=[pltpu.VMEM((2, tq, tk), F32), pltpu.VMEM((2, tq, tk), BF16), pltpu.VMEM((2, tq, LANE), F32),
                        pltpu.VMEM((tiles, tq, LANE), F32), pltpu.VMEM((tiles, tq, v_w), F32)],
        compiler_params=_params(("parallel", "parallel"), 48 * 1024 * 1024),
    )(q, k, v)


def _mid(x, target, o_a, o_b, gates, w_out_ext, tm):
    s_len = x.shape[0]
    nt = s_len // tm
    n_heads = A_HEADS + B_HEADS

    def body(x_ref, t_ref, oa_ref, ob_ref, g_ref, w_ref,
             yt_ref, dh_ref, dgate_ref, doa_ref, dob_ref, delta_ref, loss_ref, silu_scr, dsilu_scr, y_ref):
        @pl.when(pl.program_id(0) == 0)
        def _():
            loss_ref[...] = jnp.zeros_like(loss_ref)

        def o_of(h):
            return oa_ref[h] if h < A_HEADS else ob_ref[h - A_HEADS]

        for h in range(n_heads):
            cols = slice(LANE * h, LANE * (h + 1))
            g = g_ref[:, cols]
            sig = 1.0 / (1.0 + jnp.exp(-g))
            silu = g * sig
            silu_scr[:, cols] = silu
            dsilu_scr[:, cols] = sig * (1.0 + g * (1.0 - sig))
            y = o_of(h) * silu
            y_ref[:, cols] = y.astype(BF16)
            yt_ref[cols, :] = jnp.transpose(y).astype(BF16)
        err =x_ref[...] + _nn(y_ref[...], w_ref[...]) - t_ref[...]
        sq = jnp.sum(jnp.sum(err * err, axis=-1, keepdims=True), axis=0, keepdims=True)
        loss_ref[...] += jnp.broadcast_to(sq * (0.5 / D_MODEL), loss_ref.shape)
        dh = err * (1.0 / D_MODEL)
        dh_ref[...] = dh
        dy = _nt(dh.astype(BF16), w_ref[...])
        lane = lax.broadcasted_iota(jnp.int32, (tm, LANE), 1)
        delta = jnp.zeros((tm, LANE), F32)
        for h in range(n_heads):
            cols = slice(LANE * h, LANE * (h + 1))
            dyh = dy[:, cols]
            oh = o_of(h)
            do = dyh * silu_scr[:, cols]
            dgate_ref[:, cols] = (dyh * oh * dsilu_scr[:, cols]).astype(BF16)
            delta = jnp.where(lane == h, jnp.sum(do * oh, axis=-1, keepdims=True), delta)
            if h < A_HEADS:
                doa_ref[h] = do.astype(BF16)
            else:
                dob_ref[h - A_HEADS] = do.astype(BF16)
        delta_ref[...] = jnp.transpose(delta)[0:DELTA_ROWS, :]

    row = lambda w: pl.BlockSpec((tm, w), lambda i: (i, 0))
    heads = lambda n, w=LANE: pl.BlockSpec((n, tm, w), lambda i: (0, i, 0))
    return _pallas(
        body, name="mid", grid=(nt,),
        in_specs=[row(D_MODEL), row(D_MODEL), heads(A_HEADS), heads(B_HEADS), row(N_GATE), _full(w_out_ext.shape)],
        out_specs=[pl.BlockSpec((N_GATE, tm), lambda i: (0, i)), row(D_MODEL), row(N_GATE), heads(A_HEADS), heads(B_HEADS),
                   pl.BlockSpec((DELTA_ROWS, tm), lambda i: (0, i)),
                   _full((8, LANE))],
        out_shape=[jax.ShapeDtypeStruct((N_GATE, s_len), BF16), jax.ShapeDtypeStruct((s_len, D_MODEL), F32),
                   jax.ShapeDtypeStruct((s_len, N_GATE), BF16),
                   jax.ShapeDtypeStruct((A_HEADS, s_len, LANE), BF16), jax.ShapeDtypeStruct((B_HEADS, s_len, LANE), BF16),
                   jax.ShapeDtypeStruct((DELTA_ROWS, s_len), F32), jax.ShapeDtypeStruct((8, LANE), F32)],
        scratch_shapes=[pltpu.VMEM((tm, N_GATE), F32), pltpu.VMEM((tm, N_GATE), F32), pltpu.VMEM((tm, N_GATE), BF16)],
        compiler_params=_params(("arbitrary",), VMEM_LIMIT),
    )(x, target, o_a, o_b, gates, w_out_ext)


def _attn_bwd(q, k, v, do, lse, delta, group, tq, tk, tiles, name):
    n_heads, s_len, _ = q.shape
    nq = s_len // tq

    def body(q_ref, do_ref, lse_ref, delta_ref, k_ref, v_ref, dq_ref, dk_ref, dv_ref, s_buf, dp_buf, p_buf, ds_buf):
        @pl.when(pl.program_id(1) == 0)
        def _():
            dq_ref[...] = jnp.zeros_like(dq_ref)

        dk_ref[...] = jnp.zeros_like(dk_ref)
        dv_ref[...] = jnp.zeros_like(dv_ref)

        def keys(g):
            return _block_rows(g // nq, tk)

        def queries(g):
            return _block_rows(g % nq, tq)

        def scores(g, slot):
            s_buf[slot] = _nt(k_ref[keys(g), :], q_ref[queries(g), :])
            dp_buf[slot] = _nt(v_ref[keys(g), :], do_ref[queries(g), :])

        def elementwise(g, slot):
            p = jnp.exp2(s_buf[slot] - lse_ref[g % nq])
            p_buf[slot] = p.astype(BF16)
            ds_buf[slot] = (p * (dp_buf[slot] - delta_ref[g % nq])).astype(BF16)

        def grads(g, slot):
            dv_ref[keys(g), :] += _nn(p_buf[slot], do_ref[queries(g), :])
            dk_ref[keys(g), :] += _nn(ds_buf[slot], q_ref[queries(g), :])
            dq_ref[queries(g), :] += _tn(ds_buf[slot], k_ref[keys(g), :])

        _three_stage(tiles * nq, scores, elementwise, grads)

    whole = lambda: pl.BlockSpec((None, s_len, LANE), lambda h, j: (h, 0, 0))
    stat = lambda: pl.BlockSpec((None, nq, 1, tq), lambda h, j: (h, 0, 0, 0))
    kvb = lambda: pl.BlockSpec((None, tiles * tk, LANE), lambda h, j: (h // group, j, 0))
    outb = lambda: pl.BlockSpec((None, tiles * tk, LANE), lambda h, j: (h, j, 0))
    shape = jax.ShapeDtypeStruct((n_heads, s_len, LANE), F32)
    return _pallas(
        body, name=name, grid=(n_heads, s_len // (tiles * tk)),
        in_specs=[whole(), whole(), stat(), stat(), kvb(), kvb()],
        out_specs=[whole(), outb(), outb()],
        out_shape=[shape, shape, shape],
        scratch_shapes=[pltpu.VMEM((2, tk, tq), F32), pltpu.VMEM((2, tk, tq), F32),
                        pltpu.VMEM((2, tk, tq), BF16), pltpu.VMEM((2, tk, tq), BF16)],
        compiler_params=_params(("parallel", "arbitrary"), 48 * 1024 * 1024),
    )(q, do, lse, delta, k, v)


def _post(x, dh, pre, qbpre, kbpre, dgate, dqa, dka, dva, dqb, dkb, dvb, loss_part, tabs,
          w_in_ext, w_uq_pad, w_ukv_ext, gains, tm):
    s_len = x.shape[0]
    nt = s_len // tm

    def body(x_ref, dh_ref, pre_ref, qbpre_ref, kbpre_ref, dgate_ref,
             dqa_ref, dka_ref, dva_ref, dqb_ref, dkb_ref, dvb_ref, loss_ref,
             ca_ref, sa_ref, cb_ref, sb_ref, win_ref, wuq_ref, wukv_ref,
             gin_ref, gaq_ref, gak_ref, gcq_ref, gckv_ref, gbq_ref, gbk_ref,
             gx_ref, dproj_ref, dqbpre_ref, dkvb_ref, dsm_ref):
        @pl.when(pl.program_id(0) == 0)
        def _():
            dsm_ref[...] = jnp.zeros_like(dsm_ref)
            dsm_ref[SM_LOSS:SM_LOSS + 1, 0:LANE] = loss_ref[0:1, :]

        def add_small(r, dg):
            dsm_ref[r:r + 1, 0:dg.shape[1]] += dg

        def tok_sum(a):
            return jnp.sum(a, axis=0, keepdims=True)

        ca, sa, cb, sb = ca_ref[...], sa_ref[...], cb_ref[...], sb_ref[...]
        lane = lax.broadcasted_iota(jnp.int32, (tm, LANE), 1)

        nope_lanes = _lanes_of(lane, LAY_NOPE)

        def back(c0, c1):
            return _nt(dproj_ref[:, c0:c1], win_ref[:, c0:c1])

        dproj_ref[:, GA0:GA0 + N_GATE] = dgate_ref[...]
        dxn = back(GA0, GA0 + N_GATE)
        dg = jnp.zeros((1, LANE), F32)
        for h in range(A_HEADS):
            dn = _rope_bwd(dqa_ref[h] * SCALE_A, ca, sa)
            dx, dgr = _rms_bwd(dn, pre_ref[:, QA0 + LANE * h:QA0 + LANE * (h + 1)], gaq_ref[...], A_DIM)
            dproj_ref[:, QA0 + LANE * h:QA0 + LANE * (h + 1)] = dx.astype(BF16)
            dg = dg + tok_sum(dgr)
        add_small(SM_AQ, _unspread_row(dg, LAY_ROPE_A))
        dxn = dxn + back(QA0, KA0)
        dg = jnp.zeros((1, LANE), F32)
        for h in range(A_KV):
            dk = dka_ref[A_GROUP * h]
            dv = dva_ref[A_GROUP * h]
            for g in range(1, A_GROUP):
                dk = dk + dka_ref[A_GROUP * h + g]
                dv = dv + dva_ref[A_GROUP * h + g]
            dn = _rope_bwd(dk * LN2, ca, sa)
            dx, dgr = _rms_bwd(dn, pre_ref[:, KA0 + LANE * h:KA0 + LANE * (h + 1)], gak_ref[...], A_DIM)
            dproj_ref[:, KA0 + LANE * h:KA0 + LANE * (h + 1)] = dx.astype(BF16)
            dproj_ref[:, VA0 + LANE * h:VA0 + LANE * (h + 1)] = dv.astype(BF16)
            dg = dg + tok_sum(dgr)
        add_small(SM_AK, _unspread_row(dg, LAY_ROPE_A))
        dxn = dxn + back(KA0, GA0)
        dg = jnp.zeros((1, LANE), F32)
        for h in range(B_HEADS):
            cols = slice(LANE * h, LANE * (h + 1))
            dn = _rope_bwd(dqb_ref[h] * SCALE_B, cb, sb)
            dx, dgr = _rms_bwd(dn, qbpre_ref[:, cols], gbq_ref[...], B_QK)
            dqbpre_ref[:, cols] = dx.astype(BF16)
            dg = dg + tok_sum(dgr)
        add_small(SM_BQ, _unspread_row(dg, LAY_ROPE_B))
        dcq = _nt(dqbpre_ref[...], wuq_ref[...])
        dx, dgr = _rms_bwd(dcq, pre_ref[:, VA0:VA0 + B_Q_RANK], gcq_ref[...], B_Q_RANK)
        dproj_ref[:, CQ0:CQ0 + B_Q_RANK] = dx.astype(BF16)
        add_small(SM_CQ, tok_sum(dgr))
        dxn = dxn + back(CQ0, CKV0)
        dg = jnp.zeros((1, LANE), F32)
        dkr = jnp.zeros((tm, LANE), F32)
        for h in range(B_HEADS):
            cols = slice(LANE * h, LANE * (h + 1))
            dn = _rope_bwd(dkb_ref[h] * LN2, cb, sb)
            dx, dgr = _rms_bwd(dn, kbpre_ref[:, cols], gbk_ref[...], B_QK)
            dkvb_ref[:, cols] = jnp.where(nope_lanes, dx, 0.0).astype(BF16)
            dkvb_ref[:, B_HEADS * LANE + LANE * h:B_HEADS * LANE + LANE * (h + 1)] = dvb_ref[h].astype(BF16)
            dkr = dkr + dx
            dg = dg + tok_sum(dgr)
        add_small(SM_BK, _unspread_row(dg, LAY_ROPE_B))
        dproj_ref[:, KR0:KR0 + LANE] = jnp.where(_lanes_of(lane, LAY_KR), dkr, 0.0).astype(BF16)
        dckv = _nt(dkvb_ref[...], wukv_ref[...])
        dx, dgr = _rms_bwd(dckv, pre_ref[:, VA0 + B_Q_RANK:N_PRE], gckv_ref[...], B_KV_RANK)
        dproj_ref[:, CKV0:CKV0 + B_KV_RANK] = dx.astype(BF16)
        add_small(SM_CKV, tok_sum(dgr))
        dxn = dxn + back(CKV0, N_EXT)
        dx, dgr = _rms_bwd(dxn, x_ref[...], gin_ref[...], D_MODEL)
        gx_ref[...] = dh_ref[...] + dx
        add_small(SM_IN, tok_sum(dgr))

    row = lambda w: pl.BlockSpec((tm, w), lambda i: (i, 0))
    heads = lambda n: pl.BlockSpec((n, tm, LANE), lambda i: (0, i, 0))
    return _pallas(
        body, name="post", grid=(nt,),
        in_specs=[row(D_MODEL), row(D_MODEL), row(N_PRE), row(B_HEADS * LANE), row(B_HEADS * LANE), row(N_GATE),
                  heads(A_HEADS), heads(A_HEADS), heads(A_HEADS), heads(B_HEADS), heads(B_HEADS), heads(B_HEADS),
                  _full(loss_part.shape), row(LANE), row(LANE), row(LANE), row(LANE),
                  _full(w_in_ext.shape), _full(w_uq_pad.shape), _full(w_ukv_ext.shape)]
                 + [_full(g.shape) for g in gains],
        out_specs=[row(D_MODEL), row(N_EXT), row(B_HEADS * LANE), row(2 * B_HEADS * LANE), _full((SM_ROWS, SM_W))],
        out_shape=[jax.ShapeDtypeStruct((s_len, D_MODEL), F32), jax.ShapeDtypeStruct((s_len, N_EXT), BF16),
                   jax.ShapeDtypeStruct((s_len, B_HEADS * LANE), BF16),
                   jax.ShapeDtypeStruct((s_len, 2 * B_HEADS * LANE), BF16),
                   jax.ShapeDtypeStruct((SM_ROWS, SM_W), F32)],
        compiler_params=_params(("arbitrary",), VMEM_LIMIT),
    )(x, dh, pre, qbpre, kbpre, dgate, dqa, dka, dva, dqb, dkb, dvb, loss_part, *tabs,
      w_in_ext, w_uq_pad, w_ukv_ext, *gains)


def _grad_w(a_t, b, tn, ts, name):
    m, s_len = a_t.shape
    n = b.shape[1]

    def body(a_ref, b_ref, o_ref):
        @pl.when(pl.program_id(1) == 0)
        def _():
            o_ref[...] = jnp.zeros_like(o_ref)

        o_ref[...] += _nn(a_ref[...], b_ref[...].astype(BF16))

    return _pallas(
        body, name=name, grid=(n // tn, s_len // ts),
        in_specs=[pl.BlockSpec((m, ts), lambda j, t: (0, t)), pl.BlockSpec((ts, tn), lambda j, t: (t, j))],
        out_specs=pl.BlockSpec((m, tn), lambda j, t: (0, j)),
        out_shape=jax.ShapeDtypeStruct((m, n), F32),
        compiler_params=_params(("parallel", "arbitrary"), 48 * 1024 * 1024),
    )(a_t, b)


def _adam_math(w, g, m, v):
    nm = ADAM_B1 * m + (1.0 - ADAM_B1) * g
    nv = ADAM_B2 * v + (1.0 - ADAM_B2) * (g * g)
    m_hat = nm / (1.0 - ADAM_B1 ** ADAM_STEP)
    v_hat = nv / (1.0 - ADAM_B2 ** ADAM_STEP)
    return -ADAM_LR * (m_hat / (jnp.sqrt(v_hat) + ADAM_EPS) + ADAM_WD * w), nm, nv


def _adamw_rows(w, g, m, v, tr):
    rows, cols = w.shape

    def body(w_ref, g_ref, m_ref, v_ref, d_ref, nm_ref, nv_ref):
        d_ref[...], nm_ref[...], nv_ref[...] = _adam_math(w_ref[...], g_ref[...], m_ref[...], v_ref[...])

    blk = pl.BlockSpec((tr, cols), lambda i: (i, 0))
    shape = jax.ShapeDtypeStruct((rows, cols), F32)
    return _pallas(
        body, name="adamw_w_in", grid=(rows // tr,),
        in_specs=[blk] * 4, out_specs=[blk] * 3, out_shape=[shape] * 3,
        compiler_params=_params(("parallel",), 32 * 1024 * 1024),
    )(w, g, m, v)


def _adamw_rest(bigs, smalls, g_small):
    nb, ns = len(bigs), len(smalls)

    def body(*refs):
        ins, outs = refs[:4 * nb + 3 * ns + 1], refs[4 * nb + 3 * ns + 1:]
        for i in range(nb):
            w_ref, g_ref, m_ref, v_ref = ins[4 * i:4 * i + 4]
            d_ref, nm_ref, nv_ref = outs[3 * i:3 * i + 3]
            d_ref[...], nm_ref[...], nv_ref[...] = _adam_math(w_ref[...], g_ref[...], m_ref[...], v_ref[...])
        gs_ref = ins[-1]
        for i in range(ns):
            w_ref, m_ref, v_ref = ins[4 * nb + 3 * i:4 * nb + 3 * i + 3]
            g_ref, d_ref, nm_ref, nv_ref = outs[3 * nb + 4 * i:3 * nb + 4 * i + 4]
            g = gs_ref[i:i + 1, 0:w_ref.shape[1]]
            g_ref[...] = g
            d_ref[...], nm_ref[...], nv_ref[...] = _adam_math(w_ref[...], g, m_ref[...], v_ref[...])

    flat_in = [a for quad in bigs for a in quad] + [a for tri in smalls for a in tri] + [g_small]
    out_shape = ([jax.ShapeDtypeStruct(q[0].shape, F32) for q in bigs for _ in range(3)]
                 + [jax.ShapeDtypeStruct(t[0].shape, F32) for t in smalls for _ in range(4)])
    return _pallas(
        body, name="adamw_rest",
        in_specs=[pl.BlockSpec(memory_space=pltpu.VMEM)] * len(flat_in),
        out_specs=[pl.BlockSpec(memory_space=pltpu.VMEM)] * len(out_shape),
        out_shape=out_shape,
        compiler_params=_params(vmem=32 * 1024 * 1024),
    )(*flat_in)


def _place(pieces, n):
    out, at = [], 0
    for lane0, arr in sorted(pieces, key=lambda p: p[0]):
        out += [jnp.zeros((n, lane0 - at), F32), arr]
        at = lane0 + arr.shape[1]
    return jnp.concatenate(out + [jnp.zeros((n, LANE - at), F32)], axis=1)


def _rope_tables(s_len):
    rows = s_len // GRID_W
    row = jnp.arange(rows, dtype=F32)
    col = jnp.arange(GRID_W, dtype=F32)

    def lay(dim, layout, first_dim, ones):
        half = dim // 2
        inv = 1.0 / (ROPE_THETA ** (jnp.arange(0, half, 2, dtype=F32) / half))
        ang_r, ang_c = row[:, None] * inv[None, :], col[:, None] * inv[None, :]
        at = {a - first_dim: lane0 for a, _, lane0 in layout}
        q = dim // 4
        r1, r2, c1, c2 = at[0], at[q], at[2 * q], at[3 * q]
        cos_r = _place([(r1, jnp.cos(ang_r)), (r2, jnp.cos(ang_r))], rows)
        sin_r = _place([(r1, -jnp.sin(ang_r)), (r2, jnp.sin(ang_r))], rows)
        cos_c = _place([(c1, jnp.cos(ang_c)), (c2, jnp.cos(ang_c))] + [(l0, jnp.ones((GRID_W, n), F32)) for _, n, l0 in ones],
                       GRID_W)
        sin_c = _place([(c1, -jnp.sin(ang_c)), (c2, jnp.sin(ang_c))], GRID_W)
        cos = (cos_r[:, None, :] + cos_c[None, :, :]).reshape(s_len, LANE)
        sin = (sin_r[:, None, :] + sin_c[None, :, :]).reshape(s_len, LANE)
        return cos, sin

    cos_a, sin_a = lay(A_DIM, LAY_ROPE_A, 0, ())
    cos_b, sin_b = lay(B_ROPE, LAY_KR, 0, LAY_NOPE)
    return cos_a, sin_a, cos_b, sin_b


def _spread(w, n_heads, dim, axis, layout):
    w3 = w.reshape(w.shape[:axis] + (n_heads, dim) + w.shape[axis + 1:])
    out, at = [], 0

    def zeros(n):
        return jnp.zeros(w3.shape[:axis + 1] + (n,) + w3.shape[axis + 2:], w.dtype)

    for a0, n, lane0 in sorted(layout, key=lambda seg: seg[2]):
        out += [zeros(lane0 - at), lax.slice_in_dim(w3, a0, a0 + n, axis=axis + 1)]
        at = lane0 + n
    out = jnp.concatenate(out + [zeros(LANE - at)], axis=axis + 1)
    return out.reshape(w.shape[:axis] + (n_heads * LANE,) + w.shape[axis + 1:])


def _unspread(w, n_heads, dim, axis, layout):
    w3 = w.reshape(w.shape[:axis] + (n_heads, LANE) + w.shape[axis + 1:])
    parts = [lax.slice_in_dim(w3, lane0, lane0 + n, axis=axis + 1) for _, n, lane0 in sorted(layout)]
    out = parts[0] if len(parts) == 1 else jnp.concatenate(parts, axis=axis + 1)
    return out.reshape(w.shape[:axis] + (n_heads * dim,) + w.shape[axis + 1:])


def _head_cols(first, n_heads, dim, layout):
    out = np.full((n_heads * LANE,), -1, np.int32)
    for h in range(n_heads):
        for a0, n, lane0 in layout:
            out[h * LANE + lane0:h * LANE + lane0 + n] = first + h * dim + a0 + np.arange(n)
    return out


def _inverse(src, n):
    dst = np.full((n,), -1, np.int32)
    dst[src[src >= 0]] = np.nonzero(src >= 0)[0]
    return dst


def _column_maps():
    a_w, kv_w = A_HEADS * A_DIM, A_KV * A_DIM
    o_g = a_w + 2 * kv_w
    o_cq = o_g + a_w
    o_kr = o_cq + B_Q_RANK + B_KV_RANK
    src_in = np.concatenate([
        _head_cols(0, A_HEADS, A_DIM, LAY_ROPE_A), _head_cols(a_w, A_KV, A_DIM, LAY_ROPE_A),
        _head_cols(a_w + kv_w, A_KV, A_DIM, LAY_PLAIN_A), _head_cols(o_g, A_HEADS, A_DIM, LAY_PLAIN_A),
        np.arange(o_kr + B_ROPE, N_IN), np.arange(o_cq, o_kr), _head_cols(o_kr, 1, B_ROPE, LAY_KR)]).astype(np.int32)
    src_uq = _head_cols(0, B_HEADS, B_QK, LAY_ROPE_B)
    per = B_NOPE + B_V
    src_ukv = np.concatenate([_head_cols(0, B_HEADS, per, LAY_NOPE),
                              _head_cols(B_NOPE, B_HEADS, per, ((0, B_V, 0),))]).astype(np.int32)
    assert len(src_in) == N_EXT
    return src_in, src_uq, src_ukv


def _round_up(n, m):
    return (n + m - 1) // m * m


def _permute_cols(xs, maps, name):
    maps = [np.asarray(m, np.int32) for m in maps]
    n_arr = len(xs)

    def body(*refs):
        row = lax.broadcasted_iota(jnp.int32, (LANE, LANE), 0)
        for x_ref, src_ref, o_ref, src in zip(refs[:n_arr], refs[n_arr:2 * n_arr], refs[2 * n_arr:], maps):
            for c in range(len(src) // LANE):
                cols = slice(c * LANE, (c + 1) * LANE)
                acc = jnp.zeros((x_ref.shape[0], LANE), F32)
                for kb in sorted({int(v) // LANE for v in src[cols] if v >= 0}):
                    sel = jnp.where(row + kb * LANE == src_ref[:, cols], 1.0, 0.0).astype(BF16)
                    part = x_ref[:, kb * LANE:(kb + 1) * LANE]
                    if part.dtype == BF16:
                        acc = acc + _nn(part, sel)
                    else:
                        hi = part.astype(BF16)
                        rest = part - hi.astype(F32)
                        mid = rest.astype(BF16)
                        low = (rest - mid.astype(F32)).astype(BF16)
                        acc = acc + ((_nn(hi, sel) + _nn(mid, sel)) + _nn(low, sel))
                o_ref[:, cols] = acc.astype(o_ref.dtype)

    return _pallas(
        body, name=name,
        out_shape=[jax.ShapeDtypeStruct((x.shape[0], len(m)), x.dtype) for x, m in zip(xs, maps)],
        compiler_params=_params(vmem=48 * 1024 * 1024),
    )(*xs, *[jnp.asarray(m).reshape(1, -1) for m in maps])


def _pad_cols(w):
    return jnp.pad(w, ((0, 0), (0, _round_up(w.shape[1], LANE) - w.shape[1])))


def _ext_weights(g_in, g_uq, g_ukv, g_out):
    w_in = g_in.transpose(1, 0, 2).reshape(D_MODEL, N_IN)
    w_uq = g_uq.reshape(B_Q_RANK, B_HEADS * B_QK)
    w_ukv = g_ukv.transpose(1, 0, 2).reshape(B_KV_RANK, B_HEADS * (B_NOPE + B_V))
    w_out = g_out.reshape(D_MODEL, D_MODEL)
    w_in_ext, w_uq_pad, w_ukv_ext = _permute_cols([_pad_cols(w_in), w_uq, w_ukv], _column_maps(), "lay_out_weights")
    a_w = A_HEADS * A_DIM
    w_out_ext = jnp.concatenate([_spread(w_out[:a_w], A_HEADS, A_DIM, 0, LAY_PLAIN_A), w_out[a_w:]], axis=0)
    return w_in_ext, w_uq_pad, w_ukv_ext, w_out_ext


def _fold_grads(d_in_ext, d_uq_pad, d_ukv_ext, d_out_ext):
    src_in, src_uq, src_ukv = _column_maps()
    n_uq, n_ukv = B_HEADS * B_QK, B_HEADS * (B_NOPE + B_V)
    back = [np.concatenate([_inverse(src_in, N_IN), np.full((_round_up(N_IN, LANE) - N_IN,), -1, np.int32)]),
            _inverse(src_uq, n_uq), _inverse(src_ukv, n_ukv)]
    d_in, d_uq, d_ukv = _permute_cols([d_in_ext, d_uq_pad, d_ukv_ext], back, "fold_grads")
    d_in = d_in[:, :N_IN]
    a_w = A_HEADS * A_DIM
    d_out = jnp.concatenate([_unspread(d_out_ext[:A_HEADS * LANE], A_HEADS, A_DIM, 0, LAY_PLAIN_A), d_out_ext[A_HEADS * LANE:]], axis=0)
    return (d_in.reshape(D_MODEL, N_CHIPS, SH_IN[1]).transpose(1, 0, 2),
            d_uq.reshape((N_CHIPS,) + SH_UQ),
            d_ukv.reshape(B_KV_RANK, N_CHIPS, SH_UKV[1]).transpose(1, 0, 2),
            d_out.reshape((N_CHIPS,) + SH_OUT))


def kernel(x, norm_in, w_in, a_q_norm, a_k_norm, b_cq_norm, b_ckv_norm, w_uq, w_ukv, b_q_norm, b_k_norm, w_out, loss_target, m_norm_in, m_w_in, m_a_q_norm, m_a_k_norm, m_b_cq_norm, m_b_ckv_norm, m_w_uq, m_w_ukv, m_b_q_norm, m_b_k_norm, m_w_out, v_norm_in, v_w_in, v_a_q_norm, v_a_k_norm, v_b_cq_norm, v_b_ckv_norm, v_w_uq, v_w_ukv, v_b_q_norm, v_b_k_norm, v_w_out):
    s_len = x.shape[1]
    xs, ts = x[0], loss_target[0]
    tm = min(256, s_len)
    tq, tk_f = min(512, s_len // 2), min(1024, s_len // 2)
    tq_b, tk_b = min(1024, s_len // 2), min(512, s_len)
    tiles_f = min(4, s_len // tq)
    tiles_b = min(2, s_len // tk_b)

    w_in_ext, w_uq_pad, w_ukv_ext, w_out_ext = _ext_weights(*_gather_weights((w_in[0], w_uq[0], w_ukv[0], w_out[0])))
    gains = (norm_in, _spread(a_q_norm, 1, A_DIM, 1, LAY_ROPE_A), _spread(a_k_norm, 1, A_DIM, 1, LAY_ROPE_A), b_cq_norm, b_ckv_norm,
             _spread(b_q_norm, 1, B_QK, 1, LAY_ROPE_B), _spread(b_k_norm, 1, B_QK, 1, LAY_ROPE_B))
    tabs = _rope_tables(s_len)

    (xn_t, gates, pre, qbpre, kbpre, cq_t, ckv_t, qa, ka, va, qb, kb, vb) = _pre(
        xs, tabs, w_in_ext, w_uq_pad, w_ukv_ext, gains, tm)
    o_a, lse_a = _attn_fwd(qa, ka, va, A_GROUP, A_DIM, tq, tk_f, tiles_f, "attn_fwd_a")
    o_b, lse_b = _attn_fwd(qb, kb, vb, 1, B_V, tq, tk_f, tiles_f, "attn_fwd_b")
    y_t, dh, dgate, do_a, do_b, delta, loss_part = _mid(xs, ts, o_a, o_b, gates, w_out_ext, tm)

    def stat(a):
        return a.reshape(a.shape[0], s_len // tq_b, 1, tq_b)

    dqa, dka, dva = _attn_bwd(qa, ka, va, do_a, stat(lse_a), stat(delta[:A_HEADS]), A_GROUP, tq_b, tk_b, tiles_b, "attn_bwd_a")
    dqb, dkb, dvb = _attn_bwd(qb, kb, vb, do_b, stat(lse_b), stat(delta[A_HEADS:A_HEADS + B_HEADS]), 1, tq_b, tk_b,
                              tiles_b, "attn_bwd_b")
    grad_x, dproj, dqbpre, dkvb, d_small = _post(
        xs, dh, pre, qbpre, kbpre, dgate, dqa, dka, dva, dqb, dkb, dvb, loss_part, tabs,
        w_in_ext, w_uq_pad, w_ukv_ext, gains, tm)

    ts_w = min(2048, s_len)
    d_in_ext = _grad_w(xn_t, dproj, 768, ts_w, "grad_w_in")
    d_out_ext = _grad_w(y_t, dh, 512, ts_w, "grad_w_out")
    d_uq_pad = _grad_w(cq_t, dqbpre, 512, ts_w, "grad_w_uq")
    d_ukv_ext = _grad_w(ckv_t, dkvb, 1024, ts_w, "grad_w_ukv")

    g_in, g_uq, g_ukv, g_out, g_small = _reduce_grads(_fold_grads(d_in_ext, d_uq_pad, d_ukv_ext, d_out_ext), d_small)
    d_in, nm_in, nv_in = (a.T for a in _adamw_rows(w_in[0].T, g_in.T, m_w_in[0].T, v_w_in[0].T, SH_IN[1] // 7))
    rest = _adamw_rest(
        [(w_uq[0], g_uq, m_w_uq[0], v_w_uq[0]), (w_ukv[0], g_ukv, m_w_ukv[0], v_w_ukv[0]),
         (w_out[0], g_out, m_w_out[0], v_w_out[0])],
        [(norm_in, m_norm_in, v_norm_in), (a_q_norm, m_a_q_norm, v_a_q_norm), (a_k_norm, m_a_k_norm, v_a_k_norm),
         (b_cq_norm, m_b_cq_norm, v_b_cq_norm), (b_ckv_norm, m_b_ckv_norm, v_b_ckv_norm),
         (b_q_norm, m_b_q_norm, v_b_q_norm), (b_k_norm, m_b_k_norm, v_b_k_norm)], g_small)
    (d_uq, nm_uq, nv_uq), (d_ukv, nm_ukv, nv_ukv), (d_out, nm_out, nv_out) = (rest[3 * i:3 * i + 3] for i in range(3))
    sm = [rest[9 + 4 * i:9 + 4 * i + 4] for i in range(7)]

    def leaves(k, p_in, p_uq, p_ukv, p_out):
        return [sm[SM_IN][k], p_in[None], sm[SM_AQ][k], sm[SM_AK][k], sm[SM_CQ][k], sm[SM_CKV][k], p_uq[None], p_ukv[None],
                sm[SM_BQ][k], sm[SM_BK][k], p_out[None]]

    return (g_small[SM_LOSS, 0], grad_x[None], *leaves(0, g_in, g_uq, g_ukv, g_out), *leaves(1, d_in, d_uq, d_ukv, d_out),
            *leaves(2, nm_in, nm_uq, nm_ukv, nm_out), *leaves(3, nv_in, nv_uq, nv_ukv, nv_out))
```

```python
import jax
import jax.numpy as jnp
import numpy as np
from jax import lax
from jax.experimental import pallas as pl
from jax.experimental.pallas import tpu as pltpu

F32 = jnp.float32
BF16 = jnp.bfloat16
MESH = pl.DeviceIdType.MESH

D_MODEL = 1024
GRID_W = 64
ROPE_THETA = 10000.0
EPS = 1e-6
A_HEADS, A_KV, A_DIM = 8, 2, 64
A_GROUP = A_HEADS // A_KV
B_HEADS, B_NOPE, B_ROPE, B_V = 4, 64, 32, 128
B_QK = B_NOPE + B_ROPE
B_Q_RANK, B_KV_RANK = 384, 256
N_IN = 2464
SCALE_A = 1.0 / float(np.sqrt(A_DIM))
SCALE_B = 1.0 / float(np.sqrt(B_QK))
LOG2E = float(np.log2(np.e))
LN2 = float(np.log(2.0))
ADAM_LR, ADAM_B1, ADAM_B2, ADAM_EPS, ADAM_WD, ADAM_STEP = 0.001, 0.9, 0.999, 1e-08, 0.01, 10

LANE = 128
VMEM_BYTES = 64 * 1024 * 1024
VMEM_LIMIT = VMEM_BYTES - 8 * 1024 * 1024

QA0 = 0
KA0 = QA0 + A_HEADS * LANE
VA0 = KA0 + A_KV * LANE
GA0 = VA0 + A_KV * LANE
GB0 = GA0 + A_HEADS * LANE
CQ0 = GB0 + B_HEADS * LANE
CKV0 = CQ0 + B_Q_RANK
KR0 = CKV0 + B_KV_RANK
N_EXT = KR0 + LANE
N_GATE = (A_HEADS + B_HEADS) * LANE
DELTA_ROWS = 16
N_PRE = KA0 + A_KV * LANE + B_Q_RANK + B_KV_RANK

ROT = LANE // 2
_QA = A_DIM // 4
_QB = B_ROPE // 4
LAY_PLAIN_A = ((0, A_DIM, 0),)
LAY_ROPE_A = ((0, _QA, 0), (2 * _QA, _QA, _QA), (_QA, _QA, ROT), (3 * _QA, _QA, ROT + _QA))
LAY_KR = ((0, _QB, 0), (2 * _QB, _QB, _QB), (_QB, _QB, ROT), (3 * _QB, _QB, ROT + _QB))
LAY_NOPE = ((0, B_NOPE // 2, 2 * _QB), (B_NOPE // 2, B_NOPE // 2, ROT + 2 * _QB))
LAY_ROPE_B = LAY_NOPE + tuple((B_NOPE + a, n, at) for a, n, at in LAY_KR)

N_CHIPS = 4
SH_IN = (D_MODEL, N_IN // N_CHIPS)
SH_UQ = (B_Q_RANK // N_CHIPS, B_HEADS * B_QK)
SH_UKV = (B_KV_RANK, B_HEADS * (B_NOPE + B_V) // N_CHIPS)
SH_OUT = (D_MODEL // N_CHIPS, D_MODEL)
SM_ROWS, SM_W = 16, D_MODEL
SM_IN, SM_AQ, SM_AK, SM_CQ, SM_CKV, SM_BQ, SM_BK, SM_LOSS = range(8)
F32_ROWS, BF16_ROWS = 8, 16


def _pallas(body, **kw):
    return pl.pallas_call(body, **kw)


def _params(sem=None, vmem=None):
    return pltpu.CompilerParams(dimension_semantics=sem, vmem_limit_bytes=vmem)


def _rms_fwd(x, g, n):
    r = lax.rsqrt(jnp.sum(x * x, axis=-1, keepdims=True) * (1.0 / n) + EPS)
    return x * r * g


def _rms_bwd(dy, x, g, n):
    u = dy * g
    r = lax.rsqrt(jnp.sum(x * x, axis=-1, keepdims=True) * (1.0 / n) + EPS)
    ux = jnp.sum(u * x, axis=-1, keepdims=True)
    xhat = x * r
    dx = r * (u - xhat * (r * ux * (1.0 / n)))
    return dx, dy * xhat


def _rope_fwd(y, cos, sin):
    return y * cos + pltpu.roll(y, ROT, 1) * sin


def _rope_bwd(d, cos, sin):
    return d * cos - pltpu.roll(d, ROT, 1) * sin


def _lanes_of(lane, layout):
    m = None
    for _, n, at in layout:
        seg = (lane >= at) & (lane < at + n)
        m = seg if m is None else (m | seg)
    return m


def _unspread_row(v, layout):
    v8 = jnp.broadcast_to(v, (F32_ROWS, LANE))
    lane = lax.broadcasted_iota(jnp.int32, v8.shape, 1)
    out = jnp.zeros_like(v8)
    for a, n, at in layout:
        moved = v8 if a == at else pltpu.roll(v8, (a - at) % LANE, 1)
        out = jnp.where((lane >= a) & (lane < a + n), moved, out)
    return out[0:1, :]


def _nt(a, b):
    return lax.dot_general(a, b, (((1,), (1,)), ((), ())), preferred_element_type=F32)


def _tn(a, b):
    return lax.dot_general(a, b, (((0,), (0,)), ((), ())), preferred_element_type=F32)


def _nn(a, b):
    return jnp.dot(a, b, preferred_element_type=F32)


def _block_rows(i, size):
    if isinstance(i, int):
        return pl.ds(i * size, size)
    return pl.ds(pl.multiple_of(i * size, size), size)


MAX_STATIC_BLOCKS = 32


def _three_stage(n, first, second, third):
    assert n >= 2 and n % 2 == 0
    first(0, 0)
    first(1, 1)
    second(0, 0)
    if n <= MAX_STATIC_BLOCKS:
        for i in range(1, n - 1):
            first(i + 1, (i + 1) % 2)
            second(i, i % 2)
            third(i - 1, (i - 1) % 2)
    else:
        def pair(t, carry):
            i = 2 * t + 1
            first(i + 1, 0)
            second(i, 1)
            third(i - 1, 0)
            first(i + 2, 1)
            second(i + 1, 0)
            third(i, 1)
            return carry

        lax.fori_loop(0, (n - 2) // 2, pair, 0)
    second(n - 1, 1)
    third(n - 2, 0)
    third(n - 1, 1)


def _full(shape):
    return pl.BlockSpec(shape, lambda *_: (0,) * len(shape))


def _resident(shape):
    return pl.BlockSpec(shape, lambda *_: (0,) * len(shape), pipeline_mode=pl.Buffered(1))


def _gather_weights(shards):
    n = len(shards)
    halves = [w.shape[0] // 2 for w in shards]

    def body(*refs):
        w_refs, out_refs, (send_sems, recv_sems) = refs[:n], refs[n:2 * n], refs[2 * n:]
        x, y, c = lax.axis_index("x"), lax.axis_index("y"), lax.axis_index("c")
        sibling = (x, y, 1 - c)
        chips = [(1 - x, y), (x, 1 - y), (1 - x, 1 - y)]
        me = 2 * x + y

        def copy(a, k, j, hc, to):
            part = out_refs[a].at[j, pl.ds(pl.multiple_of(hc * halves[a], BF16_ROWS), halves[a]), :]
            return pltpu.make_async_remote_copy(
                src_ref=part, dst_ref=part, send_sem=send_sems.at[6 * a + k], recv_sem=recv_sems.at[6 * a + k],
                device_id=to, device_id_type=MESH)

        started = []
        for a in range(n):
            out_refs[a][me] = w_refs[a][...].astype(BF16)
            for k, chip in enumerate(chips):
                started.append(copy(a, k, me, c, (*chip, c)))
                started[-1].start()
        for k, chip in enumerate(chips):
            for a in range(n):
                copy(a, k, 2 * chip[0] + chip[1], c, (*chip, c)).wait_recv()
                started.append(copy(a, 3 + k, 2 * chip[0] + chip[1], c, sibling))
                started[-1].start()
        for k, chip in enumerate(chips):
            for a in range(n):
                copy(a, 3 + k, 2 * chip[0] + chip[1], 1 - c, sibling).wait_recv()
        for cp in started:
            cp.wait_send()

    return _pallas(
        body, name="gather_weights",
        out_shape=[jax.ShapeDtypeStruct((N_CHIPS,) + w.shape, BF16) for w in shards],
        in_specs=[pl.BlockSpec(memory_space=pltpu.VMEM)] * n,
        out_specs=[pl.BlockSpec(memory_space=pltpu.VMEM)] * n,
        scratch_shapes=[pltpu.SemaphoreType.DMA((6 * n,)), pltpu.SemaphoreType.DMA((6 * n,))],
        compiler_params=_params(vmem=32 * 1024 * 1024),
    )(*shards)


def _reduce_grads(parts, small):
    n_big = len(parts)
    n = n_big + 1
    shapes = [p.shape[1:] for p in parts] + [small.shape]
    halves = [sh[0] // 2 for sh in shapes]

    def body(*refs):
        p_refs, out_refs, rec_a, rec_b = refs[:n], refs[n:2 * n], refs[2 * n:3 * n], refs[3 * n:4 * n]
        send_b = refs[4 * n:4 * n + n_big]
        sa_send, sa_recv, sb_send, sb_recv, sc_send, sc_recv = refs[4 * n + n_big:]
        x, y, c = lax.axis_index("x"), lax.axis_index("y"), lax.axis_index("c")
        sibling = (x, y, 1 - c)
        me = 2 * x + y

        def rows(a, hc):
            return pl.ds(pl.multiple_of(hc * halves[a], F32_ROWS), halves[a])

        def partial(a, j, hc):
            return p_refs[a].at[j, rows(a, hc), :] if a < n_big else p_refs[a].at[rows(a, hc), :]

        def copy_a(a, j):
            return pltpu.make_async_remote_copy(
                src_ref=partial(a, j, 1 - c), dst_ref=rec_a[a].at[j],
                send_sem=sa_send.at[N_CHIPS * a + j], recv_sem=sa_recv.at[N_CHIPS * a + j],
                device_id=sibling, device_id_type=MESH)

        def copy_b(a, r):
            j = me ^ r
            k = (N_CHIPS - 1) * a + r - 1
            return pltpu.make_async_remote_copy(
                src_ref=(send_b[a] if a < n_big else rec_a[a]).at[j], dst_ref=rec_b[a].at[r],
                send_sem=sb_send.at[k], recv_sem=sb_recv.at[k], device_id=(j // 2, j % 2, c), device_id_type=MESH)

        def copy_c(a):
            return pltpu.make_async_remote_copy(
                src_ref=out_refs[a].at[rows(a, c), :], dst_ref=out_refs[a].at[rows(a, c), :],
                send_sem=sc_send.at[a], recv_sem=sc_recv.at[a], device_id=sibling, device_id_type=MESH)

        for a in range(n):
            for j in range(N_CHIPS):
                copy_a(a, j).start()
        for r in range(1, N_CHIPS):
            j = me ^ r
            for a in range(n):
                copy_a(a, j).wait_recv()
                chip_part = rec_a[a][j] + partial(a, j, c)[...]
                if a < n_big:
                    send_b[a][j] = chip_part.astype(BF16)
                else:
                    rec_a[a][j] = chip_part
                copy_b(a, r).start()
        for a in range(n):
            copy_a(a, me).wait_recv()
            rec_b[a][0] = (rec_a[a][me] + partial(a, me, c)[...]).astype(rec_b[a].dtype)
        for a in range(n):
            for r in range(1, N_CHIPS):
                copy_b(a, r).wait_recv()
            total = rec_b[a][me].astype(F32)
            for j in range(1, N_CHIPS):
                total = total + rec_b[a][j ^ me].astype(F32)
            out_refs[a][rows(a, c), :] = total
            copy_c(a).start()
        for a in range(n):
            copy_c(a).wait_recv()
        for a in range(n):
            for j in range(N_CHIPS):
                copy_a(a, j).wait_send()
            for r in range(1, N_CHIPS):
                copy_b(a, r).wait_send()
            copy_c(a).wait_send()

    dma = pltpu.SemaphoreType.DMA
    return _pallas(
        body, name="reduce_grads",
        out_shape=[jax.ShapeDtypeStruct(sh, F32) for sh in shapes],
        in_specs=[pl.BlockSpec(memory_space=pltpu.VMEM)] * n,
        out_specs=[pl.BlockSpec(memory_space=pltpu.VMEM)] * n,
        scratch_shapes=[pltpu.VMEM((N_CHIPS, h) + sh[1:], F32) for h, sh in zip(halves, shapes)]
                       + [pltpu.VMEM((N_CHIPS, h) + sh[1:], BF16 if a < n_big else F32)
                          for a, (h, sh) in enumerate(zip(halves, shapes))]
                       + [pltpu.VMEM((N_CHIPS, h) + sh[1:], BF16) for h, sh in zip(halves[:n_big], shapes[:n_big])]
                       + [dma((N_CHIPS * n,)), dma((N_CHIPS * n,)), dma(((N_CHIPS - 1) * n,)), dma(((N_CHIPS - 1) * n,)),
                          dma((n,)), dma((n,))],
        compiler_params=_params(vmem=VMEM_LIMIT),
    )(*parts, small)


def _pre(x, tabs, w_in_ext, w_uq_pad, w_ukv_ext, gains, tm):
    s_len = x.shape[0]
    nt = s_len // tm

    def body(x_ref, ca_ref, sa_ref, cb_ref, sb_ref, win_ref, wuq_ref, wukv_ref,
             gin_ref, gaq_ref, gak_ref, gcq_ref, gckv_ref, gbq_ref, gbk_ref,
             xn_ref, gates_ref, pre_ref, qbpre_ref, kbpre_ref, cq_ref, ckv_ref,
             qa_ref, ka_ref, va_ref, qb_ref, kb_ref, vb_ref, proj):
        xn = _rms_fwd(x_ref[...], gin_ref[...], D_MODEL)
        xn_ref[...] = jnp.transpose(xn).astype(BF16)
        proj[...] = _nn(xn.astype(BF16), win_ref[...])
        gates_ref[...] = proj[:, GA0:GA0 + N_GATE]
        pre_ref[:, 0:VA0] = proj[:, 0:VA0]
        pre_ref[:, VA0:N_PRE] = proj[:, CQ0:KR0]
        ca, sa, cb, sb = ca_ref[...], sa_ref[...], cb_ref[...], sb_ref[...]
        lane = lax.broadcasted_iota(jnp.int32, (tm, LANE), 1)
        for h in range(A_HEADS):
            yq = _rms_fwd(proj[:, QA0 + LANE * h:QA0 + LANE * (h + 1)], gaq_ref[...], A_DIM)
            qa_ref[h] = (_rope_fwd(yq, ca, sa) * (SCALE_A * LOG2E)).astype(BF16)
        for h in range(A_KV):
            yk = _rms_fwd(proj[:, KA0 + LANE * h:KA0 + LANE * (h + 1)], gak_ref[...], A_DIM)
            ka_ref[h] = _rope_fwd(yk, ca, sa).astype(BF16)
            va_ref[h] = jnp.where(lane == A_DIM, 1.0, proj[:, VA0 + LANE * h:VA0 + LANE * (h + 1)]).astype(BF16)
        cq = _rms_fwd(proj[:, CQ0:CQ0 + B_Q_RANK], gcq_ref[...], B_Q_RANK)
        cq_ref[...] = jnp.transpose(cq).astype(BF16)
        qbpre_ref[...] = _nn(cq.astype(BF16), wuq_ref[...])
        ckv = _rms_fwd(proj[:, CKV0:CKV0 + B_KV_RANK], gckv_ref[...], B_KV_RANK)
        ckv_ref[...] = jnp.transpose(ckv).astype(BF16)
        kvb = _nn(ckv.astype(BF16), wukv_ref[...])
        kr = proj[:, KR0:KR0 + LANE]
        for h in range(B_HEADS):
            yq = _rms_fwd(qbpre_ref[:, LANE * h:LANE * (h + 1)], gbq_ref[...], B_QK)
            qb_ref[h] = (_rope_fwd(yq, cb, sb) * (SCALE_B * LOG2E)).astype(BF16)
            kp = kvb[:, LANE * h:LANE * (h + 1)] + kr
            kbpre_ref[:, LANE * h:LANE * (h + 1)] = kp
            kb_ref[h] = _rope_fwd(_rms_fwd(kp, gbk_ref[...], B_QK), cb, sb).astype(BF16)
            vb_ref[h, :, 0:LANE] = kvb[:, B_HEADS * LANE + LANE * h:B_HEADS * LANE + LANE * (h + 1)].astype(BF16)
            vb_ref[h, :, LANE:2 * LANE] = jnp.where(lane == 0, 1.0, 0.0).astype(BF16)

    row = lambda w: pl.BlockSpec((tm, w), lambda i: (i, 0))
    col = lambda w: pl.BlockSpec((w, tm), lambda i: (0, i))
    heads = lambda n: pl.BlockSpec((n, tm, LANE), lambda i: (0, i, 0))
    hs = lambda n: jax.ShapeDtypeStruct((n, s_len, LANE), BF16)
    return _pallas(
        body, name="pre", grid=(nt,),
        in_specs=[row(D_MODEL), row(LANE), row(LANE), row(LANE), row(LANE),
                  _resident(w_in_ext.shape), _resident(w_uq_pad.shape), _resident(w_ukv_ext.shape)]
                 + [_full(g.shape) for g in gains],
        out_specs=[col(D_MODEL), row(N_GATE), row(N_PRE), row(B_HEADS * LANE), row(B_HEADS * LANE),
                   col(B_Q_RANK), col(B_KV_RANK),
                   heads(A_HEADS), heads(A_KV), heads(A_KV), heads(B_HEADS), heads(B_HEADS),
                   pl.BlockSpec((B_HEADS, tm, 2 * LANE), lambda i: (0, i, 0))],
        out_shape=[jax.ShapeDtypeStruct((D_MODEL, s_len), BF16), jax.ShapeDtypeStruct((s_len, N_GATE), F32),
                   jax.ShapeDtypeStruct((s_len, N_PRE), F32), jax.ShapeDtypeStruct((s_len, B_HEADS * LANE), F32),
                   jax.ShapeDtypeStruct((s_len, B_HEADS * LANE), F32),
                   jax.ShapeDtypeStruct((B_Q_RANK, s_len), BF16), jax.ShapeDtypeStruct((B_KV_RANK, s_len), BF16),
                   hs(A_HEADS), hs(A_KV), hs(A_KV), hs(B_HEADS), hs(B_HEADS),
                   jax.ShapeDtypeStruct((B_HEADS, s_len, 2 * LANE), BF16)],
        scratch_shapes=[pltpu.VMEM((tm, N_EXT), F32)],
        compiler_params=_params(("parallel",), VMEM_LIMIT),
    )(x, *tabs, w_in_ext, w_uq_pad, w_ukv_ext, *gains)


def _attn_fwd(q, k, v, group, l_col, tq, tk, tiles, name):
    n_heads, s_len, _ = q.shape
    v_w = v.shape[2]
    nk = s_len // tk

    def body(q_ref, k_ref, v_ref, o_ref, lse_ref, s_buf, p_buf, a_buf, m_ref, acc_ref):
        def scores(g, slot):
            s_buf[slot] = _nt(q_ref[_block_rows(g // nk, tq), :], k_ref[_block_rows(g % nk, tk), :])

        def softmax(g, slot):
            t = g // nk
            s = s_buf[slot]
            m_old = m_ref[t]
            m_new = jnp.maximum(m_old, jnp.max(s, axis=-1, keepdims=True))
            m_ref[t] = m_new
            a_buf[slot] = jnp.exp2(m_old - m_new)
            p_buf[slot] = jnp.exp2(s - jnp.tile(m_new, (1, tk // LANE))).astype(BF16)

        def values(g, slot):
            t = g // nk
            pv = _nn(p_buf[slot], v_ref[_block_rows(g % nk, tk), :])
            for c in range(0, v_w, LANE):
                acc_ref[t, :, c:c + LANE] = a_buf[slot] * acc_ref[t, :, c:c + LANE] + pv[:, c:c + LANE]

        m_ref[...] = jnp.full(m_ref.shape, -1e30, F32)
        acc_ref[...] = jnp.zeros(acc_ref.shape, F32)
        _three_stage(tiles * nk, scores, softmax, values)
        for t in range(tiles):
            l = acc_ref[t, :, l_col:l_col + 1]
            o = acc_ref[t, :, 0:LANE] * (1.0 / l)
            if l_col < LANE:
                lane = lax.broadcasted_iota(jnp.int32, o.shape, 1)
                o = jnp.where(lane == l_col, 0.0, o)
            o_ref[t * tq:(t + 1) * tq, :] = o
            lse_ref[t] = jnp.transpose(m_ref[t] + jnp.log2(jnp.broadcast_to(l, (tq, LANE))))[0:1, :]

    return _pallas(
        body, name=name, grid=(n_heads, s_len // (tiles * tq)),
        in_specs=[pl.BlockSpec((None, tiles * tq, LANE), lambda h, i: (h, i, 0)),
                  pl.BlockSpec((None, s_len, LANE), lambda h, i: (h // group, 0, 0)),
                  pl.BlockSpec((None, s_len, v_w), lambda h, i: (h // group, 0, 0))],
        out_specs=[pl.BlockSpec((None, tiles * tq, LANE), lambda h, i: (h, i, 0)),
                   pl.BlockSpec((None, tiles, 1, tq), lambda h, i: (h, i, 0, 0))],
        out_shape=[jax.ShapeDtypeStruct((n_heads, s_len, LANE), F32),
                   jax.ShapeDtypeStruct((n_heads, s_len // tq, 1, tq), F32)],
        scratch_shapes=[pltpu.VMEM((2, tq, tk), F32), pltpu.VMEM((2, tq, tk), BF16), pltpu.VMEM((2, tq, LANE), F32),
                        pltpu.VMEM((tiles, tq, LANE), F32), pltpu.VMEM((tiles, tq, v_w), F32)],
        compiler_params=_params(("parallel", "parallel"), 48 * 1024 * 1024),
    )(q, k, v)


def _mid(x, target, o_a, o_b, gates, w_out_ext, tm):
    s_len = x.shape[0]
    nt = s_len // tm
    n_heads = A_HEADS + B_HEADS

    def body(x_ref, t_ref, oa_ref, ob_ref, g_ref, w_ref,
             yt_ref, dh_ref, dgate_ref, doa_ref, dob_ref, delta_ref, loss_ref, silu_scr, dsilu_scr, y_ref):
        @pl.when(pl.program_id(0) == 0)
        def _():
            loss_ref[...] = jnp.zeros_like(loss_ref)

        def o_of(h):
            return oa_ref[h] if h < A_HEADS else ob_ref[h - A_HEADS]

        for h in range(n_heads):
            cols = slice(LANE * h, LANE * (h + 1))
            g = g_ref[:, cols]
            sig = 1.0 / (1.0 + jnp.exp(-g))
            silu = g * sig
            silu_scr[:, cols] = silu
            dsilu_scr[:, cols] = sig * (1.0 + g * (1.0 - sig))
            y = o_of(h) * silu
            y_ref[:, cols] = y.astype(BF16)
            yt_ref[cols, :] = jnp.transpose(y).astype(BF16)
        err =x_ref[...] + _nn(y_ref[...], w_ref[...]) - t_ref[...]
        sq = jnp.sum(jnp.sum(err * err, axis=-1, keepdims=True), axis=0, keepdims=True)
        loss_ref[...] += jnp.broadcast_to(sq * (0.5 / D_MODEL), loss_ref.shape)
        dh = err * (1.0 / D_MODEL)
        dh_ref[...] = dh
        dy = _nt(dh.astype(BF16), w_ref[...])
        lane = lax.broadcasted_iota(jnp.int32, (tm, LANE), 1)
        delta = jnp.zeros((tm, LANE), F32)
        for h in range(n_heads):
            cols = slice(LANE * h, LANE * (h + 1))
            dyh = dy[:, cols]
            oh = o_of(h)
            do = dyh * silu_scr[:, cols]
            dgate_ref[:, cols] = (dyh * oh * dsilu_scr[:, cols]).astype(BF16)
            delta = jnp.where(lane == h, jnp.sum(do * oh, axis=-1, keepdims=True), delta)
            if h < A_HEADS:
                doa_ref[h] = do.astype(BF16)
            else:
                dob_ref[h - A_HEADS] = do.astype(BF16)
        delta_ref[...] = jnp.transpose(delta)[0:DELTA_ROWS, :]

    row = lambda w: pl.BlockSpec((tm, w), lambda i: (i, 0))
    heads = lambda n, w=LANE: pl.BlockSpec((n, tm, w), lambda i: (0, i, 0))
    return _pallas(
        body, name="mid", grid=(nt,),
        in_specs=[row(D_MODEL), row(D_MODEL), heads(A_HEADS), heads(B_HEADS), row(N_GATE), _resident(w_out_ext.shape)],
        out_specs=[pl.BlockSpec((N_GATE, tm), lambda i: (0, i)), row(D_MODEL), row(N_GATE), heads(A_HEADS), heads(B_HEADS),
                   pl.BlockSpec((DELTA_ROWS, tm), lambda i: (0, i)),
                   _full((8, LANE))],
        out_shape=[jax.ShapeDtypeStruct((N_GATE, s_len), BF16), jax.ShapeDtypeStruct((s_len, D_MODEL), F32),
                   jax.ShapeDtypeStruct((s_len, N_GATE), BF16),
                   jax.ShapeDtypeStruct((A_HEADS, s_len, LANE), BF16), jax.ShapeDtypeStruct((B_HEADS, s_len, LANE), BF16),
                   jax.ShapeDtypeStruct((DELTA_ROWS, s_len), F32), jax.ShapeDtypeStruct((8, LANE), F32)],
        scratch_shapes=[pltpu.VMEM((tm, N_GATE), F32), pltpu.VMEM((tm, N_GATE), F32), pltpu.VMEM((tm, N_GATE), BF16)],
        compiler_params=_params(("arbitrary",), VMEM_LIMIT),
    )(x, target, o_a, o_b, gates, w_out_ext)


def _attn_bwd(q, k, v, do, lse, delta, group, tq, tk, tiles, name):
    n_heads, s_len, _ = q.shape
    nq = s_len // tq

    def body(q_ref, do_ref, lse_ref, delta_ref, k_ref, v_ref, dq_ref, dk_ref, dv_ref, s_buf, dp_buf, p_buf, ds_buf):
        @pl.when(pl.program_id(1) == 0)
        def _():
            dq_ref[...] = jnp.zeros_like(dq_ref)

        dk_ref[...] = jnp.zeros_like(dk_ref)
        dv_ref[...] = jnp.zeros_like(dv_ref)

        def keys(g):
            return _block_rows(g // nq, tk)

        def queries(g):
            return _block_rows(g % nq, tq)

        def scores(g, slot):
            s_buf[slot] = _nt(k_ref[keys(g), :], q_ref[queries(g), :])
            dp_buf[slot] = _nt(v_ref[keys(g), :], do_ref[queries(g), :])

        def elementwise(g, slot):
            p = jnp.exp2(s_buf[slot] - lse_ref[g % nq])
            p_buf[slot] = p.astype(BF16)
            ds_buf[slot] = (p * (dp_buf[slot] - delta_ref[g % nq])).astype(BF16)

        def grads(g, slot):
            dv_ref[keys(g), :] += _nn(p_buf[slot], do_ref[queries(g), :])
            dk_ref[keys(g), :] += _nn(ds_buf[slot], q_ref[queries(g), :])
            dq_ref[queries(g), :] += _tn(ds_buf[slot], k_ref[keys(g), :])

        _three_stage(tiles * nq, scores, elementwise, grads)

    whole = lambda: pl.BlockSpec((None, s_len, LANE), lambda h, j: (h, 0, 0))
    stat = lambda: pl.BlockSpec((None, nq, 1, tq), lambda h, j: (h, 0, 0, 0))
    kvb = lambda: pl.BlockSpec((None, tiles * tk, LANE), lambda h, j: (h // group, j, 0))
    outb = lambda: pl.BlockSpec((None, tiles * tk, LANE), lambda h, j: (h, j, 0))
    shape = jax.ShapeDtypeStruct((n_heads, s_len, LANE), F32)
    return _pallas(
        body, name=name, grid=(n_heads, s_len // (tiles * tk)),
        in_specs=[whole(), whole(), stat(), stat(), kvb(), kvb()],
        out_specs=[whole(), outb(), outb()],
        out_shape=[shape, shape, shape],
        scratch_shapes=[pltpu.VMEM((2, tk, tq), F32), pltpu.VMEM((2, tk, tq), F32),
                        pltpu.VMEM((2, tk, tq), BF16), pltpu.VMEM((2, tk, tq), BF16)],
        compiler_params=_params(("parallel", "arbitrary"), 48 * 1024 * 1024),
    )(q, do, lse, delta, k, v)


def _post(x, dh, pre, qbpre, kbpre, dgate, dqa, dka, dva, dqb, dkb, dvb, loss_part, tabs,
          w_in_ext, w_uq_pad, w_ukv_ext, gains, tm):
    s_len = x.shape[0]
    nt = s_len // tm

    def body(x_ref, dh_ref, pre_ref, qbpre_ref, kbpre_ref, dgate_ref,
             dqa_ref, dka_ref, dva_ref, dqb_ref, dkb_ref, dvb_ref, loss_ref,
             ca_ref, sa_ref, cb_ref, sb_ref, win_ref, wuq_ref, wukv_ref,
             gin_ref, gaq_ref, gak_ref, gcq_ref, gckv_ref, gbq_ref, gbk_ref,
             gx_ref, dproj_ref, dqbpre_ref, dkvb_ref, dsm_ref):
        @pl.when(pl.program_id(0) == 0)
        def _():
            dsm_ref[...] = jnp.zeros_like(dsm_ref)
            dsm_ref[SM_LOSS:SM_LOSS + 1, 0:LANE] = loss_ref[0:1, :]

        def add_small(r, dg):
            dsm_ref[r:r + 1, 0:dg.shape[1]] += dg

        def tok_sum(a):
            return jnp.sum(a, axis=0, keepdims=True)

        ca, sa, cb, sb = ca_ref[...], sa_ref[...], cb_ref[...], sb_ref[...]
        lane = lax.broadcasted_iota(jnp.int32, (tm, LANE), 1)

        nope_lanes = _lanes_of(lane, LAY_NOPE)

        def back(c0, c1):
            return _nt(dproj_ref[:, c0:c1], win_ref[:, c0:c1])

        dproj_ref[:, GA0:GA0 + N_GATE] = dgate_ref[...]
        dxn = back(GA0, GA0 + N_GATE)
        dg = jnp.zeros((1, LANE), F32)
        for h in range(A_HEADS):
            dn = _rope_bwd(dqa_ref[h] * SCALE_A, ca, sa)
            dx, dgr = _rms_bwd(dn, pre_ref[:, QA0 + LANE * h:QA0 + LANE * (h + 1)], gaq_ref[...], A_DIM)
            dproj_ref[:, QA0 + LANE * h:QA0 + LANE * (h + 1)] = dx.astype(BF16)
            dg = dg + tok_sum(dgr)
        add_small(SM_AQ, _unspread_row(dg, LAY_ROPE_A))
        dxn = dxn + back(QA0, KA0)
        dg = jnp.zeros((1, LANE), F32)
        for h in range(A_KV):
            dk = dka_ref[A_GROUP * h]
            dv = dva_ref[A_GROUP * h]
            for g in range(1, A_GROUP):
                dk = dk + dka_ref[A_GROUP * h + g]
                dv = dv + dva_ref[A_GROUP * h + g]
            dn = _rope_bwd(dk * LN2, ca, sa)
            dx, dgr = _rms_bwd(dn, pre_ref[:, KA0 + LANE * h:KA0 + LANE * (h + 1)], gak_ref[...], A_DIM)
            dproj_ref[:, KA0 + LANE * h:KA0 + LANE * (h + 1)] = dx.astype(BF16)
            dproj_ref[:, VA0 + LANE * h:VA0 + LANE * (h + 1)] = dv.astype(BF16)
            dg = dg + tok_sum(dgr)
        add_small(SM_AK, _unspread_row(dg, LAY_ROPE_A))
        dxn = dxn + back(KA0, GA0)
        dg = jnp.zeros((1, LANE), F32)
        for h in range(B_HEADS):
            cols = slice(LANE * h, LANE * (h + 1))
            dn = _rope_bwd(dqb_ref[h] * SCALE_B, cb, sb)
            dx, dgr = _rms_bwd(dn, qbpre_ref[:, cols], gbq_ref[...], B_QK)
            dqbpre_ref[:, cols] = dx.astype(BF16)
            dg = dg + tok_sum(dgr)
        add_small(SM_BQ, _unspread_row(dg, LAY_ROPE_B))
        dcq = _nt(dqbpre_ref[...], wuq_ref[...])
        dx, dgr = _rms_bwd(dcq, pre_ref[:, VA0:VA0 + B_Q_RANK], gcq_ref[...], B_Q_RANK)
        dproj_ref[:, CQ0:CQ0 + B_Q_RANK] = dx.astype(BF16)
        add_small(SM_CQ, tok_sum(dgr))
        dxn = dxn + back(CQ0, CKV0)
        dg = jnp.zeros((1, LANE), F32)
        dkr = jnp.zeros((tm, LANE), F32)
        for h in range(B_HEADS):
            cols = slice(LANE * h, LANE * (h + 1))
            dn = _rope_bwd(dkb_ref[h] * LN2, cb, sb)
            dx, dgr = _rms_bwd(dn, kbpre_ref[:, cols], gbk_ref[...], B_QK)
            dkvb_ref[:, cols] = jnp.where(nope_lanes, dx, 0.0).astype(BF16)
            dkvb_ref[:, B_HEADS * LANE + LANE * h:B_HEADS * LANE + LANE * (h + 1)] = dvb_ref[h].astype(BF16)
            dkr = dkr + dx
            dg = dg + tok_sum(dgr)
        add_small(SM_BK, _unspread_row(dg, LAY_ROPE_B))
        dproj_ref[:, KR0:KR0 + LANE] = jnp.where(_lanes_of(lane, LAY_KR), dkr, 0.0).astype(BF16)
        dckv = _nt(dkvb_ref[...], wukv_ref[...])
        dx, dgr = _rms_bwd(dckv, pre_ref[:, VA0 + B_Q_RANK:N_PRE], gckv_ref[...], B_KV_RANK)
        dproj_ref[:, CKV0:CKV0 + B_KV_RANK] = dx.astype(BF16)
        add_small(SM_CKV, tok_sum(dgr))
        dxn = dxn + back(CKV0, N_EXT)
        dx, dgr = _rms_bwd(dxn, x_ref[...], gin_ref[...], D_MODEL)
        gx_ref[...] = dh_ref[...] + dx
        add_small(SM_IN, tok_sum(dgr))

    row = lambda w: pl.BlockSpec((tm, w), lambda i: (i, 0))
    heads = lambda n: pl.BlockSpec((n, tm, LANE), lambda i: (0, i, 0))
    return _pallas(
        body, name="post", grid=(nt,),
        in_specs=[row(D_MODEL), row(D_MODEL), row(N_PRE), row(B_HEADS * LANE), row(B_HEADS * LANE), row(N_GATE),
                  heads(A_HEADS), heads(A_HEADS), heads(A_HEADS), heads(B_HEADS), heads(B_HEADS), heads(B_HEADS),
                  _full(loss_part.shape), row(LANE), row(LANE), row(LANE), row(LANE),
                  _resident(w_in_ext.shape), _resident(w_uq_pad.shape), _resident(w_ukv_ext.shape)]
                 + [_full(g.shape) for g in gains],
        out_specs=[row(D_MODEL), row(N_EXT), row(B_HEADS * LANE), row(2 * B_HEADS * LANE), _full((SM_ROWS, SM_W))],
        out_shape=[jax.ShapeDtypeStruct((s_len, D_MODEL), F32), jax.ShapeDtypeStruct((s_len, N_EXT), BF16),
                   jax.ShapeDtypeStruct((s_len, B_HEADS * LANE), BF16),
                   jax.ShapeDtypeStruct((s_len, 2 * B_HEADS * LANE), BF16),
                   jax.ShapeDtypeStruct((SM_ROWS, SM_W), F32)],
        compiler_params=_params(("arbitrary",), VMEM_LIMIT),
    )(x, dh, pre, qbpre, kbpre, dgate, dqa, dka, dva, dqb, dkb, dvb, loss_part, *tabs,
      w_in_ext, w_uq_pad, w_ukv_ext, *gains)


def _grad_w(a_t, b, tn, ts, name):
    m, s_len = a_t.shape
    n = b.shape[1]

    def body(a_ref, b_ref, o_ref):
        @pl.when(pl.program_id(1) == 0)
        def _():
            o_ref[...] = jnp.zeros_like(o_ref)

        o_ref[...] += _nn(a_ref[...], b_ref[...].astype(BF16))

    return _pallas(
        body, name=name, grid=(n // tn, s_len // ts),
        in_specs=[pl.BlockSpec((m, ts), lambda j, t: (0, t)), pl.BlockSpec((ts, tn), lambda j, t: (t, j))],
        out_specs=pl.BlockSpec((m, tn), lambda j, t: (0, j)),
        out_shape=jax.ShapeDtypeStruct((m, n), F32),
        compiler_params=_params(("parallel", "arbitrary"), 48 * 1024 * 1024),
    )(a_t, b)


def _adam_math(w, g, m, v):
    nm = ADAM_B1 * m + (1.0 - ADAM_B1) * g
    nv = ADAM_B2 * v + (1.0 - ADAM_B2) * (g * g)
    m_hat = nm / (1.0 - ADAM_B1 ** ADAM_STEP)
    v_hat = nv / (1.0 - ADAM_B2 ** ADAM_STEP)
    return -ADAM_LR * (m_hat / (jnp.sqrt(v_hat) + ADAM_EPS) + ADAM_WD * w), nm, nv


def _adamw_rows(w, g, m, v, tr):
    rows, cols = w.shape

    def body(w_ref, g_ref, m_ref, v_ref, d_ref, nm_ref, nv_ref):
        d_ref[...], nm_ref[...], nv_ref[...] = _adam_math(w_ref[...], g_ref[...], m_ref[...], v_ref[...])

    blk = pl.BlockSpec((tr, cols), lambda i: (i, 0))
    shape = jax.ShapeDtypeStruct((rows, cols), F32)
    return _pallas(
        body, name="adamw_w_in", grid=(rows // tr,),
        in_specs=[blk] * 4, out_specs=[blk] * 3, out_shape=[shape] * 3,
        compiler_params=_params(("parallel",), 32 * 1024 * 1024),
    )(w, g, m, v)


def _adamw_rest(bigs, smalls, g_small):
    nb, ns = len(bigs), len(smalls)

    def body(*refs):
        ins, outs = refs[:4 * nb + 3 * ns + 1], refs[4 * nb + 3 * ns + 1:]
        for i in range(nb):
            w_ref, g_ref, m_ref, v_ref = ins[4 * i:4 * i + 4]
            d_ref, nm_ref, nv_ref = outs[3 * i:3 * i + 3]
            d_ref[...], nm_ref[...], nv_ref[...] = _adam_math(w_ref[...], g_ref[...], m_ref[...], v_ref[...])
        gs_ref = ins[-1]
        for i in range(ns):
            w_ref, m_ref, v_ref = ins[4 * nb + 3 * i:4 * nb + 3 * i + 3]
            g_ref, d_ref, nm_ref, nv_ref = outs[3 * nb + 4 * i:3 * nb + 4 * i + 4]
            g = gs_ref[i:i + 1, 0:w_ref.shape[1]]
            g_ref[...] = g
            d_ref[...], nm_ref[...], nv_ref[...] = _adam_math(w_ref[...], g, m_ref[...], v_ref[...])

    flat_in = [a for quad in bigs for a in quad] + [a for tri in smalls for a in tri] + [g_small]
    out_shape = ([jax.ShapeDtypeStruct(q[0].shape, F32) for q in bigs for _ in range(3)]
                 + [jax.ShapeDtypeStruct(t[0].shape, F32) for t in smalls for _ in range(4)])
    return _pallas(
        body, name="adamw_rest",
        in_specs=[pl.BlockSpec(memory_space=pltpu.VMEM)] * len(flat_in),
        out_specs=[pl.BlockSpec(memory_space=pltpu.VMEM)] * len(out_shape),
        out_shape=out_shape,
        compiler_params=_params(vmem=32 * 1024 * 1024),
    )(*flat_in)


def _place(pieces, n):
    out, at = [], 0
    for lane0, arr in sorted(pieces, key=lambda p: p[0]):
        out += [jnp.zeros((n, lane0 - at), F32), arr]
        at = lane0 + arr.shape[1]
    return jnp.concatenate(out + [jnp.zeros((n, LANE - at), F32)], axis=1)


def _rope_tables(s_len):
    rows = s_len // GRID_W
    row = jnp.arange(rows, dtype=F32)
    col = jnp.arange(GRID_W, dtype=F32)

    def lay(dim, layout, first_dim, ones):
        half = dim // 2
        inv = 1.0 / (ROPE_THETA ** (jnp.arange(0, half, 2, dtype=F32) / half))
        ang_r, ang_c = row[:, None] * inv[None, :], col[:, None] * inv[None, :]
        at = {a - first_dim: lane0 for a, _, lane0 in layout}
        q = dim // 4
        r1, r2, c1, c2 = at[0], at[q], at[2 * q], at[3 * q]
        cos_r = _place([(r1, jnp.cos(ang_r)), (r2, jnp.cos(ang_r))], rows)
        sin_r = _place([(r1, -jnp.sin(ang_r)), (r2, jnp.sin(ang_r))], rows)
        cos_c = _place([(c1, jnp.cos(ang_c)), (c2, jnp.cos(ang_c))] + [(l0, jnp.ones((GRID_W, n), F32)) for _, n, l0 in ones],
                       GRID_W)
        sin_c = _place([(c1, -jnp.sin(ang_c)), (c2, jnp.sin(ang_c))], GRID_W)
        cos = (cos_r[:, None, :] + cos_c[None, :, :]).reshape(s_len, LANE)
        sin = (sin_r[:, None, :] + sin_c[None, :, :]).reshape(s_len, LANE)
        return cos, sin

    cos_a, sin_a = lay(A_DIM, LAY_ROPE_A, 0, ())
    cos_b, sin_b = lay(B_ROPE, LAY_KR, 0, LAY_NOPE)
    return cos_a, sin_a, cos_b, sin_b


def _spread(w, n_heads, dim, axis, layout):
    w3 = w.reshape(w.shape[:axis] + (n_heads, dim) + w.shape[axis + 1:])
    out, at = [], 0

    def zeros(n):
        return jnp.zeros(w3.shape[:axis + 1] + (n,) + w3.shape[axis + 2:], w.dtype)

    for a0, n, lane0 in sorted(layout, key=lambda seg: seg[2]):
        out += [zeros(lane0 - at), lax.slice_in_dim(w3, a0, a0 + n, axis=axis + 1)]
        at = lane0 + n
    out = jnp.concatenate(out + [zeros(LANE - at)], axis=axis + 1)
    return out.reshape(w.shape[:axis] + (n_heads * LANE,) + w.shape[axis + 1:])


def _unspread(w, n_heads, dim, axis, layout):
    w3 = w.reshape(w.shape[:axis] + (n_heads, LANE) + w.shape[axis + 1:])
    parts = [lax.slice_in_dim(w3, lane0, lane0 + n, axis=axis + 1) for _, n, lane0 in sorted(layout)]
    out = parts[0] if len(parts) == 1 else jnp.concatenate(parts, axis=axis + 1)
    return out.reshape(w.shape[:axis] + (n_heads * dim,) + w.shape[axis + 1:])


def _head_cols(first, n_heads, dim, layout):
    out = np.full((n_heads * LANE,), -1, np.int32)
    for h in range(n_heads):
        for a0, n, lane0 in layout:
            out[h * LANE + lane0:h * LANE + lane0 + n] = first + h * dim + a0 + np.arange(n)
    return out


def _inverse(src, n):
    dst = np.full((n,), -1, np.int32)
    dst[src[src >= 0]] = np.nonzero(src >= 0)[0]
    return dst


def _column_maps():
    a_w, kv_w = A_HEADS * A_DIM, A_KV * A_DIM
    o_g = a_w + 2 * kv_w
    o_cq = o_g + a_w
    o_kr = o_cq + B_Q_RANK + B_KV_RANK
    src_in = np.concatenate([
        _head_cols(0, A_HEADS, A_DIM, LAY_ROPE_A), _head_cols(a_w, A_KV, A_DIM, LAY_ROPE_A),
        _head_cols(a_w + kv_w, A_KV, A_DIM, LAY_PLAIN_A), _head_cols(o_g, A_HEADS, A_DIM, LAY_PLAIN_A),
        np.arange(o_kr + B_ROPE, N_IN), np.arange(o_cq, o_kr), _head_cols(o_kr, 1, B_ROPE, LAY_KR)]).astype(np.int32)
    src_uq = _head_cols(0, B_HEADS, B_QK, LAY_ROPE_B)
    per = B_NOPE + B_V
    src_ukv = np.concatenate([_head_cols(0, B_HEADS, per, LAY_NOPE),
                              _head_cols(B_NOPE, B_HEADS, per, ((0, B_V, 0),))]).astype(np.int32)
    assert len(src_in) == N_EXT
    return src_in, src_uq, src_ukv


def _round_up(n, m):
    return (n + m - 1) // m * m


def _permute_cols(xs, maps, name):
    maps = [np.asarray(m, np.int32) for m in maps]
    n_arr = len(xs)

    def body(*refs):
        row = lax.broadcasted_iota(jnp.int32, (LANE, LANE), 0)
        for x_ref, src_ref, o_ref, src in zip(refs[:n_arr], refs[n_arr:2 * n_arr], refs[2 * n_arr:], maps):
            for c in range(len(src) // LANE):
                cols = slice(c * LANE, (c + 1) * LANE)
                acc = jnp.zeros((x_ref.shape[0], LANE), F32)
                for kb in sorted({int(v) // LANE for v in src[cols] if v >= 0}):
                    sel = jnp.where(row + kb * LANE == src_ref[:, cols], 1.0, 0.0).astype(BF16)
                    part = x_ref[:, kb * LANE:(kb + 1) * LANE]
                    if part.dtype == BF16:
                        acc = acc + _nn(part, sel)
                    else:
                        hi = part.astype(BF16)
                        rest = part - hi.astype(F32)
                        mid = rest.astype(BF16)
                        low = (rest - mid.astype(F32)).astype(BF16)
                        acc = acc + ((_nn(hi, sel) + _nn(mid, sel)) + _nn(low, sel))
                o_ref[:, cols] = acc.astype(o_ref.dtype)

    return _pallas(
        body, name=name,
        out_shape=[jax.ShapeDtypeStruct((x.shape[0], len(m)), x.dtype) for x, m in zip(xs, maps)],
        compiler_params=_params(vmem=48 * 1024 * 1024),
    )(*xs, *[jnp.asarray(m).reshape(1, -1) for m in maps])


def _pad_cols(w):
    return jnp.pad(w, ((0, 0), (0, _round_up(w.shape[1], LANE) - w.shape[1])))


def _ext_weights(g_in, g_uq, g_ukv, g_out):
    w_in = g_in.transpose(1, 0, 2).reshape(D_MODEL, N_IN)
    w_uq = g_uq.reshape(B_Q_RANK, B_HEADS * B_QK)
    w_ukv = g_ukv.transpose(1, 0, 2).reshape(B_KV_RANK, B_HEADS * (B_NOPE + B_V))
    w_out = g_out.reshape(D_MODEL, D_MODEL)
    w_in_ext, w_uq_pad, w_ukv_ext = _permute_cols([_pad_cols(w_in), w_uq, w_ukv], _column_maps(), "lay_out_weights")
    a_w = A_HEADS * A_DIM
    w_out_ext = jnp.concatenate([_spread(w_out[:a_w], A_HEADS, A_DIM, 0, LAY_PLAIN_A), w_out[a_w:]], axis=0)
    return w_in_ext, w_uq_pad, w_ukv_ext, w_out_ext


def _fold_grads(d_in_ext, d_uq_pad, d_ukv_ext, d_out_ext):
    src_in, src_uq, src_ukv = _column_maps()
    n_uq, n_ukv = B_HEADS * B_QK, B_HEADS * (B_NOPE + B_V)
    back = [np.concatenate([_inverse(src_in, N_IN), np.full((_round_up(N_IN, LANE) - N_IN,), -1, np.int32)]),
            _inverse(src_uq, n_uq), _inverse(src_ukv, n_ukv)]
    d_in, d_uq, d_ukv = _permute_cols([d_in_ext, d_uq_pad, d_ukv_ext], back, "fold_grads")
    d_in = d_in[:, :N_IN]
    a_w = A_HEADS * A_DIM
    d_out = jnp.concatenate([_unspread(d_out_ext[:A_HEADS * LANE], A_HEADS, A_DIM, 0, LAY_PLAIN_A), d_out_ext[A_HEADS * LANE:]], axis=0)
    return (d_in.reshape(D_MODEL, N_CHIPS, SH_IN[1]).transpose(1, 0, 2),
            d_uq.reshape((N_CHIPS,) + SH_UQ),
            d_ukv.reshape(B_KV_RANK, N_CHIPS, SH_UKV[1]).transpose(1, 0, 2),
            d_out.reshape((N_CHIPS,) + SH_OUT))


def kernel(x, norm_in, w_in, a_q_norm, a_k_norm, b_cq_norm, b_ckv_norm, w_uq, w_ukv, b_q_norm, b_k_norm, w_out, loss_target, m_norm_in, m_w_in, m_a_q_norm, m_a_k_norm, m_b_cq_norm, m_b_ckv_norm, m_w_uq, m_w_ukv, m_b_q_norm, m_b_k_norm, m_w_out, v_norm_in, v_w_in, v_a_q_norm, v_a_k_norm, v_b_cq_norm, v_b_ckv_norm, v_w_uq, v_w_ukv, v_b_q_norm, v_b_k_norm, v_w_out):
    s_len = x.shape[1]
    xs, ts = x[0], loss_target[0]
    tm = min(256, s_len)
    tq, tk_f = min(512, s_len // 2), min(1024, s_len // 2)
    tq_b, tk_b = min(1024, s_len // 2), min(512, s_len)
    tiles_f = min(4, s_len // tq)
    tiles_b = min(2, s_len // tk_b)

    w_in_ext, w_uq_pad, w_ukv_ext, w_out_ext = _ext_weights(*_gather_weights((w_in[0], w_uq[0], w_ukv[0], w_out[0])))
    gains = (norm_in, _spread(a_q_norm, 1, A_DIM, 1, LAY_ROPE_A), _spread(a_k_norm, 1, A_DIM, 1, LAY_ROPE_A), b_cq_norm, b_ckv_norm,
             _spread(b_q_norm, 1, B_QK, 1, LAY_ROPE_B), _spread(b_k_norm, 1, B_QK, 1, LAY_ROPE_B))
    tabs = _rope_tables(s_len)

    (xn_t, gates, pre, qbpre, kbpre, cq_t, ckv_t, qa, ka, va, qb, kb, vb) = _pre(
        xs, tabs, w_in_ext, w_uq_pad, w_ukv_ext, gains, tm)
    o_a, lse_a = _attn_fwd(qa, ka, va, A_GROUP, A_DIM, tq, tk_f, tiles_f, "attn_fwd_a")
    o_b, lse_b = _attn_fwd(qb, kb, vb, 1, B_V, tq, tk_f, tiles_f, "attn_fwd_b")
    y_t, dh, dgate, do_a, do_b, delta, loss_part = _mid(xs, ts, o_a, o_b, gates, w_out_ext, min(512, s_len))

    def stat(a):
        return a.reshape(a.shape[0], s_len // tq_b, 1, tq_b)

    dqa, dka, dva = _attn_bwd(qa, ka, va, do_a, stat(lse_a), stat(delta[:A_HEADS]), A_GROUP, tq_b, tk_b, tiles_b, "attn_bwd_a")
    dqb, dkb, dvb = _attn_bwd(qb, kb, vb, do_b, stat(lse_b), stat(delta[A_HEADS:A_HEADS + B_HEADS]), 1, tq_b, tk_b,
                              tiles_b, "attn_bwd_b")
    grad_x, dproj, dqbpre, dkvb, d_small = _post(
        xs, dh, pre, qbpre, kbpre, dgate, dqa, dka, dva, dqb, dkb, dvb, loss_part, tabs,
        w_in_ext, w_uq_pad, w_ukv_ext, gains, tm)

    ts_w = min(2048, s_len)
    d_in_ext = _grad_w(xn_t, dproj, 768, ts_w, "grad_w_in")
    d_out_ext = _grad_w(y_t, dh, 512, ts_w, "grad_w_out")
    d_uq_pad = _grad_w(cq_t, dqbpre, 512, ts_w, "grad_w_uq")
    d_ukv_ext = _grad_w(ckv_t, dkvb, 1024, ts_w, "grad_w_ukv")

    g_in, g_uq, g_ukv, g_out, g_small = _reduce_grads(_fold_grads(d_in_ext, d_uq_pad, d_ukv_ext, d_out_ext), d_small)
    d_in, nm_in, nv_in = (a.T for a in _adamw_rows(w_in[0].T, g_in.T, m_w_in[0].T, v_w_in[0].T, SH_IN[1] // 7))
    rest = _adamw_rest(
        [(w_uq[0], g_uq, m_w_uq[0], v_w_uq[0]), (w_ukv[0], g_ukv, m_w_ukv[0], v_w_ukv[0]),
         (w_out[0], g_out, m_w_out[0], v_w_out[0])],
        [(norm_in, m_norm_in, v_norm_in), (a_q_norm, m_a_q_norm, v_a_q_norm), (a_k_norm, m_a_k_norm, v_a_k_norm),
         (b_cq_norm, m_b_cq_norm, v_b_cq_norm), (b_ckv_norm, m_b_ckv_norm, v_b_ckv_norm),
         (b_q_norm, m_b_q_norm, v_b_q_norm), (b_k_norm, m_b_k_norm, v_b_k_norm)], g_small)
    (d_uq, nm_uq, nv_uq), (d_ukv, nm_ukv, nv_ukv), (d_out, nm_out, nv_out) = (rest[3 * i:3 * i + 3] for i in range(3))
    sm = [rest[9 + 4 * i:9 + 4 * i + 4] for i in range(7)]

    def leaves(k, p_in, p_uq, p_ukv, p_out):
        return [sm[SM_IN][k], p_in[None], sm[SM_AQ][k], sm[SM_AK][k], sm[SM_CQ][k], sm[SM_CKV][k], p_uq[None], p_ukv[None],
                sm[SM_BQ][k], sm[SM_BK][k], p_out[None]]

    return (g_small[SM_LOSS, 0], grad_x[None], *leaves(0, g_in, g_uq, g_ukv, g_out), *leaves(1, d_in, d_uq, d_ukv, d_out),
            *leaves(2, nm_in, nm_uq, nm_ukv, nm_out), *leaves(3, nv_in, nv_uq, nv_ukv, nv_out))
```

```python
import jax
import jax.numpy as jnp
import numpy as np
from jax import lax
from jax.experimental import pallas as pl
from jax.experimental.pallas import tpu as pltpu

F32 = jnp.float32
BF16 = jnp.bfloat16
MESH = pl.DeviceIdType.MESH

D_MODEL = 1024
GRID_W = 64
ROPE_THETA = 10000.0
EPS = 1e-6
A_HEADS, A_KV, A_DIM = 8, 2, 64
A_GROUP = A_HEADS // A_KV
B_HEADS, B_NOPE, B_ROPE, B_V = 4, 64, 32, 128
B_QK = B_NOPE + B_ROPE
B_Q_RANK, B_KV_RANK = 384, 256
N_IN = 2464
SCALE_A = 1.0 / float(np.sqrt(A_DIM))
SCALE_B = 1.0 / float(np.sqrt(B_QK))
LOG2E = float(np.log2(np.e))
LN2 = float(np.log(2.0))
ADAM_LR, ADAM_B1, ADAM_B2, ADAM_EPS, ADAM_WD, ADAM_STEP = 0.001, 0.9, 0.999, 1e-08, 0.01, 10

LANE = 128
VMEM_BYTES = 64 * 1024 * 1024
VMEM_LIMIT = VMEM_BYTES - 8 * 1024 * 1024
VMEM_MID = 48 * 1024 * 1024
VMEM_SMALL = 32 * 1024 * 1024

QA0 = 0
KA0 = QA0 + A_HEADS * LANE
VA0 = KA0 + A_KV * LANE
GA0 = VA0 + A_KV * LANE
GB0 = GA0 + A_HEADS * LANE
CQ0 = GB0 + B_HEADS * LANE
CKV0 = CQ0 + B_Q_RANK
KR0 = CKV0 + B_KV_RANK
N_EXT = KR0 + LANE
N_GATE = (A_HEADS + B_HEADS) * LANE
DELTA_ROWS = 16
N_PRE = KA0 + A_KV * LANE + B_Q_RANK + B_KV_RANK

ROT = LANE // 2
_QA = A_DIM // 4
_QB = B_ROPE // 4
LAY_PLAIN_A = ((0, A_DIM, 0),)
LAY_ROPE_A = ((0, _QA, 0), (2 * _QA, _QA, _QA), (_QA, _QA, ROT), (3 * _QA, _QA, ROT + _QA))
LAY_KR = ((0, _QB, 0), (2 * _QB, _QB, _QB), (_QB, _QB, ROT), (3 * _QB, _QB, ROT + _QB))
LAY_NOPE = ((0, B_NOPE // 2, 2 * _QB), (B_NOPE // 2, B_NOPE // 2, ROT + 2 * _QB))
LAY_ROPE_B = LAY_NOPE + tuple((B_NOPE + a, n, at) for a, n, at in LAY_KR)

N_CHIPS = 4
SH_IN = (D_MODEL, N_IN // N_CHIPS)
SH_UQ = (B_Q_RANK // N_CHIPS, B_HEADS * B_QK)
SH_UKV = (B_KV_RANK, B_HEADS * (B_NOPE + B_V) // N_CHIPS)
SH_OUT = (D_MODEL // N_CHIPS, D_MODEL)
SM_ROWS, SM_W = 16, D_MODEL
SM_IN, SM_AQ, SM_AK, SM_CQ, SM_CKV, SM_BQ, SM_BK, SM_LOSS = range(8)
F32_ROWS, BF16_ROWS = 8, 16


def _pallas(body, **kw):
    return pl.pallas_call(body, **kw)


def _params(sem=None, vmem=None):
    return pltpu.CompilerParams(dimension_semantics=sem, vmem_limit_bytes=vmem)


def _rms_fwd(x, g, n):
    r = lax.rsqrt(jnp.sum(x * x, axis=-1, keepdims=True) * (1.0 / n) + EPS)
    return x * r * g


def _rms_bwd(dy, x, g, n):
    u = dy * g
    r = lax.rsqrt(jnp.sum(x * x, axis=-1, keepdims=True) * (1.0 / n) + EPS)
    ux = jnp.sum(u * x, axis=-1, keepdims=True)
    xhat = x * r
    dx = r * (u - xhat * (r * ux * (1.0 / n)))
    return dx, dy * xhat


def _rope_fwd(y, cos, sin):
    return y * cos + pltpu.roll(y, ROT, 1) * sin


def _rope_bwd(d, cos, sin):
    return d * cos - pltpu.roll(d, ROT, 1) * sin


def _lanes_of(lane, layout):
    m = None
    for _, n, at in layout:
        seg = (lane >= at) & (lane < at + n)
        m = seg if m is None else (m | seg)
    return m


def _unspread_row(v, layout):
    v8 = jnp.broadcast_to(v, (F32_ROWS, LANE))
    lane = lax.broadcasted_iota(jnp.int32, v8.shape, 1)
    out = jnp.zeros_like(v8)
    for a, n, at in layout:
        moved = v8 if a == at else pltpu.roll(v8, (a - at) % LANE, 1)
        out = jnp.where((lane >= a) & (lane < a + n), moved, out)
    return out[0:1, :]


def _nt(a, b):
    return lax.dot_general(a, b, (((1,), (1,)), ((), ())), preferred_element_type=F32)


def _tn(a, b):
    return lax.dot_general(a, b, (((0,), (0,)), ((), ())), preferred_element_type=F32)


def _nn(a, b):
    return jnp.dot(a, b, preferred_element_type=F32)


def _block_rows(i, size):
    if isinstance(i, int):
        return pl.ds(i * size, size)
    return pl.ds(pl.multiple_of(i * size, size), size)


MAX_STATIC_BLOCKS = 32


def _three_stage(n, first, second, third):
    assert n >= 2 and n % 2 == 0
    first(0, 0)
    first(1, 1)
    second(0, 0)
    if n <= MAX_STATIC_BLOCKS:
        for i in range(1, n - 1):
            first(i + 1, (i + 1) % 2)
            second(i, i % 2)
            third(i - 1, (i - 1) % 2)
    else:
        def pair(t, carry):
            i = 2 * t + 1
            first(i + 1, 0)
            second(i, 1)
            third(i - 1, 0)
            first(i + 2, 1)
            second(i + 1, 0)
            third(i, 1)
            return carry

        lax.fori_loop(0, (n - 2) // 2, pair, 0)
    second(n - 1, 1)
    third(n - 2, 0)
    third(n - 1, 1)


def _full(shape):
    return pl.BlockSpec(shape, lambda *_: (0,) * len(shape))


def _resident(shape):
    return pl.BlockSpec(shape, lambda *_: (0,) * len(shape), pipeline_mode=pl.Buffered(1))


def _gather_weights(shards):
    n = len(shards)
    halves = [w.shape[0] // 2 for w in shards]

    def body(*refs):
        w_refs, out_refs, (send_sems, recv_sems) = refs[:n], refs[n:2 * n], refs[2 * n:]
        x, y, c = lax.axis_index("x"), lax.axis_index("y"), lax.axis_index("c")
        sibling = (x, y, 1 - c)
        chips = [(1 - x, y), (x, 1 - y), (1 - x, 1 - y)]
        me = 2 * x + y

        def copy(a, k, j, hc, to):
            part = out_refs[a].at[j, pl.ds(pl.multiple_of(hc * halves[a], BF16_ROWS), halves[a]), :]
            return pltpu.make_async_remote_copy(
                src_ref=part, dst_ref=part, send_sem=send_sems.at[6 * a + k], recv_sem=recv_sems.at[6 * a + k],
                device_id=to, device_id_type=MESH)

        started = []
        for a in range(n):
            out_refs[a][me] = w_refs[a][...].astype(BF16)
            for k, chip in enumerate(chips):
                started.append(copy(a, k, me, c, (*chip, c)))
                started[-1].start()
        for k, chip in enumerate(chips):
            for a in range(n):
                copy(a, k, 2 * chip[0] + chip[1], c, (*chip, c)).wait_recv()
                started.append(copy(a, 3 + k, 2 * chip[0] + chip[1], c, sibling))
                started[-1].start()
        for k, chip in enumerate(chips):
            for a in range(n):
                copy(a, 3 + k, 2 * chip[0] + chip[1], 1 - c, sibling).wait_recv()
        for cp in started:
            cp.wait_send()

    return _pallas(
        body, name="gather_weights",
        out_shape=[jax.ShapeDtypeStruct((N_CHIPS,) + w.shape, BF16) for w in shards],
        in_specs=[pl.BlockSpec(memory_space=pltpu.VMEM)] * n,
        out_specs=[pl.BlockSpec(memory_space=pltpu.VMEM)] * n,
        scratch_shapes=[pltpu.SemaphoreType.DMA((6 * n,)), pltpu.SemaphoreType.DMA((6 * n,))],
        compiler_params=_params(vmem=VMEM_SMALL),
    )(*shards)


def _reduce_grads(parts, small):
    n_big = len(parts)
    n = n_big + 1
    shapes = [p.shape[1:] for p in parts] + [small.shape]
    halves = [sh[0] // 2 for sh in shapes]

    def body(*refs):
        p_refs, out_refs, rec_a, rec_b = refs[:n], refs[n:2 * n], refs[2 * n:3 * n], refs[3 * n:4 * n]
        send_b = refs[4 * n:4 * n + n_big]
        sa_send, sa_recv, sb_send, sb_recv, sc_send, sc_recv = refs[4 * n + n_big:]
        x, y, c = lax.axis_index("x"), lax.axis_index("y"), lax.axis_index("c")
        sibling = (x, y, 1 - c)
        me = 2 * x + y

        def rows(a, hc):
            return pl.ds(pl.multiple_of(hc * halves[a], F32_ROWS), halves[a])

        def partial(a, j, hc):
            return p_refs[a].at[j, rows(a, hc), :] if a < n_big else p_refs[a].at[rows(a, hc), :]

        def copy_a(a, j):
            return pltpu.make_async_remote_copy(
                src_ref=partial(a, j, 1 - c), dst_ref=rec_a[a].at[j],
                send_sem=sa_send.at[N_CHIPS * a + j], recv_sem=sa_recv.at[N_CHIPS * a + j],
                device_id=sibling, device_id_type=MESH)

        def copy_b(a, r):
            j = me ^ r
            k = (N_CHIPS - 1) * a + r - 1
            return pltpu.make_async_remote_copy(
                src_ref=(send_b[a] if a < n_big else rec_a[a]).at[j], dst_ref=rec_b[a].at[r],
                send_sem=sb_send.at[k], recv_sem=sb_recv.at[k], device_id=(j // 2, j % 2, c), device_id_type=MESH)

        def copy_c(a):
            return pltpu.make_async_remote_copy(
                src_ref=out_refs[a].at[rows(a, c), :], dst_ref=out_refs[a].at[rows(a, c), :],
                send_sem=sc_send.at[a], recv_sem=sc_recv.at[a], device_id=sibling, device_id_type=MESH)

        for a in range(n):
            for j in range(N_CHIPS):
                copy_a(a, j).start()
        for r in range(1, N_CHIPS):
            j = me ^ r
            for a in range(n):
                copy_a(a, j).wait_recv()
                chip_part = rec_a[a][j] + partial(a, j, c)[...]
                if a < n_big:
                    send_b[a][j] = chip_part.astype(BF16)
                else:
                    rec_a[a][j] = chip_part
                copy_b(a, r).start()
        for a in range(n):
            copy_a(a, me).wait_recv()
            rec_b[a][0] = (rec_a[a][me] + partial(a, me, c)[...]).astype(rec_b[a].dtype)
        for a in range(n):
            for r in range(1, N_CHIPS):
                copy_b(a, r).wait_recv()
            total = rec_b[a][me].astype(F32)
            for j in range(1, N_CHIPS):
                total = total + rec_b[a][j ^ me].astype(F32)
            out_refs[a][rows(a, c), :] = total
            copy_c(a).start()
        for a in range(n):
            copy_c(a).wait_recv()
        for a in range(n):
            for j in range(N_CHIPS):
                copy_a(a, j).wait_send()
            for r in range(1, N_CHIPS):
                copy_b(a, r).wait_send()
            copy_c(a).wait_send()

    dma = pltpu.SemaphoreType.DMA
    return _pallas(
        body, name="reduce_grads",
        out_shape=[jax.ShapeDtypeStruct(sh, F32) for sh in shapes],
        in_specs=[pl.BlockSpec(memory_space=pltpu.VMEM)] * n,
        out_specs=[pl.BlockSpec(memory_space=pltpu.VMEM)] * n,
        scratch_shapes=[pltpu.VMEM((N_CHIPS, h) + sh[1:], F32) for h, sh in zip(halves, shapes)]
                       + [pltpu.VMEM((N_CHIPS, h) + sh[1:], BF16 if a < n_big else F32)
                          for a, (h, sh) in enumerate(zip(halves, shapes))]
                       + [pltpu.VMEM((N_CHIPS, h) + sh[1:], BF16) for h, sh in zip(halves[:n_big], shapes[:n_big])]
                       + [dma((N_CHIPS * n,)), dma((N_CHIPS * n,)), dma(((N_CHIPS - 1) * n,)), dma(((N_CHIPS - 1) * n,)),
                          dma((n,)), dma((n,))],
        compiler_params=_params(vmem=VMEM_LIMIT),
    )(*parts, small)


def _pre(x, tabs, w_in_ext, w_uq_pad, w_ukv_ext, gains, tm):
    s_len = x.shape[0]
    nt = s_len // tm

    def body(x_ref, ca_ref, sa_ref, cb_ref, sb_ref, win_ref, wuq_ref, wukv_ref,
             gin_ref, gaq_ref, gak_ref, gcq_ref, gckv_ref, gbq_ref, gbk_ref,
             xn_ref, gates_ref, pre_ref, qbpre_ref, kbpre_ref, cq_ref, ckv_ref,
             qa_ref, ka_ref, va_ref, qb_ref, kb_ref, vb_ref, proj):
        xn = _rms_fwd(x_ref[...], gin_ref[...], D_MODEL)
        xn_ref[...] = jnp.transpose(xn).astype(BF16)
        proj[...] = _nn(xn.astype(BF16), win_ref[...])
        gates_ref[...] = proj[:, GA0:GA0 + N_GATE]
        pre_ref[:, 0:VA0] = proj[:, 0:VA0]
        pre_ref[:, VA0:N_PRE] = proj[:, CQ0:KR0]
        ca, sa, cb, sb = ca_ref[...], sa_ref[...], cb_ref[...], sb_ref[...]
        lane = lax.broadcasted_iota(jnp.int32, (tm, LANE), 1)
        for h in range(A_HEADS):
            yq = _rms_fwd(proj[:, QA0 + LANE * h:QA0 + LANE * (h + 1)], gaq_ref[...], A_DIM)
            qa_ref[h] = (_rope_fwd(yq, ca, sa) * (SCALE_A * LOG2E)).astype(BF16)
        for h in range(A_KV):
            yk = _rms_fwd(proj[:, KA0 + LANE * h:KA0 + LANE * (h + 1)], gak_ref[...], A_DIM)
            ka_ref[h] = _rope_fwd(yk, ca, sa).astype(BF16)
            va_ref[h] = jnp.where(lane == A_DIM, 1.0, proj[:, VA0 + LANE * h:VA0 + LANE * (h + 1)]).astype(BF16)
        cq = _rms_fwd(proj[:, CQ0:CQ0 + B_Q_RANK], gcq_ref[...], B_Q_RANK)
        cq_ref[...] = jnp.transpose(cq).astype(BF16)
        qbpre_ref[...] = _nn(cq.astype(BF16), wuq_ref[...])
        ckv = _rms_fwd(proj[:, CKV0:CKV0 + B_KV_RANK], gckv_ref[...], B_KV_RANK)
        ckv_ref[...] = jnp.transpose(ckv).astype(BF16)
        kvb = _nn(ckv.astype(BF16), wukv_ref[...])
        kr = proj[:, KR0:KR0 + LANE]
        for h in range(B_HEADS):
            yq = _rms_fwd(qbpre_ref[:, LANE * h:LANE * (h + 1)], gbq_ref[...], B_QK)
            qb_ref[h] = (_rope_fwd(yq, cb, sb) * (SCALE_B * LOG2E)).astype(BF16)
            kp = kvb[:, LANE * h:LANE * (h + 1)] + kr
            kbpre_ref[:, LANE * h:LANE * (h + 1)] = kp
            kb_ref[h] = _rope_fwd(_rms_fwd(kp, gbk_ref[...], B_QK), cb, sb).astype(BF16)
            vb_ref[h, :, 0:LANE] = kvb[:, B_HEADS * LANE + LANE * h:B_HEADS * LANE + LANE * (h + 1)].astype(BF16)
            vb_ref[h, :, LANE:2 * LANE] = jnp.where(lane == 0, 1.0, 0.0).astype(BF16)

    row = lambda w: pl.BlockSpec((tm, w), lambda i: (i, 0))
    col = lambda w: pl.BlockSpec((w, tm), lambda i: (0, i))
    heads = lambda n: pl.BlockSpec((n, tm, LANE), lambda i: (0, i, 0))
    hs = lambda n: jax.ShapeDtypeStruct((n, s_len, LANE), BF16)
    return _pallas(
        body, name="pre", grid=(nt,),
        in_specs=[row(D_MODEL), row(LANE), row(LANE), row(LANE), row(LANE),
                  _resident(w_in_ext.shape), _resident(w_uq_pad.shape), _resident(w_ukv_ext.shape)]
                 + [_full(g.shape) for g in gains],
        out_specs=[col(D_MODEL), row(N_GATE), row(N_PRE), row(B_HEADS * LANE), row(B_HEADS * LANE),
                   col(B_Q_RANK), col(B_KV_RANK),
                   heads(A_HEADS), heads(A_KV), heads(A_KV), heads(B_HEADS), heads(B_HEADS),
                   pl.BlockSpec((B_HEADS, tm, 2 * LANE), lambda i: (0, i, 0))],
        out_shape=[jax.ShapeDtypeStruct((D_MODEL, s_len), BF16), jax.ShapeDtypeStruct((s_len, N_GATE), F32),
                   jax.ShapeDtypeStruct((s_len, N_PRE), F32), jax.ShapeDtypeStruct((s_len, B_HEADS * LANE), F32),
                   jax.ShapeDtypeStruct((s_len, B_HEADS * LANE), F32),
                   jax.ShapeDtypeStruct((B_Q_RANK, s_len), BF16), jax.ShapeDtypeStruct((B_KV_RANK, s_len), BF16),
                   hs(A_HEADS), hs(A_KV), hs(A_KV), hs(B_HEADS), hs(B_HEADS),
                   jax.ShapeDtypeStruct((B_HEADS, s_len, 2 * LANE), BF16)],
        scratch_shapes=[pltpu.VMEM((tm, N_EXT), F32)],
        compiler_params=_params(("parallel",), VMEM_LIMIT),
    )(x, *tabs, w_in_ext, w_uq_pad, w_ukv_ext, *gains)


def _attn_fwd(q, k, v, group, l_col, tq, tk, tiles, name):
    n_heads, s_len, _ = q.shape
    v_w = v.shape[2]
    nk = s_len // tk

    def body(q_ref, k_ref, v_ref, o_ref, lse_ref, s_buf, p_buf, a_buf, m_ref, acc_ref):
        def scores(g, slot):
            s_buf[slot] = _nt(q_ref[_block_rows(g // nk, tq), :], k_ref[_block_rows(g % nk, tk), :])

        def softmax(g, slot):
            t = g // nk
            s = s_buf[slot]
            m_old = m_ref[t]
            m_new = jnp.maximum(m_old, jnp.max(s, axis=-1, keepdims=True))
            m_ref[t] = m_new
            a_buf[slot] = jnp.exp2(m_old - m_new)
            p_buf[slot] = jnp.exp2(s - jnp.tile(m_new, (1, tk // LANE))).astype(BF16)

        def values(g, slot):
            t = g // nk
            pv = _nn(p_buf[slot], v_ref[_block_rows(g % nk, tk), :])
            for c in range(0, v_w, LANE):
                acc_ref[t, :, c:c + LANE] = a_buf[slot] * acc_ref[t, :, c:c + LANE] + pv[:, c:c + LANE]

        m_ref[...] = jnp.full(m_ref.shape, -1e30, F32)
        acc_ref[...] = jnp.zeros(acc_ref.shape, F32)
        _three_stage(tiles * nk, scores, softmax, values)
        for t in range(tiles):
            l = acc_ref[t, :, l_col:l_col + 1]
            o = acc_ref[t, :, 0:LANE] * (1.0 / l)
            if l_col < LANE:
                lane = lax.broadcasted_iota(jnp.int32, o.shape, 1)
                o = jnp.where(lane == l_col, 0.0, o)
            o_ref[t * tq:(t + 1) * tq, :] = o
            lse_ref[t] = jnp.transpose(m_ref[t] + jnp.log2(jnp.broadcast_to(l, (tq, LANE))))[0:1, :]

    return _pallas(
        body, name=name, grid=(n_heads, s_len // (tiles * tq)),
        in_specs=[pl.BlockSpec((None, tiles * tq, LANE), lambda h, i: (h, i, 0)),
                  pl.BlockSpec((None, s_len, LANE), lambda h, i: (h // group, 0, 0)),
                  pl.BlockSpec((None, s_len, v_w), lambda h, i: (h // group, 0, 0))],
        out_specs=[pl.BlockSpec((None, tiles * tq, LANE), lambda h, i: (h, i, 0)),
                   pl.BlockSpec((None, tiles, 1, tq), lambda h, i: (h, i, 0, 0))],
        out_shape=[jax.ShapeDtypeStruct((n_heads, s_len, LANE), F32),
                   jax.ShapeDtypeStruct((n_heads, s_len // tq, 1, tq), F32)],
        scratch_shapes=[pltpu.VMEM((2, tq, tk), F32), pltpu.VMEM((2, tq, tk), BF16), pltpu.VMEM((2, tq, LANE), F32),
                        pltpu.VMEM((tiles, tq, LANE), F32), pltpu.VMEM((tiles, tq, v_w), F32)],
        compiler_params=_params(("parallel", "parallel"), VMEM_MID),
    )(q, k, v)


def _mid(x, target, o_a, o_b, gates, w_out_ext, tm):
    s_len = x.shape[0]
    nt = s_len // tm
    n_heads = A_HEADS + B_HEADS

    def body(x_ref, t_ref, oa_ref, ob_ref, g_ref, w_ref,
             yt_ref, dh_ref, dgate_ref, doa_ref, dob_ref, delta_ref, loss_ref, silu_scr, dsilu_scr, y_ref):
        @pl.when(pl.program_id(0) == 0)
        def _():
            loss_ref[...] = jnp.zeros_like(loss_ref)

        def o_of(h):
            return oa_ref[h] if h < A_HEADS else ob_ref[h - A_HEADS]

        for h in range(n_heads):
            cols = slice(LANE * h, LANE * (h + 1))
            g = g_ref[:, cols]
            sig = 1.0 / (1.0 + jnp.exp(-g))
            silu = g * sig
            silu_scr[:, cols] = silu
            dsilu_scr[:, cols] = sig * (1.0 + g * (1.0 - sig))
            y = o_of(h) * silu
            y_ref[:, cols] = y.astype(BF16)
            yt_ref[cols, :] = jnp.transpose(y).astype(BF16)
        err =x_ref[...] + _nn(y_ref[...], w_ref[...]) - t_ref[...]
        sq = jnp.sum(jnp.sum(err * err, axis=-1, keepdims=True), axis=0, keepdims=True)
        loss_ref[...] += jnp.broadcast_to(sq * (0.5 / D_MODEL), loss_ref.shape)
        dh = err * (1.0 / D_MODEL)
        dh_ref[...] = dh
        dy = _nt(dh.astype(BF16), w_ref[...])
        lane = lax.broadcasted_iota(jnp.int32, (tm, LANE), 1)
        delta = jnp.zeros((tm, LANE), F32)
        for h in range(n_heads):
            cols = slice(LANE * h, LANE * (h + 1))
            dyh = dy[:, cols]
            oh = o_of(h)
            do = dyh * silu_scr[:, cols]
            dgate_ref[:, cols] = (dyh * oh * dsilu_scr[:, cols]).astype(BF16)
            delta = jnp.where(lane == h, jnp.sum(do * oh, axis=-1, keepdims=True), delta)
            if h < A_HEADS:
                doa_ref[h] = do.astype(BF16)
            else:
                dob_ref[h - A_HEADS] = do.astype(BF16)
        delta_ref[...] = jnp.transpose(delta)[0:DELTA_ROWS, :]

    row = lambda w: pl.BlockSpec((tm, w), lambda i: (i, 0))
    heads = lambda n, w=LANE: pl.BlockSpec((n, tm, w), lambda i: (0, i, 0))
    return _pallas(
        body, name="mid", grid=(nt,),
        in_specs=[row(D_MODEL), row(D_MODEL), heads(A_HEADS), heads(B_HEADS), row(N_GATE), _resident(w_out_ext.shape)],
        out_specs=[pl.BlockSpec((N_GATE, tm), lambda i: (0, i)), row(D_MODEL), row(N_GATE), heads(A_HEADS), heads(B_HEADS),
                   pl.BlockSpec((DELTA_ROWS, tm), lambda i: (0, i)),
                   _full((8, LANE))],
        out_shape=[jax.ShapeDtypeStruct((N_GATE, s_len), BF16), jax.ShapeDtypeStruct((s_len, D_MODEL), F32),
                   jax.ShapeDtypeStruct((s_len, N_GATE), BF16),
                   jax.ShapeDtypeStruct((A_HEADS, s_len, LANE), BF16), jax.ShapeDtypeStruct((B_HEADS, s_len, LANE), BF16),
                   jax.ShapeDtypeStruct((DELTA_ROWS, s_len), F32), jax.ShapeDtypeStruct((8, LANE), F32)],
        scratch_shapes=[pltpu.VMEM((tm, N_GATE), F32), pltpu.VMEM((tm, N_GATE), F32), pltpu.VMEM((tm, N_GATE), BF16)],
        compiler_params=_params(("arbitrary",), VMEM_LIMIT),
    )(x, target, o_a, o_b, gates, w_out_ext)


def _attn_bwd(q, k, v, do, lse, delta, group, tq, tk, tiles, name):
    n_heads, s_len, _ = q.shape
    nq = s_len // tq

    def body(q_ref, do_ref, lse_ref, delta_ref, k_ref, v_ref, dq_ref, dk_ref, dv_ref, s_buf, dp_buf, p_buf, ds_buf):
        @pl.when(pl.program_id(1) == 0)
        def _():
            dq_ref[...] = jnp.zeros_like(dq_ref)

        dk_ref[...] = jnp.zeros_like(dk_ref)
        dv_ref[...] = jnp.zeros_like(dv_ref)

        def keys(g):
            return _block_rows(g // nq, tk)

        def queries(g):
            return _block_rows(g % nq, tq)

        def scores(g, slot):
            s_buf[slot] = _nt(k_ref[keys(g), :], q_ref[queries(g), :])
            dp_buf[slot] = _nt(v_ref[keys(g), :], do_ref[queries(g), :])

        def elementwise(g, slot):
            p = jnp.exp2(s_buf[slot] - lse_ref[g % nq])
            p_buf[slot] = p.astype(BF16)
            ds_buf[slot] = (p * (dp_buf[slot] - delta_ref[g % nq])).astype(BF16)

        def grads(g, slot):
            dv_ref[keys(g), :] += _nn(p_buf[slot], do_ref[queries(g), :])
            dk_ref[keys(g), :] += _nn(ds_buf[slot], q_ref[queries(g), :])
            dq_ref[queries(g), :] += _tn(ds_buf[slot], k_ref[keys(g), :])

        _three_stage(tiles * nq, scores, elementwise, grads)

    whole = lambda: pl.BlockSpec((None, s_len, LANE), lambda h, j: (h, 0, 0))
    stat = lambda: pl.BlockSpec((None, nq, 1, tq), lambda h, j: (h, 0, 0, 0))
    kvb = lambda: pl.BlockSpec((None, tiles * tk, LANE), lambda h, j: (h // group, j, 0))
    outb = lambda: pl.BlockSpec((None, tiles * tk, LANE), lambda h, j: (h, j, 0))
    shape = jax.ShapeDtypeStruct((n_heads, s_len, LANE), F32)
    return _pallas(
        body, name=name, grid=(n_heads, s_len // (tiles * tk)),
        in_specs=[whole(), whole(), stat(), stat(), kvb(), kvb()],
        out_specs=[whole(), outb(), outb()],
        out_shape=[shape, shape, shape],
        scratch_shapes=[pltpu.VMEM((2, tk, tq), F32), pltpu.VMEM((2, tk, tq), F32),
                        pltpu.VMEM((2, tk, tq), BF16), pltpu.VMEM((2, tk, tq), BF16)],
        compiler_params=_params(("parallel", "arbitrary"), VMEM_MID),
    )(q, do, lse, delta, k, v)


def _post(x, dh, pre, qbpre, kbpre, dgate, dqa, dka, dva, dqb, dkb, dvb, loss_part, tabs,
          w_in_ext, w_uq_pad, w_ukv_ext, gains, tm):
    s_len = x.shape[0]
    nt = s_len // tm

    def body(x_ref, dh_ref, pre_ref, qbpre_ref, kbpre_ref, dgate_ref,
             dqa_ref, dka_ref, dva_ref, dqb_ref, dkb_ref, dvb_ref, loss_ref,
             ca_ref, sa_ref, cb_ref, sb_ref, win_ref, wuq_ref, wukv_ref,
             gin_ref, gaq_ref, gak_ref, gcq_ref, gckv_ref, gbq_ref, gbk_ref,
             gx_ref, dproj_ref, dqbpre_ref, dkvb_ref, dsm_ref):
        @pl.when(pl.program_id(0) == 0)
        def _():
            dsm_ref[...] = jnp.zeros_like(dsm_ref)
            dsm_ref[SM_LOSS:SM_LOSS + 1, 0:LANE] = loss_ref[0:1, :]

        def add_small(r, dg):
            dsm_ref[r:r + 1, 0:dg.shape[1]] += dg

        def tok_sum(a):
            return jnp.sum(a, axis=0, keepdims=True)

        ca, sa, cb, sb = ca_ref[...], sa_ref[...], cb_ref[...], sb_ref[...]
        lane = lax.broadcasted_iota(jnp.int32, (tm, LANE), 1)

        nope_lanes = _lanes_of(lane, LAY_NOPE)

        def back(c0, c1):
            return _nt(dproj_ref[:, c0:c1], win_ref[:, c0:c1])

        dproj_ref[:, GA0:GA0 + N_GATE] = dgate_ref[...]
        dxn = back(GA0, GA0 + N_GATE)
        dg = jnp.zeros((1, LANE), F32)
        for h in range(A_HEADS):
            dn = _rope_bwd(dqa_ref[h] * SCALE_A, ca, sa)
            dx, dgr = _rms_bwd(dn, pre_ref[:, QA0 + LANE * h:QA0 + LANE * (h + 1)], gaq_ref[...], A_DIM)
            dproj_ref[:, QA0 + LANE * h:QA0 + LANE * (h + 1)] = dx.astype(BF16)
            dg = dg + tok_sum(dgr)
        add_small(SM_AQ, _unspread_row(dg, LAY_ROPE_A))
        dxn = dxn + back(QA0, KA0)
        dg = jnp.zeros((1, LANE), F32)
        for h in range(A_KV):
            dk = dka_ref[A_GROUP * h]
            dv = dva_ref[A_GROUP * h]
            for g in range(1, A_GROUP):
                dk = dk + dka_ref[A_GROUP * h + g]
                dv = dv + dva_ref[A_GROUP * h + g]
            dn = _rope_bwd(dk * LN2, ca, sa)
            dx, dgr = _rms_bwd(dn, pre_ref[:, KA0 + LANE * h:KA0 + LANE * (h + 1)], gak_ref[...], A_DIM)
            dproj_ref[:, KA0 + LANE * h:KA0 + LANE * (h + 1)] = dx.astype(BF16)
            dproj_ref[:, VA0 + LANE * h:VA0 + LANE * (h + 1)] = dv.astype(BF16)
            dg = dg + tok_sum(dgr)
        add_small(SM_AK, _unspread_row(dg, LAY_ROPE_A))
        dxn = dxn + back(KA0, GA0)
        dg = jnp.zeros((1, LANE), F32)
        for h in range(B_HEADS):
            cols = slice(LANE * h, LANE * (h + 1))
            dn = _rope_bwd(dqb_ref[h] * SCALE_B, cb, sb)
            dx, dgr = _rms_bwd(dn, qbpre_ref[:, cols], gbq_ref[...], B_QK)
            dqbpre_ref[:, cols] = dx.astype(BF16)
            dg = dg + tok_sum(dgr)
        add_small(SM_BQ, _unspread_row(dg, LAY_ROPE_B))
        dcq = _nt(dqbpre_ref[...], wuq_ref[...])
        dx, dgr = _rms_bwd(dcq, pre_ref[:, VA0:VA0 + B_Q_RANK], gcq_ref[...], B_Q_RANK)
        dproj_ref[:, CQ0:CQ0 + B_Q_RANK] = dx.astype(BF16)
        add_small(SM_CQ, tok_sum(dgr))
        dxn = dxn + back(CQ0, CKV0)
        dg = jnp.zeros((1, LANE), F32)
        dkr = jnp.zeros((tm, LANE), F32)
        for h in range(B_HEADS):
            cols = slice(LANE * h, LANE * (h + 1))
            dn = _rope_bwd(dkb_ref[h] * LN2, cb, sb)
            dx, dgr = _rms_bwd(dn, kbpre_ref[:, cols], gbk_ref[...], B_QK)
            dkvb_ref[:, cols] = jnp.where(nope_lanes, dx, 0.0).astype(BF16)
            dkvb_ref[:, B_HEADS * LANE + LANE * h:B_HEADS * LANE + LANE * (h + 1)] = dvb_ref[h].astype(BF16)
            dkr = dkr + dx
            dg = dg + tok_sum(dgr)
        add_small(SM_BK, _unspread_row(dg, LAY_ROPE_B))
        dproj_ref[:, KR0:KR0 + LANE] = jnp.where(_lanes_of(lane, LAY_KR), dkr, 0.0).astype(BF16)
        dckv = _nt(dkvb_ref[...], wukv_ref[...])
        dx, dgr = _rms_bwd(dckv, pre_ref[:, VA0 + B_Q_RANK:N_PRE], gckv_ref[...], B_KV_RANK)
        dproj_ref[:, CKV0:CKV0 + B_KV_RANK] = dx.astype(BF16)
        add_small(SM_CKV, tok_sum(dgr))
        dxn = dxn + back(CKV0, N_EXT)
        dx, dgr = _rms_bwd(dxn, x_ref[...], gin_ref[...], D_MODEL)
        gx_ref[...] = dh_ref[...] + dx
        add_small(SM_IN, tok_sum(dgr))

    row = lambda w: pl.BlockSpec((tm, w), lambda i: (i, 0))
    heads = lambda n: pl.BlockSpec((n, tm, LANE), lambda i: (0, i, 0))
    return _pallas(
        body, name="post", grid=(nt,),
        in_specs=[row(D_MODEL), row(D_MODEL), row(N_PRE), row(B_HEADS * LANE), row(B_HEADS * LANE), row(N_GATE),
                  heads(A_HEADS), heads(A_HEADS), heads(A_HEADS), heads(B_HEADS), heads(B_HEADS), heads(B_HEADS),
                  _full(loss_part.shape), row(LANE), row(LANE), row(LANE), row(LANE),
                  _resident(w_in_ext.shape), _resident(w_uq_pad.shape), _resident(w_ukv_ext.shape)]
                 + [_full(g.shape) for g in gains],
        out_specs=[row(D_MODEL), row(N_EXT), row(B_HEADS * LANE), row(2 * B_HEADS * LANE), _full((SM_ROWS, SM_W))],
        out_shape=[jax.ShapeDtypeStruct((s_len, D_MODEL), F32), jax.ShapeDtypeStruct((s_len, N_EXT), BF16),
                   jax.ShapeDtypeStruct((s_len, B_HEADS * LANE), BF16),
                   jax.ShapeDtypeStruct((s_len, 2 * B_HEADS * LANE), BF16),
                   jax.ShapeDtypeStruct((SM_ROWS, SM_W), F32)],
        compiler_params=_params(("arbitrary",), VMEM_LIMIT),
    )(x, dh, pre, qbpre, kbpre, dgate, dqa, dka, dva, dqb, dkb, dvb, loss_part, *tabs,
      w_in_ext, w_uq_pad, w_ukv_ext, *gains)


def _grad_w(a_t, b, tn, ts, name):
    m, s_len = a_t.shape
    n = b.shape[1]

    def body(a_ref, b_ref, o_ref):
        @pl.when(pl.program_id(1) == 0)
        def _():
            o_ref[...] = jnp.zeros_like(o_ref)

        o_ref[...] += _nn(a_ref[...], b_ref[...].astype(BF16))

    return _pallas(
        body, name=name, grid=(n // tn, s_len // ts),
        in_specs=[pl.BlockSpec((m, ts), lambda j, t: (0, t)), pl.BlockSpec((ts, tn), lambda j, t: (t, j))],
        out_specs=pl.BlockSpec((m, tn), lambda j, t: (0, j)),
        out_shape=jax.ShapeDtypeStruct((m, n), F32),
        compiler_params=_params(("parallel", "arbitrary"), VMEM_MID),
    )(a_t, b)


def _adam_math(w, g, m, v):
    nm = ADAM_B1 * m + (1.0 - ADAM_B1) * g
    nv = ADAM_B2 * v + (1.0 - ADAM_B2) * (g * g)
    m_hat = nm / (1.0 - ADAM_B1 ** ADAM_STEP)
    v_hat = nv / (1.0 - ADAM_B2 ** ADAM_STEP)
    return -ADAM_LR * (m_hat / (jnp.sqrt(v_hat) + ADAM_EPS) + ADAM_WD * w), nm, nv


def _adamw_rows(w, g, m, v, tr):
    rows, cols = w.shape

    def body(w_ref, g_ref, m_ref, v_ref, d_ref, nm_ref, nv_ref):
        d_ref[...], nm_ref[...], nv_ref[...] = _adam_math(w_ref[...], g_ref[...], m_ref[...], v_ref[...])

    blk = pl.BlockSpec((tr, cols), lambda i: (i, 0))
    shape = jax.ShapeDtypeStruct((rows, cols), F32)
    return _pallas(
        body, name="adamw_w_in", grid=(rows // tr,),
        in_specs=[blk] * 4, out_specs=[blk] * 3, out_shape=[shape] * 3,
        compiler_params=_params(("parallel",), VMEM_SMALL),
    )(w, g, m, v)


def _adamw_rest(bigs, smalls, g_small):
    nb, ns = len(bigs), len(smalls)

    def body(*refs):
        ins, outs = refs[:4 * nb + 3 * ns + 1], refs[4 * nb + 3 * ns + 1:]
        for i in range(nb):
            w_ref, g_ref, m_ref, v_ref = ins[4 * i:4 * i + 4]
            d_ref, nm_ref, nv_ref = outs[3 * i:3 * i + 3]
            d_ref[...], nm_ref[...], nv_ref[...] = _adam_math(w_ref[...], g_ref[...], m_ref[...], v_ref[...])
        gs_ref = ins[-1]
        for i in range(ns):
            w_ref, m_ref, v_ref = ins[4 * nb + 3 * i:4 * nb + 3 * i + 3]
            g_ref, d_ref, nm_ref, nv_ref = outs[3 * nb + 4 * i:3 * nb + 4 * i + 4]
            g = gs_ref[i:i + 1, 0:w_ref.shape[1]]
            g_ref[...] = g
            d_ref[...], nm_ref[...], nv_ref[...] = _adam_math(w_ref[...], g, m_ref[...], v_ref[...])

    flat_in = [a for quad in bigs for a in quad] + [a for tri in smalls for a in tri] + [g_small]
    out_shape = ([jax.ShapeDtypeStruct(q[0].shape, F32) for q in bigs for _ in range(3)]
                 + [jax.ShapeDtypeStruct(t[0].shape, F32) for t in smalls for _ in range(4)])
    return _pallas(
        body, name="adamw_rest",
        in_specs=[pl.BlockSpec(memory_space=pltpu.VMEM)] * len(flat_in),
        out_specs=[pl.BlockSpec(memory_space=pltpu.VMEM)] * len(out_shape),
        out_shape=out_shape,
        compiler_params=_params(vmem=VMEM_SMALL),
    )(*flat_in)


def _place(pieces, n):
    out, at = [], 0
    for lane0, arr in sorted(pieces, key=lambda p: p[0]):
        out += [jnp.zeros((n, lane0 - at), F32), arr]
        at = lane0 + arr.shape[1]
    return jnp.concatenate(out + [jnp.zeros((n, LANE - at), F32)], axis=1)


def _rope_tables(s_len):
    rows = s_len // GRID_W
    row = jnp.arange(rows, dtype=F32)
    col = jnp.arange(GRID_W, dtype=F32)

    def lay(dim, layout, first_dim, ones):
        half = dim // 2
        inv = 1.0 / (ROPE_THETA ** (jnp.arange(0, half, 2, dtype=F32) / half))
        ang_r, ang_c = row[:, None] * inv[None, :], col[:, None] * inv[None, :]
        at = {a - first_dim: lane0 for a, _, lane0 in layout}
        q = dim // 4
        r1, r2, c1, c2 = at[0], at[q], at[2 * q], at[3 * q]
        cos_r = _place([(r1, jnp.cos(ang_r)), (r2, jnp.cos(ang_r))], rows)
        sin_r = _place([(r1, -jnp.sin(ang_r)), (r2, jnp.sin(ang_r))], rows)
        cos_c = _place([(c1, jnp.cos(ang_c)), (c2, jnp.cos(ang_c))] + [(l0, jnp.ones((GRID_W, n), F32)) for _, n, l0 in ones],
                       GRID_W)
        sin_c = _place([(c1, -jnp.sin(ang_c)), (c2, jnp.sin(ang_c))], GRID_W)
        cos = (cos_r[:, None, :] + cos_c[None, :, :]).reshape(s_len, LANE)
        sin = (sin_r[:, None, :] + sin_c[None, :, :]).reshape(s_len, LANE)
        return cos, sin

    cos_a, sin_a = lay(A_DIM, LAY_ROPE_A, 0, ())
    cos_b, sin_b = lay(B_ROPE, LAY_KR, 0, LAY_NOPE)
    return cos_a, sin_a, cos_b, sin_b


def _spread(w, n_heads, dim, axis, layout):
    w3 = w.reshape(w.shape[:axis] + (n_heads, dim) + w.shape[axis + 1:])
    out, at = [], 0

    def zeros(n):
        return jnp.zeros(w3.shape[:axis + 1] + (n,) + w3.shape[axis + 2:], w.dtype)

    for a0, n, lane0 in sorted(layout, key=lambda seg: seg[2]):
        out += [zeros(lane0 - at), lax.slice_in_dim(w3, a0, a0 + n, axis=axis + 1)]
        at = lane0 + n
    out = jnp.concatenate(out + [zeros(LANE - at)], axis=axis + 1)
    return out.reshape(w.shape[:axis] + (n_heads * LANE,) + w.shape[axis + 1:])


def _unspread(w, n_heads, dim, axis, layout):
    w3 = w.reshape(w.shape[:axis] + (n_heads, LANE) + w.shape[axis + 1:])
    parts = [lax.slice_in_dim(w3, lane0, lane0 + n, axis=axis + 1) for _, n, lane0 in sorted(layout)]
    out = parts[0] if len(parts) == 1 else jnp.concatenate(parts, axis=axis + 1)
    return out.reshape(w.shape[:axis] + (n_heads * dim,) + w.shape[axis + 1:])


def _head_cols(first, n_heads, dim, layout):
    out = np.full((n_heads * LANE,), -1, np.int32)
    for h in range(n_heads):
        for a0, n, lane0 in layout:
            out[h * LANE + lane0:h * LANE + lane0 + n] = first + h * dim + a0 + np.arange(n)
    return out


def _inverse(src, n):
    dst = np.full((n,), -1, np.int32)
    dst[src[src >= 0]] = np.nonzero(src >= 0)[0]
    return dst


def _column_maps():
    a_w, kv_w = A_HEADS * A_DIM, A_KV * A_DIM
    o_g = a_w + 2 * kv_w
    o_cq = o_g + a_w
    o_kr = o_cq + B_Q_RANK + B_KV_RANK
    src_in = np.concatenate([
        _head_cols(0, A_HEADS, A_DIM, LAY_ROPE_A), _head_cols(a_w, A_KV, A_DIM, LAY_ROPE_A),
        _head_cols(a_w + kv_w, A_KV, A_DIM, LAY_PLAIN_A), _head_cols(o_g, A_HEADS, A_DIM, LAY_PLAIN_A),
        np.arange(o_kr + B_ROPE, N_IN), np.arange(o_cq, o_kr), _head_cols(o_kr, 1, B_ROPE, LAY_KR)]).astype(np.int32)
    src_uq = _head_cols(0, B_HEADS, B_QK, LAY_ROPE_B)
    per = B_NOPE + B_V
    src_ukv = np.concatenate([_head_cols(0, B_HEADS, per, LAY_NOPE),
                              _head_cols(B_NOPE, B_HEADS, per, ((0, B_V, 0),))]).astype(np.int32)
    assert len(src_in) == N_EXT
    return src_in, src_uq, src_ukv


def _round_up(n, m):
    return (n + m - 1) // m * m


def _permute_cols(xs, maps, stacks, name):
    maps = [np.asarray(m, np.int32) for m in maps]
    n_arr = len(xs)

    def block(ref, b):
        if len(ref.shape) == 2:
            return ref.at[:, b * LANE:(b + 1) * LANE]
        per = ref.shape[2] // LANE
        return ref.at[b // per, :, (b % per) * LANE:(b % per + 1) * LANE]

    def body(*refs):
        row = lax.broadcasted_iota(jnp.int32, (LANE, LANE), 0)
        for x_ref, src_ref, o_ref, src in zip(refs[:n_arr], refs[n_arr:2 * n_arr], refs[2 * n_arr:], maps):
            for c in range(len(src) // LANE):
                want = src[c * LANE:(c + 1) * LANE]
                if want[0] >= 0 and want[0] % LANE == 0 and np.array_equal(want, want[0] + np.arange(LANE)):
                    block(o_ref, c)[...] = block(x_ref, int(want[0]) // LANE)[...]
                    continue
                acc = jnp.zeros((x_ref.shape[-2], LANE), F32)
                for kb in sorted({int(v) // LANE for v in want if v >= 0}):
                    sel = jnp.where(row + kb * LANE == src_ref[:, c * LANE:(c + 1) * LANE], 1.0, 0.0).astype(BF16)
                    part = block(x_ref, kb)[...]
                    if part.dtype == BF16:
                        acc = acc + _nn(part, sel)
                    else:
                        hi = part.astype(BF16)
                        rest = part - hi.astype(F32)
                        mid = rest.astype(BF16)
                        low = (rest - mid.astype(F32)).astype(BF16)
                        acc = acc + ((_nn(hi, sel) + _nn(mid, sel)) + _nn(low, sel))
                block(o_ref, c)[...] = acc.astype(o_ref.dtype)

    def out_shape(x, m, stack):
        rows = x.shape[-2]
        return (rows, len(m)) if stack is None else (stack, rows, len(m) // stack)

    return _pallas(
        body, name=name,
        out_shape=[jax.ShapeDtypeStruct(out_shape(x, m, st), x.dtype) for x, m, st in zip(xs, maps, stacks)],
        compiler_params=_params(vmem=VMEM_MID),
    )(*xs, *[jnp.asarray(m).reshape(1, -1) for m in maps])


def _pad_cols(w):
    return jnp.pad(w, ((0, 0), (0, _round_up(w.shape[1], LANE) - w.shape[1])))


def _in_stack(cols, width):
    cols = np.asarray(cols)
    return np.where(cols < 0, -1, cols // width * _round_up(width, LANE) + cols % width).astype(np.int32)


def _ext_weights(g_in, g_uq, g_ukv, g_out):
    src_in, src_uq, src_ukv = _column_maps()
    w_uq = g_uq.reshape(B_Q_RANK, B_HEADS * B_QK)
    w_out = g_out.reshape(D_MODEL, D_MODEL)
    w_in_ext, w_uq_pad, w_ukv_ext = _permute_cols(
        [g_in, w_uq, g_ukv], [_in_stack(src_in, SH_IN[1]), src_uq, _in_stack(src_ukv, SH_UKV[1])], [None] * 3, "lay_out_weights")
    a_w = A_HEADS * A_DIM
    w_out_ext = jnp.concatenate([_spread(w_out[:a_w], A_HEADS, A_DIM, 0, LAY_PLAIN_A), w_out[a_w:]], axis=0)
    return w_in_ext, w_uq_pad, w_ukv_ext, w_out_ext


def _fold_grads(d_in_ext, d_uq_pad, d_ukv_ext, d_out_ext):
    src_in, src_uq, src_ukv = _column_maps()

    def back(src, n, width):
        inv = _inverse(src, n)
        wide = _round_up(width, LANE)
        out = np.full((n // width * wide,), -1, np.int32)
        for j in range(n // width):
            out[j * wide:j * wide + width] = inv[j * width:(j + 1) * width]
        return out

    n_uq, n_ukv = B_HEADS * B_QK, B_HEADS * (B_NOPE + B_V)
    d_in, d_uq, d_ukv = _permute_cols(
        [d_in_ext, d_uq_pad, d_ukv_ext], [back(src_in, N_IN, SH_IN[1]), _inverse(src_uq, n_uq), back(src_ukv, n_ukv, SH_UKV[1])],
        [N_CHIPS, None, N_CHIPS], "fold_grads")
    d_out = jnp.concatenate([_unspread(d_out_ext[:A_HEADS * LANE], A_HEADS, A_DIM, 0, LAY_PLAIN_A), d_out_ext[A_HEADS * LANE:]], axis=0)
    return d_in, d_uq.reshape((N_CHIPS,) + SH_UQ), d_ukv, d_out.reshape((N_CHIPS,) + SH_OUT)


def kernel(x, norm_in, w_in, a_q_norm, a_k_norm, b_cq_norm, b_ckv_norm, w_uq, w_ukv, b_q_norm, b_k_norm, w_out, loss_target, m_norm_in, m_w_in, m_a_q_norm, m_a_k_norm, m_b_cq_norm, m_b_ckv_norm, m_w_uq, m_w_ukv, m_b_q_norm, m_b_k_norm, m_w_out, v_norm_in, v_w_in, v_a_q_norm, v_a_k_norm, v_b_cq_norm, v_b_ckv_norm, v_w_uq, v_w_ukv, v_b_q_norm, v_b_k_norm, v_w_out):
    s_len = x.shape[1]
    xs, ts = x[0], loss_target[0]
    tm = min(256, s_len)
    tq, tk_f = min(512, s_len // 2), min(1024, s_len // 2)
    tq_b, tk_b = min(1024, s_len // 2), min(512, s_len)
    tiles_f = min(4, s_len // tq)
    tiles_b = min(2, s_len // tk_b)

    w_in_ext, w_uq_pad, w_ukv_ext, w_out_ext = _ext_weights(
        *_gather_weights((_pad_cols(w_in[0]), w_uq[0], _pad_cols(w_ukv[0]), w_out[0])))
    gains = (norm_in, _spread(a_q_norm, 1, A_DIM, 1, LAY_ROPE_A), _spread(a_k_norm, 1, A_DIM, 1, LAY_ROPE_A), b_cq_norm, b_ckv_norm,
             _spread(b_q_norm, 1, B_QK, 1, LAY_ROPE_B), _spread(b_k_norm, 1, B_QK, 1, LAY_ROPE_B))
    tabs = _rope_tables(s_len)

    (xn_t, gates, pre, qbpre, kbpre, cq_t, ckv_t, qa, ka, va, qb, kb, vb) = _pre(
        xs, tabs, w_in_ext, w_uq_pad, w_ukv_ext, gains, tm)
    o_a, lse_a = _attn_fwd(qa, ka, va, A_GROUP, A_DIM, tq, tk_f, tiles_f, "attn_fwd_a")
    o_b, lse_b = _attn_fwd(qb, kb, vb, 1, B_V, tq, tk_f, tiles_f, "attn_fwd_b")
    y_t, dh, dgate, do_a, do_b, delta, loss_part = _mid(xs, ts, o_a, o_b, gates, w_out_ext, min(512, s_len))

    def stat(a):
        return a.reshape(a.shape[0], s_len // tq_b, 1, tq_b)

    dqa, dka, dva = _attn_bwd(qa, ka, va, do_a, stat(lse_a), stat(delta[:A_HEADS]), A_GROUP, tq_b, tk_b, tiles_b, "attn_bwd_a")
    dqb, dkb, dvb = _attn_bwd(qb, kb, vb, do_b, stat(lse_b), stat(delta[A_HEADS:A_HEADS + B_HEADS]), 1, tq_b, tk_b,
                              tiles_b, "attn_bwd_b")
    grad_x, dproj, dqbpre, dkvb, d_small = _post(
        xs, dh, pre, qbpre, kbpre, dgate, dqa, dka, dva, dqb, dkb, dvb, loss_part, tabs,
        w_in_ext, w_uq_pad, w_ukv_ext, gains, tm)

    ts_w = min(2048, s_len)
    d_in_ext = _grad_w(xn_t, dproj, 768, ts_w, "grad_w_in")
    d_out_ext = _grad_w(y_t, dh, 512, ts_w, "grad_w_out")
    d_uq_pad = _grad_w(cq_t, dqbpre, 512, ts_w, "grad_w_uq")
    d_ukv_ext = _grad_w(ckv_t, dkvb, 1024, ts_w, "grad_w_ukv")

    g_in_p, g_uq, g_ukv_p, g_out, g_small = _reduce_grads(_fold_grads(d_in_ext, d_uq_pad, d_ukv_ext, d_out_ext), d_small)
    g_in, g_ukv = g_in_p[:, :SH_IN[1]], g_ukv_p[:, :SH_UKV[1]]
    d_in, nm_in, nv_in = (a.T for a in _adamw_rows(w_in[0].T, g_in_p.T[:SH_IN[1]], m_w_in[0].T, v_w_in[0].T, SH_IN[1] // 7))
    rest = _adamw_rest(
        [(w_uq[0], g_uq, m_w_uq[0], v_w_uq[0]), (w_ukv[0], g_ukv, m_w_ukv[0], v_w_ukv[0]),
         (w_out[0], g_out, m_w_out[0], v_w_out[0])],
        [(norm_in, m_norm_in, v_norm_in), (a_q_norm, m_a_q_norm, v_a_q_norm), (a_k_norm, m_a_k_norm, v_a_k_norm),
         (b_cq_norm, m_b_cq_norm, v_b_cq_norm), (b_ckv_norm, m_b_ckv_norm, v_b_ckv_norm),
         (b_q_norm, m_b_q_norm, v_b_q_norm), (b_k_norm, m_b_k_norm, v_b_k_norm)], g_small)
    (d_uq, nm_uq, nv_uq), (d_ukv, nm_ukv, nv_ukv), (d_out, nm_out, nv_out) = (rest[3 * i:3 * i + 3] for i in range(3))
    sm = [rest[9 + 4 * i:9 + 4 * i + 4] for i in range(7)]

    def leaves(k, p_in, p_uq, p_ukv, p_out):
        return [sm[SM_IN][k], p_in[None], sm[SM_AQ][k], sm[SM_AK][k], sm[SM_CQ][k], sm[SM_CKV][k], p_uq[None], p_ukv[None],
                sm[SM_BQ][k], sm[SM_BK][k], p_out[None]]

    return (g_small[SM_LOSS, 0], grad_x[None], *leaves(0, g_in, g_uq, g_ukv, g_out), *leaves(1, d_in, d_uq, d_ukv, d_out),
            *leaves(2, nm_in, nm_uq, nm_ukv, nm_out), *leaves(3, nv_in, nv_uq, nv_ukv, nv_out))
```

```python
import jax
import jax.numpy as jnp
import numpy as np
from jax import lax
from jax.experimental import pallas as pl
from jax.experimental.pallas import tpu as pltpu

F32 = jnp.float32
BF16 = jnp.bfloat16
MESH = pl.DeviceIdType.MESH

D_MODEL = 1024
GRID_W = 64
ROPE_THETA = 10000.0
EPS = 1e-6
A_HEADS, A_KV, A_DIM = 8, 2, 64
A_GROUP = A_HEADS // A_KV
B_HEADS, B_NOPE, B_ROPE, B_V = 4, 64, 32, 128
B_QK = B_NOPE + B_ROPE
B_Q_RANK, B_KV_RANK = 384, 256
N_IN = 2464
SCALE_A = 1.0 / float(np.sqrt(A_DIM))
SCALE_B = 1.0 / float(np.sqrt(B_QK))
LOG2E = float(np.log2(np.e))
LN2 = float(np.log(2.0))
ADAM_LR, ADAM_B1, ADAM_B2, ADAM_EPS, ADAM_WD, ADAM_STEP = 0.001, 0.9, 0.999, 1e-08, 0.01, 10

LANE = 128
VMEM_BYTES = 64 * 1024 * 1024
VMEM_LIMIT = VMEM_BYTES - 8 * 1024 * 1024
VMEM_MID = 48 * 1024 * 1024
VMEM_SMALL = 32 * 1024 * 1024

QA0 = 0
KA0 = QA0 + A_HEADS * LANE
VA0 = KA0 + A_KV * LANE
GA0 = VA0 + A_KV * LANE
GB0 = GA0 + A_HEADS * LANE
CQ0 = GB0 + B_HEADS * LANE
CKV0 = CQ0 + B_Q_RANK
KR0 = CKV0 + B_KV_RANK
N_EXT = KR0 + LANE
N_GATE = (A_HEADS + B_HEADS) * LANE
DELTA_ROWS = 16
N_PRE = KA0 + A_KV * LANE + B_Q_RANK + B_KV_RANK

ROT = LANE // 2
_QA = A_DIM // 4
_QB = B_ROPE // 4
LAY_PLAIN_A = ((0, A_DIM, 0),)
LAY_ROPE_A = ((0, _QA, 0), (2 * _QA, _QA, _QA), (_QA, _QA, ROT), (3 * _QA, _QA, ROT + _QA))
LAY_KR = ((0, _QB, 0), (2 * _QB, _QB, _QB), (_QB, _QB, ROT), (3 * _QB, _QB, ROT + _QB))
LAY_NOPE = ((0, B_NOPE // 2, 2 * _QB), (B_NOPE // 2, B_NOPE // 2, ROT + 2 * _QB))
LAY_ROPE_B = LAY_NOPE + tuple((B_NOPE + a, n, at) for a, n, at in LAY_KR)

N_CHIPS = 4
SH_IN = (D_MODEL, N_IN // N_CHIPS)
SH_UQ = (B_Q_RANK // N_CHIPS, B_HEADS * B_QK)
SH_UKV = (B_KV_RANK, B_HEADS * (B_NOPE + B_V) // N_CHIPS)
SH_OUT = (D_MODEL // N_CHIPS, D_MODEL)
SM_ROWS, SM_W = 16, D_MODEL
SM_IN, SM_AQ, SM_AK, SM_CQ, SM_CKV, SM_BQ, SM_BK, SM_LOSS = range(8)
F32_ROWS, BF16_ROWS = 8, 16


def _pallas(body, **kw):
    return pl.pallas_call(body, **kw)


def _params(sem=None, vmem=None):
    return pltpu.CompilerParams(dimension_semantics=sem, vmem_limit_bytes=vmem)


def _rms_fwd(x, g, n):
    r = lax.rsqrt(jnp.sum(x * x, axis=-1, keepdims=True) * (1.0 / n) + EPS)
    return x * r * g


def _rms_bwd(dy, x, g, n):
    u = dy * g
    r = lax.rsqrt(jnp.sum(x * x, axis=-1, keepdims=True) * (1.0 / n) + EPS)
    ux = jnp.sum(u * x, axis=-1, keepdims=True)
    xhat = x * r
    dx = r * (u - xhat * (r * ux * (1.0 / n)))
    return dx, dy * xhat


def _rope_fwd(y, cos, sin):
    return y * cos + pltpu.roll(y, ROT, 1) * sin


def _rope_bwd(d, cos, sin):
    return d * cos - pltpu.roll(d, ROT, 1) * sin


def _token_tables(refs):
    out = []
    for r_ref, c_ref in zip(refs[0::2], refs[1::2]):
        r, c = r_ref[...], c_ref[...]
        out.append(jnp.concatenate([r[k:k + 1, :] + c for k in range(r.shape[0])], axis=0))
    return out


def _lanes_of(lane, layout):
    m = None
    for _, n, at in layout:
        seg = (lane >= at) & (lane < at + n)
        m = seg if m is None else (m | seg)
    return m


def _unspread_row(v, layout):
    v8 = jnp.broadcast_to(v, (F32_ROWS, LANE))
    lane = lax.broadcasted_iota(jnp.int32, v8.shape, 1)
    out = jnp.zeros_like(v8)
    for a, n, at in layout:
        moved = v8 if a == at else pltpu.roll(v8, (a - at) % LANE, 1)
        out = jnp.where((lane >= a) & (lane < a + n), moved, out)
    return out[0:1, :]


def _nt(a, b):
    return lax.dot_general(a, b, (((1,), (1,)), ((), ())), preferred_element_type=F32)


def _tn(a, b):
    return lax.dot_general(a, b, (((0,), (0,)), ((), ())), preferred_element_type=F32)


def _nn(a, b):
    return jnp.dot(a, b, preferred_element_type=F32)


def _block_rows(i, size):
    if isinstance(i, int):
        return pl.ds(i * size, size)
    return pl.ds(pl.multiple_of(i * size, size), size)


MAX_STATIC_BLOCKS = 32


def _three_stage(n, first, second, third):
    assert n >= 2 and n % 2 == 0
    first(0, 0)
    first(1, 1)
    second(0, 0)
    if n <= MAX_STATIC_BLOCKS:
        for i in range(1, n - 1):
            first(i + 1, (i + 1) % 2)
            second(i, i % 2)
            third(i - 1, (i - 1) % 2)
    else:
        def pair(t, carry):
            i = 2 * t + 1
            first(i + 1, 0)
            second(i, 1)
            third(i - 1, 0)
            first(i + 2, 1)
            second(i + 1, 0)
            third(i, 1)
            return carry

        lax.fori_loop(0, (n - 2) // 2, pair, 0)
    second(n - 1, 1)
    third(n - 2, 0)
    third(n - 1, 1)


def _full(shape):
    return pl.BlockSpec(shape, lambda *_: (0,) * len(shape))


def _table_specs(tm):
    return [pl.BlockSpec((None, tm // GRID_W, LANE), lambda i: (i, 0, 0)), _full((GRID_W, LANE))] * 4


def _resident(shape):
    return pl.BlockSpec(shape, lambda *_: (0,) * len(shape), pipeline_mode=pl.Buffered(1))


def _gather_weights(shards):
    n = len(shards)
    halves = [w.shape[0] // 2 for w in shards]

    def body(*refs):
        w_refs, out_refs, (send_sems, recv_sems) = refs[:n], refs[n:2 * n], refs[2 * n:]
        x, y, c = lax.axis_index("x"), lax.axis_index("y"), lax.axis_index("c")
        sibling = (x, y, 1 - c)
        chips = [(1 - x, y), (x, 1 - y), (1 - x, 1 - y)]
        me = 2 * x + y

        def copy(a, k, j, hc, to):
            part = out_refs[a].at[j, pl.ds(pl.multiple_of(hc * halves[a], BF16_ROWS), halves[a]), :]
            return pltpu.make_async_remote_copy(
                src_ref=part, dst_ref=part, send_sem=send_sems.at[6 * a + k], recv_sem=recv_sems.at[6 * a + k],
                device_id=to, device_id_type=MESH)

        started = []
        for a in range(n):
            out_refs[a][me] = w_refs[a][...].astype(BF16)
            for k, chip in enumerate(chips):
                started.append(copy(a, k, me, c, (*chip, c)))
                started[-1].start()
        for k, chip in enumerate(chips):
            for a in range(n):
                copy(a, k, 2 * chip[0] + chip[1], c, (*chip, c)).wait_recv()
                started.append(copy(a, 3 + k, 2 * chip[0] + chip[1], c, sibling))
                started[-1].start()
        for k, chip in enumerate(chips):
            for a in range(n):
                copy(a, 3 + k, 2 * chip[0] + chip[1], 1 - c, sibling).wait_recv()
        for cp in started:
            cp.wait_send()

    return _pallas(
        body, name="gather_weights",
        out_shape=[jax.ShapeDtypeStruct((N_CHIPS,) + w.shape, BF16) for w in shards],
        in_specs=[pl.BlockSpec(memory_space=pltpu.VMEM)] * n,
        out_specs=[pl.BlockSpec(memory_space=pltpu.VMEM)] * n,
        scratch_shapes=[pltpu.SemaphoreType.DMA((6 * n,)), pltpu.SemaphoreType.DMA((6 * n,))],
        compiler_params=_params(vmem=VMEM_SMALL),
    )(*shards)


def _reduce_grads(parts, small):
    n_big = len(parts)
    n = n_big + 1
    shapes = [p.shape[1:] for p in parts] + [small.shape]
    halves = [sh[0] // 2 for sh in shapes]

    def body(*refs):
        p_refs, out_refs, rec_a, rec_b = refs[:n], refs[n:2 * n], refs[2 * n:3 * n], refs[3 * n:4 * n]
        send_b = refs[4 * n:4 * n + n_big]
        sa_send, sa_recv, sb_send, sb_recv, sc_send, sc_recv = refs[4 * n + n_big:]
        x, y, c = lax.axis_index("x"), lax.axis_index("y"), lax.axis_index("c")
        sibling = (x, y, 1 - c)
        me = 2 * x + y

        def rows(a, hc):
            return pl.ds(pl.multiple_of(hc * halves[a], F32_ROWS), halves[a])

        def partial(a, j, hc):
            return p_refs[a].at[j, rows(a, hc), :] if a < n_big else p_refs[a].at[rows(a, hc), :]

        def copy_a(a, j):
            return pltpu.make_async_remote_copy(
                src_ref=partial(a, j, 1 - c), dst_ref=rec_a[a].at[j],
                send_sem=sa_send.at[N_CHIPS * a + j], recv_sem=sa_recv.at[N_CHIPS * a + j],
                device_id=sibling, device_id_type=MESH)

        def copy_b(a, r):
            j = me ^ r
            k = (N_CHIPS - 1) * a + r - 1
            return pltpu.make_async_remote_copy(
                src_ref=(send_b[a] if a < n_big else rec_a[a]).at[j], dst_ref=rec_b[a].at[r],
                send_sem=sb_send.at[k], recv_sem=sb_recv.at[k], device_id=(j // 2, j % 2, c), device_id_type=MESH)

        def copy_c(a):
            return pltpu.make_async_remote_copy(
                src_ref=out_refs[a].at[rows(a, c), :], dst_ref=out_refs[a].at[rows(a, c), :],
                send_sem=sc_send.at[a], recv_sem=sc_recv.at[a], device_id=sibling, device_id_type=MESH)

        for a in range(n):
            for j in range(N_CHIPS):
                copy_a(a, j).start()
        for r in range(1, N_CHIPS):
            j = me ^ r
            for a in range(n):
                copy_a(a, j).wait_recv()
                chip_part = rec_a[a][j] + partial(a, j, c)[...]
                if a < n_big:
                    send_b[a][j] = chip_part.astype(BF16)
                else:
                    rec_a[a][j] = chip_part
                copy_b(a, r).start()
        for a in range(n):
            copy_a(a, me).wait_recv()
            rec_b[a][0] = (rec_a[a][me] + partial(a, me, c)[...]).astype(rec_b[a].dtype)
        for a in range(n):
            for r in range(1, N_CHIPS):
                copy_b(a, r).wait_recv()
            total = rec_b[a][me].astype(F32)
            for j in range(1, N_CHIPS):
                total = total + rec_b[a][j ^ me].astype(F32)
            out_refs[a][rows(a, c), :] = total
            copy_c(a).start()
        for a in range(n):
            copy_c(a).wait_recv()
        for a in range(n):
            for j in range(N_CHIPS):
                copy_a(a, j).wait_send()
            for r in range(1, N_CHIPS):
                copy_b(a, r).wait_send()
            copy_c(a).wait_send()

    dma = pltpu.SemaphoreType.DMA
    return _pallas(
        body, name="reduce_grads",
        out_shape=[jax.ShapeDtypeStruct(sh, F32) for sh in shapes],
        in_specs=[pl.BlockSpec(memory_space=pltpu.VMEM)] * n,
        out_specs=[pl.BlockSpec(memory_space=pltpu.VMEM)] * n,
        scratch_shapes=[pltpu.VMEM((N_CHIPS, h) + sh[1:], F32) for h, sh in zip(halves, shapes)]
                       + [pltpu.VMEM((N_CHIPS, h) + sh[1:], BF16 if a < n_big else F32)
                          for a, (h, sh) in enumerate(zip(halves, shapes))]
                       + [pltpu.VMEM((N_CHIPS, h) + sh[1:], BF16) for h, sh in zip(halves[:n_big], shapes[:n_big])]
                       + [dma((N_CHIPS * n,)), dma((N_CHIPS * n,)), dma(((N_CHIPS - 1) * n,)), dma(((N_CHIPS - 1) * n,)),
                          dma((n,)), dma((n,))],
        compiler_params=_params(vmem=VMEM_LIMIT),
    )(*parts, small)


def _pre(x, tabs, w_in_ext, w_uq_pad, w_ukv_ext, gains, tm):
    s_len = x.shape[0]
    nt = s_len // tm

    def body(x_ref, car_ref, cac_ref, sar_ref, sac_ref, cbr_ref, cbc_ref, sbr_ref, sbc_ref, win_ref, wuq_ref, wukv_ref,
             gin_ref, gaq_ref, gak_ref, gcq_ref, gckv_ref, gbq_ref, gbk_ref,
             xn_ref, gates_ref, pre_ref, qbpre_ref, kbpre_ref, cq_ref, ckv_ref,
             qa_ref, ka_ref, va_ref, qb_ref, kb_ref, vb_ref, proj):
        xn = _rms_fwd(x_ref[...], gin_ref[...], D_MODEL)
        xn_ref[...] = jnp.transpose(xn).astype(BF16)
        proj[...] = _nn(xn.astype(BF16), win_ref[...])
        gates_ref[...] = proj[:, GA0:GA0 + N_GATE]
        pre_ref[:, 0:VA0] = proj[:, 0:VA0]
        pre_ref[:, VA0:N_PRE] = proj[:, CQ0:KR0]
        ca, sa, cb, sb = _token_tables((car_ref, cac_ref, sar_ref, sac_ref, cbr_ref, cbc_ref, sbr_ref, sbc_ref))
        lane = lax.broadcasted_iota(jnp.int32, (tm, LANE), 1)
        for h in range(A_HEADS):
            yq = _rms_fwd(proj[:, QA0 + LANE * h:QA0 + LANE * (h + 1)], gaq_ref[...], A_DIM)
            qa_ref[h] = (_rope_fwd(yq, ca, sa) * (SCALE_A * LOG2E)).astype(BF16)
        for h in range(A_KV):
            yk = _rms_fwd(proj[:, KA0 + LANE * h:KA0 + LANE * (h + 1)], gak_ref[...], A_DIM)
            ka_ref[h] = _rope_fwd(yk, ca, sa).astype(BF16)
            va_ref[h] = jnp.where(lane == A_DIM, 1.0, proj[:, VA0 + LANE * h:VA0 + LANE * (h + 1)]).astype(BF16)
        cq = _rms_fwd(proj[:, CQ0:CQ0 + B_Q_RANK], gcq_ref[...], B_Q_RANK)
        cq_ref[...] = jnp.transpose(cq).astype(BF16)
        qbpre_ref[...] = _nn(cq.astype(BF16), wuq_ref[...])
        ckv = _rms_fwd(proj[:, CKV0:CKV0 + B_KV_RANK], gckv_ref[...], B_KV_RANK)
        ckv_ref[...] = jnp.transpose(ckv).astype(BF16)
        kvb = _nn(ckv.astype(BF16), wukv_ref[...])
        kr = proj[:, KR0:KR0 + LANE]
        for h in range(B_HEADS):
            yq = _rms_fwd(qbpre_ref[:, LANE * h:LANE * (h + 1)], gbq_ref[...], B_QK)
            qb_ref[h] = (_rope_fwd(yq, cb, sb) * (SCALE_B * LOG2E)).astype(BF16)
            kp = kvb[:, LANE * h:LANE * (h + 1)] + kr
            kbpre_ref[:, LANE * h:LANE * (h + 1)] = kp
            kb_ref[h] = _rope_fwd(_rms_fwd(kp, gbk_ref[...], B_QK), cb, sb).astype(BF16)
            vb_ref[h, :, 0:LANE] = kvb[:, B_HEADS * LANE + LANE * h:B_HEADS * LANE + LANE * (h + 1)].astype(BF16)
            vb_ref[h, :, LANE:2 * LANE] = jnp.where(lane == 0, 1.0, 0.0).astype(BF16)

    row = lambda w: pl.BlockSpec((tm, w), lambda i: (i, 0))
    col = lambda w: pl.BlockSpec((w, tm), lambda i: (0, i))
    heads = lambda n: pl.BlockSpec((n, tm, LANE), lambda i: (0, i, 0))
    hs = lambda n: jax.ShapeDtypeStruct((n, s_len, LANE), BF16)
    return _pallas(
        body, name="pre", grid=(nt,),
        in_specs=[row(D_MODEL)] + _table_specs(tm)
                 + [_resident(w_in_ext.shape), _resident(w_uq_pad.shape), _resident(w_ukv_ext.shape)]
                 + [_full(g.shape) for g in gains],
        out_specs=[col(D_MODEL), row(N_GATE), row(N_PRE), row(B_HEADS * LANE), row(B_HEADS * LANE),
                   col(B_Q_RANK), col(B_KV_RANK),
                   heads(A_HEADS), heads(A_KV), heads(A_KV), heads(B_HEADS), heads(B_HEADS),
                   pl.BlockSpec((B_HEADS, tm, 2 * LANE), lambda i: (0, i, 0))],
        out_shape=[jax.ShapeDtypeStruct((D_MODEL, s_len), BF16), jax.ShapeDtypeStruct((s_len, N_GATE), F32),
                   jax.ShapeDtypeStruct((s_len, N_PRE), F32), jax.ShapeDtypeStruct((s_len, B_HEADS * LANE), F32),
                   jax.ShapeDtypeStruct((s_len, B_HEADS * LANE), F32),
                   jax.ShapeDtypeStruct((B_Q_RANK, s_len), BF16), jax.ShapeDtypeStruct((B_KV_RANK, s_len), BF16),
                   hs(A_HEADS), hs(A_KV), hs(A_KV), hs(B_HEADS), hs(B_HEADS),
                   jax.ShapeDtypeStruct((B_HEADS, s_len, 2 * LANE), BF16)],
        scratch_shapes=[pltpu.VMEM((tm, N_EXT), F32)],
        compiler_params=_params(("parallel",), VMEM_LIMIT),
    )(x, *tabs, w_in_ext, w_uq_pad, w_ukv_ext, *gains)


def _attn_fwd(q, k, v, group, l_col, tq, tk, tiles, name):
    n_heads, s_len, _ = q.shape
    v_w = v.shape[2]
    nk = s_len // tk

    def body(q_ref, k_ref, v_ref, o_ref, lse_ref, s_buf, p_buf, a_buf, m_ref, acc_ref):
        def scores(g, slot):
            s_buf[slot] = _nt(q_ref[_block_rows(g // nk, tq), :], k_ref[_block_rows(g % nk, tk), :])

        def softmax(g, slot):
            t = g // nk
            s = s_buf[slot]
            m_old = m_ref[t]
            m_new = jnp.maximum(m_old, jnp.max(s, axis=-1, keepdims=True))
            m_ref[t] = m_new
            a_buf[slot] = jnp.exp2(m_old - m_new)
            p_buf[slot] = jnp.exp2(s - jnp.tile(m_new, (1, tk // LANE))).astype(BF16)

        def values(g, slot):
            t = g // nk
            pv = _nn(p_buf[slot], v_ref[_block_rows(g % nk, tk), :])
            for c in range(0, v_w, LANE):
                acc_ref[t, :, c:c + LANE] = a_buf[slot] * acc_ref[t, :, c:c + LANE] + pv[:, c:c + LANE]

        m_ref[...] = jnp.full(m_ref.shape, -1e30, F32)
        acc_ref[...] = jnp.zeros(acc_ref.shape, F32)
        _three_stage(tiles * nk, scores, softmax, values)
        for t in range(tiles):
            l = acc_ref[t, :, l_col:l_col + 1]
            o = acc_ref[t, :, 0:LANE] * (1.0 / l)
            if l_col < LANE:
                lane = lax.broadcasted_iota(jnp.int32, o.shape, 1)
                o = jnp.where(lane == l_col, 0.0, o)
            o_ref[t * tq:(t + 1) * tq, :] = o
            lse_ref[t] = jnp.transpose(m_ref[t] + jnp.log2(jnp.broadcast_to(l, (tq, LANE))))[0:1, :]

    return _pallas(
        body, name=name, grid=(n_heads, s_len // (tiles * tq)),
        in_specs=[pl.BlockSpec((None, tiles * tq, LANE), lambda h, i: (h, i, 0)),
                  pl.BlockSpec((None, s_len, LANE), lambda h, i: (h // group, 0, 0)),
                  pl.BlockSpec((None, s_len, v_w), lambda h, i: (h // group, 0, 0))],
        out_specs=[pl.BlockSpec((None, tiles * tq, LANE), lambda h, i: (h, i, 0)),
                   pl.BlockSpec((None, tiles, 1, tq), lambda h, i: (h, i, 0, 0))],
        out_shape=[jax.ShapeDtypeStruct((n_heads, s_len, LANE), F32),
                   jax.ShapeDtypeStruct((n_heads, s_len // tq, 1, tq), F32)],
        scratch_shapes=[pltpu.VMEM((2, tq, tk), F32), pltpu.VMEM((2, tq, tk), BF16), pltpu.VMEM((2, tq, LANE), F32),
                        pltpu.VMEM((tiles, tq, LANE), F32), pltpu.VMEM((tiles, tq, v_w), F32)],
        compiler_params=_params(("parallel", "parallel"), VMEM_MID),
    )(q, k, v)


def _mid(x, target, o_a, o_b, gates, w_out_ext, tm):
    s_len = x.shape[0]
    nt = s_len // tm
    n_heads = A_HEADS + B_HEADS

    def body(x_ref, t_ref, oa_ref, ob_ref, g_ref, w_ref,
             yt_ref, dh_ref, dgate_ref, doa_ref, dob_ref, delta_ref, loss_ref, silu_scr, dsilu_scr, y_ref):
        @pl.when(pl.program_id(0) == 0)
        def _():
            loss_ref[...] = jnp.zeros_like(loss_ref)

        def o_of(h):
            return oa_ref[h] if h < A_HEADS else ob_ref[h - A_HEADS]

        for h in range(n_heads):
            cols = slice(LANE * h, LANE * (h + 1))
            g = g_ref[:, cols]
            sig = 1.0 / (1.0 + jnp.exp(-g))
            silu = g * sig
            silu_scr[:, cols] = silu
            dsilu_scr[:, cols] = sig * (1.0 + g * (1.0 - sig))
            y = o_of(h) * silu
            y_ref[:, cols] = y.astype(BF16)
            yt_ref[cols, :] = jnp.transpose(y).astype(BF16)
        err =x_ref[...] + _nn(y_ref[...], w_ref[...]) - t_ref[...]
        sq = jnp.sum(jnp.sum(err * err, axis=-1, keepdims=True), axis=0, keepdims=True)
        loss_ref[...] += jnp.broadcast_to(sq * (0.5 / D_MODEL), loss_ref.shape)
        dh = err * (1.0 / D_MODEL)
        dh_ref[...] = dh
        dy = _nt(dh.astype(BF16), w_ref[...])
        lane = lax.broadcasted_iota(jnp.int32, (tm, LANE), 1)
        delta = jnp.zeros((tm, LANE), F32)
        for h in range(n_heads):
            cols = slice(LANE * h, LANE * (h + 1))
            dyh = dy[:, cols]
            oh = o_of(h)
            do = dyh * silu_scr[:, cols]
            dgate_ref[:, cols] = (dyh * oh * dsilu_scr[:, cols]).astype(BF16)
            delta = jnp.where(lane == h, jnp.sum(do * oh, axis=-1, keepdims=True), delta)
            if h < A_HEADS:
                doa_ref[h] = do.astype(BF16)
            else:
                dob_ref[h - A_HEADS] = do.astype(BF16)
        delta_ref[...] = jnp.transpose(delta)[0:DELTA_ROWS, :]

    row = lambda w: pl.BlockSpec((tm, w), lambda i: (i, 0))
    heads = lambda n, w=LANE: pl.BlockSpec((n, tm, w), lambda i: (0, i, 0))
    return _pallas(
        body, name="mid", grid=(nt,),
        in_specs=[row(D_MODEL), row(D_MODEL), heads(A_HEADS), heads(B_HEADS), row(N_GATE), _resident(w_out_ext.shape)],
        out_specs=[pl.BlockSpec((N_GATE, tm), lambda i: (0, i)), row(D_MODEL), row(N_GATE), heads(A_HEADS), heads(B_HEADS),
                   pl.BlockSpec((DELTA_ROWS, tm), lambda i: (0, i)),
                   _full((8, LANE))],
        out_shape=[jax.ShapeDtypeStruct((N_GATE, s_len), BF16), jax.ShapeDtypeStruct((s_len, D_MODEL), F32),
                   jax.ShapeDtypeStruct((s_len, N_GATE), BF16),
                   jax.ShapeDtypeStruct((A_HEADS, s_len, LANE), BF16), jax.ShapeDtypeStruct((B_HEADS, s_len, LANE), BF16),
                   jax.ShapeDtypeStruct((DELTA_ROWS, s_len), F32), jax.ShapeDtypeStruct((8, LANE), F32)],
        scratch_shapes=[pltpu.VMEM((tm, N_GATE), F32), pltpu.VMEM((tm, N_GATE), F32), pltpu.VMEM((tm, N_GATE), BF16)],
        compiler_params=_params(("arbitrary",), VMEM_LIMIT),
    )(x, target, o_a, o_b, gates, w_out_ext)


def _attn_bwd(q, k, v, do, lse, delta, group, tq, tk, tiles, name):
    n_heads, s_len, _ = q.shape
    nq = s_len // tq

    def body(q_ref, do_ref, lse_ref, delta_ref, k_ref, v_ref, dq_ref, dk_ref, dv_ref, s_buf, dp_buf, p_buf, ds_buf):
        @pl.when(pl.program_id(1) == 0)
        def _():
            dq_ref[...] = jnp.zeros_like(dq_ref)

        dk_ref[...] = jnp.zeros_like(dk_ref)
        dv_ref[...] = jnp.zeros_like(dv_ref)

        def keys(g):
            return _block_rows(g // nq, tk)

        def queries(g):
            return _block_rows(g % nq, tq)

        def scores(g, slot):
            s_buf[slot] = _nt(k_ref[keys(g), :], q_ref[queries(g), :])
            dp_buf[slot] = _nt(v_ref[keys(g), :], do_ref[queries(g), :])

        def elementwise(g, slot):
            p = jnp.exp2(s_buf[slot] - lse_ref[g % nq])
            p_buf[slot] = p.astype(BF16)
            ds_buf[slot] = (p * (dp_buf[slot] - delta_ref[g % nq])).astype(BF16)

        def grads(g, slot):
            dv_ref[keys(g), :] += _nn(p_buf[slot], do_ref[queries(g), :])
            dk_ref[keys(g), :] += _nn(ds_buf[slot], q_ref[queries(g), :])
            dq_ref[queries(g), :] += _tn(ds_buf[slot], k_ref[keys(g), :])

        _three_stage(tiles * nq, scores, elementwise, grads)

    whole = lambda: pl.BlockSpec((None, s_len, LANE), lambda h, j: (h, 0, 0))
    stat = lambda: pl.BlockSpec((None, nq, 1, tq), lambda h, j: (h, 0, 0, 0))
    kvb = lambda: pl.BlockSpec((None, tiles * tk, LANE), lambda h, j: (h // group, j, 0))
    outb = lambda: pl.BlockSpec((None, tiles * tk, LANE), lambda h, j: (h, j, 0))
    shape = jax.ShapeDtypeStruct((n_heads, s_len, LANE), F32)
    return _pallas(
        body, name=name, grid=(n_heads, s_len // (tiles * tk)),
        in_specs=[whole(), whole(), stat(), stat(), kvb(), kvb()],
        out_specs=[whole(), outb(), outb()],
        out_shape=[shape, shape, shape],
        scratch_shapes=[pltpu.VMEM((2, tk, tq), F32), pltpu.VMEM((2, tk, tq), F32),
                        pltpu.VMEM((2, tk, tq), BF16), pltpu.VMEM((2, tk, tq), BF16)],
        compiler_params=_params(("parallel", "arbitrary"), VMEM_MID),
    )(q, do, lse, delta, k, v)


def _post(x, dh, pre, qbpre, kbpre, dgate, dqa, dka, dva, dqb, dkb, dvb, loss_part, tabs,
          w_in_ext, w_uq_pad, w_ukv_ext, gains, tm):
    s_len = x.shape[0]
    nt = s_len // tm

    def body(x_ref, dh_ref, pre_ref, qbpre_ref, kbpre_ref, dgate_ref,
             dqa_ref, dka_ref, dva_ref, dqb_ref, dkb_ref, dvb_ref, loss_ref,
             car_ref, cac_ref, sar_ref, sac_ref, cbr_ref, cbc_ref, sbr_ref, sbc_ref, win_ref, wuq_ref, wukv_ref,
             gin_ref, gaq_ref, gak_ref, gcq_ref, gckv_ref, gbq_ref, gbk_ref,
             gx_ref, dproj_ref, dqbpre_ref, dkvb_ref, dsm_ref):
        @pl.when(pl.program_id(0) == 0)
        def _():
            dsm_ref[...] = jnp.zeros_like(dsm_ref)
            dsm_ref[SM_LOSS:SM_LOSS + 1, 0:LANE] = loss_ref[0:1, :]

        def add_small(r, dg):
            dsm_ref[r:r + 1, 0:dg.shape[1]] += dg

        def tok_sum(a):
            return jnp.sum(a, axis=0, keepdims=True)

        ca, sa, cb, sb = _token_tables((car_ref, cac_ref, sar_ref, sac_ref, cbr_ref, cbc_ref, sbr_ref, sbc_ref))
        lane = lax.broadcasted_iota(jnp.int32, (tm, LANE), 1)

        nope_lanes = _lanes_of(lane, LAY_NOPE)

        def back(c0, c1):
            return _nt(dproj_ref[:, c0:c1], win_ref[:, c0:c1])

        dproj_ref[:, GA0:GA0 + N_GATE] = dgate_ref[...]
        dxn = back(GA0, GA0 + N_GATE)
        dg = jnp.zeros((1, LANE), F32)
        for h in range(A_HEADS):
            dn = _rope_bwd(dqa_ref[h] * SCALE_A, ca, sa)
            dx, dgr = _rms_bwd(dn, pre_ref[:, QA0 + LANE * h:QA0 + LANE * (h + 1)], gaq_ref[...], A_DIM)
            dproj_ref[:, QA0 + LANE * h:QA0 + LANE * (h + 1)] = dx.astype(BF16)
            dg = dg + tok_sum(dgr)
        add_small(SM_AQ, _unspread_row(dg, LAY_ROPE_A))
        dxn = dxn + back(QA0, KA0)
        dg = jnp.zeros((1, LANE), F32)
        for h in range(A_KV):
            dk = dka_ref[A_GROUP * h]
            dv = dva_ref[A_GROUP * h]
            for g in range(1, A_GROUP):
                dk = dk + dka_ref[A_GROUP * h + g]
                dv = dv + dva_ref[A_GROUP * h + g]
            dn = _rope_bwd(dk * LN2, ca, sa)
            dx, dgr = _rms_bwd(dn, pre_ref[:, KA0 + LANE * h:KA0 + LANE * (h + 1)], gak_ref[...], A_DIM)
            dproj_ref[:, KA0 + LANE * h:KA0 + LANE * (h + 1)] = dx.astype(BF16)
            dproj_ref[:, VA0 + LANE * h:VA0 + LANE * (h + 1)] = dv.astype(BF16)
            dg = dg + tok_sum(dgr)
        add_small(SM_AK, _unspread_row(dg, LAY_ROPE_A))
        dxn = dxn + back(KA0, GA0)
        dg = jnp.zeros((1, LANE), F32)
        for h in range(B_HEADS):
            cols = slice(LANE * h, LANE * (h + 1))
            dn = _rope_bwd(dqb_ref[h] * SCALE_B, cb, sb)
            dx, dgr = _rms_bwd(dn, qbpre_ref[:, cols], gbq_ref[...], B_QK)
            dqbpre_ref[:, cols] = dx.astype(BF16)
            dg = dg + tok_sum(dgr)
        add_small(SM_BQ, _unspread_row(dg, LAY_ROPE_B))
        dcq = _nt(dqbpre_ref[...], wuq_ref[...])
        dx, dgr = _rms_bwd(dcq, pre_ref[:, VA0:VA0 + B_Q_RANK], gcq_ref[...], B_Q_RANK)
        dproj_ref[:, CQ0:CQ0 + B_Q_RANK] = dx.astype(BF16)
        add_small(SM_CQ, tok_sum(dgr))
        dxn = dxn + back(CQ0, CKV0)
        dg = jnp.zeros((1, LANE), F32)
        dkr = jnp.zeros((tm, LANE), F32)
        for h in range(B_HEADS):
            cols = slice(LANE * h, LANE * (h + 1))
            dn = _rope_bwd(dkb_ref[h] * LN2, cb, sb)
            dx, dgr = _rms_bwd(dn, kbpre_ref[:, cols], gbk_ref[...], B_QK)
            dkvb_ref[:, cols] = jnp.where(nope_lanes, dx, 0.0).astype(BF16)
            dkvb_ref[:, B_HEADS * LANE + LANE * h:B_HEADS * LANE + LANE * (h + 1)] = dvb_ref[h].astype(BF16)
            dkr = dkr + dx
            dg = dg + tok_sum(dgr)
        add_small(SM_BK, _unspread_row(dg, LAY_ROPE_B))
        dproj_ref[:, KR0:KR0 + LANE] = jnp.where(_lanes_of(lane, LAY_KR), dkr, 0.0).astype(BF16)
        dckv = _nt(dkvb_ref[...], wukv_ref[...])
        dx, dgr = _rms_bwd(dckv, pre_ref[:, VA0 + B_Q_RANK:N_PRE], gckv_ref[...], B_KV_RANK)
        dproj_ref[:, CKV0:CKV0 + B_KV_RANK] = dx.astype(BF16)
        add_small(SM_CKV, tok_sum(dgr))
        dxn = dxn + back(CKV0, N_EXT)
        dx, dgr = _rms_bwd(dxn, x_ref[...], gin_ref[...], D_MODEL)
        gx_ref[...] = dh_ref[...] + dx
        add_small(SM_IN, tok_sum(dgr))

    row = lambda w: pl.BlockSpec((tm, w), lambda i: (i, 0))
    heads = lambda n: pl.BlockSpec((n, tm, LANE), lambda i: (0, i, 0))
    return _pallas(
        body, name="post", grid=(nt,),
        in_specs=[row(D_MODEL), row(D_MODEL), row(N_PRE), row(B_HEADS * LANE), row(B_HEADS * LANE), row(N_GATE),
                  heads(A_HEADS), heads(A_HEADS), heads(A_HEADS), heads(B_HEADS), heads(B_HEADS), heads(B_HEADS),
                  _full(loss_part.shape)] + _table_specs(tm)
                 + [_resident(w_in_ext.shape), _resident(w_uq_pad.shape), _resident(w_ukv_ext.shape)]
                 + [_full(g.shape) for g in gains],
        out_specs=[row(D_MODEL), row(N_EXT), row(B_HEADS * LANE), row(2 * B_HEADS * LANE), _full((SM_ROWS, SM_W))],
        out_shape=[jax.ShapeDtypeStruct((s_len, D_MODEL), F32), jax.ShapeDtypeStruct((s_len, N_EXT), BF16),
                   jax.ShapeDtypeStruct((s_len, B_HEADS * LANE), BF16),
                   jax.ShapeDtypeStruct((s_len, 2 * B_HEADS * LANE), BF16),
                   jax.ShapeDtypeStruct((SM_ROWS, SM_W), F32)],
        compiler_params=_params(("arbitrary",), VMEM_LIMIT),
    )(x, dh, pre, qbpre, kbpre, dgate, dqa, dka, dva, dqb, dkb, dvb, loss_part, *tabs,
      w_in_ext, w_uq_pad, w_ukv_ext, *gains)


def _grad_w(a_t, b, tn, ts, name):
    m, s_len = a_t.shape
    n = b.shape[1]

    def body(a_ref, b_ref, o_ref):
        @pl.when(pl.program_id(1) == 0)
        def _():
            o_ref[...] = jnp.zeros_like(o_ref)

        o_ref[...] += _nn(a_ref[...], b_ref[...].astype(BF16))

    return _pallas(
        body, name=name, grid=(n // tn, s_len // ts),
        in_specs=[pl.BlockSpec((m, ts), lambda j, t: (0, t)), pl.BlockSpec((ts, tn), lambda j, t: (t, j))],
        out_specs=pl.BlockSpec((m, tn), lambda j, t: (0, j)),
        out_shape=jax.ShapeDtypeStruct((m, n), F32),
        compiler_params=_params(("parallel", "arbitrary"), VMEM_MID),
    )(a_t, b)


def _grad_w_pairs(pairs, ts, name):
    s_len = pairs[0][0].shape[1]
    n_p = len(pairs)

    def body(*refs):
        for a_ref, b_ref, o_ref in zip(refs[0:2 * n_p:2], refs[1:2 * n_p:2], refs[2 * n_p:]):
            @pl.when(pl.program_id(0) == 0)
            def _():
                o_ref[...] = jnp.zeros_like(o_ref)

            o_ref[...] += _nn(a_ref[...], b_ref[...].astype(BF16))

    in_specs, flat = [], []
    for a_t, b in pairs:
        in_specs += [pl.BlockSpec((a_t.shape[0], ts), lambda t: (0, t)), pl.BlockSpec((ts, b.shape[1]), lambda t: (t, 0))]
        flat += [a_t, b]
    return _pallas(
        body, name=name, grid=(s_len // ts,),
        in_specs=in_specs,
        out_specs=[_full((a_t.shape[0], b.shape[1])) for a_t, b in pairs],
        out_shape=[jax.ShapeDtypeStruct((a_t.shape[0], b.shape[1]), F32) for a_t, b in pairs],
        compiler_params=_params(("arbitrary",), VMEM_MID),
    )(*flat)


def _adam_math(w, g, m, v):
    nm = ADAM_B1 * m + (1.0 - ADAM_B1) * g
    nv = ADAM_B2 * v + (1.0 - ADAM_B2) * (g * g)
    m_hat = nm / (1.0 - ADAM_B1 ** ADAM_STEP)
    v_hat = nv / (1.0 - ADAM_B2 ** ADAM_STEP)
    return -ADAM_LR * (m_hat / (jnp.sqrt(v_hat) + ADAM_EPS) + ADAM_WD * w), nm, nv


def _adamw_rows(w, g, m, v, tr):
    rows, cols = w.shape

    def body(w_ref, g_ref, m_ref, v_ref, d_ref, nm_ref, nv_ref):
        d_ref[...], nm_ref[...], nv_ref[...] = _adam_math(w_ref[...], g_ref[...], m_ref[...], v_ref[...])

    blk = pl.BlockSpec((tr, cols), lambda i: (i, 0))
    shape = jax.ShapeDtypeStruct((rows, cols), F32)
    return _pallas(
        body, name="adamw_w_in", grid=(rows // tr,),
        in_specs=[blk] * 4, out_specs=[blk] * 3, out_shape=[shape] * 3,
        compiler_params=_params(("parallel",), VMEM_SMALL),
    )(w, g, m, v)


def _adamw_rest(bigs, smalls, g_small):
    nb, ns = len(bigs), len(smalls)

    def body(*refs):
        ins, outs = refs[:4 * nb + 3 * ns + 1], refs[4 * nb + 3 * ns + 1:]
        for i in range(nb):
            w_ref, g_ref, m_ref, v_ref = ins[4 * i:4 * i + 4]
            d_ref, nm_ref, nv_ref = outs[3 * i:3 * i + 3]
            d_ref[...], nm_ref[...], nv_ref[...] = _adam_math(w_ref[...], g_ref[...], m_ref[...], v_ref[...])
        gs_ref = ins[-1]
        for i in range(ns):
            w_ref, m_ref, v_ref = ins[4 * nb + 3 * i:4 * nb + 3 * i + 3]
            g_ref, d_ref, nm_ref, nv_ref = outs[3 * nb + 4 * i:3 * nb + 4 * i + 4]
            g = gs_ref[i:i + 1, 0:w_ref.shape[1]]
            g_ref[...] = g
            d_ref[...], nm_ref[...], nv_ref[...] = _adam_math(w_ref[...], g, m_ref[...], v_ref[...])

    flat_in = [a for quad in bigs for a in quad] + [a for tri in smalls for a in tri] + [g_small]
    out_shape = ([jax.ShapeDtypeStruct(q[0].shape, F32) for q in bigs for _ in range(3)]
                 + [jax.ShapeDtypeStruct(t[0].shape, F32) for t in smalls for _ in range(4)])
    return _pallas(
        body, name="adamw_rest",
        in_specs=[pl.BlockSpec(memory_space=pltpu.VMEM)] * len(flat_in),
        out_specs=[pl.BlockSpec(memory_space=pltpu.VMEM)] * len(out_shape),
        out_shape=out_shape,
        compiler_params=_params(vmem=VMEM_SMALL),
    )(*flat_in)


def _place(pieces, n):
    out, at = [], 0
    for lane0, arr in sorted(pieces, key=lambda p: p[0]):
        out += [jnp.zeros((n, lane0 - at), F32), arr]
        at = lane0 + arr.shape[1]
    return jnp.concatenate(out + [jnp.zeros((n, LANE - at), F32)], axis=1)


def _rope_tables(s_len, tm):
    rows = s_len // GRID_W
    row = jnp.arange(rows, dtype=F32)
    col = jnp.arange(GRID_W, dtype=F32)

    def lay(dim, layout, first_dim, ones):
        half = dim // 2
        inv = 1.0 / (ROPE_THETA ** (jnp.arange(0, half, 2, dtype=F32) / half))
        ang_r, ang_c = row[:, None] * inv[None, :], col[:, None] * inv[None, :]
        at = {a - first_dim: lane0 for a, _, lane0 in layout}
        q = dim // 4
        r1, r2, c1, c2 = at[0], at[q], at[2 * q], at[3 * q]
        cos_r = _place([(r1, jnp.cos(ang_r)), (r2, jnp.cos(ang_r))], rows)
        sin_r = _place([(r1, -jnp.sin(ang_r)), (r2, jnp.sin(ang_r))], rows)
        cos_c = _place([(c1, jnp.cos(ang_c)), (c2, jnp.cos(ang_c))] + [(l0, jnp.ones((GRID_W, n), F32)) for _, n, l0 in ones],
                       GRID_W)
        sin_c = _place([(c1, -jnp.sin(ang_c)), (c2, jnp.sin(ang_c))], GRID_W)
        by_block = (s_len // tm, tm // GRID_W, LANE)
        return cos_r.reshape(by_block), cos_c, sin_r.reshape(by_block), sin_c

    return lay(A_DIM, LAY_ROPE_A, 0, ()) + lay(B_ROPE, LAY_KR, 0, LAY_NOPE)


def _spread(w, n_heads, dim, axis, layout):
    w3 = w.reshape(w.shape[:axis] + (n_heads, dim) + w.shape[axis + 1:])
    out, at = [], 0

    def zeros(n):
        return jnp.zeros(w3.shape[:axis + 1] + (n,) + w3.shape[axis + 2:], w.dtype)

    for a0, n, lane0 in sorted(layout, key=lambda seg: seg[2]):
        out += [zeros(lane0 - at), lax.slice_in_dim(w3, a0, a0 + n, axis=axis + 1)]
        at = lane0 + n
    out = jnp.concatenate(out + [zeros(LANE - at)], axis=axis + 1)
    return out.reshape(w.shape[:axis] + (n_heads * LANE,) + w.shape[axis + 1:])


def _unspread(w, n_heads, dim, axis, layout):
    w3 = w.reshape(w.shape[:axis] + (n_heads, LANE) + w.shape[axis + 1:])
    parts = [lax.slice_in_dim(w3, lane0, lane0 + n, axis=axis + 1) for _, n, lane0 in sorted(layout)]
    out = parts[0] if len(parts) == 1 else jnp.concatenate(parts, axis=axis + 1)
    return out.reshape(w.shape[:axis] + (n_heads * dim,) + w.shape[axis + 1:])


def _head_cols(first, n_heads, dim, layout):
    out = np.full((n_heads * LANE,), -1, np.int32)
    for h in range(n_heads):
        for a0, n, lane0 in layout:
            out[h * LANE + lane0:h * LANE + lane0 + n] = first + h * dim + a0 + np.arange(n)
    return out


def _inverse(src, n):
    dst = np.full((n,), -1, np.int32)
    dst[src[src >= 0]] = np.nonzero(src >= 0)[0]
    return dst


def _column_maps():
    a_w, kv_w = A_HEADS * A_DIM, A_KV * A_DIM
    o_g = a_w + 2 * kv_w
    o_cq = o_g + a_w
    o_kr = o_cq + B_Q_RANK + B_KV_RANK
    src_in = np.concatenate([
        _head_cols(0, A_HEADS, A_DIM, LAY_ROPE_A), _head_cols(a_w, A_KV, A_DIM, LAY_ROPE_A),
        _head_cols(a_w + kv_w, A_KV, A_DIM, LAY_PLAIN_A), _head_cols(o_g, A_HEADS, A_DIM, LAY_PLAIN_A),
        np.arange(o_kr + B_ROPE, N_IN), np.arange(o_cq, o_kr), _head_cols(o_kr, 1, B_ROPE, LAY_KR)]).astype(np.int32)
    src_uq = _head_cols(0, B_HEADS, B_QK, LAY_ROPE_B)
    per = B_NOPE + B_V
    src_ukv = np.concatenate([_head_cols(0, B_HEADS, per, LAY_NOPE),
                              _head_cols(B_NOPE, B_HEADS, per, ((0, B_V, 0),))]).astype(np.int32)
    assert len(src_in) == N_EXT
    return src_in, src_uq, src_ukv


def _round_up(n, m):
    return (n + m - 1) // m * m


def _permute_cols(xs, maps, stacks, name):
    maps = [np.asarray(m, np.int32) for m in maps]
    n_arr = len(xs)

    def block(ref, b):
        if len(ref.shape) == 2:
            return ref.at[:, b * LANE:(b + 1) * LANE]
        per = ref.shape[2] // LANE
        return ref.at[b // per, :, (b % per) * LANE:(b % per + 1) * LANE]

    def body(*refs):
        row = lax.broadcasted_iota(jnp.int32, (LANE, LANE), 0)
        for x_ref, src_ref, o_ref, src in zip(refs[:n_arr], refs[n_arr:2 * n_arr], refs[2 * n_arr:], maps):
            for c in range(len(src) // LANE):
                want = src[c * LANE:(c + 1) * LANE]
                if want[0] >= 0 and want[0] % LANE == 0 and np.array_equal(want, want[0] + np.arange(LANE)):
                    block(o_ref, c)[...] = block(x_ref, int(want[0]) // LANE)[...]
                    continue
                acc = jnp.zeros((x_ref.shape[-2], LANE), F32)
                for kb in sorted({int(v) // LANE for v in want if v >= 0}):
                    sel = jnp.where(row + kb * LANE == src_ref[:, c * LANE:(c + 1) * LANE], 1.0, 0.0).astype(BF16)
                    part = block(x_ref, kb)[...]
                    if part.dtype == BF16:
                        acc = acc + _nn(part, sel)
                    else:
                        hi = part.astype(BF16)
                        rest = part - hi.astype(F32)
                        mid = rest.astype(BF16)
                        low = (rest - mid.astype(F32)).astype(BF16)
                        acc = acc + ((_nn(hi, sel) + _nn(mid, sel)) + _nn(low, sel))
                block(o_ref, c)[...] = acc.astype(o_ref.dtype)

    def out_shape(x, m, stack):
        rows = x.shape[-2]
        return (rows, len(m)) if stack is None else (stack, rows, len(m) // stack)

    return _pallas(
        body, name=name,
        out_shape=[jax.ShapeDtypeStruct(out_shape(x, m, st), x.dtype) for x, m, st in zip(xs, maps, stacks)],
        compiler_params=_params(vmem=VMEM_MID),
    )(*xs, *[jnp.asarray(m).reshape(1, -1) for m in maps])


def _pad_cols(w):
    return jnp.pad(w, ((0, 0), (0, _round_up(w.shape[1], LANE) - w.shape[1])))


def _in_stack(cols, width):
    cols = np.asarray(cols)
    return np.where(cols < 0, -1, cols // width * _round_up(width, LANE) + cols % width).astype(np.int32)


def _ext_weights(g_in, g_uq, g_ukv, g_out):
    src_in, src_uq, src_ukv = _column_maps()
    w_uq = g_uq.reshape(B_Q_RANK, B_HEADS * B_QK)
    w_out = g_out.reshape(D_MODEL, D_MODEL)
    w_in_ext, w_uq_pad, w_ukv_ext = _permute_cols(
        [g_in, w_uq, g_ukv], [_in_stack(src_in, SH_IN[1]), src_uq, _in_stack(src_ukv, SH_UKV[1])], [None] * 3, "lay_out_weights")
    a_w = A_HEADS * A_DIM
    w_out_ext = jnp.concatenate([_spread(w_out[:a_w], A_HEADS, A_DIM, 0, LAY_PLAIN_A), w_out[a_w:]], axis=0)
    return w_in_ext, w_uq_pad, w_ukv_ext, w_out_ext


def _fold_grads(d_in_ext, d_uq_pad, d_ukv_ext, d_out_ext):
    src_in, src_uq, src_ukv = _column_maps()

    def back(src, n, width):
        inv = _inverse(src, n)
        wide = _round_up(width, LANE)
        out = np.full((n // width * wide,), -1, np.int32)
        for j in range(n // width):
            out[j * wide:j * wide + width] = inv[j * width:(j + 1) * width]
        return out

    n_uq, n_ukv = B_HEADS * B_QK, B_HEADS * (B_NOPE + B_V)
    d_in, d_uq, d_ukv = _permute_cols(
        [d_in_ext, d_uq_pad, d_ukv_ext], [back(src_in, N_IN, SH_IN[1]), _inverse(src_uq, n_uq), back(src_ukv, n_ukv, SH_UKV[1])],
        [N_CHIPS, None, N_CHIPS], "fold_grads")
    d_out = jnp.concatenate([_unspread(d_out_ext[:A_HEADS * LANE], A_HEADS, A_DIM, 0, LAY_PLAIN_A), d_out_ext[A_HEADS * LANE:]], axis=0)
    return d_in, d_uq.reshape((N_CHIPS,) + SH_UQ), d_ukv, d_out.reshape((N_CHIPS,) + SH_OUT)


def kernel(x, norm_in, w_in, a_q_norm, a_k_norm, b_cq_norm, b_ckv_norm, w_uq, w_ukv, b_q_norm, b_k_norm, w_out, loss_target, m_norm_in, m_w_in, m_a_q_norm, m_a_k_norm, m_b_cq_norm, m_b_ckv_norm, m_w_uq, m_w_ukv, m_b_q_norm, m_b_k_norm, m_w_out, v_norm_in, v_w_in, v_a_q_norm, v_a_k_norm, v_b_cq_norm, v_b_ckv_norm, v_w_uq, v_w_ukv, v_b_q_norm, v_b_k_norm, v_w_out):
    s_len = x.shape[1]
    xs, ts = x[0], loss_target[0]
    tm = min(256, s_len)
    tq, tk_f = min(512, s_len // 2), min(1024, s_len // 2)
    tq_b, tk_b = min(1024, s_len // 2), min(512, s_len)
    tiles_f = min(4, s_len // tq)
    tiles_b = min(2, s_len // tk_b)

    w_in_ext, w_uq_pad, w_ukv_ext, w_out_ext = _ext_weights(
        *_gather_weights((_pad_cols(w_in[0]), w_uq[0], _pad_cols(w_ukv[0]), w_out[0])))
    gains = (norm_in, _spread(a_q_norm, 1, A_DIM, 1, LAY_ROPE_A), _spread(a_k_norm, 1, A_DIM, 1, LAY_ROPE_A), b_cq_norm, b_ckv_norm,
             _spread(b_q_norm, 1, B_QK, 1, LAY_ROPE_B), _spread(b_k_norm, 1, B_QK, 1, LAY_ROPE_B))
    tabs = _rope_tables(s_len, tm)

    (xn_t, gates, pre, qbpre, kbpre, cq_t, ckv_t, qa, ka, va, qb, kb, vb) = _pre(
        xs, tabs, w_in_ext, w_uq_pad, w_ukv_ext, gains, tm)
    o_a, lse_a = _attn_fwd(qa, ka, va, A_GROUP, A_DIM, tq, tk_f, tiles_f, "attn_fwd_a")
    o_b, lse_b = _attn_fwd(qb, kb, vb, 1, B_V, tq, tk_f, tiles_f, "attn_fwd_b")
    y_t, dh, dgate, do_a, do_b, delta, loss_part = _mid(xs, ts, o_a, o_b, gates, w_out_ext, min(512, s_len))

    def stat(a):
        return a.reshape(a.shape[0], s_len // tq_b, 1, tq_b)

    dqa, dka, dva = _attn_bwd(qa, ka, va, do_a, stat(lse_a), stat(delta[:A_HEADS]), A_GROUP, tq_b, tk_b, tiles_b, "attn_bwd_a")
    dqb, dkb, dvb = _attn_bwd(qb, kb, vb, do_b, stat(lse_b), stat(delta[A_HEADS:A_HEADS + B_HEADS]), 1, tq_b, tk_b,
                              tiles_b, "attn_bwd_b")
    grad_x, dproj, dqbpre, dkvb, d_small = _post(
        xs, dh, pre, qbpre, kbpre, dgate, dqa, dka, dva, dqb, dkb, dvb, loss_part, tabs,
        w_in_ext, w_uq_pad, w_ukv_ext, gains, tm)

    ts_w = min(2048, s_len)
    d_in_ext = _grad_w(xn_t, dproj, 768, min(2 * ts_w, s_len), "grad_w_in")
    d_out_ext = _grad_w(y_t, dh, 512, ts_w, "grad_w_out")
    d_uq_pad, d_ukv_ext = _grad_w_pairs([(cq_t, dqbpre), (ckv_t, dkvb)], ts_w, "grad_w_mla")

    g_in_p, g_uq, g_ukv_p, g_out, g_small = _reduce_grads(_fold_grads(d_in_ext, d_uq_pad, d_ukv_ext, d_out_ext), d_small)
    g_in, g_ukv = g_in_p[:, :SH_IN[1]], g_ukv_p[:, :SH_UKV[1]]
    d_in, nm_in, nv_in = (a.T for a in _adamw_rows(w_in[0].T, g_in_p.T[:SH_IN[1]], m_w_in[0].T, v_w_in[0].T, SH_IN[1] // 7))
    rest = _adamw_rest(
        [(w_uq[0], g_uq, m_w_uq[0], v_w_uq[0]), (w_ukv[0], g_ukv, m_w_ukv[0], v_w_ukv[0]),
         (w_out[0], g_out, m_w_out[0], v_w_out[0])],
        [(norm_in, m_norm_in, v_norm_in), (a_q_norm, m_a_q_norm, v_a_q_norm), (a_k_norm, m_a_k_norm, v_a_k_norm),
         (b_cq_norm, m_b_cq_norm, v_b_cq_norm), (b_ckv_norm, m_b_ckv_norm, v_b_ckv_norm),
         (b_q_norm, m_b_q_norm, v_b_q_norm), (b_k_norm, m_b_k_norm, v_b_k_norm)], g_small)
    (d_uq, nm_uq, nv_uq), (d_ukv, nm_ukv, nv_ukv), (d_out, nm_out, nv_out) = (rest[3 * i:3 * i + 3] for i in range(3))
    sm = [rest[9 + 4 * i:9 + 4 * i + 4] for i in range(7)]

    def leaves(k, p_in, p_uq, p_ukv, p_out):
        return [sm[SM_IN][k], p_in[None], sm[SM_AQ][k], sm[SM_AK][k], sm[SM_CQ][k], sm[SM_CKV][k], p_uq[None], p_ukv[None],
                sm[SM_BQ][k], sm[SM_BK][k], p_out[None]]

    return (g_small[SM_LOSS, 0], grad_x[None], *leaves(0, g_in, g_uq, g_ukv, g_out), *leaves(1, d_in, d_uq, d_ukv, d_out),
            *leaves(2, nm_in, nm_uq, nm_ukv, nm_out), *leaves(3, nv_in, nv_uq, nv_ukv, nv_out))
```

```python
import jax
import jax.numpy as jnp
import numpy as np
from jax import lax
from jax.experimental import pallas as pl
from jax.experimental.pallas import tpu as pltpu

F32 = jnp.float32
BF16 = jnp.bfloat16
MESH = pl.DeviceIdType.MESH

D_MODEL = 1024
GRID_W = 64
ROPE_THETA = 10000.0
EPS = 1e-6
A_HEADS, A_KV, A_DIM = 8, 2, 64
A_GROUP = A_HEADS // A_KV
B_HEADS, B_NOPE, B_ROPE, B_V = 4, 64, 32, 128
B_QK = B_NOPE + B_ROPE
B_Q_RANK, B_KV_RANK = 384, 256
N_IN = 2464
SCALE_A = 1.0 / float(np.sqrt(A_DIM))
SCALE_B = 1.0 / float(np.sqrt(B_QK))
LOG2E = float(np.log2(np.e))
LN2 = float(np.log(2.0))
ADAM_LR, ADAM_B1, ADAM_B2, ADAM_EPS, ADAM_WD, ADAM_STEP = 0.001, 0.9, 0.999, 1e-08, 0.01, 10

LANE = 128
VMEM_BYTES = 64 * 1024 * 1024
VMEM_LIMIT = VMEM_BYTES - 8 * 1024 * 1024
VMEM_MID = 48 * 1024 * 1024
VMEM_SMALL = 32 * 1024 * 1024

QA0 = 0
KA0 = QA0 + A_HEADS * LANE
VA0 = KA0 + A_KV * LANE
GA0 = VA0 + A_KV * LANE
GB0 = GA0 + A_HEADS * LANE
CQ0 = GB0 + B_HEADS * LANE
CKV0 = CQ0 + B_Q_RANK
KR0 = CKV0 + B_KV_RANK
N_EXT = KR0 + LANE
N_GATE = (A_HEADS + B_HEADS) * LANE
DELTA_ROWS = 16
N_PRE = KA0 + A_KV * LANE + B_Q_RANK + B_KV_RANK

ROT = LANE // 2
_QA = A_DIM // 4
_QB = B_ROPE // 4
LAY_PLAIN_A = ((0, A_DIM, 0),)
LAY_ROPE_A = ((0, _QA, 0), (2 * _QA, _QA, _QA), (_QA, _QA, ROT), (3 * _QA, _QA, ROT + _QA))
LAY_KR = ((0, _QB, 0), (2 * _QB, _QB, _QB), (_QB, _QB, ROT), (3 * _QB, _QB, ROT + _QB))
LAY_NOPE = ((0, B_NOPE // 2, 2 * _QB), (B_NOPE // 2, B_NOPE // 2, ROT + 2 * _QB))
LAY_ROPE_B = LAY_NOPE + tuple((B_NOPE + a, n, at) for a, n, at in LAY_KR)

N_CHIPS = 4
SH_IN = (D_MODEL, N_IN // N_CHIPS)
SH_UQ = (B_Q_RANK // N_CHIPS, B_HEADS * B_QK)
SH_UKV = (B_KV_RANK, B_HEADS * (B_NOPE + B_V) // N_CHIPS)
SH_OUT = (D_MODEL // N_CHIPS, D_MODEL)
SM_ROWS, SM_W = 16, D_MODEL
SM_IN, SM_AQ, SM_AK, SM_CQ, SM_CKV, SM_BQ, SM_BK, SM_LOSS = range(8)
F32_ROWS, BF16_ROWS = 8, 16


def _pallas(body, **kw):
    return pl.pallas_call(body, **kw)


def _params(sem=None, vmem=None):
    return pltpu.CompilerParams(dimension_semantics=sem, vmem_limit_bytes=vmem)


def _rms_fwd(x, g, n):
    r = lax.rsqrt(jnp.sum(x * x, axis=-1, keepdims=True) * (1.0 / n) + EPS)
    return x * r * g


def _rms_bwd(dy, x, g, n):
    u = dy * g
    r = lax.rsqrt(jnp.sum(x * x, axis=-1, keepdims=True) * (1.0 / n) + EPS)
    ux = jnp.sum(u * x, axis=-1, keepdims=True)
    xhat = x * r
    dx = r * (u - xhat * (r * ux * (1.0 / n)))
    return dx, dy * xhat


def _rope_fwd(y, cos, sin):
    return y * cos + pltpu.roll(y, ROT, 1) * sin


def _rope_bwd(d, cos, sin):
    return d * cos - pltpu.roll(d, ROT, 1) * sin


def _token_tables(refs):
    out = []
    for r_ref, c_ref in zip(refs[0::2], refs[1::2]):
        r, c = r_ref[...], c_ref[...]
        out.append(jnp.concatenate([r[k:k + 1, :] + c for k in range(r.shape[0])], axis=0))
    return out


def _lanes_of(lane, layout):
    m = None
    for _, n, at in layout:
        seg = (lane >= at) & (lane < at + n)
        m = seg if m is None else (m | seg)
    return m


def _unspread_row(v, layout):
    v8 = jnp.broadcast_to(v, (F32_ROWS, LANE))
    lane = lax.broadcasted_iota(jnp.int32, v8.shape, 1)
    out = jnp.zeros_like(v8)
    for a, n, at in layout:
        moved = v8 if a == at else pltpu.roll(v8, (a - at) % LANE, 1)
        out = jnp.where((lane >= a) & (lane < a + n), moved, out)
    return out[0:1, :]


def _nt(a, b):
    return lax.dot_general(a, b, (((1,), (1,)), ((), ())), preferred_element_type=F32)


def _tn(a, b):
    return lax.dot_general(a, b, (((0,), (0,)), ((), ())), preferred_element_type=F32)


def _nn(a, b):
    return jnp.dot(a, b, preferred_element_type=F32)


def _block_rows(i, size):
    if isinstance(i, int):
        return pl.ds(i * size, size)
    return pl.ds(pl.multiple_of(i * size, size), size)


MAX_STATIC_BLOCKS = 32


def _three_stage(n, first, second, third):
    assert n >= 2 and n % 2 == 0
    first(0, 0)
    first(1, 1)
    second(0, 0)
    if n <= MAX_STATIC_BLOCKS:
        for i in range(1, n - 1):
            first(i + 1, (i + 1) % 2)
            second(i, i % 2)
            third(i - 1, (i - 1) % 2)
    else:
        def pair(t, carry):
            i = 2 * t + 1
            first(i + 1, 0)
            second(i, 1)
            third(i - 1, 0)
            first(i + 2, 1)
            second(i + 1, 0)
            third(i, 1)
            return carry

        lax.fori_loop(0, (n - 2) // 2, pair, 0)
    second(n - 1, 1)
    third(n - 2, 0)
    third(n - 1, 1)


def _full(shape):
    return pl.BlockSpec(shape, lambda *_: (0,) * len(shape))


def _table_specs(tm):
    return [pl.BlockSpec((None, tm // GRID_W, LANE), lambda i: (i, 0, 0)), _full((GRID_W, LANE))] * 4


def _resident(shape):
    return pl.BlockSpec(shape, lambda *_: (0,) * len(shape), pipeline_mode=pl.Buffered(1))


def _gather_weights(shards):
    n = len(shards)
    halves = [w.shape[0] // 2 for w in shards]

    def body(*refs):
        w_refs, out_refs, (send_sems, recv_sems) = refs[:n], refs[n:2 * n], refs[2 * n:]
        x, y, c = lax.axis_index("x"), lax.axis_index("y"), lax.axis_index("c")
        sibling = (x, y, 1 - c)
        chips = [(1 - x, y), (x, 1 - y), (1 - x, 1 - y)]
        me = 2 * x + y

        def copy(a, k, j, hc, to):
            part = out_refs[a].at[j, pl.ds(pl.multiple_of(hc * halves[a], BF16_ROWS), halves[a]), :]
            return pltpu.make_async_remote_copy(
                src_ref=part, dst_ref=part, send_sem=send_sems.at[6 * a + k], recv_sem=recv_sems.at[6 * a + k],
                device_id=to, device_id_type=MESH)

        started = []
        for a in range(n):
            out_refs[a][me] = w_refs[a][...].astype(BF16)
            for k, chip in enumerate(chips):
                started.append(copy(a, k, me, c, (*chip, c)))
                started[-1].start()
        for k, chip in enumerate(chips):
            for a in range(n):
                copy(a, k, 2 * chip[0] + chip[1], c, (*chip, c)).wait_recv()
                started.append(copy(a, 3 + k, 2 * chip[0] + chip[1], c, sibling))
                started[-1].start()
        for k, chip in enumerate(chips):
            for a in range(n):
                copy(a, 3 + k, 2 * chip[0] + chip[1], 1 - c, sibling).wait_recv()
        for cp in started:
            cp.wait_send()

    return _pallas(
        body, name="gather_weights",
        out_shape=[jax.ShapeDtypeStruct((N_CHIPS,) + w.shape, BF16) for w in shards],
        in_specs=[pl.BlockSpec(memory_space=pltpu.VMEM)] * n,
        out_specs=[pl.BlockSpec(memory_space=pltpu.VMEM)] * n,
        scratch_shapes=[pltpu.SemaphoreType.DMA((6 * n,)), pltpu.SemaphoreType.DMA((6 * n,))],
        compiler_params=_params(vmem=VMEM_SMALL),
    )(*shards)


def _reduce_grads(parts, small):
    n_big = len(parts)
    n = n_big + 1
    shapes = [p.shape[1:] for p in parts] + [small.shape]
    halves = [sh[0] // 2 for sh in shapes]

    def body(*refs):
        p_refs, out_refs, rec_a, rec_b = refs[:n], refs[n:2 * n], refs[2 * n:3 * n], refs[3 * n:4 * n]
        send_b = refs[4 * n:4 * n + n_big]
        sa_send, sa_recv, sb_send, sb_recv, sc_send, sc_recv = refs[4 * n + n_big:]
        x, y, c = lax.axis_index("x"), lax.axis_index("y"), lax.axis_index("c")
        sibling = (x, y, 1 - c)
        me = 2 * x + y

        def rows(a, hc):
            return pl.ds(pl.multiple_of(hc * halves[a], F32_ROWS), halves[a])

        def partial(a, j, hc):
            return p_refs[a].at[j, rows(a, hc), :] if a < n_big else p_refs[a].at[rows(a, hc), :]

        def copy_a(a, j):
            return pltpu.make_async_remote_copy(
                src_ref=partial(a, j, 1 - c), dst_ref=rec_a[a].at[j],
                send_sem=sa_send.at[N_CHIPS * a + j], recv_sem=sa_recv.at[N_CHIPS * a + j],
                device_id=sibling, device_id_type=MESH)

        def copy_b(a, r):
            j = me ^ r
            k = (N_CHIPS - 1) * a + r - 1
            return pltpu.make_async_remote_copy(
                src_ref=(send_b[a] if a < n_big else rec_a[a]).at[j], dst_ref=rec_b[a].at[r],
                send_sem=sb_send.at[k], recv_sem=sb_recv.at[k], device_id=(j // 2, j % 2, c), device_id_type=MESH)

        def copy_c(a):
            return pltpu.make_async_remote_copy(
                src_ref=out_refs[a].at[rows(a, c), :], dst_ref=out_refs[a].at[rows(a, c), :],
                send_sem=sc_send.at[a], recv_sem=sc_recv.at[a], device_id=sibling, device_id_type=MESH)

        for a in range(n):
            for j in range(N_CHIPS):
                copy_a(a, j).start()
        for r in range(1, N_CHIPS):
            j = me ^ r
            for a in range(n):
                copy_a(a, j).wait_recv()
                chip_part = rec_a[a][j] + partial(a, j, c)[...]
                if a < n_big:
                    send_b[a][j] = chip_part.astype(BF16)
                else:
                    rec_a[a][j] = chip_part
                copy_b(a, r).start()
        for a in range(n):
            copy_a(a, me).wait_recv()
            rec_b[a][0] = (rec_a[a][me] + partial(a, me, c)[...]).astype(rec_b[a].dtype)
        for a in range(n):
            for r in range(1, N_CHIPS):
                copy_b(a, r).wait_recv()
            total = rec_b[a][me].astype(F32)
            for j in range(1, N_CHIPS):
                total = total + rec_b[a][j ^ me].astype(F32)
            out_refs[a][rows(a, c), :] = total
            copy_c(a).start()
        for a in range(n):
            copy_c(a).wait_recv()
        for a in range(n):
            for j in range(N_CHIPS):
                copy_a(a, j).wait_send()
            for r in range(1, N_CHIPS):
                copy_b(a, r).wait_send()
            copy_c(a).wait_send()

    dma = pltpu.SemaphoreType.DMA
    return _pallas(
        body, name="reduce_grads",
        out_shape=[jax.ShapeDtypeStruct(sh, F32) for sh in shapes],
        in_specs=[pl.BlockSpec(memory_space=pltpu.VMEM)] * n,
        out_specs=[pl.BlockSpec(memory_space=pltpu.VMEM)] * n,
        scratch_shapes=[pltpu.VMEM((N_CHIPS, h) + sh[1:], F32) for h, sh in zip(halves, shapes)]
                       + [pltpu.VMEM((N_CHIPS, h) + sh[1:], BF16 if a < n_big else F32)
                          for a, (h, sh) in enumerate(zip(halves, shapes))]
                       + [pltpu.VMEM((N_CHIPS, h) + sh[1:], BF16) for h, sh in zip(halves[:n_big], shapes[:n_big])]
                       + [dma((N_CHIPS * n,)), dma((N_CHIPS * n,)), dma(((N_CHIPS - 1) * n,)), dma(((N_CHIPS - 1) * n,)),
                          dma((n,)), dma((n,))],
        compiler_params=_params(vmem=VMEM_LIMIT),
    )(*parts, small)


def _pre(x, tabs, w_in_ext, w_uq_pad, w_ukv_ext, gains, tm):
    s_len = x.shape[0]
    nt = s_len // tm

    def body(x_ref, car_ref, cac_ref, sar_ref, sac_ref, cbr_ref, cbc_ref, sbr_ref, sbc_ref, win_ref, wuq_ref, wukv_ref,
             gin_ref, gaq_ref, gak_ref, gcq_ref, gckv_ref, gbq_ref, gbk_ref,
             xn_ref, gates_ref, pre_ref, qbpre_ref, kbpre_ref, cq_ref, ckv_ref,
             qa_ref, ka_ref, va_ref, qb_ref, kb_ref, vb_ref, proj):
        xn = _rms_fwd(x_ref[...], gin_ref[...], D_MODEL)
        xn_ref[...] = jnp.transpose(xn).astype(BF16)
        proj[...] = _nn(xn.astype(BF16), win_ref[...])
        gates_ref[...] = proj[:, GA0:GA0 + N_GATE]
        pre_ref[:, 0:VA0] = proj[:, 0:VA0]
        pre_ref[:, VA0:N_PRE] = proj[:, CQ0:KR0]
        ca, sa, cb, sb = _token_tables((car_ref, cac_ref, sar_ref, sac_ref, cbr_ref, cbc_ref, sbr_ref, sbc_ref))
        lane = lax.broadcasted_iota(jnp.int32, (tm, LANE), 1)
        for h in range(A_HEADS):
            yq = _rms_fwd(proj[:, QA0 + LANE * h:QA0 + LANE * (h + 1)], gaq_ref[...], A_DIM)
            qa_ref[h] = (_rope_fwd(yq, ca, sa) * (SCALE_A * LOG2E)).astype(BF16)
        for h in range(A_KV):
            yk = _rms_fwd(proj[:, KA0 + LANE * h:KA0 + LANE * (h + 1)], gak_ref[...], A_DIM)
            ka_ref[h] = _rope_fwd(yk, ca, sa).astype(BF16)
            va_ref[h] = jnp.where(lane == A_DIM, 1.0, proj[:, VA0 + LANE * h:VA0 + LANE * (h + 1)]).astype(BF16)
        cq = _rms_fwd(proj[:, CQ0:CQ0 + B_Q_RANK], gcq_ref[...], B_Q_RANK)
        cq_ref[...] = jnp.transpose(cq).astype(BF16)
        qbpre_ref[...] = _nn(cq.astype(BF16), wuq_ref[...])
        ckv = _rms_fwd(proj[:, CKV0:CKV0 + B_KV_RANK], gckv_ref[...], B_KV_RANK)
        ckv_ref[...] = jnp.transpose(ckv).astype(BF16)
        kvb = _nn(ckv.astype(BF16), wukv_ref[...])
        kr = proj[:, KR0:KR0 + LANE]
        for h in range(B_HEADS):
            yq = _rms_fwd(qbpre_ref[:, LANE * h:LANE * (h + 1)], gbq_ref[...], B_QK)
            qb_ref[h] = (_rope_fwd(yq, cb, sb) * (SCALE_B * LOG2E)).astype(BF16)
            kp = kvb[:, LANE * h:LANE * (h + 1)] + kr
            kbpre_ref[:, LANE * h:LANE * (h + 1)] = kp
            kb_ref[h] = _rope_fwd(_rms_fwd(kp, gbk_ref[...], B_QK), cb, sb).astype(BF16)
            vb_ref[h, :, 0:LANE] = kvb[:, B_HEADS * LANE + LANE * h:B_HEADS * LANE + LANE * (h + 1)].astype(BF16)
            vb_ref[h, :, LANE:2 * LANE] = jnp.where(lane == 0, 1.0, 0.0).astype(BF16)

    row = lambda w: pl.BlockSpec((tm, w), lambda i: (i, 0))
    col = lambda w: pl.BlockSpec((w, tm), lambda i: (0, i))
    heads = lambda n: pl.BlockSpec((n, tm, LANE), lambda i: (0, i, 0))
    hs = lambda n: jax.ShapeDtypeStruct((n, s_len, LANE), BF16)
    return _pallas(
        body, name="pre", grid=(nt,),
        in_specs=[row(D_MODEL)] + _table_specs(tm)
                 + [_resident(w_in_ext.shape), _resident(w_uq_pad.shape), _resident(w_ukv_ext.shape)]
                 + [_full(g.shape) for g in gains],
        out_specs=[col(D_MODEL), row(N_GATE), row(N_PRE), row(B_HEADS * LANE), row(B_HEADS * LANE),
                   col(B_Q_RANK), col(B_KV_RANK),
                   heads(A_HEADS), heads(A_KV), heads(A_KV), heads(B_HEADS), heads(B_HEADS),
                   pl.BlockSpec((B_HEADS, tm, 2 * LANE), lambda i: (0, i, 0))],
        out_shape=[jax.ShapeDtypeStruct((D_MODEL, s_len), BF16), jax.ShapeDtypeStruct((s_len, N_GATE), F32),
                   jax.ShapeDtypeStruct((s_len, N_PRE), F32), jax.ShapeDtypeStruct((s_len, B_HEADS * LANE), F32),
                   jax.ShapeDtypeStruct((s_len, B_HEADS * LANE), F32),
                   jax.ShapeDtypeStruct((B_Q_RANK, s_len), BF16), jax.ShapeDtypeStruct((B_KV_RANK, s_len), BF16),
                   hs(A_HEADS), hs(A_KV), hs(A_KV), hs(B_HEADS), hs(B_HEADS),
                   jax.ShapeDtypeStruct((B_HEADS, s_len, 2 * LANE), BF16)],
        scratch_shapes=[pltpu.VMEM((tm, N_EXT), F32)],
        compiler_params=_params(("parallel",), VMEM_LIMIT),
    )(x, *tabs, w_in_ext, w_uq_pad, w_ukv_ext, *gains)


def _attn_fwd(q, k, v, group, l_col, tq, tk, tiles, name):
    n_heads, s_len, _ = q.shape
    v_w = v.shape[2]
    nk = s_len // tk

    def body(q_ref, k_ref, v_ref, o_ref, lse_ref, s_buf, p_buf, a_buf, m_ref, acc_ref):
        def scores(g, slot):
            s_buf[slot] = _nt(q_ref[_block_rows(g // nk, tq), :], k_ref[_block_rows(g % nk, tk), :])

        def softmax(g, slot):
            t = g // nk
            s = s_buf[slot]
            m_old = m_ref[t]
            m_new = jnp.maximum(m_old, jnp.max(s, axis=-1, keepdims=True))
            m_ref[t] = m_new
            a_buf[slot] = jnp.exp2(m_old - m_new)
            p_buf[slot] = jnp.exp2(s - jnp.tile(m_new, (1, tk // LANE))).astype(BF16)

        def values(g, slot):
            t = g // nk
            pv = _nn(p_buf[slot], v_ref[_block_rows(g % nk, tk), :])
            for c in range(0, v_w, LANE):
                acc_ref[t, :, c:c + LANE] = a_buf[slot] * acc_ref[t, :, c:c + LANE] + pv[:, c:c + LANE]

        m_ref[...] = jnp.full(m_ref.shape, -1e30, F32)
        acc_ref[...] = jnp.zeros(acc_ref.shape, F32)
        _three_stage(tiles * nk, scores, softmax, values)
        for t in range(tiles):
            l = acc_ref[t, :, l_col:l_col + 1]
            o = acc_ref[t, :, 0:LANE] * (1.0 / l)
            if l_col < LANE:
                lane = lax.broadcasted_iota(jnp.int32, o.shape, 1)
                o = jnp.where(lane == l_col, 0.0, o)
            o_ref[t * tq:(t + 1) * tq, :] = o
            lse_ref[t] = jnp.transpose(m_ref[t] + jnp.log2(jnp.broadcast_to(l, (tq, LANE))))[0:1, :]

    return _pallas(
        body, name=name, grid=(n_heads, s_len // (tiles * tq)),
        in_specs=[pl.BlockSpec((None, tiles * tq, LANE), lambda h, i: (h, i, 0)),
                  pl.BlockSpec((None, s_len, LANE), lambda h, i: (h // group, 0, 0)),
                  pl.BlockSpec((None, s_len, v_w), lambda h, i: (h // group, 0, 0))],
        out_specs=[pl.BlockSpec((None, tiles * tq, LANE), lambda h, i: (h, i, 0)),
                   pl.BlockSpec((None, tiles, 1, tq), lambda h, i: (h, i, 0, 0))],
        out_shape=[jax.ShapeDtypeStruct((n_heads, s_len, LANE), F32),
                   jax.ShapeDtypeStruct((n_heads, s_len // tq, 1, tq), F32)],
        scratch_shapes=[pltpu.VMEM((2, tq, tk), F32), pltpu.VMEM((2, tq, tk), BF16), pltpu.VMEM((2, tq, LANE), F32),
                        pltpu.VMEM((tiles, tq, LANE), F32), pltpu.VMEM((tiles, tq, v_w), F32)],
        compiler_params=_params(("parallel", "parallel"), VMEM_MID),
    )(q, k, v)


def _mid(x, target, o_a, o_b, gates, w_out_ext, tm):
    s_len = x.shape[0]
    nt = s_len // tm
    n_heads = A_HEADS + B_HEADS

    def body(x_ref, t_ref, oa_ref, ob_ref, g_ref, w_ref,
             yt_ref, dh_ref, dgate_ref, doa_ref, dob_ref, delta_ref, loss_ref, silu_scr, dsilu_scr, y_ref):
        @pl.when(pl.program_id(0) == 0)
        def _():
            loss_ref[...] = jnp.zeros_like(loss_ref)

        def o_of(h):
            return oa_ref[h] if h < A_HEADS else ob_ref[h - A_HEADS]

        for h in range(n_heads):
            cols = slice(LANE * h, LANE * (h + 1))
            g = g_ref[:, cols]
            sig = 1.0 / (1.0 + jnp.exp(-g))
            silu = g * sig
            silu_scr[:, cols] = silu
            dsilu_scr[:, cols] = sig * (1.0 + g * (1.0 - sig))
            y = o_of(h) * silu
            y_ref[:, cols] = y.astype(BF16)
            yt_ref[cols, :] = jnp.transpose(y).astype(BF16)
        err =x_ref[...] + _nn(y_ref[...], w_ref[...]) - t_ref[...]
        sq = jnp.sum(jnp.sum(err * err, axis=-1, keepdims=True), axis=0, keepdims=True)
        loss_ref[...] += jnp.broadcast_to(sq * (0.5 / D_MODEL), loss_ref.shape)
        dh = err * (1.0 / D_MODEL)
        dh_ref[...] = dh
        dy = _nt(dh.astype(BF16), w_ref[...])
        lane = lax.broadcasted_iota(jnp.int32, (tm, LANE), 1)
        delta = jnp.zeros((tm, LANE), F32)
        for h in range(n_heads):
            cols = slice(LANE * h, LANE * (h + 1))
            dyh = dy[:, cols]
            oh = o_of(h)
            do = dyh * silu_scr[:, cols]
            dgate_ref[:, cols] = (dyh * oh * dsilu_scr[:, cols]).astype(BF16)
            delta = jnp.where(lane == h, jnp.sum(do * oh, axis=-1, keepdims=True), delta)
            if h < A_HEADS:
                doa_ref[h] = do.astype(BF16)
            else:
                dob_ref[h - A_HEADS] = do.astype(BF16)
        delta_ref[...] = jnp.transpose(delta)[0:DELTA_ROWS, :]

    row = lambda w: pl.BlockSpec((tm, w), lambda i: (i, 0))
    heads = lambda n, w=LANE: pl.BlockSpec((n, tm, w), lambda i: (0, i, 0))
    return _pallas(
        body, name="mid", grid=(nt,),
        in_specs=[row(D_MODEL), row(D_MODEL), heads(A_HEADS), heads(B_HEADS), row(N_GATE), _resident(w_out_ext.shape)],
        out_specs=[pl.BlockSpec((N_GATE, tm), lambda i: (0, i)), row(D_MODEL), row(N_GATE), heads(A_HEADS), heads(B_HEADS),
                   pl.BlockSpec((DELTA_ROWS, tm), lambda i: (0, i)),
                   _full((8, LANE))],
        out_shape=[jax.ShapeDtypeStruct((N_GATE, s_len), BF16), jax.ShapeDtypeStruct((s_len, D_MODEL), F32),
                   jax.ShapeDtypeStruct((s_len, N_GATE), BF16),
                   jax.ShapeDtypeStruct((A_HEADS, s_len, LANE), BF16), jax.ShapeDtypeStruct((B_HEADS, s_len, LANE), BF16),
                   jax.ShapeDtypeStruct((DELTA_ROWS, s_len), F32), jax.ShapeDtypeStruct((8, LANE), F32)],
        scratch_shapes=[pltpu.VMEM((tm, N_GATE), F32), pltpu.VMEM((tm, N_GATE), F32), pltpu.VMEM((tm, N_GATE), BF16)],
        compiler_params=_params(("arbitrary",), VMEM_LIMIT),
    )(x, target, o_a, o_b, gates, w_out_ext)


def _attn_bwd(q, k, v, do, lse, delta, group, tq, tk, tiles, name):
    n_heads, s_len, _ = q.shape
    nq = s_len // tq

    def body(q_ref, do_ref, lse_ref, delta_ref, k_ref, v_ref, dq_ref, dk_ref, dv_ref, s_buf, dp_buf, p_buf, ds_buf):
        @pl.when(pl.program_id(1) == 0)
        def _():
            dq_ref[...] = jnp.zeros_like(dq_ref)

        dk_ref[...] = jnp.zeros_like(dk_ref)
        dv_ref[...] = jnp.zeros_like(dv_ref)

        def keys(g):
            return _block_rows(g // nq, tk)

        def queries(g):
            return _block_rows(g % nq, tq)

        def scores(g, slot):
            s_buf[slot] = _nt(k_ref[keys(g), :], q_ref[queries(g), :])
            dp_buf[slot] = _nt(v_ref[keys(g), :], do_ref[queries(g), :])

        def elementwise(g, slot):
            p = jnp.exp2(s_buf[slot] - lse_ref[g % nq])
            p_buf[slot] = p.astype(BF16)
            ds_buf[slot] = (p * (dp_buf[slot] - delta_ref[g % nq])).astype(BF16)

        def grads(g, slot):
            dv_ref[keys(g), :] += _nn(p_buf[slot], do_ref[queries(g), :])
            dk_ref[keys(g), :] += _nn(ds_buf[slot], q_ref[queries(g), :])
            dq_ref[queries(g), :] += _tn(ds_buf[slot], k_ref[keys(g), :])

        _three_stage(tiles * nq, scores, elementwise, grads)

    whole = lambda: pl.BlockSpec((None, s_len, LANE), lambda h, j: (h, 0, 0))
    stat = lambda: pl.BlockSpec((None, nq, 1, tq), lambda h, j: (h, 0, 0, 0))
    kvb = lambda: pl.BlockSpec((None, tiles * tk, LANE), lambda h, j: (h // group, j, 0))
    outb = lambda: pl.BlockSpec((None, tiles * tk, LANE), lambda h, j: (h, j, 0))
    shape = jax.ShapeDtypeStruct((n_heads, s_len, LANE), F32)
    return _pallas(
        body, name=name, grid=(n_heads, s_len // (tiles * tk)),
        in_specs=[whole(), whole(), stat(), stat(), kvb(), kvb()],
        out_specs=[whole(), outb(), outb()],
        out_shape=[shape, shape, shape],
        scratch_shapes=[pltpu.VMEM((2, tk, tq), F32), pltpu.VMEM((2, tk, tq), F32),
                        pltpu.VMEM((2, tk, tq), BF16), pltpu.VMEM((2, tk, tq), BF16)],
        compiler_params=_params(("parallel", "arbitrary"), VMEM_MID),
    )(q, do, lse, delta, k, v)


def _post(x, dh, pre, qbpre, kbpre, dgate, dqa, dka, dva, dqb, dkb, dvb, loss_part, tabs,
          w_in_ext, w_uq_pad, w_ukv_ext, gains, tm):
    s_len = x.shape[0]
    nt = s_len // tm

    def body(x_ref, dh_ref, pre_ref, qbpre_ref, kbpre_ref, dgate_ref,
             dqa_ref, dka_ref, dva_ref, dqb_ref, dkb_ref, dvb_ref, loss_ref,
             car_ref, cac_ref, sar_ref, sac_ref, cbr_ref, cbc_ref, sbr_ref, sbc_ref, win_ref, wuq_ref, wukv_ref,
             gin_ref, gaq_ref, gak_ref, gcq_ref, gckv_ref, gbq_ref, gbk_ref,
             gx_ref, dproj_ref, dqbpre_ref, dkvb_ref, dsm_ref):
        @pl.when(pl.program_id(0) == 0)
        def _():
            dsm_ref[...] = jnp.zeros_like(dsm_ref)
            dsm_ref[SM_LOSS:SM_LOSS + 1, 0:LANE] = loss_ref[0:1, :]

        def add_small(r, dg):
            dsm_ref[r:r + 1, 0:dg.shape[1]] += dg

        def tok_sum(a):
            return jnp.sum(a, axis=0, keepdims=True)

        ca, sa, cb, sb = _token_tables((car_ref, cac_ref, sar_ref, sac_ref, cbr_ref, cbc_ref, sbr_ref, sbc_ref))
        lane = lax.broadcasted_iota(jnp.int32, (tm, LANE), 1)

        nope_lanes = _lanes_of(lane, LAY_NOPE)

        def back(c0, c1):
            return _nt(dproj_ref[:, c0:c1], win_ref[:, c0:c1])

        dproj_ref[:, GA0:GA0 + N_GATE] = dgate_ref[...]
        dxn = back(GA0, GA0 + N_GATE)
        dg = jnp.zeros((1, LANE), F32)
        for h in range(A_HEADS):
            dn = _rope_bwd(dqa_ref[h] * SCALE_A, ca, sa)
            dx, dgr = _rms_bwd(dn, pre_ref[:, QA0 + LANE * h:QA0 + LANE * (h + 1)], gaq_ref[...], A_DIM)
            dproj_ref[:, QA0 + LANE * h:QA0 + LANE * (h + 1)] = dx.astype(BF16)
            dg = dg + tok_sum(dgr)
        add_small(SM_AQ, _unspread_row(dg, LAY_ROPE_A))
        dxn = dxn + back(QA0, KA0)
        dg = jnp.zeros((1, LANE), F32)
        for h in range(A_KV):
            dk = dka_ref[A_GROUP * h]
            dv = dva_ref[A_GROUP * h]
            for g in range(1, A_GROUP):
                dk = dk + dka_ref[A_GROUP * h + g]
                dv = dv + dva_ref[A_GROUP * h + g]
            dn = _rope_bwd(dk * LN2, ca, sa)
            dx, dgr = _rms_bwd(dn, pre_ref[:, KA0 + LANE * h:KA0 + LANE * (h + 1)], gak_ref[...], A_DIM)
            dproj_ref[:, KA0 + LANE * h:KA0 + LANE * (h + 1)] = dx.astype(BF16)
            dproj_ref[:, VA0 + LANE * h:VA0 + LANE * (h + 1)] = dv.astype(BF16)
            dg = dg + tok_sum(dgr)
        add_small(SM_AK, _unspread_row(dg, LAY_ROPE_A))
        dxn = dxn + back(KA0, GA0)
        dg = jnp.zeros((1, LANE), F32)
        for h in range(B_HEADS):
            cols = slice(LANE * h, LANE * (h + 1))
            dn = _rope_bwd(dqb_ref[h] * SCALE_B, cb, sb)
            dx, dgr = _rms_bwd(dn, qbpre_ref[:, cols], gbq_ref[...], B_QK)
            dqbpre_ref[:, cols] = dx.astype(BF16)
            dg = dg + tok_sum(dgr)
        add_small(SM_BQ, _unspread_row(dg, LAY_ROPE_B))
        dcq = _nt(dqbpre_ref[...], wuq_ref[...])
        dx, dgr = _rms_bwd(dcq, pre_ref[:, VA0:VA0 + B_Q_RANK], gcq_ref[...], B_Q_RANK)
        dproj_ref[:, CQ0:CQ0 + B_Q_RANK] = dx.astype(BF16)
        add_small(SM_CQ, tok_sum(dgr))
        dxn = dxn + back(CQ0, CKV0)
        dg = jnp.zeros((1, LANE), F32)
        dkr = jnp.zeros((tm, LANE), F32)
        for h in range(B_HEADS):
            cols = slice(LANE * h, LANE * (h + 1))
            dn = _rope_bwd(dkb_ref[h] * LN2, cb, sb)
            dx, dgr = _rms_bwd(dn, kbpre_ref[:, cols], gbk_ref[...], B_QK)
            dkvb_ref[:, cols] = jnp.where(nope_lanes, dx, 0.0).astype(BF16)
            dkvb_ref[:, B_HEADS * LANE + LANE * h:B_HEADS * LANE + LANE * (h + 1)] = dvb_ref[h].astype(BF16)
            dkr = dkr + dx
            dg = dg + tok_sum(dgr)
        add_small(SM_BK, _unspread_row(dg, LAY_ROPE_B))
        dproj_ref[:, KR0:KR0 + LANE] = jnp.where(_lanes_of(lane, LAY_KR), dkr, 0.0).astype(BF16)
        dckv = _nt(dkvb_ref[...], wukv_ref[...])
        dx, dgr = _rms_bwd(dckv, pre_ref[:, VA0 + B_Q_RANK:N_PRE], gckv_ref[...], B_KV_RANK)
        dproj_ref[:, CKV0:CKV0 + B_KV_RANK] = dx.astype(BF16)
        add_small(SM_CKV, tok_sum(dgr))
        dxn = dxn + back(CKV0, N_EXT)
        dx, dgr = _rms_bwd(dxn, x_ref[...], gin_ref[...], D_MODEL)
        gx_ref[...] = dh_ref[...] + dx
        add_small(SM_IN, tok_sum(dgr))

    row = lambda w: pl.BlockSpec((tm, w), lambda i: (i, 0))
    heads = lambda n: pl.BlockSpec((n, tm, LANE), lambda i: (0, i, 0))
    return _pallas(
        body, name="post", grid=(nt,),
        in_specs=[row(D_MODEL), row(D_MODEL), row(N_PRE), row(B_HEADS * LANE), row(B_HEADS * LANE), row(N_GATE),
                  heads(A_HEADS), heads(A_HEADS), heads(A_HEADS), heads(B_HEADS), heads(B_HEADS), heads(B_HEADS),
                  _full(loss_part.shape)] + _table_specs(tm)
                 + [_resident(w_in_ext.shape), _resident(w_uq_pad.shape), _resident(w_ukv_ext.shape)]
                 + [_full(g.shape) for g in gains],
        out_specs=[row(D_MODEL), row(N_EXT), row(B_HEADS * LANE), row(2 * B_HEADS * LANE), _full((SM_ROWS, SM_W))],
        out_shape=[jax.ShapeDtypeStruct((s_len, D_MODEL), F32), jax.ShapeDtypeStruct((s_len, N_EXT), BF16),
                   jax.ShapeDtypeStruct((s_len, B_HEADS * LANE), BF16),
                   jax.ShapeDtypeStruct((s_len, 2 * B_HEADS * LANE), BF16),
                   jax.ShapeDtypeStruct((SM_ROWS, SM_W), F32)],
        compiler_params=_params(("arbitrary",), VMEM_LIMIT),
    )(x, dh, pre, qbpre, kbpre, dgate, dqa, dka, dva, dqb, dkb, dvb, loss_part, *tabs,
      w_in_ext, w_uq_pad, w_ukv_ext, *gains)


def _grad_w(a_t, b, tn, ts, name):
    m, s_len = a_t.shape
    n = b.shape[1]

    def body(a_ref, b_ref, o_ref):
        @pl.when(pl.program_id(1) == 0)
        def _():
            o_ref[...] = jnp.zeros_like(o_ref)

        o_ref[...] += _nn(a_ref[...], b_ref[...].astype(BF16))

    return _pallas(
        body, name=name, grid=(n // tn, s_len // ts),
        in_specs=[pl.BlockSpec((m, ts), lambda j, t: (0, t)), pl.BlockSpec((ts, tn), lambda j, t: (t, j))],
        out_specs=pl.BlockSpec((m, tn), lambda j, t: (0, j)),
        out_shape=jax.ShapeDtypeStruct((m, n), F32),
        compiler_params=_params(("parallel", "arbitrary"), VMEM_MID),
    )(a_t, b)


def _grad_w_pairs(pairs, ts, name):
    s_len = pairs[0][0].shape[1]
    n_p = len(pairs)

    def body(*refs):
        for a_ref, b_ref, o_ref in zip(refs[0:2 * n_p:2], refs[1:2 * n_p:2], refs[2 * n_p:]):
            @pl.when(pl.program_id(0) == 0)
            def _():
                o_ref[...] = jnp.zeros_like(o_ref)

            o_ref[...] += _nn(a_ref[...], b_ref[...].astype(BF16))

    in_specs, flat = [], []
    for a_t, b in pairs:
        in_specs += [pl.BlockSpec((a_t.shape[0], ts), lambda t: (0, t)), pl.BlockSpec((ts, b.shape[1]), lambda t: (t, 0))]
        flat += [a_t, b]
    return _pallas(
        body, name=name, grid=(s_len // ts,),
        in_specs=in_specs,
        out_specs=[_full((a_t.shape[0], b.shape[1])) for a_t, b in pairs],
        out_shape=[jax.ShapeDtypeStruct((a_t.shape[0], b.shape[1]), F32) for a_t, b in pairs],
        compiler_params=_params(("arbitrary",), VMEM_MID),
    )(*flat)


def _adam_math(w, g, m, v):
    nm = ADAM_B1 * m + (1.0 - ADAM_B1) * g
    nv = ADAM_B2 * v + (1.0 - ADAM_B2) * (g * g)
    m_hat = nm / (1.0 - ADAM_B1 ** ADAM_STEP)
    v_hat = nv / (1.0 - ADAM_B2 ** ADAM_STEP)
    return -ADAM_LR * (m_hat / (jnp.sqrt(v_hat) + ADAM_EPS) + ADAM_WD * w), nm, nv


def _adamw_rows(w, g, m, v, tr):
    rows, cols = w.shape

    def body(w_ref, g_ref, m_ref, v_ref, d_ref, nm_ref, nv_ref):
        d_ref[...], nm_ref[...], nv_ref[...] = _adam_math(w_ref[...], g_ref[...], m_ref[...], v_ref[...])

    blk = pl.BlockSpec((tr, cols), lambda i: (i, 0))
    shape = jax.ShapeDtypeStruct((rows, cols), F32)
    return _pallas(
        body, name="adamw_w_in", grid=(rows // tr,),
        in_specs=[blk] * 4, out_specs=[blk] * 3, out_shape=[shape] * 3,
        compiler_params=_params(("parallel",), VMEM_SMALL),
    )(w, g, m, v)


def _adamw_rest(bigs, smalls, g_small):
    nb, ns = len(bigs), len(smalls)

    def body(*refs):
        ins, outs = refs[:4 * nb + 3 * ns + 1], refs[4 * nb + 3 * ns + 1:]
        for i in range(nb):
            w_ref, g_ref, m_ref, v_ref = ins[4 * i:4 * i + 4]
            d_ref, nm_ref, nv_ref = outs[3 * i:3 * i + 3]
            d_ref[...], nm_ref[...], nv_ref[...] = _adam_math(w_ref[...], g_ref[...], m_ref[...], v_ref[...])
        gs_ref = ins[-1]
        for i in range(ns):
            w_ref, m_ref, v_ref = ins[4 * nb + 3 * i:4 * nb + 3 * i + 3]
            g_ref, d_ref, nm_ref, nv_ref = outs[3 * nb + 4 * i:3 * nb + 4 * i + 4]
            g = gs_ref[i:i + 1, 0:w_ref.shape[1]]
            g_ref[...] = g
            d_ref[...], nm_ref[...], nv_ref[...] = _adam_math(w_ref[...], g, m_ref[...], v_ref[...])

    flat_in = [a for quad in bigs for a in quad] + [a for tri in smalls for a in tri] + [g_small]
    out_shape = ([jax.ShapeDtypeStruct(q[0].shape, F32) for q in bigs for _ in range(3)]
                 + [jax.ShapeDtypeStruct(t[0].shape, F32) for t in smalls for _ in range(4)])
    return _pallas(
        body, name="adamw_rest",
        in_specs=[pl.BlockSpec(memory_space=pltpu.VMEM)] * len(flat_in),
        out_specs=[pl.BlockSpec(memory_space=pltpu.VMEM)] * len(out_shape),
        out_shape=out_shape,
        compiler_params=_params(vmem=VMEM_SMALL),
    )(*flat_in)


def _place(pieces, n):
    out, at = [], 0
    for lane0, arr in sorted(pieces, key=lambda p: p[0]):
        out += [jnp.zeros((n, lane0 - at), F32), arr]
        at = lane0 + arr.shape[1]
    return jnp.concatenate(out + [jnp.zeros((n, LANE - at), F32)], axis=1)


def _rope_tables(s_len, tm):
    rows = s_len // GRID_W
    row = jnp.arange(rows, dtype=F32)
    col = jnp.arange(GRID_W, dtype=F32)

    def lay(dim, layout, first_dim, ones):
        half = dim // 2
        inv = 1.0 / (ROPE_THETA ** (jnp.arange(0, half, 2, dtype=F32) / half))
        ang_r, ang_c = row[:, None] * inv[None, :], col[:, None] * inv[None, :]
        at = {a - first_dim: lane0 for a, _, lane0 in layout}
        q = dim // 4
        r1, r2, c1, c2 = at[0], at[q], at[2 * q], at[3 * q]
        cos_r = _place([(r1, jnp.cos(ang_r)), (r2, jnp.cos(ang_r))], rows)
        sin_r = _place([(r1, -jnp.sin(ang_r)), (r2, jnp.sin(ang_r))], rows)
        cos_c = _place([(c1, jnp.cos(ang_c)), (c2, jnp.cos(ang_c))] + [(l0, jnp.ones((GRID_W, n), F32)) for _, n, l0 in ones],
                       GRID_W)
        sin_c = _place([(c1, -jnp.sin(ang_c)), (c2, jnp.sin(ang_c))], GRID_W)
        by_block = (s_len // tm, tm // GRID_W, LANE)
        return cos_r.reshape(by_block), cos_c, sin_r.reshape(by_block), sin_c

    return lay(A_DIM, LAY_ROPE_A, 0, ()) + lay(B_ROPE, LAY_KR, 0, LAY_NOPE)


def _spread(w, n_heads, dim, axis, layout):
    w3 = w.reshape(w.shape[:axis] + (n_heads, dim) + w.shape[axis + 1:])
    out, at = [], 0

    def zeros(n):
        return jnp.zeros(w3.shape[:axis + 1] + (n,) + w3.shape[axis + 2:], w.dtype)

    for a0, n, lane0 in sorted(layout, key=lambda seg: seg[2]):
        out += [zeros(lane0 - at), lax.slice_in_dim(w3, a0, a0 + n, axis=axis + 1)]
        at = lane0 + n
    out = jnp.concatenate(out + [zeros(LANE - at)], axis=axis + 1)
    return out.reshape(w.shape[:axis] + (n_heads * LANE,) + w.shape[axis + 1:])


def _unspread(w, n_heads, dim, axis, layout):
    w3 = w.reshape(w.shape[:axis] + (n_heads, LANE) + w.shape[axis + 1:])
    parts = [lax.slice_in_dim(w3, lane0, lane0 + n, axis=axis + 1) for _, n, lane0 in sorted(layout)]
    out = parts[0] if len(parts) == 1 else jnp.concatenate(parts, axis=axis + 1)
    return out.reshape(w.shape[:axis] + (n_heads * dim,) + w.shape[axis + 1:])


def _head_cols(first, n_heads, dim, layout):
    out = np.full((n_heads * LANE,), -1, np.int32)
    for h in range(n_heads):
        for a0, n, lane0 in layout:
            out[h * LANE + lane0:h * LANE + lane0 + n] = first + h * dim + a0 + np.arange(n)
    return out


def _inverse(src, n):
    dst = np.full((n,), -1, np.int32)
    dst[src[src >= 0]] = np.nonzero(src >= 0)[0]
    return dst


def _column_maps():
    a_w, kv_w = A_HEADS * A_DIM, A_KV * A_DIM
    o_g = a_w + 2 * kv_w
    o_cq = o_g + a_w
    o_kr = o_cq + B_Q_RANK + B_KV_RANK
    src_in = np.concatenate([
        _head_cols(0, A_HEADS, A_DIM, LAY_ROPE_A), _head_cols(a_w, A_KV, A_DIM, LAY_ROPE_A),
        _head_cols(a_w + kv_w, A_KV, A_DIM, LAY_PLAIN_A), _head_cols(o_g, A_HEADS, A_DIM, LAY_PLAIN_A),
        np.arange(o_kr + B_ROPE, N_IN), np.arange(o_cq, o_kr), _head_cols(o_kr, 1, B_ROPE, LAY_KR)]).astype(np.int32)
    src_uq = _head_cols(0, B_HEADS, B_QK, LAY_ROPE_B)
    per = B_NOPE + B_V
    src_ukv = np.concatenate([_head_cols(0, B_HEADS, per, LAY_NOPE),
                              _head_cols(B_NOPE, B_HEADS, per, ((0, B_V, 0),))]).astype(np.int32)
    assert len(src_in) == N_EXT
    return src_in, src_uq, src_ukv


def _round_up(n, m):
    return (n + m - 1) // m * m


def _permute_cols(xs, maps, stacks, name):
    maps = [np.asarray(m, np.int32) for m in maps]
    n_arr = len(xs)

    def block(ref, b):
        if len(ref.shape) == 2:
            return ref.at[:, b * LANE:(b + 1) * LANE]
        per = ref.shape[2] // LANE
        return ref.at[b // per, :, (b % per) * LANE:(b % per + 1) * LANE]

    def body(*refs):
        row = lax.broadcasted_iota(jnp.int32, (LANE, LANE), 0)
        for x_ref, src_ref, o_ref, src in zip(refs[:n_arr], refs[n_arr:2 * n_arr], refs[2 * n_arr:], maps):
            for c in range(len(src) // LANE):
                want = src[c * LANE:(c + 1) * LANE]
                if want[0] >= 0 and want[0] % LANE == 0 and np.array_equal(want, want[0] + np.arange(LANE)):
                    block(o_ref, c)[...] = block(x_ref, int(want[0]) // LANE)[...]
                    continue
                acc = jnp.zeros((x_ref.shape[-2], LANE), F32)
                for kb in sorted({int(v) // LANE for v in want if v >= 0}):
                    sel = jnp.where(row + kb * LANE == src_ref[:, c * LANE:(c + 1) * LANE], 1.0, 0.0).astype(BF16)
                    part = block(x_ref, kb)[...]
                    if part.dtype == BF16:
                        acc = acc + _nn(part, sel)
                    else:
                        hi = part.astype(BF16)
                        rest = part - hi.astype(F32)
                        mid = rest.astype(BF16)
                        low = (rest - mid.astype(F32)).astype(BF16)
                        acc = acc + ((_nn(hi, sel) + _nn(mid, sel)) + _nn(low, sel))
                block(o_ref, c)[...] = acc.astype(o_ref.dtype)

    def out_shape(x, m, stack):
        rows = x.shape[-2]
        return (rows, len(m)) if stack is None else (stack, rows, len(m) // stack)

    return _pallas(
        body, name=name,
        out_shape=[jax.ShapeDtypeStruct(out_shape(x, m, st), x.dtype) for x, m, st in zip(xs, maps, stacks)],
        compiler_params=_params(vmem=VMEM_MID),
    )(*xs, *[jnp.asarray(m).reshape(1, -1) for m in maps])


def _pad_cols(w):
    return jnp.pad(w, ((0, 0), (0, _round_up(w.shape[1], LANE) - w.shape[1])))


def _in_stack(cols, width):
    cols = np.asarray(cols)
    return np.where(cols < 0, -1, cols // width * _round_up(width, LANE) + cols % width).astype(np.int32)


def _ext_weights(g_in, g_uq, g_ukv, g_out):
    src_in, src_uq, src_ukv = _column_maps()
    w_uq = g_uq.reshape(B_Q_RANK, B_HEADS * B_QK)
    w_out = g_out.reshape(D_MODEL, D_MODEL)
    w_in_ext, w_uq_pad, w_ukv_ext = _permute_cols(
        [g_in, w_uq, g_ukv], [_in_stack(src_in, SH_IN[1]), src_uq, _in_stack(src_ukv, SH_UKV[1])], [None] * 3, "lay_out_weights")
    a_w = A_HEADS * A_DIM
    w_out_ext = jnp.concatenate([_spread(w_out[:a_w], A_HEADS, A_DIM, 0, LAY_PLAIN_A), w_out[a_w:]], axis=0)
    return w_in_ext, w_uq_pad, w_ukv_ext, w_out_ext


def _fold_grads(d_in_ext, d_uq_pad, d_ukv_ext, d_out_ext):
    src_in, src_uq, src_ukv = _column_maps()

    def back(src, n, width):
        inv = _inverse(src, n)
        wide = _round_up(width, LANE)
        out = np.full((n // width * wide,), -1, np.int32)
        for j in range(n // width):
            out[j * wide:j * wide + width] = inv[j * width:(j + 1) * width]
        return out

    n_uq, n_ukv = B_HEADS * B_QK, B_HEADS * (B_NOPE + B_V)
    d_in, d_uq, d_ukv = _permute_cols(
        [d_in_ext, d_uq_pad, d_ukv_ext], [back(src_in, N_IN, SH_IN[1]), _inverse(src_uq, n_uq), back(src_ukv, n_ukv, SH_UKV[1])],
        [N_CHIPS, None, N_CHIPS], "fold_grads")
    d_out = jnp.concatenate([_unspread(d_out_ext[:A_HEADS * LANE], A_HEADS, A_DIM, 0, LAY_PLAIN_A), d_out_ext[A_HEADS * LANE:]], axis=0)
    return d_in, d_uq.reshape((N_CHIPS,) + SH_UQ), d_ukv, d_out.reshape((N_CHIPS,) + SH_OUT)


def kernel(x, norm_in, w_in, a_q_norm, a_k_norm, b_cq_norm, b_ckv_norm, w_uq, w_ukv, b_q_norm, b_k_norm, w_out, loss_target, m_norm_in, m_w_in, m_a_q_norm, m_a_k_norm, m_b_cq_norm, m_b_ckv_norm, m_w_uq, m_w_ukv, m_b_q_norm, m_b_k_norm, m_w_out, v_norm_in, v_w_in, v_a_q_norm, v_a_k_norm, v_b_cq_norm, v_b_ckv_norm, v_w_uq, v_w_ukv, v_b_q_norm, v_b_k_norm, v_w_out):
    s_len = x.shape[1]
    xs, ts = x[0], loss_target[0]
    tm = min(256, s_len)
    tq, tk_f = min(512, s_len // 2), min(2048, s_len // 2)
    tq_b, tk_b = min(1024, s_len // 2), min(512, s_len)
    tiles_f = min(4, s_len // tq)
    tiles_b = min(2, s_len // tk_b)

    w_in_ext, w_uq_pad, w_ukv_ext, w_out_ext = _ext_weights(
        *_gather_weights((_pad_cols(w_in[0]), w_uq[0], _pad_cols(w_ukv[0]), w_out[0])))
    gains = (norm_in, _spread(a_q_norm, 1, A_DIM, 1, LAY_ROPE_A), _spread(a_k_norm, 1, A_DIM, 1, LAY_ROPE_A), b_cq_norm, b_ckv_norm,
             _spread(b_q_norm, 1, B_QK, 1, LAY_ROPE_B), _spread(b_k_norm, 1, B_QK, 1, LAY_ROPE_B))
    tabs = _rope_tables(s_len, tm)

    (xn_t, gates, pre, qbpre, kbpre, cq_t, ckv_t, qa, ka, va, qb, kb, vb) = _pre(
        xs, tabs, w_in_ext, w_uq_pad, w_ukv_ext, gains, tm)
    o_a, lse_a = _attn_fwd(qa, ka, va, A_GROUP, A_DIM, tq, tk_f, tiles_f, "attn_fwd_a")
    o_b, lse_b = _attn_fwd(qb, kb, vb, 1, B_V, tq, tk_f, tiles_f, "attn_fwd_b")
    y_t, dh, dgate, do_a, do_b, delta, loss_part = _mid(xs, ts, o_a, o_b, gates, w_out_ext, min(512, s_len))

    def stat(a):
        return a.reshape(a.shape[0], s_len // tq_b, 1, tq_b)

    dqa, dka, dva = _attn_bwd(qa, ka, va, do_a, stat(lse_a), stat(delta[:A_HEADS]), A_GROUP, tq_b, tk_b, tiles_b, "attn_bwd_a")
    dqb, dkb, dvb = _attn_bwd(qb, kb, vb, do_b, stat(lse_b), stat(delta[A_HEADS:A_HEADS + B_HEADS]), 1, tq_b, tk_b,
                              tiles_b, "attn_bwd_b")
    grad_x, dproj, dqbpre, dkvb, d_small = _post(
        xs, dh, pre, qbpre, kbpre, dgate, dqa, dka, dva, dqb, dkb, dvb, loss_part, tabs,
        w_in_ext, w_uq_pad, w_ukv_ext, gains, tm)

    ts_w = min(2048, s_len)
    d_in_ext = _grad_w(xn_t, dproj, 768, min(2 * ts_w, s_len), "grad_w_in")
    d_out_ext = _grad_w(y_t, dh, 512, ts_w, "grad_w_out")
    d_uq_pad, d_ukv_ext = _grad_w_pairs([(cq_t, dqbpre), (ckv_t, dkvb)], ts_w, "grad_w_mla")

    g_in_p, g_uq, g_ukv_p, g_out, g_small = _reduce_grads(_fold_grads(d_in_ext, d_uq_pad, d_ukv_ext, d_out_ext), d_small)
    g_in, g_ukv = g_in_p[:, :SH_IN[1]], g_ukv_p[:, :SH_UKV[1]]
    d_in, nm_in, nv_in = (a.T for a in _adamw_rows(w_in[0].T, g_in_p.T[:SH_IN[1]], m_w_in[0].T, v_w_in[0].T, SH_IN[1] // 7))
    rest = _adamw_rest(
        [(w_uq[0], g_uq, m_w_uq[0], v_w_uq[0]), (w_ukv[0], g_ukv, m_w_ukv[0], v_w_ukv[0]),
         (w_out[0], g_out, m_w_out[0], v_w_out[0])],
        [(norm_in, m_norm_in, v_norm_in), (a_q_norm, m_a_q_norm, v_a_q_norm), (a_k_norm, m_a_k_norm, v_a_k_norm),
         (b_cq_norm, m_b_cq_norm, v_b_cq_norm), (b_ckv_norm, m_b_ckv_norm, v_b_ckv_norm),
         (b_q_norm, m_b_q_norm, v_b_q_norm), (b_k_norm, m_b_k_norm, v_b_k_norm)], g_small)
    (d_uq, nm_uq, nv_uq), (d_ukv, nm_ukv, nv_ukv), (d_out, nm_out, nv_out) = (rest[3 * i:3 * i + 3] for i in range(3))
    sm = [rest[9 + 4 * i:9 + 4 * i + 4] for i in range(7)]

    def leaves(k, p_in, p_uq, p_ukv, p_out):
        return [sm[SM_IN][k], p_in[None], sm[SM_AQ][k], sm[SM_AK][k], sm[SM_CQ][k], sm[SM_CKV][k], p_uq[None], p_ukv[None],
                sm[SM_BQ][k], sm[SM_BK][k], p_out[None]]

    return (g_small[SM_LOSS, 0], grad_x[None], *leaves(0, g_in, g_uq, g_ukv, g_out), *leaves(1, d_in, d_uq, d_ukv, d_out),
            *leaves(2, nm_in, nm_uq, nm_ukv, nm_out), *leaves(3, nv_in, nv_uq, nv_ukv, nv_out))
```

```python
import jax
import jax.numpy as jnp
import numpy as np
from jax import lax
from jax.experimental import pallas as pl
from jax.experimental.pallas import tpu as pltpu

F32 = jnp.float32
BF16 = jnp.bfloat16
MESH = pl.DeviceIdType.MESH

D_MODEL = 1024
GRID_W = 64
ROPE_THETA = 10000.0
EPS = 1e-6
A_HEADS, A_KV, A_DIM = 8, 2, 64
A_GROUP = A_HEADS // A_KV
B_HEADS, B_NOPE, B_ROPE, B_V = 4, 64, 32, 128
B_QK = B_NOPE + B_ROPE
B_Q_RANK, B_KV_RANK = 384, 256
N_IN = 2464
SCALE_A = 1.0 / float(np.sqrt(A_DIM))
SCALE_B = 1.0 / float(np.sqrt(B_QK))
LOG2E = float(np.log2(np.e))
LN2 = float(np.log(2.0))
ADAM_LR, ADAM_B1, ADAM_B2, ADAM_EPS, ADAM_WD, ADAM_STEP = 0.001, 0.9, 0.999, 1e-08, 0.01, 10

LANE = 128
VMEM_BYTES = 64 * 1024 * 1024
VMEM_LIMIT = VMEM_BYTES - 8 * 1024 * 1024
VMEM_MID = 48 * 1024 * 1024
VMEM_SMALL = 32 * 1024 * 1024

QA0 = 0
KA0 = QA0 + A_HEADS * LANE
VA0 = KA0 + A_KV * LANE
GA0 = VA0 + A_KV * LANE
GB0 = GA0 + A_HEADS * LANE
CQ0 = GB0 + B_HEADS * LANE
CKV0 = CQ0 + B_Q_RANK
KR0 = CKV0 + B_KV_RANK
N_EXT = KR0 + LANE
N_GATE = (A_HEADS + B_HEADS) * LANE
DELTA_ROWS = 16
N_PRE = KA0 + A_KV * LANE + B_Q_RANK + B_KV_RANK

ROT = LANE // 2
_QA = A_DIM // 4
_QB = B_ROPE // 4
LAY_PLAIN_A = ((0, A_DIM, 0),)
LAY_ROPE_A = ((0, _QA, 0), (2 * _QA, _QA, _QA), (_QA, _QA, ROT), (3 * _QA, _QA, ROT + _QA))
LAY_KR = ((0, _QB, 0), (2 * _QB, _QB, _QB), (_QB, _QB, ROT), (3 * _QB, _QB, ROT + _QB))
LAY_NOPE = ((0, B_NOPE // 2, 2 * _QB), (B_NOPE // 2, B_NOPE // 2, ROT + 2 * _QB))
LAY_ROPE_B = LAY_NOPE + tuple((B_NOPE + a, n, at) for a, n, at in LAY_KR)

N_CHIPS = 4
SH_IN = (D_MODEL, N_IN // N_CHIPS)
SH_UQ = (B_Q_RANK // N_CHIPS, B_HEADS * B_QK)
SH_UKV = (B_KV_RANK, B_HEADS * (B_NOPE + B_V) // N_CHIPS)
SH_OUT = (D_MODEL // N_CHIPS, D_MODEL)
SM_ROWS, SM_W = 16, D_MODEL
SM_IN, SM_AQ, SM_AK, SM_CQ, SM_CKV, SM_BQ, SM_BK, SM_LOSS = range(8)
F32_ROWS, BF16_ROWS = 8, 16


def _pallas(body, **kw):
    return pl.pallas_call(body, **kw)


def _params(sem=None, vmem=None):
    return pltpu.CompilerParams(dimension_semantics=sem, vmem_limit_bytes=vmem)


def _rms_fwd(x, g, n):
    r = lax.rsqrt(jnp.sum(x * x, axis=-1, keepdims=True) * (1.0 / n) + EPS)
    return x * r * g


def _rms_bwd(dy, x, g, n):
    u = dy * g
    r = lax.rsqrt(jnp.sum(x * x, axis=-1, keepdims=True) * (1.0 / n) + EPS)
    ux = jnp.sum(u * x, axis=-1, keepdims=True)
    xhat = x * r
    dx = r * (u - xhat * (r * ux * (1.0 / n)))
    return dx, dy * xhat


def _rope_fwd(y, cos, sin):
    return y * cos + pltpu.roll(y, ROT, 1) * sin


def _rope_bwd(d, cos, sin):
    return d * cos - pltpu.roll(d, ROT, 1) * sin


def _token_tables(refs):
    out = []
    for r_ref, c_ref in zip(refs[0::2], refs[1::2]):
        r, c = r_ref[...], c_ref[...]
        out.append(jnp.concatenate([r[k:k + 1, :] + c for k in range(r.shape[0])], axis=0))
    return out


def _lanes_of(lane, layout):
    m = None
    for _, n, at in layout:
        seg = (lane >= at) & (lane < at + n)
        m = seg if m is None else (m | seg)
    return m


def _unspread_row(v, layout):
    v8 = jnp.broadcast_to(v, (F32_ROWS, LANE))
    lane = lax.broadcasted_iota(jnp.int32, v8.shape, 1)
    out = jnp.zeros_like(v8)
    for a, n, at in layout:
        moved = v8 if a == at else pltpu.roll(v8, (a - at) % LANE, 1)
        out = jnp.where((lane >= a) & (lane < a + n), moved, out)
    return out[0:1, :]


def _nt(a, b):
    return lax.dot_general(a, b, (((1,), (1,)), ((), ())), preferred_element_type=F32)


def _tn(a, b):
    return lax.dot_general(a, b, (((0,), (0,)), ((), ())), preferred_element_type=F32)


def _nn(a, b):
    return jnp.dot(a, b, preferred_element_type=F32)


def _block_rows(i, size):
    if isinstance(i, int):
        return pl.ds(i * size, size)
    return pl.ds(pl.multiple_of(i * size, size), size)


MAX_STATIC_BLOCKS = 32


def _three_stage(n, first, second, third):
    assert n >= 2 and n % 2 == 0
    first(0, 0)
    first(1, 1)
    second(0, 0)
    if n <= MAX_STATIC_BLOCKS:
        for i in range(1, n - 1):
            first(i + 1, (i + 1) % 2)
            second(i, i % 2)
            third(i - 1, (i - 1) % 2)
    else:
        def pair(t, carry):
            i = 2 * t + 1
            first(i + 1, 0)
            second(i, 1)
            third(i - 1, 0)
            first(i + 2, 1)
            second(i + 1, 0)
            third(i, 1)
            return carry

        lax.fori_loop(0, (n - 2) // 2, pair, 0)
    second(n - 1, 1)
    third(n - 2, 0)
    third(n - 1, 1)


def _full(shape):
    return pl.BlockSpec(shape, lambda *_: (0,) * len(shape))


def _table_specs(tm):
    return [pl.BlockSpec((None, tm // GRID_W, LANE), lambda i: (i, 0, 0)), _full((GRID_W, LANE))] * 4


def _resident(shape):
    return pl.BlockSpec(shape, lambda *_: (0,) * len(shape), pipeline_mode=pl.Buffered(1))


def _gather_weights(shards):
    n = len(shards)
    halves = [w.shape[0] // 2 for w in shards]

    def body(*refs):
        w_refs, out_refs, (send_sems, recv_sems) = refs[:n], refs[n:2 * n], refs[2 * n:]
        x, y, c = lax.axis_index("x"), lax.axis_index("y"), lax.axis_index("c")
        sibling = (x, y, 1 - c)
        chips = [(1 - x, y), (x, 1 - y), (1 - x, 1 - y)]
        me = 2 * x + y

        def copy(a, k, j, hc, to):
            part = out_refs[a].at[j, pl.ds(pl.multiple_of(hc * halves[a], BF16_ROWS), halves[a]), :]
            return pltpu.make_async_remote_copy(
                src_ref=part, dst_ref=part, send_sem=send_sems.at[6 * a + k], recv_sem=recv_sems.at[6 * a + k],
                device_id=to, device_id_type=MESH)

        started = []
        for a in range(n):
            out_refs[a][me] = w_refs[a][...].astype(BF16)
            for k, chip in enumerate(chips):
                started.append(copy(a, k, me, c, (*chip, c)))
                started[-1].start()
        for k, chip in enumerate(chips):
            for a in range(n):
                copy(a, k, 2 * chip[0] + chip[1], c, (*chip, c)).wait_recv()
                started.append(copy(a, 3 + k, 2 * chip[0] + chip[1], c, sibling))
                started[-1].start()
        for k, chip in enumerate(chips):
            for a in range(n):
                copy(a, 3 + k, 2 * chip[0] + chip[1], 1 - c, sibling).wait_recv()
        for cp in started:
            cp.wait_send()

    return _pallas(
        body, name="gather_weights",
        out_shape=[jax.ShapeDtypeStruct((N_CHIPS,) + w.shape, BF16) for w in shards],
        in_specs=[pl.BlockSpec(memory_space=pltpu.VMEM)] * n,
        out_specs=[pl.BlockSpec(memory_space=pltpu.VMEM)] * n,
        scratch_shapes=[pltpu.SemaphoreType.DMA((6 * n,)), pltpu.SemaphoreType.DMA((6 * n,))],
        compiler_params=_params(vmem=VMEM_SMALL),
    )(*shards)


def _reduce_grads(parts, small):
    n_big = len(parts)
    n = n_big + 1
    shapes = [p.shape[1:] for p in parts] + [small.shape]
    halves = [sh[0] // 2 for sh in shapes]

    def body(*refs):
        p_refs, out_refs, rec_a, rec_b = refs[:n], refs[n:2 * n], refs[2 * n:3 * n], refs[3 * n:4 * n]
        send_b = refs[4 * n:4 * n + n_big]
        sa_send, sa_recv, sb_send, sb_recv, sc_send, sc_recv = refs[4 * n + n_big:]
        x, y, c = lax.axis_index("x"), lax.axis_index("y"), lax.axis_index("c")
        sibling = (x, y, 1 - c)
        me = 2 * x + y

        def rows(a, hc):
            return pl.ds(pl.multiple_of(hc * halves[a], F32_ROWS), halves[a])

        def partial(a, j, hc):
            return p_refs[a].at[j, rows(a, hc), :] if a < n_big else p_refs[a].at[rows(a, hc), :]

        def copy_a(a, j):
            return pltpu.make_async_remote_copy(
                src_ref=partial(a, j, 1 - c), dst_ref=rec_a[a].at[j],
                send_sem=sa_send.at[N_CHIPS * a + j], recv_sem=sa_recv.at[N_CHIPS * a + j],
                device_id=sibling, device_id_type=MESH)

        def copy_b(a, r):
            j = me ^ r
            k = (N_CHIPS - 1) * a + r - 1
            return pltpu.make_async_remote_copy(
                src_ref=(send_b[a] if a < n_big else rec_a[a]).at[j], dst_ref=rec_b[a].at[r],
                send_sem=sb_send.at[k], recv_sem=sb_recv.at[k], device_id=(j // 2, j % 2, c), device_id_type=MESH)

        def copy_c(a):
            return pltpu.make_async_remote_copy(
                src_ref=out_refs[a].at[rows(a, c), :], dst_ref=out_refs[a].at[rows(a, c), :],
                send_sem=sc_send.at[a], recv_sem=sc_recv.at[a], device_id=sibling, device_id_type=MESH)

        for a in range(n):
            for j in range(N_CHIPS):
                copy_a(a, j).start()
        for r in range(1, N_CHIPS):
            j = me ^ r
            for a in range(n):
                copy_a(a, j).wait_recv()
                chip_part = rec_a[a][j] + partial(a, j, c)[...]
                if a < n_big:
                    send_b[a][j] = chip_part.astype(BF16)
                else:
                    rec_a[a][j] = chip_part
                copy_b(a, r).start()
        for a in range(n):
            copy_a(a, me).wait_recv()
            rec_b[a][0] = (rec_a[a][me] + partial(a, me, c)[...]).astype(rec_b[a].dtype)
        for a in range(n):
            for r in range(1, N_CHIPS):
                copy_b(a, r).wait_recv()
            total = rec_b[a][me].astype(F32)
            for j in range(1, N_CHIPS):
                total = total + rec_b[a][j ^ me].astype(F32)
            out_refs[a][rows(a, c), :] = total
            copy_c(a).start()
        for a in range(n):
            copy_c(a).wait_recv()
        for a in range(n):
            for j in range(N_CHIPS):
                copy_a(a, j).wait_send()
            for r in range(1, N_CHIPS):
                copy_b(a, r).wait_send()
            copy_c(a).wait_send()

    dma = pltpu.SemaphoreType.DMA
    return _pallas(
        body, name="reduce_grads",
        out_shape=[jax.ShapeDtypeStruct(sh, F32) for sh in shapes],
        in_specs=[pl.BlockSpec(memory_space=pltpu.VMEM)] * n,
        out_specs=[pl.BlockSpec(memory_space=pltpu.VMEM)] * n,
        scratch_shapes=[pltpu.VMEM((N_CHIPS, h) + sh[1:], F32) for h, sh in zip(halves, shapes)]
                       + [pltpu.VMEM((N_CHIPS, h) + sh[1:], BF16 if a < n_big else F32)
                          for a, (h, sh) in enumerate(zip(halves, shapes))]
                       + [pltpu.VMEM((N_CHIPS, h) + sh[1:], BF16) for h, sh in zip(halves[:n_big], shapes[:n_big])]
                       + [dma((N_CHIPS * n,)), dma((N_CHIPS * n,)), dma(((N_CHIPS - 1) * n,)), dma(((N_CHIPS - 1) * n,)),
                          dma((n,)), dma((n,))],
        compiler_params=_params(vmem=VMEM_LIMIT),
    )(*parts, small)


def _pre(x, tabs, w_in_ext, w_uq_pad, w_ukv_ext, gains, tm):
    s_len = x.shape[0]
    nt = s_len // tm

    def body(x_ref, car_ref, cac_ref, sar_ref, sac_ref, cbr_ref, cbc_ref, sbr_ref, sbc_ref, win_ref, wuq_ref, wukv_ref,
             gin_ref, gaq_ref, gak_ref, gcq_ref, gckv_ref, gbq_ref, gbk_ref,
             xn_ref, gates_ref, pre_ref, qbpre_ref, kbpre_ref, cq_ref, ckv_ref,
             qa_ref, ka_ref, va_ref, qb_ref, kb_ref, vb_ref, proj):
        xn = _rms_fwd(x_ref[...], gin_ref[...], D_MODEL)
        xn_ref[...] = jnp.transpose(xn).astype(BF16)
        proj[...] = _nn(xn.astype(BF16), win_ref[...])
        gates_ref[...] = proj[:, GA0:GA0 + N_GATE]
        pre_ref[:, 0:VA0] = proj[:, 0:VA0]
        pre_ref[:, VA0:N_PRE] = proj[:, CQ0:KR0]
        ca, sa, cb, sb = _token_tables((car_ref, cac_ref, sar_ref, sac_ref, cbr_ref, cbc_ref, sbr_ref, sbc_ref))
        lane = lax.broadcasted_iota(jnp.int32, (tm, LANE), 1)
        for h in range(A_HEADS):
            yq = _rms_fwd(proj[:, QA0 + LANE * h:QA0 + LANE * (h + 1)], gaq_ref[...], A_DIM)
            qa_ref[h] = (_rope_fwd(yq, ca, sa) * (SCALE_A * LOG2E)).astype(BF16)
        for h in range(A_KV):
            yk = _rms_fwd(proj[:, KA0 + LANE * h:KA0 + LANE * (h + 1)], gak_ref[...], A_DIM)
            ka_ref[h] = _rope_fwd(yk, ca, sa).astype(BF16)
            va_ref[h] = jnp.where(lane == A_DIM, 1.0, proj[:, VA0 + LANE * h:VA0 + LANE * (h + 1)]).astype(BF16)
        cq = _rms_fwd(proj[:, CQ0:CQ0 + B_Q_RANK], gcq_ref[...], B_Q_RANK)
        cq_ref[...] = jnp.transpose(cq).astype(BF16)
        qbpre_ref[...] = _nn(cq.astype(BF16), wuq_ref[...])
        ckv = _rms_fwd(proj[:, CKV0:CKV0 + B_KV_RANK], gckv_ref[...], B_KV_RANK)
        ckv_ref[...] = jnp.transpose(ckv).astype(BF16)
        kvb = _nn(ckv.astype(BF16), wukv_ref[...])
        kr = proj[:, KR0:KR0 + LANE]
        for h in range(B_HEADS):
            yq = _rms_fwd(qbpre_ref[:, LANE * h:LANE * (h + 1)], gbq_ref[...], B_QK)
            qb_ref[h] = (_rope_fwd(yq, cb, sb) * (SCALE_B * LOG2E)).astype(BF16)
            kp = kvb[:, LANE * h:LANE * (h + 1)] + kr
            kbpre_ref[:, LANE * h:LANE * (h + 1)] = kp
            kb_ref[h] = _rope_fwd(_rms_fwd(kp, gbk_ref[...], B_QK), cb, sb).astype(BF16)
            vb_ref[h, :, 0:LANE] = kvb[:, B_HEADS * LANE + LANE * h:B_HEADS * LANE + LANE * (h + 1)].astype(BF16)
            vb_ref[h, :, LANE:2 * LANE] = jnp.where(lane == 0, 1.0, 0.0).astype(BF16)

    row = lambda w: pl.BlockSpec((tm, w), lambda i: (i, 0))
    col = lambda w: pl.BlockSpec((w, tm), lambda i: (0, i))
    heads = lambda n: pl.BlockSpec((n, tm, LANE), lambda i: (0, i, 0))
    hs = lambda n: jax.ShapeDtypeStruct((n, s_len, LANE), BF16)
    return _pallas(
        body, name="pre", grid=(nt,),
        in_specs=[row(D_MODEL)] + _table_specs(tm)
                 + [_resident(w_in_ext.shape), _resident(w_uq_pad.shape), _resident(w_ukv_ext.shape)]
                 + [_full(g.shape) for g in gains],
        out_specs=[col(D_MODEL), row(N_GATE), row(N_PRE), row(B_HEADS * LANE), row(B_HEADS * LANE),
                   col(B_Q_RANK), col(B_KV_RANK),
                   heads(A_HEADS), heads(A_KV), heads(A_KV), heads(B_HEADS), heads(B_HEADS),
                   pl.BlockSpec((B_HEADS, tm, 2 * LANE), lambda i: (0, i, 0))],
        out_shape=[jax.ShapeDtypeStruct((D_MODEL, s_len), BF16), jax.ShapeDtypeStruct((s_len, N_GATE), F32),
                   jax.ShapeDtypeStruct((s_len, N_PRE), F32), jax.ShapeDtypeStruct((s_len, B_HEADS * LANE), F32),
                   jax.ShapeDtypeStruct((s_len, B_HEADS * LANE), F32),
                   jax.ShapeDtypeStruct((B_Q_RANK, s_len), BF16), jax.ShapeDtypeStruct((B_KV_RANK, s_len), BF16),
                   hs(A_HEADS), hs(A_KV), hs(A_KV), hs(B_HEADS), hs(B_HEADS),
                   jax.ShapeDtypeStruct((B_HEADS, s_len, 2 * LANE), BF16)],
        scratch_shapes=[pltpu.VMEM((tm, N_EXT), F32)],
        compiler_params=_params(("parallel",), VMEM_LIMIT),
    )(x, *tabs, w_in_ext, w_uq_pad, w_ukv_ext, *gains)


def _attn_fwd(q, k, v, group, l_col, tq, tk, tiles, name):
    n_heads, s_len, _ = q.shape
    v_w = v.shape[2]
    nk = s_len // tk

    def body(q_ref, k_ref, v_ref, o_ref, lse_ref, s_buf, p_buf, a_buf, m_ref, acc_ref):
        def scores(g, slot):
            s_buf[slot] = _nt(q_ref[_block_rows(g // nk, tq), :], k_ref[_block_rows(g % nk, tk), :])

        def softmax(g, slot):
            t = g // nk
            s = s_buf[slot]
            m_old = m_ref[t]
            m_new = jnp.maximum(m_old, jnp.max(s, axis=-1, keepdims=True))
            m_ref[t] = m_new
            a_buf[slot] = jnp.exp2(m_old - m_new)
            p_buf[slot] = jnp.exp2(s - jnp.tile(m_new, (1, tk // LANE))).astype(BF16)

        def values(g, slot):
            t = g // nk
            pv = _nn(p_buf[slot], v_ref[_block_rows(g % nk, tk), :])
            for c in range(0, v_w, LANE):
                acc_ref[t, :, c:c + LANE] = a_buf[slot] * acc_ref[t, :, c:c + LANE] + pv[:, c:c + LANE]

        m_ref[...] = jnp.full(m_ref.shape, -1e30, F32)
        acc_ref[...] = jnp.zeros(acc_ref.shape, F32)
        _three_stage(tiles * nk, scores, softmax, values)
        for t in range(tiles):
            l = acc_ref[t, :, l_col:l_col + 1]
            o = acc_ref[t, :, 0:LANE] * (1.0 / l)
            if l_col < LANE:
                lane = lax.broadcasted_iota(jnp.int32, o.shape, 1)
                o = jnp.where(lane == l_col, 0.0, o)
            o_ref[t * tq:(t + 1) * tq, :] = o
            lse_ref[t] = jnp.transpose(m_ref[t] + jnp.log2(jnp.broadcast_to(l, (tq, LANE))))[0:1, :]

    return _pallas(
        body, name=name, grid=(n_heads, s_len // (tiles * tq)),
        in_specs=[pl.BlockSpec((None, tiles * tq, LANE), lambda h, i: (h, i, 0)),
                  pl.BlockSpec((None, s_len, LANE), lambda h, i: (h // group, 0, 0)),
                  pl.BlockSpec((None, s_len, v_w), lambda h, i: (h // group, 0, 0))],
        out_specs=[pl.BlockSpec((None, tiles * tq, LANE), lambda h, i: (h, i, 0)),
                   pl.BlockSpec((None, tiles, 1, tq), lambda h, i: (h, i, 0, 0))],
        out_shape=[jax.ShapeDtypeStruct((n_heads, s_len, LANE), F32),
                   jax.ShapeDtypeStruct((n_heads, s_len // tq, 1, tq), F32)],
        scratch_shapes=[pltpu.VMEM((2, tq, tk), F32), pltpu.VMEM((2, tq, tk), BF16), pltpu.VMEM((2, tq, LANE), F32),
                        pltpu.VMEM((tiles, tq, LANE), F32), pltpu.VMEM((tiles, tq, v_w), F32)],
        compiler_params=_params(("parallel", "parallel"), VMEM_MID),
    )(q, k, v)


def _mid(x, target, o_a, o_b, gates, w_out_ext, tm):
    s_len = x.shape[0]
    nt = s_len // tm
    n_heads = A_HEADS + B_HEADS

    def body(x_ref, t_ref, oa_ref, ob_ref, g_ref, w_ref,
             yt_ref, dh_ref, dgate_ref, doa_ref, dob_ref, delta_ref, loss_ref, silu_scr, dsilu_scr, y_ref):
        @pl.when(pl.program_id(0) == 0)
        def _():
            loss_ref[...] = jnp.zeros_like(loss_ref)

        def o_of(h):
            return oa_ref[h] if h < A_HEADS else ob_ref[h - A_HEADS]

        for h in range(n_heads):
            cols = slice(LANE * h, LANE * (h + 1))
            g = g_ref[:, cols]
            sig = 1.0 / (1.0 + jnp.exp(-g))
            silu = g * sig
            silu_scr[:, cols] = silu
            dsilu_scr[:, cols] = sig * (1.0 + g * (1.0 - sig))
            y = o_of(h) * silu
            y_ref[:, cols] = y.astype(BF16)
            yt_ref[cols, :] = jnp.transpose(y).astype(BF16)
        err =x_ref[...] + _nn(y_ref[...], w_ref[...]) - t_ref[...]
        sq = jnp.sum(jnp.sum(err * err, axis=-1, keepdims=True), axis=0, keepdims=True)
        loss_ref[...] += jnp.broadcast_to(sq * (0.5 / D_MODEL), loss_ref.shape)
        dh = err * (1.0 / D_MODEL)
        dh_ref[...] = dh
        dy = _nt(dh.astype(BF16), w_ref[...])
        lane = lax.broadcasted_iota(jnp.int32, (tm, LANE), 1)
        delta = jnp.zeros((tm, LANE), F32)
        for h in range(n_heads):
            cols = slice(LANE * h, LANE * (h + 1))
            dyh = dy[:, cols]
            oh = o_of(h)
            do = dyh * silu_scr[:, cols]
            dgate_ref[:, cols] = (dyh * oh * dsilu_scr[:, cols]).astype(BF16)
            delta = jnp.where(lane == h, jnp.sum(do * oh, axis=-1, keepdims=True), delta)
            if h < A_HEADS:
                doa_ref[h] = do.astype(BF16)
            else:
                dob_ref[h - A_HEADS] = do.astype(BF16)
        delta_ref[...] = jnp.transpose(delta)[0:DELTA_ROWS, :]

    row = lambda w: pl.BlockSpec((tm, w), lambda i: (i, 0))
    heads = lambda n, w=LANE: pl.BlockSpec((n, tm, w), lambda i: (0, i, 0))
    return _pallas(
        body, name="mid", grid=(nt,),
        in_specs=[row(D_MODEL), row(D_MODEL), heads(A_HEADS), heads(B_HEADS), row(N_GATE), _resident(w_out_ext.shape)],
        out_specs=[pl.BlockSpec((N_GATE, tm), lambda i: (0, i)), row(D_MODEL), row(N_GATE), heads(A_HEADS), heads(B_HEADS),
                   pl.BlockSpec((DELTA_ROWS, tm), lambda i: (0, i)),
                   _full((8, LANE))],
        out_shape=[jax.ShapeDtypeStruct((N_GATE, s_len), BF16), jax.ShapeDtypeStruct((s_len, D_MODEL), F32),
                   jax.ShapeDtypeStruct((s_len, N_GATE), BF16),
                   jax.ShapeDtypeStruct((A_HEADS, s_len, LANE), BF16), jax.ShapeDtypeStruct((B_HEADS, s_len, LANE), BF16),
                   jax.ShapeDtypeStruct((DELTA_ROWS, s_len), F32), jax.ShapeDtypeStruct((8, LANE), F32)],
        scratch_shapes=[pltpu.VMEM((tm, N_GATE), F32), pltpu.VMEM((tm, N_GATE), F32), pltpu.VMEM((tm, N_GATE), BF16)],
        compiler_params=_params(("arbitrary",), VMEM_LIMIT),
    )(x, target, o_a, o_b, gates, w_out_ext)


def _attn_bwd(q, k, v, do, lse, delta, group, tq, tk, tiles, name):
    n_heads, s_len, _ = q.shape
    nq = s_len // tq

    def body(q_ref, do_ref, lse_ref, delta_ref, k_ref, v_ref, dq_ref, dk_ref, dv_ref, s_buf, dp_buf, p_buf, ds_buf):
        @pl.when(pl.program_id(1) == 0)
        def _():
            dq_ref[...] = jnp.zeros_like(dq_ref)

        dk_ref[...] = jnp.zeros_like(dk_ref)
        dv_ref[...] = jnp.zeros_like(dv_ref)

        def keys(g):
            return _block_rows(g // nq, tk)

        def queries(g):
            return _block_rows(g % nq, tq)

        def scores(g, slot):
            s_buf[slot] = _nt(k_ref[keys(g), :], q_ref[queries(g), :])
            dp_buf[slot] = _nt(v_ref[keys(g), :], do_ref[queries(g), :])

        def elementwise(g, slot):
            p = jnp.exp2(s_buf[slot] - lse_ref[g % nq])
            p_buf[slot] = p.astype(BF16)
            ds_buf[slot] = (p * (dp_buf[slot] - delta_ref[g % nq])).astype(BF16)

        def grads(g, slot):
            dv_ref[keys(g), :] += _nn(p_buf[slot], do_ref[queries(g), :])
            dk_ref[keys(g), :] += _nn(ds_buf[slot], q_ref[queries(g), :])
            dq_ref[queries(g), :] += _tn(ds_buf[slot], k_ref[keys(g), :])

        _three_stage(tiles * nq, scores, elementwise, grads)

    whole = lambda: pl.BlockSpec((None, s_len, LANE), lambda h, j: (h, 0, 0))
    stat = lambda: pl.BlockSpec((None, nq, 1, tq), lambda h, j: (h, 0, 0, 0))
    kvb = lambda: pl.BlockSpec((None, tiles * tk, LANE), lambda h, j: (h // group, j, 0))
    outb = lambda: pl.BlockSpec((None, tiles * tk, LANE), lambda h, j: (h, j, 0))
    shape = jax.ShapeDtypeStruct((n_heads, s_len, LANE), F32)
    return _pallas(
        body, name=name, grid=(n_heads, s_len // (tiles * tk)),
        in_specs=[whole(), whole(), stat(), stat(), kvb(), kvb()],
        out_specs=[whole(), outb(), outb()],
        out_shape=[shape, shape, shape],
        scratch_shapes=[pltpu.VMEM((2, tk, tq), F32), pltpu.VMEM((2, tk, tq), F32),
                        pltpu.VMEM((2, tk, tq), BF16), pltpu.VMEM((2, tk, tq), BF16)],
        compiler_params=_params(("parallel", "arbitrary"), VMEM_LIMIT),
    )(q, do, lse, delta, k, v)


def _post(x, dh, pre, qbpre, kbpre, dgate, dqa, dka, dva, dqb, dkb, dvb, loss_part, tabs,
          w_in_ext, w_uq_pad, w_ukv_ext, gains, tm):
    s_len = x.shape[0]
    nt = s_len // tm

    def body(x_ref, dh_ref, pre_ref, qbpre_ref, kbpre_ref, dgate_ref,
             dqa_ref, dka_ref, dva_ref, dqb_ref, dkb_ref, dvb_ref, loss_ref,
             car_ref, cac_ref, sar_ref, sac_ref, cbr_ref, cbc_ref, sbr_ref, sbc_ref, win_ref, wuq_ref, wukv_ref,
             gin_ref, gaq_ref, gak_ref, gcq_ref, gckv_ref, gbq_ref, gbk_ref,
             gx_ref, dproj_ref, dqbpre_ref, dkvb_ref, dsm_ref):
        @pl.when(pl.program_id(0) == 0)
        def _():
            dsm_ref[...] = jnp.zeros_like(dsm_ref)
            dsm_ref[SM_LOSS:SM_LOSS + 1, 0:LANE] = loss_ref[0:1, :]

        def add_small(r, dg):
            dsm_ref[r:r + 1, 0:dg.shape[1]] += dg

        def tok_sum(a):
            return jnp.sum(a, axis=0, keepdims=True)

        ca, sa, cb, sb = _token_tables((car_ref, cac_ref, sar_ref, sac_ref, cbr_ref, cbc_ref, sbr_ref, sbc_ref))
        lane = lax.broadcasted_iota(jnp.int32, (tm, LANE), 1)

        nope_lanes = _lanes_of(lane, LAY_NOPE)

        def back(c0, c1):
            return _nt(dproj_ref[:, c0:c1], win_ref[:, c0:c1])

        dproj_ref[:, GA0:GA0 + N_GATE] = dgate_ref[...]
        dxn = back(GA0, GA0 + N_GATE)
        dg = jnp.zeros((1, LANE), F32)
        for h in range(A_HEADS):
            dn = _rope_bwd(dqa_ref[h] * SCALE_A, ca, sa)
            dx, dgr = _rms_bwd(dn, pre_ref[:, QA0 + LANE * h:QA0 + LANE * (h + 1)], gaq_ref[...], A_DIM)
            dproj_ref[:, QA0 + LANE * h:QA0 + LANE * (h + 1)] = dx.astype(BF16)
            dg = dg + tok_sum(dgr)
        add_small(SM_AQ, _unspread_row(dg, LAY_ROPE_A))
        dxn = dxn + back(QA0, KA0)
        dg = jnp.zeros((1, LANE), F32)
        for h in range(A_KV):
            dk = dka_ref[A_GROUP * h]
            dv = dva_ref[A_GROUP * h]
            for g in range(1, A_GROUP):
                dk = dk + dka_ref[A_GROUP * h + g]
                dv = dv + dva_ref[A_GROUP * h + g]
            dn = _rope_bwd(dk * LN2, ca, sa)
            dx, dgr = _rms_bwd(dn, pre_ref[:, KA0 + LANE * h:KA0 + LANE * (h + 1)], gak_ref[...], A_DIM)
            dproj_ref[:, KA0 + LANE * h:KA0 + LANE * (h + 1)] = dx.astype(BF16)
            dproj_ref[:, VA0 + LANE * h:VA0 + LANE * (h + 1)] = dv.astype(BF16)
            dg = dg + tok_sum(dgr)
        add_small(SM_AK, _unspread_row(dg, LAY_ROPE_A))
        dxn = dxn + back(KA0, GA0)
        dg = jnp.zeros((1, LANE), F32)
        for h in range(B_HEADS):
            cols = slice(LANE * h, LANE * (h + 1))
            dn = _rope_bwd(dqb_ref[h] * SCALE_B, cb, sb)
            dx, dgr = _rms_bwd(dn, qbpre_ref[:, cols], gbq_ref[...], B_QK)
            dqbpre_ref[:, cols] = dx.astype(BF16)
            dg = dg + tok_sum(dgr)
        add_small(SM_BQ, _unspread_row(dg, LAY_ROPE_B))
        dcq = _nt(dqbpre_ref[...], wuq_ref[...])
        dx, dgr = _rms_bwd(dcq, pre_ref[:, VA0:VA0 + B_Q_RANK], gcq_ref[...], B_Q_RANK)
        dproj_ref[:, CQ0:CQ0 + B_Q_RANK] = dx.astype(BF16)
        add_small(SM_CQ, tok_sum(dgr))
        dxn = dxn + back(CQ0, CKV0)
        dg = jnp.zeros((1, LANE), F32)
        dkr = jnp.zeros((tm, LANE), F32)
        for h in range(B_HEADS):
            cols = slice(LANE * h, LANE * (h + 1))
            dn = _rope_bwd(dkb_ref[h] * LN2, cb, sb)
            dx, dgr = _rms_bwd(dn, kbpre_ref[:, cols], gbk_ref[...], B_QK)
            dkvb_ref[:, cols] = jnp.where(nope_lanes, dx, 0.0).astype(BF16)
            dkvb_ref[:, B_HEADS * LANE + LANE * h:B_HEADS * LANE + LANE * (h + 1)] = dvb_ref[h].astype(BF16)
            dkr = dkr + dx
            dg = dg + tok_sum(dgr)
        add_small(SM_BK, _unspread_row(dg, LAY_ROPE_B))
        dproj_ref[:, KR0:KR0 + LANE] = jnp.where(_lanes_of(lane, LAY_KR), dkr, 0.0).astype(BF16)
        dckv = _nt(dkvb_ref[...], wukv_ref[...])
        dx, dgr = _rms_bwd(dckv, pre_ref[:, VA0 + B_Q_RANK:N_PRE], gckv_ref[...], B_KV_RANK)
        dproj_ref[:, CKV0:CKV0 + B_KV_RANK] = dx.astype(BF16)
        add_small(SM_CKV, tok_sum(dgr))
        dxn = dxn + back(CKV0, N_EXT)
        dx, dgr = _rms_bwd(dxn, x_ref[...], gin_ref[...], D_MODEL)
        gx_ref[...] = dh_ref[...] + dx
        add_small(SM_IN, tok_sum(dgr))

    row = lambda w: pl.BlockSpec((tm, w), lambda i: (i, 0))
    heads = lambda n: pl.BlockSpec((n, tm, LANE), lambda i: (0, i, 0))
    return _pallas(
        body, name="post", grid=(nt,),
        in_specs=[row(D_MODEL), row(D_MODEL), row(N_PRE), row(B_HEADS * LANE), row(B_HEADS * LANE), row(N_GATE),
                  heads(A_HEADS), heads(A_HEADS), heads(A_HEADS), heads(B_HEADS), heads(B_HEADS), heads(B_HEADS),
                  _full(loss_part.shape)] + _table_specs(tm)
                 + [_resident(w_in_ext.shape), _resident(w_uq_pad.shape), _resident(w_ukv_ext.shape)]
                 + [_full(g.shape) for g in gains],
        out_specs=[row(D_MODEL), row(N_EXT), row(B_HEADS * LANE), row(2 * B_HEADS * LANE), _full((SM_ROWS, SM_W))],
        out_shape=[jax.ShapeDtypeStruct((s_len, D_MODEL), F32), jax.ShapeDtypeStruct((s_len, N_EXT), BF16),
                   jax.ShapeDtypeStruct((s_len, B_HEADS * LANE), BF16),
                   jax.ShapeDtypeStruct((s_len, 2 * B_HEADS * LANE), BF16),
                   jax.ShapeDtypeStruct((SM_ROWS, SM_W), F32)],
        compiler_params=_params(("arbitrary",), VMEM_LIMIT),
    )(x, dh, pre, qbpre, kbpre, dgate, dqa, dka, dva, dqb, dkb, dvb, loss_part, *tabs,
      w_in_ext, w_uq_pad, w_ukv_ext, *gains)


def _grad_w(a_t, b, tn, ts, name):
    m, s_len = a_t.shape
    n = b.shape[1]

    def body(a_ref, b_ref, o_ref):
        @pl.when(pl.program_id(1) == 0)
        def _():
            o_ref[...] = jnp.zeros_like(o_ref)

        o_ref[...] += _nn(a_ref[...], b_ref[...].astype(BF16))

    return _pallas(
        body, name=name, grid=(n // tn, s_len // ts),
        in_specs=[pl.BlockSpec((m, ts), lambda j, t: (0, t)), pl.BlockSpec((ts, tn), lambda j, t: (t, j))],
        out_specs=pl.BlockSpec((m, tn), lambda j, t: (0, j)),
        out_shape=jax.ShapeDtypeStruct((m, n), F32),
        compiler_params=_params(("parallel", "arbitrary"), VMEM_MID),
    )(a_t, b)


def _grad_w_pairs(pairs, ts, name):
    s_len = pairs[0][0].shape[1]
    n_p = len(pairs)

    def body(*refs):
        for a_ref, b_ref, o_ref in zip(refs[0:2 * n_p:2], refs[1:2 * n_p:2], refs[2 * n_p:]):
            @pl.when(pl.program_id(0) == 0)
            def _():
                o_ref[...] = jnp.zeros_like(o_ref)

            o_ref[...] += _nn(a_ref[...], b_ref[...].astype(BF16))

    in_specs, flat = [], []
    for a_t, b in pairs:
        in_specs += [pl.BlockSpec((a_t.shape[0], ts), lambda t: (0, t)), pl.BlockSpec((ts, b.shape[1]), lambda t: (t, 0))]
        flat += [a_t, b]
    return _pallas(
        body, name=name, grid=(s_len // ts,),
        in_specs=in_specs,
        out_specs=[_full((a_t.shape[0], b.shape[1])) for a_t, b in pairs],
        out_shape=[jax.ShapeDtypeStruct((a_t.shape[0], b.shape[1]), F32) for a_t, b in pairs],
        compiler_params=_params(("arbitrary",), VMEM_MID),
    )(*flat)


def _adam_math(w, g, m, v):
    nm = ADAM_B1 * m + (1.0 - ADAM_B1) * g
    nv = ADAM_B2 * v + (1.0 - ADAM_B2) * (g * g)
    m_hat = nm / (1.0 - ADAM_B1 ** ADAM_STEP)
    v_hat = nv / (1.0 - ADAM_B2 ** ADAM_STEP)
    return -ADAM_LR * (m_hat / (jnp.sqrt(v_hat) + ADAM_EPS) + ADAM_WD * w), nm, nv


def _adamw_rows(w, g, m, v, tr):
    rows, cols = w.shape

    def body(w_ref, g_ref, m_ref, v_ref, d_ref, nm_ref, nv_ref):
        d_ref[...], nm_ref[...], nv_ref[...] = _adam_math(w_ref[...], g_ref[...], m_ref[...], v_ref[...])

    blk = pl.BlockSpec((tr, cols), lambda i: (i, 0))
    shape = jax.ShapeDtypeStruct((rows, cols), F32)
    return _pallas(
        body, name="adamw_w_in", grid=(rows // tr,),
        in_specs=[blk] * 4, out_specs=[blk] * 3, out_shape=[shape] * 3,
        compiler_params=_params(("parallel",), VMEM_SMALL),
    )(w, g, m, v)


def _adamw_rest(bigs, smalls, g_small):
    nb, ns = len(bigs), len(smalls)

    def body(*refs):
        ins, outs = refs[:4 * nb + 3 * ns + 1], refs[4 * nb + 3 * ns + 1:]
        for i in range(nb):
            w_ref, g_ref, m_ref, v_ref = ins[4 * i:4 * i + 4]
            d_ref, nm_ref, nv_ref = outs[3 * i:3 * i + 3]
            d_ref[...], nm_ref[...], nv_ref[...] = _adam_math(w_ref[...], g_ref[...], m_ref[...], v_ref[...])
        gs_ref = ins[-1]
        for i in range(ns):
            w_ref, m_ref, v_ref = ins[4 * nb + 3 * i:4 * nb + 3 * i + 3]
            g_ref, d_ref, nm_ref, nv_ref = outs[3 * nb + 4 * i:3 * nb + 4 * i + 4]
            g = gs_ref[i:i + 1, 0:w_ref.shape[1]]
            g_ref[...] = g
            d_ref[...], nm_ref[...], nv_ref[...] = _adam_math(w_ref[...], g, m_ref[...], v_ref[...])

    flat_in = [a for quad in bigs for a in quad] + [a for tri in smalls for a in tri] + [g_small]
    out_shape = ([jax.ShapeDtypeStruct(q[0].shape, F32) for q in bigs for _ in range(3)]
                 + [jax.ShapeDtypeStruct(t[0].shape, F32) for t in smalls for _ in range(4)])
    return _pallas(
        body, name="adamw_rest",
        in_specs=[pl.BlockSpec(memory_space=pltpu.VMEM)] * len(flat_in),
        out_specs=[pl.BlockSpec(memory_space=pltpu.VMEM)] * len(out_shape),
        out_shape=out_shape,
        compiler_params=_params(vmem=VMEM_SMALL),
    )(*flat_in)


def _place(pieces, n):
    out, at = [], 0
    for lane0, arr in sorted(pieces, key=lambda p: p[0]):
        out += [jnp.zeros((n, lane0 - at), F32), arr]
        at = lane0 + arr.shape[1]
    return jnp.concatenate(out + [jnp.zeros((n, LANE - at), F32)], axis=1)


def _rope_tables(s_len, tm):
    rows = s_len // GRID_W
    row = jnp.arange(rows, dtype=F32)
    col = jnp.arange(GRID_W, dtype=F32)

    def lay(dim, layout, first_dim, ones):
        half = dim // 2
        inv = 1.0 / (ROPE_THETA ** (jnp.arange(0, half, 2, dtype=F32) / half))
        ang_r, ang_c = row[:, None] * inv[None, :], col[:, None] * inv[None, :]
        at = {a - first_dim: lane0 for a, _, lane0 in layout}
        q = dim // 4
        r1, r2, c1, c2 = at[0], at[q], at[2 * q], at[3 * q]
        cos_r = _place([(r1, jnp.cos(ang_r)), (r2, jnp.cos(ang_r))], rows)
        sin_r = _place([(r1, -jnp.sin(ang_r)), (r2, jnp.sin(ang_r))], rows)
        cos_c = _place([(c1, jnp.cos(ang_c)), (c2, jnp.cos(ang_c))] + [(l0, jnp.ones((GRID_W, n), F32)) for _, n, l0 in ones],
                       GRID_W)
        sin_c = _place([(c1, -jnp.sin(ang_c)), (c2, jnp.sin(ang_c))], GRID_W)
        by_block = (s_len // tm, tm // GRID_W, LANE)
        return cos_r.reshape(by_block), cos_c, sin_r.reshape(by_block), sin_c

    return lay(A_DIM, LAY_ROPE_A, 0, ()) + lay(B_ROPE, LAY_KR, 0, LAY_NOPE)


def _spread(w, n_heads, dim, axis, layout):
    w3 = w.reshape(w.shape[:axis] + (n_heads, dim) + w.shape[axis + 1:])
    out, at = [], 0

    def zeros(n):
        return jnp.zeros(w3.shape[:axis + 1] + (n,) + w3.shape[axis + 2:], w.dtype)

    for a0, n, lane0 in sorted(layout, key=lambda seg: seg[2]):
        out += [zeros(lane0 - at), lax.slice_in_dim(w3, a0, a0 + n, axis=axis + 1)]
        at = lane0 + n
    out = jnp.concatenate(out + [zeros(LANE - at)], axis=axis + 1)
    return out.reshape(w.shape[:axis] + (n_heads * LANE,) + w.shape[axis + 1:])


def _unspread(w, n_heads, dim, axis, layout):
    w3 = w.reshape(w.shape[:axis] + (n_heads, LANE) + w.shape[axis + 1:])
    parts = [lax.slice_in_dim(w3, lane0, lane0 + n, axis=axis + 1) for _, n, lane0 in sorted(layout)]
    out = parts[0] if len(parts) == 1 else jnp.concatenate(parts, axis=axis + 1)
    return out.reshape(w.shape[:axis] + (n_heads * dim,) + w.shape[axis + 1:])


def _head_cols(first, n_heads, dim, layout):
    out = np.full((n_heads * LANE,), -1, np.int32)
    for h in range(n_heads):
        for a0, n, lane0 in layout:
            out[h * LANE + lane0:h * LANE + lane0 + n] = first + h * dim + a0 + np.arange(n)
    return out


def _inverse(src, n):
    dst = np.full((n,), -1, np.int32)
    dst[src[src >= 0]] = np.nonzero(src >= 0)[0]
    return dst


def _column_maps():
    a_w, kv_w = A_HEADS * A_DIM, A_KV * A_DIM
    o_g = a_w + 2 * kv_w
    o_cq = o_g + a_w
    o_kr = o_cq + B_Q_RANK + B_KV_RANK
    src_in = np.concatenate([
        _head_cols(0, A_HEADS, A_DIM, LAY_ROPE_A), _head_cols(a_w, A_KV, A_DIM, LAY_ROPE_A),
        _head_cols(a_w + kv_w, A_KV, A_DIM, LAY_PLAIN_A), _head_cols(o_g, A_HEADS, A_DIM, LAY_PLAIN_A),
        np.arange(o_kr + B_ROPE, N_IN), np.arange(o_cq, o_kr), _head_cols(o_kr, 1, B_ROPE, LAY_KR)]).astype(np.int32)
    src_uq = _head_cols(0, B_HEADS, B_QK, LAY_ROPE_B)
    per = B_NOPE + B_V
    src_ukv = np.concatenate([_head_cols(0, B_HEADS, per, LAY_NOPE),
                              _head_cols(B_NOPE, B_HEADS, per, ((0, B_V, 0),))]).astype(np.int32)
    assert len(src_in) == N_EXT
    return src_in, src_uq, src_ukv


def _round_up(n, m):
    return (n + m - 1) // m * m


def _permute_cols(xs, maps, stacks, name):
    maps = [np.asarray(m, np.int32) for m in maps]
    n_arr = len(xs)

    def block(ref, b):
        if len(ref.shape) == 2:
            return ref.at[:, b * LANE:(b + 1) * LANE]
        per = ref.shape[2] // LANE
        return ref.at[b // per, :, (b % per) * LANE:(b % per + 1) * LANE]

    def body(*refs):
        row = lax.broadcasted_iota(jnp.int32, (LANE, LANE), 0)
        for x_ref, src_ref, o_ref, src in zip(refs[:n_arr], refs[n_arr:2 * n_arr], refs[2 * n_arr:], maps):
            for c in range(len(src) // LANE):
                want = src[c * LANE:(c + 1) * LANE]
                if want[0] >= 0 and want[0] % LANE == 0 and np.array_equal(want, want[0] + np.arange(LANE)):
                    block(o_ref, c)[...] = block(x_ref, int(want[0]) // LANE)[...]
                    continue
                acc = jnp.zeros((x_ref.shape[-2], LANE), F32)
                for kb in sorted({int(v) // LANE for v in want if v >= 0}):
                    sel = jnp.where(row + kb * LANE == src_ref[:, c * LANE:(c + 1) * LANE], 1.0, 0.0).astype(BF16)
                    part = block(x_ref, kb)[...]
                    if part.dtype == BF16:
                        acc = acc + _nn(part, sel)
                    else:
                        hi = part.astype(BF16)
                        rest = part - hi.astype(F32)
                        mid = rest.astype(BF16)
                        low = (rest - mid.astype(F32)).astype(BF16)
                        acc = acc + ((_nn(hi, sel) + _nn(mid, sel)) + _nn(low, sel))
                block(o_ref, c)[...] = acc.astype(o_ref.dtype)

    def out_shape(x, m, stack):
        rows = x.shape[-2]
        return (rows, len(m)) if stack is None else (stack, rows, len(m) // stack)

    return _pallas(
        body, name=name,
        out_shape=[jax.ShapeDtypeStruct(out_shape(x, m, st), x.dtype) for x, m, st in zip(xs, maps, stacks)],
        compiler_params=_params(vmem=VMEM_MID),
    )(*xs, *[jnp.asarray(m).reshape(1, -1) for m in maps])


def _pad_cols(w):
    return jnp.pad(w, ((0, 0), (0, _round_up(w.shape[1], LANE) - w.shape[1])))


def _in_stack(cols, width):
    cols = np.asarray(cols)
    return np.where(cols < 0, -1, cols // width * _round_up(width, LANE) + cols % width).astype(np.int32)


def _ext_weights(g_in, g_uq, g_ukv, g_out):
    src_in, src_uq, src_ukv = _column_maps()
    w_uq = g_uq.reshape(B_Q_RANK, B_HEADS * B_QK)
    w_out = g_out.reshape(D_MODEL, D_MODEL)
    w_in_ext, w_uq_pad, w_ukv_ext = _permute_cols(
        [g_in, w_uq, g_ukv], [_in_stack(src_in, SH_IN[1]), src_uq, _in_stack(src_ukv, SH_UKV[1])], [None] * 3, "lay_out_weights")
    a_w = A_HEADS * A_DIM
    w_out_ext = jnp.concatenate([_spread(w_out[:a_w], A_HEADS, A_DIM, 0, LAY_PLAIN_A), w_out[a_w:]], axis=0)
    return w_in_ext, w_uq_pad, w_ukv_ext, w_out_ext


def _fold_grads(d_in_ext, d_uq_pad, d_ukv_ext, d_out_ext):
    src_in, src_uq, src_ukv = _column_maps()

    def back(src, n, width):
        inv = _inverse(src, n)
        wide = _round_up(width, LANE)
        out = np.full((n // width * wide,), -1, np.int32)
        for j in range(n // width):
            out[j * wide:j * wide + width] = inv[j * width:(j + 1) * width]
        return out

    n_uq, n_ukv = B_HEADS * B_QK, B_HEADS * (B_NOPE + B_V)
    d_in, d_uq, d_ukv = _permute_cols(
        [d_in_ext, d_uq_pad, d_ukv_ext], [back(src_in, N_IN, SH_IN[1]), _inverse(src_uq, n_uq), back(src_ukv, n_ukv, SH_UKV[1])],
        [N_CHIPS, None, N_CHIPS], "fold_grads")
    d_out = jnp.concatenate([_unspread(d_out_ext[:A_HEADS * LANE], A_HEADS, A_DIM, 0, LAY_PLAIN_A), d_out_ext[A_HEADS * LANE:]], axis=0)
    return d_in, d_uq.reshape((N_CHIPS,) + SH_UQ), d_ukv, d_out.reshape((N_CHIPS,) + SH_OUT)


def kernel(x, norm_in, w_in, a_q_norm, a_k_norm, b_cq_norm, b_ckv_norm, w_uq, w_ukv, b_q_norm, b_k_norm, w_out, loss_target, m_norm_in, m_w_in, m_a_q_norm, m_a_k_norm, m_b_cq_norm, m_b_ckv_norm, m_w_uq, m_w_ukv, m_b_q_norm, m_b_k_norm, m_w_out, v_norm_in, v_w_in, v_a_q_norm, v_a_k_norm, v_b_cq_norm, v_b_ckv_norm, v_w_uq, v_w_ukv, v_b_q_norm, v_b_k_norm, v_w_out):
    s_len = x.shape[1]
    xs, ts = x[0], loss_target[0]
    tm = min(256, s_len)
    tq, tk_f = min(512, s_len // 2), min(2048, s_len // 2)
    tq_b, tk_b = min(2048, s_len // 2), min(512, s_len)
    tiles_f = min(4, s_len // tq)
    tiles_b = min(4, s_len // tk_b)

    w_in_ext, w_uq_pad, w_ukv_ext, w_out_ext = _ext_weights(
        *_gather_weights((_pad_cols(w_in[0]), w_uq[0], _pad_cols(w_ukv[0]), w_out[0])))
    gains = (norm_in, _spread(a_q_norm, 1, A_DIM, 1, LAY_ROPE_A), _spread(a_k_norm, 1, A_DIM, 1, LAY_ROPE_A), b_cq_norm, b_ckv_norm,
             _spread(b_q_norm, 1, B_QK, 1, LAY_ROPE_B), _spread(b_k_norm, 1, B_QK, 1, LAY_ROPE_B))
    tabs = _rope_tables(s_len, tm)

    (xn_t, gates, pre, qbpre, kbpre, cq_t, ckv_t, qa, ka, va, qb, kb, vb) = _pre(
        xs, tabs, w_in_ext, w_uq_pad, w_ukv_ext, gains, tm)
    o_a, lse_a = _attn_fwd(qa, ka, va, A_GROUP, A_DIM, tq, tk_f, tiles_f, "attn_fwd_a")
    o_b, lse_b = _attn_fwd(qb, kb, vb, 1, B_V, tq, tk_f, tiles_f, "attn_fwd_b")
    y_t, dh, dgate, do_a, do_b, delta, loss_part = _mid(xs, ts, o_a, o_b, gates, w_out_ext, min(512, s_len))

    def stat(a):
        return a.reshape(a.shape[0], s_len // tq_b, 1, tq_b)

    dqa, dka, dva = _attn_bwd(qa, ka, va, do_a, stat(lse_a), stat(delta[:A_HEADS]), A_GROUP, tq_b, tk_b, tiles_b, "attn_bwd_a")
    dqb, dkb, dvb = _attn_bwd(qb, kb, vb, do_b, stat(lse_b), stat(delta[A_HEADS:A_HEADS + B_HEADS]), 1, tq_b, tk_b,
                              tiles_b, "attn_bwd_b")
    grad_x, dproj, dqbpre, dkvb, d_small = _post(
        xs, dh, pre, qbpre, kbpre, dgate, dqa, dka, dva, dqb, dkb, dvb, loss_part, tabs,
        w_in_ext, w_uq_pad, w_ukv_ext, gains, tm)

    ts_w = min(2048, s_len)
    d_in_ext = _grad_w(xn_t, dproj, 768, min(2 * ts_w, s_len), "grad_w_in")
    d_out_ext = _grad_w(y_t, dh, 512, ts_w, "grad_w_out")
    d_uq_pad, d_ukv_ext = _grad_w_pairs([(cq_t, dqbpre), (ckv_t, dkvb)], ts_w, "grad_w_mla")

    g_in_p, g_uq, g_ukv_p, g_out, g_small = _reduce_grads(_fold_grads(d_in_ext, d_uq_pad, d_ukv_ext, d_out_ext), d_small)
    g_in, g_ukv = g_in_p[:, :SH_IN[1]], g_ukv_p[:, :SH_UKV[1]]
    d_in, nm_in, nv_in = (a.T for a in _adamw_rows(w_in[0].T, g_in_p.T[:SH_IN[1]], m_w_in[0].T, v_w_in[0].T, SH_IN[1] // 7))
    rest = _adamw_rest(
        [(w_uq[0], g_uq, m_w_uq[0], v_w_uq[0]), (w_ukv[0], g_ukv, m_w_ukv[0], v_w_ukv[0]),
         (w_out[0], g_out, m_w_out[0], v_w_out[0])],
        [(norm_in, m_norm_in, v_norm_in), (a_q_norm, m_a_q_norm, v_a_q_norm), (a_k_norm, m_a_k_norm, v_a_k_norm),
         (b_cq_norm, m_b_cq_norm, v_b_cq_norm), (b_ckv_norm, m_b_ckv_norm, v_b_ckv_norm),
         (b_q_norm, m_b_q_norm, v_b_q_norm), (b_k_norm, m_b_k_norm, v_b_k_norm)], g_small)
    (d_uq, nm_uq, nv_uq), (d_ukv, nm_ukv, nv_ukv), (d_out, nm_out, nv_out) = (rest[3 * i:3 * i + 3] for i in range(3))
    sm = [rest[9 + 4 * i:9 + 4 * i + 4] for i in range(7)]

    def leaves(k, p_in, p_uq, p_ukv, p_out):
        return [sm[SM_IN][k], p_in[None], sm[SM_AQ][k], sm[SM_AK][k], sm[SM_CQ][k], sm[SM_CKV][k], p_uq[None], p_ukv[None],
                sm[SM_BQ][k], sm[SM_BK][k], p_out[None]]

    return (g_small[SM_LOSS, 0], grad_x[None], *leaves(0, g_in, g_uq, g_ukv, g_out), *leaves(1, d_in, d_uq, d_ukv, d_out),
            *leaves(2, nm_in, nm_uq, nm_ukv, nm_out), *leaves(3, nv_in, nv_uq, nv_ukv, nv_out))
```

```python
import jax
import jax.numpy as jnp
import numpy as np
from jax import lax
from jax.experimental import pallas as pl
from jax.experimental.pallas import tpu as pltpu

F32 = jnp.float32
BF16 = jnp.bfloat16
MESH = pl.DeviceIdType.MESH

D_MODEL = 1024
GRID_W = 64
ROPE_THETA = 10000.0
EPS = 1e-6
A_HEADS, A_KV, A_DIM = 8, 2, 64
A_GROUP = A_HEADS // A_KV
B_HEADS, B_NOPE, B_ROPE, B_V = 4, 64, 32, 128
B_QK = B_NOPE + B_ROPE
B_Q_RANK, B_KV_RANK = 384, 256
N_IN = 2464
SCALE_A = 1.0 / float(np.sqrt(A_DIM))
SCALE_B = 1.0 / float(np.sqrt(B_QK))
LOG2E = float(np.log2(np.e))
LN2 = float(np.log(2.0))
ADAM_LR, ADAM_B1, ADAM_B2, ADAM_EPS, ADAM_WD, ADAM_STEP = 0.001, 0.9, 0.999, 1e-08, 0.01, 10

LANE = 128
VMEM_BYTES = 64 * 1024 * 1024
VMEM_LIMIT = VMEM_BYTES - 8 * 1024 * 1024
VMEM_MID = 48 * 1024 * 1024
VMEM_SMALL = 32 * 1024 * 1024

QA0 = 0
KA0 = QA0 + A_HEADS * LANE
VA0 = KA0 + A_KV * LANE
GA0 = VA0 + A_KV * LANE
GB0 = GA0 + A_HEADS * LANE
CQ0 = GB0 + B_HEADS * LANE
CKV0 = CQ0 + B_Q_RANK
KR0 = CKV0 + B_KV_RANK
N_EXT = KR0 + LANE
N_GATE = (A_HEADS + B_HEADS) * LANE
DELTA_ROWS = 16
N_PRE = KA0 + A_KV * LANE + B_Q_RANK + B_KV_RANK

ROT = LANE // 2
_QA = A_DIM // 4
_QB = B_ROPE // 4
LAY_PLAIN_A = ((0, A_DIM, 0),)
LAY_ROPE_A = ((0, _QA, 0), (2 * _QA, _QA, _QA), (_QA, _QA, ROT), (3 * _QA, _QA, ROT + _QA))
LAY_KR = ((0, _QB, 0), (2 * _QB, _QB, _QB), (_QB, _QB, ROT), (3 * _QB, _QB, ROT + _QB))
LAY_NOPE = ((0, B_NOPE // 2, 2 * _QB), (B_NOPE // 2, B_NOPE // 2, ROT + 2 * _QB))
LAY_ROPE_B = LAY_NOPE + tuple((B_NOPE + a, n, at) for a, n, at in LAY_KR)

N_CHIPS = 4
SH_IN = (D_MODEL, N_IN // N_CHIPS)
SH_UQ = (B_Q_RANK // N_CHIPS, B_HEADS * B_QK)
SH_UKV = (B_KV_RANK, B_HEADS * (B_NOPE + B_V) // N_CHIPS)
SH_OUT = (D_MODEL // N_CHIPS, D_MODEL)
SM_ROWS, SM_W = 16, D_MODEL
SM_IN, SM_AQ, SM_AK, SM_CQ, SM_CKV, SM_BQ, SM_BK, SM_LOSS = range(8)
F32_ROWS, BF16_ROWS = 8, 16


def _pallas(body, **kw):
    return pl.pallas_call(body, **kw)


def _params(sem=None, vmem=None):
    return pltpu.CompilerParams(dimension_semantics=sem, vmem_limit_bytes=vmem)


def _rms_fwd(x, g, n):
    r = lax.rsqrt(jnp.sum(x * x, axis=-1, keepdims=True) * (1.0 / n) + EPS)
    return x * r * g


def _rms_bwd(dy, x, g, n):
    u = dy * g
    r = lax.rsqrt(jnp.sum(x * x, axis=-1, keepdims=True) * (1.0 / n) + EPS)
    ux = jnp.sum(u * x, axis=-1, keepdims=True)
    xhat = x * r
    dx = r * (u - xhat * (r * ux * (1.0 / n)))
    return dx, dy * xhat


def _rope_fwd(y, cos, sin):
    return y * cos + pltpu.roll(y, ROT, 1) * sin


def _rope_bwd(d, cos, sin):
    return d * cos - pltpu.roll(d, ROT, 1) * sin


def _token_tables(refs):
    out = []
    for r_ref, c_ref in zip(refs[0::2], refs[1::2]):
        r, c = r_ref[...], c_ref[...]
        out.append(jnp.concatenate([r[k:k + 1, :] + c for k in range(r.shape[0])], axis=0))
    return out


def _lanes_of(lane, layout):
    m = None
    for _, n, at in layout:
        seg = (lane >= at) & (lane < at + n)
        m = seg if m is None else (m | seg)
    return m


def _unspread_row(v, layout):
    v8 = jnp.broadcast_to(v, (F32_ROWS, LANE))
    lane = lax.broadcasted_iota(jnp.int32, v8.shape, 1)
    out = jnp.zeros_like(v8)
    for a, n, at in layout:
        moved = v8 if a == at else pltpu.roll(v8, (a - at) % LANE, 1)
        out = jnp.where((lane >= a) & (lane < a + n), moved, out)
    return out[0:1, :]


def _nt(a, b):
    return lax.dot_general(a, b, (((1,), (1,)), ((), ())), preferred_element_type=F32)


def _tn(a, b):
    return lax.dot_general(a, b, (((0,), (0,)), ((), ())), preferred_element_type=F32)


def _nn(a, b):
    return jnp.dot(a, b, preferred_element_type=F32)


def _block_rows(i, size):
    if isinstance(i, int):
        return pl.ds(i * size, size)
    return pl.ds(pl.multiple_of(i * size, size), size)


MAX_STATIC_BLOCKS = 32


def _three_stage(n, first, second, third):
    assert n >= 2 and n % 2 == 0
    first(0, 0)
    first(1, 1)
    second(0, 0)
    if n <= MAX_STATIC_BLOCKS:
        for i in range(1, n - 1):
            first(i + 1, (i + 1) % 2)
            second(i, i % 2)
            third(i - 1, (i - 1) % 2)
    else:
        def pair(t, carry):
            i = 2 * t + 1
            first(i + 1, 0)
            second(i, 1)
            third(i - 1, 0)
            first(i + 2, 1)
            second(i + 1, 0)
            third(i, 1)
            return carry

        lax.fori_loop(0, (n - 2) // 2, pair, 0)
    second(n - 1, 1)
    third(n - 2, 0)
    third(n - 1, 1)


def _full(shape):
    return pl.BlockSpec(shape, lambda *_: (0,) * len(shape))


def _table_specs(tm):
    return [pl.BlockSpec((None, tm // GRID_W, LANE), lambda i: (i, 0, 0)), _full((GRID_W, LANE))] * 4


def _resident(shape):
    return pl.BlockSpec(shape, lambda *_: (0,) * len(shape), pipeline_mode=pl.Buffered(1))


def _gather_weights(shards):
    n = len(shards)
    halves = [w.shape[0] // 2 for w in shards]

    def body(*refs):
        w_refs, out_refs, (send_sems, recv_sems) = refs[:n], refs[n:2 * n], refs[2 * n:]
        x, y, c = lax.axis_index("x"), lax.axis_index("y"), lax.axis_index("c")
        sibling = (x, y, 1 - c)
        chips = [(1 - x, y), (x, 1 - y), (1 - x, 1 - y)]
        me = 2 * x + y

        def copy(a, k, j, hc, to):
            part = out_refs[a].at[j, pl.ds(pl.multiple_of(hc * halves[a], BF16_ROWS), halves[a]), :]
            return pltpu.make_async_remote_copy(
                src_ref=part, dst_ref=part, send_sem=send_sems.at[6 * a + k], recv_sem=recv_sems.at[6 * a + k],
                device_id=to, device_id_type=MESH)

        started = []
        for a in range(n):
            out_refs[a][me] = w_refs[a][...].astype(BF16)
            for k, chip in enumerate(chips):
                started.append(copy(a, k, me, c, (*chip, c)))
                started[-1].start()
        for k, chip in enumerate(chips):
            for a in range(n):
                copy(a, k, 2 * chip[0] + chip[1], c, (*chip, c)).wait_recv()
                started.append(copy(a, 3 + k, 2 * chip[0] + chip[1], c, sibling))
                started[-1].start()
        for k, chip in enumerate(chips):
            for a in range(n):
                copy(a, 3 + k, 2 * chip[0] + chip[1], 1 - c, sibling).wait_recv()
        for cp in started:
            cp.wait_send()

    return _pallas(
        body, name="gather_weights",
        out_shape=[jax.ShapeDtypeStruct((N_CHIPS,) + w.shape, BF16) for w in shards],
        in_specs=[pl.BlockSpec(memory_space=pltpu.VMEM)] * n,
        out_specs=[pl.BlockSpec(memory_space=pltpu.VMEM)] * n,
        scratch_shapes=[pltpu.SemaphoreType.DMA((6 * n,)), pltpu.SemaphoreType.DMA((6 * n,))],
        compiler_params=_params(vmem=VMEM_SMALL),
    )(*shards)


def _reduce_grads(parts, small):
    n_big = len(parts)
    n = n_big + 1
    shapes = [p.shape[1:] for p in parts] + [small.shape]
    halves = [sh[0] // 2 for sh in shapes]

    def body(*refs):
        p_refs, out_refs, rec_a, rec_b = refs[:n], refs[n:2 * n], refs[2 * n:3 * n], refs[3 * n:4 * n]
        send_b = refs[4 * n:4 * n + n_big]
        sa_send, sa_recv, sb_send, sb_recv, sc_send, sc_recv = refs[4 * n + n_big:]
        x, y, c = lax.axis_index("x"), lax.axis_index("y"), lax.axis_index("c")
        sibling = (x, y, 1 - c)
        me = 2 * x + y

        def rows(a, hc):
            return pl.ds(pl.multiple_of(hc * halves[a], F32_ROWS), halves[a])

        def partial(a, j, hc):
            return p_refs[a].at[j, rows(a, hc), :] if a < n_big else p_refs[a].at[rows(a, hc), :]

        def copy_a(a, j):
            return pltpu.make_async_remote_copy(
                src_ref=partial(a, j, 1 - c), dst_ref=rec_a[a].at[j],
                send_sem=sa_send.at[N_CHIPS * a + j], recv_sem=sa_recv.at[N_CHIPS * a + j],
                device_id=sibling, device_id_type=MESH)

        def copy_b(a, r):
            j = me ^ r
            k = (N_CHIPS - 1) * a + r - 1
            return pltpu.make_async_remote_copy(
                src_ref=(send_b[a] if a < n_big else rec_a[a]).at[j], dst_ref=rec_b[a].at[r],
                send_sem=sb_send.at[k], recv_sem=sb_recv.at[k], device_id=(j // 2, j % 2, c), device_id_type=MESH)

        def copy_c(a):
            return pltpu.make_async_remote_copy(
                src_ref=out_refs[a].at[rows(a, c), :], dst_ref=out_refs[a].at[rows(a, c), :],
                send_sem=sc_send.at[a], recv_sem=sc_recv.at[a], device_id=sibling, device_id_type=MESH)

        for a in range(n):
            for j in range(N_CHIPS):
                copy_a(a, j).start()
        for r in range(1, N_CHIPS):
            j = me ^ r
            for a in range(n):
                copy_a(a, j).wait_recv()
                chip_part = rec_a[a][j] + partial(a, j, c)[...]
                if a < n_big:
                    send_b[a][j] = chip_part.astype(BF16)
                else:
                    rec_a[a][j] = chip_part
                copy_b(a, r).start()
        for a in range(n):
            copy_a(a, me).wait_recv()
            rec_b[a][0] = (rec_a[a][me] + partial(a, me, c)[...]).astype(rec_b[a].dtype)
        for a in range(n):
            for r in range(1, N_CHIPS):
                copy_b(a, r).wait_recv()
            total = rec_b[a][me].astype(F32)
            for j in range(1, N_CHIPS):
                total = total + rec_b[a][j ^ me].astype(F32)
            out_refs[a][rows(a, c), :] = total
            copy_c(a).start()
        for a in range(n):
            copy_c(a).wait_recv()
        for a in range(n):
            for j in range(N_CHIPS):
                copy_a(a, j).wait_send()
            for r in range(1, N_CHIPS):
                copy_b(a, r).wait_send()
            copy_c(a).wait_send()

    dma = pltpu.SemaphoreType.DMA
    return _pallas(
        body, name="reduce_grads",
        out_shape=[jax.ShapeDtypeStruct(sh, F32) for sh in shapes],
        in_specs=[pl.BlockSpec(memory_space=pltpu.VMEM)] * n,
        out_specs=[pl.BlockSpec(memory_space=pltpu.VMEM)] * n,
        scratch_shapes=[pltpu.VMEM((N_CHIPS, h) + sh[1:], F32) for h, sh in zip(halves, shapes)]
                       + [pltpu.VMEM((N_CHIPS, h) + sh[1:], BF16 if a < n_big else F32)
                          for a, (h, sh) in enumerate(zip(halves, shapes))]
                       + [pltpu.VMEM((N_CHIPS, h) + sh[1:], BF16) for h, sh in zip(halves[:n_big], shapes[:n_big])]
                       + [dma((N_CHIPS * n,)), dma((N_CHIPS * n,)), dma(((N_CHIPS - 1) * n,)), dma(((N_CHIPS - 1) * n,)),
                          dma((n,)), dma((n,))],
        compiler_params=_params(vmem=VMEM_LIMIT),
    )(*parts, small)


def _pre(x, tabs, w_in_ext, w_uq_pad, w_ukv_ext, gains, tm):
    s_len = x.shape[0]
    nt = s_len // tm

    def body(x_ref, car_ref, cac_ref, sar_ref, sac_ref, cbr_ref, cbc_ref, sbr_ref, sbc_ref, win_ref, wuq_ref, wukv_ref,
             gin_ref, gaq_ref, gak_ref, gcq_ref, gckv_ref, gbq_ref, gbk_ref,
             xn_ref, gates_ref, pre_ref, qbpre_ref, kbpre_ref, cq_ref, ckv_ref,
             qa_ref, ka_ref, va_ref, qb_ref, kb_ref, vb_ref, proj):
        xn = _rms_fwd(x_ref[...], gin_ref[...], D_MODEL)
        xn_ref[...] = jnp.transpose(xn).astype(BF16)
        xb = xn.astype(BF16)
        pre_ref[:, 0:VA0] = _nn(xb, win_ref[:, 0:VA0])
        gates_ref[...] = _nn(xb, win_ref[:, GA0:GA0 + N_GATE])
        pre_ref[:, VA0:N_PRE] = _nn(xb, win_ref[:, CQ0:KR0])
        proj[...] = _nn(xb, win_ref[:, VA0:GA0])
        kr = _nn(xb, win_ref[:, KR0:N_EXT])
        ca, sa, cb, sb = _token_tables((car_ref, cac_ref, sar_ref, sac_ref, cbr_ref, cbc_ref, sbr_ref, sbc_ref))
        lane = lax.broadcasted_iota(jnp.int32, (tm, LANE), 1)
        for h in range(A_HEADS):
            yq = _rms_fwd(pre_ref[:, QA0 + LANE * h:QA0 + LANE * (h + 1)], gaq_ref[...], A_DIM)
            qa_ref[h] = (_rope_fwd(yq, ca, sa) * (SCALE_A * LOG2E)).astype(BF16)
        for h in range(A_KV):
            yk = _rms_fwd(pre_ref[:, KA0 + LANE * h:KA0 + LANE * (h + 1)], gak_ref[...], A_DIM)
            ka_ref[h] = _rope_fwd(yk, ca, sa).astype(BF16)
            va_ref[h] = jnp.where(lane == A_DIM, 1.0, proj[:, LANE * h:LANE * (h + 1)]).astype(BF16)
        cq = _rms_fwd(pre_ref[:, VA0:VA0 + B_Q_RANK], gcq_ref[...], B_Q_RANK)
        cq_ref[...] = jnp.transpose(cq).astype(BF16)
        qbpre_ref[...] = _nn(cq.astype(BF16), wuq_ref[...])
        ckv = _rms_fwd(pre_ref[:, VA0 + B_Q_RANK:N_PRE], gckv_ref[...], B_KV_RANK)
        ckv_ref[...] = jnp.transpose(ckv).astype(BF16)
        kvb = _nn(ckv.astype(BF16), wukv_ref[...])
        for h in range(B_HEADS):
            yq = _rms_fwd(qbpre_ref[:, LANE * h:LANE * (h + 1)], gbq_ref[...], B_QK)
            qb_ref[h] = (_rope_fwd(yq, cb, sb) * (SCALE_B * LOG2E)).astype(BF16)
            kp = kvb[:, LANE * h:LANE * (h + 1)] + kr
            kbpre_ref[:, LANE * h:LANE * (h + 1)] = kp
            kb_ref[h] = _rope_fwd(_rms_fwd(kp, gbk_ref[...], B_QK), cb, sb).astype(BF16)
            vb_ref[h, :, 0:LANE] = kvb[:, B_HEADS * LANE + LANE * h:B_HEADS * LANE + LANE * (h + 1)].astype(BF16)
            vb_ref[h, :, LANE:2 * LANE] = jnp.where(lane == 0, 1.0, 0.0).astype(BF16)

    row = lambda w: pl.BlockSpec((tm, w), lambda i: (i, 0))
    col = lambda w: pl.BlockSpec((w, tm), lambda i: (0, i))
    heads = lambda n: pl.BlockSpec((n, tm, LANE), lambda i: (0, i, 0))
    hs = lambda n: jax.ShapeDtypeStruct((n, s_len, LANE), BF16)
    return _pallas(
        body, name="pre", grid=(nt,),
        in_specs=[row(D_MODEL)] + _table_specs(tm)
                 + [_resident(w_in_ext.shape), _resident(w_uq_pad.shape), _resident(w_ukv_ext.shape)]
                 + [_full(g.shape) for g in gains],
        out_specs=[col(D_MODEL), row(N_GATE), row(N_PRE), row(B_HEADS * LANE), row(B_HEADS * LANE),
                   col(B_Q_RANK), col(B_KV_RANK),
                   heads(A_HEADS), heads(A_KV), heads(A_KV), heads(B_HEADS), heads(B_HEADS),
                   pl.BlockSpec((B_HEADS, tm, 2 * LANE), lambda i: (0, i, 0))],
        out_shape=[jax.ShapeDtypeStruct((D_MODEL, s_len), BF16), jax.ShapeDtypeStruct((s_len, N_GATE), F32),
                   jax.ShapeDtypeStruct((s_len, N_PRE), F32), jax.ShapeDtypeStruct((s_len, B_HEADS * LANE), F32),
                   jax.ShapeDtypeStruct((s_len, B_HEADS * LANE), F32),
                   jax.ShapeDtypeStruct((B_Q_RANK, s_len), BF16), jax.ShapeDtypeStruct((B_KV_RANK, s_len), BF16),
                   hs(A_HEADS), hs(A_KV), hs(A_KV), hs(B_HEADS), hs(B_HEADS),
                   jax.ShapeDtypeStruct((B_HEADS, s_len, 2 * LANE), BF16)],
        scratch_shapes=[pltpu.VMEM((tm, A_KV * LANE), F32)],
        compiler_params=_params(("parallel",), VMEM_LIMIT),
    )(x, *tabs, w_in_ext, w_uq_pad, w_ukv_ext, *gains)


def _attn_fwd(q, k, v, group, l_col, tq, tk, tiles, name):
    n_heads, s_len, _ = q.shape
    v_w = v.shape[2]
    nk = s_len // tk

    def body(q_ref, k_ref, v_ref, o_ref, lse_ref, s_buf, p_buf, a_buf, m_ref, acc_ref):
        def scores(g, slot):
            s_buf[slot] = _nt(q_ref[_block_rows(g // nk, tq), :], k_ref[_block_rows(g % nk, tk), :])

        def softmax(g, slot):
            t = g // nk
            s = s_buf[slot]
            m_old = m_ref[t]
            m_new = jnp.maximum(m_old, jnp.max(s, axis=-1, keepdims=True))
            m_ref[t] = m_new
            a_buf[slot] = jnp.exp2(m_old - m_new)
            p_buf[slot] = jnp.exp2(s - jnp.tile(m_new, (1, tk // LANE))).astype(BF16)

        def values(g, slot):
            t = g // nk
            pv = _nn(p_buf[slot], v_ref[_block_rows(g % nk, tk), :])
            for c in range(0, v_w, LANE):
                acc_ref[t, :, c:c + LANE] = a_buf[slot] * acc_ref[t, :, c:c + LANE] + pv[:, c:c + LANE]

        m_ref[...] = jnp.full(m_ref.shape, -1e30, F32)
        acc_ref[...] = jnp.zeros(acc_ref.shape, F32)
        _three_stage(tiles * nk, scores, softmax, values)
        for t in range(tiles):
            l = acc_ref[t, :, l_col:l_col + 1]
            o = acc_ref[t, :, 0:LANE] * (1.0 / l)
            if l_col < LANE:
                lane = lax.broadcasted_iota(jnp.int32, o.shape, 1)
                o = jnp.where(lane == l_col, 0.0, o)
            o_ref[t * tq:(t + 1) * tq, :] = o
            lse_ref[t] = jnp.transpose(m_ref[t] + jnp.log2(jnp.broadcast_to(l, (tq, LANE))))[0:1, :]

    return _pallas(
        body, name=name, grid=(n_heads, s_len // (tiles * tq)),
        in_specs=[pl.BlockSpec((None, tiles * tq, LANE), lambda h, i: (h, i, 0)),
                  pl.BlockSpec((None, s_len, LANE), lambda h, i: (h // group, 0, 0)),
                  pl.BlockSpec((None, s_len, v_w), lambda h, i: (h // group, 0, 0))],
        out_specs=[pl.BlockSpec((None, tiles * tq, LANE), lambda h, i: (h, i, 0)),
                   pl.BlockSpec((None, tiles, 1, tq), lambda h, i: (h, i, 0, 0))],
        out_shape=[jax.ShapeDtypeStruct((n_heads, s_len, LANE), F32),
                   jax.ShapeDtypeStruct((n_heads, s_len // tq, 1, tq), F32)],
        scratch_shapes=[pltpu.VMEM((2, tq, tk), F32), pltpu.VMEM((2, tq, tk), BF16), pltpu.VMEM((2, tq, LANE), F32),
                        pltpu.VMEM((tiles, tq, LANE), F32), pltpu.VMEM((tiles, tq, v_w), F32)],
        compiler_params=_params(("parallel", "parallel"), VMEM_MID),
    )(q, k, v)


def _mid(x, target, o_a, o_b, gates, w_out_ext, tm):
    s_len = x.shape[0]
    nt = s_len // tm
    n_heads = A_HEADS + B_HEADS

    def body(x_ref, t_ref, oa_ref, ob_ref, g_ref, w_ref,
             yt_ref, dh_ref, dgate_ref, doa_ref, dob_ref, delta_ref, loss_ref, silu_scr, dsilu_scr, y_ref):
        @pl.when(pl.program_id(0) == 0)
        def _():
            loss_ref[...] = jnp.zeros_like(loss_ref)

        def o_of(h):
            return oa_ref[h] if h < A_HEADS else ob_ref[h - A_HEADS]

        for h in range(n_heads):
            cols = slice(LANE * h, LANE * (h + 1))
            g = g_ref[:, cols]
            sig = 1.0 / (1.0 + jnp.exp(-g))
            silu = g * sig
            silu_scr[:, cols] = silu
            dsilu_scr[:, cols] = sig * (1.0 + g * (1.0 - sig))
            y = o_of(h) * silu
            y_ref[:, cols] = y.astype(BF16)
            yt_ref[cols, :] = jnp.transpose(y).astype(BF16)
        err =x_ref[...] + _nn(y_ref[...], w_ref[...]) - t_ref[...]
        sq = jnp.sum(jnp.sum(err * err, axis=-1, keepdims=True), axis=0, keepdims=True)
        loss_ref[...] += jnp.broadcast_to(sq * (0.5 / D_MODEL), loss_ref.shape)
        dh = err * (1.0 / D_MODEL)
        dh_ref[...] = dh
        dy = _nt(dh.astype(BF16), w_ref[...])
        lane = lax.broadcasted_iota(jnp.int32, (tm, LANE), 1)
        delta = jnp.zeros((tm, LANE), F32)
        for h in range(n_heads):
            cols = slice(LANE * h, LANE * (h + 1))
            dyh = dy[:, cols]
            oh = o_of(h)
            do = dyh * silu_scr[:, cols]
            dgate_ref[:, cols] = (dyh * oh * dsilu_scr[:, cols]).astype(BF16)
            delta = jnp.where(lane == h, jnp.sum(do * oh, axis=-1, keepdims=True), delta)
            if h < A_HEADS:
                doa_ref[h] = do.astype(BF16)
            else:
                dob_ref[h - A_HEADS] = do.astype(BF16)
        delta_ref[...] = jnp.transpose(delta)[0:DELTA_ROWS, :]

    row = lambda w: pl.BlockSpec((tm, w), lambda i: (i, 0))
    heads = lambda n, w=LANE: pl.BlockSpec((n, tm, w), lambda i: (0, i, 0))
    return _pallas(
        body, name="mid", grid=(nt,),
        in_specs=[row(D_MODEL), row(D_MODEL), heads(A_HEADS), heads(B_HEADS), row(N_GATE), _resident(w_out_ext.shape)],
        out_specs=[pl.BlockSpec((N_GATE, tm), lambda i: (0, i)), row(D_MODEL), row(N_GATE), heads(A_HEADS), heads(B_HEADS),
                   pl.BlockSpec((DELTA_ROWS, tm), lambda i: (0, i)),
                   _full((8, LANE))],
        out_shape=[jax.ShapeDtypeStruct((N_GATE, s_len), BF16), jax.ShapeDtypeStruct((s_len, D_MODEL), F32),
                   jax.ShapeDtypeStruct((s_len, N_GATE), BF16),
                   jax.ShapeDtypeStruct((A_HEADS, s_len, LANE), BF16), jax.ShapeDtypeStruct((B_HEADS, s_len, LANE), BF16),
                   jax.ShapeDtypeStruct((DELTA_ROWS, s_len), F32), jax.ShapeDtypeStruct((8, LANE), F32)],
        scratch_shapes=[pltpu.VMEM((tm, N_GATE), F32), pltpu.VMEM((tm, N_GATE), F32), pltpu.VMEM((tm, N_GATE), BF16)],
        compiler_params=_params(("arbitrary",), VMEM_LIMIT),
    )(x, target, o_a, o_b, gates, w_out_ext)


def _attn_bwd(q, k, v, do, lse, delta, group, tq, tk, tiles, name):
    n_heads, s_len, _ = q.shape
    nq = s_len // tq

    def body(q_ref, do_ref, lse_ref, delta_ref, k_ref, v_ref, dq_ref, dk_ref, dv_ref, s_buf, dp_buf, p_buf, ds_buf):
        @pl.when(pl.program_id(1) == 0)
        def _():
            dq_ref[...] = jnp.zeros_like(dq_ref)

        dk_ref[...] = jnp.zeros_like(dk_ref)
        dv_ref[...] = jnp.zeros_like(dv_ref)

        def keys(g):
            return _block_rows(g // nq, tk)

        def queries(g):
            return _block_rows(g % nq, tq)

        def scores(g, slot):
            s_buf[slot] = _nt(k_ref[keys(g), :], q_ref[queries(g), :])
            dp_buf[slot] = _nt(v_ref[keys(g), :], do_ref[queries(g), :])

        def elementwise(g, slot):
            p = jnp.exp2(s_buf[slot] - lse_ref[g % nq])
            p_buf[slot] = p.astype(BF16)
            ds_buf[slot] = (p * (dp_buf[slot] - delta_ref[g % nq])).astype(BF16)

        def grads(g, slot):
            dv_ref[keys(g), :] += _nn(p_buf[slot], do_ref[queries(g), :])
            dk_ref[keys(g), :] += _nn(ds_buf[slot], q_ref[queries(g), :])
            dq_ref[queries(g), :] += _tn(ds_buf[slot], k_ref[keys(g), :])

        _three_stage(tiles * nq, scores, elementwise, grads)

    whole = lambda: pl.BlockSpec((None, s_len, LANE), lambda h, j: (h, 0, 0))
    stat = lambda: pl.BlockSpec((None, nq, 1, tq), lambda h, j: (h, 0, 0, 0))
    kvb = lambda: pl.BlockSpec((None, tiles * tk, LANE), lambda h, j: (h // group, j, 0))
    outb = lambda: pl.BlockSpec((None, tiles * tk, LANE), lambda h, j: (h, j, 0))
    shape = jax.ShapeDtypeStruct((n_heads, s_len, LANE), F32)
    return _pallas(
        body, name=name, grid=(n_heads, s_len // (tiles * tk)),
        in_specs=[whole(), whole(), stat(), stat(), kvb(), kvb()],
        out_specs=[whole(), outb(), outb()],
        out_shape=[shape, shape, shape],
        scratch_shapes=[pltpu.VMEM((2, tk, tq), F32), pltpu.VMEM((2, tk, tq), F32),
                        pltpu.VMEM((2, tk, tq), BF16), pltpu.VMEM((2, tk, tq), BF16)],
        compiler_params=_params(("parallel", "arbitrary"), VMEM_MID),
    )(q, do, lse, delta, k, v)


def _post(x, dh, pre, qbpre, kbpre, dgate, dqa, dka, dva, dqb, dkb, dvb, loss_part, tabs,
          w_in_ext, w_uq_pad, w_ukv_ext, gains, tm):
    s_len = x.shape[0]
    nt = s_len // tm

    def body(x_ref, dh_ref, pre_ref, qbpre_ref, kbpre_ref, dgate_ref,
             dqa_ref, dka_ref, dva_ref, dqb_ref, dkb_ref, dvb_ref, loss_ref,
             car_ref, cac_ref, sar_ref, sac_ref, cbr_ref, cbc_ref, sbr_ref, sbc_ref, win_ref, wuq_ref, wukv_ref,
             gin_ref, gaq_ref, gak_ref, gcq_ref, gckv_ref, gbq_ref, gbk_ref,
             gx_ref, dproj_ref, dqbpre_ref, dkvb_ref, dsm_ref):
        @pl.when(pl.program_id(0) == 0)
        def _():
            dsm_ref[...] = jnp.zeros_like(dsm_ref)
            dsm_ref[SM_LOSS:SM_LOSS + 1, 0:LANE] = loss_ref[0:1, :]

        def add_small(r, dg):
            dsm_ref[r:r + 1, 0:dg.shape[1]] += dg

        def tok_sum(a):
            return jnp.sum(a, axis=0, keepdims=True)

        ca, sa, cb, sb = _token_tables((car_ref, cac_ref, sar_ref, sac_ref, cbr_ref, cbc_ref, sbr_ref, sbc_ref))
        lane = lax.broadcasted_iota(jnp.int32, (tm, LANE), 1)

        nope_lanes = _lanes_of(lane, LAY_NOPE)

        def back(c0, c1):
            return _nt(dproj_ref[:, c0:c1], win_ref[:, c0:c1])

        dproj_ref[:, GA0:GA0 + N_GATE] = dgate_ref[...]
        dxn = back(GA0, GA0 + N_GATE)
        dg = jnp.zeros((1, LANE), F32)
        for h in range(A_HEADS):
            dn = _rope_bwd(dqa_ref[h] * SCALE_A, ca, sa)
            dx, dgr = _rms_bwd(dn, pre_ref[:, QA0 + LANE * h:QA0 + LANE * (h + 1)], gaq_ref[...], A_DIM)
            dproj_ref[:, QA0 + LANE * h:QA0 + LANE * (h + 1)] = dx.astype(BF16)
            dg = dg + tok_sum(dgr)
        add_small(SM_AQ, _unspread_row(dg, LAY_ROPE_A))
        dxn = dxn + back(QA0, KA0)
        dg = jnp.zeros((1, LANE), F32)
        for h in range(A_KV):
            dk = dka_ref[A_GROUP * h]
            dv = dva_ref[A_GROUP * h]
            for g in range(1, A_GROUP):
                dk = dk + dka_ref[A_GROUP * h + g]
                dv = dv + dva_ref[A_GROUP * h + g]
            dn = _rope_bwd(dk * LN2, ca, sa)
            dx, dgr = _rms_bwd(dn, pre_ref[:, KA0 + LANE * h:KA0 + LANE * (h + 1)], gak_ref[...], A_DIM)
            dproj_ref[:, KA0 + LANE * h:KA0 + LANE * (h + 1)] = dx.astype(BF16)
            dproj_ref[:, VA0 + LANE * h:VA0 + LANE * (h + 1)] = dv.astype(BF16)
            dg = dg + tok_sum(dgr)
        add_small(SM_AK, _unspread_row(dg, LAY_ROPE_A))
        dxn = dxn + back(KA0, GA0)
        dg = jnp.zeros((1, LANE), F32)
        for h in range(B_HEADS):
            cols = slice(LANE * h, LANE * (h + 1))
            dn = _rope_bwd(dqb_ref[h] * SCALE_B, cb, sb)
            dx, dgr = _rms_bwd(dn, qbpre_ref[:, cols], gbq_ref[...], B_QK)
            dqbpre_ref[:, cols] = dx.astype(BF16)
            dg = dg + tok_sum(dgr)
        add_small(SM_BQ, _unspread_row(dg, LAY_ROPE_B))
        dcq = _nt(dqbpre_ref[...], wuq_ref[...])
        dx, dgr = _rms_bwd(dcq, pre_ref[:, VA0:VA0 + B_Q_RANK], gcq_ref[...], B_Q_RANK)
        dproj_ref[:, CQ0:CQ0 + B_Q_RANK] = dx.astype(BF16)
        add_small(SM_CQ, tok_sum(dgr))
        dxn = dxn + back(CQ0, CKV0)
        dg = jnp.zeros((1, LANE), F32)
        dkr = jnp.zeros((tm, LANE), F32)
        for h in range(B_HEADS):
            cols = slice(LANE * h, LANE * (h + 1))
            dn = _rope_bwd(dkb_ref[h] * LN2, cb, sb)
            dx, dgr = _rms_bwd(dn, kbpre_ref[:, cols], gbk_ref[...], B_QK)
            dkvb_ref[:, cols] = jnp.where(nope_lanes, dx, 0.0).astype(BF16)
            dkvb_ref[:, B_HEADS * LANE + LANE * h:B_HEADS * LANE + LANE * (h + 1)] = dvb_ref[h].astype(BF16)
            dkr = dkr + dx
            dg = dg + tok_sum(dgr)
        add_small(SM_BK, _unspread_row(dg, LAY_ROPE_B))
        dproj_ref[:, KR0:KR0 + LANE] = jnp.where(_lanes_of(lane, LAY_KR), dkr, 0.0).astype(BF16)
        dckv = _nt(dkvb_ref[...], wukv_ref[...])
        dx, dgr = _rms_bwd(dckv, pre_ref[:, VA0 + B_Q_RANK:N_PRE], gckv_ref[...], B_KV_RANK)
        dproj_ref[:, CKV0:CKV0 + B_KV_RANK] = dx.astype(BF16)
        add_small(SM_CKV, tok_sum(dgr))
        dxn = dxn + back(CKV0, N_EXT)
        dx, dgr = _rms_bwd(dxn, x_ref[...], gin_ref[...], D_MODEL)
        gx_ref[...] = dh_ref[...] + dx
        add_small(SM_IN, tok_sum(dgr))

    row = lambda w: pl.BlockSpec((tm, w), lambda i: (i, 0))
    heads = lambda n: pl.BlockSpec((n, tm, LANE), lambda i: (0, i, 0))
    return _pallas(
        body, name="post", grid=(nt,),
        in_specs=[row(D_MODEL), row(D_MODEL), row(N_PRE), row(B_HEADS * LANE), row(B_HEADS * LANE), row(N_GATE),
                  heads(A_HEADS), heads(A_HEADS), heads(A_HEADS), heads(B_HEADS), heads(B_HEADS), heads(B_HEADS),
                  _full(loss_part.shape)] + _table_specs(tm)
                 + [_resident(w_in_ext.shape), _resident(w_uq_pad.shape), _resident(w_ukv_ext.shape)]
                 + [_full(g.shape) for g in gains],
        out_specs=[row(D_MODEL), row(N_EXT), row(B_HEADS * LANE), row(2 * B_HEADS * LANE), _full((SM_ROWS, SM_W))],
        out_shape=[jax.ShapeDtypeStruct((s_len, D_MODEL), F32), jax.ShapeDtypeStruct((s_len, N_EXT), BF16),
                   jax.ShapeDtypeStruct((s_len, B_HEADS * LANE), BF16),
                   jax.ShapeDtypeStruct((s_len, 2 * B_HEADS * LANE), BF16),
                   jax.ShapeDtypeStruct((SM_ROWS, SM_W), F32)],
        compiler_params=_params(("arbitrary",), VMEM_LIMIT),
    )(x, dh, pre, qbpre, kbpre, dgate, dqa, dka, dva, dqb, dkb, dvb, loss_part, *tabs,
      w_in_ext, w_uq_pad, w_ukv_ext, *gains)


def _grad_w(a_t, b, tn, ts, name):
    m, s_len = a_t.shape
    n = b.shape[1]

    def body(a_ref, b_ref, o_ref):
        @pl.when(pl.program_id(1) == 0)
        def _():
            o_ref[...] = jnp.zeros_like(o_ref)

        o_ref[...] += _nn(a_ref[...], b_ref[...].astype(BF16))

    return _pallas(
        body, name=name, grid=(n // tn, s_len // ts),
        in_specs=[pl.BlockSpec((m, ts), lambda j, t: (0, t)), pl.BlockSpec((ts, tn), lambda j, t: (t, j))],
        out_specs=pl.BlockSpec((m, tn), lambda j, t: (0, j)),
        out_shape=jax.ShapeDtypeStruct((m, n), F32),
        compiler_params=_params(("parallel", "arbitrary"), VMEM_MID),
    )(a_t, b)


def _grad_w_pairs(pairs, ts, name):
    s_len = pairs[0][0].shape[1]
    n_p = len(pairs)

    def body(*refs):
        for a_ref, b_ref, o_ref in zip(refs[0:2 * n_p:2], refs[1:2 * n_p:2], refs[2 * n_p:]):
            @pl.when(pl.program_id(0) == 0)
            def _():
                o_ref[...] = jnp.zeros_like(o_ref)

            o_ref[...] += _nn(a_ref[...], b_ref[...].astype(BF16))

    in_specs, flat = [], []
    for a_t, b in pairs:
        in_specs += [pl.BlockSpec((a_t.shape[0], ts), lambda t: (0, t)), pl.BlockSpec((ts, b.shape[1]), lambda t: (t, 0))]
        flat += [a_t, b]
    return _pallas(
        body, name=name, grid=(s_len // ts,),
        in_specs=in_specs,
        out_specs=[_full((a_t.shape[0], b.shape[1])) for a_t, b in pairs],
        out_shape=[jax.ShapeDtypeStruct((a_t.shape[0], b.shape[1]), F32) for a_t, b in pairs],
        compiler_params=_params(("arbitrary",), VMEM_MID),
    )(*flat)


def _adam_math(w, g, m, v):
    nm = ADAM_B1 * m + (1.0 - ADAM_B1) * g
    nv = ADAM_B2 * v + (1.0 - ADAM_B2) * (g * g)
    m_hat = nm / (1.0 - ADAM_B1 ** ADAM_STEP)
    v_hat = nv / (1.0 - ADAM_B2 ** ADAM_STEP)
    return -ADAM_LR * (m_hat / (jnp.sqrt(v_hat) + ADAM_EPS) + ADAM_WD * w), nm, nv


def _adamw_rows(w, g, m, v, tr):
    rows, cols = w.shape

    def body(w_ref, g_ref, m_ref, v_ref, d_ref, nm_ref, nv_ref):
        d_ref[...], nm_ref[...], nv_ref[...] = _adam_math(w_ref[...], g_ref[...], m_ref[...], v_ref[...])

    blk = pl.BlockSpec((tr, cols), lambda i: (i, 0))
    shape = jax.ShapeDtypeStruct((rows, cols), F32)
    return _pallas(
        body, name="adamw_w_in", grid=(rows // tr,),
        in_specs=[blk] * 4, out_specs=[blk] * 3, out_shape=[shape] * 3,
        compiler_params=_params(("parallel",), VMEM_SMALL),
    )(w, g, m, v)


def _adamw_rest(bigs, smalls, g_small):
    nb, ns = len(bigs), len(smalls)

    def body(*refs):
        ins, outs = refs[:4 * nb + 3 * ns + 1], refs[4 * nb + 3 * ns + 1:]
        for i in range(nb):
            w_ref, g_ref, m_ref, v_ref = ins[4 * i:4 * i + 4]
            d_ref, nm_ref, nv_ref = outs[3 * i:3 * i + 3]
            d_ref[...], nm_ref[...], nv_ref[...] = _adam_math(w_ref[...], g_ref[...], m_ref[...], v_ref[...])
        gs_ref = ins[-1]
        for i in range(ns):
            w_ref, m_ref, v_ref = ins[4 * nb + 3 * i:4 * nb + 3 * i + 3]
            g_ref, d_ref, nm_ref, nv_ref = outs[3 * nb + 4 * i:3 * nb + 4 * i + 4]
            g = gs_ref[i:i + 1, 0:w_ref.shape[1]]
            g_ref[...] = g
            d_ref[...], nm_ref[...], nv_ref[...] = _adam_math(w_ref[...], g, m_ref[...], v_ref[...])

    flat_in = [a for quad in bigs for a in quad] + [a for tri in smalls for a in tri] + [g_small]
    out_shape = ([jax.ShapeDtypeStruct(q[0].shape, F32) for q in bigs for _ in range(3)]
                 + [jax.ShapeDtypeStruct(t[0].shape, F32) for t in smalls for _ in range(4)])
    return _pallas(
        body, name="adamw_rest",
        in_specs=[pl.BlockSpec(memory_space=pltpu.VMEM)] * len(flat_in),
        out_specs=[pl.BlockSpec(memory_space=pltpu.VMEM)] * len(out_shape),
        out_shape=out_shape,
        compiler_params=_params(vmem=VMEM_SMALL),
    )(*flat_in)


def _place(pieces, n):
    out, at = [], 0
    for lane0, arr in sorted(pieces, key=lambda p: p[0]):
        out += [jnp.zeros((n, lane0 - at), F32), arr]
        at = lane0 + arr.shape[1]
    return jnp.concatenate(out + [jnp.zeros((n, LANE - at), F32)], axis=1)


def _rope_tables(s_len, tm):
    rows = s_len // GRID_W
    row = jnp.arange(rows, dtype=F32)
    col = jnp.arange(GRID_W, dtype=F32)

    def lay(dim, layout, first_dim, ones):
        half = dim // 2
        inv = 1.0 / (ROPE_THETA ** (jnp.arange(0, half, 2, dtype=F32) / half))
        ang_r, ang_c = row[:, None] * inv[None, :], col[:, None] * inv[None, :]
        at = {a - first_dim: lane0 for a, _, lane0 in layout}
        q = dim // 4
        r1, r2, c1, c2 = at[0], at[q], at[2 * q], at[3 * q]
        cos_r = _place([(r1, jnp.cos(ang_r)), (r2, jnp.cos(ang_r))], rows)
        sin_r = _place([(r1, -jnp.sin(ang_r)), (r2, jnp.sin(ang_r))], rows)
        cos_c = _place([(c1, jnp.cos(ang_c)), (c2, jnp.cos(ang_c))] + [(l0, jnp.ones((GRID_W, n), F32)) for _, n, l0 in ones],
                       GRID_W)
        sin_c = _place([(c1, -jnp.sin(ang_c)), (c2, jnp.sin(ang_c))], GRID_W)
        by_block = (s_len // tm, tm // GRID_W, LANE)
        return cos_r.reshape(by_block), cos_c, sin_r.reshape(by_block), sin_c

    return lay(A_DIM, LAY_ROPE_A, 0, ()) + lay(B_ROPE, LAY_KR, 0, LAY_NOPE)


def _spread(w, n_heads, dim, axis, layout):
    w3 = w.reshape(w.shape[:axis] + (n_heads, dim) + w.shape[axis + 1:])
    out, at = [], 0

    def zeros(n):
        return jnp.zeros(w3.shape[:axis + 1] + (n,) + w3.shape[axis + 2:], w.dtype)

    for a0, n, lane0 in sorted(layout, key=lambda seg: seg[2]):
        out += [zeros(lane0 - at), lax.slice_in_dim(w3, a0, a0 + n, axis=axis + 1)]
        at = lane0 + n
    out = jnp.concatenate(out + [zeros(LANE - at)], axis=axis + 1)
    return out.reshape(w.shape[:axis] + (n_heads * LANE,) + w.shape[axis + 1:])


def _unspread(w, n_heads, dim, axis, layout):
    w3 = w.reshape(w.shape[:axis] + (n_heads, LANE) + w.shape[axis + 1:])
    parts = [lax.slice_in_dim(w3, lane0, lane0 + n, axis=axis + 1) for _, n, lane0 in sorted(layout)]
    out = parts[0] if len(parts) == 1 else jnp.concatenate(parts, axis=axis + 1)
    return out.reshape(w.shape[:axis] + (n_heads * dim,) + w.shape[axis + 1:])


def _head_cols(first, n_heads, dim, layout):
    out = np.full((n_heads * LANE,), -1, np.int32)
    for h in range(n_heads):
        for a0, n, lane0 in layout:
            out[h * LANE + lane0:h * LANE + lane0 + n] = first + h * dim + a0 + np.arange(n)
    return out


def _inverse(src, n):
    dst = np.full((n,), -1, np.int32)
    dst[src[src >= 0]] = np.nonzero(src >= 0)[0]
    return dst


def _column_maps():
    a_w, kv_w = A_HEADS * A_DIM, A_KV * A_DIM
    o_g = a_w + 2 * kv_w
    o_cq = o_g + a_w
    o_kr = o_cq + B_Q_RANK + B_KV_RANK
    src_in = np.concatenate([
        _head_cols(0, A_HEADS, A_DIM, LAY_ROPE_A), _head_cols(a_w, A_KV, A_DIM, LAY_ROPE_A),
        _head_cols(a_w + kv_w, A_KV, A_DIM, LAY_PLAIN_A), _head_cols(o_g, A_HEADS, A_DIM, LAY_PLAIN_A),
        np.arange(o_kr + B_ROPE, N_IN), np.arange(o_cq, o_kr), _head_cols(o_kr, 1, B_ROPE, LAY_KR)]).astype(np.int32)
    src_uq = _head_cols(0, B_HEADS, B_QK, LAY_ROPE_B)
    per = B_NOPE + B_V
    src_ukv = np.concatenate([_head_cols(0, B_HEADS, per, LAY_NOPE),
                              _head_cols(B_NOPE, B_HEADS, per, ((0, B_V, 0),))]).astype(np.int32)
    assert len(src_in) == N_EXT
    return src_in, src_uq, src_ukv


def _round_up(n, m):
    return (n + m - 1) // m * m


def _permute_cols(xs, maps, stacks, name):
    maps = [np.asarray(m, np.int32) for m in maps]
    n_arr = len(xs)

    def block(ref, b):
        if len(ref.shape) == 2:
            return ref.at[:, b * LANE:(b + 1) * LANE]
        per = ref.shape[2] // LANE
        return ref.at[b // per, :, (b % per) * LANE:(b % per + 1) * LANE]

    def body(*refs):
        row = lax.broadcasted_iota(jnp.int32, (LANE, LANE), 0)
        for x_ref, src_ref, o_ref, src in zip(refs[:n_arr], refs[n_arr:2 * n_arr], refs[2 * n_arr:], maps):
            for c in range(len(src) // LANE):
                want = src[c * LANE:(c + 1) * LANE]
                if want[0] >= 0 and want[0] % LANE == 0 and np.array_equal(want, want[0] + np.arange(LANE)):
                    block(o_ref, c)[...] = block(x_ref, int(want[0]) // LANE)[...]
                    continue
                acc = jnp.zeros((x_ref.shape[-2], LANE), F32)
                for kb in sorted({int(v) // LANE for v in want if v >= 0}):
                    sel = jnp.where(row + kb * LANE == src_ref[:, c * LANE:(c + 1) * LANE], 1.0, 0.0).astype(BF16)
                    part = block(x_ref, kb)[...]
                    if part.dtype == BF16:
                        acc = acc + _nn(part, sel)
                    else:
                        hi = part.astype(BF16)
                        rest = part - hi.astype(F32)
                        mid = rest.astype(BF16)
                        low = (rest - mid.astype(F32)).astype(BF16)
                        acc = acc + ((_nn(hi, sel) + _nn(mid, sel)) + _nn(low, sel))
                block(o_ref, c)[...] = acc.astype(o_ref.dtype)

    def out_shape(x, m, stack):
        rows = x.shape[-2]
        return (rows, len(m)) if stack is None else (stack, rows, len(m) // stack)

    return _pallas(
        body, name=name,
        out_shape=[jax.ShapeDtypeStruct(out_shape(x, m, st), x.dtype) for x, m, st in zip(xs, maps, stacks)],
        compiler_params=_params(vmem=VMEM_MID),
    )(*xs, *[jnp.asarray(m).reshape(1, -1) for m in maps])


def _pad_cols(w):
    return jnp.pad(w, ((0, 0), (0, _round_up(w.shape[1], LANE) - w.shape[1])))


def _in_stack(cols, width):
    cols = np.asarray(cols)
    return np.where(cols < 0, -1, cols // width * _round_up(width, LANE) + cols % width).astype(np.int32)


def _ext_weights(g_in, g_uq, g_ukv, g_out):
    src_in, src_uq, src_ukv = _column_maps()
    w_uq = g_uq.reshape(B_Q_RANK, B_HEADS * B_QK)
    w_out = g_out.reshape(D_MODEL, D_MODEL)
    w_in_ext, w_uq_pad, w_ukv_ext = _permute_cols(
        [g_in, w_uq, g_ukv], [_in_stack(src_in, SH_IN[1]), src_uq, _in_stack(src_ukv, SH_UKV[1])], [None] * 3, "lay_out_weights")
    a_w = A_HEADS * A_DIM
    w_out_ext = jnp.concatenate([_spread(w_out[:a_w], A_HEADS, A_DIM, 0, LAY_PLAIN_A), w_out[a_w:]], axis=0)
    return w_in_ext, w_uq_pad, w_ukv_ext, w_out_ext


def _fold_grads(d_in_ext, d_uq_pad, d_ukv_ext, d_out_ext):
    src_in, src_uq, src_ukv = _column_maps()

    def back(src, n, width):
        inv = _inverse(src, n)
        wide = _round_up(width, LANE)
        out = np.full((n // width * wide,), -1, np.int32)
        for j in range(n // width):
            out[j * wide:j * wide + width] = inv[j * width:(j + 1) * width]
        return out

    n_uq, n_ukv = B_HEADS * B_QK, B_HEADS * (B_NOPE + B_V)
    d_in, d_uq, d_ukv = _permute_cols(
        [d_in_ext, d_uq_pad, d_ukv_ext], [back(src_in, N_IN, SH_IN[1]), _inverse(src_uq, n_uq), back(src_ukv, n_ukv, SH_UKV[1])],
        [N_CHIPS, None, N_CHIPS], "fold_grads")
    d_out = jnp.concatenate([_unspread(d_out_ext[:A_HEADS * LANE], A_HEADS, A_DIM, 0, LAY_PLAIN_A), d_out_ext[A_HEADS * LANE:]], axis=0)
    return d_in, d_uq.reshape((N_CHIPS,) + SH_UQ), d_ukv, d_out.reshape((N_CHIPS,) + SH_OUT)


def kernel(x, norm_in, w_in, a_q_norm, a_k_norm, b_cq_norm, b_ckv_norm, w_uq, w_ukv, b_q_norm, b_k_norm, w_out, loss_target, m_norm_in, m_w_in, m_a_q_norm, m_a_k_norm, m_b_cq_norm, m_b_ckv_norm, m_w_uq, m_w_ukv, m_b_q_norm, m_b_k_norm, m_w_out, v_norm_in, v_w_in, v_a_q_norm, v_a_k_norm, v_b_cq_norm, v_b_ckv_norm, v_w_uq, v_w_ukv, v_b_q_norm, v_b_k_norm, v_w_out):
    s_len = x.shape[1]
    xs, ts = x[0], loss_target[0]
    tm = min(256, s_len)
    tq, tk_f = min(512, s_len // 2), min(2048, s_len // 2)
    tq_b, tk_b = min(1024, s_len // 2), min(512, s_len)
    tiles_f = min(4, s_len // tq)
    tiles_b = min(2, s_len // tk_b)

    w_in_ext, w_uq_pad, w_ukv_ext, w_out_ext = _ext_weights(
        *_gather_weights((_pad_cols(w_in[0]), w_uq[0], _pad_cols(w_ukv[0]), w_out[0])))
    gains = (norm_in, _spread(a_q_norm, 1, A_DIM, 1, LAY_ROPE_A), _spread(a_k_norm, 1, A_DIM, 1, LAY_ROPE_A), b_cq_norm, b_ckv_norm,
             _spread(b_q_norm, 1, B_QK, 1, LAY_ROPE_B), _spread(b_k_norm, 1, B_QK, 1, LAY_ROPE_B))
    tabs = _rope_tables(s_len, tm)

    (xn_t, gates, pre, qbpre, kbpre, cq_t, ckv_t, qa, ka, va, qb, kb, vb) = _pre(
        xs, tabs, w_in_ext, w_uq_pad, w_ukv_ext, gains, tm)
    o_a, lse_a = _attn_fwd(qa, ka, va, A_GROUP, A_DIM, tq, tk_f, tiles_f, "attn_fwd_a")
    o_b, lse_b = _attn_fwd(qb, kb, vb, 1, B_V, tq, tk_f, tiles_f, "attn_fwd_b")
    y_t, dh, dgate, do_a, do_b, delta, loss_part = _mid(xs, ts, o_a, o_b, gates, w_out_ext, min(512, s_len))

    def stat(a):
        return a.reshape(a.shape[0], s_len // tq_b, 1, tq_b)

    dqa, dka, dva = _attn_bwd(qa, ka, va, do_a, stat(lse_a), stat(delta[:A_HEADS]), A_GROUP, tq_b, tk_b, tiles_b, "attn_bwd_a")
    dqb, dkb, dvb = _attn_bwd(qb, kb, vb, do_b, stat(lse_b), stat(delta[A_HEADS:A_HEADS + B_HEADS]), 1, tq_b, tk_b,
                              tiles_b, "attn_bwd_b")
    grad_x, dproj, dqbpre, dkvb, d_small = _post(
        xs, dh, pre, qbpre, kbpre, dgate, dqa, dka, dva, dqb, dkb, dvb, loss_part, tabs,
        w_in_ext, w_uq_pad, w_ukv_ext, gains, tm)

    ts_w = min(2048, s_len)
    d_in_ext = _grad_w(xn_t, dproj, 768, min(2 * ts_w, s_len), "grad_w_in")
    d_out_ext = _grad_w(y_t, dh, 512, ts_w, "grad_w_out")
    d_uq_pad, d_ukv_ext = _grad_w_pairs([(cq_t, dqbpre), (ckv_t, dkvb)], ts_w, "grad_w_mla")

    g_in_p, g_uq, g_ukv_p, g_out, g_small = _reduce_grads(_fold_grads(d_in_ext, d_uq_pad, d_ukv_ext, d_out_ext), d_small)
    g_in, g_ukv = g_in_p[:, :SH_IN[1]], g_ukv_p[:, :SH_UKV[1]]
    d_in, nm_in, nv_in = (a.T for a in _adamw_rows(w_in[0].T, g_in_p.T[:SH_IN[1]], m_w_in[0].T, v_w_in[0].T, SH_IN[1] // 7))
    rest = _adamw_rest(
        [(w_uq[0], g_uq, m_w_uq[0], v_w_uq[0]), (w_ukv[0], g_ukv, m_w_ukv[0], v_w_ukv[0]),
         (w_out[0], g_out, m_w_out[0], v_w_out[0])],
        [(norm_in, m_norm_in, v_norm_in), (a_q_norm, m_a_q_norm, v_a_q_norm), (a_k_norm, m_a_k_norm, v_a_k_norm),
         (b_cq_norm, m_b_cq_norm, v_b_cq_norm), (b_ckv_norm, m_b_ckv_norm, v_b_ckv_norm),
         (b_q_norm, m_b_q_norm, v_b_q_norm), (b_k_norm, m_b_k_norm, v_b_k_norm)], g_small)
    (d_uq, nm_uq, nv_uq), (d_ukv, nm_ukv, nv_ukv), (d_out, nm_out, nv_out) = (rest[3 * i:3 * i + 3] for i in range(3))
    sm = [rest[9 + 4 * i:9 + 4 * i + 4] for i in range(7)]

    def leaves(k, p_in, p_uq, p_ukv, p_out):
        return [sm[SM_IN][k], p_in[None], sm[SM_AQ][k], sm[SM_AK][k], sm[SM_CQ][k], sm[SM_CKV][k], p_uq[None], p_ukv[None],
                sm[SM_BQ][k], sm[SM_BK][k], p_out[None]]

    return (g_small[SM_LOSS, 0], grad_x[None], *leaves(0, g_in, g_uq, g_ukv, g_out), *leaves(1, d_in, d_uq, d_ukv, d_out),
            *leaves(2, nm_in, nm_uq, nm_ukv, nm_out), *leaves(3, nv_in, nv_uq, nv_ukv, nv_out))
```

```python
import jax
import jax.numpy as jnp
import numpy as np
from jax import lax
from jax.experimental import pallas as pl
from jax.experimental.pallas import tpu as pltpu

F32 = jnp.float32
BF16 = jnp.bfloat16
MESH = pl.DeviceIdType.MESH

D_MODEL = 1024
GRID_W = 64
ROPE_THETA = 10000.0
EPS = 1e-6
A_HEADS, A_KV, A_DIM = 8, 2, 64
A_GROUP = A_HEADS // A_KV
B_HEADS, B_NOPE, B_ROPE, B_V = 4, 64, 32, 128
B_QK = B_NOPE + B_ROPE
B_Q_RANK, B_KV_RANK = 384, 256
N_IN = 2464
SCALE_A = 1.0 / float(np.sqrt(A_DIM))
SCALE_B = 1.0 / float(np.sqrt(B_QK))
LOG2E = float(np.log2(np.e))
LN2 = float(np.log(2.0))
ADAM_LR, ADAM_B1, ADAM_B2, ADAM_EPS, ADAM_WD, ADAM_STEP = 0.001, 0.9, 0.999, 1e-08, 0.01, 10

LANE = 128
VMEM_BYTES = 64 * 1024 * 1024
VMEM_LIMIT = VMEM_BYTES - 8 * 1024 * 1024
VMEM_MID = 48 * 1024 * 1024
VMEM_SMALL = 32 * 1024 * 1024

QA0 = 0
KA0 = QA0 + A_HEADS * LANE
VA0 = KA0 + A_KV * LANE
GA0 = VA0 + A_KV * LANE
GB0 = GA0 + A_HEADS * LANE
CQ0 = GB0 + B_HEADS * LANE
CKV0 = CQ0 + B_Q_RANK
KR0 = CKV0 + B_KV_RANK
N_EXT = KR0 + LANE
N_GATE = (A_HEADS + B_HEADS) * LANE
DELTA_ROWS = 16
N_PRE = KA0 + A_KV * LANE + B_Q_RANK + B_KV_RANK

ROT = LANE // 2
_QA = A_DIM // 4
_QB = B_ROPE // 4
LAY_PLAIN_A = ((0, A_DIM, 0),)
LAY_ROPE_A = ((0, _QA, 0), (2 * _QA, _QA, _QA), (_QA, _QA, ROT), (3 * _QA, _QA, ROT + _QA))
LAY_KR = ((0, _QB, 0), (2 * _QB, _QB, _QB), (_QB, _QB, ROT), (3 * _QB, _QB, ROT + _QB))
LAY_NOPE = ((0, B_NOPE // 2, 2 * _QB), (B_NOPE // 2, B_NOPE // 2, ROT + 2 * _QB))
LAY_ROPE_B = LAY_NOPE + tuple((B_NOPE + a, n, at) for a, n, at in LAY_KR)

N_CHIPS = 4
SH_IN = (D_MODEL, N_IN // N_CHIPS)
SH_UQ = (B_Q_RANK // N_CHIPS, B_HEADS * B_QK)
SH_UKV = (B_KV_RANK, B_HEADS * (B_NOPE + B_V) // N_CHIPS)
SH_OUT = (D_MODEL // N_CHIPS, D_MODEL)
SM_ROWS, SM_W = 16, D_MODEL
SM_IN, SM_AQ, SM_AK, SM_CQ, SM_CKV, SM_BQ, SM_BK, SM_LOSS = range(8)
F32_ROWS, BF16_ROWS = 8, 16


def _pallas(body, **kw):
    return pl.pallas_call(body, **kw)


def _params(sem=None, vmem=None):
    return pltpu.CompilerParams(dimension_semantics=sem, vmem_limit_bytes=vmem)


def _rms_fwd(x, g, n):
    r = lax.rsqrt(jnp.sum(x * x, axis=-1, keepdims=True) * (1.0 / n) + EPS)
    return x * r * g


def _rms_bwd(dy, x, g, n):
    u = dy * g
    r = lax.rsqrt(jnp.sum(x * x, axis=-1, keepdims=True) * (1.0 / n) + EPS)
    ux = jnp.sum(u * x, axis=-1, keepdims=True)
    xhat = x * r
    dx = r * (u - xhat * (r * ux * (1.0 / n)))
    return dx, dy * xhat


def _rope_fwd(y, cos, sin):
    return y * cos + pltpu.roll(y, ROT, 1) * sin


def _rope_bwd(d, cos, sin):
    return d * cos - pltpu.roll(d, ROT, 1) * sin


def _token_tables(refs):
    out = []
    for r_ref, c_ref in zip(refs[0::2], refs[1::2]):
        r, c = r_ref[...], c_ref[...]
        out.append(jnp.concatenate([r[k:k + 1, :] + c for k in range(r.shape[0])], axis=0))
    return out


def _lanes_of(lane, layout):
    m = None
    for _, n, at in layout:
        seg = (lane >= at) & (lane < at + n)
        m = seg if m is None else (m | seg)
    return m


def _unspread_row(v, layout):
    v8 = jnp.broadcast_to(v, (F32_ROWS, LANE))
    lane = lax.broadcasted_iota(jnp.int32, v8.shape, 1)
    out = jnp.zeros_like(v8)
    for a, n, at in layout:
        moved = v8 if a == at else pltpu.roll(v8, (a - at) % LANE, 1)
        out = jnp.where((lane >= a) & (lane < a + n), moved, out)
    return out[0:1, :]


def _nt(a, b):
    return lax.dot_general(a, b, (((1,), (1,)), ((), ())), preferred_element_type=F32)


def _tn(a, b):
    return lax.dot_general(a, b, (((0,), (0,)), ((), ())), preferred_element_type=F32)


def _nn(a, b):
    return jnp.dot(a, b, preferred_element_type=F32)


def _block_rows(i, size):
    if isinstance(i, int):
        return pl.ds(i * size, size)
    return pl.ds(pl.multiple_of(i * size, size), size)


MAX_STATIC_BLOCKS = 32


def _three_stage(n, first, second, third):
    assert n >= 2 and n % 2 == 0
    first(0, 0)
    first(1, 1)
    second(0, 0)
    if n <= MAX_STATIC_BLOCKS:
        for i in range(1, n - 1):
            first(i + 1, (i + 1) % 2)
            second(i, i % 2)
            third(i - 1, (i - 1) % 2)
    else:
        def pair(t, carry):
            i = 2 * t + 1
            first(i + 1, 0)
            second(i, 1)
            third(i - 1, 0)
            first(i + 2, 1)
            second(i + 1, 0)
            third(i, 1)
            return carry

        lax.fori_loop(0, (n - 2) // 2, pair, 0)
    second(n - 1, 1)
    third(n - 2, 0)
    third(n - 1, 1)


def _full(shape):
    return pl.BlockSpec(shape, lambda *_: (0,) * len(shape))


def _table_specs(tm):
    return [pl.BlockSpec((None, tm // GRID_W, LANE), lambda i: (i, 0, 0)), _full((GRID_W, LANE))] * 4


def _resident(shape):
    return pl.BlockSpec(shape, lambda *_: (0,) * len(shape), pipeline_mode=pl.Buffered(1))


def _gather_weights(shards):
    n = len(shards)
    halves = [w.shape[0] // 2 for w in shards]

    def body(*refs):
        w_refs, out_refs, (send_sems, recv_sems) = refs[:n], refs[n:2 * n], refs[2 * n:]
        x, y, c = lax.axis_index("x"), lax.axis_index("y"), lax.axis_index("c")
        sibling = (x, y, 1 - c)
        chips = [(1 - x, y), (x, 1 - y), (1 - x, 1 - y)]
        me = 2 * x + y

        def copy(a, k, j, hc, to):
            part = out_refs[a].at[j, pl.ds(pl.multiple_of(hc * halves[a], BF16_ROWS), halves[a]), :]
            return pltpu.make_async_remote_copy(
                src_ref=part, dst_ref=part, send_sem=send_sems.at[6 * a + k], recv_sem=recv_sems.at[6 * a + k],
                device_id=to, device_id_type=MESH)

        started = []
        for a in range(n):
            out_refs[a][me] = w_refs[a][...].astype(BF16)
            for k, chip in enumerate(chips):
                started.append(copy(a, k, me, c, (*chip, c)))
                started[-1].start()
        for k, chip in enumerate(chips):
            for a in range(n):
                copy(a, k, 2 * chip[0] + chip[1], c, (*chip, c)).wait_recv()
                started.append(copy(a, 3 + k, 2 * chip[0] + chip[1], c, sibling))
                started[-1].start()
        for k, chip in enumerate(chips):
            for a in range(n):
                copy(a, 3 + k, 2 * chip[0] + chip[1], 1 - c, sibling).wait_recv()
        for cp in started:
            cp.wait_send()

    return _pallas(
        body, name="gather_weights",
        out_shape=[jax.ShapeDtypeStruct((N_CHIPS,) + w.shape, BF16) for w in shards],
        in_specs=[pl.BlockSpec(memory_space=pltpu.VMEM)] * n,
        out_specs=[pl.BlockSpec(memory_space=pltpu.VMEM)] * n,
        scratch_shapes=[pltpu.SemaphoreType.DMA((6 * n,)), pltpu.SemaphoreType.DMA((6 * n,))],
        compiler_params=_params(vmem=VMEM_SMALL),
    )(*shards)


def _reduce_grads(parts, small):
    n_big = len(parts)
    n = n_big + 1
    shapes = [p.shape[1:] for p in parts] + [small.shape]
    halves = [sh[0] // 2 for sh in shapes]

    def body(*refs):
        p_refs, out_refs, rec_a, rec_b = refs[:n], refs[n:2 * n], refs[2 * n:3 * n], refs[3 * n:4 * n]
        send_b = refs[4 * n:4 * n + n_big]
        sa_send, sa_recv, sb_send, sb_recv, sc_send, sc_recv = refs[4 * n + n_big:]
        x, y, c = lax.axis_index("x"), lax.axis_index("y"), lax.axis_index("c")
        sibling = (x, y, 1 - c)
        me = 2 * x + y

        def rows(a, hc):
            return pl.ds(pl.multiple_of(hc * halves[a], F32_ROWS), halves[a])

        def partial(a, j, hc):
            return p_refs[a].at[j, rows(a, hc), :] if a < n_big else p_refs[a].at[rows(a, hc), :]

        def copy_a(a, j):
            return pltpu.make_async_remote_copy(
                src_ref=partial(a, j, 1 - c), dst_ref=rec_a[a].at[j],
                send_sem=sa_send.at[N_CHIPS * a + j], recv_sem=sa_recv.at[N_CHIPS * a + j],
                device_id=sibling, device_id_type=MESH)

        def copy_b(a, r):
            j = me ^ r
            k = (N_CHIPS - 1) * a + r - 1
            return pltpu.make_async_remote_copy(
                src_ref=(send_b[a] if a < n_big else rec_a[a]).at[j], dst_ref=rec_b[a].at[r],
                send_sem=sb_send.at[k], recv_sem=sb_recv.at[k], device_id=(j // 2, j % 2, c), device_id_type=MESH)

        def copy_c(a):
            return pltpu.make_async_remote_copy(
                src_ref=out_refs[a].at[rows(a, c), :], dst_ref=out_refs[a].at[rows(a, c), :],
                send_sem=sc_send.at[a], recv_sem=sc_recv.at[a], device_id=sibling, device_id_type=MESH)

        for a in range(n):
            for j in range(N_CHIPS):
                copy_a(a, j).start()
        for r in range(1, N_CHIPS):
            j = me ^ r
            for a in range(n):
                copy_a(a, j).wait_recv()
                chip_part = rec_a[a][j] + partial(a, j, c)[...]
                if a < n_big:
                    send_b[a][j] = chip_part.astype(BF16)
                else:
                    rec_a[a][j] = chip_part
                copy_b(a, r).start()
        for a in range(n):
            copy_a(a, me).wait_recv()
            rec_b[a][0] = (rec_a[a][me] + partial(a, me, c)[...]).astype(rec_b[a].dtype)
        for a in range(n):
            for r in range(1, N_CHIPS):
                copy_b(a, r).wait_recv()
            total = rec_b[a][me].astype(F32)
            for j in range(1, N_CHIPS):
                total = total + rec_b[a][j ^ me].astype(F32)
            out_refs[a][rows(a, c), :] = total
            copy_c(a).start()
        for a in range(n):
            copy_c(a).wait_recv()
        for a in range(n):
            for j in range(N_CHIPS):
                copy_a(a, j).wait_send()
            for r in range(1, N_CHIPS):
                copy_b(a, r).wait_send()
            copy_c(a).wait_send()

    dma = pltpu.SemaphoreType.DMA
    return _pallas(
        body, name="reduce_grads",
        out_shape=[jax.ShapeDtypeStruct(sh, F32) for sh in shapes],
        in_specs=[pl.BlockSpec(memory_space=pltpu.VMEM)] * n,
        out_specs=[pl.BlockSpec(memory_space=pltpu.VMEM)] * n,
        scratch_shapes=[pltpu.VMEM((N_CHIPS, h) + sh[1:], F32) for h, sh in zip(halves, shapes)]
                       + [pltpu.VMEM((N_CHIPS, h) + sh[1:], BF16 if a < n_big else F32)
                          for a, (h, sh) in enumerate(zip(halves, shapes))]
                       + [pltpu.VMEM((N_CHIPS, h) + sh[1:], BF16) for h, sh in zip(halves[:n_big], shapes[:n_big])]
                       + [dma((N_CHIPS * n,)), dma((N_CHIPS * n,)), dma(((N_CHIPS - 1) * n,)), dma(((N_CHIPS - 1) * n,)),
                          dma((n,)), dma((n,))],
        compiler_params=_params(vmem=VMEM_LIMIT),
    )(*parts, small)


def _pre(x, tabs, w_in_ext, w_uq_pad, w_ukv_ext, gains, tm):
    s_len = x.shape[0]
    nt = s_len // tm

    def body(x_ref, car_ref, cac_ref, sar_ref, sac_ref, cbr_ref, cbc_ref, sbr_ref, sbc_ref, win_ref, wuq_ref, wukv_ref,
             gin_ref, gaq_ref, gak_ref, gcq_ref, gckv_ref, gbq_ref, gbk_ref,
             xn_ref, gates_ref, pre_ref, qbpre_ref, kbpre_ref, cq_ref, ckv_ref,
             qa_ref, ka_ref, va_ref, qb_ref, kb_ref, vb_ref, proj):
        xn = _rms_fwd(x_ref[...], gin_ref[...], D_MODEL)
        xn_ref[...] = jnp.transpose(xn).astype(BF16)
        xb = xn.astype(BF16)
        pre_ref[:, 0:VA0] = _nn(xb, win_ref[:, 0:VA0])
        gates_ref[...] = _nn(xb, win_ref[:, GA0:GA0 + N_GATE])
        pre_ref[:, VA0:N_PRE] = _nn(xb, win_ref[:, CQ0:KR0])
        proj[...] = _nn(xb, win_ref[:, VA0:GA0])
        kr = _nn(xb, win_ref[:, KR0:N_EXT])
        ca, sa, cb, sb = _token_tables((car_ref, cac_ref, sar_ref, sac_ref, cbr_ref, cbc_ref, sbr_ref, sbc_ref))
        lane = lax.broadcasted_iota(jnp.int32, (tm, LANE), 1)
        for h in range(A_HEADS):
            yq = _rms_fwd(pre_ref[:, QA0 + LANE * h:QA0 + LANE * (h + 1)], gaq_ref[...], A_DIM)
            qa_ref[h] = (_rope_fwd(yq, ca, sa) * (SCALE_A * LOG2E)).astype(BF16)
        for h in range(A_KV):
            yk = _rms_fwd(pre_ref[:, KA0 + LANE * h:KA0 + LANE * (h + 1)], gak_ref[...], A_DIM)
            ka_ref[h] = _rope_fwd(yk, ca, sa).astype(BF16)
            va_ref[h] = jnp.where(lane == A_DIM, 1.0, proj[:, LANE * h:LANE * (h + 1)]).astype(BF16)
        cq = _rms_fwd(pre_ref[:, VA0:VA0 + B_Q_RANK], gcq_ref[...], B_Q_RANK)
        cq_ref[...] = jnp.transpose(cq).astype(BF16)
        qbpre_ref[...] = _nn(cq.astype(BF16), wuq_ref[...])
        ckv = _rms_fwd(pre_ref[:, VA0 + B_Q_RANK:N_PRE], gckv_ref[...], B_KV_RANK)
        ckv_ref[...] = jnp.transpose(ckv).astype(BF16)
        kvb = _nn(ckv.astype(BF16), wukv_ref[...])
        for h in range(B_HEADS):
            yq = _rms_fwd(qbpre_ref[:, LANE * h:LANE * (h + 1)], gbq_ref[...], B_QK)
            qb_ref[h] = (_rope_fwd(yq, cb, sb) * (SCALE_B * LOG2E)).astype(BF16)
            kp = kvb[:, LANE * h:LANE * (h + 1)] + kr
            kbpre_ref[:, LANE * h:LANE * (h + 1)] = kp
            kb_ref[h] = _rope_fwd(_rms_fwd(kp, gbk_ref[...], B_QK), cb, sb).astype(BF16)
            vb_ref[h, :, 0:LANE] = kvb[:, B_HEADS * LANE + LANE * h:B_HEADS * LANE + LANE * (h + 1)].astype(BF16)
            vb_ref[h, :, LANE:2 * LANE] = jnp.where(lane == 0, 1.0, 0.0).astype(BF16)

    row = lambda w: pl.BlockSpec((tm, w), lambda i: (i, 0))
    col = lambda w: pl.BlockSpec((w, tm), lambda i: (0, i))
    heads = lambda n: pl.BlockSpec((n, tm, LANE), lambda i: (0, i, 0))
    hs = lambda n: jax.ShapeDtypeStruct((n, s_len, LANE), BF16)
    return _pallas(
        body, name="pre", grid=(nt,),
        in_specs=[row(D_MODEL)] + _table_specs(tm)
                 + [_resident(w_in_ext.shape), _resident(w_uq_pad.shape), _resident(w_ukv_ext.shape)]
                 + [_full(g.shape) for g in gains],
        out_specs=[col(D_MODEL), row(N_GATE), row(N_PRE), row(B_HEADS * LANE), row(B_HEADS * LANE),
                   col(B_Q_RANK), col(B_KV_RANK),
                   heads(A_HEADS), heads(A_KV), heads(A_KV), heads(B_HEADS), heads(B_HEADS),
                   pl.BlockSpec((B_HEADS, tm, 2 * LANE), lambda i: (0, i, 0))],
        out_shape=[jax.ShapeDtypeStruct((D_MODEL, s_len), BF16), jax.ShapeDtypeStruct((s_len, N_GATE), F32),
                   jax.ShapeDtypeStruct((s_len, N_PRE), F32), jax.ShapeDtypeStruct((s_len, B_HEADS * LANE), F32),
                   jax.ShapeDtypeStruct((s_len, B_HEADS * LANE), F32),
                   jax.ShapeDtypeStruct((B_Q_RANK, s_len), BF16), jax.ShapeDtypeStruct((B_KV_RANK, s_len), BF16),
                   hs(A_HEADS), hs(A_KV), hs(A_KV), hs(B_HEADS), hs(B_HEADS),
                   jax.ShapeDtypeStruct((B_HEADS, s_len, 2 * LANE), BF16)],
        scratch_shapes=[pltpu.VMEM((tm, A_KV * LANE), F32)],
        compiler_params=_params(("parallel",), VMEM_LIMIT),
    )(x, *tabs, w_in_ext, w_uq_pad, w_ukv_ext, *gains)


def _attn_fwd(q, k, v, group, l_col, tq, tk, tiles, name):
    n_heads, s_len, _ = q.shape
    v_w = v.shape[2]
    nk = s_len // tk

    def body(q_ref, k_ref, v_ref, o_ref, lse_ref, s_buf, p_buf, a_buf, m_ref, acc_ref):
        def scores(g, slot):
            s_buf[slot] = _nt(q_ref[_block_rows(g // nk, tq), :], k_ref[_block_rows(g % nk, tk), :])

        def softmax(g, slot):
            t = g // nk
            s = s_buf[slot]
            m_old = m_ref[t]
            m_new = jnp.maximum(m_old, jnp.max(s, axis=-1, keepdims=True))
            m_ref[t] = m_new
            a_buf[slot] = jnp.exp2(m_old - m_new)
            p_buf[slot] = jnp.exp2(s - jnp.tile(m_new, (1, tk // LANE))).astype(BF16)

        def values(g, slot):
            t = g // nk
            pv = _nn(p_buf[slot], v_ref[_block_rows(g % nk, tk), :])
            for c in range(0, v_w, LANE):
                acc_ref[t, :, c:c + LANE] = a_buf[slot] * acc_ref[t, :, c:c + LANE] + pv[:, c:c + LANE]

        m_ref[...] = jnp.full(m_ref.shape, -1e30, F32)
        acc_ref[...] = jnp.zeros(acc_ref.shape, F32)
        _three_stage(tiles * nk, scores, softmax, values)
        for t in range(tiles):
            l = acc_ref[t, :, l_col:l_col + 1]
            o = acc_ref[t, :, 0:LANE] * (1.0 / l)
            if l_col < LANE:
                lane = lax.broadcasted_iota(jnp.int32, o.shape, 1)
                o = jnp.where(lane == l_col, 0.0, o)
            o_ref[t * tq:(t + 1) * tq, :] = o
            lse_ref[t] = jnp.transpose(m_ref[t] + jnp.log2(jnp.broadcast_to(l, (tq, LANE))))[0:1, :]

    return _pallas(
        body, name=name, grid=(n_heads, s_len // (tiles * tq)),
        in_specs=[pl.BlockSpec((None, tiles * tq, LANE), lambda h, i: (h, i, 0)),
                  pl.BlockSpec((None, s_len, LANE), lambda h, i: (h // group, 0, 0)),
                  pl.BlockSpec((None, s_len, v_w), lambda h, i: (h // group, 0, 0))],
        out_specs=[pl.BlockSpec((None, tiles * tq, LANE), lambda h, i: (h, i, 0)),
                   pl.BlockSpec((None, tiles, 1, tq), lambda h, i: (h, i, 0, 0))],
        out_shape=[jax.ShapeDtypeStruct((n_heads, s_len, LANE), F32),
                   jax.ShapeDtypeStruct((n_heads, s_len // tq, 1, tq), F32)],
        scratch_shapes=[pltpu.VMEM((2, tq, tk), F32), pltpu.VMEM((2, tq, tk), BF16), pltpu.VMEM((2, tq, LANE), F32),
                        pltpu.VMEM((tiles, tq, LANE), F32), pltpu.VMEM((tiles, tq, v_w), F32)],
        compiler_params=_params(("parallel", "parallel"), VMEM_MID),
    )(q, k, v)


def _mid(x, target, o_a, o_b, gates, w_out, tm):
    s_len = x.shape[0]
    nt = s_len // tm
    n_heads = A_HEADS + B_HEADS
    d_mix = w_out.shape[0]
    pairs = A_HEADS // 2

    def body(x_ref, t_ref, oa_ref, ob_ref, g_ref, w_ref,
             yt_ref, dh_ref, dgate_ref, doa_ref, dob_ref, delta_ref, loss_ref, silu_scr, dsilu_scr, y_ref):
        @pl.when(pl.program_id(0) == 0)
        def _():
            loss_ref[...] = jnp.zeros_like(loss_ref)

        def o_of(h):
            return oa_ref[h] if h < A_HEADS else ob_ref[h - A_HEADS]

        def gated(h):
            cols = slice(LANE * h, LANE * (h + 1))
            g = g_ref[:, cols]
            sig = 1.0 / (1.0 + jnp.exp(-g))
            silu = g * sig
            silu_scr[:, cols] = silu
            dsilu_scr[:, cols] = sig * (1.0 + g * (1.0 - sig))
            return o_of(h) * silu

        for c in range(pairs + B_HEADS):
            y = gated(2 * c) + pltpu.roll(gated(2 * c + 1), ROT, 1) if c < pairs else gated(A_HEADS + c - pairs)
            cols = slice(LANE * c, LANE * (c + 1))
            y_ref[:, cols] = y.astype(BF16)
            yt_ref[cols, :] = jnp.transpose(y).astype(BF16)
        err = x_ref[...] + _nn(y_ref[...], w_ref[...]) - t_ref[...]
        sq = jnp.sum(jnp.sum(err * err, axis=-1, keepdims=True), axis=0, keepdims=True)
        loss_ref[...] += jnp.broadcast_to(sq * (0.5 / D_MODEL), loss_ref.shape)
        dh = err * (1.0 / D_MODEL)
        dh_ref[...] = dh
        dy = _nt(dh.astype(BF16), w_ref[...])
        lane = lax.broadcasted_iota(jnp.int32, (tm, LANE), 1)
        delta = jnp.zeros((tm, LANE), F32)
        for h in range(n_heads):
            cols = slice(LANE * h, LANE * (h + 1))
            if h < A_HEADS:
                packed = dy[:, LANE * (h // 2):LANE * (h // 2 + 1)]
                dyh = jnp.where(lane < A_DIM, packed if h % 2 == 0 else pltpu.roll(packed, ROT, 1), 0.0)
            else:
                dyh = dy[:, LANE * (pairs + h - A_HEADS):LANE * (pairs + h - A_HEADS + 1)]
            oh = o_of(h)
            do = dyh * silu_scr[:, cols]
            dgate_ref[:, cols] = (dyh * oh * dsilu_scr[:, cols]).astype(BF16)
            delta = jnp.where(lane == h, jnp.sum(do * oh, axis=-1, keepdims=True), delta)
            if h < A_HEADS:
                doa_ref[h] = do.astype(BF16)
            else:
                dob_ref[h - A_HEADS] = do.astype(BF16)
        delta_ref[...] = jnp.transpose(delta)[0:DELTA_ROWS, :]

    row = lambda w: pl.BlockSpec((tm, w), lambda i: (i, 0))
    heads = lambda n, w=LANE: pl.BlockSpec((n, tm, w), lambda i: (0, i, 0))
    return _pallas(
        body, name="mid", grid=(nt,),
        in_specs=[row(D_MODEL), row(D_MODEL), heads(A_HEADS), heads(B_HEADS), row(N_GATE), _resident(w_out.shape)],
        out_specs=[pl.BlockSpec((d_mix, tm), lambda i: (0, i)), row(D_MODEL), row(N_GATE), heads(A_HEADS), heads(B_HEADS),
                   pl.BlockSpec((DELTA_ROWS, tm), lambda i: (0, i)),
                   _full((8, LANE))],
        out_shape=[jax.ShapeDtypeStruct((d_mix, s_len), BF16), jax.ShapeDtypeStruct((s_len, D_MODEL), F32),
                   jax.ShapeDtypeStruct((s_len, N_GATE), BF16),
                   jax.ShapeDtypeStruct((A_HEADS, s_len, LANE), BF16), jax.ShapeDtypeStruct((B_HEADS, s_len, LANE), BF16),
                   jax.ShapeDtypeStruct((DELTA_ROWS, s_len), F32), jax.ShapeDtypeStruct((8, LANE), F32)],
        scratch_shapes=[pltpu.VMEM((tm, N_GATE), F32), pltpu.VMEM((tm, N_GATE), F32), pltpu.VMEM((tm, d_mix), BF16)],
        compiler_params=_params(("arbitrary",), VMEM_LIMIT),
    )(x, target, o_a, o_b, gates, w_out)


def _attn_bwd(q, k, v, do, lse, delta, group, tq, tk, tiles, name):
    n_heads, s_len, _ = q.shape
    nq = s_len // tq

    def body(q_ref, do_ref, lse_ref, delta_ref, k_ref, v_ref, dq_ref, dk_ref, dv_ref, s_buf, dp_buf, p_buf, ds_buf):
        @pl.when(pl.program_id(1) == 0)
        def _():
            dq_ref[...] = jnp.zeros_like(dq_ref)

        dk_ref[...] = jnp.zeros_like(dk_ref)
        dv_ref[...] = jnp.zeros_like(dv_ref)

        def keys(g):
            return _block_rows(g // nq, tk)

        def queries(g):
            return _block_rows(g % nq, tq)

        def scores(g, slot):
            s_buf[slot] = _nt(k_ref[keys(g), :], q_ref[queries(g), :])
            dp_buf[slot] = _nt(v_ref[keys(g), :], do_ref[queries(g), :])

        def elementwise(g, slot):
            p = jnp.exp2(s_buf[slot] - lse_ref[g % nq])
            p_buf[slot] = p.astype(BF16)
            ds_buf[slot] = (p * (dp_buf[slot] - delta_ref[g % nq])).astype(BF16)

        def grads(g, slot):
            dv_ref[keys(g), :] += _nn(p_buf[slot], do_ref[queries(g), :])
            dk_ref[keys(g), :] += _nn(ds_buf[slot], q_ref[queries(g), :])
            dq_ref[queries(g), :] += _tn(ds_buf[slot], k_ref[keys(g), :])

        _three_stage(tiles * nq, scores, elementwise, grads)

    whole = lambda: pl.BlockSpec((None, s_len, LANE), lambda h, j: (h, 0, 0))
    stat = lambda: pl.BlockSpec((None, nq, 1, tq), lambda h, j: (h, 0, 0, 0))
    kvb = lambda: pl.BlockSpec((None, tiles * tk, LANE), lambda h, j: (h // group, j, 0))
    outb = lambda: pl.BlockSpec((None, tiles * tk, LANE), lambda h, j: (h, j, 0))
    shape = jax.ShapeDtypeStruct((n_heads, s_len, LANE), F32)
    return _pallas(
        body, name=name, grid=(n_heads, s_len // (tiles * tk)),
        in_specs=[whole(), whole(), stat(), stat(), kvb(), kvb()],
        out_specs=[whole(), outb(), outb()],
        out_shape=[shape, shape, shape],
        scratch_shapes=[pltpu.VMEM((2, tk, tq), F32), pltpu.VMEM((2, tk, tq), F32),
                        pltpu.VMEM((2, tk, tq), BF16), pltpu.VMEM((2, tk, tq), BF16)],
        compiler_params=_params(("parallel", "arbitrary"), VMEM_MID),
    )(q, do, lse, delta, k, v)


def _post(x, dh, pre, qbpre, kbpre, dgate, dqa, dka, dva, dqb, dkb, dvb, loss_part, tabs,
          w_in_ext, w_uq_pad, w_ukv_ext, gains, tm):
    s_len = x.shape[0]
    nt = s_len // tm

    def body(x_ref, dh_ref, pre_ref, qbpre_ref, kbpre_ref, dgate_ref,
             dqa_ref, dka_ref, dva_ref, dqb_ref, dkb_ref, dvb_ref, loss_ref,
             car_ref, cac_ref, sar_ref, sac_ref, cbr_ref, cbc_ref, sbr_ref, sbc_ref, win_ref, wuq_ref, wukv_ref,
             gin_ref, gaq_ref, gak_ref, gcq_ref, gckv_ref, gbq_ref, gbk_ref,
             gx_ref, dproj_ref, dqbpre_ref, dkvb_ref, dsm_ref):
        @pl.when(pl.program_id(0) == 0)
        def _():
            dsm_ref[...] = jnp.zeros_like(dsm_ref)
            dsm_ref[SM_LOSS:SM_LOSS + 1, 0:LANE] = loss_ref[0:1, :]

        def add_small(r, dg):
            dsm_ref[r:r + 1, 0:dg.shape[1]] += dg

        def tok_sum(a):
            return jnp.sum(a, axis=0, keepdims=True)

        ca, sa, cb, sb = _token_tables((car_ref, cac_ref, sar_ref, sac_ref, cbr_ref, cbc_ref, sbr_ref, sbc_ref))
        lane = lax.broadcasted_iota(jnp.int32, (tm, LANE), 1)

        nope_lanes = _lanes_of(lane, LAY_NOPE)

        def back(c0, c1):
            return _nt(dproj_ref[:, c0:c1], win_ref[:, c0:c1])

        dproj_ref[:, GA0:GA0 + N_GATE] = dgate_ref[...]
        dxn = back(GA0, GA0 + N_GATE)
        dg = jnp.zeros((1, LANE), F32)
        for h in range(A_HEADS):
            dn = _rope_bwd(dqa_ref[h] * SCALE_A, ca, sa)
            dx, dgr = _rms_bwd(dn, pre_ref[:, QA0 + LANE * h:QA0 + LANE * (h + 1)], gaq_ref[...], A_DIM)
            dproj_ref[:, QA0 + LANE * h:QA0 + LANE * (h + 1)] = dx.astype(BF16)
            dg = dg + tok_sum(dgr)
        add_small(SM_AQ, _unspread_row(dg, LAY_ROPE_A))
        dxn = dxn + back(QA0, KA0)
        dg = jnp.zeros((1, LANE), F32)
        for h in range(A_KV):
            dk = dka_ref[A_GROUP * h]
            dv = dva_ref[A_GROUP * h]
            for g in range(1, A_GROUP):
                dk = dk + dka_ref[A_GROUP * h + g]
                dv = dv + dva_ref[A_GROUP * h + g]
            dn = _rope_bwd(dk * LN2, ca, sa)
            dx, dgr = _rms_bwd(dn, pre_ref[:, KA0 + LANE * h:KA0 + LANE * (h + 1)], gak_ref[...], A_DIM)
            dproj_ref[:, KA0 + LANE * h:KA0 + LANE * (h + 1)] = dx.astype(BF16)
            dproj_ref[:, VA0 + LANE * h:VA0 + LANE * (h + 1)] = dv.astype(BF16)
            dg = dg + tok_sum(dgr)
        add_small(SM_AK, _unspread_row(dg, LAY_ROPE_A))
        dxn = dxn + back(KA0, GA0)
        dg = jnp.zeros((1, LANE), F32)
        for h in range(B_HEADS):
            cols = slice(LANE * h, LANE * (h + 1))
            dn = _rope_bwd(dqb_ref[h] * SCALE_B, cb, sb)
            dx, dgr = _rms_bwd(dn, qbpre_ref[:, cols], gbq_ref[...], B_QK)
            dqbpre_ref[:, cols] = dx.astype(BF16)
            dg = dg + tok_sum(dgr)
        add_small(SM_BQ, _unspread_row(dg, LAY_ROPE_B))
        dcq = _nt(dqbpre_ref[...], wuq_ref[...])
        dx, dgr = _rms_bwd(dcq, pre_ref[:, VA0:VA0 + B_Q_RANK], gcq_ref[...], B_Q_RANK)
        dproj_ref[:, CQ0:CQ0 + B_Q_RANK] = dx.astype(BF16)
        add_small(SM_CQ, tok_sum(dgr))
        dxn = dxn + back(CQ0, CKV0)
        dg = jnp.zeros((1, LANE), F32)
        dkr = jnp.zeros((tm, LANE), F32)
        for h in range(B_HEADS):
            cols = slice(LANE * h, LANE * (h + 1))
            dn = _rope_bwd(dkb_ref[h] * LN2, cb, sb)
            dx, dgr = _rms_bwd(dn, kbpre_ref[:, cols], gbk_ref[...], B_QK)
            dkvb_ref[:, cols] = jnp.where(nope_lanes, dx, 0.0).astype(BF16)
            dkvb_ref[:, B_HEADS * LANE + LANE * h:B_HEADS * LANE + LANE * (h + 1)] = dvb_ref[h].astype(BF16)
            dkr = dkr + dx
            dg = dg + tok_sum(dgr)
        add_small(SM_BK, _unspread_row(dg, LAY_ROPE_B))
        dproj_ref[:, KR0:KR0 + LANE] = jnp.where(_lanes_of(lane, LAY_KR), dkr, 0.0).astype(BF16)
        dckv = _nt(dkvb_ref[...], wukv_ref[...])
        dx, dgr = _rms_bwd(dckv, pre_ref[:, VA0 + B_Q_RANK:N_PRE], gckv_ref[...], B_KV_RANK)
        dproj_ref[:, CKV0:CKV0 + B_KV_RANK] = dx.astype(BF16)
        add_small(SM_CKV, tok_sum(dgr))
        dxn = dxn + back(CKV0, N_EXT)
        dx, dgr = _rms_bwd(dxn, x_ref[...], gin_ref[...], D_MODEL)
        gx_ref[...] = dh_ref[...] + dx
        add_small(SM_IN, tok_sum(dgr))

    row = lambda w: pl.BlockSpec((tm, w), lambda i: (i, 0))
    heads = lambda n: pl.BlockSpec((n, tm, LANE), lambda i: (0, i, 0))
    return _pallas(
        body, name="post", grid=(nt,),
        in_specs=[row(D_MODEL), row(D_MODEL), row(N_PRE), row(B_HEADS * LANE), row(B_HEADS * LANE), row(N_GATE),
                  heads(A_HEADS), heads(A_HEADS), heads(A_HEADS), heads(B_HEADS), heads(B_HEADS), heads(B_HEADS),
                  _full(loss_part.shape)] + _table_specs(tm)
                 + [_resident(w_in_ext.shape), _resident(w_uq_pad.shape), _resident(w_ukv_ext.shape)]
                 + [_full(g.shape) for g in gains],
        out_specs=[row(D_MODEL), row(N_EXT), row(B_HEADS * LANE), row(2 * B_HEADS * LANE), _full((SM_ROWS, SM_W))],
        out_shape=[jax.ShapeDtypeStruct((s_len, D_MODEL), F32), jax.ShapeDtypeStruct((s_len, N_EXT), BF16),
                   jax.ShapeDtypeStruct((s_len, B_HEADS * LANE), BF16),
                   jax.ShapeDtypeStruct((s_len, 2 * B_HEADS * LANE), BF16),
                   jax.ShapeDtypeStruct((SM_ROWS, SM_W), F32)],
        compiler_params=_params(("arbitrary",), VMEM_LIMIT),
    )(x, dh, pre, qbpre, kbpre, dgate, dqa, dka, dva, dqb, dkb, dvb, loss_part, *tabs,
      w_in_ext, w_uq_pad, w_ukv_ext, *gains)


def _grad_w(a_t, b, tn, ts, name):
    m, s_len = a_t.shape
    n = b.shape[1]

    def body(a_ref, b_ref, o_ref):
        @pl.when(pl.program_id(1) == 0)
        def _():
            o_ref[...] = jnp.zeros_like(o_ref)

        o_ref[...] += _nn(a_ref[...], b_ref[...].astype(BF16))

    return _pallas(
        body, name=name, grid=(n // tn, s_len // ts),
        in_specs=[pl.BlockSpec((m, ts), lambda j, t: (0, t)), pl.BlockSpec((ts, tn), lambda j, t: (t, j))],
        out_specs=pl.BlockSpec((m, tn), lambda j, t: (0, j)),
        out_shape=jax.ShapeDtypeStruct((m, n), F32),
        compiler_params=_params(("parallel", "arbitrary"), VMEM_MID),
    )(a_t, b)


def _grad_w_pairs(pairs, ts, name):
    s_len = pairs[0][0].shape[1]
    n_p = len(pairs)

    def body(*refs):
        for a_ref, b_ref, o_ref in zip(refs[0:2 * n_p:2], refs[1:2 * n_p:2], refs[2 * n_p:]):
            @pl.when(pl.program_id(0) == 0)
            def _():
                o_ref[...] = jnp.zeros_like(o_ref)

            o_ref[...] += _nn(a_ref[...], b_ref[...].astype(BF16))

    in_specs, flat = [], []
    for a_t, b in pairs:
        in_specs += [pl.BlockSpec((a_t.shape[0], ts), lambda t: (0, t)), pl.BlockSpec((ts, b.shape[1]), lambda t: (t, 0))]
        flat += [a_t, b]
    return _pallas(
        body, name=name, grid=(s_len // ts,),
        in_specs=in_specs,
        out_specs=[_full((a_t.shape[0], b.shape[1])) for a_t, b in pairs],
        out_shape=[jax.ShapeDtypeStruct((a_t.shape[0], b.shape[1]), F32) for a_t, b in pairs],
        compiler_params=_params(("arbitrary",), VMEM_MID),
    )(*flat)


def _adam_math(w, g, m, v):
    nm = ADAM_B1 * m + (1.0 - ADAM_B1) * g
    nv = ADAM_B2 * v + (1.0 - ADAM_B2) * (g * g)
    m_hat = nm / (1.0 - ADAM_B1 ** ADAM_STEP)
    v_hat = nv / (1.0 - ADAM_B2 ** ADAM_STEP)
    return -ADAM_LR * (m_hat / (jnp.sqrt(v_hat) + ADAM_EPS) + ADAM_WD * w), nm, nv


def _adamw_rows(w, g, m, v, tr):
    rows, cols = w.shape

    def body(w_ref, g_ref, m_ref, v_ref, d_ref, nm_ref, nv_ref):
        d_ref[...], nm_ref[...], nv_ref[...] = _adam_math(w_ref[...], g_ref[...], m_ref[...], v_ref[...])

    blk = pl.BlockSpec((tr, cols), lambda i: (i, 0))
    shape = jax.ShapeDtypeStruct((rows, cols), F32)
    return _pallas(
        body, name="adamw_w_in", grid=(rows // tr,),
        in_specs=[blk] * 4, out_specs=[blk] * 3, out_shape=[shape] * 3,
        compiler_params=_params(("parallel",), VMEM_SMALL),
    )(w, g, m, v)


def _adamw_rest(bigs, smalls, g_small):
    nb, ns = len(bigs), len(smalls)

    def body(*refs):
        ins, outs = refs[:4 * nb + 3 * ns + 1], refs[4 * nb + 3 * ns + 1:]
        for i in range(nb):
            w_ref, g_ref, m_ref, v_ref = ins[4 * i:4 * i + 4]
            d_ref, nm_ref, nv_ref = outs[3 * i:3 * i + 3]
            d_ref[...], nm_ref[...], nv_ref[...] = _adam_math(w_ref[...], g_ref[...], m_ref[...], v_ref[...])
        gs_ref = ins[-1]
        for i in range(ns):
            w_ref, m_ref, v_ref = ins[4 * nb + 3 * i:4 * nb + 3 * i + 3]
            g_ref, d_ref, nm_ref, nv_ref = outs[3 * nb + 4 * i:3 * nb + 4 * i + 4]
            g = gs_ref[i:i + 1, 0:w_ref.shape[1]]
            g_ref[...] = g
            d_ref[...], nm_ref[...], nv_ref[...] = _adam_math(w_ref[...], g, m_ref[...], v_ref[...])

    flat_in = [a for quad in bigs for a in quad] + [a for tri in smalls for a in tri] + [g_small]
    out_shape = ([jax.ShapeDtypeStruct(q[0].shape, F32) for q in bigs for _ in range(3)]
                 + [jax.ShapeDtypeStruct(t[0].shape, F32) for t in smalls for _ in range(4)])
    return _pallas(
        body, name="adamw_rest",
        in_specs=[pl.BlockSpec(memory_space=pltpu.VMEM)] * len(flat_in),
        out_specs=[pl.BlockSpec(memory_space=pltpu.VMEM)] * len(out_shape),
        out_shape=out_shape,
        compiler_params=_params(vmem=VMEM_SMALL),
    )(*flat_in)


def _place(pieces, n):
    out, at = [], 0
    for lane0, arr in sorted(pieces, key=lambda p: p[0]):
        out += [jnp.zeros((n, lane0 - at), F32), arr]
        at = lane0 + arr.shape[1]
    return jnp.concatenate(out + [jnp.zeros((n, LANE - at), F32)], axis=1)


def _rope_tables(s_len, tm):
    rows = s_len // GRID_W
    row = jnp.arange(rows, dtype=F32)
    col = jnp.arange(GRID_W, dtype=F32)

    def lay(dim, layout, first_dim, ones):
        half = dim // 2
        inv = 1.0 / (ROPE_THETA ** (jnp.arange(0, half, 2, dtype=F32) / half))
        ang_r, ang_c = row[:, None] * inv[None, :], col[:, None] * inv[None, :]
        at = {a - first_dim: lane0 for a, _, lane0 in layout}
        q = dim // 4
        r1, r2, c1, c2 = at[0], at[q], at[2 * q], at[3 * q]
        cos_r = _place([(r1, jnp.cos(ang_r)), (r2, jnp.cos(ang_r))], rows)
        sin_r = _place([(r1, -jnp.sin(ang_r)), (r2, jnp.sin(ang_r))], rows)
        cos_c = _place([(c1, jnp.cos(ang_c)), (c2, jnp.cos(ang_c))] + [(l0, jnp.ones((GRID_W, n), F32)) for _, n, l0 in ones],
                       GRID_W)
        sin_c = _place([(c1, -jnp.sin(ang_c)), (c2, jnp.sin(ang_c))], GRID_W)
        by_block = (s_len // tm, tm // GRID_W, LANE)
        return cos_r.reshape(by_block), cos_c, sin_r.reshape(by_block), sin_c

    return lay(A_DIM, LAY_ROPE_A, 0, ()) + lay(B_ROPE, LAY_KR, 0, LAY_NOPE)


def _spread(w, n_heads, dim, axis, layout):
    w3 = w.reshape(w.shape[:axis] + (n_heads, dim) + w.shape[axis + 1:])
    out, at = [], 0

    def zeros(n):
        return jnp.zeros(w3.shape[:axis + 1] + (n,) + w3.shape[axis + 2:], w.dtype)

    for a0, n, lane0 in sorted(layout, key=lambda seg: seg[2]):
        out += [zeros(lane0 - at), lax.slice_in_dim(w3, a0, a0 + n, axis=axis + 1)]
        at = lane0 + n
    out = jnp.concatenate(out + [zeros(LANE - at)], axis=axis + 1)
    return out.reshape(w.shape[:axis] + (n_heads * LANE,) + w.shape[axis + 1:])


def _head_cols(first, n_heads, dim, layout):
    out = np.full((n_heads * LANE,), -1, np.int32)
    for h in range(n_heads):
        for a0, n, lane0 in layout:
            out[h * LANE + lane0:h * LANE + lane0 + n] = first + h * dim + a0 + np.arange(n)
    return out


def _inverse(src, n):
    dst = np.full((n,), -1, np.int32)
    dst[src[src >= 0]] = np.nonzero(src >= 0)[0]
    return dst


def _column_maps():
    a_w, kv_w = A_HEADS * A_DIM, A_KV * A_DIM
    o_g = a_w + 2 * kv_w
    o_cq = o_g + a_w
    o_kr = o_cq + B_Q_RANK + B_KV_RANK
    src_in = np.concatenate([
        _head_cols(0, A_HEADS, A_DIM, LAY_ROPE_A), _head_cols(a_w, A_KV, A_DIM, LAY_ROPE_A),
        _head_cols(a_w + kv_w, A_KV, A_DIM, LAY_PLAIN_A), _head_cols(o_g, A_HEADS, A_DIM, LAY_PLAIN_A),
        np.arange(o_kr + B_ROPE, N_IN), np.arange(o_cq, o_kr), _head_cols(o_kr, 1, B_ROPE, LAY_KR)]).astype(np.int32)
    src_uq = _head_cols(0, B_HEADS, B_QK, LAY_ROPE_B)
    per = B_NOPE + B_V
    src_ukv = np.concatenate([_head_cols(0, B_HEADS, per, LAY_NOPE),
                              _head_cols(B_NOPE, B_HEADS, per, ((0, B_V, 0),))]).astype(np.int32)
    assert len(src_in) == N_EXT
    return src_in, src_uq, src_ukv


def _round_up(n, m):
    return (n + m - 1) // m * m


def _permute_cols(xs, maps, stacks, name):
    maps = [np.asarray(m, np.int32) for m in maps]
    n_arr = len(xs)

    def block(ref, b):
        if len(ref.shape) == 2:
            return ref.at[:, b * LANE:(b + 1) * LANE]
        per = ref.shape[2] // LANE
        return ref.at[b // per, :, (b % per) * LANE:(b % per + 1) * LANE]

    def body(*refs):
        row = lax.broadcasted_iota(jnp.int32, (LANE, LANE), 0)
        for x_ref, src_ref, o_ref, src in zip(refs[:n_arr], refs[n_arr:2 * n_arr], refs[2 * n_arr:], maps):
            for c in range(len(src) // LANE):
                want = src[c * LANE:(c + 1) * LANE]
                if want[0] >= 0 and want[0] % LANE == 0 and np.array_equal(want, want[0] + np.arange(LANE)):
                    block(o_ref, c)[...] = block(x_ref, int(want[0]) // LANE)[...]
                    continue
                acc = jnp.zeros((x_ref.shape[-2], LANE), F32)
                for kb in sorted({int(v) // LANE for v in want if v >= 0}):
                    sel = jnp.where(row + kb * LANE == src_ref[:, c * LANE:(c + 1) * LANE], 1.0, 0.0).astype(BF16)
                    part = block(x_ref, kb)[...]
                    if part.dtype == BF16:
                        acc = acc + _nn(part, sel)
                    else:
                        hi = part.astype(BF16)
                        rest = part - hi.astype(F32)
                        mid = rest.astype(BF16)
                        low = (rest - mid.astype(F32)).astype(BF16)
                        acc = acc + ((_nn(hi, sel) + _nn(mid, sel)) + _nn(low, sel))
                block(o_ref, c)[...] = acc.astype(o_ref.dtype)

    def out_shape(x, m, stack):
        rows = x.shape[-2]
        return (rows, len(m)) if stack is None else (stack, rows, len(m) // stack)

    return _pallas(
        body, name=name,
        out_shape=[jax.ShapeDtypeStruct(out_shape(x, m, st), x.dtype) for x, m, st in zip(xs, maps, stacks)],
        compiler_params=_params(vmem=VMEM_MID),
    )(*xs, *[jnp.asarray(m).reshape(1, -1) for m in maps])


def _pad_cols(w):
    return jnp.pad(w, ((0, 0), (0, _round_up(w.shape[1], LANE) - w.shape[1])))


def _in_stack(cols, width):
    cols = np.asarray(cols)
    return np.where(cols < 0, -1, cols // width * _round_up(width, LANE) + cols % width).astype(np.int32)


def _ext_weights(g_in, g_uq, g_ukv, g_out):
    src_in, src_uq, src_ukv = _column_maps()
    w_uq = g_uq.reshape(B_Q_RANK, B_HEADS * B_QK)
    w_out = g_out.reshape(D_MODEL, D_MODEL)
    w_in_ext, w_uq_pad, w_ukv_ext = _permute_cols(
        [g_in, w_uq, g_ukv], [_in_stack(src_in, SH_IN[1]), src_uq, _in_stack(src_ukv, SH_UKV[1])], [None] * 3, "lay_out_weights")
    return w_in_ext, w_uq_pad, w_ukv_ext, w_out


def _fold_grads(d_in_ext, d_uq_pad, d_ukv_ext, d_out):
    src_in, src_uq, src_ukv = _column_maps()

    def back(src, n, width):
        inv = _inverse(src, n)
        wide = _round_up(width, LANE)
        out = np.full((n // width * wide,), -1, np.int32)
        for j in range(n // width):
            out[j * wide:j * wide + width] = inv[j * width:(j + 1) * width]
        return out

    n_uq, n_ukv = B_HEADS * B_QK, B_HEADS * (B_NOPE + B_V)
    d_in, d_uq, d_ukv = _permute_cols(
        [d_in_ext, d_uq_pad, d_ukv_ext], [back(src_in, N_IN, SH_IN[1]), _inverse(src_uq, n_uq), back(src_ukv, n_ukv, SH_UKV[1])],
        [N_CHIPS, None, N_CHIPS], "fold_grads")
    return d_in, d_uq.reshape((N_CHIPS,) + SH_UQ), d_ukv, d_out.reshape((N_CHIPS,) + SH_OUT)


def kernel(x, norm_in, w_in, a_q_norm, a_k_norm, b_cq_norm, b_ckv_norm, w_uq, w_ukv, b_q_norm, b_k_norm, w_out, loss_target, m_norm_in, m_w_in, m_a_q_norm, m_a_k_norm, m_b_cq_norm, m_b_ckv_norm, m_w_uq, m_w_ukv, m_b_q_norm, m_b_k_norm, m_w_out, v_norm_in, v_w_in, v_a_q_norm, v_a_k_norm, v_b_cq_norm, v_b_ckv_norm, v_w_uq, v_w_ukv, v_b_q_norm, v_b_k_norm, v_w_out):
    s_len = x.shape[1]
    xs, ts = x[0], loss_target[0]
    tm = min(256, s_len)
    tq, tk_f = min(512, s_len // 2), min(2048, s_len // 2)
    tq_b, tk_b = min(1024, s_len // 2), min(512, s_len)
    tiles_f = min(4, s_len // tq)
    tiles_b = min(2, s_len // tk_b)

    w_in_ext, w_uq_pad, w_ukv_ext, w_out_full = _ext_weights(
        *_gather_weights((_pad_cols(w_in[0]), w_uq[0], _pad_cols(w_ukv[0]), w_out[0])))
    gains = (norm_in, _spread(a_q_norm, 1, A_DIM, 1, LAY_ROPE_A), _spread(a_k_norm, 1, A_DIM, 1, LAY_ROPE_A), b_cq_norm, b_ckv_norm,
             _spread(b_q_norm, 1, B_QK, 1, LAY_ROPE_B), _spread(b_k_norm, 1, B_QK, 1, LAY_ROPE_B))
    tabs = _rope_tables(s_len, tm)

    (xn_t, gates, pre, qbpre, kbpre, cq_t, ckv_t, qa, ka, va, qb, kb, vb) = _pre(
        xs, tabs, w_in_ext, w_uq_pad, w_ukv_ext, gains, tm)
    o_a, lse_a = _attn_fwd(qa, ka, va, A_GROUP, A_DIM, tq, tk_f, tiles_f, "attn_fwd_a")
    o_b, lse_b = _attn_fwd(qb, kb, vb, 1, B_V, tq, tk_f, tiles_f, "attn_fwd_b")
    y_t, dh, dgate, do_a, do_b, delta, loss_part = _mid(xs, ts, o_a, o_b, gates, w_out_full, min(512, s_len))

    def stat(a):
        return a.reshape(a.shape[0], s_len // tq_b, 1, tq_b)

    dqa, dka, dva = _attn_bwd(qa, ka, va, do_a, stat(lse_a), stat(delta[:A_HEADS]), A_GROUP, tq_b, tk_b, tiles_b, "attn_bwd_a")
    dqb, dkb, dvb = _attn_bwd(qb, kb, vb, do_b, stat(lse_b), stat(delta[A_HEADS:A_HEADS + B_HEADS]), 1, tq_b, tk_b,
                              tiles_b, "attn_bwd_b")
    grad_x, dproj, dqbpre, dkvb, d_small = _post(
        xs, dh, pre, qbpre, kbpre, dgate, dqa, dka, dva, dqb, dkb, dvb, loss_part, tabs,
        w_in_ext, w_uq_pad, w_ukv_ext, gains, tm)

    ts_w = min(2048, s_len)
    d_in_ext = _grad_w(xn_t, dproj, 768, min(2 * ts_w, s_len), "grad_w_in")
    d_out_full = _grad_w(y_t, dh, 512, ts_w, "grad_w_out")
    d_uq_pad, d_ukv_ext = _grad_w_pairs([(cq_t, dqbpre), (ckv_t, dkvb)], ts_w, "grad_w_mla")

    g_in_p, g_uq, g_ukv_p, g_out, g_small = _reduce_grads(_fold_grads(d_in_ext, d_uq_pad, d_ukv_ext, d_out_full), d_small)
    g_in, g_ukv = g_in_p[:, :SH_IN[1]], g_ukv_p[:, :SH_UKV[1]]
    d_in, nm_in, nv_in = (a.T for a in _adamw_rows(w_in[0].T, g_in_p.T[:SH_IN[1]], m_w_in[0].T, v_w_in[0].T, SH_IN[1] // 7))
    rest = _adamw_rest(
        [(w_uq[0], g_uq, m_w_uq[0], v_w_uq[0]), (w_ukv[0], g_ukv, m_w_ukv[0], v_w_ukv[0]),
         (w_out[0], g_out, m_w_out[0], v_w_out[0])],
        [(norm_in, m_norm_in, v_norm_in), (a_q_norm, m_a_q_norm, v_a_q_norm), (a_k_norm, m_a_k_norm, v_a_k_norm),
         (b_cq_norm, m_b_cq_norm, v_b_cq_norm), (b_ckv_norm, m_b_ckv_norm, v_b_ckv_norm),
         (b_q_norm, m_b_q_norm, v_b_q_norm), (b_k_norm, m_b_k_norm, v_b_k_norm)], g_small)
    (d_uq, nm_uq, nv_uq), (d_ukv, nm_ukv, nv_ukv), (d_out, nm_out, nv_out) = (rest[3 * i:3 * i + 3] for i in range(3))
    sm = [rest[9 + 4 * i:9 + 4 * i + 4] for i in range(7)]

    def leaves(k, p_in, p_uq, p_ukv, p_out):
        return [sm[SM_IN][k], p_in[None], sm[SM_AQ][k], sm[SM_AK][k], sm[SM_CQ][k], sm[SM_CKV][k], p_uq[None], p_ukv[None],
                sm[SM_BQ][k], sm[SM_BK][k], p_out[None]]

    return (g_small[SM_LOSS, 0], grad_x[None], *leaves(0, g_in, g_uq, g_ukv, g_out), *leaves(1, d_in, d_uq, d_ukv, d_out),
            *leaves(2, nm_in, nm_uq, nm_ukv, nm_out), *leaves(3, nv_in, nv_uq, nv_ukv, nv_out))
```

```python
import jax
import jax.numpy as jnp
import numpy as np
from jax import lax
from jax.experimental import pallas as pl
from jax.experimental.pallas import tpu as pltpu

F32 = jnp.float32
BF16 = jnp.bfloat16
MESH = pl.DeviceIdType.MESH

D_MODEL = 1024
GRID_W = 64
ROPE_THETA = 10000.0
EPS = 1e-6
A_HEADS, A_KV, A_DIM = 8, 2, 64
A_GROUP = A_HEADS // A_KV
B_HEADS, B_NOPE, B_ROPE, B_V = 4, 64, 32, 128
B_QK = B_NOPE + B_ROPE
B_Q_RANK, B_KV_RANK = 384, 256
N_IN = 2464
SCALE_A = 1.0 / float(np.sqrt(A_DIM))
SCALE_B = 1.0 / float(np.sqrt(B_QK))
LOG2E = float(np.log2(np.e))
LN2 = float(np.log(2.0))
ADAM_LR, ADAM_B1, ADAM_B2, ADAM_EPS, ADAM_WD, ADAM_STEP = 0.001, 0.9, 0.999, 1e-08, 0.01, 10

LANE = 128
VMEM_BYTES = 64 * 1024 * 1024
VMEM_LIMIT = VMEM_BYTES - 8 * 1024 * 1024
VMEM_MID = 48 * 1024 * 1024
VMEM_SMALL = 32 * 1024 * 1024

QA0 = 0
KA0 = QA0 + A_HEADS * LANE
VA0 = KA0 + A_KV * LANE
GA0 = VA0 + A_KV * LANE
GB0 = GA0 + A_HEADS * LANE
CQ0 = GB0 + B_HEADS * LANE
CKV0 = CQ0 + B_Q_RANK
KR0 = CKV0 + B_KV_RANK
N_EXT = KR0 + LANE
N_GATE = (A_HEADS + B_HEADS) * LANE
DELTA_ROWS = 16
N_PRE = KA0 + A_KV * LANE + B_Q_RANK + B_KV_RANK

ROT = LANE // 2
_QA = A_DIM // 4
_QB = B_ROPE // 4
LAY_PLAIN_A = ((0, A_DIM, 0),)
LAY_ROPE_A = ((0, _QA, 0), (2 * _QA, _QA, _QA), (_QA, _QA, ROT), (3 * _QA, _QA, ROT + _QA))
LAY_KR = ((0, _QB, 0), (2 * _QB, _QB, _QB), (_QB, _QB, ROT), (3 * _QB, _QB, ROT + _QB))
LAY_NOPE = ((0, B_NOPE // 2, 2 * _QB), (B_NOPE // 2, B_NOPE // 2, ROT + 2 * _QB))
LAY_ROPE_B = LAY_NOPE + tuple((B_NOPE + a, n, at) for a, n, at in LAY_KR)

N_CHIPS = 4
SH_IN = (D_MODEL, N_IN // N_CHIPS)
SH_UQ = (B_Q_RANK // N_CHIPS, B_HEADS * B_QK)
SH_UKV = (B_KV_RANK, B_HEADS * (B_NOPE + B_V) // N_CHIPS)
SH_OUT = (D_MODEL // N_CHIPS, D_MODEL)
SM_ROWS, SM_W = 16, D_MODEL
SM_IN, SM_AQ, SM_AK, SM_CQ, SM_CKV, SM_BQ, SM_BK, SM_LOSS = range(8)
F32_ROWS, BF16_ROWS = 8, 16


def _pallas(body, **kw):
    return pl.pallas_call(body, **kw)


def _params(sem=None, vmem=None):
    return pltpu.CompilerParams(dimension_semantics=sem, vmem_limit_bytes=vmem)


def _rms_fwd(x, g, n):
    r = lax.rsqrt(jnp.sum(x * x, axis=-1, keepdims=True) * (1.0 / n) + EPS)
    return x * r * g


def _rms_bwd(dy, x, g, n):
    u = dy * g
    r = lax.rsqrt(jnp.sum(x * x, axis=-1, keepdims=True) * (1.0 / n) + EPS)
    ux = jnp.sum(u * x, axis=-1, keepdims=True)
    xhat = x * r
    dx = r * (u - xhat * (r * ux * (1.0 / n)))
    return dx, dy * xhat


def _rope_fwd(y, cos, sin):
    return y * cos + pltpu.roll(y, ROT, 1) * sin


def _rope_bwd(d, cos, sin):
    return d * cos - pltpu.roll(d, ROT, 1) * sin


def _token_tables(refs):
    out = []
    for r_ref, c_ref in zip(refs[0::2], refs[1::2]):
        r, c = r_ref[...], c_ref[...]
        out.append(jnp.concatenate([r[k:k + 1, :] + c for k in range(r.shape[0])], axis=0))
    return out


def _lanes_of(lane, layout):
    m = None
    for _, n, at in layout:
        seg = (lane >= at) & (lane < at + n)
        m = seg if m is None else (m | seg)
    return m


def _unspread_row(v, layout):
    v8 = jnp.broadcast_to(v, (F32_ROWS, LANE))
    lane = lax.broadcasted_iota(jnp.int32, v8.shape, 1)
    out = jnp.zeros_like(v8)
    for a, n, at in layout:
        moved = v8 if a == at else pltpu.roll(v8, (a - at) % LANE, 1)
        out = jnp.where((lane >= a) & (lane < a + n), moved, out)
    return out[0:1, :]


def _nt(a, b):
    return lax.dot_general(a, b, (((1,), (1,)), ((), ())), preferred_element_type=F32)


def _tn(a, b):
    return lax.dot_general(a, b, (((0,), (0,)), ((), ())), preferred_element_type=F32)


def _nn(a, b):
    return jnp.dot(a, b, preferred_element_type=F32)


def _block_rows(i, size):
    if isinstance(i, int):
        return pl.ds(i * size, size)
    return pl.ds(pl.multiple_of(i * size, size), size)


MAX_STATIC_BLOCKS = 32


def _three_stage(n, first, second, third):
    assert n >= 2 and n % 2 == 0
    first(0, 0)
    first(1, 1)
    second(0, 0)
    if n <= MAX_STATIC_BLOCKS:
        for i in range(1, n - 1):
            first(i + 1, (i + 1) % 2)
            second(i, i % 2)
            third(i - 1, (i - 1) % 2)
    else:
        def pair(t, carry):
            i = 2 * t + 1
            first(i + 1, 0)
            second(i, 1)
            third(i - 1, 0)
            first(i + 2, 1)
            second(i + 1, 0)
            third(i, 1)
            return carry

        lax.fori_loop(0, (n - 2) // 2, pair, 0)
    second(n - 1, 1)
    third(n - 2, 0)
    third(n - 1, 1)


def _full(shape):
    return pl.BlockSpec(shape, lambda *_: (0,) * len(shape))


def _table_specs(tm):
    return [pl.BlockSpec((None, tm // GRID_W, LANE), lambda i: (i, 0, 0)), _full((GRID_W, LANE))] * 4


def _resident(shape):
    return pl.BlockSpec(shape, lambda *_: (0,) * len(shape), pipeline_mode=pl.Buffered(1))


def _gather_weights(shards):
    n = len(shards)
    halves = [w.shape[0] // 2 for w in shards]

    def body(*refs):
        w_refs, out_refs, (send_sems, recv_sems) = refs[:n], refs[n:2 * n], refs[2 * n:]
        x, y, c = lax.axis_index("x"), lax.axis_index("y"), lax.axis_index("c")
        sibling = (x, y, 1 - c)
        chips = [(1 - x, y), (x, 1 - y), (1 - x, 1 - y)]
        me = 2 * x + y

        def copy(a, k, j, hc, to):
            part = out_refs[a].at[j, pl.ds(pl.multiple_of(hc * halves[a], BF16_ROWS), halves[a]), :]
            return pltpu.make_async_remote_copy(
                src_ref=part, dst_ref=part, send_sem=send_sems.at[6 * a + k], recv_sem=recv_sems.at[6 * a + k],
                device_id=to, device_id_type=MESH)

        started = []
        for a in range(n):
            out_refs[a][me] = w_refs[a][...].astype(BF16)
            for k, chip in enumerate(chips):
                started.append(copy(a, k, me, c, (*chip, c)))
                started[-1].start()
        for k, chip in enumerate(chips):
            for a in range(n):
                copy(a, k, 2 * chip[0] + chip[1], c, (*chip, c)).wait_recv()
                started.append(copy(a, 3 + k, 2 * chip[0] + chip[1], c, sibling))
                started[-1].start()
        for k, chip in enumerate(chips):
            for a in range(n):
                copy(a, 3 + k, 2 * chip[0] + chip[1], 1 - c, sibling).wait_recv()
        for cp in started:
            cp.wait_send()

    return _pallas(
        body, name="gather_weights",
        out_shape=[jax.ShapeDtypeStruct((N_CHIPS,) + w.shape, BF16) for w in shards],
        in_specs=[pl.BlockSpec(memory_space=pltpu.VMEM)] * n,
        out_specs=[pl.BlockSpec(memory_space=pltpu.VMEM)] * n,
        scratch_shapes=[pltpu.SemaphoreType.DMA((6 * n,)), pltpu.SemaphoreType.DMA((6 * n,))],
        compiler_params=_params(vmem=VMEM_SMALL),
    )(*shards)


def _reduce_grads(parts, small):
    n_big = len(parts)
    n = n_big + 1
    shapes = [p.shape[1:] for p in parts] + [small.shape]
    halves = [sh[0] // 2 for sh in shapes]

    def body(*refs):
        p_refs, out_refs, rec_a, rec_b = refs[:n], refs[n:2 * n], refs[2 * n:3 * n], refs[3 * n:4 * n]
        send_b = refs[4 * n:4 * n + n_big]
        sa_send, sa_recv, sb_send, sb_recv, sc_send, sc_recv = refs[4 * n + n_big:]
        x, y, c = lax.axis_index("x"), lax.axis_index("y"), lax.axis_index("c")
        sibling = (x, y, 1 - c)
        me = 2 * x + y

        def rows(a, hc):
            return pl.ds(pl.multiple_of(hc * halves[a], F32_ROWS), halves[a])

        def partial(a, j, hc):
            return p_refs[a].at[j, rows(a, hc), :] if a < n_big else p_refs[a].at[rows(a, hc), :]

        def copy_a(a, j):
            return pltpu.make_async_remote_copy(
                src_ref=partial(a, j, 1 - c), dst_ref=rec_a[a].at[j],
                send_sem=sa_send.at[N_CHIPS * a + j], recv_sem=sa_recv.at[N_CHIPS * a + j],
                device_id=sibling, device_id_type=MESH)

        def copy_b(a, r):
            j = me ^ r
            k = (N_CHIPS - 1) * a + r - 1
            return pltpu.make_async_remote_copy(
                src_ref=(send_b[a] if a < n_big else rec_a[a]).at[j], dst_ref=rec_b[a].at[r],
                send_sem=sb_send.at[k], recv_sem=sb_recv.at[k], device_id=(j // 2, j % 2, c), device_id_type=MESH)

        def copy_c(a):
            return pltpu.make_async_remote_copy(
                src_ref=out_refs[a].at[rows(a, c), :], dst_ref=out_refs[a].at[rows(a, c), :],
                send_sem=sc_send.at[a], recv_sem=sc_recv.at[a], device_id=sibling, device_id_type=MESH)

        for a in range(n):
            for j in range(N_CHIPS):
                copy_a(a, j).start()
        for r in range(1, N_CHIPS):
            j = me ^ r
            for a in range(n):
                copy_a(a, j).wait_recv()
                chip_part = rec_a[a][j] + partial(a, j, c)[...]
                if a < n_big:
                    send_b[a][j] = chip_part.astype(BF16)
                else:
                    rec_a[a][j] = chip_part
                copy_b(a, r).start()
        for a in range(n):
            copy_a(a, me).wait_recv()
            rec_b[a][0] = (rec_a[a][me] + partial(a, me, c)[...]).astype(rec_b[a].dtype)
        for a in range(n):
            for r in range(1, N_CHIPS):
                copy_b(a, r).wait_recv()
            total = rec_b[a][me].astype(F32)
            for j in range(1, N_CHIPS):
                total = total + rec_b[a][j ^ me].astype(F32)
            out_refs[a][rows(a, c), :] = total
            copy_c(a).start()
        for a in range(n):
            copy_c(a).wait_recv()
        for a in range(n):
            for j in range(N_CHIPS):
                copy_a(a, j).wait_send()
            for r in range(1, N_CHIPS):
                copy_b(a, r).wait_send()
            copy_c(a).wait_send()

    dma = pltpu.SemaphoreType.DMA
    return _pallas(
        body, name="reduce_grads",
        out_shape=[jax.ShapeDtypeStruct(sh, F32) for sh in shapes],
        in_specs=[pl.BlockSpec(memory_space=pltpu.VMEM)] * n,
        out_specs=[pl.BlockSpec(memory_space=pltpu.VMEM)] * n,
        scratch_shapes=[pltpu.VMEM((N_CHIPS, h) + sh[1:], F32) for h, sh in zip(halves, shapes)]
                       + [pltpu.VMEM((N_CHIPS, h) + sh[1:], BF16 if a < n_big else F32)
                          for a, (h, sh) in enumerate(zip(halves, shapes))]
                       + [pltpu.VMEM((N_CHIPS, h) + sh[1:], BF16) for h, sh in zip(halves[:n_big], shapes[:n_big])]
                       + [dma((N_CHIPS * n,)), dma((N_CHIPS * n,)), dma(((N_CHIPS - 1) * n,)), dma(((N_CHIPS - 1) * n,)),
                          dma((n,)), dma((n,))],
        compiler_params=_params(vmem=VMEM_LIMIT),
    )(*parts, small)


def _pre(x, tabs, w_in_ext, w_uq_pad, w_ukv_ext, gains, tm):
    s_len = x.shape[0]
    nt = s_len // tm

    def body(x_ref, car_ref, cac_ref, sar_ref, sac_ref, cbr_ref, cbc_ref, sbr_ref, sbc_ref, win_ref, wuq_ref, wukv_ref,
             gin_ref, gaq_ref, gak_ref, gcq_ref, gckv_ref, gbq_ref, gbk_ref,
             xn_ref, gates_ref, pre_ref, qbpre_ref, kbpre_ref, cq_ref, ckv_ref,
             qa_ref, ka_ref, va_ref, qb_ref, kb_ref, vb_ref, proj):
        xn = _rms_fwd(x_ref[...], gin_ref[...], D_MODEL)
        xn_ref[...] = jnp.transpose(xn).astype(BF16)
        xb = xn.astype(BF16)
        pre_ref[:, 0:VA0] = _nn(xb, win_ref[:, 0:VA0])
        gates_ref[...] = _nn(xb, win_ref[:, GA0:GA0 + N_GATE])
        pre_ref[:, VA0:N_PRE] = _nn(xb, win_ref[:, CQ0:KR0])
        proj[...] = _nn(xb, win_ref[:, VA0:GA0])
        kr = _nn(xb, win_ref[:, KR0:N_EXT])
        ca, sa, cb, sb = _token_tables((car_ref, cac_ref, sar_ref, sac_ref, cbr_ref, cbc_ref, sbr_ref, sbc_ref))
        lane = lax.broadcasted_iota(jnp.int32, (tm, LANE), 1)
        for h in range(A_HEADS):
            yq = _rms_fwd(pre_ref[:, QA0 + LANE * h:QA0 + LANE * (h + 1)], gaq_ref[...], A_DIM)
            qa_ref[h] = (_rope_fwd(yq, ca, sa) * (SCALE_A * LOG2E)).astype(BF16)
        for h in range(A_KV):
            yk = _rms_fwd(pre_ref[:, KA0 + LANE * h:KA0 + LANE * (h + 1)], gak_ref[...], A_DIM)
            ka_ref[h] = _rope_fwd(yk, ca, sa).astype(BF16)
            va_ref[h] = jnp.where(lane == A_DIM, 1.0, proj[:, LANE * h:LANE * (h + 1)]).astype(BF16)
        cq = _rms_fwd(pre_ref[:, VA0:VA0 + B_Q_RANK], gcq_ref[...], B_Q_RANK)
        cq_ref[...] = jnp.transpose(cq).astype(BF16)
        qbpre_ref[...] = _nn(cq.astype(BF16), wuq_ref[...])
        ckv = _rms_fwd(pre_ref[:, VA0 + B_Q_RANK:N_PRE], gckv_ref[...], B_KV_RANK)
        ckv_ref[...] = jnp.transpose(ckv).astype(BF16)
        kvb = _nn(ckv.astype(BF16), wukv_ref[...])
        for h in range(B_HEADS):
            yq = _rms_fwd(qbpre_ref[:, LANE * h:LANE * (h + 1)], gbq_ref[...], B_QK)
            qb_ref[h] = (_rope_fwd(yq, cb, sb) * (SCALE_B * LOG2E)).astype(BF16)
            kp = kvb[:, LANE * h:LANE * (h + 1)] + kr
            kbpre_ref[:, LANE * h:LANE * (h + 1)] = kp
            kb_ref[h] = _rope_fwd(_rms_fwd(kp, gbk_ref[...], B_QK), cb, sb).astype(BF16)
            vb_ref[h, :, 0:LANE] = kvb[:, B_HEADS * LANE + LANE * h:B_HEADS * LANE + LANE * (h + 1)].astype(BF16)
            vb_ref[h, :, LANE:2 * LANE] = jnp.where(lane == 0, 1.0, 0.0).astype(BF16)

    row = lambda w: pl.BlockSpec((tm, w), lambda i: (i, 0))
    col = lambda w: pl.BlockSpec((w, tm), lambda i: (0, i))
    heads = lambda n: pl.BlockSpec((n, tm, LANE), lambda i: (0, i, 0))
    hs = lambda n: jax.ShapeDtypeStruct((n, s_len, LANE), BF16)
    return _pallas(
        body, name="pre", grid=(nt,),
        in_specs=[row(D_MODEL)] + _table_specs(tm)
                 + [_resident(w_in_ext.shape), _resident(w_uq_pad.shape), _resident(w_ukv_ext.shape)]
                 + [_full(g.shape) for g in gains],
        out_specs=[col(D_MODEL), row(N_GATE), row(N_PRE), row(B_HEADS * LANE), row(B_HEADS * LANE),
                   col(B_Q_RANK), col(B_KV_RANK),
                   heads(A_HEADS), heads(A_KV), heads(A_KV), heads(B_HEADS), heads(B_HEADS),
                   pl.BlockSpec((B_HEADS, tm, 2 * LANE), lambda i: (0, i, 0))],
        out_shape=[jax.ShapeDtypeStruct((D_MODEL, s_len), BF16), jax.ShapeDtypeStruct((s_len, N_GATE), F32),
                   jax.ShapeDtypeStruct((s_len, N_PRE), F32), jax.ShapeDtypeStruct((s_len, B_HEADS * LANE), F32),
                   jax.ShapeDtypeStruct((s_len, B_HEADS * LANE), F32),
                   jax.ShapeDtypeStruct((B_Q_RANK, s_len), BF16), jax.ShapeDtypeStruct((B_KV_RANK, s_len), BF16),
                   hs(A_HEADS), hs(A_KV), hs(A_KV), hs(B_HEADS), hs(B_HEADS),
                   jax.ShapeDtypeStruct((B_HEADS, s_len, 2 * LANE), BF16)],
        scratch_shapes=[pltpu.VMEM((tm, A_KV * LANE), F32)],
        compiler_params=_params(("parallel",), VMEM_LIMIT),
    )(x, *tabs, w_in_ext, w_uq_pad, w_ukv_ext, *gains)


def _attn_fwd(q, k, v, group, l_col, tq, tk, tiles, name):
    n_heads, s_len, _ = q.shape
    v_w = v.shape[2]
    nk = s_len // tk

    def body(q_ref, k_ref, v_ref, o_ref, lse_ref, s_buf, p_buf, a_buf, m_ref, acc_ref):
        def scores(g, slot):
            s_buf[slot] = _nt(q_ref[_block_rows(g // nk, tq), :], k_ref[_block_rows(g % nk, tk), :])

        def softmax(g, slot):
            t = g // nk
            s = s_buf[slot]
            m_old = m_ref[t]
            m_new = jnp.maximum(m_old, jnp.max(s, axis=-1, keepdims=True))
            m_ref[t] = m_new
            a_buf[slot] = jnp.exp2(m_old - m_new)
            p_buf[slot] = jnp.exp2(s - jnp.tile(m_new, (1, tk // LANE))).astype(BF16)

        def values(g, slot):
            t = g // nk
            pv = _nn(p_buf[slot], v_ref[_block_rows(g % nk, tk), :])
            for c in range(0, v_w, LANE):
                acc_ref[t, :, c:c + LANE] = a_buf[slot] * acc_ref[t, :, c:c + LANE] + pv[:, c:c + LANE]

        m_ref[...] = jnp.full(m_ref.shape, -1e30, F32)
        acc_ref[...] = jnp.zeros(acc_ref.shape, F32)
        _three_stage(tiles * nk, scores, softmax, values)
        for t in range(tiles):
            l = acc_ref[t, :, l_col:l_col + 1]
            o = acc_ref[t, :, 0:LANE] * (1.0 / l)
            if l_col < LANE:
                lane = lax.broadcasted_iota(jnp.int32, o.shape, 1)
                o = jnp.where(lane == l_col, 0.0, o)
            o_ref[t * tq:(t + 1) * tq, :] = o
            lse_ref[t] = jnp.transpose(m_ref[t] + jnp.log2(jnp.broadcast_to(l, (tq, LANE))))[0:1, :]

    return _pallas(
        body, name=name, grid=(n_heads, s_len // (tiles * tq)),
        in_specs=[pl.BlockSpec((None, tiles * tq, LANE), lambda h, i: (h, i, 0)),
                  pl.BlockSpec((None, s_len, LANE), lambda h, i: (h // group, 0, 0)),
                  pl.BlockSpec((None, s_len, v_w), lambda h, i: (h // group, 0, 0))],
        out_specs=[pl.BlockSpec((None, tiles * tq, LANE), lambda h, i: (h, i, 0)),
                   pl.BlockSpec((None, tiles, 1, tq), lambda h, i: (h, i, 0, 0))],
        out_shape=[jax.ShapeDtypeStruct((n_heads, s_len, LANE), F32),
                   jax.ShapeDtypeStruct((n_heads, s_len // tq, 1, tq), F32)],
        scratch_shapes=[pltpu.VMEM((2, tq, tk), F32), pltpu.VMEM((2, tq, tk), BF16), pltpu.VMEM((2, tq, LANE), F32),
                        pltpu.VMEM((tiles, tq, LANE), F32), pltpu.VMEM((tiles, tq, v_w), F32)],
        compiler_params=_params(("parallel", "parallel"), VMEM_MID),
    )(q, k, v)


def _mid(x, target, o_a, o_b, gates, w_out, tm):
    s_len = x.shape[0]
    nt = s_len // tm
    n_heads = A_HEADS + B_HEADS
    d_mix = w_out.shape[0]
    pairs = A_HEADS // 2

    def body(x_ref, t_ref, oa_ref, ob_ref, g_ref, w_ref,
             yt_ref, dh_ref, dgate_ref, doa_ref, dob_ref, delta_ref, loss_ref, silu_scr, dsilu_scr, y_ref):
        @pl.when(pl.program_id(0) == 0)
        def _():
            loss_ref[...] = jnp.zeros_like(loss_ref)

        def o_of(h):
            return oa_ref[h] if h < A_HEADS else ob_ref[h - A_HEADS]

        def gated(h):
            cols = slice(LANE * h, LANE * (h + 1))
            g = g_ref[:, cols]
            sig = 1.0 / (1.0 + jnp.exp(-g))
            silu = g * sig
            silu_scr[:, cols] = silu
            dsilu_scr[:, cols] = sig * (1.0 + g * (1.0 - sig))
            return o_of(h) * silu

        for c in range(pairs + B_HEADS):
            y = gated(2 * c) + pltpu.roll(gated(2 * c + 1), ROT, 1) if c < pairs else gated(A_HEADS + c - pairs)
            cols = slice(LANE * c, LANE * (c + 1))
            y_ref[:, cols] = y.astype(BF16)
            yt_ref[cols, :] = jnp.transpose(y).astype(BF16)
        err = x_ref[...] + _nn(y_ref[...], w_ref[...]) - t_ref[...]
        sq = jnp.sum(jnp.sum(err * err, axis=-1, keepdims=True), axis=0, keepdims=True)
        loss_ref[...] += jnp.broadcast_to(sq * (0.5 / D_MODEL), loss_ref.shape)
        dh = err * (1.0 / D_MODEL)
        dh_ref[...] = dh
        dy = _nt(dh.astype(BF16), w_ref[...])
        lane = lax.broadcasted_iota(jnp.int32, (tm, LANE), 1)
        delta = jnp.zeros((tm, LANE), F32)
        for h in range(n_heads):
            cols = slice(LANE * h, LANE * (h + 1))
            if h < A_HEADS:
                packed = dy[:, LANE * (h // 2):LANE * (h // 2 + 1)]
                dyh = packed if h % 2 == 0 else pltpu.roll(packed, ROT, 1)
            else:
                dyh = dy[:, LANE * (pairs + h - A_HEADS):LANE * (pairs + h - A_HEADS + 1)]
            oh = o_of(h)
            do = dyh * silu_scr[:, cols]
            dgate_ref[:, cols] = (dyh * oh * dsilu_scr[:, cols]).astype(BF16)
            delta = jnp.where(lane == h, jnp.sum(do * oh, axis=-1, keepdims=True), delta)
            if h < A_HEADS:
                doa_ref[h] = do.astype(BF16)
            else:
                dob_ref[h - A_HEADS] = do.astype(BF16)
        delta_ref[...] = jnp.transpose(delta)[0:DELTA_ROWS, :]

    row = lambda w: pl.BlockSpec((tm, w), lambda i: (i, 0))
    heads = lambda n, w=LANE: pl.BlockSpec((n, tm, w), lambda i: (0, i, 0))
    return _pallas(
        body, name="mid", grid=(nt,),
        in_specs=[row(D_MODEL), row(D_MODEL), heads(A_HEADS), heads(B_HEADS), row(N_GATE), _resident(w_out.shape)],
        out_specs=[pl.BlockSpec((d_mix, tm), lambda i: (0, i)), row(D_MODEL), row(N_GATE), heads(A_HEADS), heads(B_HEADS),
                   pl.BlockSpec((DELTA_ROWS, tm), lambda i: (0, i)),
                   _full((8, LANE))],
        out_shape=[jax.ShapeDtypeStruct((d_mix, s_len), BF16), jax.ShapeDtypeStruct((s_len, D_MODEL), F32),
                   jax.ShapeDtypeStruct((s_len, N_GATE), BF16),
                   jax.ShapeDtypeStruct((A_HEADS, s_len, LANE), BF16), jax.ShapeDtypeStruct((B_HEADS, s_len, LANE), BF16),
                   jax.ShapeDtypeStruct((DELTA_ROWS, s_len), F32), jax.ShapeDtypeStruct((8, LANE), F32)],
        scratch_shapes=[pltpu.VMEM((tm, N_GATE), F32), pltpu.VMEM((tm, N_GATE), F32), pltpu.VMEM((tm, d_mix), BF16)],
        compiler_params=_params(("arbitrary",), VMEM_LIMIT),
    )(x, target, o_a, o_b, gates, w_out)


def _attn_bwd(q, k, v, do, lse, delta, group, tq, tk, tiles, name):
    n_heads, s_len, _ = q.shape
    nq = s_len // tq

    def body(q_ref, do_ref, lse_ref, delta_ref, k_ref, v_ref, dq_ref, dk_ref, dv_ref, s_buf, dp_buf, p_buf, ds_buf):
        @pl.when(pl.program_id(1) == 0)
        def _():
            dq_ref[...] = jnp.zeros_like(dq_ref)

        dk_ref[...] = jnp.zeros_like(dk_ref)
        dv_ref[...] = jnp.zeros_like(dv_ref)

        def keys(g):
            return _block_rows(g // nq, tk)

        def queries(g):
            return _block_rows(g % nq, tq)

        def scores(g, slot):
            s_buf[slot] = _nt(k_ref[keys(g), :], q_ref[queries(g), :])
            dp_buf[slot] = _nt(v_ref[keys(g), :], do_ref[queries(g), :])

        def elementwise(g, slot):
            p = jnp.exp2(s_buf[slot] - lse_ref[g % nq])
            p_buf[slot] = p.astype(BF16)
            ds_buf[slot] = (p * (dp_buf[slot] - delta_ref[g % nq])).astype(BF16)

        def grads(g, slot):
            dv_ref[keys(g), :] += _nn(p_buf[slot], do_ref[queries(g), :])
            dk_ref[keys(g), :] += _nn(ds_buf[slot], q_ref[queries(g), :])
            dq_ref[queries(g), :] += _tn(ds_buf[slot], k_ref[keys(g), :])

        _three_stage(tiles * nq, scores, elementwise, grads)

    whole = lambda: pl.BlockSpec((None, s_len, LANE), lambda h, j: (h, 0, 0))
    stat = lambda: pl.BlockSpec((None, nq, 1, tq), lambda h, j: (h, 0, 0, 0))
    kvb = lambda: pl.BlockSpec((None, tiles * tk, LANE), lambda h, j: (h // group, j, 0))
    outb = lambda: pl.BlockSpec((None, tiles * tk, LANE), lambda h, j: (h, j, 0))
    shape = jax.ShapeDtypeStruct((n_heads, s_len, LANE), F32)
    return _pallas(
        body, name=name, grid=(n_heads, s_len // (tiles * tk)),
        in_specs=[whole(), whole(), stat(), stat(), kvb(), kvb()],
        out_specs=[whole(), outb(), outb()],
        out_shape=[shape, shape, shape],
        scratch_shapes=[pltpu.VMEM((2, tk, tq), F32), pltpu.VMEM((2, tk, tq), F32),
                        pltpu.VMEM((2, tk, tq), BF16), pltpu.VMEM((2, tk, tq), BF16)],
        compiler_params=_params(("parallel", "arbitrary"), VMEM_MID),
    )(q, do, lse, delta, k, v)


def _post(x, dh, pre, qbpre, kbpre, dgate, dqa, dka, dva, dqb, dkb, dvb, loss_part, tabs,
          w_in_ext, w_uq_pad, w_ukv_ext, gains, tm):
    s_len = x.shape[0]
    nt = s_len // tm

    def body(x_ref, dh_ref, pre_ref, qbpre_ref, kbpre_ref, dgate_ref,
             dqa_ref, dka_ref, dva_ref, dqb_ref, dkb_ref, dvb_ref, loss_ref,
             car_ref, cac_ref, sar_ref, sac_ref, cbr_ref, cbc_ref, sbr_ref, sbc_ref, win_ref, wuq_ref, wukv_ref,
             gin_ref, gaq_ref, gak_ref, gcq_ref, gckv_ref, gbq_ref, gbk_ref,
             gx_ref, dproj_ref, dqbpre_ref, dkvb_ref, dsm_ref):
        @pl.when(pl.program_id(0) == 0)
        def _():
            dsm_ref[...] = jnp.zeros_like(dsm_ref)
            dsm_ref[SM_LOSS:SM_LOSS + 1, 0:LANE] = loss_ref[0:1, :]

        def add_small(r, dg):
            dsm_ref[r:r + 1, 0:dg.shape[1]] += dg

        def tok_sum(a):
            return jnp.sum(a, axis=0, keepdims=True)

        ca, sa, cb, sb = _token_tables((car_ref, cac_ref, sar_ref, sac_ref, cbr_ref, cbc_ref, sbr_ref, sbc_ref))
        lane = lax.broadcasted_iota(jnp.int32, (tm, LANE), 1)

        nope_lanes = _lanes_of(lane, LAY_NOPE)

        def back(c0, c1):
            return _nt(dproj_ref[:, c0:c1], win_ref[:, c0:c1])

        dproj_ref[:, GA0:GA0 + N_GATE] = dgate_ref[...]
        dxn = back(GA0, GA0 + N_GATE)
        dg = jnp.zeros((1, LANE), F32)
        for h in range(A_HEADS):
            dn = _rope_bwd(dqa_ref[h] * SCALE_A, ca, sa)
            dx, dgr = _rms_bwd(dn, pre_ref[:, QA0 + LANE * h:QA0 + LANE * (h + 1)], gaq_ref[...], A_DIM)
            dproj_ref[:, QA0 + LANE * h:QA0 + LANE * (h + 1)] = dx.astype(BF16)
            dg = dg + tok_sum(dgr)
        add_small(SM_AQ, _unspread_row(dg, LAY_ROPE_A))
        dxn = dxn + back(QA0, KA0)
        dg = jnp.zeros((1, LANE), F32)
        for h in range(A_KV):
            dk = dka_ref[A_GROUP * h]
            dv = dva_ref[A_GROUP * h]
            for g in range(1, A_GROUP):
                dk = dk + dka_ref[A_GROUP * h + g]
                dv = dv + dva_ref[A_GROUP * h + g]
            dn = _rope_bwd(dk * LN2, ca, sa)
            dx, dgr = _rms_bwd(dn, pre_ref[:, KA0 + LANE * h:KA0 + LANE * (h + 1)], gak_ref[...], A_DIM)
            dproj_ref[:, KA0 + LANE * h:KA0 + LANE * (h + 1)] = dx.astype(BF16)
            dproj_ref[:, VA0 + LANE * h:VA0 + LANE * (h + 1)] = dv.astype(BF16)
            dg = dg + tok_sum(dgr)
        add_small(SM_AK, _unspread_row(dg, LAY_ROPE_A))
        dxn = dxn + back(KA0, GA0)
        dg = jnp.zeros((1, LANE), F32)
        for h in range(B_HEADS):
            cols = slice(LANE * h, LANE * (h + 1))
            dn = _rope_bwd(dqb_ref[h] * SCALE_B, cb, sb)
            dx, dgr = _rms_bwd(dn, qbpre_ref[:, cols], gbq_ref[...], B_QK)
            dqbpre_ref[:, cols] = dx.astype(BF16)
            dg = dg + tok_sum(dgr)
        add_small(SM_BQ, _unspread_row(dg, LAY_ROPE_B))
        dcq = _nt(dqbpre_ref[...], wuq_ref[...])
        dx, dgr = _rms_bwd(dcq, pre_ref[:, VA0:VA0 + B_Q_RANK], gcq_ref[...], B_Q_RANK)
        dproj_ref[:, CQ0:CQ0 + B_Q_RANK] = dx.astype(BF16)
        add_small(SM_CQ, tok_sum(dgr))
        dxn = dxn + back(CQ0, CKV0)
        dg = jnp.zeros((1, LANE), F32)
        dkr = jnp.zeros((tm, LANE), F32)
        for h in range(B_HEADS):
            cols = slice(LANE * h, LANE * (h + 1))
            dn = _rope_bwd(dkb_ref[h] * LN2, cb, sb)
            dx, dgr = _rms_bwd(dn, kbpre_ref[:, cols], gbk_ref[...], B_QK)
            dkvb_ref[:, cols] = jnp.where(nope_lanes, dx, 0.0).astype(BF16)
            dkvb_ref[:, B_HEADS * LANE + LANE * h:B_HEADS * LANE + LANE * (h + 1)] = dvb_ref[h].astype(BF16)
            dkr = dkr + dx
            dg = dg + tok_sum(dgr)
        add_small(SM_BK, _unspread_row(dg, LAY_ROPE_B))
        dproj_ref[:, KR0:KR0 + LANE] = jnp.where(_lanes_of(lane, LAY_KR), dkr, 0.0).astype(BF16)
        dckv = _nt(dkvb_ref[...], wukv_ref[...])
        dx, dgr = _rms_bwd(dckv, pre_ref[:, VA0 + B_Q_RANK:N_PRE], gckv_ref[...], B_KV_RANK)
        dproj_ref[:, CKV0:CKV0 + B_KV_RANK] = dx.astype(BF16)
        add_small(SM_CKV, tok_sum(dgr))
        dxn = dxn + back(CKV0, N_EXT)
        dx, dgr = _rms_bwd(dxn, x_ref[...], gin_ref[...], D_MODEL)
        gx_ref[...] = dh_ref[...] + dx
        add_small(SM_IN, tok_sum(dgr))

    row = lambda w: pl.BlockSpec((tm, w), lambda i: (i, 0))
    heads = lambda n: pl.BlockSpec((n, tm, LANE), lambda i: (0, i, 0))
    return _pallas(
        body, name="post", grid=(nt,),
        in_specs=[row(D_MODEL), row(D_MODEL), row(N_PRE), row(B_HEADS * LANE), row(B_HEADS * LANE), row(N_GATE),
                  heads(A_HEADS), heads(A_HEADS), heads(A_HEADS), heads(B_HEADS), heads(B_HEADS), heads(B_HEADS),
                  _full(loss_part.shape)] + _table_specs(tm)
                 + [_resident(w_in_ext.shape), _resident(w_uq_pad.shape), _resident(w_ukv_ext.shape)]
                 + [_full(g.shape) for g in gains],
        out_specs=[row(D_MODEL), row(N_EXT), row(B_HEADS * LANE), row(2 * B_HEADS * LANE), _full((SM_ROWS, SM_W))],
        out_shape=[jax.ShapeDtypeStruct((s_len, D_MODEL), F32), jax.ShapeDtypeStruct((s_len, N_EXT), BF16),
                   jax.ShapeDtypeStruct((s_len, B_HEADS * LANE), BF16),
                   jax.ShapeDtypeStruct((s_len, 2 * B_HEADS * LANE), BF16),
                   jax.ShapeDtypeStruct((SM_ROWS, SM_W), F32)],
        compiler_params=_params(("arbitrary",), VMEM_LIMIT),
    )(x, dh, pre, qbpre, kbpre, dgate, dqa, dka, dva, dqb, dkb, dvb, loss_part, *tabs,
      w_in_ext, w_uq_pad, w_ukv_ext, *gains)


def _grad_w(a_t, b, tn, ts, name):
    m, s_len = a_t.shape
    n = b.shape[1]

    def body(a_ref, b_ref, o_ref):
        @pl.when(pl.program_id(1) == 0)
        def _():
            o_ref[...] = jnp.zeros_like(o_ref)

        o_ref[...] += _nn(a_ref[...], b_ref[...].astype(BF16))

    return _pallas(
        body, name=name, grid=(n // tn, s_len // ts),
        in_specs=[pl.BlockSpec((m, ts), lambda j, t: (0, t)), pl.BlockSpec((ts, tn), lambda j, t: (t, j))],
        out_specs=pl.BlockSpec((m, tn), lambda j, t: (0, j)),
        out_shape=jax.ShapeDtypeStruct((m, n), F32),
        compiler_params=_params(("parallel", "arbitrary"), VMEM_MID),
    )(a_t, b)


def _grad_w_pairs(pairs, ts, name):
    s_len = pairs[0][0].shape[1]
    n_p = len(pairs)

    def body(*refs):
        for a_ref, b_ref, o_ref in zip(refs[0:2 * n_p:2], refs[1:2 * n_p:2], refs[2 * n_p:]):
            @pl.when(pl.program_id(0) == 0)
            def _():
                o_ref[...] = jnp.zeros_like(o_ref)

            o_ref[...] += _nn(a_ref[...], b_ref[...].astype(BF16))

    in_specs, flat = [], []
    for a_t, b in pairs:
        in_specs += [pl.BlockSpec((a_t.shape[0], ts), lambda t: (0, t)), pl.BlockSpec((ts, b.shape[1]), lambda t: (t, 0))]
        flat += [a_t, b]
    return _pallas(
        body, name=name, grid=(s_len // ts,),
        in_specs=in_specs,
        out_specs=[_full((a_t.shape[0], b.shape[1])) for a_t, b in pairs],
        out_shape=[jax.ShapeDtypeStruct((a_t.shape[0], b.shape[1]), F32) for a_t, b in pairs],
        compiler_params=_params(("arbitrary",), VMEM_MID),
    )(*flat)


def _adam_math(w, g, m, v):
    nm = ADAM_B1 * m + (1.0 - ADAM_B1) * g
    nv = ADAM_B2 * v + (1.0 - ADAM_B2) * (g * g)
    m_hat = nm / (1.0 - ADAM_B1 ** ADAM_STEP)
    v_hat = nv / (1.0 - ADAM_B2 ** ADAM_STEP)
    return -ADAM_LR * (m_hat / (jnp.sqrt(v_hat) + ADAM_EPS) + ADAM_WD * w), nm, nv


def _adamw_rows(w, g, m, v, tr):
    rows, cols = w.shape

    def body(w_ref, g_ref, m_ref, v_ref, d_ref, nm_ref, nv_ref):
        d_ref[...], nm_ref[...], nv_ref[...] = _adam_math(w_ref[...], g_ref[...], m_ref[...], v_ref[...])

    blk = pl.BlockSpec((tr, cols), lambda i: (i, 0))
    shape = jax.ShapeDtypeStruct((rows, cols), F32)
    return _pallas(
        body, name="adamw_w_in", grid=(rows // tr,),
        in_specs=[blk] * 4, out_specs=[blk] * 3, out_shape=[shape] * 3,
        compiler_params=_params(("parallel",), VMEM_SMALL),
    )(w, g, m, v)


def _adamw_rest(bigs, smalls, g_small):
    nb, ns = len(bigs), len(smalls)

    def body(*refs):
        ins, outs = refs[:4 * nb + 3 * ns + 1], refs[4 * nb + 3 * ns + 1:]
        for i in range(nb):
            w_ref, g_ref, m_ref, v_ref = ins[4 * i:4 * i + 4]
            d_ref, nm_ref, nv_ref = outs[3 * i:3 * i + 3]
            d_ref[...], nm_ref[...], nv_ref[...] = _adam_math(w_ref[...], g_ref[...], m_ref[...], v_ref[...])
        gs_ref = ins[-1]
        for i in range(ns):
            w_ref, m_ref, v_ref = ins[4 * nb + 3 * i:4 * nb + 3 * i + 3]
            g_ref, d_ref, nm_ref, nv_ref = outs[3 * nb + 4 * i:3 * nb + 4 * i + 4]
            g = gs_ref[i:i + 1, 0:w_ref.shape[1]]
            g_ref[...] = g
            d_ref[...], nm_ref[...], nv_ref[...] = _adam_math(w_ref[...], g, m_ref[...], v_ref[...])

    flat_in = [a for quad in bigs for a in quad] + [a for tri in smalls for a in tri] + [g_small]
    out_shape = ([jax.ShapeDtypeStruct(q[0].shape, F32) for q in bigs for _ in range(3)]
                 + [jax.ShapeDtypeStruct(t[0].shape, F32) for t in smalls for _ in range(4)])
    return _pallas(
        body, name="adamw_rest",
        in_specs=[pl.BlockSpec(memory_space=pltpu.VMEM)] * len(flat_in),
        out_specs=[pl.BlockSpec(memory_space=pltpu.VMEM)] * len(out_shape),
        out_shape=out_shape,
        compiler_params=_params(vmem=VMEM_SMALL),
    )(*flat_in)


def _place(pieces, n):
    out, at = [], 0
    for lane0, arr in sorted(pieces, key=lambda p: p[0]):
        out += [jnp.zeros((n, lane0 - at), F32), arr]
        at = lane0 + arr.shape[1]
    return jnp.concatenate(out + [jnp.zeros((n, LANE - at), F32)], axis=1)


def _rope_tables(s_len, tm):
    rows = s_len // GRID_W
    row = jnp.arange(rows, dtype=F32)
    col = jnp.arange(GRID_W, dtype=F32)

    def lay(dim, layout, first_dim, ones):
        half = dim // 2
        inv = 1.0 / (ROPE_THETA ** (jnp.arange(0, half, 2, dtype=F32) / half))
        ang_r, ang_c = row[:, None] * inv[None, :], col[:, None] * inv[None, :]
        at = {a - first_dim: lane0 for a, _, lane0 in layout}
        q = dim // 4
        r1, r2, c1, c2 = at[0], at[q], at[2 * q], at[3 * q]
        cos_r = _place([(r1, jnp.cos(ang_r)), (r2, jnp.cos(ang_r))], rows)
        sin_r = _place([(r1, -jnp.sin(ang_r)), (r2, jnp.sin(ang_r))], rows)
        cos_c = _place([(c1, jnp.cos(ang_c)), (c2, jnp.cos(ang_c))] + [(l0, jnp.ones((GRID_W, n), F32)) for _, n, l0 in ones],
                       GRID_W)
        sin_c = _place([(c1, -jnp.sin(ang_c)), (c2, jnp.sin(ang_c))], GRID_W)
        by_block = (s_len // tm, tm // GRID_W, LANE)
        return cos_r.reshape(by_block), cos_c, sin_r.reshape(by_block), sin_c

    return lay(A_DIM, LAY_ROPE_A, 0, ()) + lay(B_ROPE, LAY_KR, 0, LAY_NOPE)


def _spread(w, n_heads, dim, axis, layout):
    w3 = w.reshape(w.shape[:axis] + (n_heads, dim) + w.shape[axis + 1:])
    out, at = [], 0

    def zeros(n):
        return jnp.zeros(w3.shape[:axis + 1] + (n,) + w3.shape[axis + 2:], w.dtype)

    for a0, n, lane0 in sorted(layout, key=lambda seg: seg[2]):
        out += [zeros(lane0 - at), lax.slice_in_dim(w3, a0, a0 + n, axis=axis + 1)]
        at = lane0 + n
    out = jnp.concatenate(out + [zeros(LANE - at)], axis=axis + 1)
    return out.reshape(w.shape[:axis] + (n_heads * LANE,) + w.shape[axis + 1:])


def _head_cols(first, n_heads, dim, layout):
    out = np.full((n_heads * LANE,), -1, np.int32)
    for h in range(n_heads):
        for a0, n, lane0 in layout:
            out[h * LANE + lane0:h * LANE + lane0 + n] = first + h * dim + a0 + np.arange(n)
    return out


def _inverse(src, n):
    dst = np.full((n,), -1, np.int32)
    dst[src[src >= 0]] = np.nonzero(src >= 0)[0]
    return dst


def _column_maps():
    a_w, kv_w = A_HEADS * A_DIM, A_KV * A_DIM
    o_g = a_w + 2 * kv_w
    o_cq = o_g + a_w
    o_kr = o_cq + B_Q_RANK + B_KV_RANK
    src_in = np.concatenate([
        _head_cols(0, A_HEADS, A_DIM, LAY_ROPE_A), _head_cols(a_w, A_KV, A_DIM, LAY_ROPE_A),
        _head_cols(a_w + kv_w, A_KV, A_DIM, LAY_PLAIN_A), _head_cols(o_g, A_HEADS, A_DIM, LAY_PLAIN_A),
        np.arange(o_kr + B_ROPE, N_IN), np.arange(o_cq, o_kr), _head_cols(o_kr, 1, B_ROPE, LAY_KR)]).astype(np.int32)
    src_uq = _head_cols(0, B_HEADS, B_QK, LAY_ROPE_B)
    per = B_NOPE + B_V
    src_ukv = np.concatenate([_head_cols(0, B_HEADS, per, LAY_NOPE),
                              _head_cols(B_NOPE, B_HEADS, per, ((0, B_V, 0),))]).astype(np.int32)
    assert len(src_in) == N_EXT
    return src_in, src_uq, src_ukv


def _round_up(n, m):
    return (n + m - 1) // m * m


def _permute_cols(xs, maps, stacks, name):
    maps = [np.asarray(m, np.int32) for m in maps]
    n_arr = len(xs)

    def block(ref, b):
        if len(ref.shape) == 2:
            return ref.at[:, b * LANE:(b + 1) * LANE]
        per = ref.shape[2] // LANE
        return ref.at[b // per, :, (b % per) * LANE:(b % per + 1) * LANE]

    def body(*refs):
        row = lax.broadcasted_iota(jnp.int32, (LANE, LANE), 0)
        for x_ref, src_ref, o_ref, src in zip(refs[:n_arr], refs[n_arr:2 * n_arr], refs[2 * n_arr:], maps):
            for c in range(len(src) // LANE):
                want = src[c * LANE:(c + 1) * LANE]
                if want[0] >= 0 and want[0] % LANE == 0 and np.array_equal(want, want[0] + np.arange(LANE)):
                    block(o_ref, c)[...] = block(x_ref, int(want[0]) // LANE)[...]
                    continue
                acc = jnp.zeros((x_ref.shape[-2], LANE), F32)
                for kb in sorted({int(v) // LANE for v in want if v >= 0}):
                    sel = jnp.where(row + kb * LANE == src_ref[:, c * LANE:(c + 1) * LANE], 1.0, 0.0).astype(BF16)
                    part = block(x_ref, kb)[...]
                    if part.dtype == BF16:
                        acc = acc + _nn(part, sel)
                    else:
                        hi = part.astype(BF16)
                        rest = part - hi.astype(F32)
                        mid = rest.astype(BF16)
                        low = (rest - mid.astype(F32)).astype(BF16)
                        acc = acc + ((_nn(hi, sel) + _nn(mid, sel)) + _nn(low, sel))
                block(o_ref, c)[...] = acc.astype(o_ref.dtype)

    def out_shape(x, m, stack):
        rows = x.shape[-2]
        return (rows, len(m)) if stack is None else (stack, rows, len(m) // stack)

    return _pallas(
        body, name=name,
        out_shape=[jax.ShapeDtypeStruct(out_shape(x, m, st), x.dtype) for x, m, st in zip(xs, maps, stacks)],
        compiler_params=_params(vmem=VMEM_MID),
    )(*xs, *[jnp.asarray(m).reshape(1, -1) for m in maps])


def _pad_cols(w):
    return jnp.pad(w, ((0, 0), (0, _round_up(w.shape[1], LANE) - w.shape[1])))


def _in_stack(cols, width):
    cols = np.asarray(cols)
    return np.where(cols < 0, -1, cols // width * _round_up(width, LANE) + cols % width).astype(np.int32)


def _ext_weights(g_in, g_uq, g_ukv, g_out):
    src_in, src_uq, src_ukv = _column_maps()
    w_uq = g_uq.reshape(B_Q_RANK, B_HEADS * B_QK)
    w_out = g_out.reshape(D_MODEL, D_MODEL)
    w_in_ext, w_uq_pad, w_ukv_ext = _permute_cols(
        [g_in, w_uq, g_ukv], [_in_stack(src_in, SH_IN[1]), src_uq, _in_stack(src_ukv, SH_UKV[1])], [None] * 3, "lay_out_weights")
    return w_in_ext, w_uq_pad, w_ukv_ext, w_out


def _fold_grads(d_in_ext, d_uq_pad, d_ukv_ext, d_out):
    src_in, src_uq, src_ukv = _column_maps()

    def back(src, n, width):
        inv = _inverse(src, n)
        wide = _round_up(width, LANE)
        out = np.full((n // width * wide,), -1, np.int32)
        for j in range(n // width):
            out[j * wide:j * wide + width] = inv[j * width:(j + 1) * width]
        return out

    n_uq, n_ukv = B_HEADS * B_QK, B_HEADS * (B_NOPE + B_V)
    d_in, d_uq, d_ukv = _permute_cols(
        [d_in_ext, d_uq_pad, d_ukv_ext], [back(src_in, N_IN, SH_IN[1]), _inverse(src_uq, n_uq), back(src_ukv, n_ukv, SH_UKV[1])],
        [N_CHIPS, None, N_CHIPS], "fold_grads")
    return d_in, d_uq.reshape((N_CHIPS,) + SH_UQ), d_ukv, d_out.reshape((N_CHIPS,) + SH_OUT)


def kernel(x, norm_in, w_in, a_q_norm, a_k_norm, b_cq_norm, b_ckv_norm, w_uq, w_ukv, b_q_norm, b_k_norm, w_out, loss_target, m_norm_in, m_w_in, m_a_q_norm, m_a_k_norm, m_b_cq_norm, m_b_ckv_norm, m_w_uq, m_w_ukv, m_b_q_norm, m_b_k_norm, m_w_out, v_norm_in, v_w_in, v_a_q_norm, v_a_k_norm, v_b_cq_norm, v_b_ckv_norm, v_w_uq, v_w_ukv, v_b_q_norm, v_b_k_norm, v_w_out):
    s_len = x.shape[1]
    xs, ts = x[0], loss_target[0]
    tm = min(256, s_len)
    tq, tk_f = min(512, s_len // 2), min(2048, s_len // 2)
    tq_b, tk_b = min(1024, s_len // 2), min(512, s_len)
    tiles_f = min(4, s_len // tq)
    tiles_b = min(2, s_len // tk_b)

    w_in_ext, w_uq_pad, w_ukv_ext, w_out_full = _ext_weights(
        *_gather_weights((_pad_cols(w_in[0]), w_uq[0], _pad_cols(w_ukv[0]), w_out[0])))
    gains = (norm_in, _spread(a_q_norm, 1, A_DIM, 1, LAY_ROPE_A), _spread(a_k_norm, 1, A_DIM, 1, LAY_ROPE_A), b_cq_norm, b_ckv_norm,
             _spread(b_q_norm, 1, B_QK, 1, LAY_ROPE_B), _spread(b_k_norm, 1, B_QK, 1, LAY_ROPE_B))
    tabs = _rope_tables(s_len, tm)

    (xn_t, gates, pre, qbpre, kbpre, cq_t, ckv_t, qa, ka, va, qb, kb, vb) = _pre(
        xs, tabs, w_in_ext, w_uq_pad, w_ukv_ext, gains, tm)
    o_a, lse_a = _attn_fwd(qa, ka, va, A_GROUP, A_DIM, tq, tk_f, tiles_f, "attn_fwd_a")
    o_b, lse_b = _attn_fwd(qb, kb, vb, 1, B_V, tq, tk_f, tiles_f, "attn_fwd_b")
    y_t, dh, dgate, do_a, do_b, delta, loss_part = _mid(xs, ts, o_a, o_b, gates, w_out_full, min(512, s_len))

    def stat(a):
        return a.reshape(a.shape[0], s_len // tq_b, 1, tq_b)

    dqa, dka, dva = _attn_bwd(qa, ka, va, do_a, stat(lse_a), stat(delta[:A_HEADS]), A_GROUP, tq_b, tk_b, tiles_b, "attn_bwd_a")
    dqb, dkb, dvb = _attn_bwd(qb, kb, vb, do_b, stat(lse_b), stat(delta[A_HEADS:A_HEADS + B_HEADS]), 1, tq_b, tk_b,
                              tiles_b, "attn_bwd_b")
    grad_x, dproj, dqbpre, dkvb, d_small = _post(
        xs, dh, pre, qbpre, kbpre, dgate, dqa, dka, dva, dqb, dkb, dvb, loss_part, tabs,
        w_in_ext, w_uq_pad, w_ukv_ext, gains, tm)

    ts_w = min(2048, s_len)
    d_in_ext = _grad_w(xn_t, dproj, 768, min(2 * ts_w, s_len), "grad_w_in")
    d_out_full = _grad_w(y_t, dh, 512, ts_w, "grad_w_out")
    d_uq_pad, d_ukv_ext = _grad_w_pairs([(cq_t, dqbpre), (ckv_t, dkvb)], ts_w, "grad_w_mla")

    g_in_p, g_uq, g_ukv_p, g_out, g_small = _reduce_grads(_fold_grads(d_in_ext, d_uq_pad, d_ukv_ext, d_out_full), d_small)
    g_in, g_ukv = g_in_p[:, :SH_IN[1]], g_ukv_p[:, :SH_UKV[1]]
    d_in, nm_in, nv_in = (a.T for a in _adamw_rows(w_in[0].T, g_in_p.T[:SH_IN[1]], m_w_in[0].T, v_w_in[0].T, SH_IN[1] // 7))
    rest = _adamw_rest(
        [(w_uq[0], g_uq, m_w_uq[0], v_w_uq[0]), (w_ukv[0], g_ukv, m_w_ukv[0], v_w_ukv[0]),
         (w_out[0], g_out, m_w_out[0], v_w_out[0])],
        [(norm_in, m_norm_in, v_norm_in), (a_q_norm, m_a_q_norm, v_a_q_norm), (a_k_norm, m_a_k_norm, v_a_k_norm),
         (b_cq_norm, m_b_cq_norm, v_b_cq_norm), (b_ckv_norm, m_b_ckv_norm, v_b_ckv_norm),
         (b_q_norm, m_b_q_norm, v_b_q_norm), (b_k_norm, m_b_k_norm, v_b_k_norm)], g_small)
    (d_uq, nm_uq, nv_uq), (d_ukv, nm_ukv, nv_ukv), (d_out, nm_out, nv_out) = (rest[3 * i:3 * i + 3] for i in range(3))
    sm = [rest[9 + 4 * i:9 + 4 * i + 4] for i in range(7)]

    def leaves(k, p_in, p_uq, p_ukv, p_out):
        return [sm[SM_IN][k], p_in[None], sm[SM_AQ][k], sm[SM_AK][k], sm[SM_CQ][k], sm[SM_CKV][k], p_uq[None], p_ukv[None],
                sm[SM_BQ][k], sm[SM_BK][k], p_out[None]]

    return (g_small[SM_LOSS, 0], grad_x[None], *leaves(0, g_in, g_uq, g_ukv, g_out), *leaves(1, d_in, d_uq, d_ukv, d_out),
            *leaves(2, nm_in, nm_uq, nm_ukv, nm_out), *leaves(3, nv_in, nv_uq, nv_ukv, nv_out))
```

```python
import jax
import jax.numpy as jnp
import numpy as np
from jax import lax
from jax.experimental import pallas as pl
from jax.experimental.pallas import tpu as pltpu

F32 = jnp.float32
BF16 = jnp.bfloat16
MESH = pl.DeviceIdType.MESH

D_MODEL = 1024
GRID_W = 64
ROPE_THETA = 10000.0
EPS = 1e-6
A_HEADS, A_KV, A_DIM = 8, 2, 64
A_GROUP = A_HEADS // A_KV
B_HEADS, B_NOPE, B_ROPE, B_V = 4, 64, 32, 128
B_QK = B_NOPE + B_ROPE
B_Q_RANK, B_KV_RANK = 384, 256
N_IN = 2464
SCALE_A = 1.0 / float(np.sqrt(A_DIM))
SCALE_B = 1.0 / float(np.sqrt(B_QK))
LOG2E = float(np.log2(np.e))
LN2 = float(np.log(2.0))
ADAM_LR, ADAM_B1, ADAM_B2, ADAM_EPS, ADAM_WD, ADAM_STEP = 0.001, 0.9, 0.999, 1e-08, 0.01, 10

LANE = 128
VMEM_BYTES = 64 * 1024 * 1024
VMEM_LIMIT = VMEM_BYTES - 8 * 1024 * 1024
VMEM_MID = 48 * 1024 * 1024
VMEM_SMALL = 32 * 1024 * 1024

QA0 = 0
KA0 = QA0 + A_HEADS * LANE
VA0 = KA0 + A_KV * LANE
GA0 = VA0 + A_KV * LANE
GB0 = GA0 + A_HEADS * A_DIM
CQ0 = GB0 + B_HEADS * LANE
CKV0 = CQ0 + B_Q_RANK
KR0 = CKV0 + B_KV_RANK
N_EXT = KR0 + LANE
N_GATE = (A_HEADS + B_HEADS) * LANE
N_GATE_C = A_HEADS * A_DIM + B_HEADS * LANE
DELTA_ROWS = 16
N_PRE = KA0 + A_KV * LANE + B_Q_RANK + B_KV_RANK

ROT = LANE // 2
_QA = A_DIM // 4
_QB = B_ROPE // 4
LAY_PLAIN_A = ((0, A_DIM, 0),)
LAY_ROPE_A = ((0, _QA, 0), (2 * _QA, _QA, _QA), (_QA, _QA, ROT), (3 * _QA, _QA, ROT + _QA))
LAY_KR = ((0, _QB, 0), (2 * _QB, _QB, _QB), (_QB, _QB, ROT), (3 * _QB, _QB, ROT + _QB))
LAY_NOPE = ((0, B_NOPE // 2, 2 * _QB), (B_NOPE // 2, B_NOPE // 2, ROT + 2 * _QB))
LAY_ROPE_B = LAY_NOPE + tuple((B_NOPE + a, n, at) for a, n, at in LAY_KR)

N_CHIPS = 4
SH_IN = (D_MODEL, N_IN // N_CHIPS)
SH_UQ = (B_Q_RANK // N_CHIPS, B_HEADS * B_QK)
SH_UKV = (B_KV_RANK, B_HEADS * (B_NOPE + B_V) // N_CHIPS)
SH_OUT = (D_MODEL // N_CHIPS, D_MODEL)
SM_ROWS, SM_W = 16, D_MODEL
SM_IN, SM_AQ, SM_AK, SM_CQ, SM_CKV, SM_BQ, SM_BK, SM_LOSS = range(8)
F32_ROWS, BF16_ROWS = 8, 16


def _pallas(body, **kw):
    return pl.pallas_call(body, **kw)


def _params(sem=None, vmem=None):
    return pltpu.CompilerParams(dimension_semantics=sem, vmem_limit_bytes=vmem)


def _rms_fwd(x, g, n):
    r = lax.rsqrt(jnp.sum(x * x, axis=-1, keepdims=True) * (1.0 / n) + EPS)
    return x * r * g


def _rms_bwd(dy, x, g, n):
    u = dy * g
    r = lax.rsqrt(jnp.sum(x * x, axis=-1, keepdims=True) * (1.0 / n) + EPS)
    ux = jnp.sum(u * x, axis=-1, keepdims=True)
    xhat = x * r
    dx = r * (u - xhat * (r * ux * (1.0 / n)))
    return dx, dy * xhat


def _rope_fwd(y, cos, sin):
    return y * cos + pltpu.roll(y, ROT, 1) * sin


def _rope_bwd(d, cos, sin):
    return d * cos - pltpu.roll(d, ROT, 1) * sin


def _token_tables(refs):
    out = []
    for r_ref, c_ref in zip(refs[0::2], refs[1::2]):
        r, c = r_ref[...], c_ref[...]
        out.append(jnp.concatenate([r[k:k + 1, :] + c for k in range(r.shape[0])], axis=0))
    return out


def _lanes_of(lane, layout):
    m = None
    for _, n, at in layout:
        seg = (lane >= at) & (lane < at + n)
        m = seg if m is None else (m | seg)
    return m


def _unspread_row(v, layout):
    v8 = jnp.broadcast_to(v, (F32_ROWS, LANE))
    lane = lax.broadcasted_iota(jnp.int32, v8.shape, 1)
    out = jnp.zeros_like(v8)
    for a, n, at in layout:
        moved = v8 if a == at else pltpu.roll(v8, (a - at) % LANE, 1)
        out = jnp.where((lane >= a) & (lane < a + n), moved, out)
    return out[0:1, :]


def _nt(a, b):
    return lax.dot_general(a, b, (((1,), (1,)), ((), ())), preferred_element_type=F32)


def _tn(a, b):
    return lax.dot_general(a, b, (((0,), (0,)), ((), ())), preferred_element_type=F32)


def _nn(a, b):
    return jnp.dot(a, b, preferred_element_type=F32)


def _block_rows(i, size):
    if isinstance(i, int):
        return pl.ds(i * size, size)
    return pl.ds(pl.multiple_of(i * size, size), size)


MAX_STATIC_BLOCKS = 32


def _three_stage(n, first, second, third):
    assert n >= 2 and n % 2 == 0
    first(0, 0)
    first(1, 1)
    second(0, 0)
    if n <= MAX_STATIC_BLOCKS:
        for i in range(1, n - 1):
            first(i + 1, (i + 1) % 2)
            second(i, i % 2)
            third(i - 1, (i - 1) % 2)
    else:
        def pair(t, carry):
            i = 2 * t + 1
            first(i + 1, 0)
            second(i, 1)
            third(i - 1, 0)
            first(i + 2, 1)
            second(i + 1, 0)
            third(i, 1)
            return carry

        lax.fori_loop(0, (n - 2) // 2, pair, 0)
    second(n - 1, 1)
    third(n - 2, 0)
    third(n - 1, 1)


def _full(shape):
    return pl.BlockSpec(shape, lambda *_: (0,) * len(shape))


def _table_specs(tm):
    return [pl.BlockSpec((None, tm // GRID_W, LANE), lambda i: (i, 0, 0)), _full((GRID_W, LANE))] * 4


def _resident(shape):
    return pl.BlockSpec(shape, lambda *_: (0,) * len(shape), pipeline_mode=pl.Buffered(1))


def _gather_weights(shards):
    n = len(shards)
    halves = [w.shape[0] // 2 for w in shards]

    def body(*refs):
        w_refs, out_refs, (send_sems, recv_sems) = refs[:n], refs[n:2 * n], refs[2 * n:]
        x, y, c = lax.axis_index("x"), lax.axis_index("y"), lax.axis_index("c")
        sibling = (x, y, 1 - c)
        chips = [(1 - x, y), (x, 1 - y), (1 - x, 1 - y)]
        me = 2 * x + y

        def copy(a, k, j, hc, to):
            part = out_refs[a].at[j, pl.ds(pl.multiple_of(hc * halves[a], BF16_ROWS), halves[a]), :]
            return pltpu.make_async_remote_copy(
                src_ref=part, dst_ref=part, send_sem=send_sems.at[6 * a + k], recv_sem=recv_sems.at[6 * a + k],
                device_id=to, device_id_type=MESH)

        started = []
        for a in range(n):
            out_refs[a][me] = w_refs[a][...].astype(BF16)
            for k, chip in enumerate(chips):
                started.append(copy(a, k, me, c, (*chip, c)))
                started[-1].start()
        for k, chip in enumerate(chips):
            for a in range(n):
                copy(a, k, 2 * chip[0] + chip[1], c, (*chip, c)).wait_recv()
                started.append(copy(a, 3 + k, 2 * chip[0] + chip[1], c, sibling))
                started[-1].start()
        for k, chip in enumerate(chips):
            for a in range(n):
                copy(a, 3 + k, 2 * chip[0] + chip[1], 1 - c, sibling).wait_recv()
        for cp in started:
            cp.wait_send()

    return _pallas(
        body, name="gather_weights",
        out_shape=[jax.ShapeDtypeStruct((N_CHIPS,) + w.shape, BF16) for w in shards],
        in_specs=[pl.BlockSpec(memory_space=pltpu.VMEM)] * n,
        out_specs=[pl.BlockSpec(memory_space=pltpu.VMEM)] * n,
        scratch_shapes=[pltpu.SemaphoreType.DMA((6 * n,)), pltpu.SemaphoreType.DMA((6 * n,))],
        compiler_params=_params(vmem=VMEM_SMALL),
    )(*shards)


def _reduce_grads(parts, small):
    n_big = len(parts)
    n = n_big + 1
    shapes = [p.shape[1:] for p in parts] + [small.shape]
    halves = [sh[0] // 2 for sh in shapes]

    def body(*refs):
        p_refs, out_refs, rec_a, rec_b = refs[:n], refs[n:2 * n], refs[2 * n:3 * n], refs[3 * n:4 * n]
        send_b = refs[4 * n:4 * n + n_big]
        sa_send, sa_recv, sb_send, sb_recv, sc_send, sc_recv = refs[4 * n + n_big:]
        x, y, c = lax.axis_index("x"), lax.axis_index("y"), lax.axis_index("c")
        sibling = (x, y, 1 - c)
        me = 2 * x + y

        def rows(a, hc):
            return pl.ds(pl.multiple_of(hc * halves[a], F32_ROWS), halves[a])

        def partial(a, j, hc):
            return p_refs[a].at[j, rows(a, hc), :] if a < n_big else p_refs[a].at[rows(a, hc), :]

        def copy_a(a, j):
            return pltpu.make_async_remote_copy(
                src_ref=partial(a, j, 1 - c), dst_ref=rec_a[a].at[j],
                send_sem=sa_send.at[N_CHIPS * a + j], recv_sem=sa_recv.at[N_CHIPS * a + j],
                device_id=sibling, device_id_type=MESH)

        def copy_b(a, r):
            j = me ^ r
            k = (N_CHIPS - 1) * a + r - 1
            return pltpu.make_async_remote_copy(
                src_ref=(send_b[a] if a < n_big else rec_a[a]).at[j], dst_ref=rec_b[a].at[r],
                send_sem=sb_send.at[k], recv_sem=sb_recv.at[k], device_id=(j // 2, j % 2, c), device_id_type=MESH)

        def copy_c(a):
            return pltpu.make_async_remote_copy(
                src_ref=out_refs[a].at[rows(a, c), :], dst_ref=out_refs[a].at[rows(a, c), :],
                send_sem=sc_send.at[a], recv_sem=sc_recv.at[a], device_id=sibling, device_id_type=MESH)

        for a in range(n):
            for j in range(N_CHIPS):
                copy_a(a, j).start()
        for r in range(1, N_CHIPS):
            j = me ^ r
            for a in range(n):
                copy_a(a, j).wait_recv()
                chip_part = rec_a[a][j] + partial(a, j, c)[...]
                if a < n_big:
                    send_b[a][j] = chip_part.astype(BF16)
                else:
                    rec_a[a][j] = chip_part
                copy_b(a, r).start()
        for a in range(n):
            copy_a(a, me).wait_recv()
            rec_b[a][0] = (rec_a[a][me] + partial(a, me, c)[...]).astype(rec_b[a].dtype)
        for a in range(n):
            for r in range(1, N_CHIPS):
                copy_b(a, r).wait_recv()
            total = rec_b[a][me].astype(F32)
            for j in range(1, N_CHIPS):
                total = total + rec_b[a][j ^ me].astype(F32)
            out_refs[a][rows(a, c), :] = total
            copy_c(a).start()
        for a in range(n):
            copy_c(a).wait_recv()
        for a in range(n):
            for j in range(N_CHIPS):
                copy_a(a, j).wait_send()
            for r in range(1, N_CHIPS):
                copy_b(a, r).wait_send()
            copy_c(a).wait_send()

    dma = pltpu.SemaphoreType.DMA
    return _pallas(
        body, name="reduce_grads",
        out_shape=[jax.ShapeDtypeStruct(sh, F32) for sh in shapes],
        in_specs=[pl.BlockSpec(memory_space=pltpu.VMEM)] * n,
        out_specs=[pl.BlockSpec(memory_space=pltpu.VMEM)] * n,
        scratch_shapes=[pltpu.VMEM((N_CHIPS, h) + sh[1:], F32) for h, sh in zip(halves, shapes)]
                       + [pltpu.VMEM((N_CHIPS, h) + sh[1:], BF16 if a < n_big else F32)
                          for a, (h, sh) in enumerate(zip(halves, shapes))]
                       + [pltpu.VMEM((N_CHIPS, h) + sh[1:], BF16) for h, sh in zip(halves[:n_big], shapes[:n_big])]
                       + [dma((N_CHIPS * n,)), dma((N_CHIPS * n,)), dma(((N_CHIPS - 1) * n,)), dma(((N_CHIPS - 1) * n,)),
                          dma((n,)), dma((n,))],
        compiler_params=_params(vmem=VMEM_LIMIT),
    )(*parts, small)


def _pre(x, tabs, w_in_ext, w_uq_pad, w_ukv_ext, gains, tm):
    s_len = x.shape[0]
    nt = s_len // tm

    def body(x_ref, car_ref, cac_ref, sar_ref, sac_ref, cbr_ref, cbc_ref, sbr_ref, sbc_ref, win_ref, wuq_ref, wukv_ref,
             gin_ref, gaq_ref, gak_ref, gcq_ref, gckv_ref, gbq_ref, gbk_ref,
             xn_ref, gates_ref, pre_ref, qbpre_ref, kbpre_ref, cq_ref, ckv_ref,
             qa_ref, ka_ref, va_ref, qb_ref, kb_ref, vb_ref, proj):
        xn = _rms_fwd(x_ref[...], gin_ref[...], D_MODEL)
        xn_ref[...] = jnp.transpose(xn).astype(BF16)
        xb = xn.astype(BF16)
        pre_ref[:, 0:VA0] = _nn(xb, win_ref[:, 0:VA0])
        gates_ref[...] = _nn(xb, win_ref[:, GA0:GA0 + N_GATE_C])
        pre_ref[:, VA0:N_PRE] = _nn(xb, win_ref[:, CQ0:KR0])
        proj[...] = _nn(xb, win_ref[:, VA0:GA0])
        kr = _nn(xb, win_ref[:, KR0:N_EXT])
        ca, sa, cb, sb = _token_tables((car_ref, cac_ref, sar_ref, sac_ref, cbr_ref, cbc_ref, sbr_ref, sbc_ref))
        lane = lax.broadcasted_iota(jnp.int32, (tm, LANE), 1)
        for h in range(A_HEADS):
            yq = _rms_fwd(pre_ref[:, QA0 + LANE * h:QA0 + LANE * (h + 1)], gaq_ref[...], A_DIM)
            qa_ref[h] = (_rope_fwd(yq, ca, sa) * (SCALE_A * LOG2E)).astype(BF16)
        for h in range(A_KV):
            yk = _rms_fwd(pre_ref[:, KA0 + LANE * h:KA0 + LANE * (h + 1)], gak_ref[...], A_DIM)
            ka_ref[h] = _rope_fwd(yk, ca, sa).astype(BF16)
            va_ref[h] = jnp.where(lane == A_DIM, 1.0, proj[:, LANE * h:LANE * (h + 1)]).astype(BF16)
        cq = _rms_fwd(pre_ref[:, VA0:VA0 + B_Q_RANK], gcq_ref[...], B_Q_RANK)
        cq_ref[...] = jnp.transpose(cq).astype(BF16)
        qbpre_ref[...] = _nn(cq.astype(BF16), wuq_ref[...])
        ckv = _rms_fwd(pre_ref[:, VA0 + B_Q_RANK:N_PRE], gckv_ref[...], B_KV_RANK)
        ckv_ref[...] = jnp.transpose(ckv).astype(BF16)
        kvb = _nn(ckv.astype(BF16), wukv_ref[...])
        for h in range(B_HEADS):
            yq = _rms_fwd(qbpre_ref[:, LANE * h:LANE * (h + 1)], gbq_ref[...], B_QK)
            qb_ref[h] = (_rope_fwd(yq, cb, sb) * (SCALE_B * LOG2E)).astype(BF16)
            kp = kvb[:, LANE * h:LANE * (h + 1)] + kr
            kbpre_ref[:, LANE * h:LANE * (h + 1)] = kp
            kb_ref[h] = _rope_fwd(_rms_fwd(kp, gbk_ref[...], B_QK), cb, sb).astype(BF16)
            vb_ref[h, :, 0:LANE] = kvb[:, B_HEADS * LANE + LANE * h:B_HEADS * LANE + LANE * (h + 1)].astype(BF16)
            vb_ref[h, :, LANE:2 * LANE] = jnp.where(lane == 0, 1.0, 0.0).astype(BF16)

    row = lambda w: pl.BlockSpec((tm, w), lambda i: (i, 0))
    col = lambda w: pl.BlockSpec((w, tm), lambda i: (0, i))
    heads = lambda n: pl.BlockSpec((n, tm, LANE), lambda i: (0, i, 0))
    hs = lambda n: jax.ShapeDtypeStruct((n, s_len, LANE), BF16)
    return _pallas(
        body, name="pre", grid=(nt,),
        in_specs=[row(D_MODEL)] + _table_specs(tm)
                 + [_resident(w_in_ext.shape), _resident(w_uq_pad.shape), _resident(w_ukv_ext.shape)]
                 + [_full(g.shape) for g in gains],
        out_specs=[col(D_MODEL), row(N_GATE_C), row(N_PRE), row(B_HEADS * LANE), row(B_HEADS * LANE),
                   col(B_Q_RANK), col(B_KV_RANK),
                   heads(A_HEADS), heads(A_KV), heads(A_KV), heads(B_HEADS), heads(B_HEADS),
                   pl.BlockSpec((B_HEADS, tm, 2 * LANE), lambda i: (0, i, 0))],
        out_shape=[jax.ShapeDtypeStruct((D_MODEL, s_len), BF16), jax.ShapeDtypeStruct((s_len, N_GATE_C), F32),
                   jax.ShapeDtypeStruct((s_len, N_PRE), F32), jax.ShapeDtypeStruct((s_len, B_HEADS * LANE), F32),
                   jax.ShapeDtypeStruct((s_len, B_HEADS * LANE), F32),
                   jax.ShapeDtypeStruct((B_Q_RANK, s_len), BF16), jax.ShapeDtypeStruct((B_KV_RANK, s_len), BF16),
                   hs(A_HEADS), hs(A_KV), hs(A_KV), hs(B_HEADS), hs(B_HEADS),
                   jax.ShapeDtypeStruct((B_HEADS, s_len, 2 * LANE), BF16)],
        scratch_shapes=[pltpu.VMEM((tm, A_KV * LANE), F32)],
        compiler_params=_params(("parallel",), VMEM_LIMIT),
    )(x, *tabs, w_in_ext, w_uq_pad, w_ukv_ext, *gains)


def _attn_fwd(q, k, v, group, l_col, tq, tk, tiles, name):
    n_heads, s_len, _ = q.shape
    v_w = v.shape[2]
    nk = s_len // tk

    def body(q_ref, k_ref, v_ref, o_ref, lse_ref, s_buf, p_buf, a_buf, m_ref, acc_ref):
        def scores(g, slot):
            s_buf[slot] = _nt(q_ref[_block_rows(g // nk, tq), :], k_ref[_block_rows(g % nk, tk), :])

        def softmax(g, slot):
            t = g // nk
            s = s_buf[slot]
            m_old = m_ref[t]
            m_new = jnp.maximum(m_old, jnp.max(s, axis=-1, keepdims=True))
            m_ref[t] = m_new
            a_buf[slot] = jnp.exp2(m_old - m_new)
            p_buf[slot] = jnp.exp2(s - jnp.tile(m_new, (1, tk // LANE))).astype(BF16)

        def values(g, slot):
            t = g // nk
            pv = _nn(p_buf[slot], v_ref[_block_rows(g % nk, tk), :])
            for c in range(0, v_w, LANE):
                acc_ref[t, :, c:c + LANE] = a_buf[slot] * acc_ref[t, :, c:c + LANE] + pv[:, c:c + LANE]

        m_ref[...] = jnp.full(m_ref.shape, -1e30, F32)
        acc_ref[...] = jnp.zeros(acc_ref.shape, F32)
        _three_stage(tiles * nk, scores, softmax, values)
        for t in range(tiles):
            l = acc_ref[t, :, l_col:l_col + 1]
            o = acc_ref[t, :, 0:LANE] * (1.0 / l)
            if l_col < LANE:
                lane = lax.broadcasted_iota(jnp.int32, o.shape, 1)
                o = jnp.where(lane == l_col, 0.0, o)
            o_ref[t * tq:(t + 1) * tq, :] = o
            lse_ref[t] = jnp.transpose(m_ref[t] + jnp.log2(jnp.broadcast_to(l, (tq, LANE))))[0:1, :]

    return _pallas(
        body, name=name, grid=(n_heads, s_len // (tiles * tq)),
        in_specs=[pl.BlockSpec((None, tiles * tq, LANE), lambda h, i: (h, i, 0)),
                  pl.BlockSpec((None, s_len, LANE), lambda h, i: (h // group, 0, 0)),
                  pl.BlockSpec((None, s_len, v_w), lambda h, i: (h // group, 0, 0))],
        out_specs=[pl.BlockSpec((None, tiles * tq, LANE), lambda h, i: (h, i, 0)),
                   pl.BlockSpec((None, tiles, 1, tq), lambda h, i: (h, i, 0, 0))],
        out_shape=[jax.ShapeDtypeStruct((n_heads, s_len, LANE), F32),
                   jax.ShapeDtypeStruct((n_heads, s_len // tq, 1, tq), F32)],
        scratch_shapes=[pltpu.VMEM((2, tq, tk), F32), pltpu.VMEM((2, tq, tk), BF16), pltpu.VMEM((2, tq, LANE), F32),
                        pltpu.VMEM((tiles, tq, LANE), F32), pltpu.VMEM((tiles, tq, v_w), F32)],
        compiler_params=_params(("parallel", "parallel"), VMEM_MID),
    )(q, k, v)


def _mid(x, target, o_a, o_b, gates, w_out, tm):
    s_len = x.shape[0]
    nt = s_len // tm
    n_heads = A_HEADS + B_HEADS
    d_mix = w_out.shape[0]
    pairs = A_HEADS // 2

    def body(x_ref, t_ref, oa_ref, ob_ref, g_ref, w_ref,
             yt_ref, dh_ref, dgate_ref, doa_ref, dob_ref, delta_ref, loss_ref, silu_scr, dsilu_scr, y_ref):
        @pl.when(pl.program_id(0) == 0)
        def _():
            loss_ref[...] = jnp.zeros_like(loss_ref)

        def o_of(h):
            return oa_ref[h] if h < A_HEADS else ob_ref[h - A_HEADS]

        lane = lax.broadcasted_iota(jnp.int32, (tm, LANE), 1)

        def gated(h):
            cols = slice(LANE * h, LANE * (h + 1))
            if h < A_HEADS:
                packed = g_ref[:, LANE * (h // 2):LANE * (h // 2 + 1)]
                g = jnp.where(lane < A_DIM, packed if h % 2 == 0 else pltpu.roll(packed, ROT, 1), 0.0)
            else:
                g = g_ref[:, LANE * (pairs + h - A_HEADS):LANE * (pairs + h - A_HEADS + 1)]
            sig = 1.0 / (1.0 + jnp.exp(-g))
            silu = g * sig
            silu_scr[:, cols] = silu
            dsilu_scr[:, cols] = sig * (1.0 + g * (1.0 - sig))
            return o_of(h) * silu

        for c in range(pairs + B_HEADS):
            y = gated(2 * c) + pltpu.roll(gated(2 * c + 1), ROT, 1) if c < pairs else gated(A_HEADS + c - pairs)
            cols = slice(LANE * c, LANE * (c + 1))
            y_ref[:, cols] = y.astype(BF16)
            yt_ref[cols, :] = jnp.transpose(y).astype(BF16)
        err = x_ref[...] + _nn(y_ref[...], w_ref[...]) - t_ref[...]
        sq = jnp.sum(jnp.sum(err * err, axis=-1, keepdims=True), axis=0, keepdims=True)
        loss_ref[...] += jnp.broadcast_to(sq * (0.5 / D_MODEL), loss_ref.shape)
        dh = err * (1.0 / D_MODEL)
        dh_ref[...] = dh
        dy = _nt(dh.astype(BF16), w_ref[...])
        delta = jnp.zeros((tm, LANE), F32)
        held = None
        for h in range(n_heads):
            cols = slice(LANE * h, LANE * (h + 1))
            if h < A_HEADS:
                packed = dy[:, LANE * (h // 2):LANE * (h // 2 + 1)]
                dyh = packed if h % 2 == 0 else pltpu.roll(packed, ROT, 1)
            else:
                dyh = dy[:, LANE * (pairs + h - A_HEADS):LANE * (pairs + h - A_HEADS + 1)]
            oh = o_of(h)
            do = dyh * silu_scr[:, cols]
            dg = dyh * oh * dsilu_scr[:, cols]
            if h >= A_HEADS:
                dgate_ref[:, LANE * (pairs + h - A_HEADS):LANE * (pairs + h - A_HEADS + 1)] = dg.astype(BF16)
            elif h % 2 == 0:
                held = dg
            else:
                dgate_ref[:, LANE * (h // 2):LANE * (h // 2 + 1)] = (held + pltpu.roll(dg, ROT, 1)).astype(BF16)
            delta = jnp.where(lane == h, jnp.sum(do * oh, axis=-1, keepdims=True), delta)
            if h < A_HEADS:
                doa_ref[h] = do.astype(BF16)
            else:
                dob_ref[h - A_HEADS] = do.astype(BF16)
        delta_ref[...] = jnp.transpose(delta)[0:DELTA_ROWS, :]

    row = lambda w: pl.BlockSpec((tm, w), lambda i: (i, 0))
    heads = lambda n, w=LANE: pl.BlockSpec((n, tm, w), lambda i: (0, i, 0))
    return _pallas(
        body, name="mid", grid=(nt,),
        in_specs=[row(D_MODEL), row(D_MODEL), heads(A_HEADS), heads(B_HEADS), row(N_GATE_C), _resident(w_out.shape)],
        out_specs=[pl.BlockSpec((d_mix, tm), lambda i: (0, i)), row(D_MODEL), row(N_GATE_C), heads(A_HEADS), heads(B_HEADS),
                   pl.BlockSpec((DELTA_ROWS, tm), lambda i: (0, i)),
                   _full((8, LANE))],
        out_shape=[jax.ShapeDtypeStruct((d_mix, s_len), BF16), jax.ShapeDtypeStruct((s_len, D_MODEL), F32),
                   jax.ShapeDtypeStruct((s_len, N_GATE_C), BF16),
                   jax.ShapeDtypeStruct((A_HEADS, s_len, LANE), BF16), jax.ShapeDtypeStruct((B_HEADS, s_len, LANE), BF16),
                   jax.ShapeDtypeStruct((DELTA_ROWS, s_len), F32), jax.ShapeDtypeStruct((8, LANE), F32)],
        scratch_shapes=[pltpu.VMEM((tm, N_GATE), F32), pltpu.VMEM((tm, N_GATE), F32), pltpu.VMEM((tm, d_mix), BF16)],
        compiler_params=_params(("arbitrary",), VMEM_LIMIT),
    )(x, target, o_a, o_b, gates, w_out)


def _attn_bwd(q, k, v, do, lse, delta, group, tq, tk, tiles, name):
    n_heads, s_len, _ = q.shape
    nq = s_len // tq

    def body(q_ref, do_ref, lse_ref, delta_ref, k_ref, v_ref, dq_ref, dk_ref, dv_ref, s_buf, dp_buf, p_buf, ds_buf):
        @pl.when(pl.program_id(1) == 0)
        def _():
            dq_ref[...] = jnp.zeros_like(dq_ref)

        dk_ref[...] = jnp.zeros_like(dk_ref)
        dv_ref[...] = jnp.zeros_like(dv_ref)

        def keys(g):
            return _block_rows(g // nq, tk)

        def queries(g):
            return _block_rows(g % nq, tq)

        def scores(g, slot):
            s_buf[slot] = _nt(k_ref[keys(g), :], q_ref[queries(g), :])
            dp_buf[slot] = _nt(v_ref[keys(g), :], do_ref[queries(g), :])

        def elementwise(g, slot):
            p = jnp.exp2(s_buf[slot] - lse_ref[g % nq])
            p_buf[slot] = p.astype(BF16)
            ds_buf[slot] = (p * (dp_buf[slot] - delta_ref[g % nq])).astype(BF16)

        def grads(g, slot):
            dv_ref[keys(g), :] += _nn(p_buf[slot], do_ref[queries(g), :])
            dk_ref[keys(g), :] += _nn(ds_buf[slot], q_ref[queries(g), :])
            dq_ref[queries(g), :] += _tn(ds_buf[slot], k_ref[keys(g), :])

        _three_stage(tiles * nq, scores, elementwise, grads)

    whole = lambda: pl.BlockSpec((None, s_len, LANE), lambda h, j: (h, 0, 0))
    stat = lambda: pl.BlockSpec((None, nq, 1, tq), lambda h, j: (h, 0, 0, 0))
    kvb = lambda: pl.BlockSpec((None, tiles * tk, LANE), lambda h, j: (h // group, j, 0))
    outb = lambda: pl.BlockSpec((None, tiles * tk, LANE), lambda h, j: (h, j, 0))
    shape = jax.ShapeDtypeStruct((n_heads, s_len, LANE), F32)
    return _pallas(
        body, name=name, grid=(n_heads, s_len // (tiles * tk)),
        in_specs=[whole(), whole(), stat(), stat(), kvb(), kvb()],
        out_specs=[whole(), outb(), outb()],
        out_shape=[shape, shape, shape],
        scratch_shapes=[pltpu.VMEM((2, tk, tq), F32), pltpu.VMEM((2, tk, tq), F32),
                        pltpu.VMEM((2, tk, tq), BF16), pltpu.VMEM((2, tk, tq), BF16)],
        compiler_params=_params(("parallel", "arbitrary"), VMEM_MID),
    )(q, do, lse, delta, k, v)


def _post(x, dh, pre, qbpre, kbpre, dgate, dqa, dka, dva, dqb, dkb, dvb, loss_part, tabs,
          w_in_ext, w_uq_pad, w_ukv_ext, gains, tm):
    s_len = x.shape[0]
    nt = s_len // tm

    def body(x_ref, dh_ref, pre_ref, qbpre_ref, kbpre_ref, dgate_ref,
             dqa_ref, dka_ref, dva_ref, dqb_ref, dkb_ref, dvb_ref, loss_ref,
             car_ref, cac_ref, sar_ref, sac_ref, cbr_ref, cbc_ref, sbr_ref, sbc_ref, win_ref, wuq_ref, wukv_ref,
             gin_ref, gaq_ref, gak_ref, gcq_ref, gckv_ref, gbq_ref, gbk_ref,
             gx_ref, dproj_ref, dqbpre_ref, dkvb_ref, dsm_ref):
        @pl.when(pl.program_id(0) == 0)
        def _():
            dsm_ref[...] = jnp.zeros_like(dsm_ref)
            dsm_ref[SM_LOSS:SM_LOSS + 1, 0:LANE] = loss_ref[0:1, :]

        def add_small(r, dg):
            dsm_ref[r:r + 1, 0:dg.shape[1]] += dg

        def tok_sum(a):
            return jnp.sum(a, axis=0, keepdims=True)

        ca, sa, cb, sb = _token_tables((car_ref, cac_ref, sar_ref, sac_ref, cbr_ref, cbc_ref, sbr_ref, sbc_ref))
        lane = lax.broadcasted_iota(jnp.int32, (tm, LANE), 1)

        nope_lanes = _lanes_of(lane, LAY_NOPE)

        def back(c0, c1):
            return _nt(dproj_ref[:, c0:c1], win_ref[:, c0:c1])

        dproj_ref[:, GA0:GA0 + N_GATE_C] = dgate_ref[...]
        dxn = back(GA0, GA0 + N_GATE_C)
        dg = jnp.zeros((1, LANE), F32)
        for h in range(A_HEADS):
            dn = _rope_bwd(dqa_ref[h] * SCALE_A, ca, sa)
            dx, dgr = _rms_bwd(dn, pre_ref[:, QA0 + LANE * h:QA0 + LANE * (h + 1)], gaq_ref[...], A_DIM)
            dproj_ref[:, QA0 + LANE * h:QA0 + LANE * (h + 1)] = dx.astype(BF16)
            dg = dg + tok_sum(dgr)
        add_small(SM_AQ, _unspread_row(dg, LAY_ROPE_A))
        dxn = dxn + back(QA0, KA0)
        dg = jnp.zeros((1, LANE), F32)
        for h in range(A_KV):
            dk = dka_ref[A_GROUP * h]
            dv = dva_ref[A_GROUP * h]
            for g in range(1, A_GROUP):
                dk = dk + dka_ref[A_GROUP * h + g]
                dv = dv + dva_ref[A_GROUP * h + g]
            dn = _rope_bwd(dk * LN2, ca, sa)
            dx, dgr = _rms_bwd(dn, pre_ref[:, KA0 + LANE * h:KA0 + LANE * (h + 1)], gak_ref[...], A_DIM)
            dproj_ref[:, KA0 + LANE * h:KA0 + LANE * (h + 1)] = dx.astype(BF16)
            dproj_ref[:, VA0 + LANE * h:VA0 + LANE * (h + 1)] = dv.astype(BF16)
            dg = dg + tok_sum(dgr)
        add_small(SM_AK, _unspread_row(dg, LAY_ROPE_A))
        dxn = dxn + back(KA0, GA0)
        dg = jnp.zeros((1, LANE), F32)
        for h in range(B_HEADS):
            cols = slice(LANE * h, LANE * (h + 1))
            dn = _rope_bwd(dqb_ref[h] * SCALE_B, cb, sb)
            dx, dgr = _rms_bwd(dn, qbpre_ref[:, cols], gbq_ref[...], B_QK)
            dqbpre_ref[:, cols] = dx.astype(BF16)
            dg = dg + tok_sum(dgr)
        add_small(SM_BQ, _unspread_row(dg, LAY_ROPE_B))
        dcq = _nt(dqbpre_ref[...], wuq_ref[...])
        dx, dgr = _rms_bwd(dcq, pre_ref[:, VA0:VA0 + B_Q_RANK], gcq_ref[...], B_Q_RANK)
        dproj_ref[:, CQ0:CQ0 + B_Q_RANK] = dx.astype(BF16)
        add_small(SM_CQ, tok_sum(dgr))
        dxn = dxn + back(CQ0, CKV0)
        dg = jnp.zeros((1, LANE), F32)
        dkr = jnp.zeros((tm, LANE), F32)
        for h in range(B_HEADS):
            cols = slice(LANE * h, LANE * (h + 1))
            dn = _rope_bwd(dkb_ref[h] * LN2, cb, sb)
            dx, dgr = _rms_bwd(dn, kbpre_ref[:, cols], gbk_ref[...], B_QK)
            dkvb_ref[:, cols] = jnp.where(nope_lanes, dx, 0.0).astype(BF16)
            dkvb_ref[:, B_HEADS * LANE + LANE * h:B_HEADS * LANE + LANE * (h + 1)] = dvb_ref[h].astype(BF16)
            dkr = dkr + dx
            dg = dg + tok_sum(dgr)
        add_small(SM_BK, _unspread_row(dg, LAY_ROPE_B))
        dproj_ref[:, KR0:KR0 + LANE] = jnp.where(_lanes_of(lane, LAY_KR), dkr, 0.0).astype(BF16)
        dckv = _nt(dkvb_ref[...], wukv_ref[...])
        dx, dgr = _rms_bwd(dckv, pre_ref[:, VA0 + B_Q_RANK:N_PRE], gckv_ref[...], B_KV_RANK)
        dproj_ref[:, CKV0:CKV0 + B_KV_RANK] = dx.astype(BF16)
        add_small(SM_CKV, tok_sum(dgr))
        dxn = dxn + back(CKV0, N_EXT)
        dx, dgr = _rms_bwd(dxn, x_ref[...], gin_ref[...], D_MODEL)
        gx_ref[...] = dh_ref[...] + dx
        add_small(SM_IN, tok_sum(dgr))

    row = lambda w: pl.BlockSpec((tm, w), lambda i: (i, 0))
    heads = lambda n: pl.BlockSpec((n, tm, LANE), lambda i: (0, i, 0))
    return _pallas(
        body, name="post", grid=(nt,),
        in_specs=[row(D_MODEL), row(D_MODEL), row(N_PRE), row(B_HEADS * LANE), row(B_HEADS * LANE), row(N_GATE_C),
                  heads(A_HEADS), heads(A_HEADS), heads(A_HEADS), heads(B_HEADS), heads(B_HEADS), heads(B_HEADS),
                  _full(loss_part.shape)] + _table_specs(tm)
                 + [_resident(w_in_ext.shape), _resident(w_uq_pad.shape), _resident(w_ukv_ext.shape)]
                 + [_full(g.shape) for g in gains],
        out_specs=[row(D_MODEL), row(N_EXT), row(B_HEADS * LANE), row(2 * B_HEADS * LANE), _full((SM_ROWS, SM_W))],
        out_shape=[jax.ShapeDtypeStruct((s_len, D_MODEL), F32), jax.ShapeDtypeStruct((s_len, N_EXT), BF16),
                   jax.ShapeDtypeStruct((s_len, B_HEADS * LANE), BF16),
                   jax.ShapeDtypeStruct((s_len, 2 * B_HEADS * LANE), BF16),
                   jax.ShapeDtypeStruct((SM_ROWS, SM_W), F32)],
        compiler_params=_params(("arbitrary",), VMEM_LIMIT),
    )(x, dh, pre, qbpre, kbpre, dgate, dqa, dka, dva, dqb, dkb, dvb, loss_part, *tabs,
      w_in_ext, w_uq_pad, w_ukv_ext, *gains)


def _grad_w(a_t, b, tn, ts, name):
    m, s_len = a_t.shape
    n = b.shape[1]

    def body(a_ref, b_ref, o_ref):
        @pl.when(pl.program_id(1) == 0)
        def _():
            o_ref[...] = jnp.zeros_like(o_ref)

        o_ref[...] += _nn(a_ref[...], b_ref[...].astype(BF16))

    return _pallas(
        body, name=name, grid=(n // tn, s_len // ts),
        in_specs=[pl.BlockSpec((m, ts), lambda j, t: (0, t)), pl.BlockSpec((ts, tn), lambda j, t: (t, j))],
        out_specs=pl.BlockSpec((m, tn), lambda j, t: (0, j)),
        out_shape=jax.ShapeDtypeStruct((m, n), F32),
        compiler_params=_params(("parallel", "arbitrary"), VMEM_MID),
    )(a_t, b)


def _grad_w_pairs(pairs, ts, name):
    s_len = pairs[0][0].shape[1]
    n_p = len(pairs)

    def body(*refs):
        for a_ref, b_ref, o_ref in zip(refs[0:2 * n_p:2], refs[1:2 * n_p:2], refs[2 * n_p:]):
            @pl.when(pl.program_id(0) == 0)
            def _():
                o_ref[...] = jnp.zeros_like(o_ref)

            o_ref[...] += _nn(a_ref[...], b_ref[...].astype(BF16))

    in_specs, flat = [], []
    for a_t, b in pairs:
        in_specs += [pl.BlockSpec((a_t.shape[0], ts), lambda t: (0, t)), pl.BlockSpec((ts, b.shape[1]), lambda t: (t, 0))]
        flat += [a_t, b]
    return _pallas(
        body, name=name, grid=(s_len // ts,),
        in_specs=in_specs,
        out_specs=[_full((a_t.shape[0], b.shape[1])) for a_t, b in pairs],
        out_shape=[jax.ShapeDtypeStruct((a_t.shape[0], b.shape[1]), F32) for a_t, b in pairs],
        compiler_params=_params(("arbitrary",), VMEM_MID),
    )(*flat)


def _adam_math(w, g, m, v):
    nm = ADAM_B1 * m + (1.0 - ADAM_B1) * g
    nv = ADAM_B2 * v + (1.0 - ADAM_B2) * (g * g)
    m_hat = nm / (1.0 - ADAM_B1 ** ADAM_STEP)
    v_hat = nv / (1.0 - ADAM_B2 ** ADAM_STEP)
    return -ADAM_LR * (m_hat / (jnp.sqrt(v_hat) + ADAM_EPS) + ADAM_WD * w), nm, nv


def _adamw_rows(w, g, m, v, tr):
    rows, cols = w.shape

    def body(w_ref, g_ref, m_ref, v_ref, d_ref, nm_ref, nv_ref):
        d_ref[...], nm_ref[...], nv_ref[...] = _adam_math(w_ref[...], g_ref[...], m_ref[...], v_ref[...])

    blk = pl.BlockSpec((tr, cols), lambda i: (i, 0))
    shape = jax.ShapeDtypeStruct((rows, cols), F32)
    return _pallas(
        body, name="adamw_w_in", grid=(rows // tr,),
        in_specs=[blk] * 4, out_specs=[blk] * 3, out_shape=[shape] * 3,
        compiler_params=_params(("parallel",), VMEM_SMALL),
    )(w, g, m, v)


def _adamw_rest(bigs, smalls, g_small):
    nb, ns = len(bigs), len(smalls)

    def body(*refs):
        ins, outs = refs[:4 * nb + 3 * ns + 1], refs[4 * nb + 3 * ns + 1:]
        for i in range(nb):
            w_ref, g_ref, m_ref, v_ref = ins[4 * i:4 * i + 4]
            d_ref, nm_ref, nv_ref = outs[3 * i:3 * i + 3]
            d_ref[...], nm_ref[...], nv_ref[...] = _adam_math(w_ref[...], g_ref[...], m_ref[...], v_ref[...])
        gs_ref = ins[-1]
        for i in range(ns):
            w_ref, m_ref, v_ref = ins[4 * nb + 3 * i:4 * nb + 3 * i + 3]
            g_ref, d_ref, nm_ref, nv_ref = outs[3 * nb + 4 * i:3 * nb + 4 * i + 4]
            g = gs_ref[i:i + 1, 0:w_ref.shape[1]]
            g_ref[...] = g
            d_ref[...], nm_ref[...], nv_ref[...] = _adam_math(w_ref[...], g, m_ref[...], v_ref[...])

    flat_in = [a for quad in bigs for a in quad] + [a for tri in smalls for a in tri] + [g_small]
    out_shape = ([jax.ShapeDtypeStruct(q[0].shape, F32) for q in bigs for _ in range(3)]
                 + [jax.ShapeDtypeStruct(t[0].shape, F32) for t in smalls for _ in range(4)])
    return _pallas(
        body, name="adamw_rest",
        in_specs=[pl.BlockSpec(memory_space=pltpu.VMEM)] * len(flat_in),
        out_specs=[pl.BlockSpec(memory_space=pltpu.VMEM)] * len(out_shape),
        out_shape=out_shape,
        compiler_params=_params(vmem=VMEM_SMALL),
    )(*flat_in)


def _place(pieces, n):
    out, at = [], 0
    for lane0, arr in sorted(pieces, key=lambda p: p[0]):
        out += [jnp.zeros((n, lane0 - at), F32), arr]
        at = lane0 + arr.shape[1]
    return jnp.concatenate(out + [jnp.zeros((n, LANE - at), F32)], axis=1)


def _rope_tables(s_len, tm):
    rows = s_len // GRID_W
    row = jnp.arange(rows, dtype=F32)
    col = jnp.arange(GRID_W, dtype=F32)

    def lay(dim, layout, first_dim, ones):
        half = dim // 2
        inv = 1.0 / (ROPE_THETA ** (jnp.arange(0, half, 2, dtype=F32) / half))
        ang_r, ang_c = row[:, None] * inv[None, :], col[:, None] * inv[None, :]
        at = {a - first_dim: lane0 for a, _, lane0 in layout}
        q = dim // 4
        r1, r2, c1, c2 = at[0], at[q], at[2 * q], at[3 * q]
        cos_r = _place([(r1, jnp.cos(ang_r)), (r2, jnp.cos(ang_r))], rows)
        sin_r = _place([(r1, -jnp.sin(ang_r)), (r2, jnp.sin(ang_r))], rows)
        cos_c = _place([(c1, jnp.cos(ang_c)), (c2, jnp.cos(ang_c))] + [(l0, jnp.ones((GRID_W, n), F32)) for _, n, l0 in ones],
                       GRID_W)
        sin_c = _place([(c1, -jnp.sin(ang_c)), (c2, jnp.sin(ang_c))], GRID_W)
        by_block = (s_len // tm, tm // GRID_W, LANE)
        return cos_r.reshape(by_block), cos_c, sin_r.reshape(by_block), sin_c

    return lay(A_DIM, LAY_ROPE_A, 0, ()) + lay(B_ROPE, LAY_KR, 0, LAY_NOPE)


def _spread(w, n_heads, dim, axis, layout):
    w3 = w.reshape(w.shape[:axis] + (n_heads, dim) + w.shape[axis + 1:])
    out, at = [], 0

    def zeros(n):
        return jnp.zeros(w3.shape[:axis + 1] + (n,) + w3.shape[axis + 2:], w.dtype)

    for a0, n, lane0 in sorted(layout, key=lambda seg: seg[2]):
        out += [zeros(lane0 - at), lax.slice_in_dim(w3, a0, a0 + n, axis=axis + 1)]
        at = lane0 + n
    out = jnp.concatenate(out + [zeros(LANE - at)], axis=axis + 1)
    return out.reshape(w.shape[:axis] + (n_heads * LANE,) + w.shape[axis + 1:])


def _head_cols(first, n_heads, dim, layout):
    out = np.full((n_heads * LANE,), -1, np.int32)
    for h in range(n_heads):
        for a0, n, lane0 in layout:
            out[h * LANE + lane0:h * LANE + lane0 + n] = first + h * dim + a0 + np.arange(n)
    return out


def _inverse(src, n):
    dst = np.full((n,), -1, np.int32)
    dst[src[src >= 0]] = np.nonzero(src >= 0)[0]
    return dst


def _column_maps():
    a_w, kv_w = A_HEADS * A_DIM, A_KV * A_DIM
    o_g = a_w + 2 * kv_w
    o_cq = o_g + a_w
    o_kr = o_cq + B_Q_RANK + B_KV_RANK
    src_in = np.concatenate([
        _head_cols(0, A_HEADS, A_DIM, LAY_ROPE_A), _head_cols(a_w, A_KV, A_DIM, LAY_ROPE_A),
        _head_cols(a_w + kv_w, A_KV, A_DIM, LAY_PLAIN_A), np.arange(o_g, o_g + a_w),
        np.arange(o_kr + B_ROPE, N_IN), np.arange(o_cq, o_kr), _head_cols(o_kr, 1, B_ROPE, LAY_KR)]).astype(np.int32)
    src_uq = _head_cols(0, B_HEADS, B_QK, LAY_ROPE_B)
    per = B_NOPE + B_V
    src_ukv = np.concatenate([_head_cols(0, B_HEADS, per, LAY_NOPE),
                              _head_cols(B_NOPE, B_HEADS, per, ((0, B_V, 0),))]).astype(np.int32)
    assert len(src_in) == N_EXT
    return src_in, src_uq, src_ukv


def _round_up(n, m):
    return (n + m - 1) // m * m


def _permute_cols(xs, maps, stacks, name):
    maps = [np.asarray(m, np.int32) for m in maps]
    n_arr = len(xs)

    def block(ref, b):
        if len(ref.shape) == 2:
            return ref.at[:, b * LANE:(b + 1) * LANE]
        per = ref.shape[2] // LANE
        return ref.at[b // per, :, (b % per) * LANE:(b % per + 1) * LANE]

    def body(*refs):
        row = lax.broadcasted_iota(jnp.int32, (LANE, LANE), 0)
        for x_ref, src_ref, o_ref, src in zip(refs[:n_arr], refs[n_arr:2 * n_arr], refs[2 * n_arr:], maps):
            for c in range(len(src) // LANE):
                want = src[c * LANE:(c + 1) * LANE]
                if want[0] >= 0 and want[0] % LANE == 0 and np.array_equal(want, want[0] + np.arange(LANE)):
                    block(o_ref, c)[...] = block(x_ref, int(want[0]) // LANE)[...]
                    continue
                acc = jnp.zeros((x_ref.shape[-2], LANE), F32)
                for kb in sorted({int(v) // LANE for v in want if v >= 0}):
                    sel = jnp.where(row + kb * LANE == src_ref[:, c * LANE:(c + 1) * LANE], 1.0, 0.0).astype(BF16)
                    part = block(x_ref, kb)[...]
                    if part.dtype == BF16:
                        acc = acc + _nn(part, sel)
                    else:
                        hi = part.astype(BF16)
                        rest = part - hi.astype(F32)
                        mid = rest.astype(BF16)
                        low = (rest - mid.astype(F32)).astype(BF16)
                        acc = acc + ((_nn(hi, sel) + _nn(mid, sel)) + _nn(low, sel))
                block(o_ref, c)[...] = acc.astype(o_ref.dtype)

    def out_shape(x, m, stack):
        rows = x.shape[-2]
        return (rows, len(m)) if stack is None else (stack, rows, len(m) // stack)

    return _pallas(
        body, name=name,
        out_shape=[jax.ShapeDtypeStruct(out_shape(x, m, st), x.dtype) for x, m, st in zip(xs, maps, stacks)],
        compiler_params=_params(vmem=VMEM_MID),
    )(*xs, *[jnp.asarray(m).reshape(1, -1) for m in maps])


def _pad_cols(w):
    return jnp.pad(w, ((0, 0), (0, _round_up(w.shape[1], LANE) - w.shape[1])))


def _in_stack(cols, width):
    cols = np.asarray(cols)
    return np.where(cols < 0, -1, cols // width * _round_up(width, LANE) + cols % width).astype(np.int32)


def _ext_weights(g_in, g_uq, g_ukv, g_out):
    src_in, src_uq, src_ukv = _column_maps()
    w_uq = g_uq.reshape(B_Q_RANK, B_HEADS * B_QK)
    w_out = g_out.reshape(D_MODEL, D_MODEL)
    w_in_ext, w_uq_pad, w_ukv_ext = _permute_cols(
        [g_in, w_uq, g_ukv], [_in_stack(src_in, SH_IN[1]), src_uq, _in_stack(src_ukv, SH_UKV[1])], [None] * 3, "lay_out_weights")
    return w_in_ext, w_uq_pad, w_ukv_ext, w_out


def _fold_grads(d_in_ext, d_uq_pad, d_ukv_ext, d_out):
    src_in, src_uq, src_ukv = _column_maps()

    def back(src, n, width):
        inv = _inverse(src, n)
        wide = _round_up(width, LANE)
        out = np.full((n // width * wide,), -1, np.int32)
        for j in range(n // width):
            out[j * wide:j * wide + width] = inv[j * width:(j + 1) * width]
        return out

    n_uq, n_ukv = B_HEADS * B_QK, B_HEADS * (B_NOPE + B_V)
    d_in, d_uq, d_ukv = _permute_cols(
        [d_in_ext, d_uq_pad, d_ukv_ext], [back(src_in, N_IN, SH_IN[1]), _inverse(src_uq, n_uq), back(src_ukv, n_ukv, SH_UKV[1])],
        [N_CHIPS, None, N_CHIPS], "fold_grads")
    return d_in, d_uq.reshape((N_CHIPS,) + SH_UQ), d_ukv, d_out.reshape((N_CHIPS,) + SH_OUT)


def kernel(x, norm_in, w_in, a_q_norm, a_k_norm, b_cq_norm, b_ckv_norm, w_uq, w_ukv, b_q_norm, b_k_norm, w_out, loss_target, m_norm_in, m_w_in, m_a_q_norm, m_a_k_norm, m_b_cq_norm, m_b_ckv_norm, m_w_uq, m_w_ukv, m_b_q_norm, m_b_k_norm, m_w_out, v_norm_in, v_w_in, v_a_q_norm, v_a_k_norm, v_b_cq_norm, v_b_ckv_norm, v_w_uq, v_w_ukv, v_b_q_norm, v_b_k_norm, v_w_out):
    s_len = x.shape[1]
    xs, ts = x[0], loss_target[0]
    tm = min(256, s_len)
    tq, tk_f = min(512, s_len // 2), min(2048, s_len // 2)
    tq_b, tk_b = min(1024, s_len // 2), min(512, s_len)
    tiles_f = min(4, s_len // tq)
    tiles_b = min(2, s_len // tk_b)

    w_in_ext, w_uq_pad, w_ukv_ext, w_out_full = _ext_weights(
        *_gather_weights((_pad_cols(w_in[0]), w_uq[0], _pad_cols(w_ukv[0]), w_out[0])))
    gains = (norm_in, _spread(a_q_norm, 1, A_DIM, 1, LAY_ROPE_A), _spread(a_k_norm, 1, A_DIM, 1, LAY_ROPE_A), b_cq_norm, b_ckv_norm,
             _spread(b_q_norm, 1, B_QK, 1, LAY_ROPE_B), _spread(b_k_norm, 1, B_QK, 1, LAY_ROPE_B))
    tabs = _rope_tables(s_len, tm)

    (xn_t, gates, pre, qbpre, kbpre, cq_t, ckv_t, qa, ka, va, qb, kb, vb) = _pre(
        xs, tabs, w_in_ext, w_uq_pad, w_ukv_ext, gains, tm)
    o_a, lse_a = _attn_fwd(qa, ka, va, A_GROUP, A_DIM, tq, tk_f, tiles_f, "attn_fwd_a")
    o_b, lse_b = _attn_fwd(qb, kb, vb, 1, B_V, tq, tk_f, tiles_f, "attn_fwd_b")
    y_t, dh, dgate, do_a, do_b, delta, loss_part = _mid(xs, ts, o_a, o_b, gates, w_out_full, min(512, s_len))

    def stat(a):
        return a.reshape(a.shape[0], s_len // tq_b, 1, tq_b)

    dqa, dka, dva = _attn_bwd(qa, ka, va, do_a, stat(lse_a), stat(delta[:A_HEADS]), A_GROUP, tq_b, tk_b, tiles_b, "attn_bwd_a")
    dqb, dkb, dvb = _attn_bwd(qb, kb, vb, do_b, stat(lse_b), stat(delta[A_HEADS:A_HEADS + B_HEADS]), 1, tq_b, tk_b,
                              tiles_b, "attn_bwd_b")
    grad_x, dproj, dqbpre, dkvb, d_small = _post(
        xs, dh, pre, qbpre, kbpre, dgate, dqa, dka, dva, dqb, dkb, dvb, loss_part, tabs,
        w_in_ext, w_uq_pad, w_ukv_ext, gains, tm)

    ts_w = min(2048, s_len)
    d_in_ext = _grad_w(xn_t, dproj, N_EXT // 2, ts_w, "grad_w_in")
    d_out_full = _grad_w(y_t, dh, 512, ts_w, "grad_w_out")
    d_uq_pad, d_ukv_ext = _grad_w_pairs([(cq_t, dqbpre), (ckv_t, dkvb)], ts_w, "grad_w_mla")

    g_in_p, g_uq, g_ukv_p, g_out, g_small = _reduce_grads(_fold_grads(d_in_ext, d_uq_pad, d_ukv_ext, d_out_full), d_small)
    g_in, g_ukv = g_in_p[:, :SH_IN[1]], g_ukv_p[:, :SH_UKV[1]]
    d_in, nm_in, nv_in = (a.T for a in _adamw_rows(w_in[0].T, g_in_p.T[:SH_IN[1]], m_w_in[0].T, v_w_in[0].T, SH_IN[1] // 7))
    rest = _adamw_rest(
        [(w_uq[0], g_uq, m_w_uq[0], v_w_uq[0]), (w_ukv[0], g_ukv, m_w_ukv[0], v_w_ukv[0]),
         (w_out[0], g_out, m_w_out[0], v_w_out[0])],
        [(norm_in, m_norm_in, v_norm_in), (a_q_norm, m_a_q_norm, v_a_q_norm), (a_k_norm, m_a_k_norm, v_a_k_norm),
         (b_cq_norm, m_b_cq_norm, v_b_cq_norm), (b_ckv_norm, m_b_ckv_norm, v_b_ckv_norm),
         (b_q_norm, m_b_q_norm, v_b_q_norm), (b_k_norm, m_b_k_norm, v_b_k_norm)], g_small)
    (d_uq, nm_uq, nv_uq), (d_ukv, nm_ukv, nv_ukv), (d_out, nm_out, nv_out) = (rest[3 * i:3 * i + 3] for i in range(3))
    sm = [rest[9 + 4 * i:9 + 4 * i + 4] for i in range(7)]

    def leaves(k, p_in, p_uq, p_ukv, p_out):
        return [sm[SM_IN][k], p_in[None], sm[SM_AQ][k], sm[SM_AK][k], sm[SM_CQ][k], sm[SM_CKV][k], p_uq[None], p_ukv[None],
                sm[SM_BQ][k], sm[SM_BK][k], p_out[None]]

    return (g_small[SM_LOSS, 0], grad_x[None], *leaves(0, g_in, g_uq, g_ukv, g_out), *leaves(1, d_in, d_uq, d_ukv, d_out),
            *leaves(2, nm_in, nm_uq, nm_ukv, nm_out), *leaves(3, nv_in, nv_uq, nv_ukv, nv_out))
```

```python
import jax
import jax.numpy as jnp
import numpy as np
from jax import lax
from jax.experimental import pallas as pl
from jax.experimental.pallas import tpu as pltpu

F32 = jnp.float32
BF16 = jnp.bfloat16
MESH = pl.DeviceIdType.MESH

D_MODEL = 1024
GRID_W = 64
ROPE_THETA = 10000.0
EPS = 1e-6
A_HEADS, A_KV, A_DIM = 8, 2, 64
A_GROUP = A_HEADS // A_KV
B_HEADS, B_NOPE, B_ROPE, B_V = 4, 64, 32, 128
B_QK = B_NOPE + B_ROPE
B_Q_RANK, B_KV_RANK = 384, 256
N_IN = 2464
SCALE_A = 1.0 / float(np.sqrt(A_DIM))
SCALE_B = 1.0 / float(np.sqrt(B_QK))
LOG2E = float(np.log2(np.e))
LN2 = float(np.log(2.0))
ADAM_LR, ADAM_B1, ADAM_B2, ADAM_EPS, ADAM_WD, ADAM_STEP = 0.001, 0.9, 0.999, 1e-08, 0.01, 10

LANE = 128
VMEM_BYTES = 64 * 1024 * 1024
VMEM_LIMIT = VMEM_BYTES - 8 * 1024 * 1024
VMEM_MID = 48 * 1024 * 1024
VMEM_SMALL = 32 * 1024 * 1024

QA0 = 0
KA0 = QA0 + A_HEADS * LANE
VA0 = KA0 + A_KV * LANE
GA0 = VA0 + A_KV * A_DIM
GB0 = GA0 + A_HEADS * A_DIM
CQ0 = GB0 + B_HEADS * LANE
CKV0 = CQ0 + B_Q_RANK
KR0 = CKV0 + B_KV_RANK
N_EXT = KR0 + LANE
N_GATE = (A_HEADS + B_HEADS) * LANE
N_GATE_C = A_HEADS * A_DIM + B_HEADS * LANE
DELTA_ROWS = 16
N_PRE = KA0 + A_KV * LANE + B_Q_RANK + B_KV_RANK

ROT = LANE // 2
_QA = A_DIM // 4
_QB = B_ROPE // 4
LAY_PLAIN_A = ((0, A_DIM, 0),)
LAY_ROPE_A = ((0, _QA, 0), (2 * _QA, _QA, _QA), (_QA, _QA, ROT), (3 * _QA, _QA, ROT + _QA))
LAY_KR = ((0, _QB, 0), (2 * _QB, _QB, _QB), (_QB, _QB, ROT), (3 * _QB, _QB, ROT + _QB))
LAY_NOPE = ((0, B_NOPE // 2, 2 * _QB), (B_NOPE // 2, B_NOPE // 2, ROT + 2 * _QB))
LAY_ROPE_B = LAY_NOPE + tuple((B_NOPE + a, n, at) for a, n, at in LAY_KR)

N_CHIPS = 4
assert A_KV == 2 and N_EXT % (5 * LANE) == 0
SH_IN = (D_MODEL, N_IN // N_CHIPS)
SH_UQ = (B_Q_RANK // N_CHIPS, B_HEADS * B_QK)
SH_UKV = (B_KV_RANK, B_HEADS * (B_NOPE + B_V) // N_CHIPS)
SH_OUT = (D_MODEL // N_CHIPS, D_MODEL)
SM_ROWS, SM_W = 16, D_MODEL
SM_IN, SM_AQ, SM_AK, SM_CQ, SM_CKV, SM_BQ, SM_BK, SM_LOSS = range(8)
F32_ROWS, BF16_ROWS = 8, 16


def _pallas(body, **kw):
    return pl.pallas_call(body, **kw)


def _params(sem=None, vmem=None):
    return pltpu.CompilerParams(dimension_semantics=sem, vmem_limit_bytes=vmem)


def _rms_fwd(x, g, n):
    r = lax.rsqrt(jnp.sum(x * x, axis=-1, keepdims=True) * (1.0 / n) + EPS)
    return x * r * g


def _rms_bwd(dy, x, g, n):
    u = dy * g
    r = lax.rsqrt(jnp.sum(x * x, axis=-1, keepdims=True) * (1.0 / n) + EPS)
    ux = jnp.sum(u * x, axis=-1, keepdims=True)
    xhat = x * r
    dx = r * (u - xhat * (r * ux * (1.0 / n)))
    return dx, dy * xhat


def _rope_fwd(y, cos, sin):
    return y * cos + pltpu.roll(y, ROT, 1) * sin


def _rope_bwd(d, cos, sin):
    return d * cos - pltpu.roll(d, ROT, 1) * sin


def _token_tables(refs):
    out = []
    for r_ref, c_ref in zip(refs[0::2], refs[1::2]):
        r, c = r_ref[...], c_ref[...]
        out.append(jnp.concatenate([r[k:k + 1, :] + c for k in range(r.shape[0])], axis=0))
    return out


def _lanes_of(lane, layout):
    m = None
    for _, n, at in layout:
        seg = (lane >= at) & (lane < at + n)
        m = seg if m is None else (m | seg)
    return m


def _unspread_row(v, layout):
    v8 = jnp.broadcast_to(v, (F32_ROWS, LANE))
    lane = lax.broadcasted_iota(jnp.int32, v8.shape, 1)
    out = jnp.zeros_like(v8)
    for a, n, at in layout:
        moved = v8 if a == at else pltpu.roll(v8, (a - at) % LANE, 1)
        out = jnp.where((lane >= a) & (lane < a + n), moved, out)
    return out[0:1, :]


def _nt(a, b):
    return lax.dot_general(a, b, (((1,), (1,)), ((), ())), preferred_element_type=F32)


def _tn(a, b):
    return lax.dot_general(a, b, (((0,), (0,)), ((), ())), preferred_element_type=F32)


def _nn(a, b):
    return jnp.dot(a, b, preferred_element_type=F32)


def _block_rows(i, size):
    if isinstance(i, int):
        return pl.ds(i * size, size)
    return pl.ds(pl.multiple_of(i * size, size), size)


MAX_STATIC_BLOCKS = 32


def _three_stage(n, first, second, third):
    assert n >= 2 and n % 2 == 0
    first(0, 0)
    first(1, 1)
    second(0, 0)
    if n <= MAX_STATIC_BLOCKS:
        for i in range(1, n - 1):
            first(i + 1, (i + 1) % 2)
            second(i, i % 2)
            third(i - 1, (i - 1) % 2)
    else:
        def pair(t, carry):
            i = 2 * t + 1
            first(i + 1, 0)
            second(i, 1)
            third(i - 1, 0)
            first(i + 2, 1)
            second(i + 1, 0)
            third(i, 1)
            return carry

        lax.fori_loop(0, (n - 2) // 2, pair, 0)
    second(n - 1, 1)
    third(n - 2, 0)
    third(n - 1, 1)


def _full(shape):
    return pl.BlockSpec(shape, lambda *_: (0,) * len(shape))


def _table_specs(tm):
    return [pl.BlockSpec((None, tm // GRID_W, LANE), lambda i: (i, 0, 0)), _full((GRID_W, LANE))] * 4


def _resident(shape):
    return pl.BlockSpec(shape, lambda *_: (0,) * len(shape), pipeline_mode=pl.Buffered(1))


def _gather_weights(shards):
    n = len(shards)
    halves = [w.shape[0] // 2 for w in shards]

    def body(*refs):
        w_refs, out_refs, (send_sems, recv_sems) = refs[:n], refs[n:2 * n], refs[2 * n:]
        x, y, c = lax.axis_index("x"), lax.axis_index("y"), lax.axis_index("c")
        sibling = (x, y, 1 - c)
        chips = [(1 - x, y), (x, 1 - y), (1 - x, 1 - y)]
        me = 2 * x + y

        def copy(a, k, j, hc, to):
            part = out_refs[a].at[j, pl.ds(pl.multiple_of(hc * halves[a], BF16_ROWS), halves[a]), :]
            return pltpu.make_async_remote_copy(
                src_ref=part, dst_ref=part, send_sem=send_sems.at[6 * a + k], recv_sem=recv_sems.at[6 * a + k],
                device_id=to, device_id_type=MESH)

        started = []
        for a in range(n):
            out_refs[a][me] = w_refs[a][...].astype(BF16)
            for k, chip in enumerate(chips):
                started.append(copy(a, k, me, c, (*chip, c)))
                started[-1].start()
        for k, chip in enumerate(chips):
            for a in range(n):
                copy(a, k, 2 * chip[0] + chip[1], c, (*chip, c)).wait_recv()
                started.append(copy(a, 3 + k, 2 * chip[0] + chip[1], c, sibling))
                started[-1].start()
        for k, chip in enumerate(chips):
            for a in range(n):
                copy(a, 3 + k, 2 * chip[0] + chip[1], 1 - c, sibling).wait_recv()
        for cp in started:
            cp.wait_send()

    return _pallas(
        body, name="gather_weights",
        out_shape=[jax.ShapeDtypeStruct((N_CHIPS,) + w.shape, BF16) for w in shards],
        in_specs=[pl.BlockSpec(memory_space=pltpu.VMEM)] * n,
        out_specs=[pl.BlockSpec(memory_space=pltpu.VMEM)] * n,
        scratch_shapes=[pltpu.SemaphoreType.DMA((6 * n,)), pltpu.SemaphoreType.DMA((6 * n,))],
        compiler_params=_params(vmem=VMEM_SMALL),
    )(*shards)


def _reduce_grads(parts, small):
    n_big = len(parts)
    n = n_big + 1
    shapes = [p.shape[1:] for p in parts] + [small.shape]
    halves = [sh[0] // 2 for sh in shapes]

    def body(*refs):
        p_refs, out_refs, rec_a, rec_b = refs[:n], refs[n:2 * n], refs[2 * n:3 * n], refs[3 * n:4 * n]
        send_b = refs[4 * n:4 * n + n_big]
        sa_send, sa_recv, sb_send, sb_recv, sc_send, sc_recv = refs[4 * n + n_big:]
        x, y, c = lax.axis_index("x"), lax.axis_index("y"), lax.axis_index("c")
        sibling = (x, y, 1 - c)
        me = 2 * x + y

        def rows(a, hc):
            return pl.ds(pl.multiple_of(hc * halves[a], F32_ROWS), halves[a])

        def partial(a, j, hc):
            return p_refs[a].at[j, rows(a, hc), :] if a < n_big else p_refs[a].at[rows(a, hc), :]

        def copy_a(a, j):
            return pltpu.make_async_remote_copy(
                src_ref=partial(a, j, 1 - c), dst_ref=rec_a[a].at[j],
                send_sem=sa_send.at[N_CHIPS * a + j], recv_sem=sa_recv.at[N_CHIPS * a + j],
                device_id=sibling, device_id_type=MESH)

        def copy_b(a, r):
            j = me ^ r
            k = (N_CHIPS - 1) * a + r - 1
            return pltpu.make_async_remote_copy(
                src_ref=(send_b[a] if a < n_big else rec_a[a]).at[j], dst_ref=rec_b[a].at[r],
                send_sem=sb_send.at[k], recv_sem=sb_recv.at[k], device_id=(j // 2, j % 2, c), device_id_type=MESH)

        def copy_c(a):
            return pltpu.make_async_remote_copy(
                src_ref=out_refs[a].at[rows(a, c), :], dst_ref=out_refs[a].at[rows(a, c), :],
                send_sem=sc_send.at[a], recv_sem=sc_recv.at[a], device_id=sibling, device_id_type=MESH)

        for a in range(n):
            for j in range(N_CHIPS):
                copy_a(a, j).start()
        for r in range(1, N_CHIPS):
            j = me ^ r
            for a in range(n):
                copy_a(a, j).wait_recv()
                chip_part = rec_a[a][j] + partial(a, j, c)[...]
                if a < n_big:
                    send_b[a][j] = chip_part.astype(BF16)
                else:
                    rec_a[a][j] = chip_part
                copy_b(a, r).start()
        for a in range(n):
            copy_a(a, me).wait_recv()
            rec_b[a][0] = (rec_a[a][me] + partial(a, me, c)[...]).astype(rec_b[a].dtype)
        for a in range(n):
            for r in range(1, N_CHIPS):
                copy_b(a, r).wait_recv()
            total = rec_b[a][me].astype(F32)
            for j in range(1, N_CHIPS):
                total = total + rec_b[a][j ^ me].astype(F32)
            out_refs[a][rows(a, c), :] = total
            copy_c(a).start()
        for a in range(n):
            copy_c(a).wait_recv()
        for a in range(n):
            for j in range(N_CHIPS):
                copy_a(a, j).wait_send()
            for r in range(1, N_CHIPS):
                copy_b(a, r).wait_send()
            copy_c(a).wait_send()

    dma = pltpu.SemaphoreType.DMA
    return _pallas(
        body, name="reduce_grads",
        out_shape=[jax.ShapeDtypeStruct(sh, F32) for sh in shapes],
        in_specs=[pl.BlockSpec(memory_space=pltpu.VMEM)] * n,
        out_specs=[pl.BlockSpec(memory_space=pltpu.VMEM)] * n,
        scratch_shapes=[pltpu.VMEM((N_CHIPS, h) + sh[1:], F32) for h, sh in zip(halves, shapes)]
                       + [pltpu.VMEM((N_CHIPS, h) + sh[1:], BF16 if a < n_big else F32)
                          for a, (h, sh) in enumerate(zip(halves, shapes))]
                       + [pltpu.VMEM((N_CHIPS, h) + sh[1:], BF16) for h, sh in zip(halves[:n_big], shapes[:n_big])]
                       + [dma((N_CHIPS * n,)), dma((N_CHIPS * n,)), dma(((N_CHIPS - 1) * n,)), dma(((N_CHIPS - 1) * n,)),
                          dma((n,)), dma((n,))],
        compiler_params=_params(vmem=VMEM_LIMIT),
    )(*parts, small)


def _pre(x, tabs, w_in_ext, w_uq_pad, w_ukv_ext, gains, tm):
    s_len = x.shape[0]
    nt = s_len // tm

    def body(x_ref, car_ref, cac_ref, sar_ref, sac_ref, cbr_ref, cbc_ref, sbr_ref, sbc_ref, win_ref, wuq_ref, wukv_ref,
             gin_ref, gaq_ref, gak_ref, gcq_ref, gckv_ref, gbq_ref, gbk_ref,
             xn_ref, gates_ref, pre_ref, qbpre_ref, kbpre_ref, cq_ref, ckv_ref,
             qa_ref, ka_ref, va_ref, qb_ref, kb_ref, vb_ref, proj):
        xn = _rms_fwd(x_ref[...], gin_ref[...], D_MODEL)
        xn_ref[...] = jnp.transpose(xn).astype(BF16)
        xb = xn.astype(BF16)
        pre_ref[:, 0:VA0] = _nn(xb, win_ref[:, 0:VA0])
        gates_ref[...] = _nn(xb, win_ref[:, GA0:GA0 + N_GATE_C])
        pre_ref[:, VA0:N_PRE] = _nn(xb, win_ref[:, CQ0:KR0])
        proj[...] = _nn(xb, win_ref[:, VA0:GA0])
        kr = _nn(xb, win_ref[:, KR0:N_EXT])
        ca, sa, cb, sb = _token_tables((car_ref, cac_ref, sar_ref, sac_ref, cbr_ref, cbc_ref, sbr_ref, sbc_ref))
        lane = lax.broadcasted_iota(jnp.int32, (tm, LANE), 1)
        for h in range(A_HEADS):
            yq = _rms_fwd(pre_ref[:, QA0 + LANE * h:QA0 + LANE * (h + 1)], gaq_ref[...], A_DIM)
            qa_ref[h] = (_rope_fwd(yq, ca, sa) * (SCALE_A * LOG2E)).astype(BF16)
        for h in range(A_KV):
            yk = _rms_fwd(pre_ref[:, KA0 + LANE * h:KA0 + LANE * (h + 1)], gak_ref[...], A_DIM)
            ka_ref[h] = _rope_fwd(yk, ca, sa).astype(BF16)
            vh = proj[...] if h == 0 else pltpu.roll(proj[...], ROT, 1)
            va_ref[h] = jnp.where(lane == A_DIM, 1.0, jnp.where(lane < A_DIM, vh, 0.0)).astype(BF16)
        cq = _rms_fwd(pre_ref[:, VA0:VA0 + B_Q_RANK], gcq_ref[...], B_Q_RANK)
        cq_ref[...] = jnp.transpose(cq).astype(BF16)
        qbpre_ref[...] = _nn(cq.astype(BF16), wuq_ref[...])
        ckv = _rms_fwd(pre_ref[:, VA0 + B_Q_RANK:N_PRE], gckv_ref[...], B_KV_RANK)
        ckv_ref[...] = jnp.transpose(ckv).astype(BF16)
        kvb = _nn(ckv.astype(BF16), wukv_ref[...])
        for h in range(B_HEADS):
            yq = _rms_fwd(qbpre_ref[:, LANE * h:LANE * (h + 1)], gbq_ref[...], B_QK)
            qb_ref[h] = (_rope_fwd(yq, cb, sb) * (SCALE_B * LOG2E)).astype(BF16)
            kp = kvb[:, LANE * h:LANE * (h + 1)] + kr
            kbpre_ref[:, LANE * h:LANE * (h + 1)] = kp
            kb_ref[h] = _rope_fwd(_rms_fwd(kp, gbk_ref[...], B_QK), cb, sb).astype(BF16)
            vb_ref[h, :, 0:LANE] = kvb[:, B_HEADS * LANE + LANE * h:B_HEADS * LANE + LANE * (h + 1)].astype(BF16)
            vb_ref[h, :, LANE:2 * LANE] = jnp.where(lane == 0, 1.0, 0.0).astype(BF16)

    row = lambda w: pl.BlockSpec((tm, w), lambda i: (i, 0))
    col = lambda w: pl.BlockSpec((w, tm), lambda i: (0, i))
    heads = lambda n: pl.BlockSpec((n, tm, LANE), lambda i: (0, i, 0))
    hs = lambda n: jax.ShapeDtypeStruct((n, s_len, LANE), BF16)
    return _pallas(
        body, name="pre", grid=(nt,),
        in_specs=[row(D_MODEL)] + _table_specs(tm)
                 + [_resident(w_in_ext.shape), _resident(w_uq_pad.shape), _resident(w_ukv_ext.shape)]
                 + [_full(g.shape) for g in gains],
        out_specs=[col(D_MODEL), row(N_GATE_C), row(N_PRE), row(B_HEADS * LANE), row(B_HEADS * LANE),
                   col(B_Q_RANK), col(B_KV_RANK),
                   heads(A_HEADS), heads(A_KV), heads(A_KV), heads(B_HEADS), heads(B_HEADS),
                   pl.BlockSpec((B_HEADS, tm, 2 * LANE), lambda i: (0, i, 0))],
        out_shape=[jax.ShapeDtypeStruct((D_MODEL, s_len), BF16), jax.ShapeDtypeStruct((s_len, N_GATE_C), F32),
                   jax.ShapeDtypeStruct((s_len, N_PRE), F32), jax.ShapeDtypeStruct((s_len, B_HEADS * LANE), F32),
                   jax.ShapeDtypeStruct((s_len, B_HEADS * LANE), F32),
                   jax.ShapeDtypeStruct((B_Q_RANK, s_len), BF16), jax.ShapeDtypeStruct((B_KV_RANK, s_len), BF16),
                   hs(A_HEADS), hs(A_KV), hs(A_KV), hs(B_HEADS), hs(B_HEADS),
                   jax.ShapeDtypeStruct((B_HEADS, s_len, 2 * LANE), BF16)],
        scratch_shapes=[pltpu.VMEM((tm, GA0 - VA0), F32)],
        compiler_params=_params(("parallel",), VMEM_LIMIT),
    )(x, *tabs, w_in_ext, w_uq_pad, w_ukv_ext, *gains)


def _attn_fwd(q, k, v, group, l_col, tq, tk, tiles, name):
    n_heads, s_len, _ = q.shape
    v_w = v.shape[2]
    nk = s_len // tk

    def body(q_ref, k_ref, v_ref, o_ref, lse_ref, s_buf, p_buf, a_buf, m_ref, acc_ref):
        def scores(g, slot):
            s_buf[slot] = _nt(q_ref[_block_rows(g // nk, tq), :], k_ref[_block_rows(g % nk, tk), :])

        def softmax(g, slot):
            t = g // nk
            s = s_buf[slot]
            m_old = m_ref[t]
            m_new = jnp.maximum(m_old, jnp.max(s, axis=-1, keepdims=True))
            m_ref[t] = m_new
            a_buf[slot] = jnp.exp2(m_old - m_new)
            p_buf[slot] = jnp.exp2(s - jnp.tile(m_new, (1, tk // LANE))).astype(BF16)

        def values(g, slot):
            t = g // nk
            pv = _nn(p_buf[slot], v_ref[_block_rows(g % nk, tk), :])
            for c in range(0, v_w, LANE):
                acc_ref[t, :, c:c + LANE] = a_buf[slot] * acc_ref[t, :, c:c + LANE] + pv[:, c:c + LANE]

        m_ref[...] = jnp.full(m_ref.shape, -1e30, F32)
        acc_ref[...] = jnp.zeros(acc_ref.shape, F32)
        _three_stage(tiles * nk, scores, softmax, values)
        for t in range(tiles):
            l = acc_ref[t, :, l_col:l_col + 1]
            o = acc_ref[t, :, 0:LANE] * (1.0 / l)
            if l_col < LANE:
                lane = lax.broadcasted_iota(jnp.int32, o.shape, 1)
                o = jnp.where(lane == l_col, 0.0, o)
            o_ref[t * tq:(t + 1) * tq, :] = o
            lse_ref[t] = jnp.transpose(m_ref[t] + jnp.log2(jnp.broadcast_to(l, (tq, LANE))))[0:1, :]

    return _pallas(
        body, name=name, grid=(n_heads, s_len // (tiles * tq)),
        in_specs=[pl.BlockSpec((None, tiles * tq, LANE), lambda h, i: (h, i, 0)),
                  pl.BlockSpec((None, s_len, LANE), lambda h, i: (h // group, 0, 0)),
                  pl.BlockSpec((None, s_len, v_w), lambda h, i: (h // group, 0, 0))],
        out_specs=[pl.BlockSpec((None, tiles * tq, LANE), lambda h, i: (h, i, 0)),
                   pl.BlockSpec((None, tiles, 1, tq), lambda h, i: (h, i, 0, 0))],
        out_shape=[jax.ShapeDtypeStruct((n_heads, s_len, LANE), F32),
                   jax.ShapeDtypeStruct((n_heads, s_len // tq, 1, tq), F32)],
        scratch_shapes=[pltpu.VMEM((2, tq, tk), F32), pltpu.VMEM((2, tq, tk), BF16), pltpu.VMEM((2, tq, LANE), F32),
                        pltpu.VMEM((tiles, tq, LANE), F32), pltpu.VMEM((tiles, tq, v_w), F32)],
        compiler_params=_params(("parallel", "parallel"), VMEM_MID),
    )(q, k, v)


def _mid(x, target, o_a, o_b, gates, w_out, tm):
    s_len = x.shape[0]
    nt = s_len // tm
    n_heads = A_HEADS + B_HEADS
    d_mix = w_out.shape[0]
    pairs = A_HEADS // 2

    def body(x_ref, t_ref, oa_ref, ob_ref, g_ref, w_ref,
             yt_ref, dh_ref, dgate_ref, doa_ref, dob_ref, delta_ref, loss_ref, silu_scr, dsilu_scr, y_ref):
        @pl.when(pl.program_id(0) == 0)
        def _():
            loss_ref[...] = jnp.zeros_like(loss_ref)

        def o_of(h):
            return oa_ref[h] if h < A_HEADS else ob_ref[h - A_HEADS]

        lane = lax.broadcasted_iota(jnp.int32, (tm, LANE), 1)

        def gated(h):
            cols = slice(LANE * h, LANE * (h + 1))
            if h < A_HEADS:
                packed = g_ref[:, LANE * (h // 2):LANE * (h // 2 + 1)]
                g = jnp.where(lane < A_DIM, packed if h % 2 == 0 else pltpu.roll(packed, ROT, 1), 0.0)
            else:
                g = g_ref[:, LANE * (pairs + h - A_HEADS):LANE * (pairs + h - A_HEADS + 1)]
            sig = 1.0 / (1.0 + jnp.exp(-g))
            silu = g * sig
            silu_scr[:, cols] = silu
            dsilu_scr[:, cols] = sig * (1.0 + g * (1.0 - sig))
            return o_of(h) * silu

        for c in range(pairs + B_HEADS):
            y = gated(2 * c) + pltpu.roll(gated(2 * c + 1), ROT, 1) if c < pairs else gated(A_HEADS + c - pairs)
            cols = slice(LANE * c, LANE * (c + 1))
            y_ref[:, cols] = y.astype(BF16)
            yt_ref[cols, :] = jnp.transpose(y).astype(BF16)
        err = x_ref[...] + _nn(y_ref[...], w_ref[...]) - t_ref[...]
        sq = jnp.sum(jnp.sum(err * err, axis=-1, keepdims=True), axis=0, keepdims=True)
        loss_ref[...] += jnp.broadcast_to(sq * (0.5 / D_MODEL), loss_ref.shape)
        dh = err * (1.0 / D_MODEL)
        dh_ref[...] = dh
        dy = _nt(dh.astype(BF16), w_ref[...])
        delta = jnp.zeros((tm, LANE), F32)
        held = None
        for h in range(n_heads):
            cols = slice(LANE * h, LANE * (h + 1))
            if h < A_HEADS:
                packed = dy[:, LANE * (h // 2):LANE * (h // 2 + 1)]
                dyh = packed if h % 2 == 0 else pltpu.roll(packed, ROT, 1)
            else:
                dyh = dy[:, LANE * (pairs + h - A_HEADS):LANE * (pairs + h - A_HEADS + 1)]
            oh = o_of(h)
            do = dyh * silu_scr[:, cols]
            dg = dyh * oh * dsilu_scr[:, cols]
            if h >= A_HEADS:
                dgate_ref[:, LANE * (pairs + h - A_HEADS):LANE * (pairs + h - A_HEADS + 1)] = dg.astype(BF16)
            elif h % 2 == 0:
                held = dg
            else:
                dgate_ref[:, LANE * (h // 2):LANE * (h // 2 + 1)] = (held + pltpu.roll(dg, ROT, 1)).astype(BF16)
            delta = jnp.where(lane == h, jnp.sum(do * oh, axis=-1, keepdims=True), delta)
            if h < A_HEADS:
                doa_ref[h] = do.astype(BF16)
            else:
                dob_ref[h - A_HEADS] = do.astype(BF16)
        delta_ref[...] = jnp.transpose(delta)[0:DELTA_ROWS, :]

    row = lambda w: pl.BlockSpec((tm, w), lambda i: (i, 0))
    heads = lambda n, w=LANE: pl.BlockSpec((n, tm, w), lambda i: (0, i, 0))
    return _pallas(
        body, name="mid", grid=(nt,),
        in_specs=[row(D_MODEL), row(D_MODEL), heads(A_HEADS), heads(B_HEADS), row(N_GATE_C), _resident(w_out.shape)],
        out_specs=[pl.BlockSpec((d_mix, tm), lambda i: (0, i)), row(D_MODEL), row(N_GATE_C), heads(A_HEADS), heads(B_HEADS),
                   pl.BlockSpec((DELTA_ROWS, tm), lambda i: (0, i)),
                   _full((8, LANE))],
        out_shape=[jax.ShapeDtypeStruct((d_mix, s_len), BF16), jax.ShapeDtypeStruct((s_len, D_MODEL), F32),
                   jax.ShapeDtypeStruct((s_len, N_GATE_C), BF16),
                   jax.ShapeDtypeStruct((A_HEADS, s_len, LANE), BF16), jax.ShapeDtypeStruct((B_HEADS, s_len, LANE), BF16),
                   jax.ShapeDtypeStruct((DELTA_ROWS, s_len), F32), jax.ShapeDtypeStruct((8, LANE), F32)],
        scratch_shapes=[pltpu.VMEM((tm, N_GATE), F32), pltpu.VMEM((tm, N_GATE), F32), pltpu.VMEM((tm, d_mix), BF16)],
        compiler_params=_params(("arbitrary",), VMEM_LIMIT),
    )(x, target, o_a, o_b, gates, w_out)


def _attn_bwd(q, k, v, do, lse, delta, group, tq, tk, tiles, name):
    n_heads, s_len, _ = q.shape
    nq = s_len // tq

    def body(q_ref, do_ref, lse_ref, delta_ref, k_ref, v_ref, dq_ref, dk_ref, dv_ref, s_buf, dp_buf, p_buf, ds_buf):
        @pl.when(pl.program_id(1) == 0)
        def _():
            dq_ref[...] = jnp.zeros_like(dq_ref)

        dk_ref[...] = jnp.zeros_like(dk_ref)
        dv_ref[...] = jnp.zeros_like(dv_ref)

        def keys(g):
            return _block_rows(g // nq, tk)

        def queries(g):
            return _block_rows(g % nq, tq)

        def scores(g, slot):
            s_buf[slot] = _nt(k_ref[keys(g), :], q_ref[queries(g), :])
            dp_buf[slot] = _nt(v_ref[keys(g), :], do_ref[queries(g), :])

        def elementwise(g, slot):
            p = jnp.exp2(s_buf[slot] - lse_ref[g % nq])
            p_buf[slot] = p.astype(BF16)
            ds_buf[slot] = (p * (dp_buf[slot] - delta_ref[g % nq])).astype(BF16)

        def grads(g, slot):
            dv_ref[keys(g), :] += _nn(p_buf[slot], do_ref[queries(g), :])
            dk_ref[keys(g), :] += _nn(ds_buf[slot], q_ref[queries(g), :])
            dq_ref[queries(g), :] += _tn(ds_buf[slot], k_ref[keys(g), :])

        _three_stage(tiles * nq, scores, elementwise, grads)

    whole = lambda: pl.BlockSpec((None, s_len, LANE), lambda h, j: (h, 0, 0))
    stat = lambda: pl.BlockSpec((None, nq, 1, tq), lambda h, j: (h, 0, 0, 0))
    kvb = lambda: pl.BlockSpec((None, tiles * tk, LANE), lambda h, j: (h // group, j, 0))
    outb = lambda: pl.BlockSpec((None, tiles * tk, LANE), lambda h, j: (h, j, 0))
    shape = jax.ShapeDtypeStruct((n_heads, s_len, LANE), F32)
    return _pallas(
        body, name=name, grid=(n_heads, s_len // (tiles * tk)),
        in_specs=[whole(), whole(), stat(), stat(), kvb(), kvb()],
        out_specs=[whole(), outb(), outb()],
        out_shape=[shape, shape, shape],
        scratch_shapes=[pltpu.VMEM((2, tk, tq), F32), pltpu.VMEM((2, tk, tq), F32),
                        pltpu.VMEM((2, tk, tq), BF16), pltpu.VMEM((2, tk, tq), BF16)],
        compiler_params=_params(("parallel", "arbitrary"), VMEM_MID),
    )(q, do, lse, delta, k, v)


def _post(x, dh, pre, qbpre, kbpre, dgate, dqa, dka, dva, dqb, dkb, dvb, loss_part, tabs,
          w_in_ext, w_uq_pad, w_ukv_ext, gains, tm):
    s_len = x.shape[0]
    nt = s_len // tm

    def body(x_ref, dh_ref, pre_ref, qbpre_ref, kbpre_ref, dgate_ref,
             dqa_ref, dka_ref, dva_ref, dqb_ref, dkb_ref, dvb_ref, loss_ref,
             car_ref, cac_ref, sar_ref, sac_ref, cbr_ref, cbc_ref, sbr_ref, sbc_ref, win_ref, wuq_ref, wukv_ref,
             gin_ref, gaq_ref, gak_ref, gcq_ref, gckv_ref, gbq_ref, gbk_ref,
             gx_ref, dproj_ref, dqbpre_ref, dkvb_ref, dsm_ref):
        @pl.when(pl.program_id(0) == 0)
        def _():
            dsm_ref[...] = jnp.zeros_like(dsm_ref)
            dsm_ref[SM_LOSS:SM_LOSS + 1, 0:LANE] = loss_ref[0:1, :]

        def add_small(r, dg):
            dsm_ref[r:r + 1, 0:dg.shape[1]] += dg

        def tok_sum(a):
            return jnp.sum(a, axis=0, keepdims=True)

        ca, sa, cb, sb = _token_tables((car_ref, cac_ref, sar_ref, sac_ref, cbr_ref, cbc_ref, sbr_ref, sbc_ref))
        lane = lax.broadcasted_iota(jnp.int32, (tm, LANE), 1)

        nope_lanes = _lanes_of(lane, LAY_NOPE)

        def back(c0, c1):
            return _nt(dproj_ref[:, c0:c1], win_ref[:, c0:c1])

        dproj_ref[:, GA0:GA0 + N_GATE_C] = dgate_ref[...]
        dxn = back(GA0, GA0 + N_GATE_C)
        dg = jnp.zeros((1, LANE), F32)
        for h in range(A_HEADS):
            dn = _rope_bwd(dqa_ref[h] * SCALE_A, ca, sa)
            dx, dgr = _rms_bwd(dn, pre_ref[:, QA0 + LANE * h:QA0 + LANE * (h + 1)], gaq_ref[...], A_DIM)
            dproj_ref[:, QA0 + LANE * h:QA0 + LANE * (h + 1)] = dx.astype(BF16)
            dg = dg + tok_sum(dgr)
        add_small(SM_AQ, _unspread_row(dg, LAY_ROPE_A))
        dxn = dxn + back(QA0, KA0)
        dg = jnp.zeros((1, LANE), F32)
        dvs = []
        for h in range(A_KV):
            dk = dka_ref[A_GROUP * h]
            dv = dva_ref[A_GROUP * h]
            for g in range(1, A_GROUP):
                dk = dk + dka_ref[A_GROUP * h + g]
                dv = dv + dva_ref[A_GROUP * h + g]
            dn = _rope_bwd(dk * LN2, ca, sa)
            dx, dgr = _rms_bwd(dn, pre_ref[:, KA0 + LANE * h:KA0 + LANE * (h + 1)], gak_ref[...], A_DIM)
            dproj_ref[:, KA0 + LANE * h:KA0 + LANE * (h + 1)] = dx.astype(BF16)
            dvs.append(dv)
            dg = dg + tok_sum(dgr)
        add_small(SM_AK, _unspread_row(dg, LAY_ROPE_A))
        dproj_ref[:, VA0:GA0] = (dvs[0] + pltpu.roll(dvs[1], ROT, 1)).astype(BF16)
        dxn = dxn + back(KA0, GA0)
        dg = jnp.zeros((1, LANE), F32)
        for h in range(B_HEADS):
            cols = slice(LANE * h, LANE * (h + 1))
            dn = _rope_bwd(dqb_ref[h] * SCALE_B, cb, sb)
            dx, dgr = _rms_bwd(dn, qbpre_ref[:, cols], gbq_ref[...], B_QK)
            dqbpre_ref[:, cols] = dx.astype(BF16)
            dg = dg + tok_sum(dgr)
        add_small(SM_BQ, _unspread_row(dg, LAY_ROPE_B))
        dcq = _nt(dqbpre_ref[...], wuq_ref[...])
        dx, dgr = _rms_bwd(dcq, pre_ref[:, VA0:VA0 + B_Q_RANK], gcq_ref[...], B_Q_RANK)
        dproj_ref[:, CQ0:CQ0 + B_Q_RANK] = dx.astype(BF16)
        add_small(SM_CQ, tok_sum(dgr))
        dxn = dxn + back(CQ0, CKV0)
        dg = jnp.zeros((1, LANE), F32)
        dkr = jnp.zeros((tm, LANE), F32)
        for h in range(B_HEADS):
            cols = slice(LANE * h, LANE * (h + 1))
            dn = _rope_bwd(dkb_ref[h] * LN2, cb, sb)
            dx, dgr = _rms_bwd(dn, kbpre_ref[:, cols], gbk_ref[...], B_QK)
            dkvb_ref[:, cols] = jnp.where(nope_lanes, dx, 0.0).astype(BF16)
            dkvb_ref[:, B_HEADS * LANE + LANE * h:B_HEADS * LANE + LANE * (h + 1)] = dvb_ref[h].astype(BF16)
            dkr = dkr + dx
            dg = dg + tok_sum(dgr)
        add_small(SM_BK, _unspread_row(dg, LAY_ROPE_B))
        dproj_ref[:, KR0:KR0 + LANE] = jnp.where(_lanes_of(lane, LAY_KR), dkr, 0.0).astype(BF16)
        dckv = _nt(dkvb_ref[...], wukv_ref[...])
        dx, dgr = _rms_bwd(dckv, pre_ref[:, VA0 + B_Q_RANK:N_PRE], gckv_ref[...], B_KV_RANK)
        dproj_ref[:, CKV0:CKV0 + B_KV_RANK] = dx.astype(BF16)
        add_small(SM_CKV, tok_sum(dgr))
        dxn = dxn + back(CKV0, N_EXT)
        dx, dgr = _rms_bwd(dxn, x_ref[...], gin_ref[...], D_MODEL)
        gx_ref[...] = dh_ref[...] + dx
        add_small(SM_IN, tok_sum(dgr))

    row = lambda w: pl.BlockSpec((tm, w), lambda i: (i, 0))
    heads = lambda n: pl.BlockSpec((n, tm, LANE), lambda i: (0, i, 0))
    return _pallas(
        body, name="post", grid=(nt,),
        in_specs=[row(D_MODEL), row(D_MODEL), row(N_PRE), row(B_HEADS * LANE), row(B_HEADS * LANE), row(N_GATE_C),
                  heads(A_HEADS), heads(A_HEADS), heads(A_HEADS), heads(B_HEADS), heads(B_HEADS), heads(B_HEADS),
                  _full(loss_part.shape)] + _table_specs(tm)
                 + [_resident(w_in_ext.shape), _resident(w_uq_pad.shape), _resident(w_ukv_ext.shape)]
                 + [_full(g.shape) for g in gains],
        out_specs=[row(D_MODEL), row(N_EXT), row(B_HEADS * LANE), row(2 * B_HEADS * LANE), _full((SM_ROWS, SM_W))],
        out_shape=[jax.ShapeDtypeStruct((s_len, D_MODEL), F32), jax.ShapeDtypeStruct((s_len, N_EXT), BF16),
                   jax.ShapeDtypeStruct((s_len, B_HEADS * LANE), BF16),
                   jax.ShapeDtypeStruct((s_len, 2 * B_HEADS * LANE), BF16),
                   jax.ShapeDtypeStruct((SM_ROWS, SM_W), F32)],
        compiler_params=_params(("arbitrary",), VMEM_LIMIT),
    )(x, dh, pre, qbpre, kbpre, dgate, dqa, dka, dva, dqb, dkb, dvb, loss_part, *tabs,
      w_in_ext, w_uq_pad, w_ukv_ext, *gains)


def _grad_w(a_t, b, tn, ts, name):
    m, s_len = a_t.shape
    n = b.shape[1]

    def body(a_ref, b_ref, o_ref):
        @pl.when(pl.program_id(1) == 0)
        def _():
            o_ref[...] = jnp.zeros_like(o_ref)

        o_ref[...] += _nn(a_ref[...], b_ref[...].astype(BF16))

    return _pallas(
        body, name=name, grid=(n // tn, s_len // ts),
        in_specs=[pl.BlockSpec((m, ts), lambda j, t: (0, t)), pl.BlockSpec((ts, tn), lambda j, t: (t, j))],
        out_specs=pl.BlockSpec((m, tn), lambda j, t: (0, j)),
        out_shape=jax.ShapeDtypeStruct((m, n), F32),
        compiler_params=_params(("parallel", "arbitrary"), VMEM_MID),
    )(a_t, b)


def _grad_w_pairs(pairs, ts, name):
    s_len = pairs[0][0].shape[1]
    n_p = len(pairs)

    def body(*refs):
        for a_ref, b_ref, o_ref in zip(refs[0:2 * n_p:2], refs[1:2 * n_p:2], refs[2 * n_p:]):
            @pl.when(pl.program_id(0) == 0)
            def _():
                o_ref[...] = jnp.zeros_like(o_ref)

            o_ref[...] += _nn(a_ref[...], b_ref[...].astype(BF16))

    in_specs, flat = [], []
    for a_t, b in pairs:
        in_specs += [pl.BlockSpec((a_t.shape[0], ts), lambda t: (0, t)), pl.BlockSpec((ts, b.shape[1]), lambda t: (t, 0))]
        flat += [a_t, b]
    return _pallas(
        body, name=name, grid=(s_len // ts,),
        in_specs=in_specs,
        out_specs=[_full((a_t.shape[0], b.shape[1])) for a_t, b in pairs],
        out_shape=[jax.ShapeDtypeStruct((a_t.shape[0], b.shape[1]), F32) for a_t, b in pairs],
        compiler_params=_params(("arbitrary",), VMEM_MID),
    )(*flat)


def _adam_math(w, g, m, v):
    nm = ADAM_B1 * m + (1.0 - ADAM_B1) * g
    nv = ADAM_B2 * v + (1.0 - ADAM_B2) * (g * g)
    m_hat = nm / (1.0 - ADAM_B1 ** ADAM_STEP)
    v_hat = nv / (1.0 - ADAM_B2 ** ADAM_STEP)
    return -ADAM_LR * (m_hat / (jnp.sqrt(v_hat) + ADAM_EPS) + ADAM_WD * w), nm, nv


def _adamw_rows(w, g, m, v, tr):
    rows, cols = w.shape

    def body(w_ref, g_ref, m_ref, v_ref, d_ref, nm_ref, nv_ref):
        d_ref[...], nm_ref[...], nv_ref[...] = _adam_math(w_ref[...], g_ref[...], m_ref[...], v_ref[...])

    blk = pl.BlockSpec((tr, cols), lambda i: (i, 0))
    shape = jax.ShapeDtypeStruct((rows, cols), F32)
    return _pallas(
        body, name="adamw_w_in", grid=(rows // tr,),
        in_specs=[blk] * 4, out_specs=[blk] * 3, out_shape=[shape] * 3,
        compiler_params=_params(("parallel",), VMEM_SMALL),
    )(w, g, m, v)


def _adamw_rest(bigs, smalls, g_small):
    nb, ns = len(bigs), len(smalls)

    def body(*refs):
        ins, outs = refs[:4 * nb + 3 * ns + 1], refs[4 * nb + 3 * ns + 1:]
        for i in range(nb):
            w_ref, g_ref, m_ref, v_ref = ins[4 * i:4 * i + 4]
            d_ref, nm_ref, nv_ref = outs[3 * i:3 * i + 3]
            d_ref[...], nm_ref[...], nv_ref[...] = _adam_math(w_ref[...], g_ref[...], m_ref[...], v_ref[...])
        gs_ref = ins[-1]
        for i in range(ns):
            w_ref, m_ref, v_ref = ins[4 * nb + 3 * i:4 * nb + 3 * i + 3]
            g_ref, d_ref, nm_ref, nv_ref = outs[3 * nb + 4 * i:3 * nb + 4 * i + 4]
            g = gs_ref[i:i + 1, 0:w_ref.shape[1]]
            g_ref[...] = g
            d_ref[...], nm_ref[...], nv_ref[...] = _adam_math(w_ref[...], g, m_ref[...], v_ref[...])

    flat_in = [a for quad in bigs for a in quad] + [a for tri in smalls for a in tri] + [g_small]
    out_shape = ([jax.ShapeDtypeStruct(q[0].shape, F32) for q in bigs for _ in range(3)]
                 + [jax.ShapeDtypeStruct(t[0].shape, F32) for t in smalls for _ in range(4)])
    return _pallas(
        body, name="adamw_rest",
        in_specs=[pl.BlockSpec(memory_space=pltpu.VMEM)] * len(flat_in),
        out_specs=[pl.BlockSpec(memory_space=pltpu.VMEM)] * len(out_shape),
        out_shape=out_shape,
        compiler_params=_params(vmem=VMEM_SMALL),
    )(*flat_in)


def _place(pieces, n):
    out, at = [], 0
    for lane0, arr in sorted(pieces, key=lambda p: p[0]):
        out += [jnp.zeros((n, lane0 - at), F32), arr]
        at = lane0 + arr.shape[1]
    return jnp.concatenate(out + [jnp.zeros((n, LANE - at), F32)], axis=1)


def _rope_tables(s_len, tm):
    rows = s_len // GRID_W
    row = jnp.arange(rows, dtype=F32)
    col = jnp.arange(GRID_W, dtype=F32)

    def lay(dim, layout, first_dim, ones):
        half = dim // 2
        inv = 1.0 / (ROPE_THETA ** (jnp.arange(0, half, 2, dtype=F32) / half))
        ang_r, ang_c = row[:, None] * inv[None, :], col[:, None] * inv[None, :]
        at = {a - first_dim: lane0 for a, _, lane0 in layout}
        q = dim // 4
        r1, r2, c1, c2 = at[0], at[q], at[2 * q], at[3 * q]
        cos_r = _place([(r1, jnp.cos(ang_r)), (r2, jnp.cos(ang_r))], rows)
        sin_r = _place([(r1, -jnp.sin(ang_r)), (r2, jnp.sin(ang_r))], rows)
        cos_c = _place([(c1, jnp.cos(ang_c)), (c2, jnp.cos(ang_c))] + [(l0, jnp.ones((GRID_W, n), F32)) for _, n, l0 in ones],
                       GRID_W)
        sin_c = _place([(c1, -jnp.sin(ang_c)), (c2, jnp.sin(ang_c))], GRID_W)
        by_block = (s_len // tm, tm // GRID_W, LANE)
        return cos_r.reshape(by_block), cos_c, sin_r.reshape(by_block), sin_c

    return lay(A_DIM, LAY_ROPE_A, 0, ()) + lay(B_ROPE, LAY_KR, 0, LAY_NOPE)


def _spread(w, n_heads, dim, axis, layout):
    w3 = w.reshape(w.shape[:axis] + (n_heads, dim) + w.shape[axis + 1:])
    out, at = [], 0

    def zeros(n):
        return jnp.zeros(w3.shape[:axis + 1] + (n,) + w3.shape[axis + 2:], w.dtype)

    for a0, n, lane0 in sorted(layout, key=lambda seg: seg[2]):
        out += [zeros(lane0 - at), lax.slice_in_dim(w3, a0, a0 + n, axis=axis + 1)]
        at = lane0 + n
    out = jnp.concatenate(out + [zeros(LANE - at)], axis=axis + 1)
    return out.reshape(w.shape[:axis] + (n_heads * LANE,) + w.shape[axis + 1:])


def _head_cols(first, n_heads, dim, layout):
    out = np.full((n_heads * LANE,), -1, np.int32)
    for h in range(n_heads):
        for a0, n, lane0 in layout:
            out[h * LANE + lane0:h * LANE + lane0 + n] = first + h * dim + a0 + np.arange(n)
    return out


def _inverse(src, n):
    dst = np.full((n,), -1, np.int32)
    dst[src[src >= 0]] = np.nonzero(src >= 0)[0]
    return dst


def _column_maps():
    a_w, kv_w = A_HEADS * A_DIM, A_KV * A_DIM
    o_g = a_w + 2 * kv_w
    o_cq = o_g + a_w
    o_kr = o_cq + B_Q_RANK + B_KV_RANK
    src_in = np.concatenate([
        _head_cols(0, A_HEADS, A_DIM, LAY_ROPE_A), _head_cols(a_w, A_KV, A_DIM, LAY_ROPE_A),
        np.arange(a_w + kv_w, o_g), np.arange(o_g, o_g + a_w),
        np.arange(o_kr + B_ROPE, N_IN), np.arange(o_cq, o_kr), _head_cols(o_kr, 1, B_ROPE, LAY_KR)]).astype(np.int32)
    src_uq = _head_cols(0, B_HEADS, B_QK, LAY_ROPE_B)
    per = B_NOPE + B_V
    src_ukv = np.concatenate([_head_cols(0, B_HEADS, per, LAY_NOPE),
                              _head_cols(B_NOPE, B_HEADS, per, ((0, B_V, 0),))]).astype(np.int32)
    assert len(src_in) == N_EXT
    return src_in, src_uq, src_ukv


def _round_up(n, m):
    return (n + m - 1) // m * m


def _permute_cols(xs, maps, stacks, name):
    maps = [np.asarray(m, np.int32) for m in maps]
    n_arr = len(xs)

    def block(ref, b):
        if len(ref.shape) == 2:
            return ref.at[:, b * LANE:(b + 1) * LANE]
        per = ref.shape[2] // LANE
        return ref.at[b // per, :, (b % per) * LANE:(b % per + 1) * LANE]

    def body(*refs):
        row = lax.broadcasted_iota(jnp.int32, (LANE, LANE), 0)
        for x_ref, src_ref, o_ref, src in zip(refs[:n_arr], refs[n_arr:2 * n_arr], refs[2 * n_arr:], maps):
            for c in range(len(src) // LANE):
                want = src[c * LANE:(c + 1) * LANE]
                if want[0] >= 0 and want[0] % LANE == 0 and np.array_equal(want, want[0] + np.arange(LANE)):
                    block(o_ref, c)[...] = block(x_ref, int(want[0]) // LANE)[...]
                    continue
                acc = jnp.zeros((x_ref.shape[-2], LANE), F32)
                for kb in sorted({int(v) // LANE for v in want if v >= 0}):
                    sel = jnp.where(row + kb * LANE == src_ref[:, c * LANE:(c + 1) * LANE], 1.0, 0.0).astype(BF16)
                    part = block(x_ref, kb)[...]
                    if part.dtype == BF16:
                        acc = acc + _nn(part, sel)
                    else:
                        hi = part.astype(BF16)
                        rest = part - hi.astype(F32)
                        mid = rest.astype(BF16)
                        low = (rest - mid.astype(F32)).astype(BF16)
                        acc = acc + ((_nn(hi, sel) + _nn(mid, sel)) + _nn(low, sel))
                block(o_ref, c)[...] = acc.astype(o_ref.dtype)

    def out_shape(x, m, stack):
        rows = x.shape[-2]
        return (rows, len(m)) if stack is None else (stack, rows, len(m) // stack)

    return _pallas(
        body, name=name,
        out_shape=[jax.ShapeDtypeStruct(out_shape(x, m, st), x.dtype) for x, m, st in zip(xs, maps, stacks)],
        compiler_params=_params(vmem=VMEM_MID),
    )(*xs, *[jnp.asarray(m).reshape(1, -1) for m in maps])


def _pad_cols(w):
    return jnp.pad(w, ((0, 0), (0, _round_up(w.shape[1], LANE) - w.shape[1])))


def _in_stack(cols, width):
    cols = np.asarray(cols)
    return np.where(cols < 0, -1, cols // width * _round_up(width, LANE) + cols % width).astype(np.int32)


def _ext_weights(g_in, g_uq, g_ukv, g_out):
    src_in, src_uq, src_ukv = _column_maps()
    w_uq = g_uq.reshape(B_Q_RANK, B_HEADS * B_QK)
    w_out = g_out.reshape(D_MODEL, D_MODEL)
    w_in_ext, w_uq_pad, w_ukv_ext = _permute_cols(
        [g_in, w_uq, g_ukv], [_in_stack(src_in, SH_IN[1]), src_uq, _in_stack(src_ukv, SH_UKV[1])], [None] * 3, "lay_out_weights")
    return w_in_ext, w_uq_pad, w_ukv_ext, w_out


def _fold_grads(d_in_ext, d_uq_pad, d_ukv_ext, d_out):
    src_in, src_uq, src_ukv = _column_maps()

    def back(src, n, width):
        inv = _inverse(src, n)
        wide = _round_up(width, LANE)
        out = np.full((n // width * wide,), -1, np.int32)
        for j in range(n // width):
            out[j * wide:j * wide + width] = inv[j * width:(j + 1) * width]
        return out

    n_uq, n_ukv = B_HEADS * B_QK, B_HEADS * (B_NOPE + B_V)
    d_in, d_uq, d_ukv = _permute_cols(
        [d_in_ext, d_uq_pad, d_ukv_ext], [back(src_in, N_IN, SH_IN[1]), _inverse(src_uq, n_uq), back(src_ukv, n_ukv, SH_UKV[1])],
        [N_CHIPS, None, N_CHIPS], "fold_grads")
    return d_in, d_uq.reshape((N_CHIPS,) + SH_UQ), d_ukv, d_out.reshape((N_CHIPS,) + SH_OUT)


def kernel(x, norm_in, w_in, a_q_norm, a_k_norm, b_cq_norm, b_ckv_norm, w_uq, w_ukv, b_q_norm, b_k_norm, w_out, loss_target, m_norm_in, m_w_in, m_a_q_norm, m_a_k_norm, m_b_cq_norm, m_b_ckv_norm, m_w_uq, m_w_ukv, m_b_q_norm, m_b_k_norm, m_w_out, v_norm_in, v_w_in, v_a_q_norm, v_a_k_norm, v_b_cq_norm, v_b_ckv_norm, v_w_uq, v_w_ukv, v_b_q_norm, v_b_k_norm, v_w_out):
    s_len = x.shape[1]
    xs, ts = x[0], loss_target[0]
    tm = min(256, s_len)
    tq, tk_f = min(512, s_len // 2), min(2048, s_len // 2)
    tq_b, tk_b = min(1024, s_len // 2), min(512, s_len)
    tiles_f = min(4, s_len // tq)
    tiles_b = min(2, s_len // tk_b)

    w_in_ext, w_uq_pad, w_ukv_ext, w_out_full = _ext_weights(
        *_gather_weights((_pad_cols(w_in[0]), w_uq[0], _pad_cols(w_ukv[0]), w_out[0])))
    gains = (norm_in, _spread(a_q_norm, 1, A_DIM, 1, LAY_ROPE_A), _spread(a_k_norm, 1, A_DIM, 1, LAY_ROPE_A), b_cq_norm, b_ckv_norm,
             _spread(b_q_norm, 1, B_QK, 1, LAY_ROPE_B), _spread(b_k_norm, 1, B_QK, 1, LAY_ROPE_B))
    tabs = _rope_tables(s_len, tm)

    (xn_t, gates, pre, qbpre, kbpre, cq_t, ckv_t, qa, ka, va, qb, kb, vb) = _pre(
        xs, tabs, w_in_ext, w_uq_pad, w_ukv_ext, gains, tm)
    o_a, lse_a = _attn_fwd(qa, ka, va, A_GROUP, A_DIM, tq, tk_f, tiles_f, "attn_fwd_a")
    o_b, lse_b = _attn_fwd(qb, kb, vb, 1, B_V, tq, tk_f, tiles_f, "attn_fwd_b")
    y_t, dh, dgate, do_a, do_b, delta, loss_part = _mid(xs, ts, o_a, o_b, gates, w_out_full, min(512, s_len))

    def stat(a):
        return a.reshape(a.shape[0], s_len // tq_b, 1, tq_b)

    dqa, dka, dva = _attn_bwd(qa, ka, va, do_a, stat(lse_a), stat(delta[:A_HEADS]), A_GROUP, tq_b, tk_b, tiles_b, "attn_bwd_a")
    dqb, dkb, dvb = _attn_bwd(qb, kb, vb, do_b, stat(lse_b), stat(delta[A_HEADS:A_HEADS + B_HEADS]), 1, tq_b, tk_b,
                              tiles_b, "attn_bwd_b")
    grad_x, dproj, dqbpre, dkvb, d_small = _post(
        xs, dh, pre, qbpre, kbpre, dgate, dqa, dka, dva, dqb, dkb, dvb, loss_part, tabs,
        w_in_ext, w_uq_pad, w_ukv_ext, gains, tm)

    ts_w = min(2048, s_len)
    d_in_ext = _grad_w(xn_t, dproj, N_EXT // 5, min(2 * ts_w, s_len), "grad_w_in")
    d_out_full = _grad_w(y_t, dh, 512, ts_w, "grad_w_out")
    d_uq_pad, d_ukv_ext = _grad_w_pairs([(cq_t, dqbpre), (ckv_t, dkvb)], ts_w, "grad_w_mla")

    g_in_p, g_uq, g_ukv_p, g_out, g_small = _reduce_grads(_fold_grads(d_in_ext, d_uq_pad, d_ukv_ext, d_out_full), d_small)
    g_in, g_ukv = g_in_p[:, :SH_IN[1]], g_ukv_p[:, :SH_UKV[1]]
    d_in, nm_in, nv_in = (a.T for a in _adamw_rows(w_in[0].T, g_in_p.T[:SH_IN[1]], m_w_in[0].T, v_w_in[0].T, SH_IN[1] // 7))
    rest = _adamw_rest(
        [(w_uq[0], g_uq, m_w_uq[0], v_w_uq[0]), (w_ukv[0], g_ukv, m_w_ukv[0], v_w_ukv[0]),
         (w_out[0], g_out, m_w_out[0], v_w_out[0])],
        [(norm_in, m_norm_in, v_norm_in), (a_q_norm, m_a_q_norm, v_a_q_norm), (a_k_norm, m_a_k_norm, v_a_k_norm),
         (b_cq_norm, m_b_cq_norm, v_b_cq_norm), (b_ckv_norm, m_b_ckv_norm, v_b_ckv_norm),
         (b_q_norm, m_b_q_norm, v_b_q_norm), (b_k_norm, m_b_k_norm, v_b_k_norm)], g_small)
    (d_uq, nm_uq, nv_uq), (d_ukv, nm_ukv, nv_ukv), (d_out, nm_out, nv_out) = (rest[3 * i:3 * i + 3] for i in range(3))
    sm = [rest[9 + 4 * i:9 + 4 * i + 4] for i in range(7)]

    def leaves(k, p_in, p_uq, p_ukv, p_out):
        return [sm[SM_IN][k], p_in[None], sm[SM_AQ][k], sm[SM_AK][k], sm[SM_CQ][k], sm[SM_CKV][k], p_uq[None], p_ukv[None],
                sm[SM_BQ][k], sm[SM_BK][k], p_out[None]]

    return (g_small[SM_LOSS, 0], grad_x[None], *leaves(0, g_in, g_uq, g_ukv, g_out), *leaves(1, d_in, d_uq, d_ukv, d_out),
            *leaves(2, nm_in, nm_uq, nm_ukv, nm_out), *leaves(3, nv_in, nv_uq, nv_ukv, nv_out))
```

```python
import jax
import jax.numpy as jnp
import numpy as np
from jax import lax
from jax.experimental import pallas as pl
from jax.experimental.pallas import tpu as pltpu

F32 = jnp.float32
BF16 = jnp.bfloat16
MESH = pl.DeviceIdType.MESH

D_MODEL = 1024
GRID_W = 64
ROPE_THETA = 10000.0
EPS = 1e-6
A_HEADS, A_KV, A_DIM = 8, 2, 64
A_GROUP = A_HEADS // A_KV
B_HEADS, B_NOPE, B_ROPE, B_V = 4, 64, 32, 128
B_QK = B_NOPE + B_ROPE
B_Q_RANK, B_KV_RANK = 384, 256
N_IN = 2464
SCALE_A = 1.0 / float(np.sqrt(A_DIM))
SCALE_B = 1.0 / float(np.sqrt(B_QK))
LOG2E = float(np.log2(np.e))
LN2 = float(np.log(2.0))
ADAM_LR, ADAM_B1, ADAM_B2, ADAM_EPS, ADAM_WD, ADAM_STEP = 0.001, 0.9, 0.999, 1e-08, 0.01, 10

LANE = 128
VMEM_BYTES = 64 * 1024 * 1024
VMEM_LIMIT = VMEM_BYTES - 8 * 1024 * 1024
VMEM_MID = 48 * 1024 * 1024
VMEM_SMALL = 32 * 1024 * 1024

QA0 = 0
KA0 = QA0 + A_HEADS * LANE
VA0 = KA0 + A_KV * LANE
GA0 = VA0 + A_KV * LANE
GB0 = GA0 + A_HEADS * A_DIM
CQ0 = GB0 + B_HEADS * LANE
CKV0 = CQ0 + B_Q_RANK
KR0 = CKV0 + B_KV_RANK
N_EXT = KR0 + LANE
N_GATE = (A_HEADS + B_HEADS) * LANE
N_GATE_C = A_HEADS * A_DIM + B_HEADS * LANE
DELTA_ROWS = 16
N_PRE = KA0 + A_KV * LANE + B_Q_RANK + B_KV_RANK

ROT = LANE // 2
_QA = A_DIM // 4
_QB = B_ROPE // 4
LAY_PLAIN_A = ((0, A_DIM, 0),)
LAY_ROPE_A = ((0, _QA, 0), (2 * _QA, _QA, _QA), (_QA, _QA, ROT), (3 * _QA, _QA, ROT + _QA))
LAY_KR = ((0, _QB, 0), (2 * _QB, _QB, _QB), (_QB, _QB, ROT), (3 * _QB, _QB, ROT + _QB))
LAY_NOPE = ((0, B_NOPE // 2, 2 * _QB), (B_NOPE // 2, B_NOPE // 2, ROT + 2 * _QB))
LAY_ROPE_B = LAY_NOPE + tuple((B_NOPE + a, n, at) for a, n, at in LAY_KR)

N_CHIPS = 4
SH_IN = (D_MODEL, N_IN // N_CHIPS)
SH_UQ = (B_Q_RANK // N_CHIPS, B_HEADS * B_QK)
SH_UKV = (B_KV_RANK, B_HEADS * (B_NOPE + B_V) // N_CHIPS)
SH_OUT = (D_MODEL // N_CHIPS, D_MODEL)
SM_ROWS, SM_W = 16, D_MODEL
SM_IN, SM_AQ, SM_AK, SM_CQ, SM_CKV, SM_BQ, SM_BK, SM_LOSS = range(8)
F32_ROWS, BF16_ROWS = 8, 16


def _pallas(body, **kw):
    return pl.pallas_call(body, **kw)


def _params(sem=None, vmem=None):
    return pltpu.CompilerParams(dimension_semantics=sem, vmem_limit_bytes=vmem)


def _rms_fwd(x, g, n):
    r = lax.rsqrt(jnp.sum(x * x, axis=-1, keepdims=True) * (1.0 / n) + EPS)
    return x * r * g


def _rms_bwd(dy, x, g, n):
    u = dy * g
    r = lax.rsqrt(jnp.sum(x * x, axis=-1, keepdims=True) * (1.0 / n) + EPS)
    ux = jnp.sum(u * x, axis=-1, keepdims=True)
    xhat = x * r
    dx = r * (u - xhat * (r * ux * (1.0 / n)))
    return dx, dy * xhat


def _rope_fwd(y, cos, sin):
    return y * cos + pltpu.roll(y, ROT, 1) * sin


def _rope_bwd(d, cos, sin):
    return d * cos - pltpu.roll(d, ROT, 1) * sin


def _token_tables(refs):
    out = []
    for r_ref, c_ref in zip(refs[0::2], refs[1::2]):
        r, c = r_ref[...], c_ref[...]
        out.append(jnp.concatenate([r[k:k + 1, :] + c for k in range(r.shape[0])], axis=0))
    return out


def _lanes_of(lane, layout):
    m = None
    for _, n, at in layout:
        seg = (lane >= at) & (lane < at + n)
        m = seg if m is None else (m | seg)
    return m


def _unspread_row(v, layout):
    v8 = jnp.broadcast_to(v, (F32_ROWS, LANE))
    lane = lax.broadcasted_iota(jnp.int32, v8.shape, 1)
    out = jnp.zeros_like(v8)
    for a, n, at in layout:
        moved = v8 if a == at else pltpu.roll(v8, (a - at) % LANE, 1)
        out = jnp.where((lane >= a) & (lane < a + n), moved, out)
    return out[0:1, :]


def _nt(a, b):
    return lax.dot_general(a, b, (((1,), (1,)), ((), ())), preferred_element_type=F32)


def _tn(a, b):
    return lax.dot_general(a, b, (((0,), (0,)), ((), ())), preferred_element_type=F32)


def _nn(a, b):
    return jnp.dot(a, b, preferred_element_type=F32)


def _block_rows(i, size):
    if isinstance(i, int):
        return pl.ds(i * size, size)
    return pl.ds(pl.multiple_of(i * size, size), size)


MAX_STATIC_BLOCKS = 32


def _three_stage(n, first, second, third):
    assert n >= 2 and n % 2 == 0
    first(0, 0)
    first(1, 1)
    second(0, 0)
    if n <= MAX_STATIC_BLOCKS:
        for i in range(1, n - 1):
            first(i + 1, (i + 1) % 2)
            second(i, i % 2)
            third(i - 1, (i - 1) % 2)
    else:
        def pair(t, carry):
            i = 2 * t + 1
            first(i + 1, 0)
            second(i, 1)
            third(i - 1, 0)
            first(i + 2, 1)
            second(i + 1, 0)
            third(i, 1)
            return carry

        lax.fori_loop(0, (n - 2) // 2, pair, 0)
    second(n - 1, 1)
    third(n - 2, 0)
    third(n - 1, 1)


def _full(shape):
    return pl.BlockSpec(shape, lambda *_: (0,) * len(shape))


def _table_specs(tm):
    return [pl.BlockSpec((None, tm // GRID_W, LANE), lambda i: (i, 0, 0)), _full((GRID_W, LANE))] * 4


def _resident(shape):
    return pl.BlockSpec(shape, lambda *_: (0,) * len(shape), pipeline_mode=pl.Buffered(1))


def _gather_weights(shards):
    n = len(shards)
    halves = [w.shape[0] // 2 for w in shards]

    def body(*refs):
        w_refs, out_refs, (send_sems, recv_sems) = refs[:n], refs[n:2 * n], refs[2 * n:]
        x, y, c = lax.axis_index("x"), lax.axis_index("y"), lax.axis_index("c")
        sibling = (x, y, 1 - c)
        chips = [(1 - x, y), (x, 1 - y), (1 - x, 1 - y)]
        me = 2 * x + y

        def copy(a, k, j, hc, to):
            part = out_refs[a].at[j, pl.ds(pl.multiple_of(hc * halves[a], BF16_ROWS), halves[a]), :]
            return pltpu.make_async_remote_copy(
                src_ref=part, dst_ref=part, send_sem=send_sems.at[6 * a + k], recv_sem=recv_sems.at[6 * a + k],
                device_id=to, device_id_type=MESH)

        started = []
        for a in range(n):
            out_refs[a][me] = w_refs[a][...].astype(BF16)
            for k, chip in enumerate(chips):
                started.append(copy(a, k, me, c, (*chip, c)))
                started[-1].start()
        for k, chip in enumerate(chips):
            for a in range(n):
                copy(a, k, 2 * chip[0] + chip[1], c, (*chip, c)).wait_recv()
                started.append(copy(a, 3 + k, 2 * chip[0] + chip[1], c, sibling))
                started[-1].start()
        for k, chip in enumerate(chips):
            for a in range(n):
                copy(a, 3 + k, 2 * chip[0] + chip[1], 1 - c, sibling).wait_recv()
        for cp in started:
            cp.wait_send()

    return _pallas(
        body, name="gather_weights",
        out_shape=[jax.ShapeDtypeStruct((N_CHIPS,) + w.shape, BF16) for w in shards],
        in_specs=[pl.BlockSpec(memory_space=pltpu.VMEM)] * n,
        out_specs=[pl.BlockSpec(memory_space=pltpu.VMEM)] * n,
        scratch_shapes=[pltpu.SemaphoreType.DMA((6 * n,)), pltpu.SemaphoreType.DMA((6 * n,))],
        compiler_params=_params(vmem=VMEM_SMALL),
    )(*shards)


def _reduce_grads(parts, small):
    n_big = len(parts)
    n = n_big + 1
    shapes = [p.shape[1:] for p in parts] + [small.shape]
    halves = [sh[0] // 2 for sh in shapes]

    def body(*refs):
        p_refs, out_refs, rec_a, rec_b = refs[:n], refs[n:2 * n], refs[2 * n:3 * n], refs[3 * n:4 * n]
        send_b = refs[4 * n:4 * n + n_big]
        sa_send, sa_recv, sb_send, sb_recv, sc_send, sc_recv = refs[4 * n + n_big:]
        x, y, c = lax.axis_index("x"), lax.axis_index("y"), lax.axis_index("c")
        sibling = (x, y, 1 - c)
        me = 2 * x + y

        def rows(a, hc):
            return pl.ds(pl.multiple_of(hc * halves[a], F32_ROWS), halves[a])

        def partial(a, j, hc):
            return p_refs[a].at[j, rows(a, hc), :] if a < n_big else p_refs[a].at[rows(a, hc), :]

        def copy_a(a, j):
            return pltpu.make_async_remote_copy(
                src_ref=partial(a, j, 1 - c), dst_ref=rec_a[a].at[j],
                send_sem=sa_send.at[N_CHIPS * a + j], recv_sem=sa_recv.at[N_CHIPS * a + j],
                device_id=sibling, device_id_type=MESH)

        def copy_b(a, r):
            j = me ^ r
            k = (N_CHIPS - 1) * a + r - 1
            return pltpu.make_async_remote_copy(
                src_ref=(send_b[a] if a < n_big else rec_a[a]).at[j], dst_ref=rec_b[a].at[r],
                send_sem=sb_send.at[k], recv_sem=sb_recv.at[k], device_id=(j // 2, j % 2, c), device_id_type=MESH)

        def copy_c(a):
            return pltpu.make_async_remote_copy(
                src_ref=out_refs[a].at[rows(a, c), :], dst_ref=out_refs[a].at[rows(a, c), :],
                send_sem=sc_send.at[a], recv_sem=sc_recv.at[a], device_id=sibling, device_id_type=MESH)

        for a in range(n):
            for j in range(N_CHIPS):
                copy_a(a, j).start()
        for r in range(1, N_CHIPS):
            j = me ^ r
            for a in range(n):
                copy_a(a, j).wait_recv()
                chip_part = rec_a[a][j] + partial(a, j, c)[...]
                if a < n_big:
                    send_b[a][j] = chip_part.astype(BF16)
                else:
                    rec_a[a][j] = chip_part
                copy_b(a, r).start()
        for a in range(n):
            copy_a(a, me).wait_recv()
            rec_b[a][0] = (rec_a[a][me] + partial(a, me, c)[...]).astype(rec_b[a].dtype)
        for a in range(n):
            for r in range(1, N_CHIPS):
                copy_b(a, r).wait_recv()
            total = rec_b[a][me].astype(F32)
            for j in range(1, N_CHIPS):
                total = total + rec_b[a][j ^ me].astype(F32)
            out_refs[a][rows(a, c), :] = total
            copy_c(a).start()
        for a in range(n):
            copy_c(a).wait_recv()
        for a in range(n):
            for j in range(N_CHIPS):
                copy_a(a, j).wait_send()
            for r in range(1, N_CHIPS):
                copy_b(a, r).wait_send()
            copy_c(a).wait_send()

    dma = pltpu.SemaphoreType.DMA
    return _pallas(
        body, name="reduce_grads",
        out_shape=[jax.ShapeDtypeStruct(sh, F32) for sh in shapes],
        in_specs=[pl.BlockSpec(memory_space=pltpu.VMEM)] * n,
        out_specs=[pl.BlockSpec(memory_space=pltpu.VMEM)] * n,
        scratch_shapes=[pltpu.VMEM((N_CHIPS, h) + sh[1:], F32) for h, sh in zip(halves, shapes)]
                       + [pltpu.VMEM((N_CHIPS, h) + sh[1:], BF16 if a < n_big else F32)
                          for a, (h, sh) in enumerate(zip(halves, shapes))]
                       + [pltpu.VMEM((N_CHIPS, h) + sh[1:], BF16) for h, sh in zip(halves[:n_big], shapes[:n_big])]
                       + [dma((N_CHIPS * n,)), dma((N_CHIPS * n,)), dma(((N_CHIPS - 1) * n,)), dma(((N_CHIPS - 1) * n,)),
                          dma((n,)), dma((n,))],
        compiler_params=_params(vmem=VMEM_LIMIT),
    )(*parts, small)


def _pre(x, tabs, w_in_ext, w_uq_pad, w_ukv_ext, gains, tm):
    s_len = x.shape[0]
    nt = s_len // tm

    def body(x_ref, car_ref, cac_ref, sar_ref, sac_ref, cbr_ref, cbc_ref, sbr_ref, sbc_ref, win_ref, wuq_ref, wukv_ref,
             gin_ref, gaq_ref, gak_ref, gcq_ref, gckv_ref, gbq_ref, gbk_ref,
             xn_ref, gates_ref, pre_ref, qbpre_ref, kbpre_ref, cq_ref, ckv_ref,
             qa_ref, ka_ref, va_ref, qb_ref, kb_ref, vb_ref, proj):
        xn = _rms_fwd(x_ref[...], gin_ref[...], D_MODEL)
        xn_ref[...] = jnp.transpose(xn).astype(BF16)
        xb = xn.astype(BF16)
        pre_ref[:, 0:VA0] = _nn(xb, win_ref[:, 0:VA0])
        gates_ref[...] = _nn(xb, win_ref[:, GA0:GA0 + N_GATE_C])
        pre_ref[:, VA0:N_PRE] = _nn(xb, win_ref[:, CQ0:KR0])
        proj[...] = _nn(xb, win_ref[:, VA0:GA0])
        kr = _nn(xb, win_ref[:, KR0:N_EXT])
        ca, sa, cb, sb = _token_tables((car_ref, cac_ref, sar_ref, sac_ref, cbr_ref, cbc_ref, sbr_ref, sbc_ref))
        lane = lax.broadcasted_iota(jnp.int32, (tm, LANE), 1)
        for h in range(A_HEADS):
            yq = _rms_fwd(pre_ref[:, QA0 + LANE * h:QA0 + LANE * (h + 1)], gaq_ref[...], A_DIM)
            qa_ref[h] = (_rope_fwd(yq, ca, sa) * (SCALE_A * LOG2E)).astype(BF16)
        for h in range(A_KV):
            yk = _rms_fwd(pre_ref[:, KA0 + LANE * h:KA0 + LANE * (h + 1)], gak_ref[...], A_DIM)
            ka_ref[h] = _rope_fwd(yk, ca, sa).astype(BF16)
            va_ref[h] = jnp.where(lane == A_DIM, 1.0, proj[:, LANE * h:LANE * (h + 1)]).astype(BF16)
        cq = _rms_fwd(pre_ref[:, VA0:VA0 + B_Q_RANK], gcq_ref[...], B_Q_RANK)
        cq_ref[...] = jnp.transpose(cq).astype(BF16)
        qbpre_ref[...] = _nn(cq.astype(BF16), wuq_ref[...])
        ckv = _rms_fwd(pre_ref[:, VA0 + B_Q_RANK:N_PRE], gckv_ref[...], B_KV_RANK)
        ckv_ref[...] = jnp.transpose(ckv).astype(BF16)
        kvb = _nn(ckv.astype(BF16), wukv_ref[...])
        for h in range(B_HEADS):
            yq = _rms_fwd(qbpre_ref[:, LANE * h:LANE * (h + 1)], gbq_ref[...], B_QK)
            qb_ref[h] = (_rope_fwd(yq, cb, sb) * (SCALE_B * LOG2E)).astype(BF16)
            kp = kvb[:, LANE * h:LANE * (h + 1)] + kr
            kbpre_ref[:, LANE * h:LANE * (h + 1)] = kp
            kb_ref[h] = _rope_fwd(_rms_fwd(kp, gbk_ref[...], B_QK), cb, sb).astype(BF16)
            vb_ref[h, :, 0:LANE] = kvb[:, B_HEADS * LANE + LANE * h:B_HEADS * LANE + LANE * (h + 1)].astype(BF16)
            vb_ref[h, :, LANE:2 * LANE] = jnp.where(lane == 0, 1.0, 0.0).astype(BF16)

    row = lambda w: pl.BlockSpec((tm, w), lambda i: (i, 0))
    col = lambda w: pl.BlockSpec((w, tm), lambda i: (0, i))
    heads = lambda n: pl.BlockSpec((n, tm, LANE), lambda i: (0, i, 0))
    hs = lambda n: jax.ShapeDtypeStruct((n, s_len, LANE), BF16)
    return _pallas(
        body, name="pre", grid=(nt,),
        in_specs=[row(D_MODEL)] + _table_specs(tm)
                 + [_resident(w_in_ext.shape), _resident(w_uq_pad.shape), _resident(w_ukv_ext.shape)]
                 + [_full(g.shape) for g in gains],
        out_specs=[col(D_MODEL), row(N_GATE_C), row(N_PRE), row(B_HEADS * LANE), row(B_HEADS * LANE),
                   col(B_Q_RANK), col(B_KV_RANK),
                   heads(A_HEADS), heads(A_KV), heads(A_KV), heads(B_HEADS), heads(B_HEADS),
                   pl.BlockSpec((B_HEADS, tm, 2 * LANE), lambda i: (0, i, 0))],
        out_shape=[jax.ShapeDtypeStruct((D_MODEL, s_len), BF16), jax.ShapeDtypeStruct((s_len, N_GATE_C), F32),
                   jax.ShapeDtypeStruct((s_len, N_PRE), F32), jax.ShapeDtypeStruct((s_len, B_HEADS * LANE), F32),
                   jax.ShapeDtypeStruct((s_len, B_HEADS * LANE), F32),
                   jax.ShapeDtypeStruct((B_Q_RANK, s_len), BF16), jax.ShapeDtypeStruct((B_KV_RANK, s_len), BF16),
                   hs(A_HEADS), hs(A_KV), hs(A_KV), hs(B_HEADS), hs(B_HEADS),
                   jax.ShapeDtypeStruct((B_HEADS, s_len, 2 * LANE), BF16)],
        scratch_shapes=[pltpu.VMEM((tm, A_KV * LANE), F32)],
        compiler_params=_params(("parallel",), VMEM_LIMIT),
    )(x, *tabs, w_in_ext, w_uq_pad, w_ukv_ext, *gains)


def _attn_fwd(q, k, v, group, l_col, tq, tk, tiles, name):
    n_heads, s_len, _ = q.shape
    v_w = v.shape[2]
    nk = s_len // tk

    def body(q_ref, k_ref, v_ref, o_ref, lse_ref, s_buf, p_buf, a_buf, m_ref, acc_ref):
        def scores(g, slot):
            s_buf[slot] = _nt(q_ref[_block_rows(g // nk, tq), :], k_ref[_block_rows(g % nk, tk), :])

        def softmax(g, slot):
            t = g // nk
            s = s_buf[slot]
            m_old = m_ref[t]
            m_new = jnp.maximum(m_old, jnp.max(s, axis=-1, keepdims=True))
            m_ref[t] = m_new
            a_buf[slot] = jnp.exp2(m_old - m_new)
            p_buf[slot] = jnp.exp2(s - jnp.tile(m_new, (1, tk // LANE))).astype(BF16)

        def values(g, slot):
            t = g // nk
            pv = _nn(p_buf[slot], v_ref[_block_rows(g % nk, tk), :])
            for c in range(0, v_w, LANE):
                acc_ref[t, :, c:c + LANE] = a_buf[slot] * acc_ref[t, :, c:c + LANE] + pv[:, c:c + LANE]

        m_ref[...] = jnp.full(m_ref.shape, -1e30, F32)
        acc_ref[...] = jnp.zeros(acc_ref.shape, F32)
        _three_stage(tiles * nk, scores, softmax, values)
        for t in range(tiles):
            l = acc_ref[t, :, l_col:l_col + 1]
            o = acc_ref[t, :, 0:LANE] * (1.0 / l)
            if l_col < LANE:
                lane = lax.broadcasted_iota(jnp.int32, o.shape, 1)
                o = jnp.where(lane == l_col, 0.0, o)
            o_ref[t * tq:(t + 1) * tq, :] = o
            lse_ref[t] = jnp.transpose(m_ref[t] + jnp.log2(jnp.broadcast_to(l, (tq, LANE))))[0:1, :]

    return _pallas(
        body, name=name, grid=(n_heads, s_len // (tiles * tq)),
        in_specs=[pl.BlockSpec((None, tiles * tq, LANE), lambda h, i: (h, i, 0)),
                  pl.BlockSpec((None, s_len, LANE), lambda h, i: (h // group, 0, 0)),
                  pl.BlockSpec((None, s_len, v_w), lambda h, i: (h // group, 0, 0))],
        out_specs=[pl.BlockSpec((None, tiles * tq, LANE), lambda h, i: (h, i, 0)),
                   pl.BlockSpec((None, tiles, 1, tq), lambda h, i: (h, i, 0, 0))],
        out_shape=[jax.ShapeDtypeStruct((n_heads, s_len, LANE), F32),
                   jax.ShapeDtypeStruct((n_heads, s_len // tq, 1, tq), F32)],
        scratch_shapes=[pltpu.VMEM((2, tq, tk), F32), pltpu.VMEM((2, tq, tk), BF16), pltpu.VMEM((2, tq, LANE), F32),
                        pltpu.VMEM((tiles, tq, LANE), F32), pltpu.VMEM((tiles, tq, v_w), F32)],
        compiler_params=_params(("parallel", "parallel"), VMEM_MID),
    )(q, k, v)


def _mid(x, target, o_a, o_b, gates, w_out, tm):
    s_len = x.shape[0]
    nt = s_len // tm
    n_heads = A_HEADS + B_HEADS
    d_mix = w_out.shape[0]
    pairs = A_HEADS // 2

    def body(x_ref, t_ref, oa_ref, ob_ref, g_ref, w_ref,
             yt_ref, dh_ref, dgate_ref, doa_ref, dob_ref, delta_ref, loss_ref, silu_scr, dsilu_scr, y_ref):
        @pl.when(pl.program_id(0) == 0)
        def _():
            loss_ref[...] = jnp.zeros_like(loss_ref)

        def o_of(h):
            return oa_ref[h] if h < A_HEADS else ob_ref[h - A_HEADS]

        lane = lax.broadcasted_iota(jnp.int32, (tm, LANE), 1)

        def gated(h):
            cols = slice(LANE * h, LANE * (h + 1))
            if h < A_HEADS:
                packed = g_ref[:, LANE * (h // 2):LANE * (h // 2 + 1)]
                g = jnp.where(lane < A_DIM, packed if h % 2 == 0 else pltpu.roll(packed, ROT, 1), 0.0)
            else:
                g = g_ref[:, LANE * (pairs + h - A_HEADS):LANE * (pairs + h - A_HEADS + 1)]
            sig = 1.0 / (1.0 + jnp.exp(-g))
            silu = g * sig
            silu_scr[:, cols] = silu
            dsilu_scr[:, cols] = sig * (1.0 + g * (1.0 - sig))
            return o_of(h) * silu

        for c in range(pairs + B_HEADS):
            y = gated(2 * c) + pltpu.roll(gated(2 * c + 1), ROT, 1) if c < pairs else gated(A_HEADS + c - pairs)
            cols = slice(LANE * c, LANE * (c + 1))
            y_ref[:, cols] = y.astype(BF16)
            yt_ref[cols, :] = jnp.transpose(y).astype(BF16)
        err = x_ref[...] + _nn(y_ref[...], w_ref[...]) - t_ref[...]
        sq = jnp.sum(jnp.sum(err * err, axis=-1, keepdims=True), axis=0, keepdims=True)
        loss_ref[...] += jnp.broadcast_to(sq * (0.5 / D_MODEL), loss_ref.shape)
        dh = err * (1.0 / D_MODEL)
        dh_ref[...] = dh
        dy = _nt(dh.astype(BF16), w_ref[...])
        delta = jnp.zeros((tm, LANE), F32)
        held = None
        for h in range(n_heads):
            cols = slice(LANE * h, LANE * (h + 1))
            if h < A_HEADS:
                packed = dy[:, LANE * (h // 2):LANE * (h // 2 + 1)]
                dyh = packed if h % 2 == 0 else pltpu.roll(packed, ROT, 1)
            else:
                dyh = dy[:, LANE * (pairs + h - A_HEADS):LANE * (pairs + h - A_HEADS + 1)]
            oh = o_of(h)
            do = dyh * silu_scr[:, cols]
            dg = dyh * oh * dsilu_scr[:, cols]
            if h >= A_HEADS:
                dgate_ref[:, LANE * (pairs + h - A_HEADS):LANE * (pairs + h - A_HEADS + 1)] = dg.astype(BF16)
            elif h % 2 == 0:
                held = dg
            else:
                dgate_ref[:, LANE * (h // 2):LANE * (h // 2 + 1)] = (held + pltpu.roll(dg, ROT, 1)).astype(BF16)
            delta = jnp.where(lane == h, jnp.sum(do * oh, axis=-1, keepdims=True), delta)
            if h < A_HEADS:
                doa_ref[h] = do.astype(BF16)
            else:
                dob_ref[h - A_HEADS] = do.astype(BF16)
        delta_ref[...] = jnp.transpose(delta)[0:DELTA_ROWS, :]

    row = lambda w: pl.BlockSpec((tm, w), lambda i: (i, 0))
    heads = lambda n, w=LANE: pl.BlockSpec((n, tm, w), lambda i: (0, i, 0))
    return _pallas(
        body, name="mid", grid=(nt,),
        in_specs=[row(D_MODEL), row(D_MODEL), heads(A_HEADS), heads(B_HEADS), row(N_GATE_C), _resident(w_out.shape)],
        out_specs=[pl.BlockSpec((d_mix, tm), lambda i: (0, i)), row(D_MODEL), row(N_GATE_C), heads(A_HEADS), heads(B_HEADS),
                   pl.BlockSpec((DELTA_ROWS, tm), lambda i: (0, i)),
                   _full((8, LANE))],
        out_shape=[jax.ShapeDtypeStruct((d_mix, s_len), BF16), jax.ShapeDtypeStruct((s_len, D_MODEL), F32),
                   jax.ShapeDtypeStruct((s_len, N_GATE_C), BF16),
                   jax.ShapeDtypeStruct((A_HEADS, s_len, LANE), BF16), jax.ShapeDtypeStruct((B_HEADS, s_len, LANE), BF16),
                   jax.ShapeDtypeStruct((DELTA_ROWS, s_len), F32), jax.ShapeDtypeStruct((8, LANE), F32)],
        scratch_shapes=[pltpu.VMEM((tm, N_GATE), F32), pltpu.VMEM((tm, N_GATE), F32), pltpu.VMEM((tm, d_mix), BF16)],
        compiler_params=_params(("arbitrary",), VMEM_LIMIT),
    )(x, target, o_a, o_b, gates, w_out)


def _attn_bwd(q, k, v, do, lse, delta, group, tq, tk, tiles, name):
    n_heads, s_len, _ = q.shape
    nq = s_len // tq

    def body(q_ref, do_ref, lse_ref, delta_ref, k_ref, v_ref, dq_ref, dk_ref, dv_ref, s_buf, dp_buf, p_buf, ds_buf):
        @pl.when(pl.program_id(1) == 0)
        def _():
            dq_ref[...] = jnp.zeros_like(dq_ref)

        dk_ref[...] = jnp.zeros_like(dk_ref)
        dv_ref[...] = jnp.zeros_like(dv_ref)

        def keys(g):
            return _block_rows(g // nq, tk)

        def queries(g):
            return _block_rows(g % nq, tq)

        def scores(g, slot):
            s_buf[slot] = _nt(k_ref[keys(g), :], q_ref[queries(g), :])
            dp_buf[slot] = _nt(v_ref[keys(g), :], do_ref[queries(g), :])

        def elementwise(g, slot):
            p = jnp.exp2(s_buf[slot] - lse_ref[g % nq])
            p_buf[slot] = p.astype(BF16)
            ds_buf[slot] = (p * (dp_buf[slot] - delta_ref[g % nq])).astype(BF16)

        def grads(g, slot):
            dv_ref[keys(g), :] += _nn(p_buf[slot], do_ref[queries(g), :])
            dk_ref[keys(g), :] += _nn(ds_buf[slot], q_ref[queries(g), :])
            dq_ref[queries(g), :] += _tn(ds_buf[slot], k_ref[keys(g), :])

        _three_stage(tiles * nq, scores, elementwise, grads)

    whole = lambda: pl.BlockSpec((None, s_len, LANE), lambda h, j: (h, 0, 0))
    stat = lambda: pl.BlockSpec((None, nq, 1, tq), lambda h, j: (h, 0, 0, 0))
    kvb = lambda: pl.BlockSpec((None, tiles * tk, LANE), lambda h, j: (h // group, j, 0))
    outb = lambda: pl.BlockSpec((None, tiles * tk, LANE), lambda h, j: (h, j, 0))
    shape = jax.ShapeDtypeStruct((n_heads, s_len, LANE), F32)
    return _pallas(
        body, name=name, grid=(n_heads, s_len // (tiles * tk)),
        in_specs=[whole(), whole(), stat(), stat(), kvb(), kvb()],
        out_specs=[whole(), outb(), outb()],
        out_shape=[shape, shape, shape],
        scratch_shapes=[pltpu.VMEM((2, tk, tq), F32), pltpu.VMEM((2, tk, tq), F32),
                        pltpu.VMEM((2, tk, tq), BF16), pltpu.VMEM((2, tk, tq), BF16)],
        compiler_params=_params(("parallel", "arbitrary"), VMEM_MID),
    )(q, do, lse, delta, k, v)


def _post(x, dh, pre, qbpre, kbpre, dgate, dqa, dka, dva, dqb, dkb, dvb, loss_part, tabs,
          w_in_ext, w_uq_pad, w_ukv_ext, gains, tm):
    s_len = x.shape[0]
    nt = s_len // tm

    def body(x_ref, dh_ref, pre_ref, qbpre_ref, kbpre_ref, dgate_ref,
             dqa_ref, dka_ref, dva_ref, dqb_ref, dkb_ref, dvb_ref, loss_ref,
             car_ref, cac_ref, sar_ref, sac_ref, cbr_ref, cbc_ref, sbr_ref, sbc_ref, win_ref, wuq_ref, wukv_ref,
             gin_ref, gaq_ref, gak_ref, gcq_ref, gckv_ref, gbq_ref, gbk_ref,
             gx_ref, dproj_ref, dqbpre_ref, dkvb_ref, dsm_ref):
        @pl.when(pl.program_id(0) == 0)
        def _():
            dsm_ref[...] = jnp.zeros_like(dsm_ref)
            dsm_ref[SM_LOSS:SM_LOSS + 1, 0:LANE] = loss_ref[0:1, :]

        def add_small(r, dg):
            dsm_ref[r:r + 1, 0:dg.shape[1]] += dg

        def tok_sum(a):
            return jnp.sum(a, axis=0, keepdims=True)

        ca, sa, cb, sb = _token_tables((car_ref, cac_ref, sar_ref, sac_ref, cbr_ref, cbc_ref, sbr_ref, sbc_ref))
        lane = lax.broadcasted_iota(jnp.int32, (tm, LANE), 1)

        nope_lanes = _lanes_of(lane, LAY_NOPE)

        def back(c0, c1):
            return _nt(dproj_ref[:, c0:c1], win_ref[:, c0:c1])

        dproj_ref[:, GA0:GA0 + N_GATE_C] = dgate_ref[...]
        dxn = back(GA0, GA0 + N_GATE_C)
        dg = jnp.zeros((1, LANE), F32)
        for h in range(A_HEADS):
            dn = _rope_bwd(dqa_ref[h] * SCALE_A, ca, sa)
            dx, dgr = _rms_bwd(dn, pre_ref[:, QA0 + LANE * h:QA0 + LANE * (h + 1)], gaq_ref[...], A_DIM)
            dproj_ref[:, QA0 + LANE * h:QA0 + LANE * (h + 1)] = dx.astype(BF16)
            dg = dg + tok_sum(dgr)
        add_small(SM_AQ, _unspread_row(dg, LAY_ROPE_A))
        dxn = dxn + back(QA0, KA0)
        dg = jnp.zeros((1, LANE), F32)
        for h in range(A_KV):
            dk = dka_ref[A_GROUP * h]
            dv = dva_ref[A_GROUP * h]
            for g in range(1, A_GROUP):
                dk = dk + dka_ref[A_GROUP * h + g]
                dv = dv + dva_ref[A_GROUP * h + g]
            dn = _rope_bwd(dk * LN2, ca, sa)
            dx, dgr = _rms_bwd(dn, pre_ref[:, KA0 + LANE * h:KA0 + LANE * (h + 1)], gak_ref[...], A_DIM)
            dproj_ref[:, KA0 + LANE * h:KA0 + LANE * (h + 1)] = dx.astype(BF16)
            dproj_ref[:, VA0 + LANE * h:VA0 + LANE * (h + 1)] = dv.astype(BF16)
            dg = dg + tok_sum(dgr)
        add_small(SM_AK, _unspread_row(dg, LAY_ROPE_A))
        dxn = dxn + back(KA0, GA0)
        dg = jnp.zeros((1, LANE), F32)
        for h in range(B_HEADS):
            cols = slice(LANE * h, LANE * (h + 1))
            dn = _rope_bwd(dqb_ref[h] * SCALE_B, cb, sb)
            dx, dgr = _rms_bwd(dn, qbpre_ref[:, cols], gbq_ref[...], B_QK)
            dqbpre_ref[:, cols] = dx.astype(BF16)
            dg = dg + tok_sum(dgr)
        add_small(SM_BQ, _unspread_row(dg, LAY_ROPE_B))
        dcq = _nt(dqbpre_ref[...], wuq_ref[...])
        dx, dgr = _rms_bwd(dcq, pre_ref[:, VA0:VA0 + B_Q_RANK], gcq_ref[...], B_Q_RANK)
        dproj_ref[:, CQ0:CQ0 + B_Q_RANK] = dx.astype(BF16)
        add_small(SM_CQ, tok_sum(dgr))
        dxn = dxn + back(CQ0, CKV0)
        dg = jnp.zeros((1, LANE), F32)
        dkr = jnp.zeros((tm, LANE), F32)
        for h in range(B_HEADS):
            cols = slice(LANE * h, LANE * (h + 1))
            dn = _rope_bwd(dkb_ref[h] * LN2, cb, sb)
            dx, dgr = _rms_bwd(dn, kbpre_ref[:, cols], gbk_ref[...], B_QK)
            dkvb_ref[:, cols] = jnp.where(nope_lanes, dx, 0.0).astype(BF16)
            dkvb_ref[:, B_HEADS * LANE + LANE * h:B_HEADS * LANE + LANE * (h + 1)] = dvb_ref[h].astype(BF16)
            dkr = dkr + dx
            dg = dg + tok_sum(dgr)
        add_small(SM_BK, _unspread_row(dg, LAY_ROPE_B))
        dproj_ref[:, KR0:KR0 + LANE] = jnp.where(_lanes_of(lane, LAY_KR), dkr, 0.0).astype(BF16)
        dckv = _nt(dkvb_ref[...], wukv_ref[...])
        dx, dgr = _rms_bwd(dckv, pre_ref[:, VA0 + B_Q_RANK:N_PRE], gckv_ref[...], B_KV_RANK)
        dproj_ref[:, CKV0:CKV0 + B_KV_RANK] = dx.astype(BF16)
        add_small(SM_CKV, tok_sum(dgr))
        dxn = dxn + back(CKV0, N_EXT)
        dx, dgr = _rms_bwd(dxn, x_ref[...], gin_ref[...], D_MODEL)
        gx_ref[...] = dh_ref[...] + dx
        add_small(SM_IN, tok_sum(dgr))

    row = lambda w: pl.BlockSpec((tm, w), lambda i: (i, 0))
    heads = lambda n: pl.BlockSpec((n, tm, LANE), lambda i: (0, i, 0))
    return _pallas(
        body, name="post", grid=(nt,),
        in_specs=[row(D_MODEL), row(D_MODEL), row(N_PRE), row(B_HEADS * LANE), row(B_HEADS * LANE), row(N_GATE_C),
                  heads(A_HEADS), heads(A_HEADS), heads(A_HEADS), heads(B_HEADS), heads(B_HEADS), heads(B_HEADS),
                  _full(loss_part.shape)] + _table_specs(tm)
                 + [_resident(w_in_ext.shape), _resident(w_uq_pad.shape), _resident(w_ukv_ext.shape)]
                 + [_full(g.shape) for g in gains],
        out_specs=[row(D_MODEL), row(N_EXT), row(B_HEADS * LANE), row(2 * B_HEADS * LANE), _full((SM_ROWS, SM_W))],
        out_shape=[jax.ShapeDtypeStruct((s_len, D_MODEL), F32), jax.ShapeDtypeStruct((s_len, N_EXT), BF16),
                   jax.ShapeDtypeStruct((s_len, B_HEADS * LANE), BF16),
                   jax.ShapeDtypeStruct((s_len, 2 * B_HEADS * LANE), BF16),
                   jax.ShapeDtypeStruct((SM_ROWS, SM_W), F32)],
        compiler_params=_params(("arbitrary",), VMEM_LIMIT),
    )(x, dh, pre, qbpre, kbpre, dgate, dqa, dka, dva, dqb, dkb, dvb, loss_part, *tabs,
      w_in_ext, w_uq_pad, w_ukv_ext, *gains)


def _grad_w(a_t, b, tn, ts, name):
    m, s_len = a_t.shape
    n = b.shape[1]

    def body(a_ref, b_ref, o_ref):
        @pl.when(pl.program_id(1) == 0)
        def _():
            o_ref[...] = jnp.zeros_like(o_ref)

        o_ref[...] += _nn(a_ref[...], b_ref[...].astype(BF16))

    return _pallas(
        body, name=name, grid=(n // tn, s_len // ts),
        in_specs=[pl.BlockSpec((m, ts), lambda j, t: (0, t)), pl.BlockSpec((ts, tn), lambda j, t: (t, j))],
        out_specs=pl.BlockSpec((m, tn), lambda j, t: (0, j)),
        out_shape=jax.ShapeDtypeStruct((m, n), F32),
        compiler_params=_params(("parallel", "arbitrary"), VMEM_MID),
    )(a_t, b)


def _grad_w_pairs(pairs, ts, name):
    s_len = pairs[0][0].shape[1]
    n_p = len(pairs)

    def body(*refs):
        for a_ref, b_ref, o_ref in zip(refs[0:2 * n_p:2], refs[1:2 * n_p:2], refs[2 * n_p:]):
            @pl.when(pl.program_id(0) == 0)
            def _():
                o_ref[...] = jnp.zeros_like(o_ref)

            o_ref[...] += _nn(a_ref[...], b_ref[...].astype(BF16))

    in_specs, flat = [], []
    for a_t, b in pairs:
        in_specs += [pl.BlockSpec((a_t.shape[0], ts), lambda t: (0, t)), pl.BlockSpec((ts, b.shape[1]), lambda t: (t, 0))]
        flat += [a_t, b]
    return _pallas(
        body, name=name, grid=(s_len // ts,),
        in_specs=in_specs,
        out_specs=[_full((a_t.shape[0], b.shape[1])) for a_t, b in pairs],
        out_shape=[jax.ShapeDtypeStruct((a_t.shape[0], b.shape[1]), F32) for a_t, b in pairs],
        compiler_params=_params(("arbitrary",), VMEM_MID),
    )(*flat)


def _adam_math(w, g, m, v):
    nm = ADAM_B1 * m + (1.0 - ADAM_B1) * g
    nv = ADAM_B2 * v + (1.0 - ADAM_B2) * (g * g)
    m_hat = nm / (1.0 - ADAM_B1 ** ADAM_STEP)
    v_hat = nv / (1.0 - ADAM_B2 ** ADAM_STEP)
    return -ADAM_LR * (m_hat / (jnp.sqrt(v_hat) + ADAM_EPS) + ADAM_WD * w), nm, nv


def _adamw_rows(w, g, m, v, tr):
    rows, cols = w.shape

    def body(w_ref, g_ref, m_ref, v_ref, d_ref, nm_ref, nv_ref):
        d_ref[...], nm_ref[...], nv_ref[...] = _adam_math(w_ref[...], g_ref[...], m_ref[...], v_ref[...])

    blk = pl.BlockSpec((tr, cols), lambda i: (i, 0))
    shape = jax.ShapeDtypeStruct((rows, cols), F32)
    return _pallas(
        body, name="adamw_w_in", grid=(rows // tr,),
        in_specs=[blk] * 4, out_specs=[blk] * 3, out_shape=[shape] * 3,
        compiler_params=_params(("parallel",), VMEM_SMALL),
    )(w, g, m, v)


def _adamw_rest(bigs, smalls, g_small):
    nb, ns = len(bigs), len(smalls)

    def body(*refs):
        ins, outs = refs[:4 * nb + 3 * ns + 1], refs[4 * nb + 3 * ns + 1:]
        for i in range(nb):
            w_ref, g_ref, m_ref, v_ref = ins[4 * i:4 * i + 4]
            d_ref, nm_ref, nv_ref = outs[3 * i:3 * i + 3]
            d_ref[...], nm_ref[...], nv_ref[...] = _adam_math(w_ref[...], g_ref[...], m_ref[...], v_ref[...])
        gs_ref = ins[-1]
        for i in range(ns):
            w_ref, m_ref, v_ref = ins[4 * nb + 3 * i:4 * nb + 3 * i + 3]
            g_ref, d_ref, nm_ref, nv_ref = outs[3 * nb + 4 * i:3 * nb + 4 * i + 4]
            g = gs_ref[i:i + 1, 0:w_ref.shape[1]]
            g_ref[...] = g
            d_ref[...], nm_ref[...], nv_ref[...] = _adam_math(w_ref[...], g, m_ref[...], v_ref[...])

    flat_in = [a for quad in bigs for a in quad] + [a for tri in smalls for a in tri] + [g_small]
    out_shape = ([jax.ShapeDtypeStruct(q[0].shape, F32) for q in bigs for _ in range(3)]
                 + [jax.ShapeDtypeStruct(t[0].shape, F32) for t in smalls for _ in range(4)])
    return _pallas(
        body, name="adamw_rest",
        in_specs=[pl.BlockSpec(memory_space=pltpu.VMEM)] * len(flat_in),
        out_specs=[pl.BlockSpec(memory_space=pltpu.VMEM)] * len(out_shape),
        out_shape=out_shape,
        compiler_params=_params(vmem=VMEM_SMALL),
    )(*flat_in)


def _place(pieces, n):
    out, at = [], 0
    for lane0, arr in sorted(pieces, key=lambda p: p[0]):
        out += [jnp.zeros((n, lane0 - at), F32), arr]
        at = lane0 + arr.shape[1]
    return jnp.concatenate(out + [jnp.zeros((n, LANE - at), F32)], axis=1)


def _rope_tables(s_len, tm):
    rows = s_len // GRID_W
    row = jnp.arange(rows, dtype=F32)
    col = jnp.arange(GRID_W, dtype=F32)

    def lay(dim, layout, first_dim, ones):
        half = dim // 2
        inv = 1.0 / (ROPE_THETA ** (jnp.arange(0, half, 2, dtype=F32) / half))
        ang_r, ang_c = row[:, None] * inv[None, :], col[:, None] * inv[None, :]
        at = {a - first_dim: lane0 for a, _, lane0 in layout}
        q = dim // 4
        r1, r2, c1, c2 = at[0], at[q], at[2 * q], at[3 * q]
        cos_r = _place([(r1, jnp.cos(ang_r)), (r2, jnp.cos(ang_r))], rows)
        sin_r = _place([(r1, -jnp.sin(ang_r)), (r2, jnp.sin(ang_r))], rows)
        cos_c = _place([(c1, jnp.cos(ang_c)), (c2, jnp.cos(ang_c))] + [(l0, jnp.ones((GRID_W, n), F32)) for _, n, l0 in ones],
                       GRID_W)
        sin_c = _place([(c1, -jnp.sin(ang_c)), (c2, jnp.sin(ang_c))], GRID_W)
        by_block = (s_len // tm, tm // GRID_W, LANE)
        return cos_r.reshape(by_block), cos_c, sin_r.reshape(by_block), sin_c

    return lay(A_DIM, LAY_ROPE_A, 0, ()) + lay(B_ROPE, LAY_KR, 0, LAY_NOPE)


def _spread(w, n_heads, dim, axis, layout):
    w3 = w.reshape(w.shape[:axis] + (n_heads, dim) + w.shape[axis + 1:])
    out, at = [], 0

    def zeros(n):
        return jnp.zeros(w3.shape[:axis + 1] + (n,) + w3.shape[axis + 2:], w.dtype)

    for a0, n, lane0 in sorted(layout, key=lambda seg: seg[2]):
        out += [zeros(lane0 - at), lax.slice_in_dim(w3, a0, a0 + n, axis=axis + 1)]
        at = lane0 + n
    out = jnp.concatenate(out + [zeros(LANE - at)], axis=axis + 1)
    return out.reshape(w.shape[:axis] + (n_heads * LANE,) + w.shape[axis + 1:])


def _head_cols(first, n_heads, dim, layout):
    out = np.full((n_heads * LANE,), -1, np.int32)
    for h in range(n_heads):
        for a0, n, lane0 in layout:
            out[h * LANE + lane0:h * LANE + lane0 + n] = first + h * dim + a0 + np.arange(n)
    return out


def _inverse(src, n):
    dst = np.full((n,), -1, np.int32)
    dst[src[src >= 0]] = np.nonzero(src >= 0)[0]
    return dst


def _column_maps():
    a_w, kv_w = A_HEADS * A_DIM, A_KV * A_DIM
    o_g = a_w + 2 * kv_w
    o_cq = o_g + a_w
    o_kr = o_cq + B_Q_RANK + B_KV_RANK
    src_in = np.concatenate([
        _head_cols(0, A_HEADS, A_DIM, LAY_ROPE_A), _head_cols(a_w, A_KV, A_DIM, LAY_ROPE_A),
        _head_cols(a_w + kv_w, A_KV, A_DIM, LAY_PLAIN_A), np.arange(o_g, o_g + a_w),
        np.arange(o_kr + B_ROPE, N_IN), np.arange(o_cq, o_kr), _head_cols(o_kr, 1, B_ROPE, LAY_KR)]).astype(np.int32)
    src_uq = _head_cols(0, B_HEADS, B_QK, LAY_ROPE_B)
    per = B_NOPE + B_V
    src_ukv = np.concatenate([_head_cols(0, B_HEADS, per, LAY_NOPE),
                              _head_cols(B_NOPE, B_HEADS, per, ((0, B_V, 0),))]).astype(np.int32)
    assert len(src_in) == N_EXT
    return src_in, src_uq, src_ukv


def _round_up(n, m):
    return (n + m - 1) // m * m


def _permute_cols(xs, maps, stacks, name):
    maps = [np.asarray(m, np.int32) for m in maps]
    n_arr = len(xs)

    def block(ref, b):
        if len(ref.shape) == 2:
            return ref.at[:, b * LANE:(b + 1) * LANE]
        per = ref.shape[2] // LANE
        return ref.at[b // per, :, (b % per) * LANE:(b % per + 1) * LANE]

    def body(*refs):
        row = lax.broadcasted_iota(jnp.int32, (LANE, LANE), 0)
        for x_ref, src_ref, o_ref, src in zip(refs[:n_arr], refs[n_arr:2 * n_arr], refs[2 * n_arr:], maps):
            for c in range(len(src) // LANE):
                want = src[c * LANE:(c + 1) * LANE]
                if want[0] >= 0 and want[0] % LANE == 0 and np.array_equal(want, want[0] + np.arange(LANE)):
                    block(o_ref, c)[...] = block(x_ref, int(want[0]) // LANE)[...]
                    continue
                acc = jnp.zeros((x_ref.shape[-2], LANE), F32)
                for kb in sorted({int(v) // LANE for v in want if v >= 0}):
                    sel = jnp.where(row + kb * LANE == src_ref[:, c * LANE:(c + 1) * LANE], 1.0, 0.0).astype(BF16)
                    part = block(x_ref, kb)[...]
                    if part.dtype == BF16:
                        acc = acc + _nn(part, sel)
                    else:
                        hi = part.astype(BF16)
                        rest = part - hi.astype(F32)
                        mid = rest.astype(BF16)
                        low = (rest - mid.astype(F32)).astype(BF16)
                        acc = acc + ((_nn(hi, sel) + _nn(mid, sel)) + _nn(low, sel))
                block(o_ref, c)[...] = acc.astype(o_ref.dtype)

    def out_shape(x, m, stack):
        rows = x.shape[-2]
        return (rows, len(m)) if stack is None else (stack, rows, len(m) // stack)

    return _pallas(
        body, name=name,
        out_shape=[jax.ShapeDtypeStruct(out_shape(x, m, st), x.dtype) for x, m, st in zip(xs, maps, stacks)],
        compiler_params=_params(vmem=VMEM_MID),
    )(*xs, *[jnp.asarray(m).reshape(1, -1) for m in maps])


def _pad_cols(w):
    return jnp.pad(w, ((0, 0), (0, _round_up(w.shape[1], LANE) - w.shape[1])))


def _in_stack(cols, width):
    cols = np.asarray(cols)
    return np.where(cols < 0, -1, cols // width * _round_up(width, LANE) + cols % width).astype(np.int32)


def _ext_weights(g_in, g_uq, g_ukv, g_out):
    src_in, src_uq, src_ukv = _column_maps()
    w_uq = g_uq.reshape(B_Q_RANK, B_HEADS * B_QK)
    w_out = g_out.reshape(D_MODEL, D_MODEL)
    w_in_ext, w_uq_pad, w_ukv_ext = _permute_cols(
        [g_in, w_uq, g_ukv], [_in_stack(src_in, SH_IN[1]), src_uq, _in_stack(src_ukv, SH_UKV[1])], [None] * 3, "lay_out_weights")
    return w_in_ext, w_uq_pad, w_ukv_ext, w_out


def _fold_grads(d_in_ext, d_uq_pad, d_ukv_ext, d_out):
    src_in, src_uq, src_ukv = _column_maps()

    def back(src, n, width):
        inv = _inverse(src, n)
        wide = _round_up(width, LANE)
        out = np.full((n // width * wide,), -1, np.int32)
        for j in range(n // width):
            out[j * wide:j * wide + width] = inv[j * width:(j + 1) * width]
        return out

    n_uq, n_ukv = B_HEADS * B_QK, B_HEADS * (B_NOPE + B_V)
    d_in, d_uq, d_ukv = _permute_cols(
        [d_in_ext, d_uq_pad, d_ukv_ext], [back(src_in, N_IN, SH_IN[1]), _inverse(src_uq, n_uq), back(src_ukv, n_ukv, SH_UKV[1])],
        [N_CHIPS, None, N_CHIPS], "fold_grads")
    return d_in, d_uq.reshape((N_CHIPS,) + SH_UQ), d_ukv, d_out.reshape((N_CHIPS,) + SH_OUT)


def kernel(x, norm_in, w_in, a_q_norm, a_k_norm, b_cq_norm, b_ckv_norm, w_uq, w_ukv, b_q_norm, b_k_norm, w_out, loss_target, m_norm_in, m_w_in, m_a_q_norm, m_a_k_norm, m_b_cq_norm, m_b_ckv_norm, m_w_uq, m_w_ukv, m_b_q_norm, m_b_k_norm, m_w_out, v_norm_in, v_w_in, v_a_q_norm, v_a_k_norm, v_b_cq_norm, v_b_ckv_norm, v_w_uq, v_w_ukv, v_b_q_norm, v_b_k_norm, v_w_out):
    s_len = x.shape[1]
    xs, ts = x[0], loss_target[0]
    tm = min(256, s_len)
    tq, tk_f = min(512, s_len // 2), min(2048, s_len // 2)
    tq_b, tk_b = min(1024, s_len // 2), min(512, s_len)
    tiles_f = min(4, s_len // tq)
    tiles_b = min(2, s_len // tk_b)

    w_in_ext, w_uq_pad, w_ukv_ext, w_out_full = _ext_weights(
        *_gather_weights((_pad_cols(w_in[0]), w_uq[0], _pad_cols(w_ukv[0]), w_out[0])))
    gains = (norm_in, _spread(a_q_norm, 1, A_DIM, 1, LAY_ROPE_A), _spread(a_k_norm, 1, A_DIM, 1, LAY_ROPE_A), b_cq_norm, b_ckv_norm,
             _spread(b_q_norm, 1, B_QK, 1, LAY_ROPE_B), _spread(b_k_norm, 1, B_QK, 1, LAY_ROPE_B))
    tabs = _rope_tables(s_len, tm)

    (xn_t, gates, pre, qbpre, kbpre, cq_t, ckv_t, qa, ka, va, qb, kb, vb) = _pre(
        xs, tabs, w_in_ext, w_uq_pad, w_ukv_ext, gains, tm)
    o_a, lse_a = _attn_fwd(qa, ka, va, A_GROUP, A_DIM, tq, tk_f, tiles_f, "attn_fwd_a")
    o_b, lse_b = _attn_fwd(qb, kb, vb, 1, B_V, tq, tk_f, tiles_f, "attn_fwd_b")
    y_t, dh, dgate, do_a, do_b, delta, loss_part = _mid(xs, ts, o_a, o_b, gates, w_out_full, min(512, s_len))

    def stat(a):
        return a.reshape(a.shape[0], s_len // tq_b, 1, tq_b)

    dqa, dka, dva = _attn_bwd(qa, ka, va, do_a, stat(lse_a), stat(delta[:A_HEADS]), A_GROUP, tq_b, tk_b, tiles_b, "attn_bwd_a")
    dqb, dkb, dvb = _attn_bwd(qb, kb, vb, do_b, stat(lse_b), stat(delta[A_HEADS:A_HEADS + B_HEADS]), 1, tq_b, tk_b,
                              tiles_b, "attn_bwd_b")
    grad_x, dproj, dqbpre, dkvb, d_small = _post(
        xs, dh, pre, qbpre, kbpre, dgate, dqa, dka, dva, dqb, dkb, dvb, loss_part, tabs,
        w_in_ext, w_uq_pad, w_ukv_ext, gains, tm)

    ts_w = min(2048, s_len)
    d_in_ext = _grad_w(xn_t, dproj, N_EXT // 2, ts_w, "grad_w_in")
    d_out_full = _grad_w(y_t, dh, D_MODEL, ts_w, "grad_w_out")
    d_uq_pad, d_ukv_ext = _grad_w_pairs([(cq_t, dqbpre), (ckv_t, dkvb)], ts_w, "grad_w_mla")

    g_in_p, g_uq, g_ukv_p, g_out, g_small = _reduce_grads(_fold_grads(d_in_ext, d_uq_pad, d_ukv_ext, d_out_full), d_small)
    g_in, g_ukv = g_in_p[:, :SH_IN[1]], g_ukv_p[:, :SH_UKV[1]]
    d_in, nm_in, nv_in = (a.T for a in _adamw_rows(w_in[0].T, g_in_p.T[:SH_IN[1]], m_w_in[0].T, v_w_in[0].T, SH_IN[1] // 7))
    rest = _adamw_rest(
        [(w_uq[0], g_uq, m_w_uq[0], v_w_uq[0]), (w_ukv[0], g_ukv, m_w_ukv[0], v_w_ukv[0]),
         (w_out[0], g_out, m_w_out[0], v_w_out[0])],
        [(norm_in, m_norm_in, v_norm_in), (a_q_norm, m_a_q_norm, v_a_q_norm), (a_k_norm, m_a_k_norm, v_a_k_norm),
         (b_cq_norm, m_b_cq_norm, v_b_cq_norm), (b_ckv_norm, m_b_ckv_norm, v_b_ckv_norm),
         (b_q_norm, m_b_q_norm, v_b_q_norm), (b_k_norm, m_b_k_norm, v_b_k_norm)], g_small)
    (d_uq, nm_uq, nv_uq), (d_ukv, nm_ukv, nv_ukv), (d_out, nm_out, nv_out) = (rest[3 * i:3 * i + 3] for i in range(3))
    sm = [rest[9 + 4 * i:9 + 4 * i + 4] for i in range(7)]

    def leaves(k, p_in, p_uq, p_ukv, p_out):
        return [sm[SM_IN][k], p_in[None], sm[SM_AQ][k], sm[SM_AK][k], sm[SM_CQ][k], sm[SM_CKV][k], p_uq[None], p_ukv[None],
                sm[SM_BQ][k], sm[SM_BK][k], p_out[None]]

    return (g_small[SM_LOSS, 0], grad_x[None], *leaves(0, g_in, g_uq, g_ukv, g_out), *leaves(1, d_in, d_uq, d_ukv, d_out),
            *leaves(2, nm_in, nm_uq, nm_ukv, nm_out), *leaves(3, nv_in, nv_uq, nv_ukv, nv_out))
```

```python
import jax
import jax.numpy as jnp
import numpy as np
from jax import lax
from jax.experimental import pallas as pl
from jax.experimental.pallas import tpu as pltpu

F32 = jnp.float32
BF16 = jnp.bfloat16
MESH = pl.DeviceIdType.MESH

D_MODEL = 1024
GRID_W = 64
ROPE_THETA = 10000.0
EPS = 1e-6
A_HEADS, A_KV, A_DIM = 8, 2, 64
A_GROUP = A_HEADS // A_KV
B_HEADS, B_NOPE, B_ROPE, B_V = 4, 64, 32, 128
B_QK = B_NOPE + B_ROPE
B_Q_RANK, B_KV_RANK = 384, 256
N_IN = 2464
SCALE_A = 1.0 / float(np.sqrt(A_DIM))
SCALE_B = 1.0 / float(np.sqrt(B_QK))
LOG2E = float(np.log2(np.e))
LN2 = float(np.log(2.0))
ADAM_LR, ADAM_B1, ADAM_B2, ADAM_EPS, ADAM_WD, ADAM_STEP = 0.001, 0.9, 0.999, 1e-08, 0.01, 10

LANE = 128
VMEM_BYTES = 64 * 1024 * 1024
VMEM_LIMIT = VMEM_BYTES - 8 * 1024 * 1024
VMEM_MID = 48 * 1024 * 1024
VMEM_SMALL = 32 * 1024 * 1024

QA0 = 0
KA0 = QA0 + A_HEADS * LANE
VA0 = KA0 + A_KV * LANE
GA0 = VA0 + A_KV * LANE
GB0 = GA0 + A_HEADS * A_DIM
CQ0 = GB0 + B_HEADS * LANE
CKV0 = CQ0 + B_Q_RANK
KR0 = CKV0 + B_KV_RANK
N_EXT = KR0 + LANE
N_GATE = (A_HEADS + B_HEADS) * LANE
N_GATE_C = A_HEADS * A_DIM + B_HEADS * LANE
DELTA_ROWS = 16
N_PRE = KA0 + A_KV * LANE + B_Q_RANK + B_KV_RANK

ROT = LANE // 2
_QA = A_DIM // 4
_QB = B_ROPE // 4
LAY_PLAIN_A = ((0, A_DIM, 0),)
LAY_ROPE_A = ((0, _QA, 0), (2 * _QA, _QA, _QA), (_QA, _QA, ROT), (3 * _QA, _QA, ROT + _QA))
LAY_KR = ((0, _QB, 0), (2 * _QB, _QB, _QB), (_QB, _QB, ROT), (3 * _QB, _QB, ROT + _QB))
LAY_NOPE = ((0, B_NOPE // 2, 2 * _QB), (B_NOPE // 2, B_NOPE // 2, ROT + 2 * _QB))
LAY_ROPE_B = LAY_NOPE + tuple((B_NOPE + a, n, at) for a, n, at in LAY_KR)

N_CHIPS = 4
SH_IN = (D_MODEL, N_IN // N_CHIPS)
SH_UQ = (B_Q_RANK // N_CHIPS, B_HEADS * B_QK)
SH_UKV = (B_KV_RANK, B_HEADS * (B_NOPE + B_V) // N_CHIPS)
SH_OUT = (D_MODEL // N_CHIPS, D_MODEL)
SM_ROWS, SM_W = 16, D_MODEL
SM_IN, SM_AQ, SM_AK, SM_CQ, SM_CKV, SM_BQ, SM_BK, SM_LOSS = range(8)
F32_ROWS, BF16_ROWS = 8, 16


def _pallas(body, **kw):
    return pl.pallas_call(body, **kw)


def _params(sem=None, vmem=None):
    return pltpu.CompilerParams(dimension_semantics=sem, vmem_limit_bytes=vmem)


def _rms_fwd(x, g, n):
    r = lax.rsqrt(jnp.sum(x * x, axis=-1, keepdims=True) * (1.0 / n) + EPS)
    return x * r * g


def _rms_bwd(dy, x, g, n):
    u = dy * g
    r = lax.rsqrt(jnp.sum(x * x, axis=-1, keepdims=True) * (1.0 / n) + EPS)
    ux = jnp.sum(u * x, axis=-1, keepdims=True)
    xhat = x * r
    dx = r * (u - xhat * (r * ux * (1.0 / n)))
    return dx, dy * xhat


def _rope_fwd(y, cos, sin):
    return y * cos + pltpu.roll(y, ROT, 1) * sin


def _rope_bwd(d, cos, sin):
    return d * cos - pltpu.roll(d, ROT, 1) * sin


def _token_tables(refs):
    out = []
    for r_ref, c_ref in zip(refs[0::2], refs[1::2]):
        r, c = r_ref[...], c_ref[...]
        out.append(jnp.concatenate([r[k:k + 1, :] + c for k in range(r.shape[0])], axis=0))
    return out


def _lanes_of(lane, layout):
    m = None
    for _, n, at in layout:
        seg = (lane >= at) & (lane < at + n)
        m = seg if m is None else (m | seg)
    return m


def _unspread_row(v, layout):
    v8 = jnp.broadcast_to(v, (F32_ROWS, LANE))
    lane = lax.broadcasted_iota(jnp.int32, v8.shape, 1)
    out = jnp.zeros_like(v8)
    for a, n, at in layout:
        moved = v8 if a == at else pltpu.roll(v8, (a - at) % LANE, 1)
        out = jnp.where((lane >= a) & (lane < a + n), moved, out)
    return out[0:1, :]


def _nt(a, b):
    return lax.dot_general(a, b, (((1,), (1,)), ((), ())), preferred_element_type=F32)


def _tn(a, b):
    return lax.dot_general(a, b, (((0,), (0,)), ((), ())), preferred_element_type=F32)


def _nn(a, b):
    return jnp.dot(a, b, preferred_element_type=F32)


def _block_rows(i, size):
    if isinstance(i, int):
        return pl.ds(i * size, size)
    return pl.ds(pl.multiple_of(i * size, size), size)


MAX_STATIC_BLOCKS = 32


def _three_stage(n, first, second, third):
    assert n >= 2 and n % 2 == 0
    first(0, 0)
    first(1, 1)
    second(0, 0)
    if n <= MAX_STATIC_BLOCKS:
        for i in range(1, n - 1):
            first(i + 1, (i + 1) % 2)
            second(i, i % 2)
            third(i - 1, (i - 1) % 2)
    else:
        def pair(t, carry):
            i = 2 * t + 1
            first(i + 1, 0)
            second(i, 1)
            third(i - 1, 0)
            first(i + 2, 1)
            second(i + 1, 0)
            third(i, 1)
            return carry

        lax.fori_loop(0, (n - 2) // 2, pair, 0)
    second(n - 1, 1)
    third(n - 2, 0)
    third(n - 1, 1)


def _full(shape):
    return pl.BlockSpec(shape, lambda *_: (0,) * len(shape))


def _table_specs(tm):
    return [pl.BlockSpec((None, tm // GRID_W, LANE), lambda i: (i, 0, 0)), _full((GRID_W, LANE))] * 4


def _resident(shape):
    return pl.BlockSpec(shape, lambda *_: (0,) * len(shape), pipeline_mode=pl.Buffered(1))


def _gather_weights(shards):
    n = len(shards)
    halves = [w.shape[0] // 2 for w in shards]

    def body(*refs):
        w_refs, out_refs, (send_sems, recv_sems) = refs[:n], refs[n:2 * n], refs[2 * n:]
        x, y, c = lax.axis_index("x"), lax.axis_index("y"), lax.axis_index("c")
        sibling = (x, y, 1 - c)
        chips = [(1 - x, y), (x, 1 - y), (1 - x, 1 - y)]
        me = 2 * x + y

        def copy(a, k, j, hc, to):
            part = out_refs[a].at[j, pl.ds(pl.multiple_of(hc * halves[a], BF16_ROWS), halves[a]), :]
            return pltpu.make_async_remote_copy(
                src_ref=part, dst_ref=part, send_sem=send_sems.at[6 * a + k], recv_sem=recv_sems.at[6 * a + k],
                device_id=to, device_id_type=MESH)

        started = []
        for a in range(n):
            out_refs[a][me] = w_refs[a][...].astype(BF16)
            for k, chip in enumerate(chips):
                started.append(copy(a, k, me, c, (*chip, c)))
                started[-1].start()
        for k, chip in enumerate(chips):
            for a in range(n):
                copy(a, k, 2 * chip[0] + chip[1], c, (*chip, c)).wait_recv()
                started.append(copy(a, 3 + k, 2 * chip[0] + chip[1], c, sibling))
                started[-1].start()
        for k, chip in enumerate(chips):
            for a in range(n):
                copy(a, 3 + k, 2 * chip[0] + chip[1], 1 - c, sibling).wait_recv()
        for cp in started:
            cp.wait_send()

    return _pallas(
        body, name="gather_weights",
        out_shape=[jax.ShapeDtypeStruct((N_CHIPS,) + w.shape, BF16) for w in shards],
        in_specs=[pl.BlockSpec(memory_space=pltpu.VMEM)] * n,
        out_specs=[pl.BlockSpec(memory_space=pltpu.VMEM)] * n,
        scratch_shapes=[pltpu.SemaphoreType.DMA((6 * n,)), pltpu.SemaphoreType.DMA((6 * n,))],
        compiler_params=_params(vmem=VMEM_SMALL),
    )(*shards)


def _reduce_grads(parts, small):
    n_big = len(parts)
    n = n_big + 1
    shapes = [p.shape[1:] for p in parts] + [small.shape]
    halves = [sh[0] // 2 for sh in shapes]

    def body(*refs):
        p_refs, out_refs, rec_a, rec_b = refs[:n], refs[n:2 * n], refs[2 * n:3 * n], refs[3 * n:4 * n]
        send_b = refs[4 * n:4 * n + n_big]
        sa_send, sa_recv, sb_send, sb_recv, sc_send, sc_recv = refs[4 * n + n_big:]
        x, y, c = lax.axis_index("x"), lax.axis_index("y"), lax.axis_index("c")
        sibling = (x, y, 1 - c)
        me = 2 * x + y

        def rows(a, hc):
            return pl.ds(pl.multiple_of(hc * halves[a], F32_ROWS), halves[a])

        def partial(a, j, hc):
            return p_refs[a].at[j, rows(a, hc), :] if a < n_big else p_refs[a].at[rows(a, hc), :]

        def copy_a(a, j):
            return pltpu.make_async_remote_copy(
                src_ref=partial(a, j, 1 - c), dst_ref=rec_a[a].at[j],
                send_sem=sa_send.at[N_CHIPS * a + j], recv_sem=sa_recv.at[N_CHIPS * a + j],
                device_id=sibling, device_id_type=MESH)

        def copy_b(a, r):
            j = me ^ r
            k = (N_CHIPS - 1) * a + r - 1
            return pltpu.make_async_remote_copy(
                src_ref=(send_b[a] if a < n_big else rec_a[a]).at[j], dst_ref=rec_b[a].at[r],
                send_sem=sb_send.at[k], recv_sem=sb_recv.at[k], device_id=(j // 2, j % 2, c), device_id_type=MESH)

        def copy_c(a):
            return pltpu.make_async_remote_copy(
                src_ref=out_refs[a].at[rows(a, c), :], dst_ref=out_refs[a].at[rows(a, c), :],
                send_sem=sc_send.at[a], recv_sem=sc_recv.at[a], device_id=sibling, device_id_type=MESH)

        for a in range(n):
            for j in range(N_CHIPS):
                copy_a(a, j).start()
        for r in range(1, N_CHIPS):
            j = me ^ r
            for a in range(n):
                copy_a(a, j).wait_recv()
                chip_part = rec_a[a][j] + partial(a, j, c)[...]
                if a < n_big:
                    send_b[a][j] = chip_part.astype(BF16)
                else:
                    rec_a[a][j] = chip_part
                copy_b(a, r).start()
        for a in range(n):
            copy_a(a, me).wait_recv()
            rec_b[a][0] = (rec_a[a][me] + partial(a, me, c)[...]).astype(rec_b[a].dtype)
        for a in range(n):
            for r in range(1, N_CHIPS):
                copy_b(a, r).wait_recv()
            total = rec_b[a][me].astype(F32)
            for j in range(1, N_CHIPS):
                total = total + rec_b[a][j ^ me].astype(F32)
            out_refs[a][rows(a, c), :] = total
            copy_c(a).start()
        for a in range(n):
            copy_c(a).wait_recv()
        for a in range(n):
            for j in range(N_CHIPS):
                copy_a(a, j).wait_send()
            for r in range(1, N_CHIPS):
                copy_b(a, r).wait_send()
            copy_c(a).wait_send()

    dma = pltpu.SemaphoreType.DMA
    return _pallas(
        body, name="reduce_grads",
        out_shape=[jax.ShapeDtypeStruct(sh, F32) for sh in shapes],
        in_specs=[pl.BlockSpec(memory_space=pltpu.VMEM)] * n,
        out_specs=[pl.BlockSpec(memory_space=pltpu.VMEM)] * n,
        scratch_shapes=[pltpu.VMEM((N_CHIPS, h) + sh[1:], F32) for h, sh in zip(halves, shapes)]
                       + [pltpu.VMEM((N_CHIPS, h) + sh[1:], BF16 if a < n_big else F32)
                          for a, (h, sh) in enumerate(zip(halves, shapes))]
                       + [pltpu.VMEM((N_CHIPS, h) + sh[1:], BF16) for h, sh in zip(halves[:n_big], shapes[:n_big])]
                       + [dma((N_CHIPS * n,)), dma((N_CHIPS * n,)), dma(((N_CHIPS - 1) * n,)), dma(((N_CHIPS - 1) * n,)),
                          dma((n,)), dma((n,))],
        compiler_params=_params(vmem=VMEM_LIMIT),
    )(*parts, small)


def _pre(x, tabs, w_in_ext, w_uq_pad, w_ukv_ext, gains, tm):
    s_len = x.shape[0]
    nt = s_len // tm

    def body(x_ref, car_ref, cac_ref, sar_ref, sac_ref, cbr_ref, cbc_ref, sbr_ref, sbc_ref, win_ref, wuq_ref, wukv_ref,
             gin_ref, gaq_ref, gak_ref, gcq_ref, gckv_ref, gbq_ref, gbk_ref,
             xn_ref, gates_ref, pre_ref, qbpre_ref, kbpre_ref, cq_ref, ckv_ref,
             qa_ref, ka_ref, va_ref, qb_ref, kb_ref, vb_ref, proj):
        xn = _rms_fwd(x_ref[...], gin_ref[...], D_MODEL)
        xn_ref[...] = jnp.transpose(xn).astype(BF16)
        xb = xn.astype(BF16)
        pre_ref[:, 0:VA0] = _nn(xb, win_ref[:, 0:VA0])
        gates_ref[...] = _nn(xb, win_ref[:, GA0:GA0 + N_GATE_C])
        pre_ref[:, VA0:N_PRE] = _nn(xb, win_ref[:, CQ0:KR0])
        proj[...] = _nn(xb, win_ref[:, VA0:GA0])
        kr = _nn(xb, win_ref[:, KR0:N_EXT])
        ca, sa, cb, sb = _token_tables((car_ref, cac_ref, sar_ref, sac_ref, cbr_ref, cbc_ref, sbr_ref, sbc_ref))
        lane = lax.broadcasted_iota(jnp.int32, (tm, LANE), 1)
        for h in range(A_HEADS):
            yq = _rms_fwd(pre_ref[:, QA0 + LANE * h:QA0 + LANE * (h + 1)], gaq_ref[...], A_DIM)
            qa_ref[h] = (_rope_fwd(yq, ca, sa) * (SCALE_A * LOG2E)).astype(BF16)
        for h in range(A_KV):
            yk = _rms_fwd(pre_ref[:, KA0 + LANE * h:KA0 + LANE * (h + 1)], gak_ref[...], A_DIM)
            ka_ref[h] = _rope_fwd(yk, ca, sa).astype(BF16)
            va_ref[h] = jnp.where(lane == A_DIM, 1.0, proj[:, LANE * h:LANE * (h + 1)]).astype(BF16)
        cq = _rms_fwd(pre_ref[:, VA0:VA0 + B_Q_RANK], gcq_ref[...], B_Q_RANK)
        cq_ref[...] = jnp.transpose(cq).astype(BF16)
        qbpre_ref[...] = _nn(cq.astype(BF16), wuq_ref[...])
        ckv = _rms_fwd(pre_ref[:, VA0 + B_Q_RANK:N_PRE], gckv_ref[...], B_KV_RANK)
        ckv_ref[...] = jnp.transpose(ckv).astype(BF16)
        kvb = _nn(ckv.astype(BF16), wukv_ref[...])
        for h in range(B_HEADS):
            yq = _rms_fwd(qbpre_ref[:, LANE * h:LANE * (h + 1)], gbq_ref[...], B_QK)
            qb_ref[h] = (_rope_fwd(yq, cb, sb) * (SCALE_B * LOG2E)).astype(BF16)
            kp = kvb[:, LANE * h:LANE * (h + 1)] + kr
            kbpre_ref[:, LANE * h:LANE * (h + 1)] = kp
            kb_ref[h] = _rope_fwd(_rms_fwd(kp, gbk_ref[...], B_QK), cb, sb).astype(BF16)
            vb_ref[h, :, 0:LANE] = kvb[:, B_HEADS * LANE + LANE * h:B_HEADS * LANE + LANE * (h + 1)].astype(BF16)
            vb_ref[h, :, LANE:2 * LANE] = jnp.where(lane == 0, 1.0, 0.0).astype(BF16)

    row = lambda w: pl.BlockSpec((tm, w), lambda i: (i, 0))
    col = lambda w: pl.BlockSpec((w, tm), lambda i: (0, i))
    heads = lambda n: pl.BlockSpec((n, tm, LANE), lambda i: (0, i, 0))
    hs = lambda n: jax.ShapeDtypeStruct((n, s_len, LANE), BF16)
    return _pallas(
        body, name="pre", grid=(nt,),
        in_specs=[row(D_MODEL)] + _table_specs(tm)
                 + [_resident(w_in_ext.shape), _resident(w_uq_pad.shape), _resident(w_ukv_ext.shape)]
                 + [_full(g.shape) for g in gains],
        out_specs=[col(D_MODEL), row(N_GATE_C), row(N_PRE), row(B_HEADS * LANE), row(B_HEADS * LANE),
                   col(B_Q_RANK), col(B_KV_RANK),
                   heads(A_HEADS), heads(A_KV), heads(A_KV), heads(B_HEADS), heads(B_HEADS),
                   pl.BlockSpec((B_HEADS, tm, 2 * LANE), lambda i: (0, i, 0))],
        out_shape=[jax.ShapeDtypeStruct((D_MODEL, s_len), BF16), jax.ShapeDtypeStruct((s_len, N_GATE_C), F32),
                   jax.ShapeDtypeStruct((s_len, N_PRE), F32), jax.ShapeDtypeStruct((s_len, B_HEADS * LANE), F32),
                   jax.ShapeDtypeStruct((s_len, B_HEADS * LANE), F32),
                   jax.ShapeDtypeStruct((B_Q_RANK, s_len), BF16), jax.ShapeDtypeStruct((B_KV_RANK, s_len), BF16),
                   hs(A_HEADS), hs(A_KV), hs(A_KV), hs(B_HEADS), hs(B_HEADS),
                   jax.ShapeDtypeStruct((B_HEADS, s_len, 2 * LANE), BF16)],
        scratch_shapes=[pltpu.VMEM((tm, A_KV * LANE), F32)],
        compiler_params=_params(("parallel",), VMEM_LIMIT),
    )(x, *tabs, w_in_ext, w_uq_pad, w_ukv_ext, *gains)


def _attn_fwd(q, k, v, group, l_col, tq, tk, tiles, name):
    n_heads, s_len, _ = q.shape
    v_w = v.shape[2]
    nk = s_len // tk

    def body(q_ref, k_ref, v_ref, o_ref, lse_ref, s_buf, p_buf, a_buf, m_ref, acc_ref):
        def scores(g, slot):
            s_buf[slot] = _nt(q_ref[_block_rows(g // nk, tq), :], k_ref[_block_rows(g % nk, tk), :])

        def softmax(g, slot):
            t = g // nk
            s = s_buf[slot]
            m_old = m_ref[t]
            m_new = jnp.maximum(m_old, jnp.max(s, axis=-1, keepdims=True))
            m_ref[t] = m_new
            a_buf[slot] = jnp.exp2(m_old - m_new)
            p_buf[slot] = jnp.exp2(s - jnp.tile(m_new, (1, tk // LANE))).astype(BF16)

        def values(g, slot):
            t = g // nk
            pv = _nn(p_buf[slot], v_ref[_block_rows(g % nk, tk), :])
            for c in range(0, v_w, LANE):
                acc_ref[t, :, c:c + LANE] = a_buf[slot] * acc_ref[t, :, c:c + LANE] + pv[:, c:c + LANE]

        m_ref[...] = jnp.full(m_ref.shape, -1e30, F32)
        acc_ref[...] = jnp.zeros(acc_ref.shape, F32)
        _three_stage(tiles * nk, scores, softmax, values)
        for t in range(tiles):
            l = acc_ref[t, :, l_col:l_col + 1]
            o = acc_ref[t, :, 0:LANE] * (1.0 / l)
            if l_col < LANE:
                lane = lax.broadcasted_iota(jnp.int32, o.shape, 1)
                o = jnp.where(lane == l_col, 0.0, o)
            o_ref[t * tq:(t + 1) * tq, :] = o
            lse_ref[t] = jnp.transpose(m_ref[t] + jnp.log2(jnp.broadcast_to(l, (tq, LANE))))[0:1, :]

    return _pallas(
        body, name=name, grid=(n_heads, s_len // (tiles * tq)),
        in_specs=[pl.BlockSpec((None, tiles * tq, LANE), lambda h, i: (h, i, 0)),
                  pl.BlockSpec((None, s_len, LANE), lambda h, i: (h // group, 0, 0)),
                  pl.BlockSpec((None, s_len, v_w), lambda h, i: (h // group, 0, 0))],
        out_specs=[pl.BlockSpec((None, tiles * tq, LANE), lambda h, i: (h, i, 0)),
                   pl.BlockSpec((None, tiles, 1, tq), lambda h, i: (h, i, 0, 0))],
        out_shape=[jax.ShapeDtypeStruct((n_heads, s_len, LANE), F32),
                   jax.ShapeDtypeStruct((n_heads, s_len // tq, 1, tq), F32)],
        scratch_shapes=[pltpu.VMEM((2, tq, tk), F32), pltpu.VMEM((2, tq, tk), BF16), pltpu.VMEM((2, tq, LANE), F32),
                        pltpu.VMEM((tiles, tq, LANE), F32), pltpu.VMEM((tiles, tq, v_w), F32)],
        compiler_params=_params(("parallel", "parallel"), VMEM_MID),
    )(q, k, v)


def _mid(x, target, o_a, o_b, gates, w_out, tm):
    s_len = x.shape[0]
    nt = s_len // tm
    n_heads = A_HEADS + B_HEADS
    d_mix = w_out.shape[0]
    pairs = A_HEADS // 2

    def body(x_ref, t_ref, oa_ref, ob_ref, g_ref, w_ref,
             yt_ref, dh_ref, dgate_ref, doa_ref, dob_ref, delta_ref, loss_ref, silu_scr, dsilu_scr, y_ref):
        @pl.when(pl.program_id(0) == 0)
        def _():
            loss_ref[...] = jnp.zeros_like(loss_ref)

        def o_of(h):
            return oa_ref[h] if h < A_HEADS else ob_ref[h - A_HEADS]

        lane = lax.broadcasted_iota(jnp.int32, (tm, LANE), 1)

        def gated(h):
            cols = slice(LANE * h, LANE * (h + 1))
            if h < A_HEADS:
                packed = g_ref[:, LANE * (h // 2):LANE * (h // 2 + 1)]
                g = jnp.where(lane < A_DIM, packed if h % 2 == 0 else pltpu.roll(packed, ROT, 1), 0.0)
            else:
                g = g_ref[:, LANE * (pairs + h - A_HEADS):LANE * (pairs + h - A_HEADS + 1)]
            sig = 1.0 / (1.0 + jnp.exp(-g))
            silu = g * sig
            silu_scr[:, cols] = silu
            dsilu_scr[:, cols] = sig * (1.0 + g * (1.0 - sig))
            return o_of(h) * silu

        for c in range(pairs + B_HEADS):
            y = gated(2 * c) + pltpu.roll(gated(2 * c + 1), ROT, 1) if c < pairs else gated(A_HEADS + c - pairs)
            cols = slice(LANE * c, LANE * (c + 1))
            y_ref[:, cols] = y.astype(BF16)
            yt_ref[cols, :] = jnp.transpose(y).astype(BF16)
        err = x_ref[...] + _nn(y_ref[...], w_ref[...]) - t_ref[...]
        sq = jnp.sum(jnp.sum(err * err, axis=-1, keepdims=True), axis=0, keepdims=True)
        loss_ref[...] += jnp.broadcast_to(sq * (0.5 / D_MODEL), loss_ref.shape)
        dh = err * (1.0 / D_MODEL)
        dh_ref[...] = dh
        dy = _nt(dh.astype(BF16), w_ref[...])
        delta = jnp.zeros((tm, LANE), F32)
        held = None
        for h in range(n_heads):
            cols = slice(LANE * h, LANE * (h + 1))
            if h < A_HEADS:
                packed = dy[:, LANE * (h // 2):LANE * (h // 2 + 1)]
                dyh = packed if h % 2 == 0 else pltpu.roll(packed, ROT, 1)
            else:
                dyh = dy[:, LANE * (pairs + h - A_HEADS):LANE * (pairs + h - A_HEADS + 1)]
            oh = o_of(h)
            do = dyh * silu_scr[:, cols]
            dg = dyh * oh * dsilu_scr[:, cols]
            if h >= A_HEADS:
                dgate_ref[:, LANE * (pairs + h - A_HEADS):LANE * (pairs + h - A_HEADS + 1)] = dg.astype(BF16)
            elif h % 2 == 0:
                held = dg
            else:
                dgate_ref[:, LANE * (h // 2):LANE * (h // 2 + 1)] = (held + pltpu.roll(dg, ROT, 1)).astype(BF16)
            delta = jnp.where(lane == h, jnp.sum(do * oh, axis=-1, keepdims=True), delta)
            if h < A_HEADS:
                doa_ref[h] = do.astype(BF16)
            else:
                dob_ref[h - A_HEADS] = do.astype(BF16)
        delta_ref[...] = jnp.transpose(delta)[0:DELTA_ROWS, :]

    row = lambda w: pl.BlockSpec((tm, w), lambda i: (i, 0))
    heads = lambda n, w=LANE: pl.BlockSpec((n, tm, w), lambda i: (0, i, 0))
    return _pallas(
        body, name="mid", grid=(nt,),
        in_specs=[row(D_MODEL), row(D_MODEL), heads(A_HEADS), heads(B_HEADS), row(N_GATE_C), _resident(w_out.shape)],
        out_specs=[pl.BlockSpec((d_mix, tm), lambda i: (0, i)), row(D_MODEL), row(N_GATE_C), heads(A_HEADS), heads(B_HEADS),
                   pl.BlockSpec((DELTA_ROWS, tm), lambda i: (0, i)),
                   _full((8, LANE))],
        out_shape=[jax.ShapeDtypeStruct((d_mix, s_len), BF16), jax.ShapeDtypeStruct((s_len, D_MODEL), F32),
                   jax.ShapeDtypeStruct((s_len, N_GATE_C), BF16),
                   jax.ShapeDtypeStruct((A_HEADS, s_len, LANE), BF16), jax.ShapeDtypeStruct((B_HEADS, s_len, LANE), BF16),
                   jax.ShapeDtypeStruct((DELTA_ROWS, s_len), F32), jax.ShapeDtypeStruct((8, LANE), F32)],
        scratch_shapes=[pltpu.VMEM((tm, N_GATE), F32), pltpu.VMEM((tm, N_GATE), F32), pltpu.VMEM((tm, d_mix), BF16)],
        compiler_params=_params(("arbitrary",), VMEM_LIMIT),
    )(x, target, o_a, o_b, gates, w_out)


def _attn_bwd(q, k, v, do, lse, delta, group, tq, tk, tiles, name):
    n_heads, s_len, _ = q.shape
    nq = s_len // tq

    def body(q_ref, do_ref, lse_ref, delta_ref, k_ref, v_ref, dq_ref, dk_ref, dv_ref, s_buf, dp_buf, p_buf, ds_buf):
        @pl.when(pl.program_id(1) == 0)
        def _():
            dq_ref[...] = jnp.zeros_like(dq_ref)

        dk_ref[...] = jnp.zeros_like(dk_ref)
        dv_ref[...] = jnp.zeros_like(dv_ref)

        def keys(g):
            return _block_rows(g // nq, tk)

        def queries(g):
            return _block_rows(g % nq, tq)

        def scores(g, slot):
            s_buf[slot] = _nt(k_ref[keys(g), :], q_ref[queries(g), :])
            dp_buf[slot] = _nt(v_ref[keys(g), :], do_ref[queries(g), :])

        def elementwise(g, slot):
            p = jnp.exp2(s_buf[slot] - lse_ref[g % nq])
            p_buf[slot] = p.astype(BF16)
            ds_buf[slot] = (p * (dp_buf[slot] - delta_ref[g % nq])).astype(BF16)

        def grads(g, slot):
            dv_ref[keys(g), :] += _nn(p_buf[slot], do_ref[queries(g), :])
            dk_ref[keys(g), :] += _nn(ds_buf[slot], q_ref[queries(g), :])
            dq_ref[queries(g), :] += _tn(ds_buf[slot], k_ref[keys(g), :])

        _three_stage(tiles * nq, scores, elementwise, grads)

    whole = lambda: pl.BlockSpec((None, s_len, LANE), lambda h, j: (h, 0, 0))
    stat = lambda: pl.BlockSpec((None, nq, 1, tq), lambda h, j: (h, 0, 0, 0))
    kvb = lambda: pl.BlockSpec((None, tiles * tk, LANE), lambda h, j: (h // group, j, 0))
    outb = lambda: pl.BlockSpec((None, tiles * tk, LANE), lambda h, j: (h, j, 0))
    shape = jax.ShapeDtypeStruct((n_heads, s_len, LANE), F32)
    return _pallas(
        body, name=name, grid=(n_heads, s_len // (tiles * tk)),
        in_specs=[whole(), whole(), stat(), stat(), kvb(), kvb()],
        out_specs=[whole(), outb(), outb()],
        out_shape=[shape, shape, shape],
        scratch_shapes=[pltpu.VMEM((2, tk, tq), F32), pltpu.VMEM((2, tk, tq), F32),
                        pltpu.VMEM((2, tk, tq), BF16), pltpu.VMEM((2, tk, tq), BF16)],
        compiler_params=_params(("parallel", "arbitrary"), VMEM_MID),
    )(q, do, lse, delta, k, v)


def _post(x, dh, pre, qbpre, kbpre, dgate, dqa, dka, dva, dqb, dkb, dvb, loss_part, tabs,
          w_in_ext, w_uq_pad, w_ukv_ext, gains, tm):
    s_len = x.shape[0]
    nt = s_len // tm

    def body(x_ref, dh_ref, pre_ref, qbpre_ref, kbpre_ref, dgate_ref,
             dqa_ref, dka_ref, dva_ref, dqb_ref, dkb_ref, dvb_ref, loss_ref,
             car_ref, cac_ref, sar_ref, sac_ref, cbr_ref, cbc_ref, sbr_ref, sbc_ref, win_ref, wuq_ref, wukv_ref,
             gin_ref, gaq_ref, gak_ref, gcq_ref, gckv_ref, gbq_ref, gbk_ref,
             gx_ref, dproj_ref, dqbpre_ref, dkvb_ref, dsm_ref):
        @pl.when(pl.program_id(0) == 0)
        def _():
            dsm_ref[...] = jnp.zeros_like(dsm_ref)
            dsm_ref[SM_LOSS:SM_LOSS + 1, 0:LANE] = loss_ref[0:1, :]

        def add_small(r, dg):
            dsm_ref[r:r + 1, 0:dg.shape[1]] += dg

        def tok_sum(a):
            return jnp.sum(a, axis=0, keepdims=True)

        ca, sa, cb, sb = _token_tables((car_ref, cac_ref, sar_ref, sac_ref, cbr_ref, cbc_ref, sbr_ref, sbc_ref))
        lane = lax.broadcasted_iota(jnp.int32, (tm, LANE), 1)

        nope_lanes = _lanes_of(lane, LAY_NOPE)

        def back(c0, c1):
            return _nt(dproj_ref[:, c0:c1], win_ref[:, c0:c1])

        dproj_ref[:, GA0:GA0 + N_GATE_C] = dgate_ref[...]
        dxn = back(GA0, GA0 + N_GATE_C)
        dg = jnp.zeros((1, LANE), F32)
        for h in range(A_HEADS):
            dn = _rope_bwd(dqa_ref[h] * SCALE_A, ca, sa)
            dx, dgr = _rms_bwd(dn, pre_ref[:, QA0 + LANE * h:QA0 + LANE * (h + 1)], gaq_ref[...], A_DIM)
            dproj_ref[:, QA0 + LANE * h:QA0 + LANE * (h + 1)] = dx.astype(BF16)
            dg = dg + tok_sum(dgr)
        add_small(SM_AQ, _unspread_row(dg, LAY_ROPE_A))
        dxn = dxn + back(QA0, KA0)
        dg = jnp.zeros((1, LANE), F32)
        for h in range(A_KV):
            dk = dka_ref[A_GROUP * h]
            dv = dva_ref[A_GROUP * h]
            for g in range(1, A_GROUP):
                dk = dk + dka_ref[A_GROUP * h + g]
                dv = dv + dva_ref[A_GROUP * h + g]
            dn = _rope_bwd(dk * LN2, ca, sa)
            dx, dgr = _rms_bwd(dn, pre_ref[:, KA0 + LANE * h:KA0 + LANE * (h + 1)], gak_ref[...], A_DIM)
            dproj_ref[:, KA0 + LANE * h:KA0 + LANE * (h + 1)] = dx.astype(BF16)
            dproj_ref[:, VA0 + LANE * h:VA0 + LANE * (h + 1)] = dv.astype(BF16)
            dg = dg + tok_sum(dgr)
        add_small(SM_AK, _unspread_row(dg, LAY_ROPE_A))
        dxn = dxn + back(KA0, GA0)
        dg = jnp.zeros((1, LANE), F32)
        for h in range(B_HEADS):
            cols = slice(LANE * h, LANE * (h + 1))
            dn = _rope_bwd(dqb_ref[h] * SCALE_B, cb, sb)
            dx, dgr = _rms_bwd(dn, qbpre_ref[:, cols], gbq_ref[...], B_QK)
            dqbpre_ref[:, cols] = dx.astype(BF16)
            dg = dg + tok_sum(dgr)
        add_small(SM_BQ, _unspread_row(dg, LAY_ROPE_B))
        dcq = _nt(dqbpre_ref[...], wuq_ref[...])
        dx, dgr = _rms_bwd(dcq, pre_ref[:, VA0:VA0 + B_Q_RANK], gcq_ref[...], B_Q_RANK)
        dproj_ref[:, CQ0:CQ0 + B_Q_RANK] = dx.astype(BF16)
        add_small(SM_CQ, tok_sum(dgr))
        dxn = dxn + back(CQ0, CKV0)
        dg = jnp.zeros((1, LANE), F32)
        dkr = jnp.zeros((tm, LANE), F32)
        for h in range(B_HEADS):
            cols = slice(LANE * h, LANE * (h + 1))
            dn = _rope_bwd(dkb_ref[h] * LN2, cb, sb)
            dx, dgr = _rms_bwd(dn, kbpre_ref[:, cols], gbk_ref[...], B_QK)
            dkvb_ref[:, cols] = jnp.where(nope_lanes, dx, 0.0).astype(BF16)
            dkvb_ref[:, B_HEADS * LANE + LANE * h:B_HEADS * LANE + LANE * (h + 1)] = dvb_ref[h].astype(BF16)
            dkr = dkr + dx
            dg = dg + tok_sum(dgr)
        add_small(SM_BK, _unspread_row(dg, LAY_ROPE_B))
        dproj_ref[:, KR0:KR0 + LANE] = jnp.where(_lanes_of(lane, LAY_KR), dkr, 0.0).astype(BF16)
        dckv = _nt(dkvb_ref[...], wukv_ref[...])
        dx, dgr = _rms_bwd(dckv, pre_ref[:, VA0 + B_Q_RANK:N_PRE], gckv_ref[...], B_KV_RANK)
        dproj_ref[:, CKV0:CKV0 + B_KV_RANK] = dx.astype(BF16)
        add_small(SM_CKV, tok_sum(dgr))
        dxn = dxn + back(CKV0, N_EXT)
        dx, dgr = _rms_bwd(dxn, x_ref[...], gin_ref[...], D_MODEL)
        gx_ref[...] = dh_ref[...] + dx
        add_small(SM_IN, tok_sum(dgr))

    row = lambda w: pl.BlockSpec((tm, w), lambda i: (i, 0))
    heads = lambda n: pl.BlockSpec((n, tm, LANE), lambda i: (0, i, 0))
    return _pallas(
        body, name="post", grid=(nt,),
        in_specs=[row(D_MODEL), row(D_MODEL), row(N_PRE), row(B_HEADS * LANE), row(B_HEADS * LANE), row(N_GATE_C),
                  heads(A_HEADS), heads(A_HEADS), heads(A_HEADS), heads(B_HEADS), heads(B_HEADS), heads(B_HEADS),
                  _full(loss_part.shape)] + _table_specs(tm)
                 + [_resident(w_in_ext.shape), _resident(w_uq_pad.shape), _resident(w_ukv_ext.shape)]
                 + [_full(g.shape) for g in gains],
        out_specs=[row(D_MODEL), row(N_EXT), row(B_HEADS * LANE), row(2 * B_HEADS * LANE), _full((SM_ROWS, SM_W))],
        out_shape=[jax.ShapeDtypeStruct((s_len, D_MODEL), F32), jax.ShapeDtypeStruct((s_len, N_EXT), BF16),
                   jax.ShapeDtypeStruct((s_len, B_HEADS * LANE), BF16),
                   jax.ShapeDtypeStruct((s_len, 2 * B_HEADS * LANE), BF16),
                   jax.ShapeDtypeStruct((SM_ROWS, SM_W), F32)],
        compiler_params=_params(("arbitrary",), VMEM_LIMIT),
    )(x, dh, pre, qbpre, kbpre, dgate, dqa, dka, dva, dqb, dkb, dvb, loss_part, *tabs,
      w_in_ext, w_uq_pad, w_ukv_ext, *gains)


def _grad_w(a_t, b, tn, ts, name):
    m, s_len = a_t.shape
    n = b.shape[1]

    def body(a_ref, b_ref, o_ref):
        @pl.when(pl.program_id(1) == 0)
        def _():
            o_ref[...] = jnp.zeros_like(o_ref)

        o_ref[...] += _nn(a_ref[...], b_ref[...].astype(BF16))

    return _pallas(
        body, name=name, grid=(n // tn, s_len // ts),
        in_specs=[pl.BlockSpec((m, ts), lambda j, t: (0, t)), pl.BlockSpec((ts, tn), lambda j, t: (t, j))],
        out_specs=pl.BlockSpec((m, tn), lambda j, t: (0, j)),
        out_shape=jax.ShapeDtypeStruct((m, n), F32),
        compiler_params=_params(("parallel", "arbitrary"), VMEM_MID),
    )(a_t, b)


def _grad_w_pairs(pairs, ts, name):
    s_len = pairs[0][0].shape[1]
    n_p = len(pairs)

    def body(*refs):
        for a_ref, b_ref, o_ref in zip(refs[0:2 * n_p:2], refs[1:2 * n_p:2], refs[2 * n_p:]):
            @pl.when(pl.program_id(0) == 0)
            def _():
                o_ref[...] = jnp.zeros_like(o_ref)

            o_ref[...] += _nn(a_ref[...], b_ref[...].astype(BF16))

    in_specs, flat = [], []
    for a_t, b in pairs:
        in_specs += [pl.BlockSpec((a_t.shape[0], ts), lambda t: (0, t)), pl.BlockSpec((ts, b.shape[1]), lambda t: (t, 0))]
        flat += [a_t, b]
    return _pallas(
        body, name=name, grid=(s_len // ts,),
        in_specs=in_specs,
        out_specs=[_full((a_t.shape[0], b.shape[1])) for a_t, b in pairs],
        out_shape=[jax.ShapeDtypeStruct((a_t.shape[0], b.shape[1]), F32) for a_t, b in pairs],
        compiler_params=_params(("arbitrary",), VMEM_MID),
    )(*flat)


def _adam_math(w, g, m, v):
    nm = ADAM_B1 * m + (1.0 - ADAM_B1) * g
    nv = ADAM_B2 * v + (1.0 - ADAM_B2) * (g * g)
    m_hat = nm / (1.0 - ADAM_B1 ** ADAM_STEP)
    v_hat = nv / (1.0 - ADAM_B2 ** ADAM_STEP)
    return -ADAM_LR * (m_hat / (jnp.sqrt(v_hat) + ADAM_EPS) + ADAM_WD * w), nm, nv


def _adamw_rows(w, g, m, v, tr):
    rows, cols = w.shape

    def body(w_ref, g_ref, m_ref, v_ref, d_ref, nm_ref, nv_ref):
        d_ref[...], nm_ref[...], nv_ref[...] = _adam_math(w_ref[...], g_ref[...], m_ref[...], v_ref[...])

    blk = pl.BlockSpec((tr, cols), lambda i: (i, 0))
    shape = jax.ShapeDtypeStruct((rows, cols), F32)
    return _pallas(
        body, name="adamw_w_in", grid=(rows // tr,),
        in_specs=[blk] * 4, out_specs=[blk] * 3, out_shape=[shape] * 3,
        compiler_params=_params(("parallel",), VMEM_SMALL),
    )(w, g, m, v)


def _adamw_rest(bigs, smalls, g_small):
    nb, ns = len(bigs), len(smalls)

    def body(*refs):
        ins, outs = refs[:4 * nb + 3 * ns + 1], refs[4 * nb + 3 * ns + 1:]
        for i in range(nb):
            w_ref, g_ref, m_ref, v_ref = ins[4 * i:4 * i + 4]
            d_ref, nm_ref, nv_ref = outs[3 * i:3 * i + 3]
            d_ref[...], nm_ref[...], nv_ref[...] = _adam_math(w_ref[...], g_ref[...], m_ref[...], v_ref[...])
        gs_ref = ins[-1]
        for i in range(ns):
            w_ref, m_ref, v_ref = ins[4 * nb + 3 * i:4 * nb + 3 * i + 3]
            g_ref, d_ref, nm_ref, nv_ref = outs[3 * nb + 4 * i:3 * nb + 4 * i + 4]
            g = gs_ref[i:i + 1, 0:w_ref.shape[1]]
            g_ref[...] = g
            d_ref[...], nm_ref[...], nv_ref[...] = _adam_math(w_ref[...], g, m_ref[...], v_ref[...])

    flat_in = [a for quad in bigs for a in quad] + [a for tri in smalls for a in tri] + [g_small]
    out_shape = ([jax.ShapeDtypeStruct(q[0].shape, F32) for q in bigs for _ in range(3)]
                 + [jax.ShapeDtypeStruct(t[0].shape, F32) for t in smalls for _ in range(4)])
    return _pallas(
        body, name="adamw_rest",
        in_specs=[pl.BlockSpec(memory_space=pltpu.VMEM)] * len(flat_in),
        out_specs=[pl.BlockSpec(memory_space=pltpu.VMEM)] * len(out_shape),
        out_shape=out_shape,
        compiler_params=_params(vmem=VMEM_SMALL),
    )(*flat_in)


def _place(pieces, n):
    out, at = [], 0
    for lane0, arr in sorted(pieces, key=lambda p: p[0]):
        out += [jnp.zeros((n, lane0 - at), F32), arr]
        at = lane0 + arr.shape[1]
    return jnp.concatenate(out + [jnp.zeros((n, LANE - at), F32)], axis=1)


def _rope_tables(s_len, tm):
    rows = s_len // GRID_W
    row = jnp.arange(rows, dtype=F32)
    col = jnp.arange(GRID_W, dtype=F32)

    def lay(dim, layout, first_dim, ones):
        half = dim // 2
        inv = 1.0 / (ROPE_THETA ** (jnp.arange(0, half, 2, dtype=F32) / half))
        ang_r, ang_c = row[:, None] * inv[None, :], col[:, None] * inv[None, :]
        at = {a - first_dim: lane0 for a, _, lane0 in layout}
        q = dim // 4
        r1, r2, c1, c2 = at[0], at[q], at[2 * q], at[3 * q]
        cos_r = _place([(r1, jnp.cos(ang_r)), (r2, jnp.cos(ang_r))], rows)
        sin_r = _place([(r1, -jnp.sin(ang_r)), (r2, jnp.sin(ang_r))], rows)
        cos_c = _place([(c1, jnp.cos(ang_c)), (c2, jnp.cos(ang_c))] + [(l0, jnp.ones((GRID_W, n), F32)) for _, n, l0 in ones],
                       GRID_W)
        sin_c = _place([(c1, -jnp.sin(ang_c)), (c2, jnp.sin(ang_c))], GRID_W)
        by_block = (s_len // tm, tm // GRID_W, LANE)
        return cos_r.reshape(by_block), cos_c, sin_r.reshape(by_block), sin_c

    return lay(A_DIM, LAY_ROPE_A, 0, ()) + lay(B_ROPE, LAY_KR, 0, LAY_NOPE)


def _spread(w, n_heads, dim, axis, layout):
    w3 = w.reshape(w.shape[:axis] + (n_heads, dim) + w.shape[axis + 1:])
    out, at = [], 0

    def zeros(n):
        return jnp.zeros(w3.shape[:axis + 1] + (n,) + w3.shape[axis + 2:], w.dtype)

    for a0, n, lane0 in sorted(layout, key=lambda seg: seg[2]):
        out += [zeros(lane0 - at), lax.slice_in_dim(w3, a0, a0 + n, axis=axis + 1)]
        at = lane0 + n
    out = jnp.concatenate(out + [zeros(LANE - at)], axis=axis + 1)
    return out.reshape(w.shape[:axis] + (n_heads * LANE,) + w.shape[axis + 1:])


def _head_cols(first, n_heads, dim, layout):
    out = np.full((n_heads * LANE,), -1, np.int32)
    for h in range(n_heads):
        for a0, n, lane0 in layout:
            out[h * LANE + lane0:h * LANE + lane0 + n] = first + h * dim + a0 + np.arange(n)
    return out


def _inverse(src, n):
    dst = np.full((n,), -1, np.int32)
    dst[src[src >= 0]] = np.nonzero(src >= 0)[0]
    return dst


def _column_maps():
    a_w, kv_w = A_HEADS * A_DIM, A_KV * A_DIM
    o_g = a_w + 2 * kv_w
    o_cq = o_g + a_w
    o_kr = o_cq + B_Q_RANK + B_KV_RANK
    src_in = np.concatenate([
        _head_cols(0, A_HEADS, A_DIM, LAY_ROPE_A), _head_cols(a_w, A_KV, A_DIM, LAY_ROPE_A),
        _head_cols(a_w + kv_w, A_KV, A_DIM, LAY_PLAIN_A), np.arange(o_g, o_g + a_w),
        np.arange(o_kr + B_ROPE, N_IN), np.arange(o_cq, o_kr), _head_cols(o_kr, 1, B_ROPE, LAY_KR)]).astype(np.int32)
    src_uq = _head_cols(0, B_HEADS, B_QK, LAY_ROPE_B)
    per = B_NOPE + B_V
    src_ukv = np.concatenate([_head_cols(0, B_HEADS, per, LAY_NOPE),
                              _head_cols(B_NOPE, B_HEADS, per, ((0, B_V, 0),))]).astype(np.int32)
    assert len(src_in) == N_EXT
    return src_in, src_uq, src_ukv


def _round_up(n, m):
    return (n + m - 1) // m * m


PERMUTE_STEPS = 4


def _permute_cols(xs, maps, stacks, name):
    maps = [np.asarray(m, np.int32) for m in maps]
    n_arr = len(xs)

    def block(ref, b):
        if len(ref.shape) == 2:
            return ref.at[:, b * LANE:(b + 1) * LANE]
        per = ref.shape[2] // LANE
        return ref.at[b // per, :, (b % per) * LANE:(b % per + 1) * LANE]

    def body(*refs):
        row = lax.broadcasted_iota(jnp.int32, (LANE, LANE), 0)
        for x_ref, src_ref, o_ref, src in zip(refs[:n_arr], refs[n_arr:2 * n_arr], refs[2 * n_arr:], maps):
            for c in range(len(src) // LANE):
                want = src[c * LANE:(c + 1) * LANE]
                if want[0] >= 0 and want[0] % LANE == 0 and np.array_equal(want, want[0] + np.arange(LANE)):
                    block(o_ref, c)[...] = block(x_ref, int(want[0]) // LANE)[...]
                    continue
                acc = jnp.zeros((x_ref.shape[-2], LANE), F32)
                for kb in sorted({int(v) // LANE for v in want if v >= 0}):
                    sel = jnp.where(row + kb * LANE == src_ref[:, c * LANE:(c + 1) * LANE], 1.0, 0.0).astype(BF16)
                    part = block(x_ref, kb)[...]
                    if part.dtype == BF16:
                        acc = acc + _nn(part, sel)
                    else:
                        hi = part.astype(BF16)
                        rest = part - hi.astype(F32)
                        mid = rest.astype(BF16)
                        low = (rest - mid.astype(F32)).astype(BF16)
                        acc = acc + ((_nn(hi, sel) + _nn(mid, sel)) + _nn(low, sel))
                block(o_ref, c)[...] = acc.astype(o_ref.dtype)

    def out_shape(x, m, stack):
        rows = x.shape[-2]
        return (rows, len(m)) if stack is None else (stack, rows, len(m) // stack)

    def rows_spec(shape):
        if len(shape) == 2:
            return pl.BlockSpec((shape[0] // PERMUTE_STEPS, shape[1]), lambda i: (i, 0))
        return pl.BlockSpec((shape[0], shape[1] // PERMUTE_STEPS, shape[2]), lambda i: (0, i, 0))

    shapes = [out_shape(x, m, st) for x, m, st in zip(xs, maps, stacks)]
    return _pallas(
        body, name=name, grid=(PERMUTE_STEPS,),
        in_specs=[rows_spec(x.shape) for x in xs] + [_full((1, len(m))) for m in maps],
        out_specs=[rows_spec(sh) for sh in shapes],
        out_shape=[jax.ShapeDtypeStruct(sh, x.dtype) for sh, x in zip(shapes, xs)],
        compiler_params=_params(("parallel",), VMEM_MID),
    )(*xs, *[jnp.asarray(m).reshape(1, -1) for m in maps])


def _pad_cols(w):
    return jnp.pad(w, ((0, 0), (0, _round_up(w.shape[1], LANE) - w.shape[1])))


def _in_stack(cols, width):
    cols = np.asarray(cols)
    return np.where(cols < 0, -1, cols // width * _round_up(width, LANE) + cols % width).astype(np.int32)


def _ext_weights(g_in, g_uq, g_ukv, g_out):
    src_in, src_uq, src_ukv = _column_maps()
    w_uq = g_uq.reshape(B_Q_RANK, B_HEADS * B_QK)
    w_out = g_out.reshape(D_MODEL, D_MODEL)
    w_in_ext, w_uq_pad, w_ukv_ext = _permute_cols(
        [g_in, w_uq, g_ukv], [_in_stack(src_in, SH_IN[1]), src_uq, _in_stack(src_ukv, SH_UKV[1])], [None] * 3, "lay_out_weights")
    return w_in_ext, w_uq_pad, w_ukv_ext, w_out


def _fold_grads(d_in_ext, d_uq_pad, d_ukv_ext, d_out):
    src_in, src_uq, src_ukv = _column_maps()

    def back(src, n, width):
        inv = _inverse(src, n)
        wide = _round_up(width, LANE)
        out = np.full((n // width * wide,), -1, np.int32)
        for j in range(n // width):
            out[j * wide:j * wide + width] = inv[j * width:(j + 1) * width]
        return out

    n_uq, n_ukv = B_HEADS * B_QK, B_HEADS * (B_NOPE + B_V)
    d_in, d_uq, d_ukv = _permute_cols(
        [d_in_ext, d_uq_pad, d_ukv_ext], [back(src_in, N_IN, SH_IN[1]), _inverse(src_uq, n_uq), back(src_ukv, n_ukv, SH_UKV[1])],
        [N_CHIPS, None, N_CHIPS], "fold_grads")
    return d_in, d_uq.reshape((N_CHIPS,) + SH_UQ), d_ukv, d_out.reshape((N_CHIPS,) + SH_OUT)


def kernel(x, norm_in, w_in, a_q_norm, a_k_norm, b_cq_norm, b_ckv_norm, w_uq, w_ukv, b_q_norm, b_k_norm, w_out, loss_target, m_norm_in, m_w_in, m_a_q_norm, m_a_k_norm, m_b_cq_norm, m_b_ckv_norm, m_w_uq, m_w_ukv, m_b_q_norm, m_b_k_norm, m_w_out, v_norm_in, v_w_in, v_a_q_norm, v_a_k_norm, v_b_cq_norm, v_b_ckv_norm, v_w_uq, v_w_ukv, v_b_q_norm, v_b_k_norm, v_w_out):
    s_len = x.shape[1]
    xs, ts = x[0], loss_target[0]
    tm = min(256, s_len)
    tq, tk_f = min(512, s_len // 2), min(2048, s_len // 2)
    tq_b, tk_b = min(1024, s_len // 2), min(512, s_len)
    tiles_f = min(4, s_len // tq)
    tiles_b = min(2, s_len // tk_b)

    w_in_ext, w_uq_pad, w_ukv_ext, w_out_full = _ext_weights(
        *_gather_weights((_pad_cols(w_in[0]), w_uq[0], _pad_cols(w_ukv[0]), w_out[0])))
    gains = (norm_in, _spread(a_q_norm, 1, A_DIM, 1, LAY_ROPE_A), _spread(a_k_norm, 1, A_DIM, 1, LAY_ROPE_A), b_cq_norm, b_ckv_norm,
             _spread(b_q_norm, 1, B_QK, 1, LAY_ROPE_B), _spread(b_k_norm, 1, B_QK, 1, LAY_ROPE_B))
    tabs = _rope_tables(s_len, tm)

    (xn_t, gates, pre, qbpre, kbpre, cq_t, ckv_t, qa, ka, va, qb, kb, vb) = _pre(
        xs, tabs, w_in_ext, w_uq_pad, w_ukv_ext, gains, tm)
    o_a, lse_a = _attn_fwd(qa, ka, va, A_GROUP, A_DIM, tq, tk_f, tiles_f, "attn_fwd_a")
    o_b, lse_b = _attn_fwd(qb, kb, vb, 1, B_V, tq, tk_f, tiles_f, "attn_fwd_b")
    y_t, dh, dgate, do_a, do_b, delta, loss_part = _mid(xs, ts, o_a, o_b, gates, w_out_full, min(512, s_len))

    def stat(a):
        return a.reshape(a.shape[0], s_len // tq_b, 1, tq_b)

    dqa, dka, dva = _attn_bwd(qa, ka, va, do_a, stat(lse_a), stat(delta[:A_HEADS]), A_GROUP, tq_b, tk_b, tiles_b, "attn_bwd_a")
    dqb, dkb, dvb = _attn_bwd(qb, kb, vb, do_b, stat(lse_b), stat(delta[A_HEADS:A_HEADS + B_HEADS]), 1, tq_b, tk_b,
                              tiles_b, "attn_bwd_b")
    grad_x, dproj, dqbpre, dkvb, d_small = _post(
        xs, dh, pre, qbpre, kbpre, dgate, dqa, dka, dva, dqb, dkb, dvb, loss_part, tabs,
        w_in_ext, w_uq_pad, w_ukv_ext, gains, tm)

    ts_w = min(2048, s_len)
    d_in_ext = _grad_w(xn_t, dproj, N_EXT // 2, ts_w, "grad_w_in")
    d_out_full = _grad_w(y_t, dh, D_MODEL, ts_w, "grad_w_out")
    d_uq_pad, d_ukv_ext = _grad_w_pairs([(cq_t, dqbpre), (ckv_t, dkvb)], ts_w, "grad_w_mla")

    g_in_p, g_uq, g_ukv_p, g_out, g_small = _reduce_grads(_fold_grads(d_in_ext, d_uq_pad, d_ukv_ext, d_out_full), d_small)
    g_in, g_ukv = g_in_p[:, :SH_IN[1]], g_ukv_p[:, :SH_UKV[1]]
    d_in, nm_in, nv_in = (a.T for a in _adamw_rows(w_in[0].T, g_in_p.T[:SH_IN[1]], m_w_in[0].T, v_w_in[0].T, SH_IN[1] // 7))
    rest = _adamw_rest(
        [(w_uq[0], g_uq, m_w_uq[0], v_w_uq[0]), (w_ukv[0], g_ukv, m_w_ukv[0], v_w_ukv[0]),
         (w_out[0], g_out, m_w_out[0], v_w_out[0])],
        [(norm_in, m_norm_in, v_norm_in), (a_q_norm, m_a_q_norm, v_a_q_norm), (a_k_norm, m_a_k_norm, v_a_k_norm),
         (b_cq_norm, m_b_cq_norm, v_b_cq_norm), (b_ckv_norm, m_b_ckv_norm, v_b_ckv_norm),
         (b_q_norm, m_b_q_norm, v_b_q_norm), (b_k_norm, m_b_k_norm, v_b_k_norm)], g_small)
    (d_uq, nm_uq, nv_uq), (d_ukv, nm_ukv, nv_ukv), (d_out, nm_out, nv_out) = (rest[3 * i:3 * i + 3] for i in range(3))
    sm = [rest[9 + 4 * i:9 + 4 * i + 4] for i in range(7)]

    def leaves(k, p_in, p_uq, p_ukv, p_out):
        return [sm[SM_IN][k], p_in[None], sm[SM_AQ][k], sm[SM_AK][k], sm[SM_CQ][k], sm[SM_CKV][k], p_uq[None], p_ukv[None],
                sm[SM_BQ][k], sm[SM_BK][k], p_out[None]]

    return (g_small[SM_LOSS, 0], grad_x[None], *leaves(0, g_in, g_uq, g_ukv, g_out), *leaves(1, d_in, d_uq, d_ukv, d_out),
            *leaves(2, nm_in, nm_uq, nm_ukv, nm_out), *leaves(3, nv_in, nv_uq, nv_ukv, nv_out))
```

```python
import jax
import jax.numpy as jnp
import numpy as np
from jax import lax
from jax.experimental import pallas as pl
from jax.experimental.pallas import tpu as pltpu

F32 = jnp.float32
BF16 = jnp.bfloat16
MESH = pl.DeviceIdType.MESH

D_MODEL = 1024
GRID_W = 64
ROPE_THETA = 10000.0
EPS = 1e-6
A_HEADS, A_KV, A_DIM = 8, 2, 64
A_GROUP = A_HEADS // A_KV
B_HEADS, B_NOPE, B_ROPE, B_V = 4, 64, 32, 128
B_QK = B_NOPE + B_ROPE
B_Q_RANK, B_KV_RANK = 384, 256
N_IN = 2464
SCALE_A = 1.0 / float(np.sqrt(A_DIM))
SCALE_B = 1.0 / float(np.sqrt(B_QK))
LOG2E = float(np.log2(np.e))
LN2 = float(np.log(2.0))
ADAM_LR, ADAM_B1, ADAM_B2, ADAM_EPS, ADAM_WD, ADAM_STEP = 0.001, 0.9, 0.999, 1e-08, 0.01, 10

LANE = 128
VMEM_BYTES = 64 * 1024 * 1024
VMEM_LIMIT = VMEM_BYTES - 8 * 1024 * 1024
VMEM_MID = 48 * 1024 * 1024
VMEM_SMALL = 32 * 1024 * 1024

QA0 = 0
KA0 = QA0 + A_HEADS * LANE
VA0 = KA0 + A_KV * LANE
GA0 = VA0 + A_KV * LANE
GB0 = GA0 + A_HEADS * A_DIM
CQ0 = GB0 + B_HEADS * LANE
CKV0 = CQ0 + B_Q_RANK
KR0 = CKV0 + B_KV_RANK
N_EXT = KR0 + LANE
N_GATE = (A_HEADS + B_HEADS) * LANE
N_GATE_C = A_HEADS * A_DIM + B_HEADS * LANE
DELTA_ROWS = 16
N_PRE = KA0 + A_KV * LANE + B_Q_RANK + B_KV_RANK

ROT = LANE // 2
_QA = A_DIM // 4
_QB = B_ROPE // 4
LAY_PLAIN_A = ((0, A_DIM, 0),)
LAY_ROPE_A = ((0, _QA, 0), (2 * _QA, _QA, _QA), (_QA, _QA, ROT), (3 * _QA, _QA, ROT + _QA))
LAY_KR = ((0, _QB, 0), (2 * _QB, _QB, _QB), (_QB, _QB, ROT), (3 * _QB, _QB, ROT + _QB))
LAY_NOPE = ((0, B_NOPE // 2, 2 * _QB), (B_NOPE // 2, B_NOPE // 2, ROT + 2 * _QB))
LAY_ROPE_B = LAY_NOPE + tuple((B_NOPE + a, n, at) for a, n, at in LAY_KR)

N_CHIPS = 4
SH_IN = (D_MODEL, N_IN // N_CHIPS)
SH_UQ = (B_Q_RANK // N_CHIPS, B_HEADS * B_QK)
SH_UKV = (B_KV_RANK, B_HEADS * (B_NOPE + B_V) // N_CHIPS)
SH_OUT = (D_MODEL // N_CHIPS, D_MODEL)
SM_ROWS, SM_W = 16, D_MODEL
SM_IN, SM_AQ, SM_AK, SM_CQ, SM_CKV, SM_BQ, SM_BK, SM_LOSS = range(8)
F32_ROWS, BF16_ROWS = 8, 16


def _pallas(body, **kw):
    return pl.pallas_call(body, **kw)


def _params(sem=None, vmem=None):
    return pltpu.CompilerParams(dimension_semantics=sem, vmem_limit_bytes=vmem)


def _rms_fwd(x, g, n):
    r = lax.rsqrt(jnp.sum(x * x, axis=-1, keepdims=True) * (1.0 / n) + EPS)
    return x * r * g


def _rms_bwd(dy, x, g, n):
    u = dy * g
    r = lax.rsqrt(jnp.sum(x * x, axis=-1, keepdims=True) * (1.0 / n) + EPS)
    ux = jnp.sum(u * x, axis=-1, keepdims=True)
    xhat = x * r
    dx = r * (u - xhat * (r * ux * (1.0 / n)))
    return dx, dy * xhat


def _rope_fwd(y, cos, sin):
    return y * cos + pltpu.roll(y, ROT, 1) * sin


def _rope_bwd(d, cos, sin):
    return d * cos - pltpu.roll(d, ROT, 1) * sin


def _token_tables(refs):
    out = []
    for r_ref, c_ref in zip(refs[0::2], refs[1::2]):
        r, c = r_ref[...], c_ref[...]
        out.append(jnp.concatenate([r[k:k + 1, :] + c for k in range(r.shape[0])], axis=0))
    return out


def _lanes_of(lane, layout):
    m = None
    for _, n, at in layout:
        seg = (lane >= at) & (lane < at + n)
        m = seg if m is None else (m | seg)
    return m


def _unspread_row(v, layout):
    v8 = jnp.broadcast_to(v, (F32_ROWS, LANE))
    lane = lax.broadcasted_iota(jnp.int32, v8.shape, 1)
    out = jnp.zeros_like(v8)
    for a, n, at in layout:
        moved = v8 if a == at else pltpu.roll(v8, (a - at) % LANE, 1)
        out = jnp.where((lane >= a) & (lane < a + n), moved, out)
    return out[0:1, :]


def _nt(a, b):
    return lax.dot_general(a, b, (((1,), (1,)), ((), ())), preferred_element_type=F32)


def _tn(a, b):
    return lax.dot_general(a, b, (((0,), (0,)), ((), ())), preferred_element_type=F32)


def _nn(a, b):
    return jnp.dot(a, b, preferred_element_type=F32)


def _block_rows(i, size):
    if isinstance(i, int):
        return pl.ds(i * size, size)
    return pl.ds(pl.multiple_of(i * size, size), size)


MAX_STATIC_BLOCKS = 32


def _three_stage(n, first, second, third):
    assert n >= 2 and n % 2 == 0
    first(0, 0)
    first(1, 1)
    second(0, 0)
    if n <= MAX_STATIC_BLOCKS:
        for i in range(1, n - 1):
            first(i + 1, (i + 1) % 2)
            second(i, i % 2)
            third(i - 1, (i - 1) % 2)
    else:
        def pair(t, carry):
            i = 2 * t + 1
            first(i + 1, 0)
            second(i, 1)
            third(i - 1, 0)
            first(i + 2, 1)
            second(i + 1, 0)
            third(i, 1)
            return carry

        lax.fori_loop(0, (n - 2) // 2, pair, 0)
    second(n - 1, 1)
    third(n - 2, 0)
    third(n - 1, 1)


def _full(shape):
    return pl.BlockSpec(shape, lambda *_: (0,) * len(shape))


def _table_specs(tm):
    return [pl.BlockSpec((None, tm // GRID_W, LANE), lambda i: (i, 0, 0)), _full((GRID_W, LANE))] * 4


def _resident(shape):
    return pl.BlockSpec(shape, lambda *_: (0,) * len(shape), pipeline_mode=pl.Buffered(1))


def _gather_weights(shards):
    n = len(shards)
    halves = [w.shape[0] // 2 for w in shards]

    def body(*refs):
        w_refs, out_refs, (send_sems, recv_sems) = refs[:n], refs[n:2 * n], refs[2 * n:]
        x, y, c = lax.axis_index("x"), lax.axis_index("y"), lax.axis_index("c")
        sibling = (x, y, 1 - c)
        chips = [(1 - x, y), (x, 1 - y), (1 - x, 1 - y)]
        me = 2 * x + y

        def copy(a, k, j, hc, to):
            part = out_refs[a].at[j, pl.ds(pl.multiple_of(hc * halves[a], BF16_ROWS), halves[a]), :]
            return pltpu.make_async_remote_copy(
                src_ref=part, dst_ref=part, send_sem=send_sems.at[6 * a + k], recv_sem=recv_sems.at[6 * a + k],
                device_id=to, device_id_type=MESH)

        started = []
        for a in range(n):
            out_refs[a][me] = w_refs[a][...].astype(BF16)
            for k, chip in enumerate(chips):
                started.append(copy(a, k, me, c, (*chip, c)))
                started[-1].start()
        for k, chip in enumerate(chips):
            for a in range(n):
                copy(a, k, 2 * chip[0] + chip[1], c, (*chip, c)).wait_recv()
                started.append(copy(a, 3 + k, 2 * chip[0] + chip[1], c, sibling))
                started[-1].start()
        for k, chip in enumerate(chips):
            for a in range(n):
                copy(a, 3 + k, 2 * chip[0] + chip[1], 1 - c, sibling).wait_recv()
        for cp in started:
            cp.wait_send()

    return _pallas(
        body, name="gather_weights",
        out_shape=[jax.ShapeDtypeStruct((N_CHIPS,) + w.shape, BF16) for w in shards],
        in_specs=[pl.BlockSpec(memory_space=pltpu.VMEM)] * n,
        out_specs=[pl.BlockSpec(memory_space=pltpu.VMEM)] * n,
        scratch_shapes=[pltpu.SemaphoreType.DMA((6 * n,)), pltpu.SemaphoreType.DMA((6 * n,))],
        compiler_params=_params(vmem=VMEM_SMALL),
    )(*shards)


def _reduce_grads(parts, small):
    n_big = len(parts)
    n = n_big + 1
    shapes = [p.shape[1:] for p in parts] + [small.shape]
    halves = [sh[0] // 2 for sh in shapes]

    def body(*refs):
        p_refs, out_refs, rec_a, rec_b = refs[:n], refs[n:2 * n], refs[2 * n:3 * n], refs[3 * n:4 * n]
        send_b = refs[4 * n:4 * n + n_big]
        sa_send, sa_recv, sb_send, sb_recv, sc_send, sc_recv = refs[4 * n + n_big:]
        x, y, c = lax.axis_index("x"), lax.axis_index("y"), lax.axis_index("c")
        sibling = (x, y, 1 - c)
        me = 2 * x + y

        def rows(a, hc):
            return pl.ds(pl.multiple_of(hc * halves[a], F32_ROWS), halves[a])

        def partial(a, j, hc):
            return p_refs[a].at[j, rows(a, hc), :] if a < n_big else p_refs[a].at[rows(a, hc), :]

        def copy_a(a, j):
            return pltpu.make_async_remote_copy(
                src_ref=partial(a, j, 1 - c), dst_ref=rec_a[a].at[j],
                send_sem=sa_send.at[N_CHIPS * a + j], recv_sem=sa_recv.at[N_CHIPS * a + j],
                device_id=sibling, device_id_type=MESH)

        def copy_b(a, r):
            j = me ^ r
            k = (N_CHIPS - 1) * a + r - 1
            return pltpu.make_async_remote_copy(
                src_ref=(send_b[a] if a < n_big else rec_a[a]).at[j], dst_ref=rec_b[a].at[r],
                send_sem=sb_send.at[k], recv_sem=sb_recv.at[k], device_id=(j // 2, j % 2, c), device_id_type=MESH)

        def copy_c(a):
            return pltpu.make_async_remote_copy(
                src_ref=out_refs[a].at[rows(a, c), :], dst_ref=out_refs[a].at[rows(a, c), :],
                send_sem=sc_send.at[a], recv_sem=sc_recv.at[a], device_id=sibling, device_id_type=MESH)

        for a in range(n):
            for j in range(N_CHIPS):
                copy_a(a, j).start()
        for r in range(1, N_CHIPS):
            j = me ^ r
            for a in range(n):
                copy_a(a, j).wait_recv()
                chip_part = rec_a[a][j] + partial(a, j, c)[...]
                if a < n_big:
                    send_b[a][j] = chip_part.astype(BF16)
                else:
                    rec_a[a][j] = chip_part
                copy_b(a, r).start()
        for a in range(n):
            copy_a(a, me).wait_recv()
            rec_b[a][0] = (rec_a[a][me] + partial(a, me, c)[...]).astype(rec_b[a].dtype)
        for a in range(n):
            for r in range(1, N_CHIPS):
                copy_b(a, r).wait_recv()
            total = rec_b[a][me].astype(F32)
            for j in range(1, N_CHIPS):
                total = total + rec_b[a][j ^ me].astype(F32)
            out_refs[a][rows(a, c), :] = total
            copy_c(a).start()
        for a in range(n):
            copy_c(a).wait_recv()
        for a in range(n):
            for j in range(N_CHIPS):
                copy_a(a, j).wait_send()
            for r in range(1, N_CHIPS):
                copy_b(a, r).wait_send()
            copy_c(a).wait_send()

    dma = pltpu.SemaphoreType.DMA
    return _pallas(
        body, name="reduce_grads",
        out_shape=[jax.ShapeDtypeStruct(sh, F32) for sh in shapes],
        in_specs=[pl.BlockSpec(memory_space=pltpu.VMEM)] * n,
        out_specs=[pl.BlockSpec(memory_space=pltpu.VMEM)] * n,
        scratch_shapes=[pltpu.VMEM((N_CHIPS, h) + sh[1:], F32) for h, sh in zip(halves, shapes)]
                       + [pltpu.VMEM((N_CHIPS, h) + sh[1:], BF16 if a < n_big else F32)
                          for a, (h, sh) in enumerate(zip(halves, shapes))]
                       + [pltpu.VMEM((N_CHIPS, h) + sh[1:], BF16) for h, sh in zip(halves[:n_big], shapes[:n_big])]
                       + [dma((N_CHIPS * n,)), dma((N_CHIPS * n,)), dma(((N_CHIPS - 1) * n,)), dma(((N_CHIPS - 1) * n,)),
                          dma((n,)), dma((n,))],
        compiler_params=_params(vmem=VMEM_LIMIT),
    )(*parts, small)


def _pre(x, tabs, w_in_ext, w_uq_pad, w_ukv_ext, gains, tm):
    s_len = x.shape[0]
    nt = s_len // tm

    def body(x_ref, car_ref, cac_ref, sar_ref, sac_ref, cbr_ref, cbc_ref, sbr_ref, sbc_ref, win_ref, wuq_ref, wukv_ref,
             gin_ref, gaq_ref, gak_ref, gcq_ref, gckv_ref, gbq_ref, gbk_ref,
             xn_ref, gates_ref, pre_ref, qbpre_ref, kbpre_ref, cq_ref, ckv_ref,
             qa_ref, ka_ref, va_ref, qb_ref, kb_ref, vb_ref, proj):
        xn = _rms_fwd(x_ref[...], gin_ref[...], D_MODEL)
        xn_ref[...] = jnp.transpose(xn).astype(BF16)
        xb = xn.astype(BF16)
        pre_ref[:, 0:VA0] = _nn(xb, win_ref[:, 0:VA0])
        gates_ref[...] = _nn(xb, win_ref[:, GA0:GA0 + N_GATE_C])
        pre_ref[:, VA0:N_PRE] = _nn(xb, win_ref[:, CQ0:KR0])
        proj[...] = _nn(xb, win_ref[:, VA0:GA0])
        kr = _nn(xb, win_ref[:, KR0:N_EXT])
        ca, sa, cb, sb = _token_tables((car_ref, cac_ref, sar_ref, sac_ref, cbr_ref, cbc_ref, sbr_ref, sbc_ref))
        lane = lax.broadcasted_iota(jnp.int32, (tm, LANE), 1)
        for h in range(A_HEADS):
            yq = _rms_fwd(pre_ref[:, QA0 + LANE * h:QA0 + LANE * (h + 1)], gaq_ref[...], A_DIM)
            qa_ref[h] = (_rope_fwd(yq, ca, sa) * (SCALE_A * LOG2E)).astype(BF16)
        for h in range(A_KV):
            yk = _rms_fwd(pre_ref[:, KA0 + LANE * h:KA0 + LANE * (h + 1)], gak_ref[...], A_DIM)
            ka_ref[h] = _rope_fwd(yk, ca, sa).astype(BF16)
            va_ref[h] = jnp.where(lane == A_DIM, 1.0, proj[:, LANE * h:LANE * (h + 1)]).astype(BF16)
        cq = _rms_fwd(pre_ref[:, VA0:VA0 + B_Q_RANK], gcq_ref[...], B_Q_RANK)
        cq_ref[...] = jnp.transpose(cq).astype(BF16)
        qbpre_ref[...] = _nn(cq.astype(BF16), wuq_ref[...])
        ckv = _rms_fwd(pre_ref[:, VA0 + B_Q_RANK:N_PRE], gckv_ref[...], B_KV_RANK)
        ckv_ref[...] = jnp.transpose(ckv).astype(BF16)
        kvb = _nn(ckv.astype(BF16), wukv_ref[...])
        for h in range(B_HEADS):
            yq = _rms_fwd(qbpre_ref[:, LANE * h:LANE * (h + 1)], gbq_ref[...], B_QK)
            qb_ref[h] = (_rope_fwd(yq, cb, sb) * (SCALE_B * LOG2E)).astype(BF16)
            kp = kvb[:, LANE * h:LANE * (h + 1)] + kr
            kbpre_ref[:, LANE * h:LANE * (h + 1)] = kp
            kb_ref[h] = _rope_fwd(_rms_fwd(kp, gbk_ref[...], B_QK), cb, sb).astype(BF16)
            vb_ref[h, :, 0:LANE] = kvb[:, B_HEADS * LANE + LANE * h:B_HEADS * LANE + LANE * (h + 1)].astype(BF16)
            vb_ref[h, :, LANE:2 * LANE] = jnp.where(lane == 0, 1.0, 0.0).astype(BF16)

    row = lambda w: pl.BlockSpec((tm, w), lambda i: (i, 0))
    col = lambda w: pl.BlockSpec((w, tm), lambda i: (0, i))
    heads = lambda n: pl.BlockSpec((n, tm, LANE), lambda i: (0, i, 0))
    hs = lambda n: jax.ShapeDtypeStruct((n, s_len, LANE), BF16)
    return _pallas(
        body, name="pre", grid=(nt,),
        in_specs=[row(D_MODEL)] + _table_specs(tm)
                 + [_resident(w_in_ext.shape), _resident(w_uq_pad.shape), _resident(w_ukv_ext.shape)]
                 + [_full(g.shape) for g in gains],
        out_specs=[col(D_MODEL), row(N_GATE_C), row(N_PRE), row(B_HEADS * LANE), row(B_HEADS * LANE),
                   col(B_Q_RANK), col(B_KV_RANK),
                   heads(A_HEADS), heads(A_KV), heads(A_KV), heads(B_HEADS), heads(B_HEADS),
                   pl.BlockSpec((B_HEADS, tm, 2 * LANE), lambda i: (0, i, 0))],
        out_shape=[jax.ShapeDtypeStruct((D_MODEL, s_len), BF16), jax.ShapeDtypeStruct((s_len, N_GATE_C), F32),
                   jax.ShapeDtypeStruct((s_len, N_PRE), F32), jax.ShapeDtypeStruct((s_len, B_HEADS * LANE), F32),
                   jax.ShapeDtypeStruct((s_len, B_HEADS * LANE), F32),
                   jax.ShapeDtypeStruct((B_Q_RANK, s_len), BF16), jax.ShapeDtypeStruct((B_KV_RANK, s_len), BF16),
                   hs(A_HEADS), hs(A_KV), hs(A_KV), hs(B_HEADS), hs(B_HEADS),
                   jax.ShapeDtypeStruct((B_HEADS, s_len, 2 * LANE), BF16)],
        scratch_shapes=[pltpu.VMEM((tm, A_KV * LANE), F32)],
        compiler_params=_params(("parallel",), VMEM_LIMIT),
    )(x, *tabs, w_in_ext, w_uq_pad, w_ukv_ext, *gains)


def _attn_fwd(q, k, v, group, l_col, tq, tk, tiles, name):
    n_heads, s_len, _ = q.shape
    v_w = v.shape[2]
    nk = s_len // tk

    def body(q_ref, k_ref, v_ref, o_ref, lse_ref, s_buf, p_buf, a_buf, m_ref, acc_ref):
        def scores(g, slot):
            s_buf[slot] = _nt(q_ref[_block_rows(g // nk, tq), :], k_ref[_block_rows(g % nk, tk), :])

        def softmax(g, slot):
            t = g // nk
            s = s_buf[slot]
            m_old = m_ref[t]
            m_new = jnp.maximum(m_old, jnp.max(s, axis=-1, keepdims=True))
            m_ref[t] = m_new
            a_buf[slot] = jnp.exp2(m_old - m_new)
            p_buf[slot] = jnp.exp2(s - jnp.tile(m_new, (1, tk // LANE))).astype(BF16)

        def values(g, slot):
            t = g // nk
            pv = _nn(p_buf[slot], v_ref[_block_rows(g % nk, tk), :])
            for c in range(0, v_w, LANE):
                acc_ref[t, :, c:c + LANE] = a_buf[slot] * acc_ref[t, :, c:c + LANE] + pv[:, c:c + LANE]

        m_ref[...] = jnp.full(m_ref.shape, -1e30, F32)
        acc_ref[...] = jnp.zeros(acc_ref.shape, F32)
        _three_stage(tiles * nk, scores, softmax, values)
        for t in range(tiles):
            l = acc_ref[t, :, l_col:l_col + 1]
            o = acc_ref[t, :, 0:LANE] * (1.0 / l)
            if l_col < LANE:
                lane = lax.broadcasted_iota(jnp.int32, o.shape, 1)
                o = jnp.where(lane == l_col, 0.0, o)
            o_ref[t * tq:(t + 1) * tq, :] = o
            lse_ref[t] = jnp.transpose(m_ref[t] + jnp.log2(jnp.broadcast_to(l, (tq, LANE))))[0:1, :]

    return _pallas(
        body, name=name, grid=(n_heads, s_len // (tiles * tq)),
        in_specs=[pl.BlockSpec((None, tiles * tq, LANE), lambda h, i: (h, i, 0)),
                  pl.BlockSpec((None, s_len, LANE), lambda h, i: (h // group, 0, 0)),
                  pl.BlockSpec((None, s_len, v_w), lambda h, i: (h // group, 0, 0))],
        out_specs=[pl.BlockSpec((None, tiles * tq, LANE), lambda h, i: (h, i, 0)),
                   pl.BlockSpec((None, tiles, 1, tq), lambda h, i: (h, i, 0, 0))],
        out_shape=[jax.ShapeDtypeStruct((n_heads, s_len, LANE), F32),
                   jax.ShapeDtypeStruct((n_heads, s_len // tq, 1, tq), F32)],
        scratch_shapes=[pltpu.VMEM((2, tq, tk), F32), pltpu.VMEM((2, tq, tk), BF16), pltpu.VMEM((2, tq, LANE), F32),
                        pltpu.VMEM((tiles, tq, LANE), F32), pltpu.VMEM((tiles, tq, v_w), F32)],
        compiler_params=_params(("parallel", "parallel"), VMEM_MID),
    )(q, k, v)


def _mid(x, target, o_a, o_b, gates, w_out, tm):
    s_len = x.shape[0]
    nt = s_len // tm
    n_heads = A_HEADS + B_HEADS
    d_mix = w_out.shape[0]
    pairs = A_HEADS // 2

    def body(x_ref, t_ref, oa_ref, ob_ref, g_ref, w_ref,
             yt_ref, dh_ref, dgate_ref, doa_ref, dob_ref, delta_ref, loss_ref, silu_scr, dsilu_scr, y_ref):
        @pl.when(pl.program_id(0) == 0)
        def _():
            loss_ref[...] = jnp.zeros_like(loss_ref)

        def o_of(h):
            return oa_ref[h] if h < A_HEADS else ob_ref[h - A_HEADS]

        lane = lax.broadcasted_iota(jnp.int32, (tm, LANE), 1)

        def gated(h):
            cols = slice(LANE * h, LANE * (h + 1))
            if h < A_HEADS:
                packed = g_ref[:, LANE * (h // 2):LANE * (h // 2 + 1)]
                g = jnp.where(lane < A_DIM, packed if h % 2 == 0 else pltpu.roll(packed, ROT, 1), 0.0)
            else:
                g = g_ref[:, LANE * (pairs + h - A_HEADS):LANE * (pairs + h - A_HEADS + 1)]
            sig = 1.0 / (1.0 + jnp.exp(-g))
            silu = g * sig
            silu_scr[:, cols] = silu
            dsilu_scr[:, cols] = sig * (1.0 + g * (1.0 - sig))
            return o_of(h) * silu

        for c in range(pairs + B_HEADS):
            y = gated(2 * c) + pltpu.roll(gated(2 * c + 1), ROT, 1) if c < pairs else gated(A_HEADS + c - pairs)
            cols = slice(LANE * c, LANE * (c + 1))
            y_ref[:, cols] = y.astype(BF16)
            yt_ref[cols, :] = jnp.transpose(y).astype(BF16)
        err = x_ref[...] + _nn(y_ref[...], w_ref[...]) - t_ref[...]
        sq = jnp.sum(jnp.sum(err * err, axis=-1, keepdims=True), axis=0, keepdims=True)
        loss_ref[...] += jnp.broadcast_to(sq * (0.5 / D_MODEL), loss_ref.shape)
        dh = err * (1.0 / D_MODEL)
        dh_ref[...] = dh
        dy = _nt(dh.astype(BF16), w_ref[...])
        delta = jnp.zeros((tm, LANE), F32)
        held = None
        for h in range(n_heads):
            cols = slice(LANE * h, LANE * (h + 1))
            if h < A_HEADS:
                packed = dy[:, LANE * (h // 2):LANE * (h // 2 + 1)]
                dyh = packed if h % 2 == 0 else pltpu.roll(packed, ROT, 1)
            else:
                dyh = dy[:, LANE * (pairs + h - A_HEADS):LANE * (pairs + h - A_HEADS + 1)]
            oh = o_of(h)
            do = dyh * silu_scr[:, cols]
            dg = dyh * oh * dsilu_scr[:, cols]
            if h >= A_HEADS:
                dgate_ref[:, LANE * (pairs + h - A_HEADS):LANE * (pairs + h - A_HEADS + 1)] = dg.astype(BF16)
            elif h % 2 == 0:
                held = dg
            else:
                dgate_ref[:, LANE * (h // 2):LANE * (h // 2 + 1)] = (held + pltpu.roll(dg, ROT, 1)).astype(BF16)
            delta = jnp.where(lane == h, jnp.sum(do * oh, axis=-1, keepdims=True), delta)
            if h < A_HEADS:
                doa_ref[h] = do.astype(BF16)
            else:
                dob_ref[h - A_HEADS] = do.astype(BF16)
        delta_ref[...] = jnp.transpose(delta)[0:DELTA_ROWS, :]

    row = lambda w: pl.BlockSpec((tm, w), lambda i: (i, 0))
    heads = lambda n, w=LANE: pl.BlockSpec((n, tm, w), lambda i: (0, i, 0))
    return _pallas(
        body, name="mid", grid=(nt,),
        in_specs=[row(D_MODEL), row(D_MODEL), heads(A_HEADS), heads(B_HEADS), row(N_GATE_C), _resident(w_out.shape)],
        out_specs=[pl.BlockSpec((d_mix, tm), lambda i: (0, i)), row(D_MODEL), row(N_GATE_C), heads(A_HEADS), heads(B_HEADS),
                   pl.BlockSpec((DELTA_ROWS, tm), lambda i: (0, i)),
                   _full((8, LANE))],
        out_shape=[jax.ShapeDtypeStruct((d_mix, s_len), BF16), jax.ShapeDtypeStruct((s_len, D_MODEL), F32),
                   jax.ShapeDtypeStruct((s_len, N_GATE_C), BF16),
                   jax.ShapeDtypeStruct((A_HEADS, s_len, LANE), BF16), jax.ShapeDtypeStruct((B_HEADS, s_len, LANE), BF16),
                   jax.ShapeDtypeStruct((DELTA_ROWS, s_len), F32), jax.ShapeDtypeStruct((8, LANE), F32)],
        scratch_shapes=[pltpu.VMEM((tm, N_GATE), F32), pltpu.VMEM((tm, N_GATE), F32), pltpu.VMEM((tm, d_mix), BF16)],
        compiler_params=_params(("arbitrary",), VMEM_LIMIT),
    )(x, target, o_a, o_b, gates, w_out)


def _attn_bwd(q, k, v, do, lse, delta, group, tq, tk, tiles, name):
    n_heads, s_len, _ = q.shape
    nq = s_len // tq

    def body(q_ref, do_ref, lse_ref, delta_ref, k_ref, v_ref, dq_ref, dk_ref, dv_ref, s_buf, dp_buf, p_buf, ds_buf):
        @pl.when(pl.program_id(1) == 0)
        def _():
            dq_ref[...] = jnp.zeros_like(dq_ref)

        dk_ref[...] = jnp.zeros_like(dk_ref)
        dv_ref[...] = jnp.zeros_like(dv_ref)

        def keys(g):
            return _block_rows(g // nq, tk)

        def queries(g):
            return _block_rows(g % nq, tq)

        def scores(g, slot):
            s_buf[slot] = _nt(k_ref[keys(g), :], q_ref[queries(g), :])
            dp_buf[slot] = _nt(v_ref[keys(g), :], do_ref[queries(g), :])

        def elementwise(g, slot):
            p = jnp.exp2(s_buf[slot] - lse_ref[g % nq])
            p_buf[slot] = p.astype(BF16)
            ds_buf[slot] = (p * (dp_buf[slot] - delta_ref[g % nq])).astype(BF16)

        def grads(g, slot):
            dv_ref[keys(g), :] += _nn(p_buf[slot], do_ref[queries(g), :])
            dk_ref[keys(g), :] += _nn(ds_buf[slot], q_ref[queries(g), :])
            dq_ref[queries(g), :] += _tn(ds_buf[slot], k_ref[keys(g), :])

        _three_stage(tiles * nq, scores, elementwise, grads)

    whole = lambda: pl.BlockSpec((None, s_len, LANE), lambda h, j: (h, 0, 0))
    stat = lambda: pl.BlockSpec((None, nq, 1, tq), lambda h, j: (h, 0, 0, 0))
    kvb = lambda: pl.BlockSpec((None, tiles * tk, LANE), lambda h, j: (h // group, j, 0))
    outb = lambda: pl.BlockSpec((None, tiles * tk, LANE), lambda h, j: (h, j, 0))
    shape = jax.ShapeDtypeStruct((n_heads, s_len, LANE), F32)
    return _pallas(
        body, name=name, grid=(n_heads, s_len // (tiles * tk)),
        in_specs=[whole(), whole(), stat(), stat(), kvb(), kvb()],
        out_specs=[whole(), outb(), outb()],
        out_shape=[shape, shape, shape],
        scratch_shapes=[pltpu.VMEM((2, tk, tq), F32), pltpu.VMEM((2, tk, tq), F32),
                        pltpu.VMEM((2, tk, tq), BF16), pltpu.VMEM((2, tk, tq), BF16)],
        compiler_params=_params(("parallel", "arbitrary"), VMEM_MID),
    )(q, do, lse, delta, k, v)


def _post(x, dh, pre, qbpre, kbpre, dgate, dqa, dka, dva, dqb, dkb, dvb, loss_part, tabs,
          w_in_ext, w_uq_pad, w_ukv_ext, gains, tm):
    s_len = x.shape[0]
    nt = s_len // tm

    def body(x_ref, dh_ref, pre_ref, qbpre_ref, kbpre_ref, dgate_ref,
             dqa_ref, dka_ref, dva_ref, dqb_ref, dkb_ref, dvb_ref, loss_ref,
             car_ref, cac_ref, sar_ref, sac_ref, cbr_ref, cbc_ref, sbr_ref, sbc_ref, win_ref, wuq_ref, wukv_ref,
             gin_ref, gaq_ref, gak_ref, gcq_ref, gckv_ref, gbq_ref, gbk_ref,
             gx_ref, dproj_ref, dqbpre_ref, dkvb_ref, dsm_ref):
        @pl.when(pl.program_id(0) == 0)
        def _():
            dsm_ref[...] = jnp.zeros_like(dsm_ref)
            dsm_ref[SM_LOSS:SM_LOSS + 1, 0:LANE] = loss_ref[0:1, :]

        def add_small(r, dg):
            dsm_ref[r:r + 1, 0:dg.shape[1]] += dg

        def tok_sum(a):
            return jnp.sum(a, axis=0, keepdims=True)

        ca, sa, cb, sb = _token_tables((car_ref, cac_ref, sar_ref, sac_ref, cbr_ref, cbc_ref, sbr_ref, sbc_ref))
        lane = lax.broadcasted_iota(jnp.int32, (tm, LANE), 1)

        nope_lanes = _lanes_of(lane, LAY_NOPE)

        def back(c0, c1):
            return _nt(dproj_ref[:, c0:c1], win_ref[:, c0:c1])

        dproj_ref[:, GA0:GA0 + N_GATE_C] = dgate_ref[...]
        dxn = back(GA0, GA0 + N_GATE_C)
        dg = jnp.zeros((1, LANE), F32)
        for h in range(A_HEADS):
            dn = _rope_bwd(dqa_ref[h] * SCALE_A, ca, sa)
            dx, dgr = _rms_bwd(dn, pre_ref[:, QA0 + LANE * h:QA0 + LANE * (h + 1)], gaq_ref[...], A_DIM)
            dproj_ref[:, QA0 + LANE * h:QA0 + LANE * (h + 1)] = dx.astype(BF16)
            dg = dg + tok_sum(dgr)
        add_small(SM_AQ, _unspread_row(dg, LAY_ROPE_A))
        dxn = dxn + back(QA0, KA0)
        dg = jnp.zeros((1, LANE), F32)
        for h in range(A_KV):
            dk = dka_ref[A_GROUP * h]
            dv = dva_ref[A_GROUP * h]
            for g in range(1, A_GROUP):
                dk = dk + dka_ref[A_GROUP * h + g]
                dv = dv + dva_ref[A_GROUP * h + g]
            dn = _rope_bwd(dk * LN2, ca, sa)
            dx, dgr = _rms_bwd(dn, pre_ref[:, KA0 + LANE * h:KA0 + LANE * (h + 1)], gak_ref[...], A_DIM)
            dproj_ref[:, KA0 + LANE * h:KA0 + LANE * (h + 1)] = dx.astype(BF16)
            dproj_ref[:, VA0 + LANE * h:VA0 + LANE * (h + 1)] = dv.astype(BF16)
            dg = dg + tok_sum(dgr)
        add_small(SM_AK, _unspread_row(dg, LAY_ROPE_A))
        dxn = dxn + back(KA0, GA0)
        dg = jnp.zeros((1, LANE), F32)
        for h in range(B_HEADS):
            cols = slice(LANE * h, LANE * (h + 1))
            dn = _rope_bwd(dqb_ref[h] * SCALE_B, cb, sb)
            dx, dgr = _rms_bwd(dn, qbpre_ref[:, cols], gbq_ref[...], B_QK)
            dqbpre_ref[:, cols] = dx.astype(BF16)
            dg = dg + tok_sum(dgr)
        add_small(SM_BQ, _unspread_row(dg, LAY_ROPE_B))
        dcq = _nt(dqbpre_ref[...], wuq_ref[...])
        dx, dgr = _rms_bwd(dcq, pre_ref[:, VA0:VA0 + B_Q_RANK], gcq_ref[...], B_Q_RANK)
        dproj_ref[:, CQ0:CQ0 + B_Q_RANK] = dx.astype(BF16)
        add_small(SM_CQ, tok_sum(dgr))
        dxn = dxn + back(CQ0, CKV0)
        dg = jnp.zeros((1, LANE), F32)
        dkr = jnp.zeros((tm, LANE), F32)
        for h in range(B_HEADS):
            cols = slice(LANE * h, LANE * (h + 1))
            dn = _rope_bwd(dkb_ref[h] * LN2, cb, sb)
            dx, dgr = _rms_bwd(dn, kbpre_ref[:, cols], gbk_ref[...], B_QK)
            dkvb_ref[:, cols] = jnp.where(nope_lanes, dx, 0.0).astype(BF16)
            dkvb_ref[:, B_HEADS * LANE + LANE * h:B_HEADS * LANE + LANE * (h + 1)] = dvb_ref[h].astype(BF16)
            dkr = dkr + dx
            dg = dg + tok_sum(dgr)
        add_small(SM_BK, _unspread_row(dg, LAY_ROPE_B))
        dproj_ref[:, KR0:KR0 + LANE] = jnp.where(_lanes_of(lane, LAY_KR), dkr, 0.0).astype(BF16)
        dckv = _nt(dkvb_ref[...], wukv_ref[...])
        dx, dgr = _rms_bwd(dckv, pre_ref[:, VA0 + B_Q_RANK:N_PRE], gckv_ref[...], B_KV_RANK)
        dproj_ref[:, CKV0:CKV0 + B_KV_RANK] = dx.astype(BF16)
        add_small(SM_CKV, tok_sum(dgr))
        dxn = dxn + back(CKV0, N_EXT)
        dx, dgr = _rms_bwd(dxn, x_ref[...], gin_ref[...], D_MODEL)
        gx_ref[...] = dh_ref[...] + dx
        add_small(SM_IN, tok_sum(dgr))

    row = lambda w: pl.BlockSpec((tm, w), lambda i: (i, 0))
    heads = lambda n: pl.BlockSpec((n, tm, LANE), lambda i: (0, i, 0))
    return _pallas(
        body, name="post", grid=(nt,),
        in_specs=[row(D_MODEL), row(D_MODEL), row(N_PRE), row(B_HEADS * LANE), row(B_HEADS * LANE), row(N_GATE_C),
                  heads(A_HEADS), heads(A_HEADS), heads(A_HEADS), heads(B_HEADS), heads(B_HEADS), heads(B_HEADS),
                  _full(loss_part.shape)] + _table_specs(tm)
                 + [_resident(w_in_ext.shape), _resident(w_uq_pad.shape), _resident(w_ukv_ext.shape)]
                 + [_full(g.shape) for g in gains],
        out_specs=[row(D_MODEL), row(N_EXT), row(B_HEADS * LANE), row(2 * B_HEADS * LANE), _full((SM_ROWS, SM_W))],
        out_shape=[jax.ShapeDtypeStruct((s_len, D_MODEL), F32), jax.ShapeDtypeStruct((s_len, N_EXT), BF16),
                   jax.ShapeDtypeStruct((s_len, B_HEADS * LANE), BF16),
                   jax.ShapeDtypeStruct((s_len, 2 * B_HEADS * LANE), BF16),
                   jax.ShapeDtypeStruct((SM_ROWS, SM_W), F32)],
        compiler_params=_params(("arbitrary",), VMEM_LIMIT),
    )(x, dh, pre, qbpre, kbpre, dgate, dqa, dka, dva, dqb, dkb, dvb, loss_part, *tabs,
      w_in_ext, w_uq_pad, w_ukv_ext, *gains)


def _grad_w(a_t, b, tn, ts, name):
    m, s_len = a_t.shape
    n = b.shape[1]

    def body(a_ref, b_ref, o_ref):
        @pl.when(pl.program_id(1) == 0)
        def _():
            o_ref[...] = jnp.zeros_like(o_ref)

        o_ref[...] += _nn(a_ref[...], b_ref[...].astype(BF16))

    return _pallas(
        body, name=name, grid=(n // tn, s_len // ts),
        in_specs=[pl.BlockSpec((m, ts), lambda j, t: (0, t)), pl.BlockSpec((ts, tn), lambda j, t: (t, j))],
        out_specs=pl.BlockSpec((m, tn), lambda j, t: (0, j)),
        out_shape=jax.ShapeDtypeStruct((m, n), F32),
        compiler_params=_params(("parallel", "arbitrary"), VMEM_MID),
    )(a_t, b)


def _grad_w_pairs(pairs, ts, name):
    s_len = pairs[0][0].shape[1]
    n_p = len(pairs)

    def body(*refs):
        for a_ref, b_ref, o_ref in zip(refs[0:2 * n_p:2], refs[1:2 * n_p:2], refs[2 * n_p:]):
            @pl.when(pl.program_id(0) == 0)
            def _():
                o_ref[...] = jnp.zeros_like(o_ref)

            o_ref[...] += _nn(a_ref[...], b_ref[...].astype(BF16))

    in_specs, flat = [], []
    for a_t, b in pairs:
        in_specs += [pl.BlockSpec((a_t.shape[0], ts), lambda t: (0, t)), pl.BlockSpec((ts, b.shape[1]), lambda t: (t, 0))]
        flat += [a_t, b]
    return _pallas(
        body, name=name, grid=(s_len // ts,),
        in_specs=in_specs,
        out_specs=[_full((a_t.shape[0], b.shape[1])) for a_t, b in pairs],
        out_shape=[jax.ShapeDtypeStruct((a_t.shape[0], b.shape[1]), F32) for a_t, b in pairs],
        compiler_params=_params(("arbitrary",), VMEM_MID),
    )(*flat)


def _adam_math(w, g, m, v):
    nm = ADAM_B1 * m + (1.0 - ADAM_B1) * g
    nv = ADAM_B2 * v + (1.0 - ADAM_B2) * (g * g)
    m_hat = nm / (1.0 - ADAM_B1 ** ADAM_STEP)
    v_hat = nv / (1.0 - ADAM_B2 ** ADAM_STEP)
    return -ADAM_LR * (m_hat / (jnp.sqrt(v_hat) + ADAM_EPS) + ADAM_WD * w), nm, nv


def _adamw_rows(w, g, m, v, tr):
    rows, cols = w.shape

    def body(w_ref, g_ref, m_ref, v_ref, d_ref, nm_ref, nv_ref):
        d_ref[...], nm_ref[...], nv_ref[...] = _adam_math(w_ref[...], g_ref[...], m_ref[...], v_ref[...])

    blk = pl.BlockSpec((tr, cols), lambda i: (i, 0))
    shape = jax.ShapeDtypeStruct((rows, cols), F32)
    return _pallas(
        body, name="adamw_w_in", grid=(rows // tr,),
        in_specs=[blk] * 4, out_specs=[blk] * 3, out_shape=[shape] * 3,
        compiler_params=_params(("parallel",), VMEM_SMALL),
    )(w, g, m, v)


def _adamw_rest(bigs, smalls, g_small):
    nb, ns = len(bigs), len(smalls)

    def body(*refs):
        ins, outs = refs[:4 * nb + 3 * ns + 1], refs[4 * nb + 3 * ns + 1:]
        for i in range(nb):
            w_ref, g_ref, m_ref, v_ref = ins[4 * i:4 * i + 4]
            d_ref, nm_ref, nv_ref = outs[3 * i:3 * i + 3]
            d_ref[...], nm_ref[...], nv_ref[...] = _adam_math(w_ref[...], g_ref[...], m_ref[...], v_ref[...])
        gs_ref = ins[-1]
        for i in range(ns):
            w_ref, m_ref, v_ref = ins[4 * nb + 3 * i:4 * nb + 3 * i + 3]
            g_ref, d_ref, nm_ref, nv_ref = outs[3 * nb + 4 * i:3 * nb + 4 * i + 4]
            g = gs_ref[i:i + 1, 0:w_ref.shape[1]]
            g_ref[...] = g
            d_ref[...], nm_ref[...], nv_ref[...] = _adam_math(w_ref[...], g, m_ref[...], v_ref[...])

    flat_in = [a for quad in bigs for a in quad] + [a for tri in smalls for a in tri] + [g_small]
    out_shape = ([jax.ShapeDtypeStruct(q[0].shape, F32) for q in bigs for _ in range(3)]
                 + [jax.ShapeDtypeStruct(t[0].shape, F32) for t in smalls for _ in range(4)])
    return _pallas(
        body, name="adamw_rest",
        in_specs=[pl.BlockSpec(memory_space=pltpu.VMEM)] * len(flat_in),
        out_specs=[pl.BlockSpec(memory_space=pltpu.VMEM)] * len(out_shape),
        out_shape=out_shape,
        compiler_params=_params(vmem=VMEM_SMALL),
    )(*flat_in)


def _place(pieces, n):
    out, at = [], 0
    for lane0, arr in sorted(pieces, key=lambda p: p[0]):
        out += [jnp.zeros((n, lane0 - at), F32), arr]
        at = lane0 + arr.shape[1]
    return jnp.concatenate(out + [jnp.zeros((n, LANE - at), F32)], axis=1)


def _rope_tables(s_len, tm):
    rows = s_len // GRID_W
    row = jnp.arange(rows, dtype=F32)
    col = jnp.arange(GRID_W, dtype=F32)

    def lay(dim, layout, first_dim, ones):
        half = dim // 2
        inv = 1.0 / (ROPE_THETA ** (jnp.arange(0, half, 2, dtype=F32) / half))
        ang_r, ang_c = row[:, None] * inv[None, :], col[:, None] * inv[None, :]
        at = {a - first_dim: lane0 for a, _, lane0 in layout}
        q = dim // 4
        r1, r2, c1, c2 = at[0], at[q], at[2 * q], at[3 * q]
        cos_r = _place([(r1, jnp.cos(ang_r)), (r2, jnp.cos(ang_r))], rows)
        sin_r = _place([(r1, -jnp.sin(ang_r)), (r2, jnp.sin(ang_r))], rows)
        cos_c = _place([(c1, jnp.cos(ang_c)), (c2, jnp.cos(ang_c))] + [(l0, jnp.ones((GRID_W, n), F32)) for _, n, l0 in ones],
                       GRID_W)
        sin_c = _place([(c1, -jnp.sin(ang_c)), (c2, jnp.sin(ang_c))], GRID_W)
        by_block = (s_len // tm, tm // GRID_W, LANE)
        return cos_r.reshape(by_block), cos_c, sin_r.reshape(by_block), sin_c

    return lay(A_DIM, LAY_ROPE_A, 0, ()) + lay(B_ROPE, LAY_KR, 0, LAY_NOPE)


def _spread(w, n_heads, dim, axis, layout):
    w3 = w.reshape(w.shape[:axis] + (n_heads, dim) + w.shape[axis + 1:])
    out, at = [], 0

    def zeros(n):
        return jnp.zeros(w3.shape[:axis + 1] + (n,) + w3.shape[axis + 2:], w.dtype)

    for a0, n, lane0 in sorted(layout, key=lambda seg: seg[2]):
        out += [zeros(lane0 - at), lax.slice_in_dim(w3, a0, a0 + n, axis=axis + 1)]
        at = lane0 + n
    out = jnp.concatenate(out + [zeros(LANE - at)], axis=axis + 1)
    return out.reshape(w.shape[:axis] + (n_heads * LANE,) + w.shape[axis + 1:])


def _head_cols(first, n_heads, dim, layout):
    out = np.full((n_heads * LANE,), -1, np.int32)
    for h in range(n_heads):
        for a0, n, lane0 in layout:
            out[h * LANE + lane0:h * LANE + lane0 + n] = first + h * dim + a0 + np.arange(n)
    return out


def _inverse(src, n):
    dst = np.full((n,), -1, np.int32)
    dst[src[src >= 0]] = np.nonzero(src >= 0)[0]
    return dst


def _column_maps():
    a_w, kv_w = A_HEADS * A_DIM, A_KV * A_DIM
    o_g = a_w + 2 * kv_w
    o_cq = o_g + a_w
    o_kr = o_cq + B_Q_RANK + B_KV_RANK
    src_in = np.concatenate([
        _head_cols(0, A_HEADS, A_DIM, LAY_ROPE_A), _head_cols(a_w, A_KV, A_DIM, LAY_ROPE_A),
        _head_cols(a_w + kv_w, A_KV, A_DIM, LAY_PLAIN_A), np.arange(o_g, o_g + a_w),
        np.arange(o_kr + B_ROPE, N_IN), np.arange(o_cq, o_kr), _head_cols(o_kr, 1, B_ROPE, LAY_KR)]).astype(np.int32)
    src_uq = _head_cols(0, B_HEADS, B_QK, LAY_ROPE_B)
    per = B_NOPE + B_V
    src_ukv = np.concatenate([_head_cols(0, B_HEADS, per, LAY_NOPE),
                              _head_cols(B_NOPE, B_HEADS, per, ((0, B_V, 0),))]).astype(np.int32)
    assert len(src_in) == N_EXT
    return src_in, src_uq, src_ukv


def _round_up(n, m):
    return (n + m - 1) // m * m


PERMUTE_STEPS = 8


def _permute_cols(xs, maps, stacks, name):
    maps = [np.asarray(m, np.int32) for m in maps]
    n_arr = len(xs)

    def block(ref, b):
        if len(ref.shape) == 2:
            return ref.at[:, b * LANE:(b + 1) * LANE]
        per = ref.shape[2] // LANE
        return ref.at[b // per, :, (b % per) * LANE:(b % per + 1) * LANE]

    def body(*refs):
        row = lax.broadcasted_iota(jnp.int32, (LANE, LANE), 0)
        for x_ref, src_ref, o_ref, src in zip(refs[:n_arr], refs[n_arr:2 * n_arr], refs[2 * n_arr:], maps):
            for c in range(len(src) // LANE):
                want = src[c * LANE:(c + 1) * LANE]
                if want[0] >= 0 and want[0] % LANE == 0 and np.array_equal(want, want[0] + np.arange(LANE)):
                    block(o_ref, c)[...] = block(x_ref, int(want[0]) // LANE)[...]
                    continue
                acc = jnp.zeros((x_ref.shape[-2], LANE), F32)
                for kb in sorted({int(v) // LANE for v in want if v >= 0}):
                    sel = jnp.where(row + kb * LANE == src_ref[:, c * LANE:(c + 1) * LANE], 1.0, 0.0).astype(BF16)
                    part = block(x_ref, kb)[...]
                    if part.dtype == BF16:
                        acc = acc + _nn(part, sel)
                    else:
                        hi = part.astype(BF16)
                        rest = part - hi.astype(F32)
                        mid = rest.astype(BF16)
                        low = (rest - mid.astype(F32)).astype(BF16)
                        acc = acc + ((_nn(hi, sel) + _nn(mid, sel)) + _nn(low, sel))
                block(o_ref, c)[...] = acc.astype(o_ref.dtype)

    def out_shape(x, m, stack):
        rows = x.shape[-2]
        return (rows, len(m)) if stack is None else (stack, rows, len(m) // stack)

    def rows_spec(shape):
        if len(shape) == 2:
            return pl.BlockSpec((shape[0] // PERMUTE_STEPS, shape[1]), lambda i: (i, 0))
        return pl.BlockSpec((shape[0], shape[1] // PERMUTE_STEPS, shape[2]), lambda i: (0, i, 0))

    shapes = [out_shape(x, m, st) for x, m, st in zip(xs, maps, stacks)]
    return _pallas(
        body, name=name, grid=(PERMUTE_STEPS,),
        in_specs=[rows_spec(x.shape) for x in xs] + [_full((1, len(m))) for m in maps],
        out_specs=[rows_spec(sh) for sh in shapes],
        out_shape=[jax.ShapeDtypeStruct(sh, x.dtype) for sh, x in zip(shapes, xs)],
        compiler_params=_params(("parallel",), VMEM_MID),
    )(*xs, *[jnp.asarray(m).reshape(1, -1) for m in maps])


def _pad_cols(w):
    return jnp.pad(w, ((0, 0), (0, _round_up(w.shape[1], LANE) - w.shape[1])))


def _in_stack(cols, width):
    cols = np.asarray(cols)
    return np.where(cols < 0, -1, cols // width * _round_up(width, LANE) + cols % width).astype(np.int32)


def _ext_weights(g_in, g_uq, g_ukv, g_out):
    src_in, src_uq, src_ukv = _column_maps()
    w_uq = g_uq.reshape(B_Q_RANK, B_HEADS * B_QK)
    w_out = g_out.reshape(D_MODEL, D_MODEL)
    w_in_ext, w_uq_pad, w_ukv_ext = _permute_cols(
        [g_in, w_uq, g_ukv], [_in_stack(src_in, SH_IN[1]), src_uq, _in_stack(src_ukv, SH_UKV[1])], [None] * 3, "lay_out_weights")
    return w_in_ext, w_uq_pad, w_ukv_ext, w_out


def _fold_grads(d_in_ext, d_uq_pad, d_ukv_ext, d_out):
    src_in, src_uq, src_ukv = _column_maps()

    def back(src, n, width):
        inv = _inverse(src, n)
        wide = _round_up(width, LANE)
        out = np.full((n // width * wide,), -1, np.int32)
        for j in range(n // width):
            out[j * wide:j * wide + width] = inv[j * width:(j + 1) * width]
        return out

    n_uq, n_ukv = B_HEADS * B_QK, B_HEADS * (B_NOPE + B_V)
    d_in, d_uq, d_ukv = _permute_cols(
        [d_in_ext, d_uq_pad, d_ukv_ext], [back(src_in, N_IN, SH_IN[1]), _inverse(src_uq, n_uq), back(src_ukv, n_ukv, SH_UKV[1])],
        [N_CHIPS, None, N_CHIPS], "fold_grads")
    return d_in, d_uq.reshape((N_CHIPS,) + SH_UQ), d_ukv, d_out.reshape((N_CHIPS,) + SH_OUT)


def kernel(x, norm_in, w_in, a_q_norm, a_k_norm, b_cq_norm, b_ckv_norm, w_uq, w_ukv, b_q_norm, b_k_norm, w_out, loss_target, m_norm_in, m_w_in, m_a_q_norm, m_a_k_norm, m_b_cq_norm, m_b_ckv_norm, m_w_uq, m_w_ukv, m_b_q_norm, m_b_k_norm, m_w_out, v_norm_in, v_w_in, v_a_q_norm, v_a_k_norm, v_b_cq_norm, v_b_ckv_norm, v_w_uq, v_w_ukv, v_b_q_norm, v_b_k_norm, v_w_out):
    s_len = x.shape[1]
    xs, ts = x[0], loss_target[0]
    tm = min(256, s_len)
    tq, tk_f = min(512, s_len // 2), min(2048, s_len // 2)
    tq_b, tk_b = min(1024, s_len // 2), min(512, s_len)
    tiles_f = min(4, s_len // tq)
    tiles_b = min(2, s_len // tk_b)

    w_in_ext, w_uq_pad, w_ukv_ext, w_out_full = _ext_weights(
        *_gather_weights((_pad_cols(w_in[0]), w_uq[0], _pad_cols(w_ukv[0]), w_out[0])))
    gains = (norm_in, _spread(a_q_norm, 1, A_DIM, 1, LAY_ROPE_A), _spread(a_k_norm, 1, A_DIM, 1, LAY_ROPE_A), b_cq_norm, b_ckv_norm,
             _spread(b_q_norm, 1, B_QK, 1, LAY_ROPE_B), _spread(b_k_norm, 1, B_QK, 1, LAY_ROPE_B))
    tabs = _rope_tables(s_len, tm)

    (xn_t, gates, pre, qbpre, kbpre, cq_t, ckv_t, qa, ka, va, qb, kb, vb) = _pre(
        xs, tabs, w_in_ext, w_uq_pad, w_ukv_ext, gains, tm)
    o_a, lse_a = _attn_fwd(qa, ka, va, A_GROUP, A_DIM, tq, tk_f, tiles_f, "attn_fwd_a")
    o_b, lse_b = _attn_fwd(qb, kb, vb, 1, B_V, tq, tk_f, tiles_f, "attn_fwd_b")
    y_t, dh, dgate, do_a, do_b, delta, loss_part = _mid(xs, ts, o_a, o_b, gates, w_out_full, min(512, s_len))

    def stat(a):
        return a.reshape(a.shape[0], s_len // tq_b, 1, tq_b)

    dqa, dka, dva = _attn_bwd(qa, ka, va, do_a, stat(lse_a), stat(delta[:A_HEADS]), A_GROUP, tq_b, tk_b, tiles_b, "attn_bwd_a")
    dqb, dkb, dvb = _attn_bwd(qb, kb, vb, do_b, stat(lse_b), stat(delta[A_HEADS:A_HEADS + B_HEADS]), 1, tq_b, tk_b,
                              tiles_b, "attn_bwd_b")
    grad_x, dproj, dqbpre, dkvb, d_small = _post(
        xs, dh, pre, qbpre, kbpre, dgate, dqa, dka, dva, dqb, dkb, dvb, loss_part, tabs,
        w_in_ext, w_uq_pad, w_ukv_ext, gains, tm)

    ts_w = min(2048, s_len)
    d_in_ext = _grad_w(xn_t, dproj, N_EXT // 2, ts_w, "grad_w_in")
    d_out_full = _grad_w(y_t, dh, D_MODEL, ts_w, "grad_w_out")
    d_uq_pad, d_ukv_ext = _grad_w_pairs([(cq_t, dqbpre), (ckv_t, dkvb)], ts_w, "grad_w_mla")

    g_in_p, g_uq, g_ukv_p, g_out, g_small = _reduce_grads(_fold_grads(d_in_ext, d_uq_pad, d_ukv_ext, d_out_full), d_small)
    g_in, g_ukv = g_in_p[:, :SH_IN[1]], g_ukv_p[:, :SH_UKV[1]]
    d_in, nm_in, nv_in = (a.T for a in _adamw_rows(w_in[0].T, g_in_p.T[:SH_IN[1]], m_w_in[0].T, v_w_in[0].T, SH_IN[1] // 7))
    rest = _adamw_rest(
        [(w_uq[0], g_uq, m_w_uq[0], v_w_uq[0]), (w_ukv[0], g_ukv, m_w_ukv[0], v_w_ukv[0]),
         (w_out[0], g_out, m_w_out[0], v_w_out[0])],
        [(norm_in, m_norm_in, v_norm_in), (a_q_norm, m_a_q_norm, v_a_q_norm), (a_k_norm, m_a_k_norm, v_a_k_norm),
         (b_cq_norm, m_b_cq_norm, v_b_cq_norm), (b_ckv_norm, m_b_ckv_norm, v_b_ckv_norm),
         (b_q_norm, m_b_q_norm, v_b_q_norm), (b_k_norm, m_b_k_norm, v_b_k_norm)], g_small)
    (d_uq, nm_uq, nv_uq), (d_ukv, nm_ukv, nv_ukv), (d_out, nm_out, nv_out) = (rest[3 * i:3 * i + 3] for i in range(3))
    sm = [rest[9 + 4 * i:9 + 4 * i + 4] for i in range(7)]

    def leaves(k, p_in, p_uq, p_ukv, p_out):
        return [sm[SM_IN][k], p_in[None], sm[SM_AQ][k], sm[SM_AK][k], sm[SM_CQ][k], sm[SM_CKV][k], p_uq[None], p_ukv[None],
                sm[SM_BQ][k], sm[SM_BK][k], p_out[None]]

    return (g_small[SM_LOSS, 0], grad_x[None], *leaves(0, g_in, g_uq, g_ukv, g_out), *leaves(1, d_in, d_uq, d_ukv, d_out),
            *leaves(2, nm_in, nm_uq, nm_ukv, nm_out), *leaves(3, nv_in, nv_uq, nv_ukv, nv_out))
```

```python
import jax
import jax.numpy as jnp
import numpy as np
from jax import lax
from jax.experimental import pallas as pl
from jax.experimental.pallas import tpu as pltpu

F32 = jnp.float32
BF16 = jnp.bfloat16
MESH = pl.DeviceIdType.MESH

D_MODEL = 1024
GRID_W = 64
ROPE_THETA = 10000.0
EPS = 1e-6
A_HEADS, A_KV, A_DIM = 8, 2, 64
A_GROUP = A_HEADS // A_KV
B_HEADS, B_NOPE, B_ROPE, B_V = 4, 64, 32, 128
B_QK = B_NOPE + B_ROPE
B_Q_RANK, B_KV_RANK = 384, 256
N_IN = 2464
SCALE_A = 1.0 / float(np.sqrt(A_DIM))
SCALE_B = 1.0 / float(np.sqrt(B_QK))
LOG2E = float(np.log2(np.e))
LN2 = float(np.log(2.0))
ADAM_LR, ADAM_B1, ADAM_B2, ADAM_EPS, ADAM_WD, ADAM_STEP = 0.001, 0.9, 0.999, 1e-08, 0.01, 10

LANE = 128
VMEM_BYTES = 64 * 1024 * 1024
VMEM_LIMIT = VMEM_BYTES - 8 * 1024 * 1024
VMEM_MID = 48 * 1024 * 1024
VMEM_SMALL = 32 * 1024 * 1024

QA0 = 0
KA0 = QA0 + A_HEADS * LANE
VA0 = KA0 + A_KV * LANE
GA0 = VA0 + A_KV * LANE
GB0 = GA0 + A_HEADS * A_DIM
CQ0 = GB0 + B_HEADS * LANE
CKV0 = CQ0 + B_Q_RANK
KR0 = CKV0 + B_KV_RANK
N_EXT = KR0 + LANE
N_GATE = (A_HEADS + B_HEADS) * LANE
N_GATE_C = A_HEADS * A_DIM + B_HEADS * LANE
DELTA_ROWS = 16
N_PRE = KA0 + A_KV * LANE + B_Q_RANK + B_KV_RANK

ROT = LANE // 2
_QA = A_DIM // 4
_QB = B_ROPE // 4
LAY_PLAIN_A = ((0, A_DIM, 0),)
LAY_ROPE_A = ((0, _QA, 0), (2 * _QA, _QA, _QA), (_QA, _QA, ROT), (3 * _QA, _QA, ROT + _QA))
LAY_KR = ((0, _QB, 0), (2 * _QB, _QB, _QB), (_QB, _QB, ROT), (3 * _QB, _QB, ROT + _QB))
LAY_NOPE = ((0, B_NOPE // 2, 2 * _QB), (B_NOPE // 2, B_NOPE // 2, ROT + 2 * _QB))
LAY_ROPE_B = LAY_NOPE + tuple((B_NOPE + a, n, at) for a, n, at in LAY_KR)

N_CHIPS = 4
SH_IN = (D_MODEL, N_IN // N_CHIPS)
SH_UQ = (B_Q_RANK // N_CHIPS, B_HEADS * B_QK)
SH_UKV = (B_KV_RANK, B_HEADS * (B_NOPE + B_V) // N_CHIPS)
SH_OUT = (D_MODEL // N_CHIPS, D_MODEL)
SM_ROWS, SM_W = 16, D_MODEL
SM_IN, SM_AQ, SM_AK, SM_CQ, SM_CKV, SM_BQ, SM_BK, SM_LOSS = range(8)
F32_ROWS, BF16_ROWS = 8, 16


def _pallas(body, **kw):
    return pl.pallas_call(body, **kw)


def _params(sem=None, vmem=None):
    return pltpu.CompilerParams(dimension_semantics=sem, vmem_limit_bytes=vmem)


def _rms_fwd(x, g, n):
    r = lax.rsqrt(jnp.sum(x * x, axis=-1, keepdims=True) * (1.0 / n) + EPS)
    return x * r * g


def _rms_bwd(dy, x, g, n):
    u = dy * g
    r = lax.rsqrt(jnp.sum(x * x, axis=-1, keepdims=True) * (1.0 / n) + EPS)
    ux = jnp.sum(u * x, axis=-1, keepdims=True)
    xhat = x * r
    dx = r * (u - xhat * (r * ux * (1.0 / n)))
    return dx, dy * xhat


def _rope_fwd(y, cos, sin):
    return y * cos + pltpu.roll(y, ROT, 1) * sin


def _rope_bwd(d, cos, sin):
    return d * cos - pltpu.roll(d, ROT, 1) * sin


def _token_tables(refs):
    out = []
    for r_ref, c_ref in zip(refs[0::2], refs[1::2]):
        r, c = r_ref[...], c_ref[...]
        out.append(jnp.concatenate([r[k:k + 1, :] + c for k in range(r.shape[0])], axis=0))
    return out


def _lanes_of(lane, layout):
    m = None
    for _, n, at in layout:
        seg = (lane >= at) & (lane < at + n)
        m = seg if m is None else (m | seg)
    return m


def _unspread_row(v, layout):
    v8 = jnp.broadcast_to(v, (F32_ROWS, LANE))
    lane = lax.broadcasted_iota(jnp.int32, v8.shape, 1)
    out = jnp.zeros_like(v8)
    for a, n, at in layout:
        moved = v8 if a == at else pltpu.roll(v8, (a - at) % LANE, 1)
        out = jnp.where((lane >= a) & (lane < a + n), moved, out)
    return out[0:1, :]


def _nt(a, b):
    return lax.dot_general(a, b, (((1,), (1,)), ((), ())), preferred_element_type=F32)


def _tn(a, b):
    return lax.dot_general(a, b, (((0,), (0,)), ((), ())), preferred_element_type=F32)


def _nn(a, b):
    return jnp.dot(a, b, preferred_element_type=F32)


def _block_rows(i, size):
    if isinstance(i, int):
        return pl.ds(i * size, size)
    return pl.ds(pl.multiple_of(i * size, size), size)


MAX_STATIC_BLOCKS = 32


def _three_stage(n, first, second, third):
    assert n >= 2 and n % 2 == 0
    first(0, 0)
    first(1, 1)
    second(0, 0)
    if n <= MAX_STATIC_BLOCKS:
        for i in range(1, n - 1):
            first(i + 1, (i + 1) % 2)
            second(i, i % 2)
            third(i - 1, (i - 1) % 2)
    else:
        def pair(t, carry):
            i = 2 * t + 1
            first(i + 1, 0)
            second(i, 1)
            third(i - 1, 0)
            first(i + 2, 1)
            second(i + 1, 0)
            third(i, 1)
            return carry

        lax.fori_loop(0, (n - 2) // 2, pair, 0)
    second(n - 1, 1)
    third(n - 2, 0)
    third(n - 1, 1)


def _full(shape):
    return pl.BlockSpec(shape, lambda *_: (0,) * len(shape))


def _table_specs(tm, n_tables=4):
    return [pl.BlockSpec((None, tm // GRID_W, LANE), lambda i: (i, 0, 0)), _full((GRID_W, LANE))] * n_tables


def _resident(shape):
    return pl.BlockSpec(shape, lambda *_: (0,) * len(shape), pipeline_mode=pl.Buffered(1))


def _gather_weights(shards):
    n = len(shards)
    halves = [w.shape[0] // 2 for w in shards]

    def body(*refs):
        w_refs, out_refs, (send_sems, recv_sems) = refs[:n], refs[n:2 * n], refs[2 * n:]
        x, y, c = lax.axis_index("x"), lax.axis_index("y"), lax.axis_index("c")
        sibling = (x, y, 1 - c)
        chips = [(1 - x, y), (x, 1 - y), (1 - x, 1 - y)]
        me = 2 * x + y

        def copy(a, k, j, hc, to):
            part = out_refs[a].at[j, pl.ds(pl.multiple_of(hc * halves[a], BF16_ROWS), halves[a]), :]
            return pltpu.make_async_remote_copy(
                src_ref=part, dst_ref=part, send_sem=send_sems.at[6 * a + k], recv_sem=recv_sems.at[6 * a + k],
                device_id=to, device_id_type=MESH)

        started = []
        for a in range(n):
            out_refs[a][me] = w_refs[a][...].astype(BF16)
            for k, chip in enumerate(chips):
                started.append(copy(a, k, me, c, (*chip, c)))
                started[-1].start()
        for k, chip in enumerate(chips):
            for a in range(n):
                copy(a, k, 2 * chip[0] + chip[1], c, (*chip, c)).wait_recv()
                started.append(copy(a, 3 + k, 2 * chip[0] + chip[1], c, sibling))
                started[-1].start()
        for k, chip in enumerate(chips):
            for a in range(n):
                copy(a, 3 + k, 2 * chip[0] + chip[1], 1 - c, sibling).wait_recv()
        for cp in started:
            cp.wait_send()

    return _pallas(
        body, name="gather_weights",
        out_shape=[jax.ShapeDtypeStruct((N_CHIPS,) + w.shape, BF16) for w in shards],
        in_specs=[pl.BlockSpec(memory_space=pltpu.VMEM)] * n,
        out_specs=[pl.BlockSpec(memory_space=pltpu.VMEM)] * n,
        scratch_shapes=[pltpu.SemaphoreType.DMA((6 * n,)), pltpu.SemaphoreType.DMA((6 * n,))],
        compiler_params=_params(vmem=VMEM_SMALL),
    )(*shards)


def _reduce_grads(parts, small):
    n_big = len(parts)
    n = n_big + 1
    shapes = [p.shape[1:] for p in parts] + [small.shape]
    halves = [sh[0] // 2 for sh in shapes]

    def body(*refs):
        p_refs, out_refs, rec_a, rec_b = refs[:n], refs[n:2 * n], refs[2 * n:3 * n], refs[3 * n:4 * n]
        send_b = refs[4 * n:4 * n + n_big]
        sa_send, sa_recv, sb_send, sb_recv, sc_send, sc_recv = refs[4 * n + n_big:]
        x, y, c = lax.axis_index("x"), lax.axis_index("y"), lax.axis_index("c")
        sibling = (x, y, 1 - c)
        me = 2 * x + y

        def rows(a, hc):
            return pl.ds(pl.multiple_of(hc * halves[a], F32_ROWS), halves[a])

        def partial(a, j, hc):
            return p_refs[a].at[j, rows(a, hc), :] if a < n_big else p_refs[a].at[rows(a, hc), :]

        def copy_a(a, j):
            return pltpu.make_async_remote_copy(
                src_ref=partial(a, j, 1 - c), dst_ref=rec_a[a].at[j],
                send_sem=sa_send.at[N_CHIPS * a + j], recv_sem=sa_recv.at[N_CHIPS * a + j],
                device_id=sibling, device_id_type=MESH)

        def copy_b(a, r):
            j = me ^ r
            k = (N_CHIPS - 1) * a + r - 1
            return pltpu.make_async_remote_copy(
                src_ref=(send_b[a] if a < n_big else rec_a[a]).at[j], dst_ref=rec_b[a].at[r],
                send_sem=sb_send.at[k], recv_sem=sb_recv.at[k], device_id=(j // 2, j % 2, c), device_id_type=MESH)

        def copy_c(a):
            return pltpu.make_async_remote_copy(
                src_ref=out_refs[a].at[rows(a, c), :], dst_ref=out_refs[a].at[rows(a, c), :],
                send_sem=sc_send.at[a], recv_sem=sc_recv.at[a], device_id=sibling, device_id_type=MESH)

        for a in range(n):
            for j in range(N_CHIPS):
                copy_a(a, j).start()
        for r in range(1, N_CHIPS):
            j = me ^ r
            for a in range(n):
                copy_a(a, j).wait_recv()
                chip_part = rec_a[a][j] + partial(a, j, c)[...]
                if a < n_big:
                    send_b[a][j] = chip_part.astype(BF16)
                else:
                    rec_a[a][j] = chip_part
                copy_b(a, r).start()
        for a in range(n):
            copy_a(a, me).wait_recv()
            rec_b[a][0] = (rec_a[a][me] + partial(a, me, c)[...]).astype(rec_b[a].dtype)
        for a in range(n):
            for r in range(1, N_CHIPS):
                copy_b(a, r).wait_recv()
            total = rec_b[a][me].astype(F32)
            for j in range(1, N_CHIPS):
                total = total + rec_b[a][j ^ me].astype(F32)
            out_refs[a][rows(a, c), :] = total
            copy_c(a).start()
        for a in range(n):
            copy_c(a).wait_recv()
        for a in range(n):
            for j in range(N_CHIPS):
                copy_a(a, j).wait_send()
            for r in range(1, N_CHIPS):
                copy_b(a, r).wait_send()
            copy_c(a).wait_send()

    dma = pltpu.SemaphoreType.DMA
    return _pallas(
        body, name="reduce_grads",
        out_shape=[jax.ShapeDtypeStruct(sh, F32) for sh in shapes],
        in_specs=[pl.BlockSpec(memory_space=pltpu.VMEM)] * n,
        out_specs=[pl.BlockSpec(memory_space=pltpu.VMEM)] * n,
        scratch_shapes=[pltpu.VMEM((N_CHIPS, h) + sh[1:], F32) for h, sh in zip(halves, shapes)]
                       + [pltpu.VMEM((N_CHIPS, h) + sh[1:], BF16 if a < n_big else F32)
                          for a, (h, sh) in enumerate(zip(halves, shapes))]
                       + [pltpu.VMEM((N_CHIPS, h) + sh[1:], BF16) for h, sh in zip(halves[:n_big], shapes[:n_big])]
                       + [dma((N_CHIPS * n,)), dma((N_CHIPS * n,)), dma(((N_CHIPS - 1) * n,)), dma(((N_CHIPS - 1) * n,)),
                          dma((n,)), dma((n,))],
        compiler_params=_params(vmem=VMEM_LIMIT),
    )(*parts, small)


def _pre(x, tabs, w_in_ext, w_uq_pad, w_ukv_ext, gains, tm):
    s_len = x.shape[0]
    nt = s_len // tm

    def body(x_ref, car_ref, cac_ref, sar_ref, sac_ref, cbr_ref, cbc_ref, sbr_ref, sbc_ref, win_ref, wuq_ref, wukv_ref,
             gin_ref, gaq_ref, gak_ref, gcq_ref, gckv_ref, gbq_ref, gbk_ref,
             xn_ref, gates_ref, pre_ref, qbpre_ref, kbpre_ref, cq_ref, ckv_ref,
             qa_ref, ka_ref, va_ref, qb_ref, kb_ref, vb_ref, proj):
        xn = _rms_fwd(x_ref[...], gin_ref[...], D_MODEL)
        xn_ref[...] = jnp.transpose(xn).astype(BF16)
        xb = xn.astype(BF16)
        pre_ref[:, 0:VA0] = _nn(xb, win_ref[:, 0:VA0])
        gates_ref[...] = _nn(xb, win_ref[:, GA0:GA0 + N_GATE_C])
        pre_ref[:, VA0:N_PRE] = _nn(xb, win_ref[:, CQ0:KR0])
        proj[...] = _nn(xb, win_ref[:, VA0:GA0])
        kr = _nn(xb, win_ref[:, KR0:N_EXT])
        ca, sa, cb, sb = _token_tables((car_ref, cac_ref, sar_ref, sac_ref, cbr_ref, cbc_ref, sbr_ref, sbc_ref))
        lane = lax.broadcasted_iota(jnp.int32, (tm, LANE), 1)
        for h in range(A_HEADS):
            yq = _rms_fwd(pre_ref[:, QA0 + LANE * h:QA0 + LANE * (h + 1)], gaq_ref[...], A_DIM)
            qa_ref[h] = (_rope_fwd(yq, ca, sa) * (SCALE_A * LOG2E)).astype(BF16)
        for h in range(A_KV):
            yk = _rms_fwd(pre_ref[:, KA0 + LANE * h:KA0 + LANE * (h + 1)], gak_ref[...], A_DIM)
            ka_ref[h] = _rope_fwd(yk, ca, sa).astype(BF16)
            va_ref[h] = jnp.where(lane == A_DIM, 1.0, proj[:, LANE * h:LANE * (h + 1)]).astype(BF16)
        cq = _rms_fwd(pre_ref[:, VA0:VA0 + B_Q_RANK], gcq_ref[...], B_Q_RANK)
        cq_ref[...] = jnp.transpose(cq).astype(BF16)
        qbpre_ref[...] = _nn(cq.astype(BF16), wuq_ref[...])
        ckv = _rms_fwd(pre_ref[:, VA0 + B_Q_RANK:N_PRE], gckv_ref[...], B_KV_RANK)
        ckv_ref[...] = jnp.transpose(ckv).astype(BF16)
        kvb = _nn(ckv.astype(BF16), wukv_ref[...])
        for h in range(B_HEADS):
            yq = _rms_fwd(qbpre_ref[:, LANE * h:LANE * (h + 1)], gbq_ref[...], B_QK)
            qb_ref[h] = (_rope_fwd(yq, cb, sb) * (SCALE_B * LOG2E)).astype(BF16)
            kp = kvb[:, LANE * h:LANE * (h + 1)] + kr
            kbpre_ref[:, LANE * h:LANE * (h + 1)] = kp
            kb_ref[h] = _rope_fwd(_rms_fwd(kp, gbk_ref[...], B_QK), cb, sb).astype(BF16)
            vb_ref[h, :, 0:LANE] = kvb[:, B_HEADS * LANE + LANE * h:B_HEADS * LANE + LANE * (h + 1)].astype(BF16)
            vb_ref[h, :, LANE:2 * LANE] = jnp.where(lane == 0, 1.0, 0.0).astype(BF16)

    row = lambda w: pl.BlockSpec((tm, w), lambda i: (i, 0))
    col = lambda w: pl.BlockSpec((w, tm), lambda i: (0, i))
    heads = lambda n: pl.BlockSpec((n, tm, LANE), lambda i: (0, i, 0))
    hs = lambda n: jax.ShapeDtypeStruct((n, s_len, LANE), BF16)
    return _pallas(
        body, name="pre", grid=(nt,),
        in_specs=[row(D_MODEL)] + _table_specs(tm)
                 + [_resident(w_in_ext.shape), _resident(w_uq_pad.shape), _resident(w_ukv_ext.shape)]
                 + [_full(g.shape) for g in gains],
        out_specs=[col(D_MODEL), row(N_GATE_C), row(N_PRE), row(B_HEADS * LANE), row(B_HEADS * LANE),
                   col(B_Q_RANK), col(B_KV_RANK),
                   heads(A_HEADS), heads(A_KV), heads(A_KV), heads(B_HEADS), heads(B_HEADS),
                   pl.BlockSpec((B_HEADS, tm, 2 * LANE), lambda i: (0, i, 0))],
        out_shape=[jax.ShapeDtypeStruct((D_MODEL, s_len), BF16), jax.ShapeDtypeStruct((s_len, N_GATE_C), F32),
                   jax.ShapeDtypeStruct((s_len, N_PRE), F32), jax.ShapeDtypeStruct((s_len, B_HEADS * LANE), F32),
                   jax.ShapeDtypeStruct((s_len, B_HEADS * LANE), F32),
                   jax.ShapeDtypeStruct((B_Q_RANK, s_len), BF16), jax.ShapeDtypeStruct((B_KV_RANK, s_len), BF16),
                   hs(A_HEADS), hs(A_KV), hs(A_KV), hs(B_HEADS), hs(B_HEADS),
                   jax.ShapeDtypeStruct((B_HEADS, s_len, 2 * LANE), BF16)],
        scratch_shapes=[pltpu.VMEM((tm, A_KV * LANE), F32)],
        compiler_params=_params(("parallel",), VMEM_LIMIT),
    )(x, *tabs, w_in_ext, w_uq_pad, w_ukv_ext, *gains)


def _attn_fwd(q, k, v, group, l_col, tq, tk, tiles, name):
    n_heads, s_len, _ = q.shape
    v_w = v.shape[2]
    nk = s_len // tk

    def body(q_ref, k_ref, v_ref, o_ref, lse_ref, s_buf, p_buf, a_buf, m_ref, acc_ref):
        def scores(g, slot):
            s_buf[slot] = _nt(q_ref[_block_rows(g // nk, tq), :], k_ref[_block_rows(g % nk, tk), :])

        def softmax(g, slot):
            t = g // nk
            s = s_buf[slot]
            m_old = m_ref[t]
            m_new = jnp.maximum(m_old, jnp.max(s, axis=-1, keepdims=True))
            m_ref[t] = m_new
            a_buf[slot] = jnp.exp2(m_old - m_new)
            p_buf[slot] = jnp.exp2(s - jnp.tile(m_new, (1, tk // LANE))).astype(BF16)

        def values(g, slot):
            t = g // nk
            pv = _nn(p_buf[slot], v_ref[_block_rows(g % nk, tk), :])
            for c in range(0, v_w, LANE):
                acc_ref[t, :, c:c + LANE] = a_buf[slot] * acc_ref[t, :, c:c + LANE] + pv[:, c:c + LANE]

        m_ref[...] = jnp.full(m_ref.shape, -1e30, F32)
        acc_ref[...] = jnp.zeros(acc_ref.shape, F32)
        _three_stage(tiles * nk, scores, softmax, values)
        for t in range(tiles):
            l = acc_ref[t, :, l_col:l_col + 1]
            o = acc_ref[t, :, 0:LANE] * (1.0 / l)
            if l_col < LANE:
                lane = lax.broadcasted_iota(jnp.int32, o.shape, 1)
                o = jnp.where(lane == l_col, 0.0, o)
            o_ref[t * tq:(t + 1) * tq, :] = o
            lse_ref[t] = jnp.transpose(m_ref[t] + jnp.log2(jnp.broadcast_to(l, (tq, LANE))))[0:1, :]

    return _pallas(
        body, name=name, grid=(n_heads, s_len // (tiles * tq)),
        in_specs=[pl.BlockSpec((None, tiles * tq, LANE), lambda h, i: (h, i, 0)),
                  pl.BlockSpec((None, s_len, LANE), lambda h, i: (h // group, 0, 0)),
                  pl.BlockSpec((None, s_len, v_w), lambda h, i: (h // group, 0, 0))],
        out_specs=[pl.BlockSpec((None, tiles * tq, LANE), lambda h, i: (h, i, 0)),
                   pl.BlockSpec((None, tiles, 1, tq), lambda h, i: (h, i, 0, 0))],
        out_shape=[jax.ShapeDtypeStruct((n_heads, s_len, LANE), F32),
                   jax.ShapeDtypeStruct((n_heads, s_len // tq, 1, tq), F32)],
        scratch_shapes=[pltpu.VMEM((2, tq, tk), F32), pltpu.VMEM((2, tq, tk), BF16), pltpu.VMEM((2, tq, LANE), F32),
                        pltpu.VMEM((tiles, tq, LANE), F32), pltpu.VMEM((tiles, tq, v_w), F32)],
        compiler_params=_params(("parallel", "parallel"), VMEM_MID),
    )(q, k, v)


def _mid(x, target, o_a, o_b, gates, w_out, tm):
    s_len = x.shape[0]
    nt = s_len // tm
    n_heads = A_HEADS + B_HEADS
    d_mix = w_out.shape[0]
    pairs = A_HEADS // 2

    def body(x_ref, t_ref, oa_ref, ob_ref, g_ref, w_ref,
             yt_ref, dh_ref, dgate_ref, doa_ref, dob_ref, delta_ref, loss_ref, silu_scr, dsilu_scr, y_ref):
        @pl.when(pl.program_id(0) == 0)
        def _():
            loss_ref[...] = jnp.zeros_like(loss_ref)

        def o_of(h):
            return oa_ref[h] if h < A_HEADS else ob_ref[h - A_HEADS]

        lane = lax.broadcasted_iota(jnp.int32, (tm, LANE), 1)

        def gated(h):
            cols = slice(LANE * h, LANE * (h + 1))
            if h < A_HEADS:
                packed = g_ref[:, LANE * (h // 2):LANE * (h // 2 + 1)]
                g = jnp.where(lane < A_DIM, packed if h % 2 == 0 else pltpu.roll(packed, ROT, 1), 0.0)
            else:
                g = g_ref[:, LANE * (pairs + h - A_HEADS):LANE * (pairs + h - A_HEADS + 1)]
            sig = 1.0 / (1.0 + jnp.exp(-g))
            silu = g * sig
            silu_scr[:, cols] = silu
            dsilu_scr[:, cols] = sig * (1.0 + g * (1.0 - sig))
            return o_of(h) * silu

        for c in range(pairs + B_HEADS):
            y = gated(2 * c) + pltpu.roll(gated(2 * c + 1), ROT, 1) if c < pairs else gated(A_HEADS + c - pairs)
            cols = slice(LANE * c, LANE * (c + 1))
            y_ref[:, cols] = y.astype(BF16)
            yt_ref[cols, :] = jnp.transpose(y).astype(BF16)
        err = x_ref[...] + _nn(y_ref[...], w_ref[...]) - t_ref[...]
        sq = jnp.sum(jnp.sum(err * err, axis=-1, keepdims=True), axis=0, keepdims=True)
        loss_ref[...] += jnp.broadcast_to(sq * (0.5 / D_MODEL), loss_ref.shape)
        dh = err * (1.0 / D_MODEL)
        dh_ref[...] = dh
        dy = _nt(dh.astype(BF16), w_ref[...])
        delta = jnp.zeros((tm, LANE), F32)
        held = None
        for h in range(n_heads):
            cols = slice(LANE * h, LANE * (h + 1))
            if h < A_HEADS:
                packed = dy[:, LANE * (h // 2):LANE * (h // 2 + 1)]
                dyh = packed if h % 2 == 0 else pltpu.roll(packed, ROT, 1)
            else:
                dyh = dy[:, LANE * (pairs + h - A_HEADS):LANE * (pairs + h - A_HEADS + 1)]
            oh = o_of(h)
            do = dyh * silu_scr[:, cols]
            dg = dyh * oh * dsilu_scr[:, cols]
            if h >= A_HEADS:
                dgate_ref[:, LANE * (pairs + h - A_HEADS):LANE * (pairs + h - A_HEADS + 1)] = dg.astype(BF16)
            elif h % 2 == 0:
                held = dg
            else:
                dgate_ref[:, LANE * (h // 2):LANE * (h // 2 + 1)] = (held + pltpu.roll(dg, ROT, 1)).astype(BF16)
            delta = jnp.where(lane == h, jnp.sum(do * oh, axis=-1, keepdims=True), delta)
            if h < A_HEADS:
                doa_ref[h] = do.astype(BF16)
            else:
                dob_ref[h - A_HEADS] = do.astype(BF16)
        delta_ref[...] = jnp.transpose(delta)[0:DELTA_ROWS, :]

    row = lambda w: pl.BlockSpec((tm, w), lambda i: (i, 0))
    heads = lambda n, w=LANE: pl.BlockSpec((n, tm, w), lambda i: (0, i, 0))
    return _pallas(
        body, name="mid", grid=(nt,),
        in_specs=[row(D_MODEL), row(D_MODEL), heads(A_HEADS), heads(B_HEADS), row(N_GATE_C), _resident(w_out.shape)],
        out_specs=[pl.BlockSpec((d_mix, tm), lambda i: (0, i)), row(D_MODEL), row(N_GATE_C), heads(A_HEADS), heads(B_HEADS),
                   pl.BlockSpec((DELTA_ROWS, tm), lambda i: (0, i)),
                   _full((8, LANE))],
        out_shape=[jax.ShapeDtypeStruct((d_mix, s_len), BF16), jax.ShapeDtypeStruct((s_len, D_MODEL), F32),
                   jax.ShapeDtypeStruct((s_len, N_GATE_C), BF16),
                   jax.ShapeDtypeStruct((A_HEADS, s_len, LANE), BF16), jax.ShapeDtypeStruct((B_HEADS, s_len, LANE), BF16),
                   jax.ShapeDtypeStruct((DELTA_ROWS, s_len), F32), jax.ShapeDtypeStruct((8, LANE), F32)],
        scratch_shapes=[pltpu.VMEM((tm, N_GATE), F32), pltpu.VMEM((tm, N_GATE), F32), pltpu.VMEM((tm, d_mix), BF16)],
        compiler_params=_params(("arbitrary",), VMEM_LIMIT),
    )(x, target, o_a, o_b, gates, w_out)


def _attn_bwd(q, k, v, do, lse, delta, group, tq, tk, tiles, name):
    n_heads, s_len, _ = q.shape
    nq = s_len // tq

    def body(q_ref, do_ref, lse_ref, delta_ref, k_ref, v_ref, dq_ref, dk_ref, dv_ref, s_buf, dp_buf, p_buf, ds_buf):
        @pl.when(pl.program_id(1) == 0)
        def _():
            dq_ref[...] = jnp.zeros_like(dq_ref)

        dk_ref[...] = jnp.zeros_like(dk_ref)
        dv_ref[...] = jnp.zeros_like(dv_ref)

        def keys(g):
            return _block_rows(g // nq, tk)

        def queries(g):
            return _block_rows(g % nq, tq)

        def scores(g, slot):
            s_buf[slot] = _nt(k_ref[keys(g), :], q_ref[queries(g), :])
            dp_buf[slot] = _nt(v_ref[keys(g), :], do_ref[queries(g), :])

        def elementwise(g, slot):
            p = jnp.exp2(s_buf[slot] - lse_ref[g % nq])
            p_buf[slot] = p.astype(BF16)
            ds_buf[slot] = (p * (dp_buf[slot] - delta_ref[g % nq])).astype(BF16)

        def grads(g, slot):
            dv_ref[keys(g), :] += _nn(p_buf[slot], do_ref[queries(g), :])
            dk_ref[keys(g), :] += _nn(ds_buf[slot], q_ref[queries(g), :])
            dq_ref[queries(g), :] += _tn(ds_buf[slot], k_ref[keys(g), :])

        _three_stage(tiles * nq, scores, elementwise, grads)

    whole = lambda: pl.BlockSpec((None, s_len, LANE), lambda h, j: (h, 0, 0))
    stat = lambda: pl.BlockSpec((None, nq, 1, tq), lambda h, j: (h, 0, 0, 0))
    kvb = lambda: pl.BlockSpec((None, tiles * tk, LANE), lambda h, j: (h // group, j, 0))
    outb = lambda: pl.BlockSpec((None, tiles * tk, LANE), lambda h, j: (h, j, 0))
    shape = jax.ShapeDtypeStruct((n_heads, s_len, LANE), F32)
    return _pallas(
        body, name=name, grid=(n_heads, s_len // (tiles * tk)),
        in_specs=[whole(), whole(), stat(), stat(), kvb(), kvb()],
        out_specs=[whole(), outb(), outb()],
        out_shape=[shape, shape, shape],
        scratch_shapes=[pltpu.VMEM((2, tk, tq), F32), pltpu.VMEM((2, tk, tq), F32),
                        pltpu.VMEM((2, tk, tq), BF16), pltpu.VMEM((2, tk, tq), BF16)],
        compiler_params=_params(("parallel", "arbitrary"), VMEM_MID),
    )(q, do, lse, delta, k, v)


def _post(x, dh, pre, qbpre, kbpre, dgate, dqa, dka, dva, dqb, dkb, dvb, loss_part, tabs,
          w_in_ext, w_uq_pad, w_ukv_ext, gains, tm):
    s_len = x.shape[0]
    nt = s_len // tm

    def body(x_ref, dh_ref, pre_ref, qbpre_ref, kbpre_ref, dgate_ref,
             dqa_ref, dka_ref, dva_ref, dqb_ref, dkb_ref, dvb_ref, loss_ref,
             t0, t1, t2, t3, t4, t5, t6, t7, t8, t9, t10, t11, t12, t13, t14, t15, win_ref, wuq_ref, wukv_ref,
             gin_ref, gaq_ref, gak_ref, gcq_ref, gckv_ref, gbq_ref, gbk_ref,
             gx_ref, dproj_ref, dqbpre_ref, dkvb_ref, dsm_ref):
        @pl.when(pl.program_id(0) == 0)
        def _():
            dsm_ref[...] = jnp.zeros_like(dsm_ref)
            dsm_ref[SM_LOSS:SM_LOSS + 1, 0:LANE] = loss_ref[0:1, :]

        def add_small(r, dg):
            dsm_ref[r:r + 1, 0:dg.shape[1]] += dg

        def tok_sum(a):
            return jnp.sum(a, axis=0, keepdims=True)

        caq, saq, cak, sak, cbq, sbq, cbk, sbk = _token_tables(
            (t0, t1, t2, t3, t4, t5, t6, t7, t8, t9, t10, t11, t12, t13, t14, t15))
        lane = lax.broadcasted_iota(jnp.int32, (tm, LANE), 1)

        nope_lanes = _lanes_of(lane, LAY_NOPE)

        def back(c0, c1):
            return _nt(dproj_ref[:, c0:c1], win_ref[:, c0:c1])

        dproj_ref[:, GA0:GA0 + N_GATE_C] = dgate_ref[...]
        dxn = back(GA0, GA0 + N_GATE_C)
        dg = jnp.zeros((1, LANE), F32)
        for h in range(A_HEADS):
            dn = _rope_bwd(dqa_ref[h], caq, saq)
            dx, dgr = _rms_bwd(dn, pre_ref[:, QA0 + LANE * h:QA0 + LANE * (h + 1)], gaq_ref[...], A_DIM)
            dproj_ref[:, QA0 + LANE * h:QA0 + LANE * (h + 1)] = dx.astype(BF16)
            dg = dg + tok_sum(dgr)
        add_small(SM_AQ, _unspread_row(dg, LAY_ROPE_A))
        dxn = dxn + back(QA0, KA0)
        dg = jnp.zeros((1, LANE), F32)
        for h in range(A_KV):
            dk = dka_ref[A_GROUP * h]
            dv = dva_ref[A_GROUP * h]
            for g in range(1, A_GROUP):
                dk = dk + dka_ref[A_GROUP * h + g]
                dv = dv + dva_ref[A_GROUP * h + g]
            dn = _rope_bwd(dk, cak, sak)
            dx, dgr = _rms_bwd(dn, pre_ref[:, KA0 + LANE * h:KA0 + LANE * (h + 1)], gak_ref[...], A_DIM)
            dproj_ref[:, KA0 + LANE * h:KA0 + LANE * (h + 1)] = dx.astype(BF16)
            dproj_ref[:, VA0 + LANE * h:VA0 + LANE * (h + 1)] = dv.astype(BF16)
            dg = dg + tok_sum(dgr)
        add_small(SM_AK, _unspread_row(dg, LAY_ROPE_A))
        dxn = dxn + back(KA0, GA0)
        dg = jnp.zeros((1, LANE), F32)
        for h in range(B_HEADS):
            cols = slice(LANE * h, LANE * (h + 1))
            dn = _rope_bwd(dqb_ref[h], cbq, sbq)
            dx, dgr = _rms_bwd(dn, qbpre_ref[:, cols], gbq_ref[...], B_QK)
            dqbpre_ref[:, cols] = dx.astype(BF16)
            dg = dg + tok_sum(dgr)
        add_small(SM_BQ, _unspread_row(dg, LAY_ROPE_B))
        dcq = _nt(dqbpre_ref[...], wuq_ref[...])
        dx, dgr = _rms_bwd(dcq, pre_ref[:, VA0:VA0 + B_Q_RANK], gcq_ref[...], B_Q_RANK)
        dproj_ref[:, CQ0:CQ0 + B_Q_RANK] = dx.astype(BF16)
        add_small(SM_CQ, tok_sum(dgr))
        dxn = dxn + back(CQ0, CKV0)
        dg = jnp.zeros((1, LANE), F32)
        dkr = jnp.zeros((tm, LANE), F32)
        for h in range(B_HEADS):
            cols = slice(LANE * h, LANE * (h + 1))
            dn = _rope_bwd(dkb_ref[h], cbk, sbk)
            dx, dgr = _rms_bwd(dn, kbpre_ref[:, cols], gbk_ref[...], B_QK)
            dkvb_ref[:, cols] = jnp.where(nope_lanes, dx, 0.0).astype(BF16)
            dkvb_ref[:, B_HEADS * LANE + LANE * h:B_HEADS * LANE + LANE * (h + 1)] = dvb_ref[h].astype(BF16)
            dkr = dkr + dx
            dg = dg + tok_sum(dgr)
        add_small(SM_BK, _unspread_row(dg, LAY_ROPE_B))
        dproj_ref[:, KR0:KR0 + LANE] = jnp.where(_lanes_of(lane, LAY_KR), dkr, 0.0).astype(BF16)
        dckv = _nt(dkvb_ref[...], wukv_ref[...])
        dx, dgr = _rms_bwd(dckv, pre_ref[:, VA0 + B_Q_RANK:N_PRE], gckv_ref[...], B_KV_RANK)
        dproj_ref[:, CKV0:CKV0 + B_KV_RANK] = dx.astype(BF16)
        add_small(SM_CKV, tok_sum(dgr))
        dxn = dxn + back(CKV0, N_EXT)
        dx, dgr = _rms_bwd(dxn, x_ref[...], gin_ref[...], D_MODEL)
        gx_ref[...] = dh_ref[...] + dx
        add_small(SM_IN, tok_sum(dgr))

    row = lambda w: pl.BlockSpec((tm, w), lambda i: (i, 0))
    heads = lambda n: pl.BlockSpec((n, tm, LANE), lambda i: (0, i, 0))
    return _pallas(
        body, name="post", grid=(nt,),
        in_specs=[row(D_MODEL), row(D_MODEL), row(N_PRE), row(B_HEADS * LANE), row(B_HEADS * LANE), row(N_GATE_C),
                  heads(A_HEADS), heads(A_HEADS), heads(A_HEADS), heads(B_HEADS), heads(B_HEADS), heads(B_HEADS),
                  _full(loss_part.shape)] + _table_specs(tm, len(tabs) // 2)
                 + [_resident(w_in_ext.shape), _resident(w_uq_pad.shape), _resident(w_ukv_ext.shape)]
                 + [_full(g.shape) for g in gains],
        out_specs=[row(D_MODEL), row(N_EXT), row(B_HEADS * LANE), row(2 * B_HEADS * LANE), _full((SM_ROWS, SM_W))],
        out_shape=[jax.ShapeDtypeStruct((s_len, D_MODEL), F32), jax.ShapeDtypeStruct((s_len, N_EXT), BF16),
                   jax.ShapeDtypeStruct((s_len, B_HEADS * LANE), BF16),
                   jax.ShapeDtypeStruct((s_len, 2 * B_HEADS * LANE), BF16),
                   jax.ShapeDtypeStruct((SM_ROWS, SM_W), F32)],
        compiler_params=_params(("arbitrary",), VMEM_LIMIT),
    )(x, dh, pre, qbpre, kbpre, dgate, dqa, dka, dva, dqb, dkb, dvb, loss_part, *tabs,
      w_in_ext, w_uq_pad, w_ukv_ext, *gains)


def _grad_w(a_t, b, tn, ts, name):
    m, s_len = a_t.shape
    n = b.shape[1]

    def body(a_ref, b_ref, o_ref):
        @pl.when(pl.program_id(1) == 0)
        def _():
            o_ref[...] = jnp.zeros_like(o_ref)

        o_ref[...] += _nn(a_ref[...], b_ref[...].astype(BF16))

    return _pallas(
        body, name=name, grid=(n // tn, s_len // ts),
        in_specs=[pl.BlockSpec((m, ts), lambda j, t: (0, t)), pl.BlockSpec((ts, tn), lambda j, t: (t, j))],
        out_specs=pl.BlockSpec((m, tn), lambda j, t: (0, j)),
        out_shape=jax.ShapeDtypeStruct((m, n), F32),
        compiler_params=_params(("parallel", "arbitrary"), VMEM_MID),
    )(a_t, b)


def _grad_w_pairs(pairs, ts, name):
    s_len = pairs[0][0].shape[1]
    n_p = len(pairs)

    def body(*refs):
        for a_ref, b_ref, o_ref in zip(refs[0:2 * n_p:2], refs[1:2 * n_p:2], refs[2 * n_p:]):
            @pl.when(pl.program_id(0) == 0)
            def _():
                o_ref[...] = jnp.zeros_like(o_ref)

            o_ref[...] += _nn(a_ref[...], b_ref[...].astype(BF16))

    in_specs, flat = [], []
    for a_t, b in pairs:
        in_specs += [pl.BlockSpec((a_t.shape[0], ts), lambda t: (0, t)), pl.BlockSpec((ts, b.shape[1]), lambda t: (t, 0))]
        flat += [a_t, b]
    return _pallas(
        body, name=name, grid=(s_len // ts,),
        in_specs=in_specs,
        out_specs=[_full((a_t.shape[0], b.shape[1])) for a_t, b in pairs],
        out_shape=[jax.ShapeDtypeStruct((a_t.shape[0], b.shape[1]), F32) for a_t, b in pairs],
        compiler_params=_params(("arbitrary",), VMEM_MID),
    )(*flat)


def _adam_math(w, g, m, v):
    nm = ADAM_B1 * m + (1.0 - ADAM_B1) * g
    nv = ADAM_B2 * v + (1.0 - ADAM_B2) * (g * g)
    m_hat = nm / (1.0 - ADAM_B1 ** ADAM_STEP)
    v_hat = nv / (1.0 - ADAM_B2 ** ADAM_STEP)
    return -ADAM_LR * (m_hat / (jnp.sqrt(v_hat) + ADAM_EPS) + ADAM_WD * w), nm, nv


def _adamw_rows(w, g, m, v, tr):
    rows, cols = w.shape

    def body(w_ref, g_ref, m_ref, v_ref, d_ref, nm_ref, nv_ref):
        d_ref[...], nm_ref[...], nv_ref[...] = _adam_math(w_ref[...], g_ref[...], m_ref[...], v_ref[...])

    blk = pl.BlockSpec((tr, cols), lambda i: (i, 0))
    shape = jax.ShapeDtypeStruct((rows, cols), F32)
    return _pallas(
        body, name="adamw_w_in", grid=(rows // tr,),
        in_specs=[blk] * 4, out_specs=[blk] * 3, out_shape=[shape] * 3,
        compiler_params=_params(("parallel",), VMEM_SMALL),
    )(w, g, m, v)


def _adamw_rest(bigs, smalls, g_small):
    nb, ns = len(bigs), len(smalls)

    def body(*refs):
        ins, outs = refs[:4 * nb + 3 * ns + 1], refs[4 * nb + 3 * ns + 1:]
        for i in range(nb):
            w_ref, g_ref, m_ref, v_ref = ins[4 * i:4 * i + 4]
            d_ref, nm_ref, nv_ref = outs[3 * i:3 * i + 3]
            d_ref[...], nm_ref[...], nv_ref[...] = _adam_math(w_ref[...], g_ref[...], m_ref[...], v_ref[...])
        gs_ref = ins[-1]
        for i in range(ns):
            w_ref, m_ref, v_ref = ins[4 * nb + 3 * i:4 * nb + 3 * i + 3]
            g_ref, d_ref, nm_ref, nv_ref = outs[3 * nb + 4 * i:3 * nb + 4 * i + 4]
            g = gs_ref[i:i + 1, 0:w_ref.shape[1]]
            g_ref[...] = g
            d_ref[...], nm_ref[...], nv_ref[...] = _adam_math(w_ref[...], g, m_ref[...], v_ref[...])

    flat_in = [a for quad in bigs for a in quad] + [a for tri in smalls for a in tri] + [g_small]
    out_shape = ([jax.ShapeDtypeStruct(q[0].shape, F32) for q in bigs for _ in range(3)]
                 + [jax.ShapeDtypeStruct(t[0].shape, F32) for t in smalls for _ in range(4)])
    return _pallas(
        body, name="adamw_rest",
        in_specs=[pl.BlockSpec(memory_space=pltpu.VMEM)] * len(flat_in),
        out_specs=[pl.BlockSpec(memory_space=pltpu.VMEM)] * len(out_shape),
        out_shape=out_shape,
        compiler_params=_params(vmem=VMEM_SMALL),
    )(*flat_in)


def _place(pieces, n):
    out, at = [], 0
    for lane0, arr in sorted(pieces, key=lambda p: p[0]):
        out += [jnp.zeros((n, lane0 - at), F32), arr]
        at = lane0 + arr.shape[1]
    return jnp.concatenate(out + [jnp.zeros((n, LANE - at), F32)], axis=1)


def _rope_tables(s_len, tm):
    rows = s_len // GRID_W
    row = jnp.arange(rows, dtype=F32)
    col = jnp.arange(GRID_W, dtype=F32)

    def lay(dim, layout, first_dim, ones):
        half = dim // 2
        inv = 1.0 / (ROPE_THETA ** (jnp.arange(0, half, 2, dtype=F32) / half))
        ang_r, ang_c = row[:, None] * inv[None, :], col[:, None] * inv[None, :]
        at = {a - first_dim: lane0 for a, _, lane0 in layout}
        q = dim // 4
        r1, r2, c1, c2 = at[0], at[q], at[2 * q], at[3 * q]
        cos_r = _place([(r1, jnp.cos(ang_r)), (r2, jnp.cos(ang_r))], rows)
        sin_r = _place([(r1, -jnp.sin(ang_r)), (r2, jnp.sin(ang_r))], rows)
        cos_c = _place([(c1, jnp.cos(ang_c)), (c2, jnp.cos(ang_c))] + [(l0, jnp.ones((GRID_W, n), F32)) for _, n, l0 in ones],
                       GRID_W)
        sin_c = _place([(c1, -jnp.sin(ang_c)), (c2, jnp.sin(ang_c))], GRID_W)
        by_block = (s_len // tm, tm // GRID_W, LANE)
        return cos_r.reshape(by_block), cos_c, sin_r.reshape(by_block), sin_c

    return lay(A_DIM, LAY_ROPE_A, 0, ()) + lay(B_ROPE, LAY_KR, 0, LAY_NOPE)


def _spread(w, n_heads, dim, axis, layout):
    w3 = w.reshape(w.shape[:axis] + (n_heads, dim) + w.shape[axis + 1:])
    out, at = [], 0

    def zeros(n):
        return jnp.zeros(w3.shape[:axis + 1] + (n,) + w3.shape[axis + 2:], w.dtype)

    for a0, n, lane0 in sorted(layout, key=lambda seg: seg[2]):
        out += [zeros(lane0 - at), lax.slice_in_dim(w3, a0, a0 + n, axis=axis + 1)]
        at = lane0 + n
    out = jnp.concatenate(out + [zeros(LANE - at)], axis=axis + 1)
    return out.reshape(w.shape[:axis] + (n_heads * LANE,) + w.shape[axis + 1:])


def _head_cols(first, n_heads, dim, layout):
    out = np.full((n_heads * LANE,), -1, np.int32)
    for h in range(n_heads):
        for a0, n, lane0 in layout:
            out[h * LANE + lane0:h * LANE + lane0 + n] = first + h * dim + a0 + np.arange(n)
    return out


def _inverse(src, n):
    dst = np.full((n,), -1, np.int32)
    dst[src[src >= 0]] = np.nonzero(src >= 0)[0]
    return dst


def _column_maps():
    a_w, kv_w = A_HEADS * A_DIM, A_KV * A_DIM
    o_g = a_w + 2 * kv_w
    o_cq = o_g + a_w
    o_kr = o_cq + B_Q_RANK + B_KV_RANK
    src_in = np.concatenate([
        _head_cols(0, A_HEADS, A_DIM, LAY_ROPE_A), _head_cols(a_w, A_KV, A_DIM, LAY_ROPE_A),
        _head_cols(a_w + kv_w, A_KV, A_DIM, LAY_PLAIN_A), np.arange(o_g, o_g + a_w),
        np.arange(o_kr + B_ROPE, N_IN), np.arange(o_cq, o_kr), _head_cols(o_kr, 1, B_ROPE, LAY_KR)]).astype(np.int32)
    src_uq = _head_cols(0, B_HEADS, B_QK, LAY_ROPE_B)
    per = B_NOPE + B_V
    src_ukv = np.concatenate([_head_cols(0, B_HEADS, per, LAY_NOPE),
                              _head_cols(B_NOPE, B_HEADS, per, ((0, B_V, 0),))]).astype(np.int32)
    assert len(src_in) == N_EXT
    return src_in, src_uq, src_ukv


def _round_up(n, m):
    return (n + m - 1) // m * m


PERMUTE_STEPS = 4


def _permute_cols(xs, maps, stacks, name):
    maps = [np.asarray(m, np.int32) for m in maps]
    n_arr = len(xs)

    def block(ref, b):
        if len(ref.shape) == 2:
            return ref.at[:, b * LANE:(b + 1) * LANE]
        per = ref.shape[2] // LANE
        return ref.at[b // per, :, (b % per) * LANE:(b % per + 1) * LANE]

    def body(*refs):
        row = lax.broadcasted_iota(jnp.int32, (LANE, LANE), 0)
        for x_ref, src_ref, o_ref, src in zip(refs[:n_arr], refs[n_arr:2 * n_arr], refs[2 * n_arr:], maps):
            for c in range(len(src) // LANE):
                want = src[c * LANE:(c + 1) * LANE]
                if want[0] >= 0 and want[0] % LANE == 0 and np.array_equal(want, want[0] + np.arange(LANE)):
                    block(o_ref, c)[...] = block(x_ref, int(want[0]) // LANE)[...]
                    continue
                acc = jnp.zeros((x_ref.shape[-2], LANE), F32)
                for kb in sorted({int(v) // LANE for v in want if v >= 0}):
                    sel = jnp.where(row + kb * LANE == src_ref[:, c * LANE:(c + 1) * LANE], 1.0, 0.0).astype(BF16)
                    part = block(x_ref, kb)[...]
                    if part.dtype == BF16:
                        acc = acc + _nn(part, sel)
                    else:
                        hi = part.astype(BF16)
                        rest = part - hi.astype(F32)
                        mid = rest.astype(BF16)
                        low = (rest - mid.astype(F32)).astype(BF16)
                        acc = acc + ((_nn(hi, sel) + _nn(mid, sel)) + _nn(low, sel))
                block(o_ref, c)[...] = acc.astype(o_ref.dtype)

    def out_shape(x, m, stack):
        rows = x.shape[-2]
        return (rows, len(m)) if stack is None else (stack, rows, len(m) // stack)

    def rows_spec(shape):
        if len(shape) == 2:
            return pl.BlockSpec((shape[0] // PERMUTE_STEPS, shape[1]), lambda i: (i, 0))
        return pl.BlockSpec((shape[0], shape[1] // PERMUTE_STEPS, shape[2]), lambda i: (0, i, 0))

    shapes = [out_shape(x, m, st) for x, m, st in zip(xs, maps, stacks)]
    return _pallas(
        body, name=name, grid=(PERMUTE_STEPS,),
        in_specs=[rows_spec(x.shape) for x in xs] + [_full((1, len(m))) for m in maps],
        out_specs=[rows_spec(sh) for sh in shapes],
        out_shape=[jax.ShapeDtypeStruct(sh, x.dtype) for sh, x in zip(shapes, xs)],
        compiler_params=_params(("parallel",), VMEM_MID),
    )(*xs, *[jnp.asarray(m).reshape(1, -1) for m in maps])


def _pad_cols(w):
    return jnp.pad(w, ((0, 0), (0, _round_up(w.shape[1], LANE) - w.shape[1])))


def _in_stack(cols, width):
    cols = np.asarray(cols)
    return np.where(cols < 0, -1, cols // width * _round_up(width, LANE) + cols % width).astype(np.int32)


def _ext_weights(g_in, g_uq, g_ukv, g_out):
    src_in, src_uq, src_ukv = _column_maps()
    w_uq = g_uq.reshape(B_Q_RANK, B_HEADS * B_QK)
    w_out = g_out.reshape(D_MODEL, D_MODEL)
    w_in_ext, w_uq_pad, w_ukv_ext = _permute_cols(
        [g_in, w_uq, g_ukv], [_in_stack(src_in, SH_IN[1]), src_uq, _in_stack(src_ukv, SH_UKV[1])], [None] * 3, "lay_out_weights")
    return w_in_ext, w_uq_pad, w_ukv_ext, w_out


def _fold_grads(d_in_ext, d_uq_pad, d_ukv_ext, d_out):
    src_in, src_uq, src_ukv = _column_maps()

    def back(src, n, width):
        inv = _inverse(src, n)
        wide = _round_up(width, LANE)
        out = np.full((n // width * wide,), -1, np.int32)
        for j in range(n // width):
            out[j * wide:j * wide + width] = inv[j * width:(j + 1) * width]
        return out

    n_uq, n_ukv = B_HEADS * B_QK, B_HEADS * (B_NOPE + B_V)
    d_in, d_uq, d_ukv = _permute_cols(
        [d_in_ext, d_uq_pad, d_ukv_ext], [back(src_in, N_IN, SH_IN[1]), _inverse(src_uq, n_uq), back(src_ukv, n_ukv, SH_UKV[1])],
        [N_CHIPS, None, N_CHIPS], "fold_grads")
    return d_in, d_uq.reshape((N_CHIPS,) + SH_UQ), d_ukv, d_out.reshape((N_CHIPS,) + SH_OUT)


def kernel(x, norm_in, w_in, a_q_norm, a_k_norm, b_cq_norm, b_ckv_norm, w_uq, w_ukv, b_q_norm, b_k_norm, w_out, loss_target, m_norm_in, m_w_in, m_a_q_norm, m_a_k_norm, m_b_cq_norm, m_b_ckv_norm, m_w_uq, m_w_ukv, m_b_q_norm, m_b_k_norm, m_w_out, v_norm_in, v_w_in, v_a_q_norm, v_a_k_norm, v_b_cq_norm, v_b_ckv_norm, v_w_uq, v_w_ukv, v_b_q_norm, v_b_k_norm, v_w_out):
    s_len = x.shape[1]
    xs, ts = x[0], loss_target[0]
    tm = min(256, s_len)
    tq, tk_f = min(512, s_len // 2), min(2048, s_len // 2)
    tq_b, tk_b = min(1024, s_len // 2), min(512, s_len)
    tiles_f = min(4, s_len // tq)
    tiles_b = min(2, s_len // tk_b)

    w_in_ext, w_uq_pad, w_ukv_ext, w_out_full = _ext_weights(
        *_gather_weights((_pad_cols(w_in[0]), w_uq[0], _pad_cols(w_ukv[0]), w_out[0])))
    gains = (norm_in, _spread(a_q_norm, 1, A_DIM, 1, LAY_ROPE_A), _spread(a_k_norm, 1, A_DIM, 1, LAY_ROPE_A), b_cq_norm, b_ckv_norm,
             _spread(b_q_norm, 1, B_QK, 1, LAY_ROPE_B), _spread(b_k_norm, 1, B_QK, 1, LAY_ROPE_B))
    tabs = _rope_tables(s_len, tm)

    (xn_t, gates, pre, qbpre, kbpre, cq_t, ckv_t, qa, ka, va, qb, kb, vb) = _pre(
        xs, tabs, w_in_ext, w_uq_pad, w_ukv_ext, gains, tm)
    o_a, lse_a = _attn_fwd(qa, ka, va, A_GROUP, A_DIM, tq, tk_f, tiles_f, "attn_fwd_a")
    o_b, lse_b = _attn_fwd(qb, kb, vb, 1, B_V, tq, tk_f, tiles_f, "attn_fwd_b")
    y_t, dh, dgate, do_a, do_b, delta, loss_part = _mid(xs, ts, o_a, o_b, gates, w_out_full, min(512, s_len))

    def stat(a):
        return a.reshape(a.shape[0], s_len // tq_b, 1, tq_b)

    dqa, dka, dva = _attn_bwd(qa, ka, va, do_a, stat(lse_a), stat(delta[:A_HEADS]), A_GROUP, tq_b, tk_b, tiles_b, "attn_bwd_a")
    dqb, dkb, dvb = _attn_bwd(qb, kb, vb, do_b, stat(lse_b), stat(delta[A_HEADS:A_HEADS + B_HEADS]), 1, tq_b, tk_b,
                              tiles_b, "attn_bwd_b")
    tabs_bwd = tuple(t * f for t4, fs in ((tabs[:4], (SCALE_A, LN2)), (tabs[4:], (SCALE_B, LN2))) for f in fs for t in t4)
    grad_x, dproj, dqbpre, dkvb, d_small = _post(
        xs, dh, pre, qbpre, kbpre, dgate, dqa, dka, dva, dqb, dkb, dvb, loss_part, tabs_bwd,
        w_in_ext, w_uq_pad, w_ukv_ext, gains, tm)

    ts_w = min(2048, s_len)
    d_in_ext = _grad_w(xn_t, dproj, N_EXT // 2, ts_w, "grad_w_in")
    d_out_full = _grad_w(y_t, dh, D_MODEL, ts_w, "grad_w_out")
    d_uq_pad, d_ukv_ext = _grad_w_pairs([(cq_t, dqbpre), (ckv_t, dkvb)], ts_w, "grad_w_mla")

    g_in_p, g_uq, g_ukv_p, g_out, g_small = _reduce_grads(_fold_grads(d_in_ext, d_uq_pad, d_ukv_ext, d_out_full), d_small)
    g_in, g_ukv = g_in_p[:, :SH_IN[1]], g_ukv_p[:, :SH_UKV[1]]
    d_in, nm_in, nv_in = (a.T for a in _adamw_rows(w_in[0].T, g_in_p.T[:SH_IN[1]], m_w_in[0].T, v_w_in[0].T, SH_IN[1] // 7))
    rest = _adamw_rest(
        [(w_uq[0], g_uq, m_w_uq[0], v_w_uq[0]), (w_ukv[0], g_ukv, m_w_ukv[0], v_w_ukv[0]),
         (w_out[0], g_out, m_w_out[0], v_w_out[0])],
        [(norm_in, m_norm_in, v_norm_in), (a_q_norm, m_a_q_norm, v_a_q_norm), (a_k_norm, m_a_k_norm, v_a_k_norm),
         (b_cq_norm, m_b_cq_norm, v_b_cq_norm), (b_ckv_norm, m_b_ckv_norm, v_b_ckv_norm),
         (b_q_norm, m_b_q_norm, v_b_q_norm), (b_k_norm, m_b_k_norm, v_b_k_norm)], g_small)
    (d_uq, nm_uq, nv_uq), (d_ukv, nm_ukv, nv_ukv), (d_out, nm_out, nv_out) = (rest[3 * i:3 * i + 3] for i in range(3))
    sm = [rest[9 + 4 * i:9 + 4 * i + 4] for i in range(7)]

    def leaves(k, p_in, p_uq, p_ukv, p_out):
        return [sm[SM_IN][k], p_in[None], sm[SM_AQ][k], sm[SM_AK][k], sm[SM_CQ][k], sm[SM_CKV][k], p_uq[None], p_ukv[None],
                sm[SM_BQ][k], sm[SM_BK][k], p_out[None]]

    return (g_small[SM_LOSS, 0], grad_x[None], *leaves(0, g_in, g_uq, g_ukv, g_out), *leaves(1, d_in, d_uq, d_ukv, d_out),
            *leaves(2, nm_in, nm_uq, nm_ukv, nm_out), *leaves(3, nv_in, nv_uq, nv_ukv, nv_out))
```

```python
import jax
import jax.numpy as jnp
import numpy as np
from jax import lax
from jax.experimental import pallas as pl
from jax.experimental.pallas import tpu as pltpu

F32 = jnp.float32
BF16 = jnp.bfloat16
MESH = pl.DeviceIdType.MESH

D_MODEL = 1024
GRID_W = 64
ROPE_THETA = 10000.0
EPS = 1e-6
A_HEADS, A_KV, A_DIM = 8, 2, 64
A_GROUP = A_HEADS // A_KV
B_HEADS, B_NOPE, B_ROPE, B_V = 4, 64, 32, 128
B_QK = B_NOPE + B_ROPE
B_Q_RANK, B_KV_RANK = 384, 256
N_IN = 2464
SCALE_A = 1.0 / float(np.sqrt(A_DIM))
SCALE_B = 1.0 / float(np.sqrt(B_QK))
LOG2E = float(np.log2(np.e))
LN2 = float(np.log(2.0))
ADAM_LR, ADAM_B1, ADAM_B2, ADAM_EPS, ADAM_WD, ADAM_STEP = 0.001, 0.9, 0.999, 1e-08, 0.01, 10

LANE = 128
VMEM_BYTES = 64 * 1024 * 1024
VMEM_LIMIT = VMEM_BYTES - 8 * 1024 * 1024
VMEM_MID = 48 * 1024 * 1024
VMEM_SMALL = 32 * 1024 * 1024

QA0 = 0
KA0 = QA0 + A_HEADS * LANE
VA0 = KA0 + A_KV * LANE
GA0 = VA0 + A_KV * LANE
GB0 = GA0 + A_HEADS * A_DIM
CQ0 = GB0 + B_HEADS * LANE
CKV0 = CQ0 + B_Q_RANK
KR0 = CKV0 + B_KV_RANK
N_EXT = KR0 + LANE
N_GATE = (A_HEADS + B_HEADS) * LANE
N_GATE_C = A_HEADS * A_DIM + B_HEADS * LANE
DELTA_ROWS = 16
N_PRE = KA0 + A_KV * LANE + B_Q_RANK + B_KV_RANK

ROT = LANE // 2
_QA = A_DIM // 4
_QB = B_ROPE // 4
LAY_PLAIN_A = ((0, A_DIM, 0),)
LAY_ROPE_A = ((0, _QA, 0), (2 * _QA, _QA, _QA), (_QA, _QA, ROT), (3 * _QA, _QA, ROT + _QA))
LAY_KR = ((0, _QB, 0), (2 * _QB, _QB, _QB), (_QB, _QB, ROT), (3 * _QB, _QB, ROT + _QB))
LAY_NOPE = ((0, B_NOPE // 2, 2 * _QB), (B_NOPE // 2, B_NOPE // 2, ROT + 2 * _QB))
LAY_ROPE_B = LAY_NOPE + tuple((B_NOPE + a, n, at) for a, n, at in LAY_KR)

N_CHIPS = 4
SH_IN = (D_MODEL, N_IN // N_CHIPS)
SH_UQ = (B_Q_RANK // N_CHIPS, B_HEADS * B_QK)
SH_UKV = (B_KV_RANK, B_HEADS * (B_NOPE + B_V) // N_CHIPS)
SH_OUT = (D_MODEL // N_CHIPS, D_MODEL)
SM_ROWS, SM_W = 16, D_MODEL
SM_IN, SM_AQ, SM_AK, SM_CQ, SM_CKV, SM_BQ, SM_BK, SM_LOSS = range(8)
F32_ROWS, BF16_ROWS = 8, 16


def _pallas(body, **kw):
    return pl.pallas_call(body, **kw)


def _params(sem=None, vmem=None):
    return pltpu.CompilerParams(dimension_semantics=sem, vmem_limit_bytes=vmem)


def _rms_fwd(x, g, n):
    r = lax.rsqrt(jnp.sum(x * x, axis=-1, keepdims=True) * (1.0 / n) + EPS)
    return x * r * g


def _rms_bwd(dy, x, g, n):
    u = dy * g
    r = lax.rsqrt(jnp.sum(x * x, axis=-1, keepdims=True) * (1.0 / n) + EPS)
    ux = jnp.sum(u * x, axis=-1, keepdims=True)
    xhat = x * r
    dx = r * (u - xhat * (r * ux * (1.0 / n)))
    return dx, dy * xhat


def _rope_fwd(y, cos, sin):
    return y * cos + pltpu.roll(y, ROT, 1) * sin


def _rope_bwd(d, cos, sin):
    return d * cos - pltpu.roll(d, ROT, 1) * sin


def _token_tables(refs):
    out = []
    for r_ref, c_ref in zip(refs[0::2], refs[1::2]):
        r, c = r_ref[...], c_ref[...]
        out.append(jnp.concatenate([r[k:k + 1, :] + c for k in range(r.shape[0])], axis=0))
    return out


def _lanes_of(lane, layout):
    m = None
    for _, n, at in layout:
        seg = (lane >= at) & (lane < at + n)
        m = seg if m is None else (m | seg)
    return m


def _unspread_row(v, layout):
    v8 = jnp.broadcast_to(v, (F32_ROWS, LANE))
    lane = lax.broadcasted_iota(jnp.int32, v8.shape, 1)
    out = jnp.zeros_like(v8)
    for a, n, at in layout:
        moved = v8 if a == at else pltpu.roll(v8, (a - at) % LANE, 1)
        out = jnp.where((lane >= a) & (lane < a + n), moved, out)
    return out[0:1, :]


def _nt(a, b):
    return lax.dot_general(a, b, (((1,), (1,)), ((), ())), preferred_element_type=F32)


def _tn(a, b):
    return lax.dot_general(a, b, (((0,), (0,)), ((), ())), preferred_element_type=F32)


def _nn(a, b):
    return jnp.dot(a, b, preferred_element_type=F32)


def _block_rows(i, size):
    if isinstance(i, int):
        return pl.ds(i * size, size)
    return pl.ds(pl.multiple_of(i * size, size), size)


MAX_STATIC_BLOCKS = 32


def _three_stage(n, first, second, third):
    assert n >= 2 and n % 2 == 0
    first(0, 0)
    first(1, 1)
    second(0, 0)
    if n <= MAX_STATIC_BLOCKS:
        for i in range(1, n - 1):
            first(i + 1, (i + 1) % 2)
            second(i, i % 2)
            third(i - 1, (i - 1) % 2)
    else:
        def pair(t, carry):
            i = 2 * t + 1
            first(i + 1, 0)
            second(i, 1)
            third(i - 1, 0)
            first(i + 2, 1)
            second(i + 1, 0)
            third(i, 1)
            return carry

        lax.fori_loop(0, (n - 2) // 2, pair, 0)
    second(n - 1, 1)
    third(n - 2, 0)
    third(n - 1, 1)


def _full(shape):
    return pl.BlockSpec(shape, lambda *_: (0,) * len(shape))


def _table_specs(tm, n_tables=4):
    return [pl.BlockSpec((None, tm // GRID_W, LANE), lambda i: (i, 0, 0)), _full((GRID_W, LANE))] * n_tables


def _resident(shape):
    return pl.BlockSpec(shape, lambda *_: (0,) * len(shape), pipeline_mode=pl.Buffered(1))


def _gather_weights(shards):
    n = len(shards)
    halves = [w.shape[0] // 2 for w in shards]

    def body(*refs):
        w_refs, out_refs, (send_sems, recv_sems) = refs[:n], refs[n:2 * n], refs[2 * n:]
        x, y, c = lax.axis_index("x"), lax.axis_index("y"), lax.axis_index("c")
        sibling = (x, y, 1 - c)
        chips = [(1 - x, y), (x, 1 - y), (1 - x, 1 - y)]
        me = 2 * x + y

        def copy(a, k, j, hc, to):
            part = out_refs[a].at[j, pl.ds(pl.multiple_of(hc * halves[a], BF16_ROWS), halves[a]), :]
            return pltpu.make_async_remote_copy(
                src_ref=part, dst_ref=part, send_sem=send_sems.at[6 * a + k], recv_sem=recv_sems.at[6 * a + k],
                device_id=to, device_id_type=MESH)

        started = []
        for a in range(n):
            out_refs[a][me] = w_refs[a][...].astype(BF16)
            for k, chip in enumerate(chips):
                started.append(copy(a, k, me, c, (*chip, c)))
                started[-1].start()
        for k, chip in enumerate(chips):
            for a in range(n):
                copy(a, k, 2 * chip[0] + chip[1], c, (*chip, c)).wait_recv()
                started.append(copy(a, 3 + k, 2 * chip[0] + chip[1], c, sibling))
                started[-1].start()
        for k, chip in enumerate(chips):
            for a in range(n):
                copy(a, 3 + k, 2 * chip[0] + chip[1], 1 - c, sibling).wait_recv()
        for cp in started:
            cp.wait_send()

    return _pallas(
        body, name="gather_weights",
        out_shape=[jax.ShapeDtypeStruct((N_CHIPS,) + w.shape, BF16) for w in shards],
        in_specs=[pl.BlockSpec(memory_space=pltpu.VMEM)] * n,
        out_specs=[pl.BlockSpec(memory_space=pltpu.VMEM)] * n,
        scratch_shapes=[pltpu.SemaphoreType.DMA((6 * n,)), pltpu.SemaphoreType.DMA((6 * n,))],
        compiler_params=_params(vmem=VMEM_SMALL),
    )(*shards)


def _reduce_grads(parts, small):
    n_big = len(parts)
    n = n_big + 1
    shapes = [p.shape[1:] for p in parts] + [small.shape]
    halves = [sh[0] // 2 for sh in shapes]

    def body(*refs):
        p_refs, out_refs, rec_a, rec_b = refs[:n], refs[n:2 * n], refs[2 * n:3 * n], refs[3 * n:4 * n]
        send_b = refs[4 * n:4 * n + n_big]
        sa_send, sa_recv, sb_send, sb_recv, sc_send, sc_recv = refs[4 * n + n_big:]
        x, y, c = lax.axis_index("x"), lax.axis_index("y"), lax.axis_index("c")
        sibling = (x, y, 1 - c)
        me = 2 * x + y

        def rows(a, hc):
            return pl.ds(pl.multiple_of(hc * halves[a], F32_ROWS), halves[a])

        def partial(a, j, hc):
            return p_refs[a].at[j, rows(a, hc), :] if a < n_big else p_refs[a].at[rows(a, hc), :]

        def copy_a(a, j):
            return pltpu.make_async_remote_copy(
                src_ref=partial(a, j, 1 - c), dst_ref=rec_a[a].at[j],
                send_sem=sa_send.at[N_CHIPS * a + j], recv_sem=sa_recv.at[N_CHIPS * a + j],
                device_id=sibling, device_id_type=MESH)

        def copy_b(a, r):
            j = me ^ r
            k = (N_CHIPS - 1) * a + r - 1
            return pltpu.make_async_remote_copy(
                src_ref=(send_b[a] if a < n_big else rec_a[a]).at[j], dst_ref=rec_b[a].at[r],
                send_sem=sb_send.at[k], recv_sem=sb_recv.at[k], device_id=(j // 2, j % 2, c), device_id_type=MESH)

        def copy_c(a):
            return pltpu.make_async_remote_copy(
                src_ref=out_refs[a].at[rows(a, c), :], dst_ref=out_refs[a].at[rows(a, c), :],
                send_sem=sc_send.at[a], recv_sem=sc_recv.at[a], device_id=sibling, device_id_type=MESH)

        for a in range(n):
            for j in range(N_CHIPS):
                copy_a(a, j).start()
        for r in range(1, N_CHIPS):
            j = me ^ r
            for a in range(n):
                copy_a(a, j).wait_recv()
                chip_part = rec_a[a][j] + partial(a, j, c)[...]
                if a < n_big:
                    send_b[a][j] = chip_part.astype(BF16)
                else:
                    rec_a[a][j] = chip_part
                copy_b(a, r).start()
        for a in range(n):
            copy_a(a, me).wait_recv()
            rec_b[a][0] = (rec_a[a][me] + partial(a, me, c)[...]).astype(rec_b[a].dtype)
        for a in range(n):
            for r in range(1, N_CHIPS):
                copy_b(a, r).wait_recv()
            total = rec_b[a][me].astype(F32)
            for j in range(1, N_CHIPS):
                total = total + rec_b[a][j ^ me].astype(F32)
            out_refs[a][rows(a, c), :] = total
            copy_c(a).start()
        for a in range(n):
            copy_c(a).wait_recv()
        for a in range(n):
            for j in range(N_CHIPS):
                copy_a(a, j).wait_send()
            for r in range(1, N_CHIPS):
                copy_b(a, r).wait_send()
            copy_c(a).wait_send()

    dma = pltpu.SemaphoreType.DMA
    return _pallas(
        body, name="reduce_grads",
        out_shape=[jax.ShapeDtypeStruct(sh, F32) for sh in shapes],
        in_specs=[pl.BlockSpec(memory_space=pltpu.VMEM)] * n,
        out_specs=[pl.BlockSpec(memory_space=pltpu.VMEM)] * n,
        scratch_shapes=[pltpu.VMEM((N_CHIPS, h) + sh[1:], F32) for h, sh in zip(halves, shapes)]
                       + [pltpu.VMEM((N_CHIPS, h) + sh[1:], BF16 if a < n_big else F32)
                          for a, (h, sh) in enumerate(zip(halves, shapes))]
                       + [pltpu.VMEM((N_CHIPS, h) + sh[1:], BF16) for h, sh in zip(halves[:n_big], shapes[:n_big])]
                       + [dma((N_CHIPS * n,)), dma((N_CHIPS * n,)), dma(((N_CHIPS - 1) * n,)), dma(((N_CHIPS - 1) * n,)),
                          dma((n,)), dma((n,))],
        compiler_params=_params(vmem=VMEM_LIMIT),
    )(*parts, small)


def _pre(x, tabs, w_in_ext, w_uq_pad, w_ukv_ext, gains, tm):
    s_len = x.shape[0]
    nt = s_len // tm

    def body(x_ref, car_ref, cac_ref, sar_ref, sac_ref, cbr_ref, cbc_ref, sbr_ref, sbc_ref, win_ref, wuq_ref, wukv_ref,
             gin_ref, gaq_ref, gak_ref, gcq_ref, gckv_ref, gbq_ref, gbk_ref,
             xn_ref, gates_ref, pre_ref, qbpre_ref, kbpre_ref, cq_ref, ckv_ref,
             qa_ref, ka_ref, va_ref, qb_ref, kb_ref, vb_ref, proj):
        xn = _rms_fwd(x_ref[...], gin_ref[...], D_MODEL)
        xn_ref[...] = jnp.transpose(xn).astype(BF16)
        xb = xn.astype(BF16)
        pre_ref[:, 0:VA0] = _nn(xb, win_ref[:, 0:VA0])
        gates_ref[...] = _nn(xb, win_ref[:, GA0:GA0 + N_GATE_C])
        pre_ref[:, VA0:N_PRE] = _nn(xb, win_ref[:, CQ0:KR0])
        proj[...] = _nn(xb, win_ref[:, VA0:GA0])
        kr = _nn(xb, win_ref[:, KR0:N_EXT])
        ca, sa, cb, sb = _token_tables((car_ref, cac_ref, sar_ref, sac_ref, cbr_ref, cbc_ref, sbr_ref, sbc_ref))
        lane = lax.broadcasted_iota(jnp.int32, (tm, LANE), 1)
        for h in range(A_HEADS):
            yq = _rms_fwd(pre_ref[:, QA0 + LANE * h:QA0 + LANE * (h + 1)], gaq_ref[...], A_DIM)
            qa_ref[h] = (_rope_fwd(yq, ca, sa) * (SCALE_A * LOG2E)).astype(BF16)
        for h in range(A_KV):
            yk = _rms_fwd(pre_ref[:, KA0 + LANE * h:KA0 + LANE * (h + 1)], gak_ref[...], A_DIM)
            ka_ref[h] = _rope_fwd(yk, ca, sa).astype(BF16)
            va_ref[h] = jnp.where(lane == A_DIM, 1.0, proj[:, LANE * h:LANE * (h + 1)]).astype(BF16)
        cq = _rms_fwd(pre_ref[:, VA0:VA0 + B_Q_RANK], gcq_ref[...], B_Q_RANK)
        cq_ref[...] = jnp.transpose(cq).astype(BF16)
        qbpre_ref[...] = _nn(cq.astype(BF16), wuq_ref[...])
        ckv = _rms_fwd(pre_ref[:, VA0 + B_Q_RANK:N_PRE], gckv_ref[...], B_KV_RANK)
        ckv_ref[...] = jnp.transpose(ckv).astype(BF16)
        kvb = _nn(ckv.astype(BF16), wukv_ref[...])
        for h in range(B_HEADS):
            yq = _rms_fwd(qbpre_ref[:, LANE * h:LANE * (h + 1)], gbq_ref[...], B_QK)
            qb_ref[h] = (_rope_fwd(yq, cb, sb) * (SCALE_B * LOG2E)).astype(BF16)
            kp = kvb[:, LANE * h:LANE * (h + 1)] + kr
            kbpre_ref[:, LANE * h:LANE * (h + 1)] = kp
            kb_ref[h] = _rope_fwd(_rms_fwd(kp, gbk_ref[...], B_QK), cb, sb).astype(BF16)
            vb_ref[h, :, 0:LANE] = kvb[:, B_HEADS * LANE + LANE * h:B_HEADS * LANE + LANE * (h + 1)].astype(BF16)
            vb_ref[h, :, LANE:2 * LANE] = jnp.where(lane == 0, 1.0, 0.0).astype(BF16)

    row = lambda w: pl.BlockSpec((tm, w), lambda i: (i, 0))
    col = lambda w: pl.BlockSpec((w, tm), lambda i: (0, i))
    heads = lambda n: pl.BlockSpec((n, tm, LANE), lambda i: (0, i, 0))
    hs = lambda n: jax.ShapeDtypeStruct((n, s_len, LANE), BF16)
    return _pallas(
        body, name="pre", grid=(nt,),
        in_specs=[row(D_MODEL)] + _table_specs(tm)
                 + [_resident(w_in_ext.shape), _resident(w_uq_pad.shape), _resident(w_ukv_ext.shape)]
                 + [_full(g.shape) for g in gains],
        out_specs=[col(D_MODEL), row(N_GATE_C), row(N_PRE), row(B_HEADS * LANE), row(B_HEADS * LANE),
                   col(B_Q_RANK), col(B_KV_RANK),
                   heads(A_HEADS), heads(A_KV), heads(A_KV), heads(B_HEADS), heads(B_HEADS),
                   pl.BlockSpec((B_HEADS, tm, 2 * LANE), lambda i: (0, i, 0))],
        out_shape=[jax.ShapeDtypeStruct((D_MODEL, s_len), BF16), jax.ShapeDtypeStruct((s_len, N_GATE_C), F32),
                   jax.ShapeDtypeStruct((s_len, N_PRE), F32), jax.ShapeDtypeStruct((s_len, B_HEADS * LANE), F32),
                   jax.ShapeDtypeStruct((s_len, B_HEADS * LANE), F32),
                   jax.ShapeDtypeStruct((B_Q_RANK, s_len), BF16), jax.ShapeDtypeStruct((B_KV_RANK, s_len), BF16),
                   hs(A_HEADS), hs(A_KV), hs(A_KV), hs(B_HEADS), hs(B_HEADS),
                   jax.ShapeDtypeStruct((B_HEADS, s_len, 2 * LANE), BF16)],
        scratch_shapes=[pltpu.VMEM((tm, A_KV * LANE), F32)],
        compiler_params=_params(("parallel",), VMEM_LIMIT),
    )(x, *tabs, w_in_ext, w_uq_pad, w_ukv_ext, *gains)


def _attn_fwd(q, k, v, group, l_col, tq, tk, tiles, name):
    n_heads, s_len, _ = q.shape
    v_w = v.shape[2]
    nk = s_len // tk

    def body(q_ref, k_ref, v_ref, o_ref, lse_ref, s_buf, p_buf, a_buf, m_ref, acc_ref):
        def scores(g, slot):
            s_buf[slot] = _nt(q_ref[_block_rows(g // nk, tq), :], k_ref[_block_rows(g % nk, tk), :])

        def softmax(g, slot):
            t = g // nk
            s = s_buf[slot]
            m_old = m_ref[t]
            m_new = jnp.maximum(m_old, jnp.max(s, axis=-1, keepdims=True))
            m_ref[t] = m_new
            a_buf[slot] = jnp.exp2(m_old - m_new)
            p_buf[slot] = jnp.exp2(s - jnp.tile(m_new, (1, tk // LANE))).astype(BF16)

        def values(g, slot):
            t = g // nk
            pv = _nn(p_buf[slot], v_ref[_block_rows(g % nk, tk), :])
            for c in range(0, v_w, LANE):
                acc_ref[t, :, c:c + LANE] = a_buf[slot] * acc_ref[t, :, c:c + LANE] + pv[:, c:c + LANE]

        m_ref[...] = jnp.full(m_ref.shape, -1e30, F32)
        acc_ref[...] = jnp.zeros(acc_ref.shape, F32)
        _three_stage(tiles * nk, scores, softmax, values)
        for t in range(tiles):
            l = acc_ref[t, :, l_col:l_col + 1]
            o = acc_ref[t, :, 0:LANE] * (1.0 / l)
            if l_col < LANE:
                lane = lax.broadcasted_iota(jnp.int32, o.shape, 1)
                o = jnp.where(lane == l_col, 0.0, o)
            o_ref[t * tq:(t + 1) * tq, :] = o
            lse_ref[t] = jnp.transpose(m_ref[t] + jnp.log2(jnp.broadcast_to(l, (tq, LANE))))[0:1, :]

    return _pallas(
        body, name=name, grid=(n_heads, s_len // (tiles * tq)),
        in_specs=[pl.BlockSpec((None, tiles * tq, LANE), lambda h, i: (h, i, 0)),
                  pl.BlockSpec((None, s_len, LANE), lambda h, i: (h // group, 0, 0)),
                  pl.BlockSpec((None, s_len, v_w), lambda h, i: (h // group, 0, 0))],
        out_specs=[pl.BlockSpec((None, tiles * tq, LANE), lambda h, i: (h, i, 0)),
                   pl.BlockSpec((None, tiles, 1, tq), lambda h, i: (h, i, 0, 0))],
        out_shape=[jax.ShapeDtypeStruct((n_heads, s_len, LANE), F32),
                   jax.ShapeDtypeStruct((n_heads, s_len // tq, 1, tq), F32)],
        scratch_shapes=[pltpu.VMEM((2, tq, tk), F32), pltpu.VMEM((2, tq, tk), BF16), pltpu.VMEM((2, tq, LANE), F32),
                        pltpu.VMEM((tiles, tq, LANE), F32), pltpu.VMEM((tiles, tq, v_w), F32)],
        compiler_params=_params(("parallel", "parallel"), VMEM_MID),
    )(q, k, v)


def _mid(x, target, o_a, o_b, gates, w_out, tm):
    s_len = x.shape[0]
    nt = s_len // tm
    n_heads = A_HEADS + B_HEADS
    d_mix = w_out.shape[0]
    pairs = A_HEADS // 2

    def body(x_ref, t_ref, oa_ref, ob_ref, g_ref, w_ref,
             yt_ref, dh_ref, dgate_ref, doa_ref, dob_ref, delta_ref, loss_ref, silu_scr, dsilu_scr, y_ref):
        @pl.when(pl.program_id(0) == 0)
        def _():
            loss_ref[...] = jnp.zeros_like(loss_ref)

        def o_of(h):
            return oa_ref[h] if h < A_HEADS else ob_ref[h - A_HEADS]

        lane = lax.broadcasted_iota(jnp.int32, (tm, LANE), 1)

        def gated(h):
            cols = slice(LANE * h, LANE * (h + 1))
            if h < A_HEADS:
                packed = g_ref[:, LANE * (h // 2):LANE * (h // 2 + 1)]
                g = jnp.where(lane < A_DIM, packed if h % 2 == 0 else pltpu.roll(packed, ROT, 1), 0.0)
            else:
                g = g_ref[:, LANE * (pairs + h - A_HEADS):LANE * (pairs + h - A_HEADS + 1)]
            sig = 1.0 / (1.0 + jnp.exp(-g))
            silu = g * sig
            silu_scr[:, cols] = silu
            dsilu_scr[:, cols] = sig * (1.0 + g * (1.0 - sig))
            return o_of(h) * silu

        for c in range(pairs + B_HEADS):
            y = gated(2 * c) + pltpu.roll(gated(2 * c + 1), ROT, 1) if c < pairs else gated(A_HEADS + c - pairs)
            cols = slice(LANE * c, LANE * (c + 1))
            y_ref[:, cols] = y.astype(BF16)
            yt_ref[cols, :] = jnp.transpose(y).astype(BF16)
        err = x_ref[...] + _nn(y_ref[...], w_ref[...]) - t_ref[...]
        sq = jnp.sum(jnp.sum(err * err, axis=-1, keepdims=True), axis=0, keepdims=True)
        loss_ref[...] += jnp.broadcast_to(sq * (0.5 / D_MODEL), loss_ref.shape)
        dh = err * (1.0 / D_MODEL)
        dh_ref[...] = dh
        dy = _nt(dh.astype(BF16), w_ref[...])
        delta = jnp.zeros((tm, LANE), F32)
        held = None
        for h in range(n_heads):
            cols = slice(LANE * h, LANE * (h + 1))
            if h < A_HEADS:
                packed = dy[:, LANE * (h // 2):LANE * (h // 2 + 1)]
                dyh = packed if h % 2 == 0 else pltpu.roll(packed, ROT, 1)
            else:
                dyh = dy[:, LANE * (pairs + h - A_HEADS):LANE * (pairs + h - A_HEADS + 1)]
            oh = o_of(h)
            do = dyh * silu_scr[:, cols]
            dg = dyh * oh * dsilu_scr[:, cols]
            if h >= A_HEADS:
                dgate_ref[:, LANE * (pairs + h - A_HEADS):LANE * (pairs + h - A_HEADS + 1)] = dg.astype(BF16)
            elif h % 2 == 0:
                held = dg
            else:
                dgate_ref[:, LANE * (h // 2):LANE * (h // 2 + 1)] = (held + pltpu.roll(dg, ROT, 1)).astype(BF16)
            delta = jnp.where(lane == h, jnp.sum(do * oh, axis=-1, keepdims=True), delta)
            if h < A_HEADS:
                doa_ref[h] = do.astype(BF16)
            else:
                dob_ref[h - A_HEADS] = do.astype(BF16)
        delta_ref[...] = jnp.transpose(delta)[0:DELTA_ROWS, :]

    row = lambda w: pl.BlockSpec((tm, w), lambda i: (i, 0))
    heads = lambda n, w=LANE: pl.BlockSpec((n, tm, w), lambda i: (0, i, 0))
    return _pallas(
        body, name="mid", grid=(nt,),
        in_specs=[row(D_MODEL), row(D_MODEL), heads(A_HEADS), heads(B_HEADS), row(N_GATE_C), _resident(w_out.shape)],
        out_specs=[pl.BlockSpec((d_mix, tm), lambda i: (0, i)), row(D_MODEL), row(N_GATE_C), heads(A_HEADS), heads(B_HEADS),
                   pl.BlockSpec((DELTA_ROWS, tm), lambda i: (0, i)),
                   _full((8, LANE))],
        out_shape=[jax.ShapeDtypeStruct((d_mix, s_len), BF16), jax.ShapeDtypeStruct((s_len, D_MODEL), F32),
                   jax.ShapeDtypeStruct((s_len, N_GATE_C), BF16),
                   jax.ShapeDtypeStruct((A_HEADS, s_len, LANE), BF16), jax.ShapeDtypeStruct((B_HEADS, s_len, LANE), BF16),
                   jax.ShapeDtypeStruct((DELTA_ROWS, s_len), F32), jax.ShapeDtypeStruct((8, LANE), F32)],
        scratch_shapes=[pltpu.VMEM((tm, N_GATE), F32), pltpu.VMEM((tm, N_GATE), F32), pltpu.VMEM((tm, d_mix), BF16)],
        compiler_params=_params(("arbitrary",), VMEM_LIMIT),
    )(x, target, o_a, o_b, gates, w_out)


def _attn_bwd(q, k, v, do, lse, delta, group, tq, tk, tiles, name):
    n_heads, s_len, _ = q.shape
    nq = s_len // tq

    def body(q_ref, do_ref, lse_ref, delta_ref, k_ref, v_ref, dq_ref, dk_ref, dv_ref, s_buf, dp_buf, p_buf, ds_buf):
        @pl.when(pl.program_id(1) == 0)
        def _():
            dq_ref[...] = jnp.zeros_like(dq_ref)

        assign_first = tiles * nq <= MAX_STATIC_BLOCKS
        if not assign_first:
            dk_ref[...] = jnp.zeros_like(dk_ref)
            dv_ref[...] = jnp.zeros_like(dv_ref)

        def keys(g):
            return _block_rows(g // nq, tk)

        def queries(g):
            return _block_rows(g % nq, tq)

        def scores(g, slot):
            s_buf[slot] = _nt(k_ref[keys(g), :], q_ref[queries(g), :])
            dp_buf[slot] = _nt(v_ref[keys(g), :], do_ref[queries(g), :])

        def elementwise(g, slot):
            p = jnp.exp2(s_buf[slot] - lse_ref[g % nq])
            p_buf[slot] = p.astype(BF16)
            ds_buf[slot] = (p * (dp_buf[slot] - delta_ref[g % nq])).astype(BF16)

        def grads(g, slot):
            dv = _nn(p_buf[slot], do_ref[queries(g), :])
            dk = _nn(ds_buf[slot], q_ref[queries(g), :])
            if assign_first and g % nq == 0:
                dv_ref[keys(g), :] = dv
                dk_ref[keys(g), :] = dk
            else:
                dv_ref[keys(g), :] += dv
                dk_ref[keys(g), :] += dk
            dq_ref[queries(g), :] += _tn(ds_buf[slot], k_ref[keys(g), :])

        _three_stage(tiles * nq, scores, elementwise, grads)

    whole = lambda: pl.BlockSpec((None, s_len, LANE), lambda h, j: (h, 0, 0))
    stat = lambda: pl.BlockSpec((None, nq, 1, tq), lambda h, j: (h, 0, 0, 0))
    kvb = lambda: pl.BlockSpec((None, tiles * tk, LANE), lambda h, j: (h // group, j, 0))
    outb = lambda: pl.BlockSpec((None, tiles * tk, LANE), lambda h, j: (h, j, 0))
    shape = jax.ShapeDtypeStruct((n_heads, s_len, LANE), F32)
    return _pallas(
        body, name=name, grid=(n_heads, s_len // (tiles * tk)),
        in_specs=[whole(), whole(), stat(), stat(), kvb(), kvb()],
        out_specs=[whole(), outb(), outb()],
        out_shape=[shape, shape, shape],
        scratch_shapes=[pltpu.VMEM((2, tk, tq), F32), pltpu.VMEM((2, tk, tq), F32),
                        pltpu.VMEM((2, tk, tq), BF16), pltpu.VMEM((2, tk, tq), BF16)],
        compiler_params=_params(("parallel", "arbitrary"), VMEM_MID),
    )(q, do, lse, delta, k, v)


def _post(x, dh, pre, qbpre, kbpre, dgate, dqa, dka, dva, dqb, dkb, dvb, loss_part, tabs,
          w_in_ext, w_uq_pad, w_ukv_ext, gains, tm):
    s_len = x.shape[0]
    nt = s_len // tm

    def body(x_ref, dh_ref, pre_ref, qbpre_ref, kbpre_ref, dgate_ref,
             dqa_ref, dka_ref, dva_ref, dqb_ref, dkb_ref, dvb_ref, loss_ref,
             t0, t1, t2, t3, t4, t5, t6, t7, t8, t9, t10, t11, t12, t13, t14, t15, win_ref, wuq_ref, wukv_ref,
             gin_ref, gaq_ref, gak_ref, gcq_ref, gckv_ref, gbq_ref, gbk_ref,
             gx_ref, dproj_ref, dqbpre_ref, dkvb_ref, dsm_ref):
        @pl.when(pl.program_id(0) == 0)
        def _():
            dsm_ref[...] = jnp.zeros_like(dsm_ref)
            dsm_ref[SM_LOSS:SM_LOSS + 1, 0:LANE] = loss_ref[0:1, :]

        def add_small(r, dg):
            dsm_ref[r:r + 1, 0:dg.shape[1]] += dg

        def tok_sum(a):
            return jnp.sum(a, axis=0, keepdims=True)

        caq, saq, cak, sak, cbq, sbq, cbk, sbk = _token_tables(
            (t0, t1, t2, t3, t4, t5, t6, t7, t8, t9, t10, t11, t12, t13, t14, t15))
        lane = lax.broadcasted_iota(jnp.int32, (tm, LANE), 1)

        nope_lanes = _lanes_of(lane, LAY_NOPE)

        def back(c0, c1):
            return _nt(dproj_ref[:, c0:c1], win_ref[:, c0:c1])

        dproj_ref[:, GA0:GA0 + N_GATE_C] = dgate_ref[...]
        dxn = back(GA0, GA0 + N_GATE_C)
        dg = jnp.zeros((1, LANE), F32)
        for h in range(A_HEADS):
            dn = _rope_bwd(dqa_ref[h], caq, saq)
            dx, dgr = _rms_bwd(dn, pre_ref[:, QA0 + LANE * h:QA0 + LANE * (h + 1)], gaq_ref[...], A_DIM)
            dproj_ref[:, QA0 + LANE * h:QA0 + LANE * (h + 1)] = dx.astype(BF16)
            dg = dg + tok_sum(dgr)
        add_small(SM_AQ, _unspread_row(dg, LAY_ROPE_A))
        dxn = dxn + back(QA0, KA0)
        dg = jnp.zeros((1, LANE), F32)
        for h in range(A_KV):
            dk = dka_ref[A_GROUP * h]
            dv = dva_ref[A_GROUP * h]
            for g in range(1, A_GROUP):
                dk = dk + dka_ref[A_GROUP * h + g]
                dv = dv + dva_ref[A_GROUP * h + g]
            dn = _rope_bwd(dk, cak, sak)
            dx, dgr = _rms_bwd(dn, pre_ref[:, KA0 + LANE * h:KA0 + LANE * (h + 1)], gak_ref[...], A_DIM)
            dproj_ref[:, KA0 + LANE * h:KA0 + LANE * (h + 1)] = dx.astype(BF16)
            dproj_ref[:, VA0 + LANE * h:VA0 + LANE * (h + 1)] = dv.astype(BF16)
            dg = dg + tok_sum(dgr)
        add_small(SM_AK, _unspread_row(dg, LAY_ROPE_A))
        dxn = dxn + back(KA0, GA0)
        dg = jnp.zeros((1, LANE), F32)
        for h in range(B_HEADS):
            cols = slice(LANE * h, LANE * (h + 1))
            dn = _rope_bwd(dqb_ref[h], cbq, sbq)
            dx, dgr = _rms_bwd(dn, qbpre_ref[:, cols], gbq_ref[...], B_QK)
            dqbpre_ref[:, cols] = dx.astype(BF16)
            dg = dg + tok_sum(dgr)
        add_small(SM_BQ, _unspread_row(dg, LAY_ROPE_B))
        dcq = _nt(dqbpre_ref[...], wuq_ref[...])
        dx, dgr = _rms_bwd(dcq, pre_ref[:, VA0:VA0 + B_Q_RANK], gcq_ref[...], B_Q_RANK)
        dproj_ref[:, CQ0:CQ0 + B_Q_RANK] = dx.astype(BF16)
        add_small(SM_CQ, tok_sum(dgr))
        dxn = dxn + back(CQ0, CKV0)
        dg = jnp.zeros((1, LANE), F32)
        dkr = jnp.zeros((tm, LANE), F32)
        for h in range(B_HEADS):
            cols = slice(LANE * h, LANE * (h + 1))
            dn = _rope_bwd(dkb_ref[h], cbk, sbk)
            dx, dgr = _rms_bwd(dn, kbpre_ref[:, cols], gbk_ref[...], B_QK)
            dkvb_ref[:, cols] = jnp.where(nope_lanes, dx, 0.0).astype(BF16)
            dkvb_ref[:, B_HEADS * LANE + LANE * h:B_HEADS * LANE + LANE * (h + 1)] = dvb_ref[h].astype(BF16)
            dkr = dkr + dx
            dg = dg + tok_sum(dgr)
        add_small(SM_BK, _unspread_row(dg, LAY_ROPE_B))
        dproj_ref[:, KR0:KR0 + LANE] = jnp.where(_lanes_of(lane, LAY_KR), dkr, 0.0).astype(BF16)
        dckv = _nt(dkvb_ref[...], wukv_ref[...])
        dx, dgr = _rms_bwd(dckv, pre_ref[:, VA0 + B_Q_RANK:N_PRE], gckv_ref[...], B_KV_RANK)
        dproj_ref[:, CKV0:CKV0 + B_KV_RANK] = dx.astype(BF16)
        add_small(SM_CKV, tok_sum(dgr))
        dxn = dxn + back(CKV0, N_EXT)
        dx, dgr = _rms_bwd(dxn, x_ref[...], gin_ref[...], D_MODEL)
        gx_ref[...] = dh_ref[...] + dx
        add_small(SM_IN, tok_sum(dgr))

    row = lambda w: pl.BlockSpec((tm, w), lambda i: (i, 0))
    heads = lambda n: pl.BlockSpec((n, tm, LANE), lambda i: (0, i, 0))
    return _pallas(
        body, name="post", grid=(nt,),
        in_specs=[row(D_MODEL), row(D_MODEL), row(N_PRE), row(B_HEADS * LANE), row(B_HEADS * LANE), row(N_GATE_C),
                  heads(A_HEADS), heads(A_HEADS), heads(A_HEADS), heads(B_HEADS), heads(B_HEADS), heads(B_HEADS),
                  _full(loss_part.shape)] + _table_specs(tm, len(tabs) // 2)
                 + [_resident(w_in_ext.shape), _resident(w_uq_pad.shape), _resident(w_ukv_ext.shape)]
                 + [_full(g.shape) for g in gains],
        out_specs=[row(D_MODEL), row(N_EXT), row(B_HEADS * LANE), row(2 * B_HEADS * LANE), _full((SM_ROWS, SM_W))],
        out_shape=[jax.ShapeDtypeStruct((s_len, D_MODEL), F32), jax.ShapeDtypeStruct((s_len, N_EXT), BF16),
                   jax.ShapeDtypeStruct((s_len, B_HEADS * LANE), BF16),
                   jax.ShapeDtypeStruct((s_len, 2 * B_HEADS * LANE), BF16),
                   jax.ShapeDtypeStruct((SM_ROWS, SM_W), F32)],
        compiler_params=_params(("arbitrary",), VMEM_LIMIT),
    )(x, dh, pre, qbpre, kbpre, dgate, dqa, dka, dva, dqb, dkb, dvb, loss_part, *tabs,
      w_in_ext, w_uq_pad, w_ukv_ext, *gains)


def _grad_w(a_t, b, tn, ts, name):
    m, s_len = a_t.shape
    n = b.shape[1]

    def body(a_ref, b_ref, o_ref):
        @pl.when(pl.program_id(1) == 0)
        def _():
            o_ref[...] = jnp.zeros_like(o_ref)

        o_ref[...] += _nn(a_ref[...], b_ref[...].astype(BF16))

    return _pallas(
        body, name=name, grid=(n // tn, s_len // ts),
        in_specs=[pl.BlockSpec((m, ts), lambda j, t: (0, t)), pl.BlockSpec((ts, tn), lambda j, t: (t, j))],
        out_specs=pl.BlockSpec((m, tn), lambda j, t: (0, j)),
        out_shape=jax.ShapeDtypeStruct((m, n), F32),
        compiler_params=_params(("parallel", "arbitrary"), VMEM_MID),
    )(a_t, b)


def _grad_w_pairs(pairs, ts, name):
    s_len = pairs[0][0].shape[1]
    n_p = len(pairs)

    def body(*refs):
        for a_ref, b_ref, o_ref in zip(refs[0:2 * n_p:2], refs[1:2 * n_p:2], refs[2 * n_p:]):
            @pl.when(pl.program_id(0) == 0)
            def _():
                o_ref[...] = jnp.zeros_like(o_ref)

            o_ref[...] += _nn(a_ref[...], b_ref[...].astype(BF16))

    in_specs, flat = [], []
    for a_t, b in pairs:
        in_specs += [pl.BlockSpec((a_t.shape[0], ts), lambda t: (0, t)), pl.BlockSpec((ts, b.shape[1]), lambda t: (t, 0))]
        flat += [a_t, b]
    return _pallas(
        body, name=name, grid=(s_len // ts,),
        in_specs=in_specs,
        out_specs=[_full((a_t.shape[0], b.shape[1])) for a_t, b in pairs],
        out_shape=[jax.ShapeDtypeStruct((a_t.shape[0], b.shape[1]), F32) for a_t, b in pairs],
        compiler_params=_params(("arbitrary",), VMEM_MID),
    )(*flat)


def _adam_math(w, g, m, v):
    nm = ADAM_B1 * m + (1.0 - ADAM_B1) * g
    nv = ADAM_B2 * v + (1.0 - ADAM_B2) * (g * g)
    m_hat = nm / (1.0 - ADAM_B1 ** ADAM_STEP)
    v_hat = nv / (1.0 - ADAM_B2 ** ADAM_STEP)
    return -ADAM_LR * (m_hat / (jnp.sqrt(v_hat) + ADAM_EPS) + ADAM_WD * w), nm, nv


def _adamw_rows(w, g, m, v, tr):
    rows, cols = w.shape

    def body(w_ref, g_ref, m_ref, v_ref, d_ref, nm_ref, nv_ref):
        d_ref[...], nm_ref[...], nv_ref[...] = _adam_math(w_ref[...], g_ref[...], m_ref[...], v_ref[...])

    blk = pl.BlockSpec((tr, cols), lambda i: (i, 0))
    shape = jax.ShapeDtypeStruct((rows, cols), F32)
    return _pallas(
        body, name="adamw_w_in", grid=(rows // tr,),
        in_specs=[blk] * 4, out_specs=[blk] * 3, out_shape=[shape] * 3,
        compiler_params=_params(("parallel",), VMEM_SMALL),
    )(w, g, m, v)


def _adamw_rest(bigs, smalls, g_small):
    nb, ns = len(bigs), len(smalls)

    def body(*refs):
        ins, outs = refs[:4 * nb + 3 * ns + 1], refs[4 * nb + 3 * ns + 1:]
        for i in range(nb):
            w_ref, g_ref, m_ref, v_ref = ins[4 * i:4 * i + 4]
            d_ref, nm_ref, nv_ref = outs[3 * i:3 * i + 3]
            d_ref[...], nm_ref[...], nv_ref[...] = _adam_math(w_ref[...], g_ref[...], m_ref[...], v_ref[...])
        gs_ref = ins[-1]
        for i in range(ns):
            w_ref, m_ref, v_ref = ins[4 * nb + 3 * i:4 * nb + 3 * i + 3]
            g_ref, d_ref, nm_ref, nv_ref = outs[3 * nb + 4 * i:3 * nb + 4 * i + 4]
            g = gs_ref[i:i + 1, 0:w_ref.shape[1]]
            g_ref[...] = g
            d_ref[...], nm_ref[...], nv_ref[...] = _adam_math(w_ref[...], g, m_ref[...], v_ref[...])

    flat_in = [a for quad in bigs for a in quad] + [a for tri in smalls for a in tri] + [g_small]
    out_shape = ([jax.ShapeDtypeStruct(q[0].shape, F32) for q in bigs for _ in range(3)]
                 + [jax.ShapeDtypeStruct(t[0].shape, F32) for t in smalls for _ in range(4)])
    return _pallas(
        body, name="adamw_rest",
        in_specs=[pl.BlockSpec(memory_space=pltpu.VMEM)] * len(flat_in),
        out_specs=[pl.BlockSpec(memory_space=pltpu.VMEM)] * len(out_shape),
        out_shape=out_shape,
        compiler_params=_params(vmem=VMEM_SMALL),
    )(*flat_in)


def _place(pieces, n):
    out, at = [], 0
    for lane0, arr in sorted(pieces, key=lambda p: p[0]):
        out += [jnp.zeros((n, lane0 - at), F32), arr]
        at = lane0 + arr.shape[1]
    return jnp.concatenate(out + [jnp.zeros((n, LANE - at), F32)], axis=1)


def _rope_tables(s_len, tm):
    rows = s_len // GRID_W
    row = jnp.arange(rows, dtype=F32)
    col = jnp.arange(GRID_W, dtype=F32)

    def lay(dim, layout, first_dim, ones):
        half = dim // 2
        inv = 1.0 / (ROPE_THETA ** (jnp.arange(0, half, 2, dtype=F32) / half))
        ang_r, ang_c = row[:, None] * inv[None, :], col[:, None] * inv[None, :]
        at = {a - first_dim: lane0 for a, _, lane0 in layout}
        q = dim // 4
        r1, r2, c1, c2 = at[0], at[q], at[2 * q], at[3 * q]
        cos_r = _place([(r1, jnp.cos(ang_r)), (r2, jnp.cos(ang_r))], rows)
        sin_r = _place([(r1, -jnp.sin(ang_r)), (r2, jnp.sin(ang_r))], rows)
        cos_c = _place([(c1, jnp.cos(ang_c)), (c2, jnp.cos(ang_c))] + [(l0, jnp.ones((GRID_W, n), F32)) for _, n, l0 in ones],
                       GRID_W)
        sin_c = _place([(c1, -jnp.sin(ang_c)), (c2, jnp.sin(ang_c))], GRID_W)
        by_block = (s_len // tm, tm // GRID_W, LANE)
        return cos_r.reshape(by_block), cos_c, sin_r.reshape(by_block), sin_c

    return lay(A_DIM, LAY_ROPE_A, 0, ()) + lay(B_ROPE, LAY_KR, 0, LAY_NOPE)


def _spread(w, n_heads, dim, axis, layout):
    w3 = w.reshape(w.shape[:axis] + (n_heads, dim) + w.shape[axis + 1:])
    out, at = [], 0

    def zeros(n):
        return jnp.zeros(w3.shape[:axis + 1] + (n,) + w3.shape[axis + 2:], w.dtype)

    for a0, n, lane0 in sorted(layout, key=lambda seg: seg[2]):
        out += [zeros(lane0 - at), lax.slice_in_dim(w3, a0, a0 + n, axis=axis + 1)]
        at = lane0 + n
    out = jnp.concatenate(out + [zeros(LANE - at)], axis=axis + 1)
    return out.reshape(w.shape[:axis] + (n_heads * LANE,) + w.shape[axis + 1:])


def _head_cols(first, n_heads, dim, layout):
    out = np.full((n_heads * LANE,), -1, np.int32)
    for h in range(n_heads):
        for a0, n, lane0 in layout:
            out[h * LANE + lane0:h * LANE + lane0 + n] = first + h * dim + a0 + np.arange(n)
    return out


def _inverse(src, n):
    dst = np.full((n,), -1, np.int32)
    dst[src[src >= 0]] = np.nonzero(src >= 0)[0]
    return dst


def _column_maps():
    a_w, kv_w = A_HEADS * A_DIM, A_KV * A_DIM
    o_g = a_w + 2 * kv_w
    o_cq = o_g + a_w
    o_kr = o_cq + B_Q_RANK + B_KV_RANK
    src_in = np.concatenate([
        _head_cols(0, A_HEADS, A_DIM, LAY_ROPE_A), _head_cols(a_w, A_KV, A_DIM, LAY_ROPE_A),
        _head_cols(a_w + kv_w, A_KV, A_DIM, LAY_PLAIN_A), np.arange(o_g, o_g + a_w),
        np.arange(o_kr + B_ROPE, N_IN), np.arange(o_cq, o_kr), _head_cols(o_kr, 1, B_ROPE, LAY_KR)]).astype(np.int32)
    src_uq = _head_cols(0, B_HEADS, B_QK, LAY_ROPE_B)
    per = B_NOPE + B_V
    src_ukv = np.concatenate([_head_cols(0, B_HEADS, per, LAY_NOPE),
                              _head_cols(B_NOPE, B_HEADS, per, ((0, B_V, 0),))]).astype(np.int32)
    assert len(src_in) == N_EXT
    return src_in, src_uq, src_ukv


def _round_up(n, m):
    return (n + m - 1) // m * m


PERMUTE_STEPS = 4


def _permute_cols(xs, maps, stacks, name):
    maps = [np.asarray(m, np.int32) for m in maps]
    n_arr = len(xs)

    def block(ref, b):
        if len(ref.shape) == 2:
            return ref.at[:, b * LANE:(b + 1) * LANE]
        per = ref.shape[2] // LANE
        return ref.at[b // per, :, (b % per) * LANE:(b % per + 1) * LANE]

    def body(*refs):
        row = lax.broadcasted_iota(jnp.int32, (LANE, LANE), 0)
        for x_ref, src_ref, o_ref, src in zip(refs[:n_arr], refs[n_arr:2 * n_arr], refs[2 * n_arr:], maps):
            for c in range(len(src) // LANE):
                want = src[c * LANE:(c + 1) * LANE]
                if want[0] >= 0 and want[0] % LANE == 0 and np.array_equal(want, want[0] + np.arange(LANE)):
                    block(o_ref, c)[...] = block(x_ref, int(want[0]) // LANE)[...]
                    continue
                acc = jnp.zeros((x_ref.shape[-2], LANE), F32)
                for kb in sorted({int(v) // LANE for v in want if v >= 0}):
                    sel = jnp.where(row + kb * LANE == src_ref[:, c * LANE:(c + 1) * LANE], 1.0, 0.0).astype(BF16)
                    part = block(x_ref, kb)[...]
                    if part.dtype == BF16:
                        acc = acc + _nn(part, sel)
                    else:
                        hi = part.astype(BF16)
                        rest = part - hi.astype(F32)
                        mid = rest.astype(BF16)
                        low = (rest - mid.astype(F32)).astype(BF16)
                        acc = acc + ((_nn(hi, sel) + _nn(mid, sel)) + _nn(low, sel))
                block(o_ref, c)[...] = acc.astype(o_ref.dtype)

    def out_shape(x, m, stack):
        rows = x.shape[-2]
        return (rows, len(m)) if stack is None else (stack, rows, len(m) // stack)

    def rows_spec(shape):
        if len(shape) == 2:
            return pl.BlockSpec((shape[0] // PERMUTE_STEPS, shape[1]), lambda i: (i, 0))
        return pl.BlockSpec((shape[0], shape[1] // PERMUTE_STEPS, shape[2]), lambda i: (0, i, 0))

    shapes = [out_shape(x, m, st) for x, m, st in zip(xs, maps, stacks)]
    return _pallas(
        body, name=name, grid=(PERMUTE_STEPS,),
        in_specs=[rows_spec(x.shape) for x in xs] + [_full((1, len(m))) for m in maps],
        out_specs=[rows_spec(sh) for sh in shapes],
        out_shape=[jax.ShapeDtypeStruct(sh, x.dtype) for sh, x in zip(shapes, xs)],
        compiler_params=_params(("parallel",), VMEM_MID),
    )(*xs, *[jnp.asarray(m).reshape(1, -1) for m in maps])


def _pad_cols(w):
    return jnp.pad(w, ((0, 0), (0, _round_up(w.shape[1], LANE) - w.shape[1])))


def _in_stack(cols, width):
    cols = np.asarray(cols)
    return np.where(cols < 0, -1, cols // width * _round_up(width, LANE) + cols % width).astype(np.int32)


def _ext_weights(g_in, g_uq, g_ukv, g_out):
    src_in, src_uq, src_ukv = _column_maps()
    w_uq = g_uq.reshape(B_Q_RANK, B_HEADS * B_QK)
    w_out = g_out.reshape(D_MODEL, D_MODEL)
    w_in_ext, w_uq_pad, w_ukv_ext = _permute_cols(
        [g_in, w_uq, g_ukv], [_in_stack(src_in, SH_IN[1]), src_uq, _in_stack(src_ukv, SH_UKV[1])], [None] * 3, "lay_out_weights")
    return w_in_ext, w_uq_pad, w_ukv_ext, w_out


def _fold_grads(d_in_ext, d_uq_pad, d_ukv_ext, d_out):
    src_in, src_uq, src_ukv = _column_maps()

    def back(src, n, width):
        inv = _inverse(src, n)
        wide = _round_up(width, LANE)
        out = np.full((n // width * wide,), -1, np.int32)
        for j in range(n // width):
            out[j * wide:j * wide + width] = inv[j * width:(j + 1) * width]
        return out

    n_uq, n_ukv = B_HEADS * B_QK, B_HEADS * (B_NOPE + B_V)
    d_in, d_uq, d_ukv = _permute_cols(
        [d_in_ext, d_uq_pad, d_ukv_ext], [back(src_in, N_IN, SH_IN[1]), _inverse(src_uq, n_uq), back(src_ukv, n_ukv, SH_UKV[1])],
        [N_CHIPS, None, N_CHIPS], "fold_grads")
    return d_in, d_uq.reshape((N_CHIPS,) + SH_UQ), d_ukv, d_out.reshape((N_CHIPS,) + SH_OUT)


def kernel(x, norm_in, w_in, a_q_norm, a_k_norm, b_cq_norm, b_ckv_norm, w_uq, w_ukv, b_q_norm, b_k_norm, w_out, loss_target, m_norm_in, m_w_in, m_a_q_norm, m_a_k_norm, m_b_cq_norm, m_b_ckv_norm, m_w_uq, m_w_ukv, m_b_q_norm, m_b_k_norm, m_w_out, v_norm_in, v_w_in, v_a_q_norm, v_a_k_norm, v_b_cq_norm, v_b_ckv_norm, v_w_uq, v_w_ukv, v_b_q_norm, v_b_k_norm, v_w_out):
    s_len = x.shape[1]
    xs, ts = x[0], loss_target[0]
    tm = min(256, s_len)
    tq, tk_f = min(512, s_len // 2), min(2048, s_len // 2)
    tq_b, tk_b = min(1024, s_len // 2), min(512, s_len)
    tiles_f = min(4, s_len // tq)
    tiles_b = min(2, s_len // tk_b)

    w_in_ext, w_uq_pad, w_ukv_ext, w_out_full = _ext_weights(
        *_gather_weights((_pad_cols(w_in[0]), w_uq[0], _pad_cols(w_ukv[0]), w_out[0])))
    gains = (norm_in, _spread(a_q_norm, 1, A_DIM, 1, LAY_ROPE_A), _spread(a_k_norm, 1, A_DIM, 1, LAY_ROPE_A), b_cq_norm, b_ckv_norm,
             _spread(b_q_norm, 1, B_QK, 1, LAY_ROPE_B), _spread(b_k_norm, 1, B_QK, 1, LAY_ROPE_B))
    tabs = _rope_tables(s_len, tm)

    (xn_t, gates, pre, qbpre, kbpre, cq_t, ckv_t, qa, ka, va, qb, kb, vb) = _pre(
        xs, tabs, w_in_ext, w_uq_pad, w_ukv_ext, gains, tm)
    o_a, lse_a = _attn_fwd(qa, ka, va, A_GROUP, A_DIM, tq, tk_f, tiles_f, "attn_fwd_a")
    o_b, lse_b = _attn_fwd(qb, kb, vb, 1, B_V, tq, tk_f, tiles_f, "attn_fwd_b")
    y_t, dh, dgate, do_a, do_b, delta, loss_part = _mid(xs, ts, o_a, o_b, gates, w_out_full, min(512, s_len))

    def stat(a):
        return a.reshape(a.shape[0], s_len // tq_b, 1, tq_b)

    dqa, dka, dva = _attn_bwd(qa, ka, va, do_a, stat(lse_a), stat(delta[:A_HEADS]), A_GROUP, tq_b, tk_b, tiles_b, "attn_bwd_a")
    dqb, dkb, dvb = _attn_bwd(qb, kb, vb, do_b, stat(lse_b), stat(delta[A_HEADS:A_HEADS + B_HEADS]), 1, tq_b, tk_b,
                              tiles_b, "attn_bwd_b")
    tabs_bwd = tuple(t * f for t4, fs in ((tabs[:4], (SCALE_A, LN2)), (tabs[4:], (SCALE_B, LN2))) for f in fs for t in t4)
    grad_x, dproj, dqbpre, dkvb, d_small = _post(
        xs, dh, pre, qbpre, kbpre, dgate, dqa, dka, dva, dqb, dkb, dvb, loss_part, tabs_bwd,
        w_in_ext, w_uq_pad, w_ukv_ext, gains, tm)

    ts_w = min(2048, s_len)
    d_in_ext = _grad_w(xn_t, dproj, N_EXT // 2, ts_w, "grad_w_in")
    d_out_full = _grad_w(y_t, dh, D_MODEL, ts_w, "grad_w_out")
    d_uq_pad, d_ukv_ext = _grad_w_pairs([(cq_t, dqbpre), (ckv_t, dkvb)], ts_w, "grad_w_mla")

    g_in_p, g_uq, g_ukv_p, g_out, g_small = _reduce_grads(_fold_grads(d_in_ext, d_uq_pad, d_ukv_ext, d_out_full), d_small)
    g_in, g_ukv = g_in_p[:, :SH_IN[1]], g_ukv_p[:, :SH_UKV[1]]
    d_in, nm_in, nv_in = (a.T for a in _adamw_rows(w_in[0].T, g_in_p.T[:SH_IN[1]], m_w_in[0].T, v_w_in[0].T, SH_IN[1] // 7))
    rest = _adamw_rest(
        [(w_uq[0], g_uq, m_w_uq[0], v_w_uq[0]), (w_ukv[0], g_ukv, m_w_ukv[0], v_w_ukv[0]),
         (w_out[0], g_out, m_w_out[0], v_w_out[0])],
        [(norm_in, m_norm_in, v_norm_in), (a_q_norm, m_a_q_norm, v_a_q_norm), (a_k_norm, m_a_k_norm, v_a_k_norm),
         (b_cq_norm, m_b_cq_norm, v_b_cq_norm), (b_ckv_norm, m_b_ckv_norm, v_b_ckv_norm),
         (b_q_norm, m_b_q_norm, v_b_q_norm), (b_k_norm, m_b_k_norm, v_b_k_norm)], g_small)
    (d_uq, nm_uq, nv_uq), (d_ukv, nm_ukv, nv_ukv), (d_out, nm_out, nv_out) = (rest[3 * i:3 * i + 3] for i in range(3))
    sm = [rest[9 + 4 * i:9 + 4 * i + 4] for i in range(7)]

    def leaves(k, p_in, p_uq, p_ukv, p_out):
        return [sm[SM_IN][k], p_in[None], sm[SM_AQ][k], sm[SM_AK][k], sm[SM_CQ][k], sm[SM_CKV][k], p_uq[None], p_ukv[None],
                sm[SM_BQ][k], sm[SM_BK][k], p_out[None]]

    return (g_small[SM_LOSS, 0], grad_x[None], *leaves(0, g_in, g_uq, g_ukv, g_out), *leaves(1, d_in, d_uq, d_ukv, d_out),
            *leaves(2, nm_in, nm_uq, nm_ukv, nm_out), *leaves(3, nv_in, nv_uq, nv_ukv, nv_out))
```

```python
import jax
import jax.numpy as jnp
import numpy as np
from jax import lax
from jax.experimental import pallas as pl
from jax.experimental.pallas import tpu as pltpu

F32 = jnp.float32
BF16 = jnp.bfloat16
MESH = pl.DeviceIdType.MESH

D_MODEL = 1024
GRID_W = 64
ROPE_THETA = 10000.0
EPS = 1e-6
A_HEADS, A_KV, A_DIM = 8, 2, 64
A_GROUP = A_HEADS // A_KV
B_HEADS, B_NOPE, B_ROPE, B_V = 4, 64, 32, 128
B_QK = B_NOPE + B_ROPE
B_Q_RANK, B_KV_RANK = 384, 256
N_IN = 2464
SCALE_A = 1.0 / float(np.sqrt(A_DIM))
SCALE_B = 1.0 / float(np.sqrt(B_QK))
LOG2E = float(np.log2(np.e))
LN2 = float(np.log(2.0))
ADAM_LR, ADAM_B1, ADAM_B2, ADAM_EPS, ADAM_WD, ADAM_STEP = 0.001, 0.9, 0.999, 1e-08, 0.01, 10

LANE = 128
VMEM_BYTES = 64 * 1024 * 1024
VMEM_LIMIT = VMEM_BYTES - 8 * 1024 * 1024
VMEM_MID = 48 * 1024 * 1024
VMEM_SMALL = 32 * 1024 * 1024

QA0 = 0
KA0 = QA0 + A_HEADS * LANE
VA0 = KA0 + A_KV * LANE
GA0 = VA0 + A_KV * LANE
GB0 = GA0 + A_HEADS * A_DIM
CQ0 = GB0 + B_HEADS * LANE
CKV0 = CQ0 + B_Q_RANK
KR0 = CKV0 + B_KV_RANK
N_EXT = KR0 + LANE
N_GATE = (A_HEADS + B_HEADS) * LANE
N_GATE_C = A_HEADS * A_DIM + B_HEADS * LANE
DELTA_ROWS = 16
N_PRE = KA0 + A_KV * LANE + B_Q_RANK + B_KV_RANK

ROT = LANE // 2
_QA = A_DIM // 4
_QB = B_ROPE // 4
LAY_PLAIN_A = ((0, A_DIM, 0),)
LAY_ROPE_A = ((0, _QA, 0), (2 * _QA, _QA, _QA), (_QA, _QA, ROT), (3 * _QA, _QA, ROT + _QA))
LAY_KR = ((0, _QB, 0), (2 * _QB, _QB, _QB), (_QB, _QB, ROT), (3 * _QB, _QB, ROT + _QB))
LAY_NOPE = ((0, B_NOPE // 2, 2 * _QB), (B_NOPE // 2, B_NOPE // 2, ROT + 2 * _QB))
LAY_ROPE_B = LAY_NOPE + tuple((B_NOPE + a, n, at) for a, n, at in LAY_KR)

N_CHIPS = 4
SH_IN = (D_MODEL, N_IN // N_CHIPS)
SH_UQ = (B_Q_RANK // N_CHIPS, B_HEADS * B_QK)
SH_UKV = (B_KV_RANK, B_HEADS * (B_NOPE + B_V) // N_CHIPS)
SH_OUT = (D_MODEL // N_CHIPS, D_MODEL)
SM_ROWS, SM_W = 16, D_MODEL
SM_IN, SM_AQ, SM_AK, SM_CQ, SM_CKV, SM_BQ, SM_BK, SM_LOSS = range(8)
F32_ROWS, BF16_ROWS = 8, 16


def _pallas(body, **kw):
    return pl.pallas_call(body, **kw)


def _params(sem=None, vmem=None):
    return pltpu.CompilerParams(dimension_semantics=sem, vmem_limit_bytes=vmem)


def _rms_fwd(x, g, n):
    r = lax.rsqrt(jnp.sum(x * x, axis=-1, keepdims=True) * (1.0 / n) + EPS)
    return x * r * g


def _rms_bwd(dy, x, g, n):
    u = dy * g
    r = lax.rsqrt(jnp.sum(x * x, axis=-1, keepdims=True) * (1.0 / n) + EPS)
    ux = jnp.sum(u * x, axis=-1, keepdims=True)
    xhat = x * r
    dx = r * (u - xhat * (r * ux * (1.0 / n)))
    return dx, dy * xhat


def _rope_fwd(y, cos, sin):
    return y * cos + pltpu.roll(y, ROT, 1) * sin


def _rope_bwd(d, cos, sin):
    return d * cos - pltpu.roll(d, ROT, 1) * sin


def _token_tables(refs):
    out = []
    for r_ref, c_ref in zip(refs[0::2], refs[1::2]):
        r, c = r_ref[...], c_ref[...]
        out.append(jnp.concatenate([r[k:k + 1, :] + c for k in range(r.shape[0])], axis=0))
    return out


def _lanes_of(lane, layout):
    m = None
    for _, n, at in layout:
        seg = (lane >= at) & (lane < at + n)
        m = seg if m is None else (m | seg)
    return m


def _unspread_row(v, layout):
    v8 = jnp.broadcast_to(v, (F32_ROWS, LANE))
    lane = lax.broadcasted_iota(jnp.int32, v8.shape, 1)
    out = jnp.zeros_like(v8)
    for a, n, at in layout:
        moved = v8 if a == at else pltpu.roll(v8, (a - at) % LANE, 1)
        out = jnp.where((lane >= a) & (lane < a + n), moved, out)
    return out[0:1, :]


def _nt(a, b):
    return lax.dot_general(a, b, (((1,), (1,)), ((), ())), preferred_element_type=F32)


def _tn(a, b):
    return lax.dot_general(a, b, (((0,), (0,)), ((), ())), preferred_element_type=F32)


def _nn(a, b):
    return jnp.dot(a, b, preferred_element_type=F32)


def _block_rows(i, size):
    if isinstance(i, int):
        return pl.ds(i * size, size)
    return pl.ds(pl.multiple_of(i * size, size), size)


MAX_STATIC_BLOCKS = 32


def _three_stage(n, first, second, third):
    assert n >= 2 and n % 2 == 0
    first(0, 0)
    first(1, 1)
    second(0, 0)
    if n <= MAX_STATIC_BLOCKS:
        for i in range(1, n - 1):
            first(i + 1, (i + 1) % 2)
            second(i, i % 2)
            third(i - 1, (i - 1) % 2)
    else:
        def pair(t, carry):
            i = 2 * t + 1
            first(i + 1, 0)
            second(i, 1)
            third(i - 1, 0)
            first(i + 2, 1)
            second(i + 1, 0)
            third(i, 1)
            return carry

        lax.fori_loop(0, (n - 2) // 2, pair, 0)
    second(n - 1, 1)
    third(n - 2, 0)
    third(n - 1, 1)


def _full(shape):
    return pl.BlockSpec(shape, lambda *_: (0,) * len(shape))


def _table_specs(tm, n_tables=4):
    return [pl.BlockSpec((None, tm // GRID_W, LANE), lambda i: (i, 0, 0)), _full((GRID_W, LANE))] * n_tables


def _resident(shape):
    return pl.BlockSpec(shape, lambda *_: (0,) * len(shape), pipeline_mode=pl.Buffered(1))


def _gather_weights(shards):
    n = len(shards)
    halves = [w.shape[0] // 2 for w in shards]

    def body(*refs):
        w_refs, out_refs, (send_sems, recv_sems) = refs[:n], refs[n:2 * n], refs[2 * n:]
        x, y, c = lax.axis_index("x"), lax.axis_index("y"), lax.axis_index("c")
        sibling = (x, y, 1 - c)
        chips = [(1 - x, y), (x, 1 - y), (1 - x, 1 - y)]
        me = 2 * x + y

        def copy(a, k, j, hc, to):
            part = out_refs[a].at[j, pl.ds(pl.multiple_of(hc * halves[a], BF16_ROWS), halves[a]), :]
            return pltpu.make_async_remote_copy(
                src_ref=part, dst_ref=part, send_sem=send_sems.at[6 * a + k], recv_sem=recv_sems.at[6 * a + k],
                device_id=to, device_id_type=MESH)

        started = []
        for a in range(n):
            out_refs[a][me] = w_refs[a][...].astype(BF16)
            for k, chip in enumerate(chips):
                started.append(copy(a, k, me, c, (*chip, c)))
                started[-1].start()
        for k, chip in enumerate(chips):
            for a in range(n):
                copy(a, k, 2 * chip[0] + chip[1], c, (*chip, c)).wait_recv()
                started.append(copy(a, 3 + k, 2 * chip[0] + chip[1], c, sibling))
                started[-1].start()
        for k, chip in enumerate(chips):
            for a in range(n):
                copy(a, 3 + k, 2 * chip[0] + chip[1], 1 - c, sibling).wait_recv()
        for cp in started:
            cp.wait_send()

    return _pallas(
        body, name="gather_weights",
        out_shape=[jax.ShapeDtypeStruct((N_CHIPS,) + w.shape, BF16) for w in shards],
        in_specs=[pl.BlockSpec(memory_space=pltpu.VMEM)] * n,
        out_specs=[pl.BlockSpec(memory_space=pltpu.VMEM)] * n,
        scratch_shapes=[pltpu.SemaphoreType.DMA((6 * n,)), pltpu.SemaphoreType.DMA((6 * n,))],
        compiler_params=_params(vmem=VMEM_SMALL),
    )(*shards)


def _reduce_grads(parts, small):
    n_big = len(parts)
    n = n_big + 1
    shapes = [p.shape[1:] for p in parts] + [small.shape]
    halves = [sh[0] // 2 for sh in shapes]

    def body(*refs):
        p_refs, out_refs, rec_a, rec_b = refs[:n], refs[n:2 * n], refs[2 * n:3 * n], refs[3 * n:4 * n]
        send_b = refs[4 * n:4 * n + n_big]
        sa_send, sa_recv, sb_send, sb_recv, sc_send, sc_recv = refs[4 * n + n_big:]
        x, y, c = lax.axis_index("x"), lax.axis_index("y"), lax.axis_index("c")
        sibling = (x, y, 1 - c)
        me = 2 * x + y

        def rows(a, hc):
            return pl.ds(pl.multiple_of(hc * halves[a], F32_ROWS), halves[a])

        def partial(a, j, hc):
            return p_refs[a].at[j, rows(a, hc), :] if a < n_big else p_refs[a].at[rows(a, hc), :]

        def copy_a(a, j):
            return pltpu.make_async_remote_copy(
                src_ref=partial(a, j, 1 - c), dst_ref=rec_a[a].at[j],
                send_sem=sa_send.at[N_CHIPS * a + j], recv_sem=sa_recv.at[N_CHIPS * a + j],
                device_id=sibling, device_id_type=MESH)

        def copy_b(a, r):
            j = me ^ r
            k = (N_CHIPS - 1) * a + r - 1
            return pltpu.make_async_remote_copy(
                src_ref=(send_b[a] if a < n_big else rec_a[a]).at[j], dst_ref=rec_b[a].at[r],
                send_sem=sb_send.at[k], recv_sem=sb_recv.at[k], device_id=(j // 2, j % 2, c), device_id_type=MESH)

        def copy_c(a):
            return pltpu.make_async_remote_copy(
                src_ref=out_refs[a].at[rows(a, c), :], dst_ref=out_refs[a].at[rows(a, c), :],
                send_sem=sc_send.at[a], recv_sem=sc_recv.at[a], device_id=sibling, device_id_type=MESH)

        for a in range(n):
            for j in range(N_CHIPS):
                copy_a(a, j).start()
        for r in range(1, N_CHIPS):
            j = me ^ r
            for a in range(n):
                copy_a(a, j).wait_recv()
                chip_part = rec_a[a][j] + partial(a, j, c)[...]
                if a < n_big:
                    send_b[a][j] = chip_part.astype(BF16)
                else:
                    rec_a[a][j] = chip_part
                copy_b(a, r).start()
        for a in range(n):
            copy_a(a, me).wait_recv()
            rec_b[a][0] = (rec_a[a][me] + partial(a, me, c)[...]).astype(rec_b[a].dtype)
        for a in range(n):
            for r in range(1, N_CHIPS):
                copy_b(a, r).wait_recv()
            total = rec_b[a][me].astype(F32)
            for j in range(1, N_CHIPS):
                total = total + rec_b[a][j ^ me].astype(F32)
            out_refs[a][rows(a, c), :] = total
            copy_c(a).start()
        for a in range(n):
            copy_c(a).wait_recv()
        for a in range(n):
            for j in range(N_CHIPS):
                copy_a(a, j).wait_send()
            for r in range(1, N_CHIPS):
                copy_b(a, r).wait_send()
            copy_c(a).wait_send()

    dma = pltpu.SemaphoreType.DMA
    return _pallas(
        body, name="reduce_grads",
        out_shape=[jax.ShapeDtypeStruct(sh, F32) for sh in shapes],
        in_specs=[pl.BlockSpec(memory_space=pltpu.VMEM)] * n,
        out_specs=[pl.BlockSpec(memory_space=pltpu.VMEM)] * n,
        scratch_shapes=[pltpu.VMEM((N_CHIPS, h) + sh[1:], F32) for h, sh in zip(halves, shapes)]
                       + [pltpu.VMEM((N_CHIPS, h) + sh[1:], BF16 if a < n_big else F32)
                          for a, (h, sh) in enumerate(zip(halves, shapes))]
                       + [pltpu.VMEM((N_CHIPS, h) + sh[1:], BF16) for h, sh in zip(halves[:n_big], shapes[:n_big])]
                       + [dma((N_CHIPS * n,)), dma((N_CHIPS * n,)), dma(((N_CHIPS - 1) * n,)), dma(((N_CHIPS - 1) * n,)),
                          dma((n,)), dma((n,))],
        compiler_params=_params(vmem=VMEM_LIMIT),
    )(*parts, small)


def _pre(x, tabs, w_in_ext, w_uq_pad, w_ukv_ext, gains, tm):
    s_len = x.shape[0]
    nt = s_len // tm

    def body(x_ref, car_ref, cac_ref, sar_ref, sac_ref, cbr_ref, cbc_ref, sbr_ref, sbc_ref, win_ref, wuq_ref, wukv_ref,
             gin_ref, gaq_ref, gak_ref, gcq_ref, gckv_ref, gbq_ref, gbk_ref,
             xn_ref, gates_ref, pre_ref, qbpre_ref, kbpre_ref, cq_ref, ckv_ref,
             qa_ref, ka_ref, va_ref, qb_ref, kb_ref, vb_ref, proj):
        xn = _rms_fwd(x_ref[...], gin_ref[...], D_MODEL)
        xn_ref[...] = jnp.transpose(xn).astype(BF16)
        xb = xn.astype(BF16)
        pre_ref[:, 0:VA0] = _nn(xb, win_ref[:, 0:VA0])
        gates_ref[...] = _nn(xb, win_ref[:, GA0:GA0 + N_GATE_C])
        pre_ref[:, VA0:N_PRE] = _nn(xb, win_ref[:, CQ0:KR0])
        proj[...] = _nn(xb, win_ref[:, VA0:GA0])
        kr = _nn(xb, win_ref[:, KR0:N_EXT])
        ca, sa, cb, sb = _token_tables((car_ref, cac_ref, sar_ref, sac_ref, cbr_ref, cbc_ref, sbr_ref, sbc_ref))
        lane = lax.broadcasted_iota(jnp.int32, (tm, LANE), 1)
        for h in range(A_HEADS):
            yq = _rms_fwd(pre_ref[:, QA0 + LANE * h:QA0 + LANE * (h + 1)], gaq_ref[...], A_DIM)
            qa_ref[h] = (_rope_fwd(yq, ca, sa) * (SCALE_A * LOG2E)).astype(BF16)
        for h in range(A_KV):
            yk = _rms_fwd(pre_ref[:, KA0 + LANE * h:KA0 + LANE * (h + 1)], gak_ref[...], A_DIM)
            ka_ref[h] = _rope_fwd(yk, ca, sa).astype(BF16)
            va_ref[h] = jnp.where(lane == A_DIM, 1.0, proj[:, LANE * h:LANE * (h + 1)]).astype(BF16)
        cq = _rms_fwd(pre_ref[:, VA0:VA0 + B_Q_RANK], gcq_ref[...], B_Q_RANK)
        cq_ref[...] = jnp.transpose(cq).astype(BF16)
        qbpre_ref[...] = _nn(cq.astype(BF16), wuq_ref[...])
        ckv = _rms_fwd(pre_ref[:, VA0 + B_Q_RANK:N_PRE], gckv_ref[...], B_KV_RANK)
        ckv_ref[...] = jnp.transpose(ckv).astype(BF16)
        kvb = _nn(ckv.astype(BF16), wukv_ref[...])
        for h in range(B_HEADS):
            yq = _rms_fwd(qbpre_ref[:, LANE * h:LANE * (h + 1)], gbq_ref[...], B_QK)
            qb_ref[h] = (_rope_fwd(yq, cb, sb) * (SCALE_B * LOG2E)).astype(BF16)
            kp = kvb[:, LANE * h:LANE * (h + 1)] + kr
            kbpre_ref[:, LANE * h:LANE * (h + 1)] = kp
            kb_ref[h] = _rope_fwd(_rms_fwd(kp, gbk_ref[...], B_QK), cb, sb).astype(BF16)
            vb_ref[h, :, 0:LANE] = kvb[:, B_HEADS * LANE + LANE * h:B_HEADS * LANE + LANE * (h + 1)].astype(BF16)
            vb_ref[h, :, LANE:2 * LANE] = jnp.where(lane == 0, 1.0, 0.0).astype(BF16)

    row = lambda w: pl.BlockSpec((tm, w), lambda i: (i, 0))
    col = lambda w: pl.BlockSpec((w, tm), lambda i: (0, i))
    heads = lambda n: pl.BlockSpec((n, tm, LANE), lambda i: (0, i, 0))
    hs = lambda n: jax.ShapeDtypeStruct((n, s_len, LANE), BF16)
    return _pallas(
        body, name="pre", grid=(nt,),
        in_specs=[row(D_MODEL)] + _table_specs(tm)
                 + [_resident(w_in_ext.shape), _resident(w_uq_pad.shape), _resident(w_ukv_ext.shape)]
                 + [_full(g.shape) for g in gains],
        out_specs=[col(D_MODEL), row(N_GATE_C), row(N_PRE), row(B_HEADS * LANE), row(B_HEADS * LANE),
                   col(B_Q_RANK), col(B_KV_RANK),
                   heads(A_HEADS), heads(A_KV), heads(A_KV), heads(B_HEADS), heads(B_HEADS),
                   pl.BlockSpec((B_HEADS, tm, 2 * LANE), lambda i: (0, i, 0))],
        out_shape=[jax.ShapeDtypeStruct((D_MODEL, s_len), BF16), jax.ShapeDtypeStruct((s_len, N_GATE_C), F32),
                   jax.ShapeDtypeStruct((s_len, N_PRE), F32), jax.ShapeDtypeStruct((s_len, B_HEADS * LANE), F32),
                   jax.ShapeDtypeStruct((s_len, B_HEADS * LANE), F32),
                   jax.ShapeDtypeStruct((B_Q_RANK, s_len), BF16), jax.ShapeDtypeStruct((B_KV_RANK, s_len), BF16),
                   hs(A_HEADS), hs(A_KV), hs(A_KV), hs(B_HEADS), hs(B_HEADS),
                   jax.ShapeDtypeStruct((B_HEADS, s_len, 2 * LANE), BF16)],
        scratch_shapes=[pltpu.VMEM((tm, A_KV * LANE), F32)],
        compiler_params=_params(("parallel",), VMEM_LIMIT),
    )(x, *tabs, w_in_ext, w_uq_pad, w_ukv_ext, *gains)


def _attn_fwd(q, k, v, group, l_col, tq, tk, tiles, name):
    n_heads, s_len, _ = q.shape
    v_w = v.shape[2]
    nk = s_len // tk

    def body(q_ref, k_ref, v_ref, o_ref, lse_ref, s_buf, p_buf, a_buf, m_ref, acc_ref):
        def scores(g, slot):
            s_buf[slot] = _nt(q_ref[_block_rows(g // nk, tq), :], k_ref[_block_rows(g % nk, tk), :])

        def softmax(g, slot):
            t = g // nk
            s = s_buf[slot]
            if assign_first and g % nk == 0:
                m_new = jnp.broadcast_to(jnp.max(s, axis=-1, keepdims=True), (tq, LANE))
            else:
                m_old = m_ref[t]
                m_new = jnp.maximum(m_old, jnp.max(s, axis=-1, keepdims=True))
                a_buf[slot] = jnp.exp2(m_old - m_new)
            m_ref[t] = m_new
            p_buf[slot] = jnp.exp2(s - jnp.tile(m_new, (1, tk // LANE))).astype(BF16)

        def values(g, slot):
            t = g // nk
            pv = _nn(p_buf[slot], v_ref[_block_rows(g % nk, tk), :])
            for c in range(0, v_w, LANE):
                if assign_first and g % nk == 0:
                    acc_ref[t, :, c:c + LANE] = pv[:, c:c + LANE]
                else:
                    acc_ref[t, :, c:c + LANE] = a_buf[slot] * acc_ref[t, :, c:c + LANE] + pv[:, c:c + LANE]

        assign_first = tiles * nk <= MAX_STATIC_BLOCKS
        if not assign_first:
            m_ref[...] = jnp.full(m_ref.shape, -1e30, F32)
            acc_ref[...] = jnp.zeros(acc_ref.shape, F32)
        _three_stage(tiles * nk, scores, softmax, values)
        for t in range(tiles):
            l = acc_ref[t, :, l_col:l_col + 1]
            o = acc_ref[t, :, 0:LANE] * (1.0 / l)
            if l_col < LANE:
                lane = lax.broadcasted_iota(jnp.int32, o.shape, 1)
                o = jnp.where(lane == l_col, 0.0, o)
            o_ref[t * tq:(t + 1) * tq, :] = o
            lse_ref[t] = jnp.transpose(m_ref[t] + jnp.log2(jnp.broadcast_to(l, (tq, LANE))))[0:1, :]

    return _pallas(
        body, name=name, grid=(n_heads, s_len // (tiles * tq)),
        in_specs=[pl.BlockSpec((None, tiles * tq, LANE), lambda h, i: (h, i, 0)),
                  pl.BlockSpec((None, s_len, LANE), lambda h, i: (h // group, 0, 0)),
                  pl.BlockSpec((None, s_len, v_w), lambda h, i: (h // group, 0, 0))],
        out_specs=[pl.BlockSpec((None, tiles * tq, LANE), lambda h, i: (h, i, 0)),
                   pl.BlockSpec((None, tiles, 1, tq), lambda h, i: (h, i, 0, 0))],
        out_shape=[jax.ShapeDtypeStruct((n_heads, s_len, LANE), F32),
                   jax.ShapeDtypeStruct((n_heads, s_len // tq, 1, tq), F32)],
        scratch_shapes=[pltpu.VMEM((2, tq, tk), F32), pltpu.VMEM((2, tq, tk), BF16), pltpu.VMEM((2, tq, LANE), F32),
                        pltpu.VMEM((tiles, tq, LANE), F32), pltpu.VMEM((tiles, tq, v_w), F32)],
        compiler_params=_params(("parallel", "parallel"), VMEM_MID),
    )(q, k, v)


def _mid(x, target, o_a, o_b, gates, w_out, tm):
    s_len = x.shape[0]
    nt = s_len // tm
    n_heads = A_HEADS + B_HEADS
    d_mix = w_out.shape[0]
    pairs = A_HEADS // 2

    def body(x_ref, t_ref, oa_ref, ob_ref, g_ref, w_ref,
             yt_ref, dh_ref, dgate_ref, doa_ref, dob_ref, delta_ref, loss_ref, silu_scr, dsilu_scr, y_ref):
        @pl.when(pl.program_id(0) == 0)
        def _():
            loss_ref[...] = jnp.zeros_like(loss_ref)

        def o_of(h):
            return oa_ref[h] if h < A_HEADS else ob_ref[h - A_HEADS]

        lane = lax.broadcasted_iota(jnp.int32, (tm, LANE), 1)

        def gated(h):
            cols = slice(LANE * h, LANE * (h + 1))
            if h < A_HEADS:
                packed = g_ref[:, LANE * (h // 2):LANE * (h // 2 + 1)]
                g = jnp.where(lane < A_DIM, packed if h % 2 == 0 else pltpu.roll(packed, ROT, 1), 0.0)
            else:
                g = g_ref[:, LANE * (pairs + h - A_HEADS):LANE * (pairs + h - A_HEADS + 1)]
            sig = 1.0 / (1.0 + jnp.exp(-g))
            silu = g * sig
            silu_scr[:, cols] = silu
            dsilu_scr[:, cols] = sig * (1.0 + g * (1.0 - sig))
            return o_of(h) * silu

        for c in range(pairs + B_HEADS):
            y = gated(2 * c) + pltpu.roll(gated(2 * c + 1), ROT, 1) if c < pairs else gated(A_HEADS + c - pairs)
            cols = slice(LANE * c, LANE * (c + 1))
            y_ref[:, cols] = y.astype(BF16)
            yt_ref[cols, :] = jnp.transpose(y).astype(BF16)
        err = x_ref[...] + _nn(y_ref[...], w_ref[...]) - t_ref[...]
        sq = jnp.sum(jnp.sum(err * err, axis=-1, keepdims=True), axis=0, keepdims=True)
        loss_ref[...] += jnp.broadcast_to(sq * (0.5 / D_MODEL), loss_ref.shape)
        dh = err * (1.0 / D_MODEL)
        dh_ref[...] = dh
        dy = _nt(dh.astype(BF16), w_ref[...])
        delta = jnp.zeros((tm, LANE), F32)
        held = None
        for h in range(n_heads):
            cols = slice(LANE * h, LANE * (h + 1))
            if h < A_HEADS:
                packed = dy[:, LANE * (h // 2):LANE * (h // 2 + 1)]
                dyh = packed if h % 2 == 0 else pltpu.roll(packed, ROT, 1)
            else:
                dyh = dy[:, LANE * (pairs + h - A_HEADS):LANE * (pairs + h - A_HEADS + 1)]
            oh = o_of(h)
            do = dyh * silu_scr[:, cols]
            dg = dyh * oh * dsilu_scr[:, cols]
            if h >= A_HEADS:
                dgate_ref[:, LANE * (pairs + h - A_HEADS):LANE * (pairs + h - A_HEADS + 1)] = dg.astype(BF16)
            elif h % 2 == 0:
                held = dg
            else:
                dgate_ref[:, LANE * (h // 2):LANE * (h // 2 + 1)] = (held + pltpu.roll(dg, ROT, 1)).astype(BF16)
            delta = jnp.where(lane == h, jnp.sum(do * oh, axis=-1, keepdims=True), delta)
            if h < A_HEADS:
                doa_ref[h] = do.astype(BF16)
            else:
                dob_ref[h - A_HEADS] = do.astype(BF16)
        delta_ref[...] = jnp.transpose(delta)[0:DELTA_ROWS, :]

    row = lambda w: pl.BlockSpec((tm, w), lambda i: (i, 0))
    heads = lambda n, w=LANE: pl.BlockSpec((n, tm, w), lambda i: (0, i, 0))
    return _pallas(
        body, name="mid", grid=(nt,),
        in_specs=[row(D_MODEL), row(D_MODEL), heads(A_HEADS), heads(B_HEADS), row(N_GATE_C), _resident(w_out.shape)],
        out_specs=[pl.BlockSpec((d_mix, tm), lambda i: (0, i)), row(D_MODEL), row(N_GATE_C), heads(A_HEADS), heads(B_HEADS),
                   pl.BlockSpec((DELTA_ROWS, tm), lambda i: (0, i)),
                   _full((8, LANE))],
        out_shape=[jax.ShapeDtypeStruct((d_mix, s_len), BF16), jax.ShapeDtypeStruct((s_len, D_MODEL), F32),
                   jax.ShapeDtypeStruct((s_len, N_GATE_C), BF16),
                   jax.ShapeDtypeStruct((A_HEADS, s_len, LANE), BF16), jax.ShapeDtypeStruct((B_HEADS, s_len, LANE), BF16),
                   jax.ShapeDtypeStruct((DELTA_ROWS, s_len), F32), jax.ShapeDtypeStruct((8, LANE), F32)],
        scratch_shapes=[pltpu.VMEM((tm, N_GATE), F32), pltpu.VMEM((tm, N_GATE), F32), pltpu.VMEM((tm, d_mix), BF16)],
        compiler_params=_params(("arbitrary",), VMEM_LIMIT),
    )(x, target, o_a, o_b, gates, w_out)


def _attn_bwd(q, k, v, do, lse, delta, group, tq, tk, tiles, name):
    n_heads, s_len, _ = q.shape
    nq = s_len // tq

    def body(q_ref, do_ref, lse_ref, delta_ref, k_ref, v_ref, dq_ref, dk_ref, dv_ref, s_buf, dp_buf, p_buf, ds_buf):
        @pl.when(pl.program_id(1) == 0)
        def _():
            dq_ref[...] = jnp.zeros_like(dq_ref)

        assign_first = tiles * nq <= MAX_STATIC_BLOCKS
        if not assign_first:
            dk_ref[...] = jnp.zeros_like(dk_ref)
            dv_ref[...] = jnp.zeros_like(dv_ref)

        def keys(g):
            return _block_rows(g // nq, tk)

        def queries(g):
            return _block_rows(g % nq, tq)

        def scores(g, slot):
            s_buf[slot] = _nt(k_ref[keys(g), :], q_ref[queries(g), :])
            dp_buf[slot] = _nt(v_ref[keys(g), :], do_ref[queries(g), :])

        def elementwise(g, slot):
            p = jnp.exp2(s_buf[slot] - lse_ref[g % nq])
            p_buf[slot] = p.astype(BF16)
            ds_buf[slot] = (p * (dp_buf[slot] - delta_ref[g % nq])).astype(BF16)

        def grads(g, slot):
            dv = _nn(p_buf[slot], do_ref[queries(g), :])
            dk = _nn(ds_buf[slot], q_ref[queries(g), :])
            if assign_first and g % nq == 0:
                dv_ref[keys(g), :] = dv
                dk_ref[keys(g), :] = dk
            else:
                dv_ref[keys(g), :] += dv
                dk_ref[keys(g), :] += dk
            dq_ref[queries(g), :] += _tn(ds_buf[slot], k_ref[keys(g), :])

        _three_stage(tiles * nq, scores, elementwise, grads)

    whole = lambda: pl.BlockSpec((None, s_len, LANE), lambda h, j: (h, 0, 0))
    stat = lambda: pl.BlockSpec((None, nq, 1, tq), lambda h, j: (h, 0, 0, 0))
    kvb = lambda: pl.BlockSpec((None, tiles * tk, LANE), lambda h, j: (h // group, j, 0))
    outb = lambda: pl.BlockSpec((None, tiles * tk, LANE), lambda h, j: (h, j, 0))
    shape = jax.ShapeDtypeStruct((n_heads, s_len, LANE), F32)
    return _pallas(
        body, name=name, grid=(n_heads, s_len // (tiles * tk)),
        in_specs=[whole(), whole(), stat(), stat(), kvb(), kvb()],
        out_specs=[whole(), outb(), outb()],
        out_shape=[shape, shape, shape],
        scratch_shapes=[pltpu.VMEM((2, tk, tq), F32), pltpu.VMEM((2, tk, tq), F32),
                        pltpu.VMEM((2, tk, tq), BF16), pltpu.VMEM((2, tk, tq), BF16)],
        compiler_params=_params(("parallel", "arbitrary"), VMEM_MID),
    )(q, do, lse, delta, k, v)


def _post(x, dh, pre, qbpre, kbpre, dgate, dqa, dka, dva, dqb, dkb, dvb, loss_part, tabs,
          w_in_ext, w_uq_pad, w_ukv_ext, gains, tm):
    s_len = x.shape[0]
    nt = s_len // tm

    def body(x_ref, dh_ref, pre_ref, qbpre_ref, kbpre_ref, dgate_ref,
             dqa_ref, dka_ref, dva_ref, dqb_ref, dkb_ref, dvb_ref, loss_ref,
             t0, t1, t2, t3, t4, t5, t6, t7, t8, t9, t10, t11, t12, t13, t14, t15, win_ref, wuq_ref, wukv_ref,
             gin_ref, gaq_ref, gak_ref, gcq_ref, gckv_ref, gbq_ref, gbk_ref,
             gx_ref, dproj_ref, dqbpre_ref, dkvb_ref, dsm_ref):
        @pl.when(pl.program_id(0) == 0)
        def _():
            dsm_ref[...] = jnp.zeros_like(dsm_ref)
            dsm_ref[SM_LOSS:SM_LOSS + 1, 0:LANE] = loss_ref[0:1, :]

        def add_small(r, dg):
            dsm_ref[r:r + 1, 0:dg.shape[1]] += dg

        def tok_sum(a):
            return jnp.sum(a, axis=0, keepdims=True)

        caq, saq, cak, sak, cbq, sbq, cbk, sbk = _token_tables(
            (t0, t1, t2, t3, t4, t5, t6, t7, t8, t9, t10, t11, t12, t13, t14, t15))
        lane = lax.broadcasted_iota(jnp.int32, (tm, LANE), 1)

        nope_lanes = _lanes_of(lane, LAY_NOPE)

        def back(c0, c1):
            return _nt(dproj_ref[:, c0:c1], win_ref[:, c0:c1])

        dproj_ref[:, GA0:GA0 + N_GATE_C] = dgate_ref[...]
        dxn = back(GA0, GA0 + N_GATE_C)
        dg = jnp.zeros((1, LANE), F32)
        for h in range(A_HEADS):
            dn = _rope_bwd(dqa_ref[h], caq, saq)
            dx, dgr = _rms_bwd(dn, pre_ref[:, QA0 + LANE * h:QA0 + LANE * (h + 1)], gaq_ref[...], A_DIM)
            dproj_ref[:, QA0 + LANE * h:QA0 + LANE * (h + 1)] = dx.astype(BF16)
            dg = dg + tok_sum(dgr)
        add_small(SM_AQ, _unspread_row(dg, LAY_ROPE_A))
        dxn = dxn + back(QA0, KA0)
        dg = jnp.zeros((1, LANE), F32)
        for h in range(A_KV):
            dk = dka_ref[A_GROUP * h]
            dv = dva_ref[A_GROUP * h]
            for g in range(1, A_GROUP):
                dk = dk + dka_ref[A_GROUP * h + g]
                dv = dv + dva_ref[A_GROUP * h + g]
            dn = _rope_bwd(dk, cak, sak)
            dx, dgr = _rms_bwd(dn, pre_ref[:, KA0 + LANE * h:KA0 + LANE * (h + 1)], gak_ref[...], A_DIM)
            dproj_ref[:, KA0 + LANE * h:KA0 + LANE * (h + 1)] = dx.astype(BF16)
            dproj_ref[:, VA0 + LANE * h:VA0 + LANE * (h + 1)] = dv.astype(BF16)
            dg = dg + tok_sum(dgr)
        add_small(SM_AK, _unspread_row(dg, LAY_ROPE_A))
        dxn = dxn + back(KA0, GA0)
        dg = jnp.zeros((1, LANE), F32)
        for h in range(B_HEADS):
            cols = slice(LANE * h, LANE * (h + 1))
            dn = _rope_bwd(dqb_ref[h], cbq, sbq)
            dx, dgr = _rms_bwd(dn, qbpre_ref[:, cols], gbq_ref[...], B_QK)
            dqbpre_ref[:, cols] = dx.astype(BF16)
            dg = dg + tok_sum(dgr)
        add_small(SM_BQ, _unspread_row(dg, LAY_ROPE_B))
        dcq = _nt(dqbpre_ref[...], wuq_ref[...])
        dx, dgr = _rms_bwd(dcq, pre_ref[:, VA0:VA0 + B_Q_RANK], gcq_ref[...], B_Q_RANK)
        dproj_ref[:, CQ0:CQ0 + B_Q_RANK] = dx.astype(BF16)
        add_small(SM_CQ, tok_sum(dgr))
        dxn = dxn + back(CQ0, CKV0)
        dg = jnp.zeros((1, LANE), F32)
        dkr = jnp.zeros((tm, LANE), F32)
        for h in range(B_HEADS):
            cols = slice(LANE * h, LANE * (h + 1))
            dn = _rope_bwd(dkb_ref[h], cbk, sbk)
            dx, dgr = _rms_bwd(dn, kbpre_ref[:, cols], gbk_ref[...], B_QK)
            dkvb_ref[:, cols] = jnp.where(nope_lanes, dx, 0.0).astype(BF16)
            dkvb_ref[:, B_HEADS * LANE + LANE * h:B_HEADS * LANE + LANE * (h + 1)] = dvb_ref[h].astype(BF16)
            dkr = dkr + dx
            dg = dg + tok_sum(dgr)
        add_small(SM_BK, _unspread_row(dg, LAY_ROPE_B))
        dproj_ref[:, KR0:KR0 + LANE] = jnp.where(_lanes_of(lane, LAY_KR), dkr, 0.0).astype(BF16)
        dckv = _nt(dkvb_ref[...], wukv_ref[...])
        dx, dgr = _rms_bwd(dckv, pre_ref[:, VA0 + B_Q_RANK:N_PRE], gckv_ref[...], B_KV_RANK)
        dproj_ref[:, CKV0:CKV0 + B_KV_RANK] = dx.astype(BF16)
        add_small(SM_CKV, tok_sum(dgr))
        dxn = dxn + back(CKV0, N_EXT)
        dx, dgr = _rms_bwd(dxn, x_ref[...], gin_ref[...], D_MODEL)
        gx_ref[...] = dh_ref[...] + dx
        add_small(SM_IN, tok_sum(dgr))

    row = lambda w: pl.BlockSpec((tm, w), lambda i: (i, 0))
    heads = lambda n: pl.BlockSpec((n, tm, LANE), lambda i: (0, i, 0))
    return _pallas(
        body, name="post", grid=(nt,),
        in_specs=[row(D_MODEL), row(D_MODEL), row(N_PRE), row(B_HEADS * LANE), row(B_HEADS * LANE), row(N_GATE_C),
                  heads(A_HEADS), heads(A_HEADS), heads(A_HEADS), heads(B_HEADS), heads(B_HEADS), heads(B_HEADS),
                  _full(loss_part.shape)] + _table_specs(tm, len(tabs) // 2)
                 + [_resident(w_in_ext.shape), _resident(w_uq_pad.shape), _resident(w_ukv_ext.shape)]
                 + [_full(g.shape) for g in gains],
        out_specs=[row(D_MODEL), row(N_EXT), row(B_HEADS * LANE), row(2 * B_HEADS * LANE), _full((SM_ROWS, SM_W))],
        out_shape=[jax.ShapeDtypeStruct((s_len, D_MODEL), F32), jax.ShapeDtypeStruct((s_len, N_EXT), BF16),
                   jax.ShapeDtypeStruct((s_len, B_HEADS * LANE), BF16),
                   jax.ShapeDtypeStruct((s_len, 2 * B_HEADS * LANE), BF16),
                   jax.ShapeDtypeStruct((SM_ROWS, SM_W), F32)],
        compiler_params=_params(("arbitrary",), VMEM_LIMIT),
    )(x, dh, pre, qbpre, kbpre, dgate, dqa, dka, dva, dqb, dkb, dvb, loss_part, *tabs,
      w_in_ext, w_uq_pad, w_ukv_ext, *gains)


def _grad_w(a_t, b, tn, ts, name):
    m, s_len = a_t.shape
    n = b.shape[1]

    def body(a_ref, b_ref, o_ref):
        @pl.when(pl.program_id(1) == 0)
        def _():
            o_ref[...] = jnp.zeros_like(o_ref)

        o_ref[...] += _nn(a_ref[...], b_ref[...].astype(BF16))

    return _pallas(
        body, name=name, grid=(n // tn, s_len // ts),
        in_specs=[pl.BlockSpec((m, ts), lambda j, t: (0, t)), pl.BlockSpec((ts, tn), lambda j, t: (t, j))],
        out_specs=pl.BlockSpec((m, tn), lambda j, t: (0, j)),
        out_shape=jax.ShapeDtypeStruct((m, n), F32),
        compiler_params=_params(("parallel", "arbitrary"), VMEM_MID),
    )(a_t, b)


def _grad_w_pairs(pairs, ts, name):
    s_len = pairs[0][0].shape[1]
    n_p = len(pairs)

    def body(*refs):
        for a_ref, b_ref, o_ref in zip(refs[0:2 * n_p:2], refs[1:2 * n_p:2], refs[2 * n_p:]):
            @pl.when(pl.program_id(0) == 0)
            def _():
                o_ref[...] = jnp.zeros_like(o_ref)

            o_ref[...] += _nn(a_ref[...], b_ref[...].astype(BF16))

    in_specs, flat = [], []
    for a_t, b in pairs:
        in_specs += [pl.BlockSpec((a_t.shape[0], ts), lambda t: (0, t)), pl.BlockSpec((ts, b.shape[1]), lambda t: (t, 0))]
        flat += [a_t, b]
    return _pallas(
        body, name=name, grid=(s_len // ts,),
        in_specs=in_specs,
        out_specs=[_full((a_t.shape[0], b.shape[1])) for a_t, b in pairs],
        out_shape=[jax.ShapeDtypeStruct((a_t.shape[0], b.shape[1]), F32) for a_t, b in pairs],
        compiler_params=_params(("arbitrary",), VMEM_MID),
    )(*flat)


def _adam_math(w, g, m, v):
    nm = ADAM_B1 * m + (1.0 - ADAM_B1) * g
    nv = ADAM_B2 * v + (1.0 - ADAM_B2) * (g * g)
    m_hat = nm / (1.0 - ADAM_B1 ** ADAM_STEP)
    v_hat = nv / (1.0 - ADAM_B2 ** ADAM_STEP)
    return -ADAM_LR * (m_hat / (jnp.sqrt(v_hat) + ADAM_EPS) + ADAM_WD * w), nm, nv


def _adamw_rows(w, g, m, v, tr):
    rows, cols = w.shape

    def body(w_ref, g_ref, m_ref, v_ref, d_ref, nm_ref, nv_ref):
        d_ref[...], nm_ref[...], nv_ref[...] = _adam_math(w_ref[...], g_ref[...], m_ref[...], v_ref[...])

    blk = pl.BlockSpec((tr, cols), lambda i: (i, 0))
    shape = jax.ShapeDtypeStruct((rows, cols), F32)
    return _pallas(
        body, name="adamw_w_in", grid=(rows // tr,),
        in_specs=[blk] * 4, out_specs=[blk] * 3, out_shape=[shape] * 3,
        compiler_params=_params(("parallel",), VMEM_SMALL),
    )(w, g, m, v)


def _adamw_rest(bigs, smalls, g_small):
    nb, ns = len(bigs), len(smalls)

    def body(*refs):
        ins, outs = refs[:4 * nb + 3 * ns + 1], refs[4 * nb + 3 * ns + 1:]
        for i in range(nb):
            w_ref, g_ref, m_ref, v_ref = ins[4 * i:4 * i + 4]
            d_ref, nm_ref, nv_ref = outs[3 * i:3 * i + 3]
            d_ref[...], nm_ref[...], nv_ref[...] = _adam_math(w_ref[...], g_ref[...], m_ref[...], v_ref[...])
        gs_ref = ins[-1]
        for i in range(ns):
            w_ref, m_ref, v_ref = ins[4 * nb + 3 * i:4 * nb + 3 * i + 3]
            g_ref, d_ref, nm_ref, nv_ref = outs[3 * nb + 4 * i:3 * nb + 4 * i + 4]
            g = gs_ref[i:i + 1, 0:w_ref.shape[1]]
            g_ref[...] = g
            d_ref[...], nm_ref[...], nv_ref[...] = _adam_math(w_ref[...], g, m_ref[...], v_ref[...])

    flat_in = [a for quad in bigs for a in quad] + [a for tri in smalls for a in tri] + [g_small]
    out_shape = ([jax.ShapeDtypeStruct(q[0].shape, F32) for q in bigs for _ in range(3)]
                 + [jax.ShapeDtypeStruct(t[0].shape, F32) for t in smalls for _ in range(4)])
    return _pallas(
        body, name="adamw_rest",
        in_specs=[pl.BlockSpec(memory_space=pltpu.VMEM)] * len(flat_in),
        out_specs=[pl.BlockSpec(memory_space=pltpu.VMEM)] * len(out_shape),
        out_shape=out_shape,
        compiler_params=_params(vmem=VMEM_SMALL),
    )(*flat_in)


def _place(pieces, n):
    out, at = [], 0
    for lane0, arr in sorted(pieces, key=lambda p: p[0]):
        out += [jnp.zeros((n, lane0 - at), F32), arr]
        at = lane0 + arr.shape[1]
    return jnp.concatenate(out + [jnp.zeros((n, LANE - at), F32)], axis=1)


def _rope_tables(s_len, tm):
    rows = s_len // GRID_W
    row = jnp.arange(rows, dtype=F32)
    col = jnp.arange(GRID_W, dtype=F32)

    def lay(dim, layout, first_dim, ones):
        half = dim // 2
        inv = 1.0 / (ROPE_THETA ** (jnp.arange(0, half, 2, dtype=F32) / half))
        ang_r, ang_c = row[:, None] * inv[None, :], col[:, None] * inv[None, :]
        at = {a - first_dim: lane0 for a, _, lane0 in layout}
        q = dim // 4
        r1, r2, c1, c2 = at[0], at[q], at[2 * q], at[3 * q]
        cos_r = _place([(r1, jnp.cos(ang_r)), (r2, jnp.cos(ang_r))], rows)
        sin_r = _place([(r1, -jnp.sin(ang_r)), (r2, jnp.sin(ang_r))], rows)
        cos_c = _place([(c1, jnp.cos(ang_c)), (c2, jnp.cos(ang_c))] + [(l0, jnp.ones((GRID_W, n), F32)) for _, n, l0 in ones],
                       GRID_W)
        sin_c = _place([(c1, -jnp.sin(ang_c)), (c2, jnp.sin(ang_c))], GRID_W)
        by_block = (s_len // tm, tm // GRID_W, LANE)
        return cos_r.reshape(by_block), cos_c, sin_r.reshape(by_block), sin_c

    return lay(A_DIM, LAY_ROPE_A, 0, ()) + lay(B_ROPE, LAY_KR, 0, LAY_NOPE)


def _spread(w, n_heads, dim, axis, layout):
    w3 = w.reshape(w.shape[:axis] + (n_heads, dim) + w.shape[axis + 1:])
    out, at = [], 0

    def zeros(n):
        return jnp.zeros(w3.shape[:axis + 1] + (n,) + w3.shape[axis + 2:], w.dtype)

    for a0, n, lane0 in sorted(layout, key=lambda seg: seg[2]):
        out += [zeros(lane0 - at), lax.slice_in_dim(w3, a0, a0 + n, axis=axis + 1)]
        at = lane0 + n
    out = jnp.concatenate(out + [zeros(LANE - at)], axis=axis + 1)
    return out.reshape(w.shape[:axis] + (n_heads * LANE,) + w.shape[axis + 1:])


def _head_cols(first, n_heads, dim, layout):
    out = np.full((n_heads * LANE,), -1, np.int32)
    for h in range(n_heads):
        for a0, n, lane0 in layout:
            out[h * LANE + lane0:h * LANE + lane0 + n] = first + h * dim + a0 + np.arange(n)
    return out


def _inverse(src, n):
    dst = np.full((n,), -1, np.int32)
    dst[src[src >= 0]] = np.nonzero(src >= 0)[0]
    return dst


def _column_maps():
    a_w, kv_w = A_HEADS * A_DIM, A_KV * A_DIM
    o_g = a_w + 2 * kv_w
    o_cq = o_g + a_w
    o_kr = o_cq + B_Q_RANK + B_KV_RANK
    src_in = np.concatenate([
        _head_cols(0, A_HEADS, A_DIM, LAY_ROPE_A), _head_cols(a_w, A_KV, A_DIM, LAY_ROPE_A),
        _head_cols(a_w + kv_w, A_KV, A_DIM, LAY_PLAIN_A), np.arange(o_g, o_g + a_w),
        np.arange(o_kr + B_ROPE, N_IN), np.arange(o_cq, o_kr), _head_cols(o_kr, 1, B_ROPE, LAY_KR)]).astype(np.int32)
    src_uq = _head_cols(0, B_HEADS, B_QK, LAY_ROPE_B)
    per = B_NOPE + B_V
    src_ukv = np.concatenate([_head_cols(0, B_HEADS, per, LAY_NOPE),
                              _head_cols(B_NOPE, B_HEADS, per, ((0, B_V, 0),))]).astype(np.int32)
    assert len(src_in) == N_EXT
    return src_in, src_uq, src_ukv


def _round_up(n, m):
    return (n + m - 1) // m * m


PERMUTE_STEPS = 4


def _permute_cols(xs, maps, stacks, name):
    maps = [np.asarray(m, np.int32) for m in maps]
    n_arr = len(xs)

    def block(ref, b):
        if len(ref.shape) == 2:
            return ref.at[:, b * LANE:(b + 1) * LANE]
        per = ref.shape[2] // LANE
        return ref.at[b // per, :, (b % per) * LANE:(b % per + 1) * LANE]

    def body(*refs):
        row = lax.broadcasted_iota(jnp.int32, (LANE, LANE), 0)
        for x_ref, src_ref, o_ref, src in zip(refs[:n_arr], refs[n_arr:2 * n_arr], refs[2 * n_arr:], maps):
            for c in range(len(src) // LANE):
                want = src[c * LANE:(c + 1) * LANE]
                if want[0] >= 0 and want[0] % LANE == 0 and np.array_equal(want, want[0] + np.arange(LANE)):
                    block(o_ref, c)[...] = block(x_ref, int(want[0]) // LANE)[...]
                    continue
                acc = jnp.zeros((x_ref.shape[-2], LANE), F32)
                for kb in sorted({int(v) // LANE for v in want if v >= 0}):
                    sel = jnp.where(row + kb * LANE == src_ref[:, c * LANE:(c + 1) * LANE], 1.0, 0.0).astype(BF16)
                    part = block(x_ref, kb)[...]
                    if part.dtype == BF16:
                        acc = acc + _nn(part, sel)
                    else:
                        hi = part.astype(BF16)
                        rest = part - hi.astype(F32)
                        mid = rest.astype(BF16)
                        low = (rest - mid.astype(F32)).astype(BF16)
                        acc = acc + ((_nn(hi, sel) + _nn(mid, sel)) + _nn(low, sel))
                block(o_ref, c)[...] = acc.astype(o_ref.dtype)

    def out_shape(x, m, stack):
        rows = x.shape[-2]
        return (rows, len(m)) if stack is None else (stack, rows, len(m) // stack)

    def rows_spec(shape):
        if len(shape) == 2:
            return pl.BlockSpec((shape[0] // PERMUTE_STEPS, shape[1]), lambda i: (i, 0))
        return pl.BlockSpec((shape[0], shape[1] // PERMUTE_STEPS, shape[2]), lambda i: (0, i, 0))

    shapes = [out_shape(x, m, st) for x, m, st in zip(xs, maps, stacks)]
    return _pallas(
        body, name=name, grid=(PERMUTE_STEPS,),
        in_specs=[rows_spec(x.shape) for x in xs] + [_full((1, len(m))) for m in maps],
        out_specs=[rows_spec(sh) for sh in shapes],
        out_shape=[jax.ShapeDtypeStruct(sh, x.dtype) for sh, x in zip(shapes, xs)],
        compiler_params=_params(("parallel",), VMEM_MID),
    )(*xs, *[jnp.asarray(m).reshape(1, -1) for m in maps])


def _pad_cols(w):
    return jnp.pad(w, ((0, 0), (0, _round_up(w.shape[1], LANE) - w.shape[1])))


def _in_stack(cols, width):
    cols = np.asarray(cols)
    return np.where(cols < 0, -1, cols // width * _round_up(width, LANE) + cols % width).astype(np.int32)


def _ext_weights(g_in, g_uq, g_ukv, g_out):
    src_in, src_uq, src_ukv = _column_maps()
    w_uq = g_uq.reshape(B_Q_RANK, B_HEADS * B_QK)
    w_out = g_out.reshape(D_MODEL, D_MODEL)
    w_in_ext, w_uq_pad, w_ukv_ext = _permute_cols(
        [g_in, w_uq, g_ukv], [_in_stack(src_in, SH_IN[1]), src_uq, _in_stack(src_ukv, SH_UKV[1])], [None] * 3, "lay_out_weights")
    return w_in_ext, w_uq_pad, w_ukv_ext, w_out


def _fold_grads(d_in_ext, d_uq_pad, d_ukv_ext, d_out):
    src_in, src_uq, src_ukv = _column_maps()

    def back(src, n, width):
        inv = _inverse(src, n)
        wide = _round_up(width, LANE)
        out = np.full((n // width * wide,), -1, np.int32)
        for j in range(n // width):
            out[j * wide:j * wide + width] = inv[j * width:(j + 1) * width]
        return out

    n_uq, n_ukv = B_HEADS * B_QK, B_HEADS * (B_NOPE + B_V)
    d_in, d_uq, d_ukv = _permute_cols(
        [d_in_ext, d_uq_pad, d_ukv_ext], [back(src_in, N_IN, SH_IN[1]), _inverse(src_uq, n_uq), back(src_ukv, n_ukv, SH_UKV[1])],
        [N_CHIPS, None, N_CHIPS], "fold_grads")
    return d_in, d_uq.reshape((N_CHIPS,) + SH_UQ), d_ukv, d_out.reshape((N_CHIPS,) + SH_OUT)


def kernel(x, norm_in, w_in, a_q_norm, a_k_norm, b_cq_norm, b_ckv_norm, w_uq, w_ukv, b_q_norm, b_k_norm, w_out, loss_target, m_norm_in, m_w_in, m_a_q_norm, m_a_k_norm, m_b_cq_norm, m_b_ckv_norm, m_w_uq, m_w_ukv, m_b_q_norm, m_b_k_norm, m_w_out, v_norm_in, v_w_in, v_a_q_norm, v_a_k_norm, v_b_cq_norm, v_b_ckv_norm, v_w_uq, v_w_ukv, v_b_q_norm, v_b_k_norm, v_w_out):
    s_len = x.shape[1]
    xs, ts = x[0], loss_target[0]
    tm = min(256, s_len)
    tq, tk_f = min(512, s_len // 2), min(2048, s_len // 2)
    tq_b, tk_b = min(1024, s_len // 2), min(512, s_len)
    tiles_f = min(4, s_len // tq)
    tiles_b = min(2, s_len // tk_b)

    w_in_ext, w_uq_pad, w_ukv_ext, w_out_full = _ext_weights(
        *_gather_weights((_pad_cols(w_in[0]), w_uq[0], _pad_cols(w_ukv[0]), w_out[0])))
    gains = (norm_in, _spread(a_q_norm, 1, A_DIM, 1, LAY_ROPE_A), _spread(a_k_norm, 1, A_DIM, 1, LAY_ROPE_A), b_cq_norm, b_ckv_norm,
             _spread(b_q_norm, 1, B_QK, 1, LAY_ROPE_B), _spread(b_k_norm, 1, B_QK, 1, LAY_ROPE_B))
    tabs = _rope_tables(s_len, tm)

    (xn_t, gates, pre, qbpre, kbpre, cq_t, ckv_t, qa, ka, va, qb, kb, vb) = _pre(
        xs, tabs, w_in_ext, w_uq_pad, w_ukv_ext, gains, tm)
    o_a, lse_a = _attn_fwd(qa, ka, va, A_GROUP, A_DIM, tq, tk_f, tiles_f, "attn_fwd_a")
    o_b, lse_b = _attn_fwd(qb, kb, vb, 1, B_V, tq, tk_f, tiles_f, "attn_fwd_b")
    y_t, dh, dgate, do_a, do_b, delta, loss_part = _mid(xs, ts, o_a, o_b, gates, w_out_full, min(512, s_len))

    def stat(a):
        return a.reshape(a.shape[0], s_len // tq_b, 1, tq_b)

    dqa, dka, dva = _attn_bwd(qa, ka, va, do_a, stat(lse_a), stat(delta[:A_HEADS]), A_GROUP, tq_b, tk_b, tiles_b, "attn_bwd_a")
    dqb, dkb, dvb = _attn_bwd(qb, kb, vb, do_b, stat(lse_b), stat(delta[A_HEADS:A_HEADS + B_HEADS]), 1, tq_b, tk_b,
                              tiles_b, "attn_bwd_b")
    tabs_bwd = tuple(t * f for t4, fs in ((tabs[:4], (SCALE_A, LN2)), (tabs[4:], (SCALE_B, LN2))) for f in fs for t in t4)
    grad_x, dproj, dqbpre, dkvb, d_small = _post(
        xs, dh, pre, qbpre, kbpre, dgate, dqa, dka, dva, dqb, dkb, dvb, loss_part, tabs_bwd,
        w_in_ext, w_uq_pad, w_ukv_ext, gains, tm)

    ts_w = min(2048, s_len)
    d_in_ext = _grad_w(xn_t, dproj, N_EXT // 2, ts_w, "grad_w_in")
    d_out_full = _grad_w(y_t, dh, D_MODEL, ts_w, "grad_w_out")
    d_uq_pad, d_ukv_ext = _grad_w_pairs([(cq_t, dqbpre), (ckv_t, dkvb)], ts_w, "grad_w_mla")

    g_in_p, g_uq, g_ukv_p, g_out, g_small = _reduce_grads(_fold_grads(d_in_ext, d_uq_pad, d_ukv_ext, d_out_full), d_small)
    g_in, g_ukv = g_in_p[:, :SH_IN[1]], g_ukv_p[:, :SH_UKV[1]]
    d_in, nm_in, nv_in = (a.T for a in _adamw_rows(w_in[0].T, g_in_p.T[:SH_IN[1]], m_w_in[0].T, v_w_in[0].T, SH_IN[1] // 7))
    rest = _adamw_rest(
        [(w_uq[0], g_uq, m_w_uq[0], v_w_uq[0]), (w_ukv[0], g_ukv, m_w_ukv[0], v_w_ukv[0]),
         (w_out[0], g_out, m_w_out[0], v_w_out[0])],
        [(norm_in, m_norm_in, v_norm_in), (a_q_norm, m_a_q_norm, v_a_q_norm), (a_k_norm, m_a_k_norm, v_a_k_norm),
         (b_cq_norm, m_b_cq_norm, v_b_cq_norm), (b_ckv_norm, m_b_ckv_norm, v_b_ckv_norm),
         (b_q_norm, m_b_q_norm, v_b_q_norm), (b_k_norm, m_b_k_norm, v_b_k_norm)], g_small)
    (d_uq, nm_uq, nv_uq), (d_ukv, nm_ukv, nv_ukv), (d_out, nm_out, nv_out) = (rest[3 * i:3 * i + 3] for i in range(3))
    sm = [rest[9 + 4 * i:9 + 4 * i + 4] for i in range(7)]

    def leaves(k, p_in, p_uq, p_ukv, p_out):
        return [sm[SM_IN][k], p_in[None], sm[SM_AQ][k], sm[SM_AK][k], sm[SM_CQ][k], sm[SM_CKV][k], p_uq[None], p_ukv[None],
                sm[SM_BQ][k], sm[SM_BK][k], p_out[None]]

    return (g_small[SM_LOSS, 0], grad_x[None], *leaves(0, g_in, g_uq, g_ukv, g_out), *leaves(1, d_in, d_uq, d_ukv, d_out),
            *leaves(2, nm_in, nm_uq, nm_ukv, nm_out), *leaves(3, nv_in, nv_uq, nv_ukv, nv_out))
```
